```python
import math
import jax, jax.numpy as jnp
from jax import lax
import numpy as np

D_MODEL = 2048
BATCH = 8
SEQ = 4096
DEPTH = 1

D_SSM = D_MODEL // 2
SSM_GROUP = 16
N_SSM_GROUPS = D_SSM // SSM_GROUP
SSM_STATE = 64
D_GMLP = D_MODEL - D_SSM
GMLP_HEAD = 128
N_GMLP_HEADS = D_GMLP // GMLP_HEAD
CHUNK = 128
D_IN = D_SSM + 2 * D_GMLP
D_FF = 5632
D_PLE = 256
EPS = 1e-6
DT_MIN = 1e-3
DT_MAX = 1e-1

kernel_name = "hybrid_s5_gmlp_macaron_ple"


def rms_norm(x, g):
    xf = x.astype(jnp.float32)
    y = xf * lax.rsqrt(jnp.mean(xf * xf, axis=-1, keepdims=True) + EPS) * g.astype(jnp.float32)
    return y.astype(x.dtype)


def layer_norm(x, g):
    xf = x.astype(jnp.float32)
    xc = xf - jnp.mean(xf, axis=-1, keepdims=True)
    y = xc * lax.rsqrt(jnp.mean(xc * xc, axis=-1, keepdims=True) + EPS) * g.astype(jnp.float32)
    return y.astype(x.dtype)


def swiglu(x, w_gate, w_up, w_down):
    return (jax.nn.silu(x @ w_gate) * (x @ w_up)) @ w_down


def _complex_affine_combine(e1, e2):
    a1r, a1i, b1r, b1i = e1
    a2r, a2i, b2r, b2i = e2
    return (a2r * a1r - a2i * a1i,
            a2r * a1i + a2i * a1r,
            a2r * b1r - a2i * b1i + b2r,
            a2r * b1i + a2i * b1r + b2i)


def s5_mixer(u, log_dt, a_re, a_im, b_re, b_im, c_re, c_im, d, w_glu):
    bsz, seqlen, _ = u.shape
    f32 = jnp.float32
    uf = u.astype(f32).reshape(bsz, seqlen, N_SSM_GROUPS, SSM_GROUP)
    dt = jnp.exp(log_dt.astype(f32))[:, None]
    lr = jnp.minimum(a_re.astype(f32), -1e-4)
    li = a_im.astype(f32)
    mag = jnp.exp(lr * dt)
    ang = li * dt
    abar_r = mag * jnp.cos(ang)
    abar_i = mag * jnp.sin(ang)
    den = lr * lr + li * li
    xr = abar_r - 1.0
    xi = abar_i
    zr = (xr * lr + xi * li) / den
    zi = (xi * lr - xr * li) / den
    br = b_re.astype(f32)
    bi = b_im.astype(f32)
    bbar_r = zr[..., None] * br - zi[..., None] * bi
    bbar_i = zr[..., None] * bi + zi[..., None] * br
    drive_r = jnp.einsum("blgp,gnp->lbgn", uf, bbar_r)
    drive_i = jnp.einsum("blgp,gnp->lbgn", uf, bbar_i)
    ar = jnp.broadcast_to(abar_r[None, None], (seqlen, 1, N_SSM_GROUPS, SSM_STATE))
    ai = jnp.broadcast_to(abar_i[None, None], (seqlen, 1, N_SSM_GROUPS, SSM_STATE))
    _, _, sr, si = lax.associative_scan(_complex_affine_combine, (ar, ai, drive_r, drive_i), axis=0)
    y = (jnp.einsum("lbgn,gpn->blgp", sr, c_re.astype(f32))
         - jnp.einsum("lbgn,gpn->blgp", si, c_im.astype(f32)))
    y = y + d.astype(f32).reshape(N_SSM_GROUPS, SSM_GROUP) * uf
    y = jax.nn.gelu(y.reshape(bsz, seqlen, D_SSM))
    y = y * jax.nn.sigmoid(y @ w_glu.astype(f32))
    return y.astype(u.dtype)


def gmlp_mixer(z_u, z_v, norm_v, w_s, b_s):
    bsz, seqlen, _ = z_u.shape
    n_chunks = seqlen // CHUNK
    u = jax.nn.gelu(z_u)
    v = layer_norm(jax.nn.gelu(z_v), norm_v)
    causal = jnp.tril(jnp.ones((CHUNK, CHUNK), dtype=bool))
    w = jnp.where(causal[None], w_s, jnp.zeros_like(w_s))
    vc = v.reshape(bsz, n_chunks, CHUNK, N_GMLP_HEADS, GMLP_HEAD)
    s = jnp.einsum("hts,bcshp->bcthp", w, vc) + b_s.T[None, None, :, :, None]
    out = u.reshape(bsz, n_chunks, CHUNK, N_GMLP_HEADS, GMLP_HEAD) * s
    return out.reshape(bsz, seqlen, D_GMLP)


def _fwd_setup_inputs(seed: int = 0) -> dict:
    key = jax.random.key(seed)
    ks = jax.random.split(key, 32)
    f32 = jnp.float32

    def nrm(k, shape, std):
        return (jax.random.normal(k, shape, f32) * std).astype(f32)

    def gain(k, shape):
        return 1.0 + nrm(k, shape, 0.02)

    L = DEPTH
    x = nrm(ks[0], (BATCH, SEQ, D_MODEL), 1.0)
    p = nrm(ks[1], (L, BATCH, SEQ, D_PLE), 1.0)
    n_idx = jnp.arange(SSM_STATE, dtype=f32)
    return {
        "x": x,
        "p": p,
        "norm_ffn1": gain(ks[2], (L, D_MODEL)),
        "w1_gate": nrm(ks[3], (L, D_MODEL, D_FF), D_MODEL ** -0.5),
        "w1_up": nrm(ks[4], (L, D_MODEL, D_FF), D_MODEL ** -0.5),
        "w1_down": nrm(ks[5], (L, D_FF, D_MODEL), D_FF ** -0.5),
        "norm_mix": gain(ks[6], (L, D_MODEL)),
        "w_in": nrm(ks[7], (L, D_MODEL, D_IN), D_MODEL ** -0.5),
        "ssm_log_dt": jax.random.uniform(ks[8], (L, N_SSM_GROUPS), f32, math.log(DT_MIN), math.log(DT_MAX)),
        "ssm_a_re": -0.5 + nrm(ks[9], (L, N_SSM_GROUPS, SSM_STATE), 0.01),
        "ssm_a_im": math.pi * n_idx[None, None, :] + nrm(ks[10], (L, N_SSM_GROUPS, SSM_STATE), 0.01),
        "ssm_b_re": nrm(ks[11], (L, N_SSM_GROUPS, SSM_STATE, SSM_GROUP), (2 * SSM_GROUP) ** -0.5),
        "ssm_b_im": nrm(ks[12], (L, N_SSM_GROUPS, SSM_STATE, SSM_GROUP), (2 * SSM_GROUP) ** -0.5),
        "ssm_c_re": nrm(ks[13], (L, N_SSM_GROUPS, SSM_GROUP, SSM_STATE), 0.5 ** 0.5),
        "ssm_c_im": nrm(ks[14], (L, N_SSM_GROUPS, SSM_GROUP, SSM_STATE), 0.5 ** 0.5),
        "ssm_d": nrm(ks[15], (L, D_SSM), 1.0),
        "ssm_w_glu": nrm(ks[16], (L, D_SSM, D_SSM), D_SSM ** -0.5),
        "gmlp_norm_v": gain(ks[17], (L, D_GMLP)),
        "gmlp_w_s": nrm(ks[18], (L, N_GMLP_HEADS, CHUNK, CHUNK), CHUNK ** -0.5),
        "gmlp_b_s": 1.0 + nrm(ks[19], (L, N_GMLP_HEADS, CHUNK), 0.01),
        "norm_ssm_out": gain(ks[20], (L, D_SSM)),
        "norm_gmlp_out": gain(ks[21], (L, D_GMLP)),
        "w_out": nrm(ks[22], (L, D_MODEL, D_MODEL), D_MODEL ** -0.5),
        "norm_ffn2": gain(ks[23], (L, D_MODEL)),
        "w2_gate": nrm(ks[24], (L, D_MODEL, D_FF), D_MODEL ** -0.5),
        "w2_up": nrm(ks[25], (L, D_MODEL, D_FF), D_MODEL ** -0.5),
        "w2_down": nrm(ks[26], (L, D_FF, D_MODEL), D_FF ** -0.5),
        "norm_ple": gain(ks[27], (L, D_MODEL)),
        "w_ple_gate": nrm(ks[28], (L, D_MODEL, D_MODEL), D_MODEL ** -0.5),
        "w_ple_proj": nrm(ks[29], (L, D_PLE, D_MODEL), D_PLE ** -0.5),
        "norm_final": gain(ks[30], (D_MODEL,)),
    }


def _fwd_reference(x, p, norm_ffn1, w1_gate, w1_up, w1_down, norm_mix, w_in,
              ssm_log_dt, ssm_a_re, ssm_a_im, ssm_b_re, ssm_b_im, ssm_c_re, ssm_c_im,
              ssm_d, ssm_w_glu, gmlp_norm_v, gmlp_w_s, gmlp_b_s,
              norm_ssm_out, norm_gmlp_out, w_out, norm_ffn2, w2_gate, w2_up, w2_down,
              norm_ple, w_ple_gate, w_ple_proj, norm_final):
    h = x
    for i in range(DEPTH):
        h = h + 0.5 * swiglu(rms_norm(h, norm_ffn1[i]), w1_gate[i], w1_up[i], w1_down[i])
        z = rms_norm(h, norm_mix[i]) @ w_in[i]
        z_ssm = z[..., :D_SSM]
        z_u = z[..., D_SSM:D_SSM + D_GMLP]
        z_v = z[..., D_SSM + D_GMLP:]
        y_ssm = s5_mixer(z_ssm, ssm_log_dt[i], ssm_a_re[i], ssm_a_im[i], ssm_b_re[i], ssm_b_im[i],
                         ssm_c_re[i], ssm_c_im[i], ssm_d[i], ssm_w_glu[i])
        y_gmlp = gmlp_mixer(z_u, z_v, gmlp_norm_v[i], gmlp_w_s[i], gmlp_b_s[i])
        y = jnp.concatenate([rms_norm(y_ssm, norm_ssm_out[i]), rms_norm(y_gmlp, norm_gmlp_out[i])], axis=-1)
        h = h + y @ w_out[i]
        h = h + 0.5 * swiglu(rms_norm(h, norm_ffn2[i]), w2_gate[i], w2_up[i], w2_down[i])
        gate = jax.nn.sigmoid(rms_norm(h, norm_ple[i]) @ w_ple_gate[i])
        h = h + gate * (p[i] @ w_ple_proj[i])
    return rms_norm(h, norm_final)


import jax as _jax
import jax.numpy as _jnp

TWIN_FORMAT = 'train_step'
FWD_PARAMS = ['x', 'p', 'norm_ffn1', 'w1_gate', 'w1_up', 'w1_down', 'norm_mix', 'w_in', 'ssm_log_dt', 'ssm_a_re', 'ssm_a_im', 'ssm_b_re', 'ssm_b_im', 'ssm_c_re', 'ssm_c_im', 'ssm_d', 'ssm_w_glu', 'gmlp_norm_v', 'gmlp_w_s', 'gmlp_b_s', 'norm_ssm_out', 'norm_gmlp_out', 'w_out', 'norm_ffn2', 'w2_gate', 'w2_up', 'w2_down', 'norm_ple', 'w_ple_gate', 'w_ple_proj', 'norm_final']
TWIN_WEIGHTS = ['norm_ffn1', 'w1_gate', 'w1_up', 'w1_down', 'norm_mix', 'w_in', 'ssm_log_dt', 'ssm_a_re', 'ssm_a_im', 'ssm_b_re', 'ssm_b_im', 'ssm_c_re', 'ssm_c_im', 'ssm_d', 'ssm_w_glu', 'gmlp_norm_v', 'gmlp_w_s', 'gmlp_b_s', 'norm_ssm_out', 'norm_gmlp_out', 'w_out', 'norm_ffn2', 'w2_gate', 'w2_up', 'w2_down', 'norm_ple', 'w_ple_gate', 'w_ple_proj', 'norm_final']
TWIN_DIFF_INPUT = 'x'
TWIN_INPUTS = ['x', 'p', 'norm_ffn1', 'w1_gate', 'w1_up', 'w1_down', 'norm_mix', 'w_in', 'ssm_log_dt', 'ssm_a_re', 'ssm_a_im', 'ssm_b_re', 'ssm_b_im', 'ssm_c_re', 'ssm_c_im', 'ssm_d', 'ssm_w_glu', 'gmlp_norm_v', 'gmlp_w_s', 'gmlp_b_s', 'norm_ssm_out', 'norm_gmlp_out', 'w_out', 'norm_ffn2', 'w2_gate', 'w2_up', 'w2_down', 'norm_ple', 'w_ple_gate', 'w_ple_proj', 'norm_final', 'loss_target', 'm_norm_ffn1', 'm_w1_gate', 'm_w1_up', 'm_w1_down', 'm_norm_mix', 'm_w_in', 'm_ssm_log_dt', 'm_ssm_a_re', 'm_ssm_a_im', 'm_ssm_b_re', 'm_ssm_b_im', 'm_ssm_c_re', 'm_ssm_c_im', 'm_ssm_d', 'm_ssm_w_glu', 'm_gmlp_norm_v', 'm_gmlp_w_s', 'm_gmlp_b_s', 'm_norm_ssm_out', 'm_norm_gmlp_out', 'm_w_out', 'm_norm_ffn2', 'm_w2_gate', 'm_w2_up', 'm_w2_down', 'm_norm_ple', 'm_w_ple_gate', 'm_w_ple_proj', 'm_norm_final', 'v_norm_ffn1', 'v_w1_gate', 'v_w1_up', 'v_w1_down', 'v_norm_mix', 'v_w_in', 'v_ssm_log_dt', 'v_ssm_a_re', 'v_ssm_a_im', 'v_ssm_b_re', 'v_ssm_b_im', 'v_ssm_c_re', 'v_ssm_c_im', 'v_ssm_d', 'v_ssm_w_glu', 'v_gmlp_norm_v', 'v_gmlp_w_s', 'v_gmlp_b_s', 'v_norm_ssm_out', 'v_norm_gmlp_out', 'v_w_out', 'v_norm_ffn2', 'v_w2_gate', 'v_w2_up', 'v_w2_down', 'v_norm_ple', 'v_w_ple_gate', 'v_w_ple_proj', 'v_norm_final']
TWIN_OUTPUTS = ['loss', 'grad_x', 'grad_norm_ffn1', 'grad_w1_gate', 'grad_w1_up', 'grad_w1_down', 'grad_norm_mix', 'grad_w_in', 'grad_ssm_log_dt', 'grad_ssm_a_re', 'grad_ssm_a_im', 'grad_ssm_b_re', 'grad_ssm_b_im', 'grad_ssm_c_re', 'grad_ssm_c_im', 'grad_ssm_d', 'grad_ssm_w_glu', 'grad_gmlp_norm_v', 'grad_gmlp_w_s', 'grad_gmlp_b_s', 'grad_norm_ssm_out', 'grad_norm_gmlp_out', 'grad_w_out', 'grad_norm_ffn2', 'grad_w2_gate', 'grad_w2_up', 'grad_w2_down', 'grad_norm_ple', 'grad_w_ple_gate', 'grad_w_ple_proj', 'grad_norm_final', 'delta_norm_ffn1', 'delta_w1_gate', 'delta_w1_up', 'delta_w1_down', 'delta_norm_mix', 'delta_w_in', 'delta_ssm_log_dt', 'delta_ssm_a_re', 'delta_ssm_a_im', 'delta_ssm_b_re', 'delta_ssm_b_im', 'delta_ssm_c_re', 'delta_ssm_c_im', 'delta_ssm_d', 'delta_ssm_w_glu', 'delta_gmlp_norm_v', 'delta_gmlp_w_s', 'delta_gmlp_b_s', 'delta_norm_ssm_out', 'delta_norm_gmlp_out', 'delta_w_out', 'delta_norm_ffn2', 'delta_w2_gate', 'delta_w2_up', 'delta_w2_down', 'delta_norm_ple', 'delta_w_ple_gate', 'delta_w_ple_proj', 'delta_norm_final', 'new_m_norm_ffn1', 'new_m_w1_gate', 'new_m_w1_up', 'new_m_w1_down', 'new_m_norm_mix', 'new_m_w_in', 'new_m_ssm_log_dt', 'new_m_ssm_a_re', 'new_m_ssm_a_im', 'new_m_ssm_b_re', 'new_m_ssm_b_im', 'new_m_ssm_c_re', 'new_m_ssm_c_im', 'new_m_ssm_d', 'new_m_ssm_w_glu', 'new_m_gmlp_norm_v', 'new_m_gmlp_w_s', 'new_m_gmlp_b_s', 'new_m_norm_ssm_out', 'new_m_norm_gmlp_out', 'new_m_w_out', 'new_m_norm_ffn2', 'new_m_w2_gate', 'new_m_w2_up', 'new_m_w2_down', 'new_m_norm_ple', 'new_m_w_ple_gate', 'new_m_w_ple_proj', 'new_m_norm_final', 'new_v_norm_ffn1', 'new_v_w1_gate', 'new_v_w1_up', 'new_v_w1_down', 'new_v_norm_mix', 'new_v_w_in', 'new_v_ssm_log_dt', 'new_v_ssm_a_re', 'new_v_ssm_a_im', 'new_v_ssm_b_re', 'new_v_ssm_b_im', 'new_v_ssm_c_re', 'new_v_ssm_c_im', 'new_v_ssm_d', 'new_v_ssm_w_glu', 'new_v_gmlp_norm_v', 'new_v_gmlp_w_s', 'new_v_gmlp_b_s', 'new_v_norm_ssm_out', 'new_v_norm_gmlp_out', 'new_v_w_out', 'new_v_norm_ffn2', 'new_v_w2_gate', 'new_v_w2_up', 'new_v_w2_down', 'new_v_norm_ple', 'new_v_w_ple_gate', 'new_v_w_ple_proj', 'new_v_norm_final']
TWIN_LEAF_KINDS = {'loss': 'loss', 'grad_x': 'grad_x', 'grad_norm_ffn1': 'grad_w', 'grad_w1_gate': 'grad_w', 'grad_w1_up': 'grad_w', 'grad_w1_down': 'grad_w', 'grad_norm_mix': 'grad_w', 'grad_w_in': 'grad_w', 'grad_ssm_log_dt': 'grad_w', 'grad_ssm_a_re': 'grad_w', 'grad_ssm_a_im': 'grad_w', 'grad_ssm_b_re': 'grad_w', 'grad_ssm_b_im': 'grad_w', 'grad_ssm_c_re': 'grad_w', 'grad_ssm_c_im': 'grad_w', 'grad_ssm_d': 'grad_w', 'grad_ssm_w_glu': 'grad_w', 'grad_gmlp_norm_v': 'grad_w', 'grad_gmlp_w_s': 'grad_w', 'grad_gmlp_b_s': 'grad_w', 'grad_norm_ssm_out': 'grad_w', 'grad_norm_gmlp_out': 'grad_w', 'grad_w_out': 'grad_w', 'grad_norm_ffn2': 'grad_w', 'grad_w2_gate': 'grad_w', 'grad_w2_up': 'grad_w', 'grad_w2_down': 'grad_w', 'grad_norm_ple': 'grad_w', 'grad_w_ple_gate': 'grad_w', 'grad_w_ple_proj': 'grad_w', 'grad_norm_final': 'grad_w', 'delta_norm_ffn1': 'delta_w', 'delta_w1_gate': 'delta_w', 'delta_w1_up': 'delta_w', 'delta_w1_down': 'delta_w', 'delta_norm_mix': 'delta_w', 'delta_w_in': 'delta_w', 'delta_ssm_log_dt': 'delta_w', 'delta_ssm_a_re': 'delta_w', 'delta_ssm_a_im': 'delta_w', 'delta_ssm_b_re': 'delta_w', 'delta_ssm_b_im': 'delta_w', 'delta_ssm_c_re': 'delta_w', 'delta_ssm_c_im': 'delta_w', 'delta_ssm_d': 'delta_w', 'delta_ssm_w_glu': 'delta_w', 'delta_gmlp_norm_v': 'delta_w', 'delta_gmlp_w_s': 'delta_w', 'delta_gmlp_b_s': 'delta_w', 'delta_norm_ssm_out': 'delta_w', 'delta_norm_gmlp_out': 'delta_w', 'delta_w_out': 'delta_w', 'delta_norm_ffn2': 'delta_w', 'delta_w2_gate': 'delta_w', 'delta_w2_up': 'delta_w', 'delta_w2_down': 'delta_w', 'delta_norm_ple': 'delta_w', 'delta_w_ple_gate': 'delta_w', 'delta_w_ple_proj': 'delta_w', 'delta_norm_final': 'delta_w', 'new_m_norm_ffn1': 'new_m', 'new_m_w1_gate': 'new_m', 'new_m_w1_up': 'new_m', 'new_m_w1_down': 'new_m', 'new_m_norm_mix': 'new_m', 'new_m_w_in': 'new_m', 'new_m_ssm_log_dt': 'new_m', 'new_m_ssm_a_re': 'new_m', 'new_m_ssm_a_im': 'new_m', 'new_m_ssm_b_re': 'new_m', 'new_m_ssm_b_im': 'new_m', 'new_m_ssm_c_re': 'new_m', 'new_m_ssm_c_im': 'new_m', 'new_m_ssm_d': 'new_m', 'new_m_ssm_w_glu': 'new_m', 'new_m_gmlp_norm_v': 'new_m', 'new_m_gmlp_w_s': 'new_m', 'new_m_gmlp_b_s': 'new_m', 'new_m_norm_ssm_out': 'new_m', 'new_m_norm_gmlp_out': 'new_m', 'new_m_w_out': 'new_m', 'new_m_norm_ffn2': 'new_m', 'new_m_w2_gate': 'new_m', 'new_m_w2_up': 'new_m', 'new_m_w2_down': 'new_m', 'new_m_norm_ple': 'new_m', 'new_m_w_ple_gate': 'new_m', 'new_m_w_ple_proj': 'new_m', 'new_m_norm_final': 'new_m', 'new_v_norm_ffn1': 'new_v', 'new_v_w1_gate': 'new_v', 'new_v_w1_up': 'new_v', 'new_v_w1_down': 'new_v', 'new_v_norm_mix': 'new_v', 'new_v_w_in': 'new_v', 'new_v_ssm_log_dt': 'new_v', 'new_v_ssm_a_re': 'new_v', 'new_v_ssm_a_im': 'new_v', 'new_v_ssm_b_re': 'new_v', 'new_v_ssm_b_im': 'new_v', 'new_v_ssm_c_re': 'new_v', 'new_v_ssm_c_im': 'new_v', 'new_v_ssm_d': 'new_v', 'new_v_ssm_w_glu': 'new_v', 'new_v_gmlp_norm_v': 'new_v', 'new_v_gmlp_w_s': 'new_v', 'new_v_gmlp_b_s': 'new_v', 'new_v_norm_ssm_out': 'new_v', 'new_v_norm_gmlp_out': 'new_v', 'new_v_w_out': 'new_v', 'new_v_norm_ffn2': 'new_v', 'new_v_w2_gate': 'new_v', 'new_v_w2_up': 'new_v', 'new_v_w2_down': 'new_v', 'new_v_norm_ple': 'new_v', 'new_v_w_ple_gate': 'new_v', 'new_v_w_ple_proj': 'new_v', 'new_v_norm_final': 'new_v'}


def _forward(args):
    return _fwd_reference(*[args[k] for k in FWD_PARAMS])


def _output_shape():
    def fwd():
        inp = _fwd_setup_inputs(0)
        return _fwd_reference(*[inp[k] for k in FWD_PARAMS])
    out = _jax.eval_shape(fwd)
    return out.shape, out.dtype

N_MICROBATCH = 1
ADAM_LR = 0.001
ADAM_B1 = 0.9
ADAM_B2 = 0.999
ADAM_EPS = 1e-08
ADAM_WD = 0.01
ADAM_STEP = 10
PER_EXAMPLE_BATCH_AXIS = {'x': 0, 'p': 1, 'loss_target': 0}
SHARED_INPUTS = []
_WEIGHT_DTYPES = {'norm_ffn1': _jnp.float32, 'w1_gate': _jnp.float32, 'w1_up': _jnp.float32, 'w1_down': _jnp.float32, 'norm_mix': _jnp.float32, 'w_in': _jnp.float32, 'ssm_log_dt': _jnp.float32, 'ssm_a_re': _jnp.float32, 'ssm_a_im': _jnp.float32, 'ssm_b_re': _jnp.float32, 'ssm_b_im': _jnp.float32, 'ssm_c_re': _jnp.float32, 'ssm_c_im': _jnp.float32, 'ssm_d': _jnp.float32, 'ssm_w_glu': _jnp.float32, 'gmlp_norm_v': _jnp.float32, 'gmlp_w_s': _jnp.float32, 'gmlp_b_s': _jnp.float32, 'norm_ssm_out': _jnp.float32, 'norm_gmlp_out': _jnp.float32, 'w_out': _jnp.float32, 'norm_ffn2': _jnp.float32, 'w2_gate': _jnp.float32, 'w2_up': _jnp.float32, 'w2_down': _jnp.float32, 'norm_ple': _jnp.float32, 'w_ple_gate': _jnp.float32, 'w_ple_proj': _jnp.float32, 'norm_final': _jnp.float32}
MOMENT_SCALE = {'norm_ffn1': 4.200988e-02, 'w1_gate': 1.649892e-02, 'w1_up': 1.598160e-02, 'w1_down': 2.652780e-02, 'norm_mix': 7.037092e-02, 'w_in': 5.621243e-02, 'ssm_log_dt': 1.102690e+01, 'ssm_a_re': 2.355159e-02, 'ssm_a_im': 2.100839e-02, 'ssm_b_re': 1.490667e-02, 'ssm_b_im': 1.526739e-02, 'ssm_c_re': 3.780078e-03, 'ssm_c_im': 3.712400e-03, 'ssm_d': 6.601826e-02, 'ssm_w_glu': 2.049125e-02, 'gmlp_norm_v': 3.484040e-02, 'gmlp_w_s': 3.331347e-02, 'gmlp_b_s': 4.973516e-02, 'norm_ssm_out': 6.917026e-02, 'norm_gmlp_out': 6.643795e-02, 'w_out': 6.815781e-02, 'norm_ffn2': 2.474500e-02, 'w2_gate': 1.061558e-02, 'w2_up': 1.043064e-02, 'w2_down': 1.718318e-02, 'norm_ple': 1.327616e-02, 'w_ple_gate': 1.296770e-02, 'w_ple_proj': 3.077414e-02, 'norm_final': 1.607230e+01}


def _to_microbatches(a, axis):
    t = _jnp.moveaxis(a, axis, 0)
    t = t.reshape((N_MICROBATCH, t.shape[0] // N_MICROBATCH) + t.shape[1:])
    return _jnp.moveaxis(t, 1, axis + 1)


def setup_inputs(seed: int = 0) -> dict:
    inp = _fwd_setup_inputs(seed)
    key = _jax.random.fold_in(_jax.random.key(seed), 7919)
    shape, _ = _output_shape()
    out = dict(inp)
    out["loss_target"] = _jax.random.normal(_jax.random.fold_in(key, 0), shape, _jnp.float32)
    for i, name in enumerate(TWIN_WEIGHTS):
        w = inp[name].astype(_jnp.float32)
        if MOMENT_SCALE is None:
            s = _jnp.sqrt(_jnp.mean(_jnp.square(w)) + 1e-30)
        else:
            s = MOMENT_SCALE[name]
        km, kv = _jax.random.split(_jax.random.fold_in(key, i + 1))
        out[name] = w
        out["m_" + name] = s * _jax.random.normal(km, w.shape, _jnp.float32)
        out["v_" + name] = (s * s) * _jax.random.uniform(kv, w.shape, _jnp.float32, 0.5, 1.5)
    if N_MICROBATCH > 1:
        for name, axis in PER_EXAMPLE_BATCH_AXIS.items():
            out[name] = _to_microbatches(out[name], axis)
    return {'x': out['x'], 'p': out['p'], 'norm_ffn1': out['norm_ffn1'], 'w1_gate': out['w1_gate'], 'w1_up': out['w1_up'], 'w1_down': out['w1_down'], 'norm_mix': out['norm_mix'], 'w_in': out['w_in'], 'ssm_log_dt': out['ssm_log_dt'], 'ssm_a_re': out['ssm_a_re'], 'ssm_a_im': out['ssm_a_im'], 'ssm_b_re': out['ssm_b_re'], 'ssm_b_im': out['ssm_b_im'], 'ssm_c_re': out['ssm_c_re'], 'ssm_c_im': out['ssm_c_im'], 'ssm_d': out['ssm_d'], 'ssm_w_glu': out['ssm_w_glu'], 'gmlp_norm_v': out['gmlp_norm_v'], 'gmlp_w_s': out['gmlp_w_s'], 'gmlp_b_s': out['gmlp_b_s'], 'norm_ssm_out': out['norm_ssm_out'], 'norm_gmlp_out': out['norm_gmlp_out'], 'w_out': out['w_out'], 'norm_ffn2': out['norm_ffn2'], 'w2_gate': out['w2_gate'], 'w2_up': out['w2_up'], 'w2_down': out['w2_down'], 'norm_ple': out['norm_ple'], 'w_ple_gate': out['w_ple_gate'], 'w_ple_proj': out['w_ple_proj'], 'norm_final': out['norm_final'], 'loss_target': out['loss_target'], 'm_norm_ffn1': out['m_norm_ffn1'], 'm_w1_gate': out['m_w1_gate'], 'm_w1_up': out['m_w1_up'], 'm_w1_down': out['m_w1_down'], 'm_norm_mix': out['m_norm_mix'], 'm_w_in': out['m_w_in'], 'm_ssm_log_dt': out['m_ssm_log_dt'], 'm_ssm_a_re': out['m_ssm_a_re'], 'm_ssm_a_im': out['m_ssm_a_im'], 'm_ssm_b_re': out['m_ssm_b_re'], 'm_ssm_b_im': out['m_ssm_b_im'], 'm_ssm_c_re': out['m_ssm_c_re'], 'm_ssm_c_im': out['m_ssm_c_im'], 'm_ssm_d': out['m_ssm_d'], 'm_ssm_w_glu': out['m_ssm_w_glu'], 'm_gmlp_norm_v': out['m_gmlp_norm_v'], 'm_gmlp_w_s': out['m_gmlp_w_s'], 'm_gmlp_b_s': out['m_gmlp_b_s'], 'm_norm_ssm_out': out['m_norm_ssm_out'], 'm_norm_gmlp_out': out['m_norm_gmlp_out'], 'm_w_out': out['m_w_out'], 'm_norm_ffn2': out['m_norm_ffn2'], 'm_w2_gate': out['m_w2_gate'], 'm_w2_up': out['m_w2_up'], 'm_w2_down': out['m_w2_down'], 'm_norm_ple': out['m_norm_ple'], 'm_w_ple_gate': out['m_w_ple_gate'], 'm_w_ple_proj': out['m_w_ple_proj'], 'm_norm_final': out['m_norm_final'], 'v_norm_ffn1': out['v_norm_ffn1'], 'v_w1_gate': out['v_w1_gate'], 'v_w1_up': out['v_w1_up'], 'v_w1_down': out['v_w1_down'], 'v_norm_mix': out['v_norm_mix'], 'v_w_in': out['v_w_in'], 'v_ssm_log_dt': out['v_ssm_log_dt'], 'v_ssm_a_re': out['v_ssm_a_re'], 'v_ssm_a_im': out['v_ssm_a_im'], 'v_ssm_b_re': out['v_ssm_b_re'], 'v_ssm_b_im': out['v_ssm_b_im'], 'v_ssm_c_re': out['v_ssm_c_re'], 'v_ssm_c_im': out['v_ssm_c_im'], 'v_ssm_d': out['v_ssm_d'], 'v_ssm_w_glu': out['v_ssm_w_glu'], 'v_gmlp_norm_v': out['v_gmlp_norm_v'], 'v_gmlp_w_s': out['v_gmlp_w_s'], 'v_gmlp_b_s': out['v_gmlp_b_s'], 'v_norm_ssm_out': out['v_norm_ssm_out'], 'v_norm_gmlp_out': out['v_norm_gmlp_out'], 'v_w_out': out['v_w_out'], 'v_norm_ffn2': out['v_norm_ffn2'], 'v_w2_gate': out['v_w2_gate'], 'v_w2_up': out['v_w2_up'], 'v_w2_down': out['v_w2_down'], 'v_norm_ple': out['v_norm_ple'], 'v_w_ple_gate': out['v_w_ple_gate'], 'v_w_ple_proj': out['v_w_ple_proj'], 'v_norm_final': out['v_norm_final']}


def _loss(weights, diff, rest, loss_target):
    with _jax.named_scope("forward"):
        args = {**rest, TWIN_DIFF_INPUT: diff, **{k: w.astype(_WEIGHT_DTYPES[k]) for k, w in weights.items()}}
        y = _forward(args)
    with _jax.named_scope("loss_head"):
        err = _jnp.square(y.astype(_jnp.float32) - loss_target)
        return 0.5 * _jnp.sum(_jnp.mean(err, axis=-1)) if err.ndim else 0.5 * err


def _adamw(w, g, m, v):
    m = ADAM_B1 * m + (1.0 - ADAM_B1) * g
    v = ADAM_B2 * v + (1.0 - ADAM_B2) * _jnp.square(g)
    m_hat = m / (1.0 - ADAM_B1 ** ADAM_STEP)
    v_hat = v / (1.0 - ADAM_B2 ** ADAM_STEP)
    delta = -ADAM_LR * (m_hat / (_jnp.sqrt(v_hat) + ADAM_EPS) + ADAM_WD * w)
    return delta, m, v


def reference(x, p, norm_ffn1, w1_gate, w1_up, w1_down, norm_mix, w_in, ssm_log_dt, ssm_a_re, ssm_a_im, ssm_b_re, ssm_b_im, ssm_c_re, ssm_c_im, ssm_d, ssm_w_glu, gmlp_norm_v, gmlp_w_s, gmlp_b_s, norm_ssm_out, norm_gmlp_out, w_out, norm_ffn2, w2_gate, w2_up, w2_down, norm_ple, w_ple_gate, w_ple_proj, norm_final, loss_target, m_norm_ffn1, m_w1_gate, m_w1_up, m_w1_down, m_norm_mix, m_w_in, m_ssm_log_dt, m_ssm_a_re, m_ssm_a_im, m_ssm_b_re, m_ssm_b_im, m_ssm_c_re, m_ssm_c_im, m_ssm_d, m_ssm_w_glu, m_gmlp_norm_v, m_gmlp_w_s, m_gmlp_b_s, m_norm_ssm_out, m_norm_gmlp_out, m_w_out, m_norm_ffn2, m_w2_gate, m_w2_up, m_w2_down, m_norm_ple, m_w_ple_gate, m_w_ple_proj, m_norm_final, v_norm_ffn1, v_w1_gate, v_w1_up, v_w1_down, v_norm_mix, v_w_in, v_ssm_log_dt, v_ssm_a_re, v_ssm_a_im, v_ssm_b_re, v_ssm_b_im, v_ssm_c_re, v_ssm_c_im, v_ssm_d, v_ssm_w_glu, v_gmlp_norm_v, v_gmlp_w_s, v_gmlp_b_s, v_norm_ssm_out, v_norm_gmlp_out, v_w_out, v_norm_ffn2, v_w2_gate, v_w2_up, v_w2_down, v_norm_ple, v_w_ple_gate, v_w_ple_proj, v_norm_final):
    given = dict(x=x, p=p, norm_ffn1=norm_ffn1, w1_gate=w1_gate, w1_up=w1_up, w1_down=w1_down, norm_mix=norm_mix, w_in=w_in, ssm_log_dt=ssm_log_dt, ssm_a_re=ssm_a_re, ssm_a_im=ssm_a_im, ssm_b_re=ssm_b_re, ssm_b_im=ssm_b_im, ssm_c_re=ssm_c_re, ssm_c_im=ssm_c_im, ssm_d=ssm_d, ssm_w_glu=ssm_w_glu, gmlp_norm_v=gmlp_norm_v, gmlp_w_s=gmlp_w_s, gmlp_b_s=gmlp_b_s, norm_ssm_out=norm_ssm_out, norm_gmlp_out=norm_gmlp_out, w_out=w_out, norm_ffn2=norm_ffn2, w2_gate=w2_gate, w2_up=w2_up, w2_down=w2_down, norm_ple=norm_ple, w_ple_gate=w_ple_gate, w_ple_proj=w_ple_proj, norm_final=norm_final, loss_target=loss_target, m_norm_ffn1=m_norm_ffn1, m_w1_gate=m_w1_gate, m_w1_up=m_w1_up, m_w1_down=m_w1_down, m_norm_mix=m_norm_mix, m_w_in=m_w_in, m_ssm_log_dt=m_ssm_log_dt, m_ssm_a_re=m_ssm_a_re, m_ssm_a_im=m_ssm_a_im, m_ssm_b_re=m_ssm_b_re, m_ssm_b_im=m_ssm_b_im, m_ssm_c_re=m_ssm_c_re, m_ssm_c_im=m_ssm_c_im, m_ssm_d=m_ssm_d, m_ssm_w_glu=m_ssm_w_glu, m_gmlp_norm_v=m_gmlp_norm_v, m_gmlp_w_s=m_gmlp_w_s, m_gmlp_b_s=m_gmlp_b_s, m_norm_ssm_out=m_norm_ssm_out, m_norm_gmlp_out=m_norm_gmlp_out, m_w_out=m_w_out, m_norm_ffn2=m_norm_ffn2, m_w2_gate=m_w2_gate, m_w2_up=m_w2_up, m_w2_down=m_w2_down, m_norm_ple=m_norm_ple, m_w_ple_gate=m_w_ple_gate, m_w_ple_proj=m_w_ple_proj, m_norm_final=m_norm_final, v_norm_ffn1=v_norm_ffn1, v_w1_gate=v_w1_gate, v_w1_up=v_w1_up, v_w1_down=v_w1_down, v_norm_mix=v_norm_mix, v_w_in=v_w_in, v_ssm_log_dt=v_ssm_log_dt, v_ssm_a_re=v_ssm_a_re, v_ssm_a_im=v_ssm_a_im, v_ssm_b_re=v_ssm_b_re, v_ssm_b_im=v_ssm_b_im, v_ssm_c_re=v_ssm_c_re, v_ssm_c_im=v_ssm_c_im, v_ssm_d=v_ssm_d, v_ssm_w_glu=v_ssm_w_glu, v_gmlp_norm_v=v_gmlp_norm_v, v_gmlp_w_s=v_gmlp_w_s, v_gmlp_b_s=v_gmlp_b_s, v_norm_ssm_out=v_norm_ssm_out, v_norm_gmlp_out=v_norm_gmlp_out, v_w_out=v_w_out, v_norm_ffn2=v_norm_ffn2, v_w2_gate=v_w2_gate, v_w2_up=v_w2_up, v_w2_down=v_w2_down, v_norm_ple=v_norm_ple, v_w_ple_gate=v_w_ple_gate, v_w_ple_proj=v_w_ple_proj, v_norm_final=v_norm_final)
    weights = {n: given[n] for n in TWIN_WEIGHTS}
    shared = {n: given[n] for n in SHARED_INPUTS}
    per_example = {n: given[n] for n in ['x', 'p']}
    grad_fn = _jax.value_and_grad(_loss, argnums=(0, 1))

    def one_microbatch(ex, loss_target):
        ex = dict(ex)
        diff = ex.pop(TWIN_DIFF_INPUT)
        return grad_fn(weights, diff, {**shared, **ex}, loss_target)

    if N_MICROBATCH == 1:
        loss, (grad_w, grad_x) = one_microbatch(per_example, given["loss_target"])
    else:
        def body(carry, xs):
            loss_sum, grad_sum = carry
            l_k, (gw_k, gx_k) = one_microbatch(xs[0], xs[1])
            with _jax.named_scope("update"):
                return (loss_sum + l_k, _jax.tree.map(_jnp.add, grad_sum, gw_k)), gx_k

        init = (_jnp.zeros((), _jnp.float32), _jax.tree.map(_jnp.zeros_like, weights))
        (loss, grad_w), grad_x = _jax.lax.scan(body, init, (per_example, given["loss_target"]))
    with _jax.named_scope("update"):
        delta_w, new_m, new_v = {}, {}, {}
        for n in TWIN_WEIGHTS:
            delta_w[n], new_m[n], new_v[n] = _adamw(weights[n], grad_w[n], given["m_" + n], given["v_" + n])
    return (loss, grad_x, *[grad_w[n] for n in TWIN_WEIGHTS], *[delta_w[n] for n in TWIN_WEIGHTS],
            *[new_m[n] for n in TWIN_WEIGHTS], *[new_v[n] for n in TWIN_WEIGHTS])
```

```python
import math

import jax
import jax.numpy as jnp
from jax import lax
from jax.experimental import pallas as pl
from jax.experimental.pallas import tpu as pltpu

F32 = jnp.float32
BF16 = jnp.bfloat16
MESH_DT = pl.DeviceIdType.MESH

N_DEV = 8
N_CHIP = 4
LANE = 128
SUBLANE = 8
VMEM_LIMIT = 56 * 1024 * 1024

EPS = 1e-6
SSM_GROUP = 16
SSM_STATE = 64
GROUPS_PER_BLOCK = LANE // SSM_GROUP
STATE_BLOCK = GROUPS_PER_BLOCK * SSM_STATE
GMLP_HEAD = 128
CHUNK = 128

ADAM_LR = 0.001
ADAM_B1 = 0.9
ADAM_B2 = 0.999
ADAM_EPS = 1e-08
ADAM_WD = 0.01
ADAM_STEP = 10

GELU_K = math.sqrt(2.0 / math.pi)
GELU_C = 0.044715


def _cparams():
    return pltpu.CompilerParams(vmem_limit_bytes=VMEM_LIMIT)


def _tile(n, pref):
    if n <= pref:
        return n
    t = (pref // LANE) * LANE
    while t > 0:
        if n % t == 0:
            return t
        t -= LANE
    return n


def _row_tile(n, pref):
    if n <= pref:
        return n
    t = (pref // SUBLANE) * SUBLANE
    while t > 0:
        if n % t == 0:
            return t
        t -= SUBLANE
    return n


def _gelu(x):
    t = jnp.tanh(GELU_K * (x + GELU_C * x * x * x))
    return 0.5 * x * (1.0 + t)


def _gelu_grad(x):
    t = jnp.tanh(GELU_K * (x + GELU_C * x * x * x))
    return 0.5 * (1.0 + t) + 0.5 * x * (1.0 - t * t) * (GELU_K * (1.0 + 3.0 * GELU_C * x * x))


def _sigmoid(x):
    return 1.0 / (1.0 + jnp.exp(-x))


_DN = {
    "nn": (((1,), (0,)), ((), ())),
    "nt": (((1,), (1,)), ((), ())),
    "tn": (((0,), (0,)), ((), ())),
}


def _dot(a, b, mode="nn"):
    return lax.dot_general(a, b, _DN[mode], preferred_element_type=F32)


def _mm_dims(a, b, mode):
    if mode == "nn":
        (m, k), (k2, n) = a.shape, b.shape
    elif mode == "nt":
        (m, k), (n, k2) = a.shape, b.shape
    else:
        (k, m), (k2, n) = a.shape, b.shape
    assert k == k2, (a.shape, b.shape, mode)
    return m, n, k


def _mm_specs(mode, tm, tn, tk):
    if mode == "tn":
        a_spec = pl.BlockSpec((tk, tm), lambda i, j, k: (k, i))
    else:
        a_spec = pl.BlockSpec((tm, tk), lambda i, j, k: (i, k))
    if mode == "nt":
        b_spec = pl.BlockSpec((tn, tk), lambda i, j, k: (j, k))
    else:
        b_spec = pl.BlockSpec((tk, tn), lambda i, j, k: (k, j))
    return a_spec, b_spec


def matmul(a, b, mode, name, out_dtype=F32, res=None, scale=1.0, tm=1024, tn=1024, tk=512):
    m, n, k = _mm_dims(a, b, mode)
    tm, tn, tk = _tile(m, tm), _tile(n, tn), _tile(k, tk)
    nk = k // tk
    a_spec, b_spec = _mm_specs(mode, tm, tn, tk)
    o_spec = pl.BlockSpec((tm, tn), lambda i, j, k: (i, j))
    has_res = res is not None

    def body(*refs):
        if has_res:
            a_ref, b_ref, r_ref, o_ref, acc = refs
        else:
            a_ref, b_ref, o_ref, acc = refs
        kk = pl.program_id(2)

        @pl.when(kk == 0)
        def _():
            acc[...] = jnp.zeros_like(acc)

        acc[...] += _dot(a_ref[...], b_ref[...], mode)

        @pl.when(kk == nk - 1)
        def _():
            v = acc[...]
            if scale != 1.0:
                v = v * scale
            if has_res:
                v = r_ref[...] + v
            o_ref[...] = v.astype(out_dtype)

    in_specs = [a_spec, b_spec] + ([o_spec] if has_res else [])
    args = (a, b) + ((res,) if has_res else ())
    return pl.pallas_call(
        body, name=name, grid=(m // tm, n // tn, nk), in_specs=in_specs, out_specs=o_spec,
        out_shape=jax.ShapeDtypeStruct((m, n), out_dtype),
        scratch_shapes=[pltpu.VMEM((tm, tn), F32)], compiler_params=_cparams(),
    )(*args)


def ffn_up(xn, wg, wu, name, tm=1024, tn=1024, tk=512):
    m, n, k = _mm_dims(xn, wg, "nn")
    tm, tn, tk = _tile(m, tm), _tile(n, tn), _tile(k, tk)
    nk = k // tk
    a_spec, b_spec = _mm_specs("nn", tm, tn, tk)
    o_spec = pl.BlockSpec((tm, tn), lambda i, j, k: (i, j))

    def body(a_ref, g_ref, u_ref, gate_ref, up_ref, act_ref, accg, accu):
        kk = pl.program_id(2)

        @pl.when(kk == 0)
        def _():
            accg[...] = jnp.zeros_like(accg)
            accu[...] = jnp.zeros_like(accu)

        a = a_ref[...]
        accg[...] += _dot(a, g_ref[...])
        accu[...] += _dot(a, u_ref[...])

        @pl.when(kk == nk - 1)
        def _():
            g = accg[...]
            u = accu[...]
            gate_ref[...] = g
            up_ref[...] = u
            act_ref[...] = (g * _sigmoid(g) * u).astype(BF16)

    return pl.pallas_call(
        body, name=name, grid=(m // tm, n // tn, nk), in_specs=[a_spec, b_spec, b_spec],
        out_specs=[o_spec, o_spec, o_spec],
        out_shape=[jax.ShapeDtypeStruct((m, n), F32), jax.ShapeDtypeStruct((m, n), F32),
                   jax.ShapeDtypeStruct((m, n), BF16)],
        scratch_shapes=[pltpu.VMEM((tm, tn), F32), pltpu.VMEM((tm, tn), F32)], compiler_params=_cparams(),
    )(xn, wg, wu)


def ffn_bwd_act(dh, wd, gate, up, name, tm=1024, tn=1024, tk=512):
    m, n, k = _mm_dims(dh, wd, "nt")
    tm, tn, tk = _tile(m, tm), _tile(n, tn), _tile(k, tk)
    nk = k // tk
    a_spec, b_spec = _mm_specs("nt", tm, tn, tk)
    o_spec = pl.BlockSpec((tm, tn), lambda i, j, k: (i, j))

    def body(a_ref, b_ref, gate_ref, up_ref, dg_ref, du_ref, acc):
        kk = pl.program_id(2)

        @pl.when(kk == 0)
        def _():
            acc[...] = jnp.zeros_like(acc)

        acc[...] += _dot(a_ref[...], b_ref[...], "nt")

        @pl.when(kk == nk - 1)
        def _():
            dact = 0.5 * acc[...]
            g = gate_ref[...]
            sg = _sigmoid(g)
            du_ref[...] = (dact * (g * sg)).astype(BF16)
            dg_ref[...] = (dact * up_ref[...] * (sg * (1.0 + g * (1.0 - sg)))).astype(BF16)

    return pl.pallas_call(
        body, name=name, grid=(m // tm, n // tn, nk), in_specs=[a_spec, b_spec, o_spec, o_spec],
        out_specs=[o_spec, o_spec],
        out_shape=[jax.ShapeDtypeStruct((m, n), BF16), jax.ShapeDtypeStruct((m, n), BF16)],
        scratch_shapes=[pltpu.VMEM((tm, tn), F32)], compiler_params=_cparams(),
    )(dh, wd, gate, up)


def _rows(t, d, tr):
    return pl.BlockSpec((tr, d), lambda i: (i, 0))


def _vec(d):
    return pl.BlockSpec((1, d), lambda i: (0, 0))


def rmsnorm_fwd(x, g, name, tr=512):
    t, d = x.shape
    tr = _row_tile(t, tr)

    def body(x_ref, g_ref, o_ref):
        xf = x_ref[...]
        r = lax.rsqrt(jnp.mean(xf * xf, axis=-1, keepdims=True) + EPS)
        o_ref[...] = (xf * r * g_ref[...]).astype(BF16)

    return pl.pallas_call(
        body, name=name, grid=(t // tr,), in_specs=[_rows(t, d, tr), _vec(d)], out_specs=_rows(t, d, tr),
        out_shape=jax.ShapeDtypeStruct((t, d), BF16), compiler_params=_cparams(),
    )(x, g)


def _rms_bwd(dxn, xf, g):
    r = lax.rsqrt(jnp.mean(xf * xf, axis=-1, keepdims=True) + EPS)
    xhat = xf * r
    dg = jnp.sum(dxn * xhat, axis=0, keepdims=True)
    dxh = dxn * g
    dx = r * (dxh - xhat * jnp.mean(dxh * xhat, axis=-1, keepdims=True))
    return dx, dg


def rmsnorm_bwd(dxn, x, g, dres, name, tr=256):
    t, d = x.shape
    tr = _row_tile(t, tr)

    def body(dxn_ref, x_ref, g_ref, dres_ref, o_ref, ob_ref, dg_ref):
        dx, dg = _rms_bwd(dxn_ref[...], x_ref[...], g_ref[...])
        out = dres_ref[...] + dx
        o_ref[...] = out
        ob_ref[...] = out.astype(BF16)

        @pl.when(pl.program_id(0) == 0)
        def _():
            dg_ref[...] = jnp.zeros_like(dg_ref)

        dg_ref[...] += dg

    return pl.pallas_call(
        body, name=name, grid=(t // tr,),
        in_specs=[_rows(t, d, tr), _rows(t, d, tr), _vec(d), _rows(t, d, tr)],
        out_specs=[_rows(t, d, tr), _rows(t, d, tr), _vec(d)],
        out_shape=[jax.ShapeDtypeStruct((t, d), F32), jax.ShapeDtypeStruct((t, d), BF16),
                   jax.ShapeDtypeStruct((1, d), F32)],
        compiler_params=_cparams(),
    )(dxn, x, g, dres)


def final_loss(h, target, g, name, tr=256):
    t, d = h.shape
    tr = _row_tile(t, tr)

    def body(h_ref, t_ref, g_ref, dh_ref, loss_ref, dg_ref):
        xf = h_ref[...]
        gg = g_ref[...]
        r = lax.rsqrt(jnp.mean(xf * xf, axis=-1, keepdims=True) + EPS)
        xhat = xf * r
        e = xhat * gg - t_ref[...]
        part = jnp.sum(jnp.sum(e * e, axis=1, keepdims=True), axis=0, keepdims=True) * (0.5 / d)
        dout = e * (1.0 / d)
        dg = jnp.sum(dout * xhat, axis=0, keepdims=True)
        dxh = dout * gg
        dh_ref[...] = r * (dxh - xhat * jnp.mean(dxh * xhat, axis=-1, keepdims=True))

        @pl.when(pl.program_id(0) == 0)
        def _():
            dg_ref[...] = jnp.zeros_like(dg_ref)
            loss_ref[...] = jnp.zeros_like(loss_ref)

        dg_ref[...] += dg
        loss_ref[...] += jnp.broadcast_to(part, loss_ref.shape)

    return pl.pallas_call(
        body, name=name, grid=(t // tr,),
        in_specs=[_rows(t, d, tr), _rows(t, d, tr), _vec(d)],
        out_specs=[_rows(t, d, tr), pl.BlockSpec((SUBLANE, LANE), lambda i: (0, 0)), _vec(d)],
        out_shape=[jax.ShapeDtypeStruct((t, d), F32), jax.ShapeDtypeStruct((SUBLANE, LANE), F32),
                   jax.ShapeDtypeStruct((1, d), F32)],
        compiler_params=_cparams(),
    )(h, target, g)


def ple_fwd(h, glin, pp, name, tr=512):
    t, d = h.shape
    tr = _row_tile(t, tr)

    def body(h_ref, gl_ref, pp_ref, o_ref):
        o_ref[...] = h_ref[...] + _sigmoid(gl_ref[...]) * pp_ref[...]

    sp = _rows(t, d, tr)
    return pl.pallas_call(
        body, name=name, grid=(t // tr,), in_specs=[sp, sp, sp], out_specs=sp,
        out_shape=jax.ShapeDtypeStruct((t, d), F32), compiler_params=_cparams(),
    )(h, glin, pp)


def ple_bwd(dh, glin, pp, name, tr=512):
    t, d = dh.shape
    tr = _row_tile(t, tr)

    def body(dh_ref, gl_ref, pp_ref, dpp_ref, dgl_ref):
        gate = _sigmoid(gl_ref[...])
        dh_ = dh_ref[...]
        dpp_ref[...] = (dh_ * gate).astype(BF16)
        dgl_ref[...] = (dh_ * pp_ref[...] * gate * (1.0 - gate)).astype(BF16)

    sp = _rows(t, d, tr)
    return pl.pallas_call(
        body, name=name, grid=(t // tr,), in_specs=[sp, sp, sp], out_specs=[sp, sp],
        out_shape=[jax.ShapeDtypeStruct((t, d), BF16), jax.ShapeDtypeStruct((t, d), BF16)],
        compiler_params=_cparams(),
    )(dh, glin, pp)


def mix_out_fwd(y_pre, glin, y_gmlp, g_so, g_go, name, tr=512):
    t, d = y_pre.shape
    tr = _row_tile(t, tr)

    def body(yp_ref, gl_ref, yg_ref, gs_ref, gg_ref, o_ref):
        ys = _gelu(yp_ref[...]) * _sigmoid(gl_ref[...])
        r = lax.rsqrt(jnp.mean(ys * ys, axis=-1, keepdims=True) + EPS)
        o_ref[:, 0:d] = (ys * r * gs_ref[...]).astype(BF16)
        yq = yg_ref[...]
        r2 = lax.rsqrt(jnp.mean(yq * yq, axis=-1, keepdims=True) + EPS)
        o_ref[:, d:2 * d] = (yq * r2 * gg_ref[...]).astype(BF16)

    sp = _rows(t, d, tr)
    return pl.pallas_call(
        body, name=name, grid=(t // tr,), in_specs=[sp, sp, sp, _vec(d), _vec(d)],
        out_specs=_rows(t, 2 * d, tr), out_shape=jax.ShapeDtypeStruct((t, 2 * d), BF16),
        compiler_params=_cparams(),
    )(y_pre, glin, y_gmlp, g_so, g_go)


def mix_out_bwd(dycat, y_pre, glin, y_gmlp, g_so, g_go, name, tr=256):
    t, d = y_pre.shape
    tr = _row_tile(t, tr)

    def body(dy_ref, yp_ref, gl_ref, yg_ref, gs_ref, gg_ref, dyg_ref, dl_ref, dyq_ref, dgs_ref, dgg_ref):
        yg = _gelu(yp_ref[...])
        sg = _sigmoid(gl_ref[...])
        dys, dgs = _rms_bwd(dy_ref[:, 0:d], yg * sg, gs_ref[...])
        dyg_ref[...] = dys * sg
        dl_ref[...] = (dys * yg * sg * (1.0 - sg)).astype(BF16)
        dyq, dgg = _rms_bwd(dy_ref[:, d:2 * d], yg_ref[...], gg_ref[...])
        dyq_ref[...] = dyq

        @pl.when(pl.program_id(0) == 0)
        def _():
            dgs_ref[...] = jnp.zeros_like(dgs_ref)
            dgg_ref[...] = jnp.zeros_like(dgg_ref)

        dgs_ref[...] += dgs
        dgg_ref[...] += dgg

    sp = _rows(t, d, tr)
    return pl.pallas_call(
        body, name=name, grid=(t // tr,),
        in_specs=[_rows(t, 2 * d, tr), sp, sp, sp, _vec(d), _vec(d)],
        out_specs=[sp, sp, sp, _vec(d), _vec(d)],
        out_shape=[jax.ShapeDtypeStruct((t, d), F32), jax.ShapeDtypeStruct((t, d), BF16),
                   jax.ShapeDtypeStruct((t, d), F32), jax.ShapeDtypeStruct((1, d), F32),
                   jax.ShapeDtypeStruct((1, d), F32)],
        compiler_params=_cparams(),
    )(dycat, y_pre, glin, y_gmlp, g_so, g_go)


SCAN_COLS = 512


def _scan_tile(xr, xi, const, cr, ci, reverse):
    for lvl, sh in enumerate((1, 2, 4)):
        ar, ai = const(2 * lvl), const(2 * lvl + 1)
        s = (SUBLANE - sh) if reverse else sh
        rr = pltpu.roll(xr, s, 0)
        ri = pltpu.roll(xi, s, 0)
        xr, xi = xr + ar * rr - ai * ri, xi + ar * ri + ai * rr
    pr, pi_ = const(6), const(7)
    xr, xi = xr + pr * cr - pi_ * ci, xi + pr * ci + pi_ * cr
    return xr, xi


def _bcast_row(x, row):
    return jnp.broadcast_to(x[row:row + 1, :], x.shape)


def s5_fwd(z, bc_r, bc_i, cc_r, cc_i, apw, dvec, name, tc=128):
    t = z.shape[0]
    nblk = bc_r.shape[0]
    d = nblk * LANE
    ns = nblk * STATE_BLOCK
    tc = _row_tile(t, tc)
    ntile = tc // SUBLANE

    def body(z_ref, br_ref, bi_ref, cr_ref, ci_ref, apw_ref, d_ref, y_ref, yg_ref, sr_ref, si_ref, carry):
        @pl.when(pl.program_id(0) == 0)
        def _():
            carry[...] = jnp.zeros_like(carry)

        for j in range(nblk):
            uj = z_ref[:, j * LANE:(j + 1) * LANE]
            ub = uj.astype(BF16)
            for q in range(STATE_BLOCK // SCAN_COLS):
                c0 = j * STATE_BLOCK + q * SCAN_COLS
                cs = pl.ds(c0, SCAN_COLS)
                bs = slice(q * SCAN_COLS, (q + 1) * SCAN_COLS)
                sr_ref[:, cs] = _dot(ub, br_ref[j, :, bs])
                si_ref[:, cs] = _dot(ub, bi_ref[j, :, bs])
                const = lambda k, cs=cs: apw_ref[k, :, cs]

                def tile(k, c, cs=cs, const=const):
                    rows = pl.ds(pl.multiple_of(k * SUBLANE, SUBLANE), SUBLANE)
                    xr, xi = _scan_tile(sr_ref[rows, cs], si_ref[rows, cs], const, c[0], c[1], False)
                    sr_ref[rows, cs] = xr
                    si_ref[rows, cs] = xi
                    return _bcast_row(xr, SUBLANE - 1), _bcast_row(xi, SUBLANE - 1)

                c_r, c_i = lax.fori_loop(0, ntile, tile, (carry[0, :, cs], carry[1, :, cs]))
                carry[0, :, cs] = c_r
                carry[1, :, cs] = c_i
            sb = pl.ds(j * STATE_BLOCK, STATE_BLOCK)
            y = (_dot(sr_ref[:, sb].astype(BF16), cr_ref[j]) - _dot(si_ref[:, sb].astype(BF16), ci_ref[j])
                 + d_ref[:, j * LANE:(j + 1) * LANE] * uj)
            y_ref[:, j * LANE:(j + 1) * LANE] = y
            yg_ref[:, j * LANE:(j + 1) * LANE] = _gelu(y).astype(BF16)

    full3 = lambda shp: pl.BlockSpec(shp, lambda i: (0, 0, 0))
    return pl.pallas_call(
        body, name=name, grid=(t // tc,),
        in_specs=[pl.BlockSpec((tc, d), lambda i: (i, 0)), full3(bc_r.shape), full3(bc_i.shape),
                  full3(cc_r.shape), full3(cc_i.shape), full3(apw.shape), _vec(d)],
        out_specs=[pl.BlockSpec((tc, d), lambda i: (i, 0)), pl.BlockSpec((tc, d), lambda i: (i, 0)),
                   pl.BlockSpec((tc, ns), lambda i: (i, 0)), pl.BlockSpec((tc, ns), lambda i: (i, 0))],
        out_shape=[jax.ShapeDtypeStruct((t, d), F32), jax.ShapeDtypeStruct((t, d), BF16),
                   jax.ShapeDtypeStruct((t, ns), F32), jax.ShapeDtypeStruct((t, ns), F32)],
        scratch_shapes=[pltpu.VMEM((2, SUBLANE, ns), F32)], compiler_params=_cparams(),
    )(z, bc_r, bc_i, cc_r, cc_i, apw, dvec)


def s5_bwd(dyg, y_pre, z, sr, si, bc_r, bc_i, cc_r, cc_i, apw_rev, dvec, name, tc=128):
    t = z.shape[0]
    nblk = bc_r.shape[0]
    d = nblk * LANE
    ns = nblk * STATE_BLOCK
    tc = _row_tile(t, tc)
    ntile = tc // SUBLANE
    nchunk = t // tc
    tiles_per_chunk = tc // SUBLANE

    def body(dyg_ref, yp_ref, z_ref, sr_ref, si_ref, pr_ref, pi_ref, br_ref, bi_ref, cr_ref, ci_ref, apw_ref,
             d_ref, du_ref, gd_ref, gcr_ref, gci_ref, gbr_ref, gbi_ref, gar_ref, gai_ref, lr_ref, li_ref, carry):
        step = pl.program_id(0)

        @pl.when(step == 0)
        def _():
            carry[...] = jnp.zeros_like(carry)
            for ref in (gd_ref, gcr_ref, gci_ref, gbr_ref, gbi_ref, gar_ref, gai_ref):
                ref[...] = jnp.zeros_like(ref)

        first_chunk = (step == nchunk - 1).astype(F32)
        keep_prev = 1.0 - first_chunk
        row0 = lax.broadcasted_iota(jnp.int32, (SUBLANE, SCAN_COLS), 0) == 0

        for j in range(nblk):
            lanes = slice(j * LANE, (j + 1) * LANE)
            uj = z_ref[:, lanes]
            ub = uj.astype(BF16)
            gy = dyg_ref[:, lanes] * _gelu_grad(yp_ref[:, lanes])
            gyb = gy.astype(BF16)
            gd_ref[:, lanes] += jnp.sum(gy * uj, axis=0, keepdims=True)
            for q in range(STATE_BLOCK // SCAN_COLS):
                c0 = j * STATE_BLOCK + q * SCAN_COLS
                cs = pl.ds(c0, SCAN_COLS)
                bs = slice(q * SCAN_COLS, (q + 1) * SCAN_COLS)
                lr_ref[:, cs] = _dot(gyb, cr_ref[j, bs, :], "nt")
                li_ref[:, cs] = -_dot(gyb, ci_ref[j, bs, :], "nt")
                const = lambda k, cs=cs: apw_ref[k, :, cs]

                def one_tile(rows, prev_r, prev_i, c, cs=cs, const=const):
                    cr_, ci_, gar, gai = c
                    xr, xi = _scan_tile(lr_ref[rows, cs], li_ref[rows, cs], const, cr_, ci_, True)
                    lr_ref[rows, cs] = xr
                    li_ref[rows, cs] = xi
                    spr = jnp.where(row0, prev_r, pltpu.roll(sr_ref[rows, cs], 1, 0))
                    spi = jnp.where(row0, prev_i, pltpu.roll(si_ref[rows, cs], 1, 0))
                    gar = gar + xr * spr + xi * spi
                    gai = gai + xi * spr - xr * spi
                    return _bcast_row(xr, 0), _bcast_row(xi, 0), gar, gai

                def tile(k, c, cs=cs, one_tile=one_tile):
                    kk = ntile - 1 - k
                    rows = pl.ds(pl.multiple_of(kk * SUBLANE, SUBLANE), SUBLANE)
                    prow = pl.ds(pl.multiple_of((kk - 1) * SUBLANE, SUBLANE), SUBLANE)
                    prev_r = _bcast_row(sr_ref[prow, cs], SUBLANE - 1)
                    prev_i = _bcast_row(si_ref[prow, cs], SUBLANE - 1)
                    return one_tile(rows, prev_r, prev_i, c)

                zero = jnp.zeros((SUBLANE, SCAN_COLS), F32)
                c = lax.fori_loop(0, ntile - 1, tile, (carry[0, :, cs], carry[1, :, cs], zero, zero))
                prev_r = _bcast_row(pr_ref[:, cs], SUBLANE - 1) * keep_prev
                prev_i = _bcast_row(pi_ref[:, cs], SUBLANE - 1) * keep_prev
                c_r, c_i, gar, gai = one_tile(pl.ds(0, SUBLANE), prev_r, prev_i, c)
                carry[0, :, cs] = c_r
                carry[1, :, cs] = c_i
                gar_ref[:, cs] += gar
                gai_ref[:, cs] += gai
            sb = pl.ds(j * STATE_BLOCK, STATE_BLOCK)
            lrb = lr_ref[:, sb].astype(BF16)
            lib = li_ref[:, sb].astype(BF16)
            gcr_ref[j] += _dot(gyb, sr_ref[:, sb].astype(BF16), "tn")
            gci_ref[j] -= _dot(gyb, si_ref[:, sb].astype(BF16), "tn")
            gbr_ref[j] += _dot(ub, lrb, "tn")
            gbi_ref[j] += _dot(ub, lib, "tn")
            du = _dot(lrb, br_ref[j], "nt") + _dot(lib, bi_ref[j], "nt") + gy * d_ref[:, lanes]
            du_ref[:, lanes] = du.astype(BF16)

    rev = lambda i: (nchunk - 1 - i, 0)
    prev = lambda i: (jnp.maximum((nchunk - 1 - i) * tiles_per_chunk - 1, 0), 0)
    full3 = lambda shp: pl.BlockSpec(shp, lambda i: (0, 0, 0))
    acc3 = pl.BlockSpec((nblk, LANE, STATE_BLOCK), lambda i: (0, 0, 0))
    acc_rows = pl.BlockSpec((SUBLANE, ns), lambda i: (0, 0))
    return pl.pallas_call(
        body, name=name, grid=(nchunk,),
        in_specs=[pl.BlockSpec((tc, d), rev), pl.BlockSpec((tc, d), rev), pl.BlockSpec((tc, d), rev),
                  pl.BlockSpec((tc, ns), rev), pl.BlockSpec((tc, ns), rev),
                  pl.BlockSpec((SUBLANE, ns), prev), pl.BlockSpec((SUBLANE, ns), prev),
                  full3(bc_r.shape), full3(bc_i.shape), full3(cc_r.shape), full3(cc_i.shape), full3(apw_rev.shape),
                  _vec(d)],
        out_specs=[pl.BlockSpec((tc, d), rev), _vec(d), acc3, acc3, acc3, acc3, acc_rows, acc_rows],
        out_shape=[jax.ShapeDtypeStruct((t, d), BF16), jax.ShapeDtypeStruct((1, d), F32)]
        + [jax.ShapeDtypeStruct((nblk, LANE, STATE_BLOCK), F32)] * 4
        + [jax.ShapeDtypeStruct((SUBLANE, ns), F32)] * 2,
        scratch_shapes=[pltpu.VMEM((tc, ns), F32), pltpu.VMEM((tc, ns), F32), pltpu.VMEM((2, SUBLANE, ns), F32)],
        compiler_params=_cparams(),
    )(dyg, y_pre, z, sr, si, sr, si, bc_r, bc_i, cc_r, cc_i, apw_rev, dvec)


def _cmul(a, b):
    return a[0] * b[0] - a[1] * b[1], a[0] * b[1] + a[1] * b[0]


def _scan_constants(abar_r, abar_i, reverse):
    ar = abar_r.reshape(1, -1)
    ai = abar_i.reshape(1, -1)
    if reverse:
        ai = -ai
    pw = [(ar, ai)]
    for _ in range(SUBLANE - 1):
        pw.append(_cmul(pw[-1], (ar, ai)))
    rows = lax.broadcasted_iota(jnp.int32, (SUBLANE, 1), 0)
    out = []
    for sh in (1, 2, 4):
        keep = (rows <= SUBLANE - 1 - sh) if reverse else (rows >= sh)
        for part in pw[sh - 1]:
            out.append(jnp.where(keep, part, 0.0))
    for comp in (0, 1):
        stack = jnp.concatenate([pw[k][comp] for k in range(SUBLANE)], axis=0)
        out.append(stack[::-1] if reverse else stack)
    return jnp.stack(out, axis=0).astype(F32)


def _ssm_discretize(log_dt, a_re, a_im, b_re, b_im):
    dt = jnp.exp(log_dt)[:, None]
    lr = jnp.minimum(a_re, -1e-4)
    li = a_im
    mag = jnp.exp(lr * dt)
    ang = li * dt
    abar_r = mag * jnp.cos(ang)
    abar_i = mag * jnp.sin(ang)
    den = lr * lr + li * li
    xr = abar_r - 1.0
    xi = abar_i
    zr = (xr * lr + xi * li) / den
    zi = (xi * lr - xr * li) / den
    bbar_r = zr[..., None] * b_re - zi[..., None] * b_im
    bbar_i = zr[..., None] * b_im + zi[..., None] * b_re
    return abar_r, abar_i, bbar_r, bbar_i


def _block_diag(w):
    g, a, b = w.shape
    nb = g // GROUPS_PER_BLOCK
    eye = jnp.eye(GROUPS_PER_BLOCK, dtype=w.dtype)
    w5 = w.reshape(nb, GROUPS_PER_BLOCK, a, b)
    out = w5[:, :, :, None, :] * eye[None, :, None, :, None]
    return out.reshape(nb, GROUPS_PER_BLOCK * a, GROUPS_PER_BLOCK * b)


def _block_diag_extract(m, a, b):
    nb = m.shape[0]
    eye = jnp.eye(GROUPS_PER_BLOCK, dtype=m.dtype)
    m5 = m.reshape(nb, GROUPS_PER_BLOCK, a, GROUPS_PER_BLOCK, b)
    out = jnp.sum(m5 * eye[None, :, None, :, None], axis=3)
    return out.reshape(nb * GROUPS_PER_BLOCK, a, b)


def _layer_norm(gv, nv):
    mu = jnp.mean(gv, axis=-1, keepdims=True)
    xc = gv - mu
    r = lax.rsqrt(jnp.mean(xc * xc, axis=-1, keepdims=True) + EPS)
    xhat = xc * r
    return xhat * nv, xhat, r


def gmlp_fwd(z, norm_v, wm, bs, name, tr=256):
    t = z.shape[0]
    nh = wm.shape[0]
    d = nh * GMLP_HEAD
    col0 = (z.shape[1] - 2 * d) // d
    tr = _row_tile(t, tr)

    def body(zu_ref, zv_ref, nv_ref, wm_ref, bs_ref, o_ref):
        v, _, _ = _layer_norm(_gelu(zv_ref[...]), nv_ref[...])
        vb = v.astype(BF16)
        u = _gelu(zu_ref[...])
        for c in range(tr // CHUNK):
            rows = slice(c * CHUNK, (c + 1) * CHUNK)
            for h in range(nh):
                cols = slice(h * GMLP_HEAD, (h + 1) * GMLP_HEAD)
                s = _dot(wm_ref[h], vb[rows, cols]) + bs_ref[h]
                o_ref[rows, cols] = u[rows, cols] * s

    return pl.pallas_call(
        body, name=name, grid=(t // tr,),
        in_specs=[pl.BlockSpec((tr, d), lambda i: (i, col0)), pl.BlockSpec((tr, d), lambda i: (i, col0 + 1)),
                  _vec(d), pl.BlockSpec(wm.shape, lambda i: (0, 0, 0)), pl.BlockSpec(bs.shape, lambda i: (0, 0, 0))],
        out_specs=pl.BlockSpec((tr, d), lambda i: (i, 0)),
        out_shape=jax.ShapeDtypeStruct((t, d), F32), compiler_params=_cparams(),
    )(z, z, norm_v, wm, bs)


def gmlp_bwd(dy, z, norm_v, wm, wmt, bs, name, tr=256):
    t = z.shape[0]
    nh = wm.shape[0]
    d = nh * GMLP_HEAD
    col0 = (z.shape[1] - 2 * d) // d
    tr = _row_tile(t, tr)

    def body(dy_ref, zu_ref, zv_ref, nv_ref, wm_ref, wmt_ref, bs_ref, dzu_ref, dzv_ref, dnv_ref, dwm_ref, dbs_ref,
             dv_ref):
        @pl.when(pl.program_id(0) == 0)
        def _():
            dnv_ref[...] = jnp.zeros_like(dnv_ref)
            dwm_ref[...] = jnp.zeros_like(dwm_ref)
            dbs_ref[...] = jnp.zeros_like(dbs_ref)

        zv = zv_ref[...]
        nv = nv_ref[...]
        v, xhat, r = _layer_norm(_gelu(zv), nv)
        vb = v.astype(BF16)
        zu = zu_ref[...]
        u = _gelu(zu)
        dy_ = dy_ref[...]
        for c in range(tr // CHUNK):
            rows = slice(c * CHUNK, (c + 1) * CHUNK)
            for h in range(nh):
                cols = slice(h * GMLP_HEAD, (h + 1) * GMLP_HEAD)
                vh = vb[rows, cols]
                s = _dot(wm_ref[h], vh) + bs_ref[h]
                dyh = dy_[rows, cols]
                dzu_ref[rows, cols] = (dyh * s * _gelu_grad(zu[rows, cols])).astype(BF16)
                ds = dyh * u[rows, cols]
                dsb = ds.astype(BF16)
                dbs_ref[h] += jnp.sum(ds, axis=1, keepdims=True)
                dwm_ref[h] += _dot(dsb, vh, "nt")
                dv_ref[rows, cols] = _dot(wmt_ref[h], dsb)
        dv = dv_ref[...]
        dnv_ref[...] += jnp.sum(dv * xhat, axis=0, keepdims=True)
        dxh = dv * nv
        dgv = r * (dxh - jnp.mean(dxh, axis=-1, keepdims=True) - xhat * jnp.mean(dxh * xhat, axis=-1, keepdims=True))
        dzv_ref[...] = (dgv * _gelu_grad(zv)).astype(BF16)

    full3 = lambda shp: pl.BlockSpec(shp, lambda i: (0, 0, 0))
    rows_d = pl.BlockSpec((tr, d), lambda i: (i, 0))
    return pl.pallas_call(
        body, name=name, grid=(t // tr,),
        in_specs=[rows_d, pl.BlockSpec((tr, d), lambda i: (i, col0)), pl.BlockSpec((tr, d), lambda i: (i, col0 + 1)),
                  _vec(d), full3(wm.shape), full3(wmt.shape), full3(bs.shape)],
        out_specs=[rows_d, rows_d, _vec(d), full3((nh, CHUNK, CHUNK)), full3((nh, CHUNK, 1))],
        out_shape=[jax.ShapeDtypeStruct((t, d), BF16), jax.ShapeDtypeStruct((t, d), BF16),
                   jax.ShapeDtypeStruct((1, d), F32), jax.ShapeDtypeStruct((nh, CHUNK, CHUNK), F32),
                   jax.ShapeDtypeStruct((nh, CHUNK, 1), F32)],
        scratch_shapes=[pltpu.VMEM((tr, d), F32)], compiler_params=_cparams(),
    )(dy, z, z, norm_v, wm, wmt, bs)


def _block(ref, axis, size, k):
    start = pl.multiple_of(k * size, size)
    if axis == 0:
        return ref.at[pl.ds(start, size), :]
    return ref.at[:, pl.ds(start, size)]


def _place():
    x, y, c = lax.axis_index("x"), lax.axis_index("y"), lax.axis_index("c")
    chips = [(1 - x, y), (x, 1 - y), (1 - x, 1 - y)]
    return x, y, c, chips


def _dev(x, y, c):
    return 4 * x + 2 * y + c


def all_gather(shards, axes, name):
    n = len(shards)
    sizes = [s.shape[ax] for s, ax in zip(shards, axes)]
    out_shape = [
        jax.ShapeDtypeStruct((s.shape[0] * N_DEV, s.shape[1]) if ax == 0 else (s.shape[0], s.shape[1] * N_DEV), s.dtype)
        for s, ax in zip(shards, axes)
    ]

    def body(*refs):
        ins, outs = refs[:n], refs[n:2 * n]
        send_sems, recv_sems, local_sems = refs[2 * n:]
        x, y, c, chips = _place()
        me, sibling = (x, y, c), (x, y, 1 - c)

        def copy(t, k, block, to, from_input=False):
            dst = _block(outs[t], axes[t], sizes[t], _dev(*block))
            return pltpu.make_async_remote_copy(
                src_ref=ins[t] if from_input else dst, dst_ref=dst,
                send_sem=send_sems.at[t * 7 + k], recv_sem=recv_sems.at[t * 7 + k],
                device_id=to, device_id_type=MESH_DT)

        mine = [pltpu.make_async_copy(ins[t], _block(outs[t], axes[t], sizes[t], _dev(*me)), local_sems.at[t])
                for t in range(n)]
        for cp in mine:
            cp.start()
        sends = []
        for t in range(n):
            sends.append(copy(t, 0, me, sibling, True))
            sends += [copy(t, 1 + j, me, (*chip, c), True) for j, chip in enumerate(chips)]
        for cp in sends:
            cp.start()
        for t in range(n):
            for j, chip in enumerate(chips):
                copy(t, 1 + j, (*chip, c), me).wait_recv()
                fwd = copy(t, 4 + j, (*chip, c), sibling)
                fwd.start()
                sends.append(fwd)
        for t in range(n):
            copy(t, 0, sibling, me).wait_recv()
            for j, chip in enumerate(chips):
                copy(t, 4 + j, (*chip, 1 - c), me).wait_recv()
        for cp in sends:
            cp.wait_send()
        for cp in mine:
            cp.wait()

    any_spec = pl.BlockSpec(memory_space=pl.ANY)
    return pl.pallas_call(
        body, name=name, in_specs=[any_spec] * n, out_specs=[any_spec] * n, out_shape=out_shape,
        scratch_shapes=[pltpu.SemaphoreType.DMA((7 * n,)), pltpu.SemaphoreType.DMA((7 * n,)),
                        pltpu.SemaphoreType.DMA((n,))],
    )(*shards)


def _blk3(shape2, axis):
    r, c = shape2
    return (r // N_DEV, c) if axis == 0 else (r, c // N_DEV)


def rs_to_sibling(grads, axes, name):
    n = len(grads)
    blks = [_blk3(g.shape, ax) for g, ax in zip(grads, axes)]
    sizes = [b[ax] for b, ax in zip(blks, axes)]

    def body(*refs):
        ins, outs = refs[:n], refs[n:2 * n]
        send_sems, recv_sems = refs[2 * n:]
        x, y, c, _ = _place()
        sibling = (x, y, 1 - c)
        copies = []
        for t in range(n):
            for i in range(N_CHIP):
                copies.append(pltpu.make_async_remote_copy(
                    src_ref=_block(ins[t], axes[t], sizes[t], 2 * i + (1 - c)), dst_ref=outs[t].at[i],
                    send_sem=send_sems.at[t * N_CHIP + i], recv_sem=recv_sems.at[t * N_CHIP + i],
                    device_id=sibling, device_id_type=MESH_DT))
        for cp in copies:
            cp.start()
        for cp in copies:
            cp.wait_recv()
        for cp in copies:
            cp.wait_send()

    any_spec = pl.BlockSpec(memory_space=pl.ANY)
    return pl.pallas_call(
        body, name=name, in_specs=[any_spec] * n, out_specs=[any_spec] * n,
        out_shape=[jax.ShapeDtypeStruct((N_CHIP,) + b, g.dtype) for b, g in zip(blks, grads)],
        scratch_shapes=[pltpu.SemaphoreType.DMA((N_CHIP * n,)), pltpu.SemaphoreType.DMA((N_CHIP * n,))],
    )(*grads)


def rs_chip_sum(grad, recv, axis, core, name, tr=512):
    br, bc = _blk3(grad.shape, axis)
    tr = _row_tile(br, tr)
    nrb = br // tr

    if axis == 0:
        g_map = lambda i, r, c_ref: ((2 * i + c_ref[0]) * nrb + r, 0)
    else:
        g_map = lambda i, r, c_ref: (r, 2 * i + c_ref[0])

    def body(c_ref, g_ref, r_ref, o_ref):
        o_ref[...] = (g_ref[...].astype(F32) + r_ref[...].astype(F32)).astype(BF16)

    return pl.pallas_call(
        body, name=name,
        grid_spec=pltpu.PrefetchScalarGridSpec(
            num_scalar_prefetch=1, grid=(N_CHIP, nrb),
            in_specs=[pl.BlockSpec((tr, bc), g_map), pl.BlockSpec((None, tr, bc), lambda i, r, c_ref: (i, r, 0))],
            out_specs=pl.BlockSpec((None, tr, bc), lambda i, r, c_ref: (i, r, 0))),
        out_shape=jax.ShapeDtypeStruct((N_CHIP, br, bc), BF16), compiler_params=_cparams(),
    )(core, grad, recv)


def rs_across_chips(parts, name):
    n = len(parts)

    def body(*refs):
        ins, outs = refs[:n], refs[n:2 * n]
        send_sems, recv_sems, local_sems = refs[2 * n:]
        x, y, c, chips = _place()
        my_chip = 2 * x + y
        mine = [pltpu.make_async_copy(ins[t].at[my_chip], outs[t].at[my_chip], local_sems.at[t]) for t in range(n)]
        for cp in mine:
            cp.start()
        copies = []
        for t in range(n):
            for j, chip in enumerate(chips):
                to_chip = 2 * chip[0] + chip[1]
                copies.append(pltpu.make_async_remote_copy(
                    src_ref=ins[t].at[to_chip], dst_ref=outs[t].at[my_chip],
                    send_sem=send_sems.at[t * 3 + j], recv_sem=recv_sems.at[t * 3 + j],
                    device_id=(*chip, c), device_id_type=MESH_DT))
        for cp in copies:
            cp.start()
        for cp in copies:
            cp.wait_recv()
        for cp in copies:
            cp.wait_send()
        for cp in mine:
            cp.wait()

    any_spec = pl.BlockSpec(memory_space=pl.ANY)
    return pl.pallas_call(
        body, name=name, in_specs=[any_spec] * n, out_specs=[any_spec] * n,
        out_shape=[jax.ShapeDtypeStruct(p.shape, p.dtype) for p in parts],
        scratch_shapes=[pltpu.SemaphoreType.DMA((3 * n,)), pltpu.SemaphoreType.DMA((3 * n,)),
                        pltpu.SemaphoreType.DMA((n,))],
    )(*parts)


def _adamw(w, g, m, v):
    m = ADAM_B1 * m + (1.0 - ADAM_B1) * g
    v = ADAM_B2 * v + (1.0 - ADAM_B2) * (g * g)
    m_hat = m / (1.0 - ADAM_B1 ** ADAM_STEP)
    v_hat = v / (1.0 - ADAM_B2 ** ADAM_STEP)
    delta = -ADAM_LR * (m_hat / (jnp.sqrt(v_hat) + ADAM_EPS) + ADAM_WD * w)
    return delta, m, v


def _sum_chips(p_ref):
    g = p_ref[0].astype(F32)
    for i in range(1, N_CHIP):
        g = g + p_ref[i].astype(F32)
    return g


def adam_sharded(parts, w, m, v, name, tr=256):
    r, c = w.shape
    assert parts.shape[2] == c
    tr = _row_tile(r, tr)

    def body(p_ref, w_ref, m_ref, v_ref, g_ref, d_ref, nm_ref, nv_ref):
        g = _sum_chips(p_ref)
        delta, nm, nv = _adamw(w_ref[...], g, m_ref[...], v_ref[...])
        g_ref[...] = g
        d_ref[...] = delta
        nm_ref[...] = nm
        nv_ref[...] = nv

    sp = pl.BlockSpec((tr, c), lambda i: (i, 0))
    return pl.pallas_call(
        body, name=name, grid=(r // tr,),
        in_specs=[pl.BlockSpec((N_CHIP, tr, c), lambda i: (0, i, 0)), sp, sp, sp],
        out_specs=[sp, sp, sp, sp], out_shape=[jax.ShapeDtypeStruct((r, c), F32)] * 4,
        compiler_params=_cparams(),
    )(parts, w, m, v)


def sum_chips(parts, name, tr=256):
    _, r, c = parts.shape
    tr = _row_tile(r, tr)

    def body(p_ref, g_ref):
        g_ref[...] = _sum_chips(p_ref)

    return pl.pallas_call(
        body, name=name, grid=(r // tr,), in_specs=[pl.BlockSpec((N_CHIP, tr, c), lambda i: (0, i, 0))],
        out_specs=pl.BlockSpec((tr, c), lambda i: (i, 0)), out_shape=jax.ShapeDtypeStruct((r, c), F32),
        compiler_params=_cparams(),
    )(parts)


def adam_grid(g, w, m, v, name, tr=256):
    r, c = w.shape
    tr = _row_tile(r, tr)

    def body(g_ref, w_ref, m_ref, v_ref, d_ref, nm_ref, nv_ref):
        delta, nm, nv = _adamw(w_ref[...], g_ref[...], m_ref[...], v_ref[...])
        d_ref[...] = delta
        nm_ref[...] = nm
        nv_ref[...] = nv

    sp = pl.BlockSpec((tr, c), lambda i: (i, 0))
    return pl.pallas_call(
        body, name=name, grid=(r // tr,), in_specs=[sp, sp, sp, sp], out_specs=[sp, sp, sp],
        out_shape=[jax.ShapeDtypeStruct((r, c), F32)] * 3, compiler_params=_cparams(),
    )(g, w, m, v)


def adam_small(g, w, m, v, name):
    def body(g_ref, w_ref, m_ref, v_ref, d_ref, nm_ref, nv_ref):
        delta, nm, nv = _adamw(w_ref[...], g_ref[...], m_ref[...], v_ref[...])
        d_ref[...] = delta
        nm_ref[...] = nm
        nv_ref[...] = nv

    return pl.pallas_call(
        body, name=name, out_shape=[jax.ShapeDtypeStruct(w.shape, F32)] * 3, compiler_params=_cparams(),
    )(g, w, m, v)


def sum_devices(gathered, name, tr=512):
    _, r, c = gathered.shape
    tr = _row_tile(r, tr)

    def body(x_ref, o_ref):
        s = x_ref[0]
        for k in range(1, N_DEV):
            s = s + x_ref[k]
        o_ref[...] = s

    return pl.pallas_call(
        body, name=name, grid=(r // tr,), in_specs=[pl.BlockSpec((N_DEV, tr, c), lambda i: (0, i, 0))],
        out_specs=pl.BlockSpec((tr, c), lambda i: (i, 0)), out_shape=jax.ShapeDtypeStruct((r, c), F32),
        compiler_params=_cparams(),
    )(gathered)


def _pad_to(a, axis, mult):
    size = a.shape[axis]
    pad = (-size) % mult
    if pad == 0:
        return a
    cfg = [(0, 0)] * a.ndim
    cfg[axis] = (0, pad)
    return jnp.pad(a, cfg)


def _as2d(a):
    if a.ndim == 1:
        return a.reshape(1, -1)
    return a.reshape(-1, a.shape[-1])


def kernel(x, p, norm_ffn1, w1_gate, w1_up, w1_down, norm_mix, w_in, ssm_log_dt, ssm_a_re, ssm_a_im, ssm_b_re, ssm_b_im, ssm_c_re, ssm_c_im, ssm_d, ssm_w_glu, gmlp_norm_v, gmlp_w_s, gmlp_b_s, norm_ssm_out, norm_gmlp_out, w_out, norm_ffn2, w2_gate, w2_up, w2_down, norm_ple, w_ple_gate, w_ple_proj, norm_final, loss_target, m_norm_ffn1, m_w1_gate, m_w1_up, m_w1_down, m_norm_mix, m_w_in, m_ssm_log_dt, m_ssm_a_re, m_ssm_a_im, m_ssm_b_re, m_ssm_b_im, m_ssm_c_re, m_ssm_c_im, m_ssm_d, m_ssm_w_glu, m_gmlp_norm_v, m_gmlp_w_s, m_gmlp_b_s, m_norm_ssm_out, m_norm_gmlp_out, m_w_out, m_norm_ffn2, m_w2_gate, m_w2_up, m_w2_down, m_norm_ple, m_w_ple_gate, m_w_ple_proj, m_norm_final, v_norm_ffn1, v_w1_gate, v_w1_up, v_w1_down, v_norm_mix, v_w_in, v_ssm_log_dt, v_ssm_a_re, v_ssm_a_im, v_ssm_b_re, v_ssm_b_im, v_ssm_c_re, v_ssm_c_im, v_ssm_d, v_ssm_w_glu, v_gmlp_norm_v, v_gmlp_w_s, v_gmlp_b_s, v_norm_ssm_out, v_norm_gmlp_out, v_w_out, v_norm_ffn2, v_w2_gate, v_w2_up, v_w2_down, v_norm_ple, v_w_ple_gate, v_w_ple_proj, v_norm_final):
    weights = dict(
        norm_ffn1=norm_ffn1, w1_gate=w1_gate, w1_up=w1_up, w1_down=w1_down, norm_mix=norm_mix, w_in=w_in,
        ssm_log_dt=ssm_log_dt, ssm_a_re=ssm_a_re, ssm_a_im=ssm_a_im, ssm_b_re=ssm_b_re, ssm_b_im=ssm_b_im,
        ssm_c_re=ssm_c_re, ssm_c_im=ssm_c_im, ssm_d=ssm_d, ssm_w_glu=ssm_w_glu, gmlp_norm_v=gmlp_norm_v,
        gmlp_w_s=gmlp_w_s, gmlp_b_s=gmlp_b_s, norm_ssm_out=norm_ssm_out, norm_gmlp_out=norm_gmlp_out, w_out=w_out,
        norm_ffn2=norm_ffn2, w2_gate=w2_gate, w2_up=w2_up, w2_down=w2_down, norm_ple=norm_ple,
        w_ple_gate=w_ple_gate, w_ple_proj=w_ple_proj, norm_final=norm_final)
    moments_m = dict(
        norm_ffn1=m_norm_ffn1, w1_gate=m_w1_gate, w1_up=m_w1_up, w1_down=m_w1_down, norm_mix=m_norm_mix, w_in=m_w_in,
        ssm_log_dt=m_ssm_log_dt, ssm_a_re=m_ssm_a_re, ssm_a_im=m_ssm_a_im, ssm_b_re=m_ssm_b_re, ssm_b_im=m_ssm_b_im,
        ssm_c_re=m_ssm_c_re, ssm_c_im=m_ssm_c_im, ssm_d=m_ssm_d, ssm_w_glu=m_ssm_w_glu, gmlp_norm_v=m_gmlp_norm_v,
        gmlp_w_s=m_gmlp_w_s, gmlp_b_s=m_gmlp_b_s, norm_ssm_out=m_norm_ssm_out, norm_gmlp_out=m_norm_gmlp_out,
        w_out=m_w_out, norm_ffn2=m_norm_ffn2, w2_gate=m_w2_gate, w2_up=m_w2_up, w2_down=m_w2_down,
        norm_ple=m_norm_ple, w_ple_gate=m_w_ple_gate, w_ple_proj=m_w_ple_proj, norm_final=m_norm_final)
    moments_v = dict(
        norm_ffn1=v_norm_ffn1, w1_gate=v_w1_gate, w1_up=v_w1_up, w1_down=v_w1_down, norm_mix=v_norm_mix, w_in=v_w_in,
        ssm_log_dt=v_ssm_log_dt, ssm_a_re=v_ssm_a_re, ssm_a_im=v_ssm_a_im, ssm_b_re=v_ssm_b_re, ssm_b_im=v_ssm_b_im,
        ssm_c_re=v_ssm_c_re, ssm_c_im=v_ssm_c_im, ssm_d=v_ssm_d, ssm_w_glu=v_ssm_w_glu, gmlp_norm_v=v_gmlp_norm_v,
        gmlp_w_s=v_gmlp_w_s, gmlp_b_s=v_gmlp_b_s, norm_ssm_out=v_norm_ssm_out, norm_gmlp_out=v_norm_gmlp_out,
        w_out=v_w_out, norm_ffn2=v_norm_ffn2, w2_gate=v_w2_gate, w2_up=v_w2_up, w2_down=v_w2_down,
        norm_ple=v_norm_ple, w_ple_gate=v_w_ple_gate, w_ple_proj=v_w_ple_proj, norm_final=v_norm_final)
    names = list(weights)

    xs = x[0]
    ps = p[0, 0].astype(BF16)
    tgt = loss_target[0]
    d_model = xs.shape[1]
    d_ssm = d_model // 2
    n_groups = d_ssm // SSM_GROUP

    big = {
        "w1_gate": 1, "w1_up": 1, "w1_down": 0, "w_in": 1, "ssm_w_glu": 0, "w_out": 0,
        "w2_gate": 1, "w2_up": 1, "w2_down": 0, "w_ple_gate": 0, "w_ple_proj": 1}
    big_names = list(big)
    shards = [_pad_to(weights[k][0].astype(BF16), big[k], LANE) for k in big_names]
    gathered = dict(zip(big_names, all_gather(shards, [big[k] for k in big_names], "gather_weights")))
    W = gathered

    abar_r, abar_i, bbar_r, bbar_i = _ssm_discretize(ssm_log_dt[0], ssm_a_re[0], ssm_a_im[0], ssm_b_re[0], ssm_b_im[0])
    bc_r = _block_diag(jnp.swapaxes(bbar_r, 1, 2)).astype(BF16)
    bc_i = _block_diag(jnp.swapaxes(bbar_i, 1, 2)).astype(BF16)
    cc_r = _block_diag(jnp.swapaxes(ssm_c_re[0], 1, 2)).astype(BF16)
    cc_i = _block_diag(jnp.swapaxes(ssm_c_im[0], 1, 2)).astype(BF16)
    apw_f = _scan_constants(abar_r, abar_i, False)
    apw_b = _scan_constants(abar_r, abar_i, True)
    causal = jnp.tril(jnp.ones((CHUNK, CHUNK), dtype=bool))
    wm = jnp.where(causal[None], gmlp_w_s[0], 0.0).astype(BF16)
    wmt = jnp.swapaxes(wm, 1, 2)
    bs = gmlp_b_s[0][:, :, None]

    xn1 = rmsnorm_fwd(xs, norm_ffn1, "norm_ffn1")
    gate1, up1, act1 = ffn_up(xn1, W["w1_gate"], W["w1_up"], "ffn1_up")
    h1 = matmul(act1, W["w1_down"], "nn", "ffn1_down", res=xs, scale=0.5)
    xn2 = rmsnorm_fwd(h1, norm_mix, "norm_mix")
    z = matmul(xn2, W["w_in"], "nn", "proj_in")
    y_pre, yg, sr, si = s5_fwd(z, bc_r, bc_i, cc_r, cc_i, apw_f, ssm_d, "s5_fwd")
    glin = matmul(yg, W["ssm_w_glu"], "nn", "ssm_glu")
    y_gmlp = gmlp_fwd(z, gmlp_norm_v, wm, bs, "gmlp_fwd")
    ycat = mix_out_fwd(y_pre, glin, y_gmlp, norm_ssm_out, norm_gmlp_out, "mix_out")
    h2 = matmul(ycat, W["w_out"], "nn", "proj_out", res=h1)
    xn3 = rmsnorm_fwd(h2, norm_ffn2, "norm_ffn2")
    gate2, up2, act2 = ffn_up(xn3, W["w2_gate"], W["w2_up"], "ffn2_up")
    h3 = matmul(act2, W["w2_down"], "nn", "ffn2_down", res=h2, scale=0.5)
    xn4 = rmsnorm_fwd(h3, norm_ple, "norm_ple")
    pg_lin = matmul(xn4, W["w_ple_gate"], "nn", "ple_gate")
    pp = matmul(ps, W["w_ple_proj"], "nn", "ple_proj")
    h4 = ple_fwd(h3, pg_lin, pp, "ple_fwd")
    dh4, loss_part, g_norm_final = final_loss(h4, tgt, norm_final.reshape(1, -1), "final_loss")
    loss = lax.psum(loss_part[0, 0], ("x", "y", "c"))

    G = {}
    small = {}
    small["norm_final"] = g_norm_final
    dpp, dpg = ple_bwd(dh4, pg_lin, pp, "ple_bwd")
    G["w_ple_proj"] = matmul(ps, dpp, "tn", "grad_ple_proj", out_dtype=BF16)
    G["w_ple_gate"] = matmul(xn4, dpg, "tn", "grad_ple_gate", out_dtype=BF16)
    dxn4 = matmul(dpg, W["w_ple_gate"], "nt", "ple_gate_bwd")
    dh3, dh3b, small["norm_ple"] = rmsnorm_bwd(dxn4, h3, norm_ple, dh4, "norm_ple_bwd")

    def ffn_bwd(tag, dhb, xn, gate, up, act, wg, wu, wd):
        dgate, dup = ffn_bwd_act(dhb, W[wd], gate, up, tag + "_act_bwd")
        G[wd] = matmul(act, dhb, "tn", tag + "_grad_down", out_dtype=BF16, scale=0.5)
        G[wg] = matmul(xn, dgate, "tn", tag + "_grad_gate", out_dtype=BF16)
        G[wu] = matmul(xn, dup, "tn", tag + "_grad_up", out_dtype=BF16)
        dxn = matmul(dgate, W[wg], "nt", tag + "_gate_bwd")
        return matmul(dup, W[wu], "nt", tag + "_up_bwd", res=dxn)

    dxn3 = ffn_bwd("ffn2", dh3b, xn3, gate2, up2, act2, "w2_gate", "w2_up", "w2_down")
    dh2, dh2b, small["norm_ffn2"] = rmsnorm_bwd(dxn3, h2, norm_ffn2, dh3, "norm_ffn2_bwd")

    G["w_out"] = matmul(ycat, dh2b, "tn", "grad_out", out_dtype=BF16)
    dycat = matmul(dh2b, W["w_out"], "nt", "proj_out_bwd")
    dyg_direct, dglin, dy_gmlp, small["norm_ssm_out"], small["norm_gmlp_out"] = mix_out_bwd(
        dycat, y_pre, glin, y_gmlp, norm_ssm_out, norm_gmlp_out, "mix_out_bwd")
    G["ssm_w_glu"] = matmul(yg, dglin, "tn", "grad_glu", out_dtype=BF16)
    dyg = matmul(dglin, W["ssm_w_glu"], "nt", "ssm_glu_bwd", res=dyg_direct)
    du, small["ssm_d"], gc_r, gc_i, gb_r, gb_i, ga_r, ga_i = s5_bwd(
        dyg, y_pre, z, sr, si, bc_r, bc_i, cc_r, cc_i, apw_b, ssm_d, "s5_bwd")
    dzu, dzv, small["gmlp_norm_v"], g_wm, g_bs = gmlp_bwd(dy_gmlp, z, gmlp_norm_v, wm, wmt, bs, "gmlp_bwd")
    small["gmlp_w_s"] = g_wm
    small["gmlp_b_s"] = g_bs
    small["c_re"] = _block_diag_extract(gc_r, SSM_GROUP, SSM_STATE)
    small["c_im"] = _block_diag_extract(gc_i, SSM_GROUP, SSM_STATE)
    small["bbar_r"] = jnp.swapaxes(_block_diag_extract(gb_r, SSM_GROUP, SSM_STATE), 1, 2)
    small["bbar_i"] = jnp.swapaxes(_block_diag_extract(gb_i, SSM_GROUP, SSM_STATE), 1, 2)
    small["abar_r"] = jnp.sum(ga_r, axis=0).reshape(n_groups, SSM_STATE)
    small["abar_i"] = jnp.sum(ga_i, axis=0).reshape(n_groups, SSM_STATE)

    dz = jnp.concatenate([du, dzu, dzv], axis=1)
    G["w_in"] = matmul(xn2, dz, "tn", "grad_in", out_dtype=BF16)
    dxn2 = matmul(dz, W["w_in"], "nt", "proj_in_bwd")
    dh1, dh1b, small["norm_mix"] = rmsnorm_bwd(dxn2, h1, norm_mix, dh2, "norm_mix_bwd")

    dxn1 = ffn_bwd("ffn1", dh1b, xn1, gate1, up1, act1, "w1_gate", "w1_up", "w1_down")
    grad_x, _, small["norm_ffn1"] = rmsnorm_bwd(dxn1, xs, norm_ffn1, dh1, "norm_ffn1_bwd")

    core = lax.axis_index("c").astype(jnp.int32).reshape(1)
    g_list = [G[k] for k in big_names]
    ax_list = [big[k] for k in big_names]
    from_sibling = rs_to_sibling(g_list, ax_list, "rs_sibling")
    chip_parts = [rs_chip_sum(g, r, ax, core, "rs_sum_" + k) for g, r, ax, k in zip(g_list, from_sibling, ax_list, big_names)]
    reduced = dict(zip(big_names, rs_across_chips(chip_parts, "rs_chips")))

    out_g, out_d, out_m, out_v = {}, {}, {}, {}
    for k in big_names:
        w2, m2, v2 = weights[k][0], moments_m[k][0], moments_v[k][0]
        if reduced[k].shape[2] == w2.shape[1]:
            g, dl, nm, nv = adam_sharded(reduced[k], w2, m2, v2, "adam_" + k)
        else:
            g = sum_chips(reduced[k], "sum_" + k)[:, :w2.shape[1]]
            dl, nm, nv = adam_grid(g, w2, m2, v2, "adam_" + k)
        shp = weights[k].shape
        out_g[k], out_d[k], out_m[k], out_v[k] = g.reshape(shp), dl.reshape(shp), nm.reshape(shp), nv.reshape(shp)

    small_names = list(small)
    flat = jnp.concatenate([small[k].reshape(-1) for k in small_names])
    n_flat = flat.shape[0]
    flat = _pad_to(flat, 0, SUBLANE * LANE).reshape(-1, LANE)
    rows = flat.shape[0]
    (all_small,) = all_gather([flat], [0], "gather_small")
    summed = sum_devices(all_small.reshape(N_DEV, rows, LANE), "sum_small").reshape(-1)[:n_flat]
    tot = {}
    off = 0
    for k in small_names:
        sz = small[k].size
        tot[k] = summed[off:off + sz].reshape(small[k].shape)
        off += sz

    _, ssm_vjp = jax.vjp(_ssm_discretize, ssm_log_dt[0], ssm_a_re[0], ssm_a_im[0], ssm_b_re[0], ssm_b_im[0])
    g_log_dt, g_a_re, g_a_im, g_b_re, g_b_im = ssm_vjp((tot["abar_r"], tot["abar_i"], tot["bbar_r"], tot["bbar_i"]))
    small_grads = {
        "norm_ffn1": tot["norm_ffn1"], "norm_mix": tot["norm_mix"], "ssm_log_dt": g_log_dt, "ssm_a_re": g_a_re,
        "ssm_a_im": g_a_im, "ssm_b_re": g_b_re, "ssm_b_im": g_b_im, "ssm_c_re": tot["c_re"], "ssm_c_im": tot["c_im"],
        "ssm_d": tot["ssm_d"], "gmlp_norm_v": tot["gmlp_norm_v"],
        "gmlp_w_s": jnp.where(causal[None], tot["gmlp_w_s"], 0.0), "gmlp_b_s": tot["gmlp_b_s"],
        "norm_ssm_out": tot["norm_ssm_out"], "norm_gmlp_out": tot["norm_gmlp_out"], "norm_ffn2": tot["norm_ffn2"],
        "norm_ple": tot["norm_ple"], "norm_final": tot["norm_final"]}
    for k, g in small_grads.items():
        shp = weights[k].shape
        g2 = _as2d(g.reshape(shp))
        dl, nm, nv = adam_small(g2, _as2d(weights[k]), _as2d(moments_m[k]), _as2d(moments_v[k]), "adam_" + k)
        out_g[k], out_d[k], out_m[k], out_v[k] = g2.reshape(shp), dl.reshape(shp), nm.reshape(shp), nv.reshape(shp)

    return (loss, grad_x[None], *[out_g[k] for k in names], *[out_d[k] for k in names],
            *[out_m[k] for k in names], *[out_v[k] for k in names])
```

```python
import math

import jax
import jax.numpy as jnp
from jax import lax
from jax.experimental import pallas as pl
from jax.experimental.pallas import tpu as pltpu

F32 = jnp.float32
BF16 = jnp.bfloat16
MESH_DT = pl.DeviceIdType.MESH

N_DEV = 8
N_CHIP = 4
LANE = 128
SUBLANE = 8
VMEM_LIMIT = 56 * 1024 * 1024

EPS = 1e-6
SSM_GROUP = 16
SSM_STATE = 64
GROUPS_PER_BLOCK = LANE // SSM_GROUP
STATE_BLOCK = GROUPS_PER_BLOCK * SSM_STATE
GMLP_HEAD = 128
CHUNK = 128

ADAM_LR = 0.001
ADAM_B1 = 0.9
ADAM_B2 = 0.999
ADAM_EPS = 1e-08
ADAM_WD = 0.01
ADAM_STEP = 10

GELU_K = math.sqrt(2.0 / math.pi)
GELU_C = 0.044715


def _cparams():
    return pltpu.CompilerParams(vmem_limit_bytes=VMEM_LIMIT)


def _tile(n, pref):
    if n <= pref:
        return n
    t = (pref // LANE) * LANE
    while t > 0:
        if n % t == 0:
            return t
        t -= LANE
    return n


def _row_tile(n, pref):
    if n <= pref:
        return n
    t = (pref // SUBLANE) * SUBLANE
    while t > 0:
        if n % t == 0:
            return t
        t -= SUBLANE
    return n


def _gelu(x):
    t = jnp.tanh(GELU_K * (x + GELU_C * x * x * x))
    return 0.5 * x * (1.0 + t)


def _gelu_grad(x):
    t = jnp.tanh(GELU_K * (x + GELU_C * x * x * x))
    return 0.5 * (1.0 + t) + 0.5 * x * (1.0 - t * t) * (GELU_K * (1.0 + 3.0 * GELU_C * x * x))


def _sigmoid(x):
    return 1.0 / (1.0 + jnp.exp(-x))


_DN = {
    "nn": (((1,), (0,)), ((), ())),
    "nt": (((1,), (1,)), ((), ())),
    "tn": (((0,), (0,)), ((), ())),
}


def _dot(a, b, mode="nn"):
    return lax.dot_general(a, b, _DN[mode], preferred_element_type=F32)


class CommTask:
    def __init__(self, inputs, out_shape, n_sems, start, late, finish):
        self.inputs, self.out_shape, self.n_sems = list(inputs), list(out_shape), n_sems
        self.start, self.late, self.finish = start, late, finish


def _call(body, *, name, grid, in_specs, out_specs, out_shape, args, scratch_shapes=(), tasks=()):
    in_specs, out_specs, out_shape = list(in_specs), list(out_specs), list(out_shape)
    scratch_shapes = list(scratch_shapes)
    if not tasks:
        return pl.pallas_call(
            body, name=name, grid=grid, in_specs=in_specs, out_specs=out_specs, out_shape=out_shape,
            scratch_shapes=scratch_shapes, compiler_params=_cparams())(*args)
    n_in, n_out, n_scr = len(in_specs), len(out_specs), len(scratch_shapes)
    t_in = [len(t.inputs) for t in tasks]
    t_out = [len(t.out_shape) for t in tasks]
    late_step = grid[0] - max(1, grid[0] // 4)
    has_late = grid[0] >= 2

    def carried(*refs):
        pos = n_in
        task_ins = []
        for k in t_in:
            task_ins.append(refs[pos:pos + k])
            pos += k
        outs = refs[pos:pos + n_out]
        pos += n_out
        task_outs = []
        for k in t_out:
            task_outs.append(refs[pos:pos + k])
            pos += k
        scratch = refs[pos:pos + n_scr]
        pos += n_scr
        sems = [refs[pos + 3 * i:pos + 3 * i + 3] for i in range(len(tasks))]
        ids = [pl.program_id(d) for d in range(len(grid))]
        rest_zero = True
        for d in range(1, len(grid)):
            rest_zero = jnp.logical_and(rest_zero, ids[d] == 0)
        first = jnp.logical_and(ids[0] == 0, rest_zero)
        last = ids[0] == grid[0] - 1
        for d in range(1, len(grid)):
            last = jnp.logical_and(last, ids[d] == grid[d] - 1)

        @pl.when(first)
        def _():
            for t, ti, to, s in zip(tasks, task_ins, task_outs, sems):
                t.start(ti, to, *s)

        if has_late:
            @pl.when(jnp.logical_and(ids[0] == late_step, rest_zero))
            def _():
                for t, ti, to, s in zip(tasks, task_ins, task_outs, sems):
                    t.late(ti, to, *s)

        body(*refs[:n_in], *outs, *scratch)

        @pl.when(last)
        def _():
            for t, ti, to, s in zip(tasks, task_ins, task_outs, sems):
                if not has_late:
                    t.late(ti, to, *s)
                t.finish(ti, to, *s)

    any_spec = pl.BlockSpec(memory_space=pl.ANY)
    sem_shapes = [pltpu.SemaphoreType.DMA((n,)) for t in tasks for n in t.n_sems]
    res = pl.pallas_call(
        carried, name=name, grid=grid,
        in_specs=in_specs + [any_spec] * sum(t_in), out_specs=out_specs + [any_spec] * sum(t_out),
        out_shape=out_shape + [s for t in tasks for s in t.out_shape],
        scratch_shapes=scratch_shapes + sem_shapes, compiler_params=_cparams(),
    )(*args, *[a for t in tasks for a in t.inputs])
    res = list(res)
    task_res, pos = [], n_out
    for k in t_out:
        task_res.append(res[pos:pos + k])
        pos += k
    return res[:n_out], task_res


def _mm_dims(a, b, mode):
    if mode == "nn":
        (m, k), (k2, n) = a.shape, b.shape
    elif mode == "nt":
        (m, k), (n, k2) = a.shape, b.shape
    else:
        (k, m), (k2, n) = a.shape, b.shape
    assert k == k2, (a.shape, b.shape, mode)
    return m, n, k


def _mm_specs(mode, tm, tn, tk):
    if mode == "tn":
        a_spec = pl.BlockSpec((tk, tm), lambda i, j, k: (k, i))
    else:
        a_spec = pl.BlockSpec((tm, tk), lambda i, j, k: (i, k))
    if mode == "nt":
        b_spec = pl.BlockSpec((tn, tk), lambda i, j, k: (j, k))
    else:
        b_spec = pl.BlockSpec((tk, tn), lambda i, j, k: (k, j))
    return a_spec, b_spec


def _accumulate(acc, nk, partial, emit):
    if nk == 1:
        emit(partial)
        return
    kk = pl.program_id(2)

    @pl.when(kk == 0)
    def _():
        acc[...] = partial

    @pl.when(kk > 0)
    def _():
        acc[...] += partial

    @pl.when(kk == nk - 1)
    def _():
        emit(acc[...])


def matmul(a, b, mode, name, out_dtype=F32, res=None, scale=1.0, tm=1024, tn=1024, tk=2048, tasks=()):
    m, n, k = _mm_dims(a, b, mode)
    tm, tn, tk = _tile(m, tm), _tile(n, tn), _tile(k, tk)
    nk = k // tk
    a_spec, b_spec = _mm_specs(mode, tm, tn, tk)
    o_spec = pl.BlockSpec((tm, tn), lambda i, j, k: (i, j))
    has_res = res is not None

    def body(*refs):
        if has_res:
            a_ref, b_ref, r_ref, o_ref, acc = refs
        else:
            a_ref, b_ref, o_ref, acc = refs

        def emit(v):
            if scale != 1.0:
                v = v * scale
            if has_res:
                v = r_ref[...] + v
            o_ref[...] = v.astype(out_dtype)

        _accumulate(acc, nk, _dot(a_ref[...], b_ref[...], mode), emit)

    out = _call(
        body, name=name, grid=(m // tm, n // tn, nk),
        in_specs=[a_spec, b_spec] + ([o_spec] if has_res else []), out_specs=[o_spec],
        out_shape=[jax.ShapeDtypeStruct((m, n), out_dtype)], args=(a, b) + ((res,) if has_res else ()),
        scratch_shapes=[pltpu.VMEM((tm, tn) if nk > 1 else (SUBLANE, LANE), F32)], tasks=tasks)
    return (out[0][0], out[1]) if tasks else out[0]


def ffn_up(xn, wg, wu, name, tm=1024, tn=512, tk=2048, tasks=()):
    m, n, k = _mm_dims(xn, wg, "nn")
    tm, tn, tk = _tile(m, tm), _tile(n, tn), _tile(k, tk)
    nk = k // tk
    a_spec, b_spec = _mm_specs("nn", tm, tn, tk)
    o_spec = pl.BlockSpec((tm, tn), lambda i, j, k: (i, j))
    acc_shape = (2, tm, tn) if nk > 1 else (2, SUBLANE, LANE)

    def body(a_ref, g_ref, u_ref, gate_ref, up_ref, act_ref, acc):
        a = a_ref[...]

        def emit_gate(g):
            gate_ref[...] = g

        def emit_up(u):
            g = gate_ref[...]
            up_ref[...] = u
            act_ref[...] = (g * _sigmoid(g) * u).astype(BF16)

        _accumulate(acc.at[0], nk, _dot(a, g_ref[...]), emit_gate)
        _accumulate(acc.at[1], nk, _dot(a, u_ref[...]), emit_up)

    out = _call(
        body, name=name, grid=(m // tm, n // tn, nk), in_specs=[a_spec, b_spec, b_spec],
        out_specs=[o_spec, o_spec, o_spec],
        out_shape=[jax.ShapeDtypeStruct((m, n), F32), jax.ShapeDtypeStruct((m, n), F32),
                   jax.ShapeDtypeStruct((m, n), BF16)],
        args=(xn, wg, wu), scratch_shapes=[pltpu.VMEM(acc_shape, F32)], tasks=tasks)
    return (tuple(out[0]), out[1]) if tasks else tuple(out)


def ffn_bwd_act(dh, wd, gate, up, name, tm=1024, tn=512, tk=2048, tasks=()):
    m, n, k = _mm_dims(dh, wd, "nt")
    tm, tn, tk = _tile(m, tm), _tile(n, tn), _tile(k, tk)
    nk = k // tk
    a_spec, b_spec = _mm_specs("nt", tm, tn, tk)
    o_spec = pl.BlockSpec((tm, tn), lambda i, j, k: (i, j))

    def body(a_ref, b_ref, gate_ref, up_ref, dg_ref, du_ref, acc):
        def emit(total):
            dact = 0.5 * total
            g = gate_ref[...]
            sg = _sigmoid(g)
            du_ref[...] = (dact * (g * sg)).astype(BF16)
            dg_ref[...] = (dact * up_ref[...] * (sg * (1.0 + g * (1.0 - sg)))).astype(BF16)

        _accumulate(acc, nk, _dot(a_ref[...], b_ref[...], "nt"), emit)

    out = _call(
        body, name=name, grid=(m // tm, n // tn, nk), in_specs=[a_spec, b_spec, o_spec, o_spec],
        out_specs=[o_spec, o_spec],
        out_shape=[jax.ShapeDtypeStruct((m, n), BF16), jax.ShapeDtypeStruct((m, n), BF16)],
        args=(dh, wd, gate, up), scratch_shapes=[pltpu.VMEM((tm, tn) if nk > 1 else (SUBLANE, LANE), F32)],
        tasks=tasks)
    return (tuple(out[0]), out[1]) if tasks else tuple(out)


def _rows(t, d, tr):
    return pl.BlockSpec((tr, d), lambda i: (i, 0))


def _vec(d):
    return pl.BlockSpec((1, d), lambda i: (0, 0))


def rmsnorm_fwd(x, g, name, tr=512):
    t, d = x.shape
    tr = _row_tile(t, tr)

    def body(x_ref, g_ref, o_ref):
        xf = x_ref[...]
        r = lax.rsqrt(jnp.mean(xf * xf, axis=-1, keepdims=True) + EPS)
        o_ref[...] = (xf * r * g_ref[...]).astype(BF16)

    return pl.pallas_call(
        body, name=name, grid=(t // tr,), in_specs=[_rows(t, d, tr), _vec(d)], out_specs=_rows(t, d, tr),
        out_shape=jax.ShapeDtypeStruct((t, d), BF16), compiler_params=_cparams(),
    )(x, g)


def _rms_bwd(dxn, xf, g):
    r = lax.rsqrt(jnp.mean(xf * xf, axis=-1, keepdims=True) + EPS)
    xhat = xf * r
    dg = jnp.sum(dxn * xhat, axis=0, keepdims=True)
    dxh = dxn * g
    dx = r * (dxh - xhat * jnp.mean(dxh * xhat, axis=-1, keepdims=True))
    return dx, dg


def rmsnorm_bwd(dxn, x, g, dres, name, tr=256):
    t, d = x.shape
    tr = _row_tile(t, tr)

    def body(dxn_ref, x_ref, g_ref, dres_ref, o_ref, ob_ref, dg_ref):
        dx, dg = _rms_bwd(dxn_ref[...], x_ref[...], g_ref[...])
        out = dres_ref[...] + dx
        o_ref[...] = out
        ob_ref[...] = out.astype(BF16)

        @pl.when(pl.program_id(0) == 0)
        def _():
            dg_ref[...] = jnp.zeros_like(dg_ref)

        dg_ref[...] += dg

    return pl.pallas_call(
        body, name=name, grid=(t // tr,),
        in_specs=[_rows(t, d, tr), _rows(t, d, tr), _vec(d), _rows(t, d, tr)],
        out_specs=[_rows(t, d, tr), _rows(t, d, tr), _vec(d)],
        out_shape=[jax.ShapeDtypeStruct((t, d), F32), jax.ShapeDtypeStruct((t, d), BF16),
                   jax.ShapeDtypeStruct((1, d), F32)],
        compiler_params=_cparams(),
    )(dxn, x, g, dres)


def final_loss(h, target, g, name, tr=256):
    t, d = h.shape
    tr = _row_tile(t, tr)

    def body(h_ref, t_ref, g_ref, dh_ref, loss_ref, dg_ref):
        xf = h_ref[...]
        gg = g_ref[...]
        r = lax.rsqrt(jnp.mean(xf * xf, axis=-1, keepdims=True) + EPS)
        xhat = xf * r
        e = xhat * gg - t_ref[...]
        part = jnp.sum(jnp.sum(e * e, axis=1, keepdims=True), axis=0, keepdims=True) * (0.5 / d)
        dout = e * (1.0 / d)
        dg = jnp.sum(dout * xhat, axis=0, keepdims=True)
        dxh = dout * gg
        dh_ref[...] = r * (dxh - xhat * jnp.mean(dxh * xhat, axis=-1, keepdims=True))

        @pl.when(pl.program_id(0) == 0)
        def _():
            dg_ref[...] = jnp.zeros_like(dg_ref)
            loss_ref[...] = jnp.zeros_like(loss_ref)

        dg_ref[...] += dg
        loss_ref[...] += jnp.broadcast_to(part, loss_ref.shape)

    return pl.pallas_call(
        body, name=name, grid=(t // tr,),
        in_specs=[_rows(t, d, tr), _rows(t, d, tr), _vec(d)],
        out_specs=[_rows(t, d, tr), pl.BlockSpec((SUBLANE, LANE), lambda i: (0, 0)), _vec(d)],
        out_shape=[jax.ShapeDtypeStruct((t, d), F32), jax.ShapeDtypeStruct((SUBLANE, LANE), F32),
                   jax.ShapeDtypeStruct((1, d), F32)],
        compiler_params=_cparams(),
    )(h, target, g)


def ple_fwd(h, glin, pp, name, tr=512):
    t, d = h.shape
    tr = _row_tile(t, tr)

    def body(h_ref, gl_ref, pp_ref, o_ref):
        o_ref[...] = h_ref[...] + _sigmoid(gl_ref[...]) * pp_ref[...]

    sp = _rows(t, d, tr)
    return pl.pallas_call(
        body, name=name, grid=(t // tr,), in_specs=[sp, sp, sp], out_specs=sp,
        out_shape=jax.ShapeDtypeStruct((t, d), F32), compiler_params=_cparams(),
    )(h, glin, pp)


def ple_bwd(dh, glin, pp, name, tr=512):
    t, d = dh.shape
    tr = _row_tile(t, tr)

    def body(dh_ref, gl_ref, pp_ref, dpp_ref, dgl_ref):
        gate = _sigmoid(gl_ref[...])
        dh_ = dh_ref[...]
        dpp_ref[...] = (dh_ * gate).astype(BF16)
        dgl_ref[...] = (dh_ * pp_ref[...] * gate * (1.0 - gate)).astype(BF16)

    sp = _rows(t, d, tr)
    return pl.pallas_call(
        body, name=name, grid=(t // tr,), in_specs=[sp, sp, sp], out_specs=[sp, sp],
        out_shape=[jax.ShapeDtypeStruct((t, d), BF16), jax.ShapeDtypeStruct((t, d), BF16)],
        compiler_params=_cparams(),
    )(dh, glin, pp)


def mix_out_fwd(y_pre, glin, y_gmlp, g_so, g_go, name, tr=512):
    t, d = y_pre.shape
    tr = _row_tile(t, tr)

    def body(yp_ref, gl_ref, yg_ref, gs_ref, gg_ref, o_ref):
        ys = _gelu(yp_ref[...]) * _sigmoid(gl_ref[...])
        r = lax.rsqrt(jnp.mean(ys * ys, axis=-1, keepdims=True) + EPS)
        o_ref[:, 0:d] = (ys * r * gs_ref[...]).astype(BF16)
        yq = yg_ref[...]
        r2 = lax.rsqrt(jnp.mean(yq * yq, axis=-1, keepdims=True) + EPS)
        o_ref[:, d:2 * d] = (yq * r2 * gg_ref[...]).astype(BF16)

    sp = _rows(t, d, tr)
    return pl.pallas_call(
        body, name=name, grid=(t // tr,), in_specs=[sp, sp, sp, _vec(d), _vec(d)],
        out_specs=_rows(t, 2 * d, tr), out_shape=jax.ShapeDtypeStruct((t, 2 * d), BF16),
        compiler_params=_cparams(),
    )(y_pre, glin, y_gmlp, g_so, g_go)


def mix_out_bwd(dycat, y_pre, glin, y_gmlp, g_so, g_go, name, tr=256):
    t, d = y_pre.shape
    tr = _row_tile(t, tr)

    def body(dy_ref, yp_ref, gl_ref, yg_ref, gs_ref, gg_ref, dyg_ref, dl_ref, dyq_ref, dgs_ref, dgg_ref):
        yg = _gelu(yp_ref[...])
        sg = _sigmoid(gl_ref[...])
        dys, dgs = _rms_bwd(dy_ref[:, 0:d], yg * sg, gs_ref[...])
        dyg_ref[...] = dys * sg
        dl_ref[...] = (dys * yg * sg * (1.0 - sg)).astype(BF16)
        dyq, dgg = _rms_bwd(dy_ref[:, d:2 * d], yg_ref[...], gg_ref[...])
        dyq_ref[...] = dyq

        @pl.when(pl.program_id(0) == 0)
        def _():
            dgs_ref[...] = jnp.zeros_like(dgs_ref)
            dgg_ref[...] = jnp.zeros_like(dgg_ref)

        dgs_ref[...] += dgs
        dgg_ref[...] += dgg

    sp = _rows(t, d, tr)
    return pl.pallas_call(
        body, name=name, grid=(t // tr,),
        in_specs=[_rows(t, 2 * d, tr), sp, sp, sp, _vec(d), _vec(d)],
        out_specs=[sp, sp, sp, _vec(d), _vec(d)],
        out_shape=[jax.ShapeDtypeStruct((t, d), F32), jax.ShapeDtypeStruct((t, d), BF16),
                   jax.ShapeDtypeStruct((t, d), F32), jax.ShapeDtypeStruct((1, d), F32),
                   jax.ShapeDtypeStruct((1, d), F32)],
        compiler_params=_cparams(),
    )(dycat, y_pre, glin, y_gmlp, g_so, g_go)


SCAN_COLS = 512


def _scan_tile(xr, xi, const, cr, ci, reverse):
    for lvl, sh in enumerate((1, 2, 4)):
        ar, ai = const(2 * lvl), const(2 * lvl + 1)
        s = (SUBLANE - sh) if reverse else sh
        rr = pltpu.roll(xr, s, 0)
        ri = pltpu.roll(xi, s, 0)
        xr, xi = xr + ar * rr - ai * ri, xi + ar * ri + ai * rr
    pr, pi_ = const(6), const(7)
    xr, xi = xr + pr * cr - pi_ * ci, xi + pr * ci + pi_ * cr
    return xr, xi


def _bcast_row(x, row):
    return jnp.broadcast_to(x[row:row + 1, :], x.shape)


def s5_fwd(z, bc_r, bc_i, cc_r, cc_i, apw, dvec, name, tc=128, tasks=()):
    t = z.shape[0]
    nblk = bc_r.shape[0]
    d = nblk * LANE
    ns = nblk * STATE_BLOCK
    tc = _row_tile(t, tc)
    ntile = tc // SUBLANE

    def body(z_ref, br_ref, bi_ref, cr_ref, ci_ref, apw_ref, d_ref, y_ref, yg_ref, sr_ref, si_ref, carry):
        @pl.when(pl.program_id(0) == 0)
        def _():
            carry[...] = jnp.zeros_like(carry)

        for j in range(nblk):
            uj = z_ref[:, j * LANE:(j + 1) * LANE]
            ub = uj.astype(BF16)
            for q in range(STATE_BLOCK // SCAN_COLS):
                c0 = j * STATE_BLOCK + q * SCAN_COLS
                cs = pl.ds(c0, SCAN_COLS)
                bs = slice(q * SCAN_COLS, (q + 1) * SCAN_COLS)
                sr_ref[:, cs] = _dot(ub, br_ref[j, :, bs])
                si_ref[:, cs] = _dot(ub, bi_ref[j, :, bs])
                const = lambda k, cs=cs: apw_ref[k, :, cs]

                def tile(k, c, cs=cs, const=const):
                    rows = pl.ds(pl.multiple_of(k * SUBLANE, SUBLANE), SUBLANE)
                    xr, xi = _scan_tile(sr_ref[rows, cs], si_ref[rows, cs], const, c[0], c[1], False)
                    sr_ref[rows, cs] = xr
                    si_ref[rows, cs] = xi
                    return _bcast_row(xr, SUBLANE - 1), _bcast_row(xi, SUBLANE - 1)

                c_r, c_i = lax.fori_loop(0, ntile, tile, (carry[0, :, cs], carry[1, :, cs]))
                carry[0, :, cs] = c_r
                carry[1, :, cs] = c_i
            sb = pl.ds(j * STATE_BLOCK, STATE_BLOCK)
            y = (_dot(sr_ref[:, sb].astype(BF16), cr_ref[j]) - _dot(si_ref[:, sb].astype(BF16), ci_ref[j])
                 + d_ref[:, j * LANE:(j + 1) * LANE] * uj)
            y_ref[:, j * LANE:(j + 1) * LANE] = y
            yg_ref[:, j * LANE:(j + 1) * LANE] = _gelu(y).astype(BF16)

    full3 = lambda shp: pl.BlockSpec(shp, lambda i: (0, 0, 0))
    out = _call(
        body, name=name, grid=(t // tc,),
        in_specs=[pl.BlockSpec((tc, d), lambda i: (i, 0)), full3(bc_r.shape), full3(bc_i.shape),
                  full3(cc_r.shape), full3(cc_i.shape), full3(apw.shape), _vec(d)],
        out_specs=[pl.BlockSpec((tc, d), lambda i: (i, 0)), pl.BlockSpec((tc, d), lambda i: (i, 0)),
                   pl.BlockSpec((tc, ns), lambda i: (i, 0)), pl.BlockSpec((tc, ns), lambda i: (i, 0))],
        out_shape=[jax.ShapeDtypeStruct((t, d), F32), jax.ShapeDtypeStruct((t, d), BF16),
                   jax.ShapeDtypeStruct((t, ns), F32), jax.ShapeDtypeStruct((t, ns), F32)],
        args=(z, bc_r, bc_i, cc_r, cc_i, apw, dvec), scratch_shapes=[pltpu.VMEM((2, SUBLANE, ns), F32)], tasks=tasks)
    return (tuple(out[0]), out[1]) if tasks else tuple(out)


def s5_bwd(dyg, y_pre, z, sr, si, bc_r, bc_i, cc_r, cc_i, apw_rev, dvec, name, tc=128, tasks=()):
    t = z.shape[0]
    nblk = bc_r.shape[0]
    d = nblk * LANE
    ns = nblk * STATE_BLOCK
    tc = _row_tile(t, tc)
    ntile = tc // SUBLANE
    nchunk = t // tc
    tiles_per_chunk = tc // SUBLANE

    def body(dyg_ref, yp_ref, z_ref, sr_ref, si_ref, pr_ref, pi_ref, br_ref, bi_ref, cr_ref, ci_ref, apw_ref,
             d_ref, du_ref, gd_ref, gcr_ref, gci_ref, gbr_ref, gbi_ref, gar_ref, gai_ref, lr_ref, li_ref, carry):
        step = pl.program_id(0)

        @pl.when(step == 0)
        def _():
            carry[...] = jnp.zeros_like(carry)
            for ref in (gd_ref, gcr_ref, gci_ref, gbr_ref, gbi_ref, gar_ref, gai_ref):
                ref[...] = jnp.zeros_like(ref)

        first_chunk = (step == nchunk - 1).astype(F32)
        keep_prev = 1.0 - first_chunk
        row0 = lax.broadcasted_iota(jnp.int32, (SUBLANE, SCAN_COLS), 0) == 0

        for j in range(nblk):
            lanes = slice(j * LANE, (j + 1) * LANE)
            uj = z_ref[:, lanes]
            ub = uj.astype(BF16)
            gy = dyg_ref[:, lanes] * _gelu_grad(yp_ref[:, lanes])
            gyb = gy.astype(BF16)
            gd_ref[:, lanes] += jnp.sum(gy * uj, axis=0, keepdims=True)
            for q in range(STATE_BLOCK // SCAN_COLS):
                c0 = j * STATE_BLOCK + q * SCAN_COLS
                cs = pl.ds(c0, SCAN_COLS)
                bs = slice(q * SCAN_COLS, (q + 1) * SCAN_COLS)
                lr_ref[:, cs] = _dot(gyb, cr_ref[j, bs, :], "nt")
                li_ref[:, cs] = -_dot(gyb, ci_ref[j, bs, :], "nt")
                const = lambda k, cs=cs: apw_ref[k, :, cs]

                def one_tile(rows, prev_r, prev_i, c, cs=cs, const=const):
                    cr_, ci_, gar, gai = c
                    xr, xi = _scan_tile(lr_ref[rows, cs], li_ref[rows, cs], const, cr_, ci_, True)
                    lr_ref[rows, cs] = xr
                    li_ref[rows, cs] = xi
                    spr = jnp.where(row0, prev_r, pltpu.roll(sr_ref[rows, cs], 1, 0))
                    spi = jnp.where(row0, prev_i, pltpu.roll(si_ref[rows, cs], 1, 0))
                    gar = gar + xr * spr + xi * spi
                    gai = gai + xi * spr - xr * spi
                    return _bcast_row(xr, 0), _bcast_row(xi, 0), gar, gai

                def tile(k, c, cs=cs, one_tile=one_tile):
                    kk = ntile - 1 - k
                    rows = pl.ds(pl.multiple_of(kk * SUBLANE, SUBLANE), SUBLANE)
                    prow = pl.ds(pl.multiple_of((kk - 1) * SUBLANE, SUBLANE), SUBLANE)
                    prev_r = _bcast_row(sr_ref[prow, cs], SUBLANE - 1)
                    prev_i = _bcast_row(si_ref[prow, cs], SUBLANE - 1)
                    return one_tile(rows, prev_r, prev_i, c)

                zero = jnp.zeros((SUBLANE, SCAN_COLS), F32)
                c = lax.fori_loop(0, ntile - 1, tile, (carry[0, :, cs], carry[1, :, cs], zero, zero))
                prev_r = _bcast_row(pr_ref[:, cs], SUBLANE - 1) * keep_prev
                prev_i = _bcast_row(pi_ref[:, cs], SUBLANE - 1) * keep_prev
                c_r, c_i, gar, gai = one_tile(pl.ds(0, SUBLANE), prev_r, prev_i, c)
                carry[0, :, cs] = c_r
                carry[1, :, cs] = c_i
                gar_ref[:, cs] += gar
                gai_ref[:, cs] += gai
            sb = pl.ds(j * STATE_BLOCK, STATE_BLOCK)
            lrb = lr_ref[:, sb].astype(BF16)
            lib = li_ref[:, sb].astype(BF16)
            gcr_ref[j] += _dot(gyb, sr_ref[:, sb].astype(BF16), "tn")
            gci_ref[j] -= _dot(gyb, si_ref[:, sb].astype(BF16), "tn")
            gbr_ref[j] += _dot(ub, lrb, "tn")
            gbi_ref[j] += _dot(ub, lib, "tn")
            du = _dot(lrb, br_ref[j], "nt") + _dot(lib, bi_ref[j], "nt") + gy * d_ref[:, lanes]
            du_ref[:, lanes] = du.astype(BF16)

    rev = lambda i: (nchunk - 1 - i, 0)
    prev = lambda i: (jnp.maximum((nchunk - 1 - i) * tiles_per_chunk - 1, 0), 0)
    full3 = lambda shp: pl.BlockSpec(shp, lambda i: (0, 0, 0))
    acc3 = pl.BlockSpec((nblk, LANE, STATE_BLOCK), lambda i: (0, 0, 0))
    acc_rows = pl.BlockSpec((SUBLANE, ns), lambda i: (0, 0))
    out = _call(
        body, name=name, grid=(nchunk,),
        in_specs=[pl.BlockSpec((tc, d), rev), pl.BlockSpec((tc, d), rev), pl.BlockSpec((tc, d), rev),
                  pl.BlockSpec((tc, ns), rev), pl.BlockSpec((tc, ns), rev),
                  pl.BlockSpec((SUBLANE, ns), prev), pl.BlockSpec((SUBLANE, ns), prev),
                  full3(bc_r.shape), full3(bc_i.shape), full3(cc_r.shape), full3(cc_i.shape), full3(apw_rev.shape),
                  _vec(d)],
        out_specs=[pl.BlockSpec((tc, d), rev), _vec(d), acc3, acc3, acc3, acc3, acc_rows, acc_rows],
        out_shape=[jax.ShapeDtypeStruct((t, d), BF16), jax.ShapeDtypeStruct((1, d), F32)]
        + [jax.ShapeDtypeStruct((nblk, LANE, STATE_BLOCK), F32)] * 4
        + [jax.ShapeDtypeStruct((SUBLANE, ns), F32)] * 2,
        args=(dyg, y_pre, z, sr, si, sr, si, bc_r, bc_i, cc_r, cc_i, apw_rev, dvec),
        scratch_shapes=[pltpu.VMEM((tc, ns), F32), pltpu.VMEM((tc, ns), F32), pltpu.VMEM((2, SUBLANE, ns), F32)],
        tasks=tasks)
    return (tuple(out[0]), out[1]) if tasks else tuple(out)


def _cmul(a, b):
    return a[0] * b[0] - a[1] * b[1], a[0] * b[1] + a[1] * b[0]


def _scan_constants(abar_r, abar_i, reverse):
    ar = abar_r.reshape(1, -1)
    ai = abar_i.reshape(1, -1)
    if reverse:
        ai = -ai
    pw = [(ar, ai)]
    for _ in range(SUBLANE - 1):
        pw.append(_cmul(pw[-1], (ar, ai)))
    rows = lax.broadcasted_iota(jnp.int32, (SUBLANE, 1), 0)
    out = []
    for sh in (1, 2, 4):
        keep = (rows <= SUBLANE - 1 - sh) if reverse else (rows >= sh)
        for part in pw[sh - 1]:
            out.append(jnp.where(keep, part, 0.0))
    for comp in (0, 1):
        stack = jnp.concatenate([pw[k][comp] for k in range(SUBLANE)], axis=0)
        out.append(stack[::-1] if reverse else stack)
    return jnp.stack(out, axis=0).astype(F32)


def _ssm_discretize(log_dt, a_re, a_im, b_re, b_im):
    dt = jnp.exp(log_dt)[:, None]
    lr = jnp.minimum(a_re, -1e-4)
    li = a_im
    mag = jnp.exp(lr * dt)
    ang = li * dt
    abar_r = mag * jnp.cos(ang)
    abar_i = mag * jnp.sin(ang)
    den = lr * lr + li * li
    xr = abar_r - 1.0
    xi = abar_i
    zr = (xr * lr + xi * li) / den
    zi = (xi * lr - xr * li) / den
    bbar_r = zr[..., None] * b_re - zi[..., None] * b_im
    bbar_i = zr[..., None] * b_im + zi[..., None] * b_re
    return abar_r, abar_i, bbar_r, bbar_i


def _block_diag(w):
    g, a, b = w.shape
    nb = g // GROUPS_PER_BLOCK
    eye = jnp.eye(GROUPS_PER_BLOCK, dtype=w.dtype)
    w5 = w.reshape(nb, GROUPS_PER_BLOCK, a, b)
    out = w5[:, :, :, None, :] * eye[None, :, None, :, None]
    return out.reshape(nb, GROUPS_PER_BLOCK * a, GROUPS_PER_BLOCK * b)


def _block_diag_extract(m, a, b):
    nb = m.shape[0]
    eye = jnp.eye(GROUPS_PER_BLOCK, dtype=m.dtype)
    m5 = m.reshape(nb, GROUPS_PER_BLOCK, a, GROUPS_PER_BLOCK, b)
    out = jnp.sum(m5 * eye[None, :, None, :, None], axis=3)
    return out.reshape(nb * GROUPS_PER_BLOCK, a, b)


def _layer_norm(gv, nv):
    mu = jnp.mean(gv, axis=-1, keepdims=True)
    xc = gv - mu
    r = lax.rsqrt(jnp.mean(xc * xc, axis=-1, keepdims=True) + EPS)
    xhat = xc * r
    return xhat * nv, xhat, r


def gmlp_fwd(z, norm_v, wm, bs, name, tr=256):
    t = z.shape[0]
    nh = wm.shape[0]
    d = nh * GMLP_HEAD
    col0 = (z.shape[1] - 2 * d) // d
    tr = _row_tile(t, tr)

    def body(zu_ref, zv_ref, nv_ref, wm_ref, bs_ref, o_ref):
        v, _, _ = _layer_norm(_gelu(zv_ref[...]), nv_ref[...])
        vb = v.astype(BF16)
        u = _gelu(zu_ref[...])
        for c in range(tr // CHUNK):
            rows = slice(c * CHUNK, (c + 1) * CHUNK)
            for h in range(nh):
                cols = slice(h * GMLP_HEAD, (h + 1) * GMLP_HEAD)
                s = _dot(wm_ref[h], vb[rows, cols]) + bs_ref[h]
                o_ref[rows, cols] = u[rows, cols] * s

    return pl.pallas_call(
        body, name=name, grid=(t // tr,),
        in_specs=[pl.BlockSpec((tr, d), lambda i: (i, col0)), pl.BlockSpec((tr, d), lambda i: (i, col0 + 1)),
                  _vec(d), pl.BlockSpec(wm.shape, lambda i: (0, 0, 0)), pl.BlockSpec(bs.shape, lambda i: (0, 0, 0))],
        out_specs=pl.BlockSpec((tr, d), lambda i: (i, 0)),
        out_shape=jax.ShapeDtypeStruct((t, d), F32), compiler_params=_cparams(),
    )(z, z, norm_v, wm, bs)


def gmlp_bwd(dy, z, norm_v, wm, wmt, bs, name, tr=256):
    t = z.shape[0]
    nh = wm.shape[0]
    d = nh * GMLP_HEAD
    col0 = (z.shape[1] - 2 * d) // d
    tr = _row_tile(t, tr)

    def body(dy_ref, zu_ref, zv_ref, nv_ref, wm_ref, wmt_ref, bs_ref, dzu_ref, dzv_ref, dnv_ref, dwm_ref, dbs_ref,
             dv_ref):
        @pl.when(pl.program_id(0) == 0)
        def _():
            dnv_ref[...] = jnp.zeros_like(dnv_ref)
            dwm_ref[...] = jnp.zeros_like(dwm_ref)
            dbs_ref[...] = jnp.zeros_like(dbs_ref)

        zv = zv_ref[...]
        nv = nv_ref[...]
        v, xhat, r = _layer_norm(_gelu(zv), nv)
        vb = v.astype(BF16)
        zu = zu_ref[...]
        u = _gelu(zu)
        dy_ = dy_ref[...]
        for c in range(tr // CHUNK):
            rows = slice(c * CHUNK, (c + 1) * CHUNK)
            for h in range(nh):
                cols = slice(h * GMLP_HEAD, (h + 1) * GMLP_HEAD)
                vh = vb[rows, cols]
                s = _dot(wm_ref[h], vh) + bs_ref[h]
                dyh = dy_[rows, cols]
                dzu_ref[rows, cols] = (dyh * s * _gelu_grad(zu[rows, cols])).astype(BF16)
                ds = dyh * u[rows, cols]
                dsb = ds.astype(BF16)
                dbs_ref[h] += jnp.sum(ds, axis=1, keepdims=True)
                dwm_ref[h] += _dot(dsb, vh, "nt")
                dv_ref[rows, cols] = _dot(wmt_ref[h], dsb)
        dv = dv_ref[...]
        dnv_ref[...] += jnp.sum(dv * xhat, axis=0, keepdims=True)
        dxh = dv * nv
        dgv = r * (dxh - jnp.mean(dxh, axis=-1, keepdims=True) - xhat * jnp.mean(dxh * xhat, axis=-1, keepdims=True))
        dzv_ref[...] = (dgv * _gelu_grad(zv)).astype(BF16)

    full3 = lambda shp: pl.BlockSpec(shp, lambda i: (0, 0, 0))
    rows_d = pl.BlockSpec((tr, d), lambda i: (i, 0))
    return pl.pallas_call(
        body, name=name, grid=(t // tr,),
        in_specs=[rows_d, pl.BlockSpec((tr, d), lambda i: (i, col0)), pl.BlockSpec((tr, d), lambda i: (i, col0 + 1)),
                  _vec(d), full3(wm.shape), full3(wmt.shape), full3(bs.shape)],
        out_specs=[rows_d, rows_d, _vec(d), full3((nh, CHUNK, CHUNK)), full3((nh, CHUNK, 1))],
        out_shape=[jax.ShapeDtypeStruct((t, d), BF16), jax.ShapeDtypeStruct((t, d), BF16),
                   jax.ShapeDtypeStruct((1, d), F32), jax.ShapeDtypeStruct((nh, CHUNK, CHUNK), F32),
                   jax.ShapeDtypeStruct((nh, CHUNK, 1), F32)],
        scratch_shapes=[pltpu.VMEM((tr, d), F32)], compiler_params=_cparams(),
    )(dy, z, z, norm_v, wm, wmt, bs)


def _block(ref, axis, size, k):
    start = pl.multiple_of(k * size, size)
    if axis == 0:
        return ref.at[pl.ds(start, size), :]
    return ref.at[:, pl.ds(start, size)]


def _place():
    x, y, c = lax.axis_index("x"), lax.axis_index("y"), lax.axis_index("c")
    chips = [(1 - x, y), (x, 1 - y), (1 - x, 1 - y)]
    return x, y, c, chips


def _dev(x, y, c):
    return 4 * x + 2 * y + c


def gather_task(shards, axes):
    n = len(shards)
    sizes = [s.shape[ax] for s, ax in zip(shards, axes)]
    out_shape = [
        jax.ShapeDtypeStruct((s.shape[0] * N_DEV, s.shape[1]) if ax == 0 else (s.shape[0], s.shape[1] * N_DEV), s.dtype)
        for s, ax in zip(shards, axes)
    ]

    def copy(ins, outs, send_sems, recv_sems, t, k, block, to, from_input=False):
        dst = _block(outs[t], axes[t], sizes[t], _dev(*block))
        return pltpu.make_async_remote_copy(
            src_ref=ins[t] if from_input else dst, dst_ref=dst,
            send_sem=send_sems.at[t * 7 + k], recv_sem=recv_sems.at[t * 7 + k],
            device_id=to, device_id_type=MESH_DT)

    def local(ins, outs, local_sems, t, me):
        return pltpu.make_async_copy(ins[t], _block(outs[t], axes[t], sizes[t], _dev(*me)), local_sems.at[t])

    def start(ins, outs, send_sems, recv_sems, local_sems):
        x, y, c, chips = _place()
        me, sibling = (x, y, c), (x, y, 1 - c)
        for t in range(n):
            local(ins, outs, local_sems, t, me).start()
        for t in range(n):
            copy(ins, outs, send_sems, recv_sems, t, 0, me, sibling, True).start()
            for j, chip in enumerate(chips):
                copy(ins, outs, send_sems, recv_sems, t, 1 + j, me, (*chip, c), True).start()

    def late(ins, outs, send_sems, recv_sems, local_sems):
        x, y, c, chips = _place()
        me, sibling = (x, y, c), (x, y, 1 - c)
        for t in range(n):
            for j, chip in enumerate(chips):
                copy(ins, outs, send_sems, recv_sems, t, 1 + j, (*chip, c), me).wait_recv()
                copy(ins, outs, send_sems, recv_sems, t, 4 + j, (*chip, c), sibling).start()

    def finish(ins, outs, send_sems, recv_sems, local_sems):
        x, y, c, chips = _place()
        me, sibling = (x, y, c), (x, y, 1 - c)
        for t in range(n):
            copy(ins, outs, send_sems, recv_sems, t, 0, sibling, me).wait_recv()
            for j, chip in enumerate(chips):
                copy(ins, outs, send_sems, recv_sems, t, 4 + j, (*chip, 1 - c), me).wait_recv()
        for t in range(n):
            copy(ins, outs, send_sems, recv_sems, t, 0, me, sibling, True).wait_send()
            for j, chip in enumerate(chips):
                copy(ins, outs, send_sems, recv_sems, t, 1 + j, me, (*chip, c), True).wait_send()
                copy(ins, outs, send_sems, recv_sems, t, 4 + j, (*chip, c), sibling).wait_send()
            local(ins, outs, local_sems, t, me).wait()

    return CommTask(shards, out_shape, (7 * n, 7 * n, n), start, late, finish)


def _blk3(shape2, axis):
    r, c = shape2
    return (r // N_DEV, c) if axis == 0 else (r, c // N_DEV)


def _no_late(ins, outs, send_sems, recv_sems, local_sems):
    pass


def to_sibling_task(grads, axes):
    n = len(grads)
    blks = [_blk3(g.shape, ax) for g, ax in zip(grads, axes)]
    sizes = [b[ax] for b, ax in zip(blks, axes)]

    def copies(ins, outs, send_sems, recv_sems):
        x, y, c, _ = _place()
        return [pltpu.make_async_remote_copy(
            src_ref=_block(ins[t], axes[t], sizes[t], 2 * i + (1 - c)), dst_ref=outs[t].at[i],
            send_sem=send_sems.at[t * N_CHIP + i], recv_sem=recv_sems.at[t * N_CHIP + i],
            device_id=(x, y, 1 - c), device_id_type=MESH_DT) for t in range(n) for i in range(N_CHIP)]

    def start(ins, outs, send_sems, recv_sems, local_sems):
        for cp in copies(ins, outs, send_sems, recv_sems):
            cp.start()

    def finish(ins, outs, send_sems, recv_sems, local_sems):
        cps = copies(ins, outs, send_sems, recv_sems)
        for cp in cps:
            cp.wait_recv()
        for cp in cps:
            cp.wait_send()

    out_shape = [jax.ShapeDtypeStruct((N_CHIP,) + b, g.dtype) for b, g in zip(blks, grads)]
    return CommTask(grads, out_shape, (N_CHIP * n, N_CHIP * n, 1), start, _no_late, finish)


def across_chips_task(parts):
    n = len(parts)

    def copies(ins, outs, send_sems, recv_sems):
        x, y, c, chips = _place()
        my_chip = 2 * x + y
        return [pltpu.make_async_remote_copy(
            src_ref=ins[t].at[2 * chip[0] + chip[1]], dst_ref=outs[t].at[my_chip],
            send_sem=send_sems.at[t * 3 + j], recv_sem=recv_sems.at[t * 3 + j],
            device_id=(*chip, c), device_id_type=MESH_DT) for t in range(n) for j, chip in enumerate(chips)]

    def mine(ins, outs, local_sems):
        x, y, _, _ = _place()
        my_chip = 2 * x + y
        return [pltpu.make_async_copy(ins[t].at[my_chip], outs[t].at[my_chip], local_sems.at[t]) for t in range(n)]

    def start(ins, outs, send_sems, recv_sems, local_sems):
        for cp in mine(ins, outs, local_sems):
            cp.start()
        for cp in copies(ins, outs, send_sems, recv_sems):
            cp.start()

    def finish(ins, outs, send_sems, recv_sems, local_sems):
        cps = copies(ins, outs, send_sems, recv_sems)
        for cp in cps:
            cp.wait_recv()
        for cp in cps:
            cp.wait_send()
        for cp in mine(ins, outs, local_sems):
            cp.wait()

    out_shape = [jax.ShapeDtypeStruct(p.shape, p.dtype) for p in parts]
    return CommTask(parts, out_shape, (3 * n, 3 * n, n), start, _no_late, finish)


def run_tasks(tasks, name):
    t_in = [len(t.inputs) for t in tasks]
    t_out = [len(t.out_shape) for t in tasks]

    def body(*refs):
        pos, views = 0, []
        for k in t_in:
            views.append([refs[pos:pos + k]])
            pos += k
        for v, k in zip(views, t_out):
            v.append(refs[pos:pos + k])
            pos += k
        for i, v in enumerate(views):
            v.extend(refs[pos + 3 * i:pos + 3 * i + 3])
        for phase in ("start", "late", "finish"):
            for t, v in zip(tasks, views):
                getattr(t, phase)(*v)

    any_spec = pl.BlockSpec(memory_space=pl.ANY)
    res = pl.pallas_call(
        body, name=name, in_specs=[any_spec] * sum(t_in), out_specs=[any_spec] * sum(t_out),
        out_shape=[s for t in tasks for s in t.out_shape],
        scratch_shapes=[pltpu.SemaphoreType.DMA((k,)) for t in tasks for k in t.n_sems],
    )(*[a for t in tasks for a in t.inputs])
    res, out, pos = list(res), [], 0
    for k in t_out:
        out.append(res[pos:pos + k])
        pos += k
    return out


def rs_chip_sum(grad, recv, axis, core, name, tr=512):
    br, bc = _blk3(grad.shape, axis)
    tr = _row_tile(br, tr)
    nrb = br // tr

    if axis == 0:
        g_map = lambda i, r, c_ref: ((2 * i + c_ref[0]) * nrb + r, 0)
    else:
        g_map = lambda i, r, c_ref: (r, 2 * i + c_ref[0])

    def body(c_ref, g_ref, r_ref, o_ref):
        o_ref[...] = (g_ref[...].astype(F32) + r_ref[...].astype(F32)).astype(BF16)

    return pl.pallas_call(
        body, name=name,
        grid_spec=pltpu.PrefetchScalarGridSpec(
            num_scalar_prefetch=1, grid=(N_CHIP, nrb),
            in_specs=[pl.BlockSpec((tr, bc), g_map), pl.BlockSpec((None, tr, bc), lambda i, r, c_ref: (i, r, 0))],
            out_specs=pl.BlockSpec((None, tr, bc), lambda i, r, c_ref: (i, r, 0))),
        out_shape=jax.ShapeDtypeStruct((N_CHIP, br, bc), BF16), compiler_params=_cparams(),
    )(core, grad, recv)


def _adamw(w, g, m, v):
    m = ADAM_B1 * m + (1.0 - ADAM_B1) * g
    v = ADAM_B2 * v + (1.0 - ADAM_B2) * (g * g)
    m_hat = m / (1.0 - ADAM_B1 ** ADAM_STEP)
    v_hat = v / (1.0 - ADAM_B2 ** ADAM_STEP)
    delta = -ADAM_LR * (m_hat / (jnp.sqrt(v_hat) + ADAM_EPS) + ADAM_WD * w)
    return delta, m, v


def _sum_chips(p_ref):
    g = p_ref[0].astype(F32)
    for i in range(1, N_CHIP):
        g = g + p_ref[i].astype(F32)
    return g


def adam_sharded(parts, w, m, v, name, tr=256):
    r, c = w.shape
    assert parts.shape[2] == c
    tr = _row_tile(r, tr)

    def body(p_ref, w_ref, m_ref, v_ref, g_ref, d_ref, nm_ref, nv_ref):
        g = _sum_chips(p_ref)
        delta, nm, nv = _adamw(w_ref[...], g, m_ref[...], v_ref[...])
        g_ref[...] = g
        d_ref[...] = delta
        nm_ref[...] = nm
        nv_ref[...] = nv

    sp = pl.BlockSpec((tr, c), lambda i: (i, 0))
    return pl.pallas_call(
        body, name=name, grid=(r // tr,),
        in_specs=[pl.BlockSpec((N_CHIP, tr, c), lambda i: (0, i, 0)), sp, sp, sp],
        out_specs=[sp, sp, sp, sp], out_shape=[jax.ShapeDtypeStruct((r, c), F32)] * 4,
        compiler_params=_cparams(),
    )(parts, w, m, v)


def sum_chips(parts, name, tr=256):
    _, r, c = parts.shape
    tr = _row_tile(r, tr)

    def body(p_ref, g_ref):
        g_ref[...] = _sum_chips(p_ref)

    return pl.pallas_call(
        body, name=name, grid=(r // tr,), in_specs=[pl.BlockSpec((N_CHIP, tr, c), lambda i: (0, i, 0))],
        out_specs=pl.BlockSpec((tr, c), lambda i: (i, 0)), out_shape=jax.ShapeDtypeStruct((r, c), F32),
        compiler_params=_cparams(),
    )(parts)


def adam_grid(g, w, m, v, name, tr=256):
    r, c = w.shape
    tr = _row_tile(r, tr)

    def body(g_ref, w_ref, m_ref, v_ref, d_ref, nm_ref, nv_ref):
        delta, nm, nv = _adamw(w_ref[...], g_ref[...], m_ref[...], v_ref[...])
        d_ref[...] = delta
        nm_ref[...] = nm
        nv_ref[...] = nv

    sp = pl.BlockSpec((tr, c), lambda i: (i, 0))
    return pl.pallas_call(
        body, name=name, grid=(r // tr,), in_specs=[sp, sp, sp, sp], out_specs=[sp, sp, sp],
        out_shape=[jax.ShapeDtypeStruct((r, c), F32)] * 3, compiler_params=_cparams(),
    )(g, w, m, v)


def adam_small(g, w, m, v, name):
    def body(g_ref, w_ref, m_ref, v_ref, d_ref, nm_ref, nv_ref):
        delta, nm, nv = _adamw(w_ref[...], g_ref[...], m_ref[...], v_ref[...])
        d_ref[...] = delta
        nm_ref[...] = nm
        nv_ref[...] = nv

    return pl.pallas_call(
        body, name=name, out_shape=[jax.ShapeDtypeStruct(w.shape, F32)] * 3, compiler_params=_cparams(),
    )(g, w, m, v)


def sum_devices(gathered, name, tr=512):
    _, r, c = gathered.shape
    tr = _row_tile(r, tr)

    def body(x_ref, o_ref):
        s = x_ref[0]
        for k in range(1, N_DEV):
            s = s + x_ref[k]
        o_ref[...] = s

    return pl.pallas_call(
        body, name=name, grid=(r // tr,), in_specs=[pl.BlockSpec((N_DEV, tr, c), lambda i: (0, i, 0))],
        out_specs=pl.BlockSpec((tr, c), lambda i: (i, 0)), out_shape=jax.ShapeDtypeStruct((r, c), F32),
        compiler_params=_cparams(),
    )(gathered)


def _pad_to(a, axis, mult):
    size = a.shape[axis]
    pad = (-size) % mult
    if pad == 0:
        return a
    cfg = [(0, 0)] * a.ndim
    cfg[axis] = (0, pad)
    return jnp.pad(a, cfg)


def _as2d(a):
    if a.ndim == 1:
        return a.reshape(1, -1)
    return a.reshape(-1, a.shape[-1])


def kernel(x, p, norm_ffn1, w1_gate, w1_up, w1_down, norm_mix, w_in, ssm_log_dt, ssm_a_re, ssm_a_im, ssm_b_re, ssm_b_im, ssm_c_re, ssm_c_im, ssm_d, ssm_w_glu, gmlp_norm_v, gmlp_w_s, gmlp_b_s, norm_ssm_out, norm_gmlp_out, w_out, norm_ffn2, w2_gate, w2_up, w2_down, norm_ple, w_ple_gate, w_ple_proj, norm_final, loss_target, m_norm_ffn1, m_w1_gate, m_w1_up, m_w1_down, m_norm_mix, m_w_in, m_ssm_log_dt, m_ssm_a_re, m_ssm_a_im, m_ssm_b_re, m_ssm_b_im, m_ssm_c_re, m_ssm_c_im, m_ssm_d, m_ssm_w_glu, m_gmlp_norm_v, m_gmlp_w_s, m_gmlp_b_s, m_norm_ssm_out, m_norm_gmlp_out, m_w_out, m_norm_ffn2, m_w2_gate, m_w2_up, m_w2_down, m_norm_ple, m_w_ple_gate, m_w_ple_proj, m_norm_final, v_norm_ffn1, v_w1_gate, v_w1_up, v_w1_down, v_norm_mix, v_w_in, v_ssm_log_dt, v_ssm_a_re, v_ssm_a_im, v_ssm_b_re, v_ssm_b_im, v_ssm_c_re, v_ssm_c_im, v_ssm_d, v_ssm_w_glu, v_gmlp_norm_v, v_gmlp_w_s, v_gmlp_b_s, v_norm_ssm_out, v_norm_gmlp_out, v_w_out, v_norm_ffn2, v_w2_gate, v_w2_up, v_w2_down, v_norm_ple, v_w_ple_gate, v_w_ple_proj, v_norm_final):
    weights = dict(
        norm_ffn1=norm_ffn1, w1_gate=w1_gate, w1_up=w1_up, w1_down=w1_down, norm_mix=norm_mix, w_in=w_in,
        ssm_log_dt=ssm_log_dt, ssm_a_re=ssm_a_re, ssm_a_im=ssm_a_im, ssm_b_re=ssm_b_re, ssm_b_im=ssm_b_im,
        ssm_c_re=ssm_c_re, ssm_c_im=ssm_c_im, ssm_d=ssm_d, ssm_w_glu=ssm_w_glu, gmlp_norm_v=gmlp_norm_v,
        gmlp_w_s=gmlp_w_s, gmlp_b_s=gmlp_b_s, norm_ssm_out=norm_ssm_out, norm_gmlp_out=norm_gmlp_out, w_out=w_out,
        norm_ffn2=norm_ffn2, w2_gate=w2_gate, w2_up=w2_up, w2_down=w2_down, norm_ple=norm_ple,
        w_ple_gate=w_ple_gate, w_ple_proj=w_ple_proj, norm_final=norm_final)
    moments_m = dict(
        norm_ffn1=m_norm_ffn1, w1_gate=m_w1_gate, w1_up=m_w1_up, w1_down=m_w1_down, norm_mix=m_norm_mix, w_in=m_w_in,
        ssm_log_dt=m_ssm_log_dt, ssm_a_re=m_ssm_a_re, ssm_a_im=m_ssm_a_im, ssm_b_re=m_ssm_b_re, ssm_b_im=m_ssm_b_im,
        ssm_c_re=m_ssm_c_re, ssm_c_im=m_ssm_c_im, ssm_d=m_ssm_d, ssm_w_glu=m_ssm_w_glu, gmlp_norm_v=m_gmlp_norm_v,
        gmlp_w_s=m_gmlp_w_s, gmlp_b_s=m_gmlp_b_s, norm_ssm_out=m_norm_ssm_out, norm_gmlp_out=m_norm_gmlp_out,
        w_out=m_w_out, norm_ffn2=m_norm_ffn2, w2_gate=m_w2_gate, w2_up=m_w2_up, w2_down=m_w2_down,
        norm_ple=m_norm_ple, w_ple_gate=m_w_ple_gate, w_ple_proj=m_w_ple_proj, norm_final=m_norm_final)
    moments_v = dict(
        norm_ffn1=v_norm_ffn1, w1_gate=v_w1_gate, w1_up=v_w1_up, w1_down=v_w1_down, norm_mix=v_norm_mix, w_in=v_w_in,
        ssm_log_dt=v_ssm_log_dt, ssm_a_re=v_ssm_a_re, ssm_a_im=v_ssm_a_im, ssm_b_re=v_ssm_b_re, ssm_b_im=v_ssm_b_im,
        ssm_c_re=v_ssm_c_re, ssm_c_im=v_ssm_c_im, ssm_d=v_ssm_d, ssm_w_glu=v_ssm_w_glu, gmlp_norm_v=v_gmlp_norm_v,
        gmlp_w_s=v_gmlp_w_s, gmlp_b_s=v_gmlp_b_s, norm_ssm_out=v_norm_ssm_out, norm_gmlp_out=v_norm_gmlp_out,
        w_out=v_w_out, norm_ffn2=v_norm_ffn2, w2_gate=v_w2_gate, w2_up=v_w2_up, w2_down=v_w2_down,
        norm_ple=v_norm_ple, w_ple_gate=v_w_ple_gate, w_ple_proj=v_w_ple_proj, norm_final=v_norm_final)
    names = list(weights)

    xs = x[0]
    ps = p[0, 0].astype(BF16)
    tgt = loss_target[0]
    d_model = xs.shape[1]
    d_ssm = d_model // 2
    n_groups = d_ssm // SSM_GROUP

    big = {
        "w1_gate": 1, "w1_up": 1, "w1_down": 0, "w_in": 1, "ssm_w_glu": 0, "w_out": 0,
        "w2_gate": 1, "w2_up": 1, "w2_down": 0, "w_ple_gate": 0, "w_ple_proj": 1}
    big_names = list(big)
    shard = {k: _pad_to(weights[k][0].astype(BF16), big[k], LANE) for k in big_names}
    W = {}

    def gather(group):
        return gather_task([shard[k] for k in group], [big[k] for k in group])

    def gathered(group, outs):
        W.update(zip(group, outs))

    abar_r, abar_i, bbar_r, bbar_i = _ssm_discretize(ssm_log_dt[0], ssm_a_re[0], ssm_a_im[0], ssm_b_re[0], ssm_b_im[0])
    bc_r = _block_diag(jnp.swapaxes(bbar_r, 1, 2)).astype(BF16)
    bc_i = _block_diag(jnp.swapaxes(bbar_i, 1, 2)).astype(BF16)
    cc_r = _block_diag(jnp.swapaxes(ssm_c_re[0], 1, 2)).astype(BF16)
    cc_i = _block_diag(jnp.swapaxes(ssm_c_im[0], 1, 2)).astype(BF16)
    apw_f = _scan_constants(abar_r, abar_i, False)
    apw_b = _scan_constants(abar_r, abar_i, True)
    causal = jnp.tril(jnp.ones((CHUNK, CHUNK), dtype=bool))
    wm = jnp.where(causal[None], gmlp_w_s[0], 0.0).astype(BF16)
    wmt = jnp.swapaxes(wm, 1, 2)
    bs = gmlp_b_s[0][:, :, None]

    group0 = ["w1_gate", "w1_up"]
    group1 = ["w1_down", "w_in", "ssm_w_glu", "w_out"]
    group2 = ["w2_up", "w_ple_proj"]
    group3 = ["w2_gate", "w_ple_gate"]
    group4 = ["w2_down"]
    gathered(group0, run_tasks([gather(group0)], "gather_ffn1_in")[0])
    xn1 = rmsnorm_fwd(xs, norm_ffn1, "norm_ffn1")
    (gate1, up1, act1), (got,) = ffn_up(xn1, W["w1_gate"], W["w1_up"], "ffn1_up", tasks=[gather(group1)])
    gathered(group1, got)
    h1, (got,) = matmul(act1, W["w1_down"], "nn", "ffn1_down", res=xs, scale=0.5, tasks=[gather(group2)])
    gathered(group2, got)
    xn2 = rmsnorm_fwd(h1, norm_mix, "norm_mix")
    z = matmul(xn2, W["w_in"], "nn", "proj_in")
    (y_pre, yg, sr, si), (got,) = s5_fwd(z, bc_r, bc_i, cc_r, cc_i, apw_f, ssm_d, "s5_fwd", tasks=[gather(group3)])
    gathered(group3, got)
    glin = matmul(yg, W["ssm_w_glu"], "nn", "ssm_glu")
    y_gmlp = gmlp_fwd(z, gmlp_norm_v, wm, bs, "gmlp_fwd")
    ycat = mix_out_fwd(y_pre, glin, y_gmlp, norm_ssm_out, norm_gmlp_out, "mix_out")
    h2 = matmul(ycat, W["w_out"], "nn", "proj_out", res=h1)
    xn3 = rmsnorm_fwd(h2, norm_ffn2, "norm_ffn2")
    (gate2, up2, act2), (got,) = ffn_up(xn3, W["w2_gate"], W["w2_up"], "ffn2_up", tasks=[gather(group4)])
    gathered(group4, got)
    h3 = matmul(act2, W["w2_down"], "nn", "ffn2_down", res=h2, scale=0.5)
    xn4 = rmsnorm_fwd(h3, norm_ple, "norm_ple")
    pg_lin = matmul(xn4, W["w_ple_gate"], "nn", "ple_gate")
    pp = matmul(ps, W["w_ple_proj"], "nn", "ple_proj")
    h4 = ple_fwd(h3, pg_lin, pp, "ple_fwd")
    dh4, loss_part, g_norm_final = final_loss(h4, tgt, norm_final.reshape(1, -1), "final_loss")
    loss = lax.psum(loss_part[0, 0], ("x", "y", "c"))

    G = {}
    reduced = {}
    chip_part = {}
    wait_sibling, wait_chips = [], []
    core = lax.axis_index("c").astype(jnp.int32).reshape(1)

    def grad(name_, value):
        G[name_] = value
        wait_sibling.append(name_)

    def carry(fn, *a, levels="ab", **kw):
        tasks, kinds = [], []
        if "a" in levels and wait_sibling:
            group = list(wait_sibling)
            wait_sibling.clear()
            tasks.append(to_sibling_task([G[k] for k in group], [big[k] for k in group]))
            kinds.append(("a", group))
        if "b" in levels and wait_chips:
            group = list(wait_chips)
            wait_chips.clear()
            tasks.append(across_chips_task([chip_part[k] for k in group]))
            kinds.append(("b", group))
        if not tasks:
            return fn(*a, **kw)
        out, task_outs = fn(*a, tasks=tasks, **kw)
        for (kind, group), outs in zip(kinds, task_outs):
            for k, r in zip(group, outs):
                if kind == "a":
                    chip_part[k] = rs_chip_sum(G[k], r, big[k], core, "rs_sum_" + k)
                    wait_chips.append(k)
                else:
                    reduced[k] = r
        return out

    small = {}
    small["norm_final"] = g_norm_final
    dpp, dpg = ple_bwd(dh4, pg_lin, pp, "ple_bwd")
    grad("w_ple_proj", matmul(ps, dpp, "tn", "grad_ple_proj", out_dtype=BF16))
    grad("w_ple_gate", carry(matmul, xn4, dpg, "tn", "grad_ple_gate", out_dtype=BF16))
    dxn4 = carry(matmul, dpg, W["w_ple_gate"], "nt", "ple_gate_bwd")
    dh3, dh3b, small["norm_ple"] = rmsnorm_bwd(dxn4, h3, norm_ple, dh4, "norm_ple_bwd")

    def ffn_bwd(tag, dhb, xn, gate, up, act, wg, wu, wd):
        dgate, dup = carry(ffn_bwd_act, dhb, W[wd], gate, up, tag + "_act_bwd")
        grad(wd, carry(matmul, act, dhb, "tn", tag + "_grad_down", out_dtype=BF16, scale=0.5))
        grad(wg, carry(matmul, xn, dgate, "tn", tag + "_grad_gate", out_dtype=BF16))
        grad(wu, carry(matmul, xn, dup, "tn", tag + "_grad_up", out_dtype=BF16))
        dxn = carry(matmul, dgate, W[wg], "nt", tag + "_gate_bwd")
        return carry(matmul, dup, W[wu], "nt", tag + "_up_bwd", res=dxn)

    dxn3 = ffn_bwd("ffn2", dh3b, xn3, gate2, up2, act2, "w2_gate", "w2_up", "w2_down")
    dh2, dh2b, small["norm_ffn2"] = rmsnorm_bwd(dxn3, h2, norm_ffn2, dh3, "norm_ffn2_bwd")

    grad("w_out", matmul(ycat, dh2b, "tn", "grad_out", out_dtype=BF16))
    dycat = carry(matmul, dh2b, W["w_out"], "nt", "proj_out_bwd")
    dyg_direct, dglin, dy_gmlp, small["norm_ssm_out"], small["norm_gmlp_out"] = mix_out_bwd(
        dycat, y_pre, glin, y_gmlp, norm_ssm_out, norm_gmlp_out, "mix_out_bwd")
    grad("ssm_w_glu", matmul(yg, dglin, "tn", "grad_glu", out_dtype=BF16))
    dyg = carry(matmul, dglin, W["ssm_w_glu"], "nt", "ssm_glu_bwd", res=dyg_direct, levels="a")
    du, small["ssm_d"], gc_r, gc_i, gb_r, gb_i, ga_r, ga_i = carry(
        s5_bwd, dyg, y_pre, z, sr, si, bc_r, bc_i, cc_r, cc_i, apw_b, ssm_d, "s5_bwd")
    dzu, dzv, small["gmlp_norm_v"], g_wm, g_bs = gmlp_bwd(dy_gmlp, z, gmlp_norm_v, wm, wmt, bs, "gmlp_bwd")
    small["gmlp_w_s"] = g_wm
    small["gmlp_b_s"] = g_bs
    small["c_re"] = _block_diag_extract(gc_r, SSM_GROUP, SSM_STATE)
    small["c_im"] = _block_diag_extract(gc_i, SSM_GROUP, SSM_STATE)
    small["bbar_r"] = jnp.swapaxes(_block_diag_extract(gb_r, SSM_GROUP, SSM_STATE), 1, 2)
    small["bbar_i"] = jnp.swapaxes(_block_diag_extract(gb_i, SSM_GROUP, SSM_STATE), 1, 2)
    small["abar_r"] = jnp.sum(ga_r, axis=0).reshape(n_groups, SSM_STATE)
    small["abar_i"] = jnp.sum(ga_i, axis=0).reshape(n_groups, SSM_STATE)

    dz = jnp.concatenate([du, dzu, dzv], axis=1)
    grad("w_in", matmul(xn2, dz, "tn", "grad_in", out_dtype=BF16))
    dxn2 = carry(matmul, dz, W["w_in"], "nt", "proj_in_bwd")
    dh1, dh1b, small["norm_mix"] = rmsnorm_bwd(dxn2, h1, norm_mix, dh2, "norm_mix_bwd")

    dxn1 = ffn_bwd("ffn1", dh1b, xn1, gate1, up1, act1, "w1_gate", "w1_up", "w1_down")
    grad_x, _, small["norm_ffn1"] = rmsnorm_bwd(dxn1, xs, norm_ffn1, dh1, "norm_ffn1_bwd")
    assert not wait_sibling and not wait_chips and set(reduced) == set(big_names)

    out_g, out_d, out_m, out_v = {}, {}, {}, {}
    for k in big_names:
        w2, m2, v2 = weights[k][0], moments_m[k][0], moments_v[k][0]
        if reduced[k].shape[2] == w2.shape[1]:
            g, dl, nm, nv = adam_sharded(reduced[k], w2, m2, v2, "adam_" + k)
        else:
            g = sum_chips(reduced[k], "sum_" + k)[:, :w2.shape[1]]
            dl, nm, nv = adam_grid(g, w2, m2, v2, "adam_" + k)
        shp = weights[k].shape
        out_g[k], out_d[k], out_m[k], out_v[k] = g.reshape(shp), dl.reshape(shp), nm.reshape(shp), nv.reshape(shp)

    small_names = list(small)
    flat = jnp.concatenate([small[k].reshape(-1) for k in small_names])
    n_flat = flat.shape[0]
    flat = _pad_to(flat, 0, SUBLANE * LANE).reshape(-1, LANE)
    rows = flat.shape[0]
    ((all_small,),) = run_tasks([gather_task([flat], [0])], "gather_small")
    summed = sum_devices(all_small.reshape(N_DEV, rows, LANE), "sum_small").reshape(-1)[:n_flat]
    tot = {}
    off = 0
    for k in small_names:
        sz = small[k].size
        tot[k] = summed[off:off + sz].reshape(small[k].shape)
        off += sz

    _, ssm_vjp = jax.vjp(_ssm_discretize, ssm_log_dt[0], ssm_a_re[0], ssm_a_im[0], ssm_b_re[0], ssm_b_im[0])
    g_log_dt, g_a_re, g_a_im, g_b_re, g_b_im = ssm_vjp((tot["abar_r"], tot["abar_i"], tot["bbar_r"], tot["bbar_i"]))
    small_grads = {
        "norm_ffn1": tot["norm_ffn1"], "norm_mix": tot["norm_mix"], "ssm_log_dt": g_log_dt, "ssm_a_re": g_a_re,
        "ssm_a_im": g_a_im, "ssm_b_re": g_b_re, "ssm_b_im": g_b_im, "ssm_c_re": tot["c_re"], "ssm_c_im": tot["c_im"],
        "ssm_d": tot["ssm_d"], "gmlp_norm_v": tot["gmlp_norm_v"],
        "gmlp_w_s": jnp.where(causal[None], tot["gmlp_w_s"], 0.0), "gmlp_b_s": tot["gmlp_b_s"],
        "norm_ssm_out": tot["norm_ssm_out"], "norm_gmlp_out": tot["norm_gmlp_out"], "norm_ffn2": tot["norm_ffn2"],
        "norm_ple": tot["norm_ple"], "norm_final": tot["norm_final"]}
    for k, g in small_grads.items():
        shp = weights[k].shape
        g2 = _as2d(g.reshape(shp))
        dl, nm, nv = adam_small(g2, _as2d(weights[k]), _as2d(moments_m[k]), _as2d(moments_v[k]), "adam_" + k)
        out_g[k], out_d[k], out_m[k], out_v[k] = g2.reshape(shp), dl.reshape(shp), nm.reshape(shp), nv.reshape(shp)

    return (loss, grad_x[None], *[out_g[k] for k in names], *[out_d[k] for k in names],
            *[out_m[k] for k in names], *[out_v[k] for k in names])
```

```python
import math

import jax
import jax.numpy as jnp
from jax import lax
from jax.experimental import pallas as pl
from jax.experimental.pallas import tpu as pltpu

F32 = jnp.float32
BF16 = jnp.bfloat16
MESH_DT = pl.DeviceIdType.MESH

N_DEV = 8
N_CHIP = 4
LANE = 128
SUBLANE = 8
VMEM_LIMIT = 56 * 1024 * 1024

EPS = 1e-6
SSM_GROUP = 16
SSM_STATE = 64
GROUPS_PER_BLOCK = LANE // SSM_GROUP
STATE_BLOCK = GROUPS_PER_BLOCK * SSM_STATE
GMLP_HEAD = 128
CHUNK = 128

ADAM_LR = 0.001
ADAM_B1 = 0.9
ADAM_B2 = 0.999
ADAM_EPS = 1e-08
ADAM_WD = 0.01
ADAM_STEP = 10

GELU_K = math.sqrt(2.0 / math.pi)
GELU_C = 0.044715


def _cparams():
    return pltpu.CompilerParams(vmem_limit_bytes=VMEM_LIMIT)


def _tile(n, pref):
    if n <= pref:
        return n
    t = (pref // LANE) * LANE
    while t > 0:
        if n % t == 0:
            return t
        t -= LANE
    return n


def _row_tile(n, pref):
    if n <= pref:
        return n
    t = (pref // SUBLANE) * SUBLANE
    while t > 0:
        if n % t == 0:
            return t
        t -= SUBLANE
    return n


def _gelu(x):
    t = jnp.tanh(GELU_K * (x + GELU_C * x * x * x))
    return 0.5 * x * (1.0 + t)


def _gelu_grad(x):
    t = jnp.tanh(GELU_K * (x + GELU_C * x * x * x))
    return 0.5 * (1.0 + t) + 0.5 * x * (1.0 - t * t) * (GELU_K * (1.0 + 3.0 * GELU_C * x * x))


def _sigmoid(x):
    return 1.0 / (1.0 + jnp.exp(-x))


_DN = {
    "nn": (((1,), (0,)), ((), ())),
    "nt": (((1,), (1,)), ((), ())),
    "tn": (((0,), (0,)), ((), ())),
}


def _dot(a, b, mode="nn"):
    return lax.dot_general(a, b, _DN[mode], preferred_element_type=F32)


class CommTask:
    def __init__(self, inputs, out_shape, n_sems, start, late, finish):
        self.inputs, self.out_shape, self.n_sems = list(inputs), list(out_shape), n_sems
        self.start, self.late, self.finish = start, late, finish


def _call(body, *, name, grid, in_specs, out_specs, out_shape, args, scratch_shapes=(), tasks=()):
    in_specs, out_specs, out_shape = list(in_specs), list(out_specs), list(out_shape)
    scratch_shapes = list(scratch_shapes)
    if not tasks:
        return pl.pallas_call(
            body, name=name, grid=grid, in_specs=in_specs, out_specs=out_specs, out_shape=out_shape,
            scratch_shapes=scratch_shapes, compiler_params=_cparams())(*args)
    n_in, n_out, n_scr = len(in_specs), len(out_specs), len(scratch_shapes)
    t_in = [len(t.inputs) for t in tasks]
    t_out = [len(t.out_shape) for t in tasks]
    late_step = grid[0] - max(1, grid[0] // 4)
    has_late = grid[0] >= 2

    def carried(*refs):
        pos = n_in
        task_ins = []
        for k in t_in:
            task_ins.append(refs[pos:pos + k])
            pos += k
        outs = refs[pos:pos + n_out]
        pos += n_out
        task_outs = []
        for k in t_out:
            task_outs.append(refs[pos:pos + k])
            pos += k
        scratch = refs[pos:pos + n_scr]
        pos += n_scr
        sems = [refs[pos + 3 * i:pos + 3 * i + 3] for i in range(len(tasks))]
        ids = [pl.program_id(d) for d in range(len(grid))]
        rest_zero = True
        for d in range(1, len(grid)):
            rest_zero = jnp.logical_and(rest_zero, ids[d] == 0)
        first = jnp.logical_and(ids[0] == 0, rest_zero)
        last = ids[0] == grid[0] - 1
        for d in range(1, len(grid)):
            last = jnp.logical_and(last, ids[d] == grid[d] - 1)

        @pl.when(first)
        def _():
            for t, ti, to, s in zip(tasks, task_ins, task_outs, sems):
                t.start(ti, to, *s)

        if has_late:
            @pl.when(jnp.logical_and(ids[0] == late_step, rest_zero))
            def _():
                for t, ti, to, s in zip(tasks, task_ins, task_outs, sems):
                    t.late(ti, to, *s)

        body(*refs[:n_in], *outs, *scratch)

        @pl.when(last)
        def _():
            for t, ti, to, s in zip(tasks, task_ins, task_outs, sems):
                if not has_late:
                    t.late(ti, to, *s)
                t.finish(ti, to, *s)

    any_spec = pl.BlockSpec(memory_space=pl.ANY)
    sem_shapes = [pltpu.SemaphoreType.DMA((n,)) for t in tasks for n in t.n_sems]
    res = pl.pallas_call(
        carried, name=name, grid=grid,
        in_specs=in_specs + [any_spec] * sum(t_in), out_specs=out_specs + [any_spec] * sum(t_out),
        out_shape=out_shape + [s for t in tasks for s in t.out_shape],
        scratch_shapes=scratch_shapes + sem_shapes, compiler_params=_cparams(),
    )(*args, *[a for t in tasks for a in t.inputs])
    res = list(res)
    task_res, pos = [], n_out
    for k in t_out:
        task_res.append(res[pos:pos + k])
        pos += k
    return res[:n_out], task_res


def _mm_dims(a, b, mode):
    if mode == "nn":
        (m, k), (k2, n) = a.shape, b.shape
    elif mode == "nt":
        (m, k), (n, k2) = a.shape, b.shape
    else:
        (k, m), (k2, n) = a.shape, b.shape
    assert k == k2, (a.shape, b.shape, mode)
    return m, n, k


def _mm_specs(mode, tm, tn, tk):
    if mode == "tn":
        a_spec = pl.BlockSpec((tk, tm), lambda i, j, k: (k, i))
    else:
        a_spec = pl.BlockSpec((tm, tk), lambda i, j, k: (i, k))
    if mode == "nt":
        b_spec = pl.BlockSpec((tn, tk), lambda i, j, k: (j, k))
    else:
        b_spec = pl.BlockSpec((tk, tn), lambda i, j, k: (k, j))
    return a_spec, b_spec


def _accumulate(acc, nk, partial, emit):
    if nk == 1:
        emit(partial)
        return
    kk = pl.program_id(2)

    @pl.when(kk == 0)
    def _():
        acc[...] = partial

    @pl.when(kk > 0)
    def _():
        acc[...] += partial

    @pl.when(kk == nk - 1)
    def _():
        emit(acc[...])


def matmul(a, b, mode, name, out_dtype=F32, res=None, scale=1.0, tm=1024, tn=1024, tk=2048, tasks=()):
    m, n, k = _mm_dims(a, b, mode)
    tm, tn, tk = _tile(m, tm), _tile(n, tn), _tile(k, tk)
    nk = k // tk
    a_spec, b_spec = _mm_specs(mode, tm, tn, tk)
    o_spec = pl.BlockSpec((tm, tn), lambda i, j, k: (i, j))
    has_res = res is not None

    def body(*refs):
        if has_res:
            a_ref, b_ref, r_ref, o_ref, acc = refs
        else:
            a_ref, b_ref, o_ref, acc = refs

        def emit(v):
            if scale != 1.0:
                v = v * scale
            if has_res:
                v = r_ref[...] + v
            o_ref[...] = v.astype(out_dtype)

        _accumulate(acc, nk, _dot(a_ref[...], b_ref[...], mode), emit)

    out = _call(
        body, name=name, grid=(m // tm, n // tn, nk),
        in_specs=[a_spec, b_spec] + ([o_spec] if has_res else []), out_specs=[o_spec],
        out_shape=[jax.ShapeDtypeStruct((m, n), out_dtype)], args=(a, b) + ((res,) if has_res else ()),
        scratch_shapes=[pltpu.VMEM((tm, tn) if nk > 1 else (SUBLANE, LANE), F32)], tasks=tasks)
    return (out[0][0], out[1]) if tasks else out[0]


def ffn_up(xn, wg, wu, name, tm=1024, tn=512, tk=2048, tasks=()):
    m, n, k = _mm_dims(xn, wg, "nt")
    tm, tn, tk = _tile(m, tm), _tile(n, tn), _tile(k, tk)
    nk = k // tk
    a_spec, b_spec = _mm_specs("nt", tm, tn, tk)
    o_spec = pl.BlockSpec((tm, tn), lambda i, j, k: (i, j))
    acc_shape = (2, tm, tn) if nk > 1 else (2, SUBLANE, LANE)

    def body(a_ref, g_ref, u_ref, gate_ref, up_ref, act_ref, acc):
        a = a_ref[...]

        def emit_gate(g):
            gate_ref[...] = g

        def emit_up(u):
            g = gate_ref[...]
            up_ref[...] = u
            act_ref[...] = (g * _sigmoid(g) * u).astype(BF16)

        _accumulate(acc.at[0], nk, _dot(a, g_ref[...], "nt"), emit_gate)
        _accumulate(acc.at[1], nk, _dot(a, u_ref[...], "nt"), emit_up)

    out = _call(
        body, name=name, grid=(m // tm, n // tn, nk), in_specs=[a_spec, b_spec, b_spec],
        out_specs=[o_spec, o_spec, o_spec],
        out_shape=[jax.ShapeDtypeStruct((m, n), F32), jax.ShapeDtypeStruct((m, n), F32),
                   jax.ShapeDtypeStruct((m, n), BF16)],
        args=(xn, wg, wu), scratch_shapes=[pltpu.VMEM(acc_shape, F32)], tasks=tasks)
    return (tuple(out[0]), out[1]) if tasks else tuple(out)


def ffn_bwd_act(dh, wd, gate, up, name, tm=1024, tn=512, tk=2048, tasks=()):
    m, n, k = _mm_dims(dh, wd, "nt")
    tm, tn, tk = _tile(m, tm), _tile(n, tn), _tile(k, tk)
    nk = k // tk
    a_spec, b_spec = _mm_specs("nt", tm, tn, tk)
    o_spec = pl.BlockSpec((tm, tn), lambda i, j, k: (i, j))

    def body(a_ref, b_ref, gate_ref, up_ref, dg_ref, du_ref, acc):
        def emit(total):
            dact = 0.5 * total
            g = gate_ref[...]
            sg = _sigmoid(g)
            du_ref[...] = (dact * (g * sg)).astype(BF16)
            dg_ref[...] = (dact * up_ref[...] * (sg * (1.0 + g * (1.0 - sg)))).astype(BF16)

        _accumulate(acc, nk, _dot(a_ref[...], b_ref[...], "nt"), emit)

    out = _call(
        body, name=name, grid=(m // tm, n // tn, nk), in_specs=[a_spec, b_spec, o_spec, o_spec],
        out_specs=[o_spec, o_spec],
        out_shape=[jax.ShapeDtypeStruct((m, n), BF16), jax.ShapeDtypeStruct((m, n), BF16)],
        args=(dh, wd, gate, up), scratch_shapes=[pltpu.VMEM((tm, tn) if nk > 1 else (SUBLANE, LANE), F32)],
        tasks=tasks)
    return (tuple(out[0]), out[1]) if tasks else tuple(out)


def _rows(t, d, tr):
    return pl.BlockSpec((tr, d), lambda i: (i, 0))


def _vec(d):
    return pl.BlockSpec((1, d), lambda i: (0, 0))


def rmsnorm_fwd(x, g, name, tr=512):
    t, d = x.shape
    tr = _row_tile(t, tr)

    def body(x_ref, g_ref, o_ref):
        xf = x_ref[...]
        r = lax.rsqrt(jnp.mean(xf * xf, axis=-1, keepdims=True) + EPS)
        o_ref[...] = (xf * r * g_ref[...]).astype(BF16)

    return pl.pallas_call(
        body, name=name, grid=(t // tr,), in_specs=[_rows(t, d, tr), _vec(d)], out_specs=_rows(t, d, tr),
        out_shape=jax.ShapeDtypeStruct((t, d), BF16), compiler_params=_cparams(),
    )(x, g)


def _rms_bwd(dxn, xf, g):
    r = lax.rsqrt(jnp.mean(xf * xf, axis=-1, keepdims=True) + EPS)
    xhat = xf * r
    dg = jnp.sum(dxn * xhat, axis=0, keepdims=True)
    dxh = dxn * g
    dx = r * (dxh - xhat * jnp.mean(dxh * xhat, axis=-1, keepdims=True))
    return dx, dg


def rmsnorm_bwd(dxn, x, g, dres, name, tr=256):
    t, d = x.shape
    tr = _row_tile(t, tr)

    def body(dxn_ref, x_ref, g_ref, dres_ref, o_ref, ob_ref, dg_ref):
        dx, dg = _rms_bwd(dxn_ref[...], x_ref[...], g_ref[...])
        out = dres_ref[...] + dx
        o_ref[...] = out
        ob_ref[...] = out.astype(BF16)

        @pl.when(pl.program_id(0) == 0)
        def _():
            dg_ref[...] = jnp.zeros_like(dg_ref)

        dg_ref[...] += dg

    return pl.pallas_call(
        body, name=name, grid=(t // tr,),
        in_specs=[_rows(t, d, tr), _rows(t, d, tr), _vec(d), _rows(t, d, tr)],
        out_specs=[_rows(t, d, tr), _rows(t, d, tr), _vec(d)],
        out_shape=[jax.ShapeDtypeStruct((t, d), F32), jax.ShapeDtypeStruct((t, d), BF16),
                   jax.ShapeDtypeStruct((1, d), F32)],
        compiler_params=_cparams(),
    )(dxn, x, g, dres)


def final_loss(h, target, g, name, tr=256):
    t, d = h.shape
    tr = _row_tile(t, tr)

    def body(h_ref, t_ref, g_ref, dh_ref, loss_ref, dg_ref):
        xf = h_ref[...]
        gg = g_ref[...]
        r = lax.rsqrt(jnp.mean(xf * xf, axis=-1, keepdims=True) + EPS)
        xhat = xf * r
        e = xhat * gg - t_ref[...]
        part = jnp.sum(jnp.sum(e * e, axis=1, keepdims=True), axis=0, keepdims=True) * (0.5 / d)
        dout = e * (1.0 / d)
        dg = jnp.sum(dout * xhat, axis=0, keepdims=True)
        dxh = dout * gg
        dh_ref[...] = r * (dxh - xhat * jnp.mean(dxh * xhat, axis=-1, keepdims=True))

        @pl.when(pl.program_id(0) == 0)
        def _():
            dg_ref[...] = jnp.zeros_like(dg_ref)
            loss_ref[...] = jnp.zeros_like(loss_ref)

        dg_ref[...] += dg
        loss_ref[...] += jnp.broadcast_to(part, loss_ref.shape)

    return pl.pallas_call(
        body, name=name, grid=(t // tr,),
        in_specs=[_rows(t, d, tr), _rows(t, d, tr), _vec(d)],
        out_specs=[_rows(t, d, tr), pl.BlockSpec((SUBLANE, LANE), lambda i: (0, 0)), _vec(d)],
        out_shape=[jax.ShapeDtypeStruct((t, d), F32), jax.ShapeDtypeStruct((SUBLANE, LANE), F32),
                   jax.ShapeDtypeStruct((1, d), F32)],
        compiler_params=_cparams(),
    )(h, target, g)


def ple_fwd(h, glin, pp, name, tr=512):
    t, d = h.shape
    tr = _row_tile(t, tr)

    def body(h_ref, gl_ref, pp_ref, o_ref):
        o_ref[...] = h_ref[...] + _sigmoid(gl_ref[...]) * pp_ref[...]

    sp = _rows(t, d, tr)
    return pl.pallas_call(
        body, name=name, grid=(t // tr,), in_specs=[sp, sp, sp], out_specs=sp,
        out_shape=jax.ShapeDtypeStruct((t, d), F32), compiler_params=_cparams(),
    )(h, glin, pp)


def ple_bwd(dh, glin, pp, name, tr=512):
    t, d = dh.shape
    tr = _row_tile(t, tr)

    def body(dh_ref, gl_ref, pp_ref, dpp_ref, dgl_ref):
        gate = _sigmoid(gl_ref[...])
        dh_ = dh_ref[...]
        dpp_ref[...] = (dh_ * gate).astype(BF16)
        dgl_ref[...] = (dh_ * pp_ref[...] * gate * (1.0 - gate)).astype(BF16)

    sp = _rows(t, d, tr)
    return pl.pallas_call(
        body, name=name, grid=(t // tr,), in_specs=[sp, sp, sp], out_specs=[sp, sp],
        out_shape=[jax.ShapeDtypeStruct((t, d), BF16), jax.ShapeDtypeStruct((t, d), BF16)],
        compiler_params=_cparams(),
    )(dh, glin, pp)


def mix_out_fwd(y_pre, glin, y_gmlp, g_so, g_go, name, tr=512):
    t, d = y_pre.shape
    tr = _row_tile(t, tr)

    def body(yp_ref, gl_ref, yg_ref, gs_ref, gg_ref, o_ref):
        ys = _gelu(yp_ref[...]) * _sigmoid(gl_ref[...])
        r = lax.rsqrt(jnp.mean(ys * ys, axis=-1, keepdims=True) + EPS)
        o_ref[:, 0:d] = (ys * r * gs_ref[...]).astype(BF16)
        yq = yg_ref[...]
        r2 = lax.rsqrt(jnp.mean(yq * yq, axis=-1, keepdims=True) + EPS)
        o_ref[:, d:2 * d] = (yq * r2 * gg_ref[...]).astype(BF16)

    sp = _rows(t, d, tr)
    return pl.pallas_call(
        body, name=name, grid=(t // tr,), in_specs=[sp, sp, sp, _vec(d), _vec(d)],
        out_specs=_rows(t, 2 * d, tr), out_shape=jax.ShapeDtypeStruct((t, 2 * d), BF16),
        compiler_params=_cparams(),
    )(y_pre, glin, y_gmlp, g_so, g_go)


def mix_out_bwd(dycat, y_pre, glin, y_gmlp, g_so, g_go, name, tr=256):
    t, d = y_pre.shape
    tr = _row_tile(t, tr)

    def body(dy_ref, yp_ref, gl_ref, yg_ref, gs_ref, gg_ref, dyg_ref, dl_ref, dyq_ref, dgs_ref, dgg_ref):
        yg = _gelu(yp_ref[...])
        sg = _sigmoid(gl_ref[...])
        dys, dgs = _rms_bwd(dy_ref[:, 0:d], yg * sg, gs_ref[...])
        dyg_ref[...] = dys * sg
        dl_ref[...] = (dys * yg * sg * (1.0 - sg)).astype(BF16)
        dyq, dgg = _rms_bwd(dy_ref[:, d:2 * d], yg_ref[...], gg_ref[...])
        dyq_ref[...] = dyq

        @pl.when(pl.program_id(0) == 0)
        def _():
            dgs_ref[...] = jnp.zeros_like(dgs_ref)
            dgg_ref[...] = jnp.zeros_like(dgg_ref)

        dgs_ref[...] += dgs
        dgg_ref[...] += dgg

    sp = _rows(t, d, tr)
    return pl.pallas_call(
        body, name=name, grid=(t // tr,),
        in_specs=[_rows(t, 2 * d, tr), sp, sp, sp, _vec(d), _vec(d)],
        out_specs=[sp, sp, sp, _vec(d), _vec(d)],
        out_shape=[jax.ShapeDtypeStruct((t, d), F32), jax.ShapeDtypeStruct((t, d), BF16),
                   jax.ShapeDtypeStruct((t, d), F32), jax.ShapeDtypeStruct((1, d), F32),
                   jax.ShapeDtypeStruct((1, d), F32)],
        compiler_params=_cparams(),
    )(dycat, y_pre, glin, y_gmlp, g_so, g_go)


SCAN_COLS = 512


def _scan_tile(xr, xi, const, cr, ci, reverse):
    for lvl, sh in enumerate((1, 2, 4)):
        ar, ai = const(2 * lvl), const(2 * lvl + 1)
        s = (SUBLANE - sh) if reverse else sh
        rr = pltpu.roll(xr, s, 0)
        ri = pltpu.roll(xi, s, 0)
        xr, xi = xr + ar * rr - ai * ri, xi + ar * ri + ai * rr
    pr, pi_ = const(6), const(7)
    xr, xi = xr + pr * cr - pi_ * ci, xi + pr * ci + pi_ * cr
    return xr, xi


def _bcast_row(x, row):
    return jnp.broadcast_to(x[row:row + 1, :], x.shape)


def s5_fwd(z, bc_r, bc_i, cc_r, cc_i, apw, dvec, name, tc=256, tasks=()):
    t = z.shape[0]
    nblk = bc_r.shape[0]
    d = nblk * LANE
    ns = nblk * STATE_BLOCK
    tc = _row_tile(t, tc)
    ntile = tc // SUBLANE

    def body(z_ref, br_ref, bi_ref, cr_ref, ci_ref, apw_ref, d_ref, y_ref, yg_ref, sr_ref, si_ref, carry):
        @pl.when(pl.program_id(0) == 0)
        def _():
            carry[...] = jnp.zeros_like(carry)

        for j in range(nblk):
            uj = z_ref[:, j * LANE:(j + 1) * LANE]
            ub = uj.astype(BF16)
            for q in range(STATE_BLOCK // SCAN_COLS):
                c0 = j * STATE_BLOCK + q * SCAN_COLS
                cs = pl.ds(c0, SCAN_COLS)
                bs = slice(q * SCAN_COLS, (q + 1) * SCAN_COLS)
                sr_ref[:, cs] = _dot(ub, br_ref[j, :, bs])
                si_ref[:, cs] = _dot(ub, bi_ref[j, :, bs])
                const = lambda k, cs=cs: apw_ref[k, :, cs]

                def tile(k, c, cs=cs, const=const):
                    rows = pl.ds(pl.multiple_of(k * SUBLANE, SUBLANE), SUBLANE)
                    xr, xi = _scan_tile(sr_ref[rows, cs], si_ref[rows, cs], const, c[0], c[1], False)
                    sr_ref[rows, cs] = xr
                    si_ref[rows, cs] = xi
                    return _bcast_row(xr, SUBLANE - 1), _bcast_row(xi, SUBLANE - 1)

                c_r, c_i = lax.fori_loop(0, ntile, tile, (carry[0, :, cs], carry[1, :, cs]))
                carry[0, :, cs] = c_r
                carry[1, :, cs] = c_i
            sb = pl.ds(j * STATE_BLOCK, STATE_BLOCK)
            y = (_dot(sr_ref[:, sb].astype(BF16), cr_ref[j]) - _dot(si_ref[:, sb].astype(BF16), ci_ref[j])
                 + d_ref[:, j * LANE:(j + 1) * LANE] * uj)
            y_ref[:, j * LANE:(j + 1) * LANE] = y
            yg_ref[:, j * LANE:(j + 1) * LANE] = _gelu(y).astype(BF16)

    full3 = lambda shp: pl.BlockSpec(shp, lambda i: (0, 0, 0))
    out = _call(
        body, name=name, grid=(t // tc,),
        in_specs=[pl.BlockSpec((tc, d), lambda i: (i, 0)), full3(bc_r.shape), full3(bc_i.shape),
                  full3(cc_r.shape), full3(cc_i.shape), full3(apw.shape), _vec(d)],
        out_specs=[pl.BlockSpec((tc, d), lambda i: (i, 0)), pl.BlockSpec((tc, d), lambda i: (i, 0)),
                   pl.BlockSpec((tc, ns), lambda i: (i, 0)), pl.BlockSpec((tc, ns), lambda i: (i, 0))],
        out_shape=[jax.ShapeDtypeStruct((t, d), F32), jax.ShapeDtypeStruct((t, d), BF16),
                   jax.ShapeDtypeStruct((t, ns), F32), jax.ShapeDtypeStruct((t, ns), F32)],
        args=(z, bc_r, bc_i, cc_r, cc_i, apw, dvec), scratch_shapes=[pltpu.VMEM((2, SUBLANE, ns), F32)], tasks=tasks)
    return (tuple(out[0]), out[1]) if tasks else tuple(out)


def s5_bwd(dyg, y_pre, z, sr, si, bc_r, bc_i, cc_r, cc_i, apw_rev, dvec, name, tc=128, tasks=()):
    t = z.shape[0]
    nblk = bc_r.shape[0]
    d = nblk * LANE
    ns = nblk * STATE_BLOCK
    tc = _row_tile(t, tc)
    ntile = tc // SUBLANE
    nchunk = t // tc
    tiles_per_chunk = tc // SUBLANE

    def body(dyg_ref, yp_ref, z_ref, sr_ref, si_ref, pr_ref, pi_ref, br_ref, bi_ref, cr_ref, ci_ref, apw_ref,
             d_ref, du_ref, gd_ref, gcr_ref, gci_ref, gbr_ref, gbi_ref, gar_ref, gai_ref, lr_ref, li_ref, carry):
        step = pl.program_id(0)

        @pl.when(step == 0)
        def _():
            carry[...] = jnp.zeros_like(carry)
            for ref in (gd_ref, gcr_ref, gci_ref, gbr_ref, gbi_ref, gar_ref, gai_ref):
                ref[...] = jnp.zeros_like(ref)

        first_chunk = (step == nchunk - 1).astype(F32)
        keep_prev = 1.0 - first_chunk
        row0 = lax.broadcasted_iota(jnp.int32, (SUBLANE, SCAN_COLS), 0) == 0

        for j in range(nblk):
            lanes = slice(j * LANE, (j + 1) * LANE)
            uj = z_ref[:, lanes]
            ub = uj.astype(BF16)
            gy = dyg_ref[:, lanes] * _gelu_grad(yp_ref[:, lanes])
            gyb = gy.astype(BF16)
            gd_ref[:, lanes] += jnp.sum(gy * uj, axis=0, keepdims=True)
            for q in range(STATE_BLOCK // SCAN_COLS):
                c0 = j * STATE_BLOCK + q * SCAN_COLS
                cs = pl.ds(c0, SCAN_COLS)
                bs = slice(q * SCAN_COLS, (q + 1) * SCAN_COLS)
                lr_ref[:, cs] = _dot(gyb, cr_ref[j, bs, :], "nt")
                li_ref[:, cs] = -_dot(gyb, ci_ref[j, bs, :], "nt")
                const = lambda k, cs=cs: apw_ref[k, :, cs]

                def one_tile(rows, prev_r, prev_i, c, cs=cs, const=const):
                    cr_, ci_, gar, gai = c
                    xr, xi = _scan_tile(lr_ref[rows, cs], li_ref[rows, cs], const, cr_, ci_, True)
                    lr_ref[rows, cs] = xr
                    li_ref[rows, cs] = xi
                    spr = jnp.where(row0, prev_r, pltpu.roll(sr_ref[rows, cs], 1, 0))
                    spi = jnp.where(row0, prev_i, pltpu.roll(si_ref[rows, cs], 1, 0))
                    gar = gar + xr * spr + xi * spi
                    gai = gai + xi * spr - xr * spi
                    return _bcast_row(xr, 0), _bcast_row(xi, 0), gar, gai

                def tile(k, c, cs=cs, one_tile=one_tile):
                    kk = ntile - 1 - k
                    rows = pl.ds(pl.multiple_of(kk * SUBLANE, SUBLANE), SUBLANE)
                    prow = pl.ds(pl.multiple_of((kk - 1) * SUBLANE, SUBLANE), SUBLANE)
                    prev_r = _bcast_row(sr_ref[prow, cs], SUBLANE - 1)
                    prev_i = _bcast_row(si_ref[prow, cs], SUBLANE - 1)
                    return one_tile(rows, prev_r, prev_i, c)

                zero = jnp.zeros((SUBLANE, SCAN_COLS), F32)
                c = lax.fori_loop(0, ntile - 1, tile, (carry[0, :, cs], carry[1, :, cs], zero, zero))
                prev_r = _bcast_row(pr_ref[:, cs], SUBLANE - 1) * keep_prev
                prev_i = _bcast_row(pi_ref[:, cs], SUBLANE - 1) * keep_prev
                c_r, c_i, gar, gai = one_tile(pl.ds(0, SUBLANE), prev_r, prev_i, c)
                carry[0, :, cs] = c_r
                carry[1, :, cs] = c_i
                gar_ref[:, cs] += gar
                gai_ref[:, cs] += gai
            sb = pl.ds(j * STATE_BLOCK, STATE_BLOCK)
            lrb = lr_ref[:, sb].astype(BF16)
            lib = li_ref[:, sb].astype(BF16)
            gcr_ref[j] += _dot(gyb, sr_ref[:, sb].astype(BF16), "tn")
            gci_ref[j] -= _dot(gyb, si_ref[:, sb].astype(BF16), "tn")
            gbr_ref[j] += _dot(ub, lrb, "tn")
            gbi_ref[j] += _dot(ub, lib, "tn")
            du = _dot(lrb, br_ref[j], "nt") + _dot(lib, bi_ref[j], "nt") + gy * d_ref[:, lanes]
            du_ref[:, lanes] = du.astype(BF16)

    rev = lambda i: (nchunk - 1 - i, 0)
    prev = lambda i: (jnp.maximum((nchunk - 1 - i) * tiles_per_chunk - 1, 0), 0)
    full3 = lambda shp: pl.BlockSpec(shp, lambda i: (0, 0, 0))
    acc3 = pl.BlockSpec((nblk, LANE, STATE_BLOCK), lambda i: (0, 0, 0))
    acc_rows = pl.BlockSpec((SUBLANE, ns), lambda i: (0, 0))
    out = _call(
        body, name=name, grid=(nchunk,),
        in_specs=[pl.BlockSpec((tc, d), rev), pl.BlockSpec((tc, d), rev), pl.BlockSpec((tc, d), rev),
                  pl.BlockSpec((tc, ns), rev), pl.BlockSpec((tc, ns), rev),
                  pl.BlockSpec((SUBLANE, ns), prev), pl.BlockSpec((SUBLANE, ns), prev),
                  full3(bc_r.shape), full3(bc_i.shape), full3(cc_r.shape), full3(cc_i.shape), full3(apw_rev.shape),
                  _vec(d)],
        out_specs=[pl.BlockSpec((tc, d), rev), _vec(d), acc3, acc3, acc3, acc3, acc_rows, acc_rows],
        out_shape=[jax.ShapeDtypeStruct((t, d), BF16), jax.ShapeDtypeStruct((1, d), F32)]
        + [jax.ShapeDtypeStruct((nblk, LANE, STATE_BLOCK), F32)] * 4
        + [jax.ShapeDtypeStruct((SUBLANE, ns), F32)] * 2,
        args=(dyg, y_pre, z, sr, si, sr, si, bc_r, bc_i, cc_r, cc_i, apw_rev, dvec),
        scratch_shapes=[pltpu.VMEM((tc, ns), F32), pltpu.VMEM((tc, ns), F32), pltpu.VMEM((2, SUBLANE, ns), F32)],
        tasks=tasks)
    return (tuple(out[0]), out[1]) if tasks else tuple(out)


def _cmul(a, b):
    return a[0] * b[0] - a[1] * b[1], a[0] * b[1] + a[1] * b[0]


def _scan_constants(abar_r, abar_i, reverse):
    ar = abar_r.reshape(1, -1)
    ai = abar_i.reshape(1, -1)
    if reverse:
        ai = -ai
    pw = [(ar, ai)]
    for _ in range(SUBLANE - 1):
        pw.append(_cmul(pw[-1], (ar, ai)))
    rows = lax.broadcasted_iota(jnp.int32, (SUBLANE, 1), 0)
    out = []
    for sh in (1, 2, 4):
        keep = (rows <= SUBLANE - 1 - sh) if reverse else (rows >= sh)
        for part in pw[sh - 1]:
            out.append(jnp.where(keep, part, 0.0))
    for comp in (0, 1):
        stack = jnp.concatenate([pw[k][comp] for k in range(SUBLANE)], axis=0)
        out.append(stack[::-1] if reverse else stack)
    return jnp.stack(out, axis=0).astype(F32)


def _ssm_discretize(log_dt, a_re, a_im, b_re, b_im):
    dt = jnp.exp(log_dt)[:, None]
    lr = jnp.minimum(a_re, -1e-4)
    li = a_im
    mag = jnp.exp(lr * dt)
    ang = li * dt
    abar_r = mag * jnp.cos(ang)
    abar_i = mag * jnp.sin(ang)
    den = lr * lr + li * li
    xr = abar_r - 1.0
    xi = abar_i
    zr = (xr * lr + xi * li) / den
    zi = (xi * lr - xr * li) / den
    bbar_r = zr[..., None] * b_re - zi[..., None] * b_im
    bbar_i = zr[..., None] * b_im + zi[..., None] * b_re
    return abar_r, abar_i, bbar_r, bbar_i


def _block_diag(w):
    g, a, b = w.shape
    nb = g // GROUPS_PER_BLOCK
    eye = jnp.eye(GROUPS_PER_BLOCK, dtype=w.dtype)
    w5 = w.reshape(nb, GROUPS_PER_BLOCK, a, b)
    out = w5[:, :, :, None, :] * eye[None, :, None, :, None]
    return out.reshape(nb, GROUPS_PER_BLOCK * a, GROUPS_PER_BLOCK * b)


def _block_diag_extract(m, a, b):
    nb = m.shape[0]
    eye = jnp.eye(GROUPS_PER_BLOCK, dtype=m.dtype)
    m5 = m.reshape(nb, GROUPS_PER_BLOCK, a, GROUPS_PER_BLOCK, b)
    out = jnp.sum(m5 * eye[None, :, None, :, None], axis=3)
    return out.reshape(nb * GROUPS_PER_BLOCK, a, b)


def _layer_norm(gv, nv):
    mu = jnp.mean(gv, axis=-1, keepdims=True)
    xc = gv - mu
    r = lax.rsqrt(jnp.mean(xc * xc, axis=-1, keepdims=True) + EPS)
    xhat = xc * r
    return xhat * nv, xhat, r


def gmlp_fwd(z, norm_v, wm, bs, name, tr=256):
    t = z.shape[0]
    nh = wm.shape[0]
    d = nh * GMLP_HEAD
    col0 = (z.shape[1] - 2 * d) // d
    tr = _row_tile(t, tr)

    def body(zu_ref, zv_ref, nv_ref, wm_ref, bs_ref, o_ref):
        v, _, _ = _layer_norm(_gelu(zv_ref[...]), nv_ref[...])
        vb = v.astype(BF16)
        u = _gelu(zu_ref[...])
        for c in range(tr // CHUNK):
            rows = slice(c * CHUNK, (c + 1) * CHUNK)
            for h in range(nh):
                cols = slice(h * GMLP_HEAD, (h + 1) * GMLP_HEAD)
                s = _dot(wm_ref[h], vb[rows, cols]) + bs_ref[h]
                o_ref[rows, cols] = u[rows, cols] * s

    return pl.pallas_call(
        body, name=name, grid=(t // tr,),
        in_specs=[pl.BlockSpec((tr, d), lambda i: (i, col0)), pl.BlockSpec((tr, d), lambda i: (i, col0 + 1)),
                  _vec(d), pl.BlockSpec(wm.shape, lambda i: (0, 0, 0)), pl.BlockSpec(bs.shape, lambda i: (0, 0, 0))],
        out_specs=pl.BlockSpec((tr, d), lambda i: (i, 0)),
        out_shape=jax.ShapeDtypeStruct((t, d), F32), compiler_params=_cparams(),
    )(z, z, norm_v, wm, bs)


def gmlp_bwd(dy, z, norm_v, wm, wmt, bs, name, tr=256):
    t = z.shape[0]
    nh = wm.shape[0]
    d = nh * GMLP_HEAD
    col0 = (z.shape[1] - 2 * d) // d
    tr = _row_tile(t, tr)

    def body(dy_ref, zu_ref, zv_ref, nv_ref, wm_ref, wmt_ref, bs_ref, dzu_ref, dzv_ref, dnv_ref, dwm_ref, dbs_ref,
             dv_ref):
        @pl.when(pl.program_id(0) == 0)
        def _():
            dnv_ref[...] = jnp.zeros_like(dnv_ref)
            dwm_ref[...] = jnp.zeros_like(dwm_ref)
            dbs_ref[...] = jnp.zeros_like(dbs_ref)

        zv = zv_ref[...]
        nv = nv_ref[...]
        v, xhat, r = _layer_norm(_gelu(zv), nv)
        vb = v.astype(BF16)
        zu = zu_ref[...]
        u = _gelu(zu)
        dy_ = dy_ref[...]
        for c in range(tr // CHUNK):
            rows = slice(c * CHUNK, (c + 1) * CHUNK)
            for h in range(nh):
                cols = slice(h * GMLP_HEAD, (h + 1) * GMLP_HEAD)
                vh = vb[rows, cols]
                s = _dot(wm_ref[h], vh) + bs_ref[h]
                dyh = dy_[rows, cols]
                dzu_ref[rows, cols] = (dyh * s * _gelu_grad(zu[rows, cols])).astype(BF16)
                ds = dyh * u[rows, cols]
                dsb = ds.astype(BF16)
                dbs_ref[h] += jnp.sum(ds, axis=1, keepdims=True)
                dwm_ref[h] += _dot(dsb, vh, "nt")
                dv_ref[rows, cols] = _dot(wmt_ref[h], dsb)
        dv = dv_ref[...]
        dnv_ref[...] += jnp.sum(dv * xhat, axis=0, keepdims=True)
        dxh = dv * nv
        dgv = r * (dxh - jnp.mean(dxh, axis=-1, keepdims=True) - xhat * jnp.mean(dxh * xhat, axis=-1, keepdims=True))
        dzv_ref[...] = (dgv * _gelu_grad(zv)).astype(BF16)

    full3 = lambda shp: pl.BlockSpec(shp, lambda i: (0, 0, 0))
    rows_d = pl.BlockSpec((tr, d), lambda i: (i, 0))
    return pl.pallas_call(
        body, name=name, grid=(t // tr,),
        in_specs=[rows_d, pl.BlockSpec((tr, d), lambda i: (i, col0)), pl.BlockSpec((tr, d), lambda i: (i, col0 + 1)),
                  _vec(d), full3(wm.shape), full3(wmt.shape), full3(bs.shape)],
        out_specs=[rows_d, rows_d, _vec(d), full3((nh, CHUNK, CHUNK)), full3((nh, CHUNK, 1))],
        out_shape=[jax.ShapeDtypeStruct((t, d), BF16), jax.ShapeDtypeStruct((t, d), BF16),
                   jax.ShapeDtypeStruct((1, d), F32), jax.ShapeDtypeStruct((nh, CHUNK, CHUNK), F32),
                   jax.ShapeDtypeStruct((nh, CHUNK, 1), F32)],
        scratch_shapes=[pltpu.VMEM((tr, d), F32)], compiler_params=_cparams(),
    )(dy, z, z, norm_v, wm, wmt, bs)


def _block(ref, axis, size, k):
    start = pl.multiple_of(k * size, size)
    if axis == 0:
        return ref.at[pl.ds(start, size), :]
    return ref.at[:, pl.ds(start, size)]


def _place():
    x, y, c = lax.axis_index("x"), lax.axis_index("y"), lax.axis_index("c")
    chips = [(1 - x, y), (x, 1 - y), (1 - x, 1 - y)]
    return x, y, c, chips


def _dev(x, y, c):
    return 4 * x + 2 * y + c


def gather_task(shards, axes):
    n = len(shards)
    sizes = [s.shape[ax] for s, ax in zip(shards, axes)]
    out_shape = [
        jax.ShapeDtypeStruct((s.shape[0] * N_DEV, s.shape[1]) if ax == 0 else (s.shape[0], s.shape[1] * N_DEV), s.dtype)
        for s, ax in zip(shards, axes)
    ]

    def copy(ins, outs, send_sems, recv_sems, t, k, block, to, from_input=False):
        dst = _block(outs[t], axes[t], sizes[t], _dev(*block))
        return pltpu.make_async_remote_copy(
            src_ref=ins[t] if from_input else dst, dst_ref=dst,
            send_sem=send_sems.at[t * 7 + k], recv_sem=recv_sems.at[t * 7 + k],
            device_id=to, device_id_type=MESH_DT)

    def local(ins, outs, local_sems, t, me):
        return pltpu.make_async_copy(ins[t], _block(outs[t], axes[t], sizes[t], _dev(*me)), local_sems.at[t])

    def start(ins, outs, send_sems, recv_sems, local_sems):
        x, y, c, chips = _place()
        me, sibling = (x, y, c), (x, y, 1 - c)
        for t in range(n):
            local(ins, outs, local_sems, t, me).start()
        for t in range(n):
            copy(ins, outs, send_sems, recv_sems, t, 0, me, sibling, True).start()
            for j, chip in enumerate(chips):
                copy(ins, outs, send_sems, recv_sems, t, 1 + j, me, (*chip, c), True).start()

    def late(ins, outs, send_sems, recv_sems, local_sems):
        x, y, c, chips = _place()
        me, sibling = (x, y, c), (x, y, 1 - c)
        for t in range(n):
            for j, chip in enumerate(chips):
                copy(ins, outs, send_sems, recv_sems, t, 1 + j, (*chip, c), me).wait_recv()
                copy(ins, outs, send_sems, recv_sems, t, 4 + j, (*chip, c), sibling).start()

    def finish(ins, outs, send_sems, recv_sems, local_sems):
        x, y, c, chips = _place()
        me, sibling = (x, y, c), (x, y, 1 - c)
        for t in range(n):
            copy(ins, outs, send_sems, recv_sems, t, 0, sibling, me).wait_recv()
            for j, chip in enumerate(chips):
                copy(ins, outs, send_sems, recv_sems, t, 4 + j, (*chip, 1 - c), me).wait_recv()
        for t in range(n):
            copy(ins, outs, send_sems, recv_sems, t, 0, me, sibling, True).wait_send()
            for j, chip in enumerate(chips):
                copy(ins, outs, send_sems, recv_sems, t, 1 + j, me, (*chip, c), True).wait_send()
                copy(ins, outs, send_sems, recv_sems, t, 4 + j, (*chip, c), sibling).wait_send()
            local(ins, outs, local_sems, t, me).wait()

    return CommTask(shards, out_shape, (7 * n, 7 * n, n), start, late, finish)


def _blk3(shape2, axis):
    r, c = shape2
    return (r // N_DEV, c) if axis == 0 else (r, c // N_DEV)


def _no_late(ins, outs, send_sems, recv_sems, local_sems):
    pass


def to_sibling_task(grads, axes):
    n = len(grads)
    blks = [_blk3(g.shape, ax) for g, ax in zip(grads, axes)]
    sizes = [b[ax] for b, ax in zip(blks, axes)]

    def copies(ins, outs, send_sems, recv_sems):
        x, y, c, _ = _place()
        return [pltpu.make_async_remote_copy(
            src_ref=_block(ins[t], axes[t], sizes[t], 2 * i + (1 - c)), dst_ref=outs[t].at[i],
            send_sem=send_sems.at[t * N_CHIP + i], recv_sem=recv_sems.at[t * N_CHIP + i],
            device_id=(x, y, 1 - c), device_id_type=MESH_DT) for t in range(n) for i in range(N_CHIP)]

    def start(ins, outs, send_sems, recv_sems, local_sems):
        for cp in copies(ins, outs, send_sems, recv_sems):
            cp.start()

    def finish(ins, outs, send_sems, recv_sems, local_sems):
        cps = copies(ins, outs, send_sems, recv_sems)
        for cp in cps:
            cp.wait_recv()
        for cp in cps:
            cp.wait_send()

    out_shape = [jax.ShapeDtypeStruct((N_CHIP,) + b, g.dtype) for b, g in zip(blks, grads)]
    return CommTask(grads, out_shape, (N_CHIP * n, N_CHIP * n, 1), start, _no_late, finish)


def across_chips_task(parts):
    n = len(parts)

    def copies(ins, outs, send_sems, recv_sems):
        x, y, c, chips = _place()
        my_chip = 2 * x + y
        return [pltpu.make_async_remote_copy(
            src_ref=ins[t].at[2 * chip[0] + chip[1]], dst_ref=outs[t].at[my_chip],
            send_sem=send_sems.at[t * 3 + j], recv_sem=recv_sems.at[t * 3 + j],
            device_id=(*chip, c), device_id_type=MESH_DT) for t in range(n) for j, chip in enumerate(chips)]

    def mine(ins, outs, local_sems):
        x, y, _, _ = _place()
        my_chip = 2 * x + y
        return [pltpu.make_async_copy(ins[t].at[my_chip], outs[t].at[my_chip], local_sems.at[t]) for t in range(n)]

    def start(ins, outs, send_sems, recv_sems, local_sems):
        for cp in mine(ins, outs, local_sems):
            cp.start()
        for cp in copies(ins, outs, send_sems, recv_sems):
            cp.start()

    def finish(ins, outs, send_sems, recv_sems, local_sems):
        cps = copies(ins, outs, send_sems, recv_sems)
        for cp in cps:
            cp.wait_recv()
        for cp in cps:
            cp.wait_send()
        for cp in mine(ins, outs, local_sems):
            cp.wait()

    out_shape = [jax.ShapeDtypeStruct(p.shape, p.dtype) for p in parts]
    return CommTask(parts, out_shape, (3 * n, 3 * n, n), start, _no_late, finish)


def run_tasks(tasks, name):
    t_in = [len(t.inputs) for t in tasks]
    t_out = [len(t.out_shape) for t in tasks]

    def body(*refs):
        pos, views = 0, []
        for k in t_in:
            views.append([refs[pos:pos + k]])
            pos += k
        for v, k in zip(views, t_out):
            v.append(refs[pos:pos + k])
            pos += k
        for i, v in enumerate(views):
            v.extend(refs[pos + 3 * i:pos + 3 * i + 3])
        for phase in ("start", "late", "finish"):
            for t, v in zip(tasks, views):
                getattr(t, phase)(*v)

    any_spec = pl.BlockSpec(memory_space=pl.ANY)
    res = pl.pallas_call(
        body, name=name, in_specs=[any_spec] * sum(t_in), out_specs=[any_spec] * sum(t_out),
        out_shape=[s for t in tasks for s in t.out_shape],
        scratch_shapes=[pltpu.SemaphoreType.DMA((k,)) for t in tasks for k in t.n_sems],
    )(*[a for t in tasks for a in t.inputs])
    res, out, pos = list(res), [], 0
    for k in t_out:
        out.append(res[pos:pos + k])
        pos += k
    return out


def rs_chip_sum(grad, recv, axis, core, name, tr=512):
    br, bc = _blk3(grad.shape, axis)
    tr = _row_tile(br, tr)
    nrb = br // tr

    if axis == 0:
        g_map = lambda i, r, c_ref: ((2 * i + c_ref[0]) * nrb + r, 0)
    else:
        g_map = lambda i, r, c_ref: (r, 2 * i + c_ref[0])

    def body(c_ref, g_ref, r_ref, o_ref):
        o_ref[...] = (g_ref[...].astype(F32) + r_ref[...].astype(F32)).astype(BF16)

    return pl.pallas_call(
        body, name=name,
        grid_spec=pltpu.PrefetchScalarGridSpec(
            num_scalar_prefetch=1, grid=(N_CHIP, nrb),
            in_specs=[pl.BlockSpec((tr, bc), g_map), pl.BlockSpec((None, tr, bc), lambda i, r, c_ref: (i, r, 0))],
            out_specs=pl.BlockSpec((None, tr, bc), lambda i, r, c_ref: (i, r, 0))),
        out_shape=jax.ShapeDtypeStruct((N_CHIP, br, bc), BF16), compiler_params=_cparams(),
    )(core, grad, recv)


def _adamw(w, g, m, v):
    m = ADAM_B1 * m + (1.0 - ADAM_B1) * g
    v = ADAM_B2 * v + (1.0 - ADAM_B2) * (g * g)
    m_hat = m / (1.0 - ADAM_B1 ** ADAM_STEP)
    v_hat = v / (1.0 - ADAM_B2 ** ADAM_STEP)
    delta = -ADAM_LR * (m_hat / (jnp.sqrt(v_hat) + ADAM_EPS) + ADAM_WD * w)
    return delta, m, v


def _sum_chips(p_ref):
    g = p_ref[0].astype(F32)
    for i in range(1, N_CHIP):
        g = g + p_ref[i].astype(F32)
    return g


def adam_sharded(parts, w, m, v, name, tr=256):
    r, c = w.shape
    assert parts.shape[2] == c
    tr = _row_tile(r, tr)

    def body(p_ref, w_ref, m_ref, v_ref, g_ref, d_ref, nm_ref, nv_ref):
        g = _sum_chips(p_ref)
        delta, nm, nv = _adamw(w_ref[...], g, m_ref[...], v_ref[...])
        g_ref[...] = g
        d_ref[...] = delta
        nm_ref[...] = nm
        nv_ref[...] = nv

    sp = pl.BlockSpec((tr, c), lambda i: (i, 0))
    return pl.pallas_call(
        body, name=name, grid=(r // tr,),
        in_specs=[pl.BlockSpec((N_CHIP, tr, c), lambda i: (0, i, 0)), sp, sp, sp],
        out_specs=[sp, sp, sp, sp], out_shape=[jax.ShapeDtypeStruct((r, c), F32)] * 4,
        compiler_params=_cparams(),
    )(parts, w, m, v)


def adam_small(g, w, m, v, name):
    def body(g_ref, w_ref, m_ref, v_ref, d_ref, nm_ref, nv_ref):
        delta, nm, nv = _adamw(w_ref[...], g_ref[...], m_ref[...], v_ref[...])
        d_ref[...] = delta
        nm_ref[...] = nm
        nv_ref[...] = nv

    return pl.pallas_call(
        body, name=name, out_shape=[jax.ShapeDtypeStruct(w.shape, F32)] * 3, compiler_params=_cparams(),
    )(g, w, m, v)


def sum_devices(gathered, name, tr=512):
    _, r, c = gathered.shape
    tr = _row_tile(r, tr)

    def body(x_ref, o_ref):
        s = x_ref[0]
        for k in range(1, N_DEV):
            s = s + x_ref[k]
        o_ref[...] = s

    return pl.pallas_call(
        body, name=name, grid=(r // tr,), in_specs=[pl.BlockSpec((N_DEV, tr, c), lambda i: (0, i, 0))],
        out_specs=pl.BlockSpec((tr, c), lambda i: (i, 0)), out_shape=jax.ShapeDtypeStruct((r, c), F32),
        compiler_params=_cparams(),
    )(gathered)


def _pad_to(a, axis, mult):
    size = a.shape[axis]
    pad = (-size) % mult
    if pad == 0:
        return a
    cfg = [(0, 0)] * a.ndim
    cfg[axis] = (0, pad)
    return jnp.pad(a, cfg)


def _as2d(a):
    if a.ndim == 1:
        return a.reshape(1, -1)
    return a.reshape(-1, a.shape[-1])


def kernel(x, p, norm_ffn1, w1_gate, w1_up, w1_down, norm_mix, w_in, ssm_log_dt, ssm_a_re, ssm_a_im, ssm_b_re, ssm_b_im, ssm_c_re, ssm_c_im, ssm_d, ssm_w_glu, gmlp_norm_v, gmlp_w_s, gmlp_b_s, norm_ssm_out, norm_gmlp_out, w_out, norm_ffn2, w2_gate, w2_up, w2_down, norm_ple, w_ple_gate, w_ple_proj, norm_final, loss_target, m_norm_ffn1, m_w1_gate, m_w1_up, m_w1_down, m_norm_mix, m_w_in, m_ssm_log_dt, m_ssm_a_re, m_ssm_a_im, m_ssm_b_re, m_ssm_b_im, m_ssm_c_re, m_ssm_c_im, m_ssm_d, m_ssm_w_glu, m_gmlp_norm_v, m_gmlp_w_s, m_gmlp_b_s, m_norm_ssm_out, m_norm_gmlp_out, m_w_out, m_norm_ffn2, m_w2_gate, m_w2_up, m_w2_down, m_norm_ple, m_w_ple_gate, m_w_ple_proj, m_norm_final, v_norm_ffn1, v_w1_gate, v_w1_up, v_w1_down, v_norm_mix, v_w_in, v_ssm_log_dt, v_ssm_a_re, v_ssm_a_im, v_ssm_b_re, v_ssm_b_im, v_ssm_c_re, v_ssm_c_im, v_ssm_d, v_ssm_w_glu, v_gmlp_norm_v, v_gmlp_w_s, v_gmlp_b_s, v_norm_ssm_out, v_norm_gmlp_out, v_w_out, v_norm_ffn2, v_w2_gate, v_w2_up, v_w2_down, v_norm_ple, v_w_ple_gate, v_w_ple_proj, v_norm_final):
    weights = dict(
        norm_ffn1=norm_ffn1, w1_gate=w1_gate, w1_up=w1_up, w1_down=w1_down, norm_mix=norm_mix, w_in=w_in,
        ssm_log_dt=ssm_log_dt, ssm_a_re=ssm_a_re, ssm_a_im=ssm_a_im, ssm_b_re=ssm_b_re, ssm_b_im=ssm_b_im,
        ssm_c_re=ssm_c_re, ssm_c_im=ssm_c_im, ssm_d=ssm_d, ssm_w_glu=ssm_w_glu, gmlp_norm_v=gmlp_norm_v,
        gmlp_w_s=gmlp_w_s, gmlp_b_s=gmlp_b_s, norm_ssm_out=norm_ssm_out, norm_gmlp_out=norm_gmlp_out, w_out=w_out,
        norm_ffn2=norm_ffn2, w2_gate=w2_gate, w2_up=w2_up, w2_down=w2_down, norm_ple=norm_ple,
        w_ple_gate=w_ple_gate, w_ple_proj=w_ple_proj, norm_final=norm_final)
    moments_m = dict(
        norm_ffn1=m_norm_ffn1, w1_gate=m_w1_gate, w1_up=m_w1_up, w1_down=m_w1_down, norm_mix=m_norm_mix, w_in=m_w_in,
        ssm_log_dt=m_ssm_log_dt, ssm_a_re=m_ssm_a_re, ssm_a_im=m_ssm_a_im, ssm_b_re=m_ssm_b_re, ssm_b_im=m_ssm_b_im,
        ssm_c_re=m_ssm_c_re, ssm_c_im=m_ssm_c_im, ssm_d=m_ssm_d, ssm_w_glu=m_ssm_w_glu, gmlp_norm_v=m_gmlp_norm_v,
        gmlp_w_s=m_gmlp_w_s, gmlp_b_s=m_gmlp_b_s, norm_ssm_out=m_norm_ssm_out, norm_gmlp_out=m_norm_gmlp_out,
        w_out=m_w_out, norm_ffn2=m_norm_ffn2, w2_gate=m_w2_gate, w2_up=m_w2_up, w2_down=m_w2_down,
        norm_ple=m_norm_ple, w_ple_gate=m_w_ple_gate, w_ple_proj=m_w_ple_proj, norm_final=m_norm_final)
    moments_v = dict(
        norm_ffn1=v_norm_ffn1, w1_gate=v_w1_gate, w1_up=v_w1_up, w1_down=v_w1_down, norm_mix=v_norm_mix, w_in=v_w_in,
        ssm_log_dt=v_ssm_log_dt, ssm_a_re=v_ssm_a_re, ssm_a_im=v_ssm_a_im, ssm_b_re=v_ssm_b_re, ssm_b_im=v_ssm_b_im,
        ssm_c_re=v_ssm_c_re, ssm_c_im=v_ssm_c_im, ssm_d=v_ssm_d, ssm_w_glu=v_ssm_w_glu, gmlp_norm_v=v_gmlp_norm_v,
        gmlp_w_s=v_gmlp_w_s, gmlp_b_s=v_gmlp_b_s, norm_ssm_out=v_norm_ssm_out, norm_gmlp_out=v_norm_gmlp_out,
        w_out=v_w_out, norm_ffn2=v_norm_ffn2, w2_gate=v_w2_gate, w2_up=v_w2_up, w2_down=v_w2_down,
        norm_ple=v_norm_ple, w_ple_gate=v_w_ple_gate, w_ple_proj=v_w_ple_proj, norm_final=v_norm_final)
    names = list(weights)

    xs = x[0]
    ps = p[0, 0].astype(BF16)
    tgt = loss_target[0]
    d_model = xs.shape[1]
    d_ssm = d_model // 2
    n_groups = d_ssm // SSM_GROUP

    transposed = ("w1_gate", "w1_up", "w2_gate", "w2_up")
    big = {
        "w1_gate": 0, "w1_up": 0, "w1_down": 0, "w_in": 1, "ssm_w_glu": 0, "w_out": 0,
        "w2_gate": 0, "w2_up": 0, "w2_down": 0, "w_ple_gate": 0, "w_ple_proj": 1}
    big_names = list(big)

    def view(a, k):
        return a[0].T if k in transposed else a[0]

    def unview(a, k):
        return a.T[None] if k in transposed else a[None]

    shard = {k: _pad_to(view(weights[k], k).astype(BF16), big[k], LANE) for k in big_names}
    W = {}

    def gather(group):
        return gather_task([shard[k] for k in group], [big[k] for k in group])

    def gathered(group, outs):
        W.update(zip(group, outs))

    abar_r, abar_i, bbar_r, bbar_i = _ssm_discretize(ssm_log_dt[0], ssm_a_re[0], ssm_a_im[0], ssm_b_re[0], ssm_b_im[0])
    bc_r = _block_diag(jnp.swapaxes(bbar_r, 1, 2)).astype(BF16)
    bc_i = _block_diag(jnp.swapaxes(bbar_i, 1, 2)).astype(BF16)
    cc_r = _block_diag(jnp.swapaxes(ssm_c_re[0], 1, 2)).astype(BF16)
    cc_i = _block_diag(jnp.swapaxes(ssm_c_im[0], 1, 2)).astype(BF16)
    apw_f = _scan_constants(abar_r, abar_i, False)
    apw_b = _scan_constants(abar_r, abar_i, True)
    causal = jnp.tril(jnp.ones((CHUNK, CHUNK), dtype=bool))
    wm = jnp.where(causal[None], gmlp_w_s[0], 0.0).astype(BF16)
    wmt = jnp.swapaxes(wm, 1, 2)
    bs = gmlp_b_s[0][:, :, None]

    group0 = ["w1_gate", "w1_up"]
    group1 = ["w1_down", "w_in", "ssm_w_glu"]
    group2 = ["w2_up"]
    group3 = ["w_out"]
    group4 = ["w2_gate", "w_ple_proj"]
    group5 = ["w2_down", "w_ple_gate"]
    gathered(group0, run_tasks([gather(group0)], "gather_ffn1_in")[0])
    xn1 = rmsnorm_fwd(xs, norm_ffn1, "norm_ffn1")
    (gate1, up1, act1), (got,) = ffn_up(xn1, W["w1_gate"], W["w1_up"], "ffn1_up", tasks=[gather(group1)])
    gathered(group1, got)
    h1, (got,) = matmul(act1, W["w1_down"], "nn", "ffn1_down", res=xs, scale=0.5, tasks=[gather(group2)])
    gathered(group2, got)
    xn2 = rmsnorm_fwd(h1, norm_mix, "norm_mix")
    z, (got,) = matmul(xn2, W["w_in"], "nn", "proj_in", tasks=[gather(group3)])
    gathered(group3, got)
    (y_pre, yg, sr, si), (got,) = s5_fwd(z, bc_r, bc_i, cc_r, cc_i, apw_f, ssm_d, "s5_fwd", tasks=[gather(group4)])
    gathered(group4, got)
    glin = matmul(yg, W["ssm_w_glu"], "nn", "ssm_glu")
    y_gmlp = gmlp_fwd(z, gmlp_norm_v, wm, bs, "gmlp_fwd")
    ycat = mix_out_fwd(y_pre, glin, y_gmlp, norm_ssm_out, norm_gmlp_out, "mix_out")
    h2 = matmul(ycat, W["w_out"], "nn", "proj_out", res=h1)
    xn3 = rmsnorm_fwd(h2, norm_ffn2, "norm_ffn2")
    (gate2, up2, act2), (got,) = ffn_up(xn3, W["w2_gate"], W["w2_up"], "ffn2_up", tasks=[gather(group5)])
    gathered(group5, got)
    h3 = matmul(act2, W["w2_down"], "nn", "ffn2_down", res=h2, scale=0.5)
    xn4 = rmsnorm_fwd(h3, norm_ple, "norm_ple")
    pg_lin = matmul(xn4, W["w_ple_gate"], "nn", "ple_gate")
    pp = matmul(ps, W["w_ple_proj"], "nn", "ple_proj")
    h4 = ple_fwd(h3, pg_lin, pp, "ple_fwd")
    dh4, loss_part, g_norm_final = final_loss(h4, tgt, norm_final.reshape(1, -1), "final_loss")
    loss = lax.psum(loss_part[0, 0], ("x", "y", "c"))

    G = {}
    reduced = {}
    chip_part = {}
    wait_sibling, wait_chips = [], []
    core = lax.axis_index("c").astype(jnp.int32).reshape(1)

    def grad(name_, value):
        G[name_] = value
        wait_sibling.append(name_)

    def carry(fn, *a, levels="ab", extra=None, **kw):
        tasks, kinds = [], []
        if extra is not None:
            tasks.append(extra[0])
            kinds.append(("x", extra[1]))
        if "a" in levels and wait_sibling:
            group = list(wait_sibling)
            wait_sibling.clear()
            tasks.append(to_sibling_task([G[k] for k in group], [big[k] for k in group]))
            kinds.append(("a", group))
        if "b" in levels and wait_chips:
            group = list(wait_chips)
            wait_chips.clear()
            tasks.append(across_chips_task([chip_part[k] for k in group]))
            kinds.append(("b", group))
        if not tasks:
            return fn(*a, **kw)
        out, task_outs = fn(*a, tasks=tasks, **kw)
        for (kind, group), outs in zip(kinds, task_outs):
            if kind == "x":
                group(outs)
                continue
            for k, r in zip(group, outs):
                if kind == "a":
                    chip_part[k] = rs_chip_sum(G[k], r, big[k], core, "rs_sum_" + k)
                    wait_chips.append(k)
                else:
                    reduced[k] = r
        return out

    small = {}
    small["norm_final"] = g_norm_final
    dpp, dpg = ple_bwd(dh4, pg_lin, pp, "ple_bwd")
    grad("w_ple_proj", matmul(ps, dpp, "tn", "grad_ple_proj", out_dtype=BF16))
    grad("w_ple_gate", carry(matmul, xn4, dpg, "tn", "grad_ple_gate", out_dtype=BF16))
    dxn4 = carry(matmul, dpg, W["w_ple_gate"], "nt", "ple_gate_bwd")
    dh3, dh3b, small["norm_ple"] = rmsnorm_bwd(dxn4, h3, norm_ple, dh4, "norm_ple_bwd")

    def ffn_bwd(tag, dhb, xn, gate, up, act, wg, wu, wd, extra=None):
        dgate, dup = carry(ffn_bwd_act, dhb, W[wd], gate, up, tag + "_act_bwd", extra=extra)
        grad(wd, carry(matmul, act, dhb, "tn", tag + "_grad_down", out_dtype=BF16, scale=0.5))
        grad(wg, carry(matmul, dgate, xn, "tn", tag + "_grad_gate", out_dtype=BF16))
        grad(wu, carry(matmul, dup, xn, "tn", tag + "_grad_up", out_dtype=BF16))
        dxn = carry(matmul, dgate, W[wg], "nn", tag + "_gate_bwd")
        return carry(matmul, dup, W[wu], "nn", tag + "_up_bwd", res=dxn)

    dxn3 = ffn_bwd("ffn2", dh3b, xn3, gate2, up2, act2, "w2_gate", "w2_up", "w2_down")
    dh2, dh2b, small["norm_ffn2"] = rmsnorm_bwd(dxn3, h2, norm_ffn2, dh3, "norm_ffn2_bwd")

    grad("w_out", matmul(ycat, dh2b, "tn", "grad_out", out_dtype=BF16))
    dycat = carry(matmul, dh2b, W["w_out"], "nt", "proj_out_bwd")
    dyg_direct, dglin, dy_gmlp, small["norm_ssm_out"], small["norm_gmlp_out"] = mix_out_bwd(
        dycat, y_pre, glin, y_gmlp, norm_ssm_out, norm_gmlp_out, "mix_out_bwd")
    grad("ssm_w_glu", matmul(yg, dglin, "tn", "grad_glu", out_dtype=BF16))
    dyg = carry(matmul, dglin, W["ssm_w_glu"], "nt", "ssm_glu_bwd", res=dyg_direct, levels="a")
    du, small["ssm_d"], gc_r, gc_i, gb_r, gb_i, ga_r, ga_i = carry(
        s5_bwd, dyg, y_pre, z, sr, si, bc_r, bc_i, cc_r, cc_i, apw_b, ssm_d, "s5_bwd")
    dzu, dzv, small["gmlp_norm_v"], g_wm, g_bs = gmlp_bwd(dy_gmlp, z, gmlp_norm_v, wm, wmt, bs, "gmlp_bwd")
    small["gmlp_w_s"] = g_wm
    small["gmlp_b_s"] = g_bs
    small["c_re"] = _block_diag_extract(gc_r, SSM_GROUP, SSM_STATE)
    small["c_im"] = _block_diag_extract(gc_i, SSM_GROUP, SSM_STATE)
    small["bbar_r"] = jnp.swapaxes(_block_diag_extract(gb_r, SSM_GROUP, SSM_STATE), 1, 2)
    small["bbar_i"] = jnp.swapaxes(_block_diag_extract(gb_i, SSM_GROUP, SSM_STATE), 1, 2)
    small["abar_r"] = jnp.sum(ga_r, axis=0).reshape(n_groups, SSM_STATE)
    small["abar_i"] = jnp.sum(ga_i, axis=0).reshape(n_groups, SSM_STATE)

    dz = jnp.concatenate([du, dzu, dzv], axis=1)
    grad("w_in", matmul(xn2, dz, "tn", "grad_in", out_dtype=BF16))
    dxn2 = carry(matmul, dz, W["w_in"], "nt", "proj_in_bwd")
    dh1, dh1b, small["norm_mix"] = rmsnorm_bwd(dxn2, h1, norm_mix, dh2, "norm_mix_bwd")

    def pack(parts):
        flat = jnp.concatenate([v.reshape(-1) for v in parts.values()])
        return _pad_to(flat, 0, SUBLANE * LANE).reshape(-1, LANE), flat.shape[0]

    def unpack(everyones, n, parts, tag):
        rows = everyones.shape[0] // N_DEV
        summed = sum_devices(everyones.reshape(N_DEV, rows, LANE), "sum_" + tag).reshape(-1)[:n]
        out, off = {}, 0
        for k, v in parts.items():
            out[k] = summed[off:off + v.size].reshape(v.shape)
            off += v.size
        return out

    early = dict(small)
    flat_early, n_early = pack(early)
    landed = []
    dxn1 = ffn_bwd("ffn1", dh1b, xn1, gate1, up1, act1, "w1_gate", "w1_up", "w1_down",
                   extra=(gather_task([flat_early], [0]), landed.extend))
    tot = unpack(landed[0], n_early, early, "small")
    grad_x, _, g_norm_ffn1 = rmsnorm_bwd(dxn1, xs, norm_ffn1, dh1, "norm_ffn1_bwd")
    assert not wait_sibling and not wait_chips and set(reduced) == set(big_names)
    last = {"norm_ffn1": g_norm_ffn1}
    flat_last, n_last = pack(last)
    ((everyones_last,),) = run_tasks([gather_task([flat_last], [0])], "gather_last")
    tot.update(unpack(everyones_last, n_last, last, "last"))

    out_g, out_d, out_m, out_v = {}, {}, {}, {}
    for k in big_names:
        g, dl, nm, nv = adam_sharded(reduced[k], view(weights[k], k), view(moments_m[k], k), view(moments_v[k], k),
                                     "adam_" + k)
        out_g[k], out_d[k], out_m[k], out_v[k] = unview(g, k), unview(dl, k), unview(nm, k), unview(nv, k)

    _, ssm_vjp = jax.vjp(_ssm_discretize, ssm_log_dt[0], ssm_a_re[0], ssm_a_im[0], ssm_b_re[0], ssm_b_im[0])
    g_log_dt, g_a_re, g_a_im, g_b_re, g_b_im = ssm_vjp((tot["abar_r"], tot["abar_i"], tot["bbar_r"], tot["bbar_i"]))
    small_grads = {
        "norm_ffn1": tot["norm_ffn1"], "norm_mix": tot["norm_mix"], "ssm_log_dt": g_log_dt, "ssm_a_re": g_a_re,
        "ssm_a_im": g_a_im, "ssm_b_re": g_b_re, "ssm_b_im": g_b_im, "ssm_c_re": tot["c_re"], "ssm_c_im": tot["c_im"],
        "ssm_d": tot["ssm_d"], "gmlp_norm_v": tot["gmlp_norm_v"],
        "gmlp_w_s": jnp.where(causal[None], tot["gmlp_w_s"], 0.0), "gmlp_b_s": tot["gmlp_b_s"],
        "norm_ssm_out": tot["norm_ssm_out"], "norm_gmlp_out": tot["norm_gmlp_out"], "norm_ffn2": tot["norm_ffn2"],
        "norm_ple": tot["norm_ple"], "norm_final": tot["norm_final"]}
    for k, g in small_grads.items():
        shp = weights[k].shape
        g2 = _as2d(g.reshape(shp))
        dl, nm, nv = adam_small(g2, _as2d(weights[k]), _as2d(moments_m[k]), _as2d(moments_v[k]), "adam_" + k)
        out_g[k], out_d[k], out_m[k], out_v[k] = g2.reshape(shp), dl.reshape(shp), nm.reshape(shp), nv.reshape(shp)

    return (loss, grad_x[None], *[out_g[k] for k in names], *[out_d[k] for k in names],
            *[out_m[k] for k in names], *[out_v[k] for k in names])
```

```python
import math

import jax
import jax.numpy as jnp
from jax import lax
from jax.experimental import pallas as pl
from jax.experimental.pallas import tpu as pltpu

F32 = jnp.float32
BF16 = jnp.bfloat16
MESH_DT = pl.DeviceIdType.MESH

N_DEV = 8
N_CHIP = 4
LANE = 128
SUBLANE = 8
VMEM_LIMIT = 56 * 1024 * 1024

EPS = 1e-6
SSM_GROUP = 16
SSM_STATE = 64
GROUPS_PER_BLOCK = LANE // SSM_GROUP
STATE_BLOCK = GROUPS_PER_BLOCK * SSM_STATE
GMLP_HEAD = 128
CHUNK = 128

ADAM_LR = 0.001
ADAM_B1 = 0.9
ADAM_B2 = 0.999
ADAM_EPS = 1e-08
ADAM_WD = 0.01
ADAM_STEP = 10

GELU_K = math.sqrt(2.0 / math.pi)
GELU_C = 0.044715


def _cparams():
    return pltpu.CompilerParams(vmem_limit_bytes=VMEM_LIMIT)


def _tile(n, pref):
    if n <= pref:
        return n
    t = (pref // LANE) * LANE
    while t > 0:
        if n % t == 0:
            return t
        t -= LANE
    return n


def _row_tile(n, pref):
    if n <= pref:
        return n
    t = (pref // SUBLANE) * SUBLANE
    while t > 0:
        if n % t == 0:
            return t
        t -= SUBLANE
    return n


def _gelu(x):
    t = jnp.tanh(GELU_K * (x + GELU_C * x * x * x))
    return 0.5 * x * (1.0 + t)


def _gelu_grad(x):
    t = jnp.tanh(GELU_K * (x + GELU_C * x * x * x))
    return 0.5 * (1.0 + t) + 0.5 * x * (1.0 - t * t) * (GELU_K * (1.0 + 3.0 * GELU_C * x * x))


def _sigmoid(x):
    return 1.0 / (1.0 + jnp.exp(-x))


_DN = {
    "nn": (((1,), (0,)), ((), ())),
    "nt": (((1,), (1,)), ((), ())),
    "tn": (((0,), (0,)), ((), ())),
}


def _dot(a, b, mode="nn"):
    return lax.dot_general(a, b, _DN[mode], preferred_element_type=F32)


class CommTask:
    def __init__(self, inputs, out_shape, n_sems, start, late, finish):
        self.inputs, self.out_shape, self.n_sems = list(inputs), list(out_shape), n_sems
        self.start, self.late, self.finish = start, late, finish


def _call(body, *, name, grid, in_specs, out_specs, out_shape, args, scratch_shapes=(), tasks=()):
    in_specs, out_specs, out_shape = list(in_specs), list(out_specs), list(out_shape)
    scratch_shapes = list(scratch_shapes)
    if not tasks:
        return pl.pallas_call(
            body, name=name, grid=grid, in_specs=in_specs, out_specs=out_specs, out_shape=out_shape,
            scratch_shapes=scratch_shapes, compiler_params=_cparams())(*args)
    n_in, n_out, n_scr = len(in_specs), len(out_specs), len(scratch_shapes)
    t_in = [len(t.inputs) for t in tasks]
    t_out = [len(t.out_shape) for t in tasks]
    late_step = grid[0] - max(1, grid[0] // 4)
    has_late = grid[0] >= 2

    def carried(*refs):
        pos = n_in
        task_ins = []
        for k in t_in:
            task_ins.append(refs[pos:pos + k])
            pos += k
        outs = refs[pos:pos + n_out]
        pos += n_out
        task_outs = []
        for k in t_out:
            task_outs.append(refs[pos:pos + k])
            pos += k
        scratch = refs[pos:pos + n_scr]
        pos += n_scr
        sems = [refs[pos + 3 * i:pos + 3 * i + 3] for i in range(len(tasks))]
        ids = [pl.program_id(d) for d in range(len(grid))]
        rest_zero = True
        for d in range(1, len(grid)):
            rest_zero = jnp.logical_and(rest_zero, ids[d] == 0)
        first = jnp.logical_and(ids[0] == 0, rest_zero)
        last = ids[0] == grid[0] - 1
        for d in range(1, len(grid)):
            last = jnp.logical_and(last, ids[d] == grid[d] - 1)

        @pl.when(first)
        def _():
            for t, ti, to, s in zip(tasks, task_ins, task_outs, sems):
                t.start(ti, to, *s)

        if has_late:
            @pl.when(jnp.logical_and(ids[0] == late_step, rest_zero))
            def _():
                for t, ti, to, s in zip(tasks, task_ins, task_outs, sems):
                    t.late(ti, to, *s)

        body(*refs[:n_in], *outs, *scratch)

        @pl.when(last)
        def _():
            for t, ti, to, s in zip(tasks, task_ins, task_outs, sems):
                if not has_late:
                    t.late(ti, to, *s)
                t.finish(ti, to, *s)

    any_spec = pl.BlockSpec(memory_space=pl.ANY)
    sem_shapes = [pltpu.SemaphoreType.DMA((n,)) for t in tasks for n in t.n_sems]
    res = pl.pallas_call(
        carried, name=name, grid=grid,
        in_specs=in_specs + [any_spec] * sum(t_in), out_specs=out_specs + [any_spec] * sum(t_out),
        out_shape=out_shape + [s for t in tasks for s in t.out_shape],
        scratch_shapes=scratch_shapes + sem_shapes, compiler_params=_cparams(),
    )(*args, *[a for t in tasks for a in t.inputs])
    res = list(res)
    task_res, pos = [], n_out
    for k in t_out:
        task_res.append(res[pos:pos + k])
        pos += k
    return res[:n_out], task_res


def _mm_dims(a, b, mode):
    if mode == "nn":
        (m, k), (k2, n) = a.shape, b.shape
    elif mode == "nt":
        (m, k), (n, k2) = a.shape, b.shape
    else:
        (k, m), (k2, n) = a.shape, b.shape
    assert k == k2, (a.shape, b.shape, mode)
    return m, n, k


def _mm_specs(mode, tm, tn, tk):
    if mode == "tn":
        a_spec = pl.BlockSpec((tk, tm), lambda i, j, k: (k, i))
    else:
        a_spec = pl.BlockSpec((tm, tk), lambda i, j, k: (i, k))
    if mode == "nt":
        b_spec = pl.BlockSpec((tn, tk), lambda i, j, k: (j, k))
    else:
        b_spec = pl.BlockSpec((tk, tn), lambda i, j, k: (k, j))
    return a_spec, b_spec


def _accumulate(acc, nk, partial, emit):
    if nk == 1:
        emit(partial)
        return
    kk = pl.program_id(2)

    @pl.when(kk == 0)
    def _():
        acc[...] = partial

    @pl.when(kk > 0)
    def _():
        acc[...] += partial

    @pl.when(kk == nk - 1)
    def _():
        emit(acc[...])


def matmul(a, b, mode, name, out_dtype=F32, res=None, scale=1.0, tm=1024, tn=1024, tk=2048, tasks=()):
    m, n, k = _mm_dims(a, b, mode)
    tm, tn, tk = _tile(m, tm), _tile(n, tn), _tile(k, tk)
    nk = k // tk
    a_spec, b_spec = _mm_specs(mode, tm, tn, tk)
    o_spec = pl.BlockSpec((tm, tn), lambda i, j, k: (i, j))
    has_res = res is not None

    def body(*refs):
        if has_res:
            a_ref, b_ref, r_ref, o_ref, acc = refs
        else:
            a_ref, b_ref, o_ref, acc = refs

        def emit(v):
            if scale != 1.0:
                v = v * scale
            if has_res:
                v = r_ref[...] + v
            o_ref[...] = v.astype(out_dtype)

        _accumulate(acc, nk, _dot(a_ref[...], b_ref[...], mode), emit)

    out = _call(
        body, name=name, grid=(m // tm, n // tn, nk),
        in_specs=[a_spec, b_spec] + ([o_spec] if has_res else []), out_specs=[o_spec],
        out_shape=[jax.ShapeDtypeStruct((m, n), out_dtype)], args=(a, b) + ((res,) if has_res else ()),
        scratch_shapes=[pltpu.VMEM((tm, tn) if nk > 1 else (SUBLANE, LANE), F32)], tasks=tasks)
    return (out[0][0], out[1]) if tasks else out[0]


def ffn_up(xn, wu, gate, name, tm=1024, tn=1024, tk=2048, tasks=()):
    m, n, k = _mm_dims(xn, wu, "nt")
    tm, tn, tk = _tile(m, tm), _tile(n, tn), _tile(k, tk)
    nk = k // tk
    a_spec, b_spec = _mm_specs("nt", tm, tn, tk)
    o_spec = pl.BlockSpec((tm, tn), lambda i, j, k: (i, j))

    def body(a_ref, u_ref, gate_ref, up_ref, act_ref, acc):
        def emit(u):
            g = gate_ref[...]
            up_ref[...] = u
            act_ref[...] = (g * _sigmoid(g) * u).astype(BF16)

        _accumulate(acc, nk, _dot(a_ref[...], u_ref[...], "nt"), emit)

    out = _call(
        body, name=name, grid=(m // tm, n // tn, nk), in_specs=[a_spec, b_spec, o_spec],
        out_specs=[o_spec, o_spec],
        out_shape=[jax.ShapeDtypeStruct((m, n), F32), jax.ShapeDtypeStruct((m, n), BF16)],
        args=(xn, wu, gate), scratch_shapes=[pltpu.VMEM((tm, tn) if nk > 1 else (SUBLANE, LANE), F32)], tasks=tasks)
    return (tuple(out[0]), out[1]) if tasks else tuple(out)


def ffn_bwd_act(dh, wd, gate, up, name, tm=1024, tn=512, tk=2048, tasks=()):
    m, n, k = _mm_dims(dh, wd, "nt")
    tm, tn, tk = _tile(m, tm), _tile(n, tn), _tile(k, tk)
    nk = k // tk
    a_spec, b_spec = _mm_specs("nt", tm, tn, tk)
    o_spec = pl.BlockSpec((tm, tn), lambda i, j, k: (i, j))

    def body(a_ref, b_ref, gate_ref, up_ref, dg_ref, du_ref, acc):
        def emit(total):
            dact = 0.5 * total
            g = gate_ref[...]
            sg = _sigmoid(g)
            du_ref[...] = (dact * (g * sg)).astype(BF16)
            dg_ref[...] = (dact * up_ref[...] * (sg * (1.0 + g * (1.0 - sg)))).astype(BF16)

        _accumulate(acc, nk, _dot(a_ref[...], b_ref[...], "nt"), emit)

    out = _call(
        body, name=name, grid=(m // tm, n // tn, nk), in_specs=[a_spec, b_spec, o_spec, o_spec],
        out_specs=[o_spec, o_spec],
        out_shape=[jax.ShapeDtypeStruct((m, n), BF16), jax.ShapeDtypeStruct((m, n), BF16)],
        args=(dh, wd, gate, up), scratch_shapes=[pltpu.VMEM((tm, tn) if nk > 1 else (SUBLANE, LANE), F32)],
        tasks=tasks)
    return (tuple(out[0]), out[1]) if tasks else tuple(out)


def _rows(t, d, tr):
    return pl.BlockSpec((tr, d), lambda i: (i, 0))


def _vec(d):
    return pl.BlockSpec((1, d), lambda i: (0, 0))


def rmsnorm_fwd(x, g, name, tr=512):
    t, d = x.shape
    tr = _row_tile(t, tr)

    def body(x_ref, g_ref, o_ref):
        xf = x_ref[...]
        r = lax.rsqrt(jnp.mean(xf * xf, axis=-1, keepdims=True) + EPS)
        o_ref[...] = (xf * r * g_ref[...]).astype(BF16)

    return pl.pallas_call(
        body, name=name, grid=(t // tr,), in_specs=[_rows(t, d, tr), _vec(d)], out_specs=_rows(t, d, tr),
        out_shape=jax.ShapeDtypeStruct((t, d), BF16), compiler_params=_cparams(),
    )(x, g)


def _rms_bwd(dxn, xf, g):
    r = lax.rsqrt(jnp.mean(xf * xf, axis=-1, keepdims=True) + EPS)
    xhat = xf * r
    dg = jnp.sum(dxn * xhat, axis=0, keepdims=True)
    dxh = dxn * g
    dx = r * (dxh - xhat * jnp.mean(dxh * xhat, axis=-1, keepdims=True))
    return dx, dg


def rmsnorm_bwd(dxn, x, g, dres, name, tr=256):
    t, d = x.shape
    tr = _row_tile(t, tr)

    def body(dxn_ref, x_ref, g_ref, dres_ref, o_ref, ob_ref, dg_ref):
        dx, dg = _rms_bwd(dxn_ref[...], x_ref[...], g_ref[...])
        out = dres_ref[...] + dx
        o_ref[...] = out
        ob_ref[...] = out.astype(BF16)

        @pl.when(pl.program_id(0) == 0)
        def _():
            dg_ref[...] = jnp.zeros_like(dg_ref)

        dg_ref[...] += dg

    return pl.pallas_call(
        body, name=name, grid=(t // tr,),
        in_specs=[_rows(t, d, tr), _rows(t, d, tr), _vec(d), _rows(t, d, tr)],
        out_specs=[_rows(t, d, tr), _rows(t, d, tr), _vec(d)],
        out_shape=[jax.ShapeDtypeStruct((t, d), F32), jax.ShapeDtypeStruct((t, d), BF16),
                   jax.ShapeDtypeStruct((1, d), F32)],
        compiler_params=_cparams(),
    )(dxn, x, g, dres)


def final_loss(h, target, g, name, tr=256):
    t, d = h.shape
    tr = _row_tile(t, tr)

    def body(h_ref, t_ref, g_ref, dh_ref, loss_ref, dg_ref):
        xf = h_ref[...]
        gg = g_ref[...]
        r = lax.rsqrt(jnp.mean(xf * xf, axis=-1, keepdims=True) + EPS)
        xhat = xf * r
        e = xhat * gg - t_ref[...]
        part = jnp.sum(jnp.sum(e * e, axis=1, keepdims=True), axis=0, keepdims=True) * (0.5 / d)
        dout = e * (1.0 / d)
        dg = jnp.sum(dout * xhat, axis=0, keepdims=True)
        dxh = dout * gg
        dh_ref[...] = r * (dxh - xhat * jnp.mean(dxh * xhat, axis=-1, keepdims=True))

        @pl.when(pl.program_id(0) == 0)
        def _():
            dg_ref[...] = jnp.zeros_like(dg_ref)
            loss_ref[...] = jnp.zeros_like(loss_ref)

        dg_ref[...] += dg
        loss_ref[...] += jnp.broadcast_to(part, loss_ref.shape)

    return pl.pallas_call(
        body, name=name, grid=(t // tr,),
        in_specs=[_rows(t, d, tr), _rows(t, d, tr), _vec(d)],
        out_specs=[_rows(t, d, tr), pl.BlockSpec((SUBLANE, LANE), lambda i: (0, 0)), _vec(d)],
        out_shape=[jax.ShapeDtypeStruct((t, d), F32), jax.ShapeDtypeStruct((SUBLANE, LANE), F32),
                   jax.ShapeDtypeStruct((1, d), F32)],
        compiler_params=_cparams(),
    )(h, target, g)


def ple_fwd(h, glin, pp, name, tr=512):
    t, d = h.shape
    tr = _row_tile(t, tr)

    def body(h_ref, gl_ref, pp_ref, o_ref):
        o_ref[...] = h_ref[...] + _sigmoid(gl_ref[...]) * pp_ref[...]

    sp = _rows(t, d, tr)
    return pl.pallas_call(
        body, name=name, grid=(t // tr,), in_specs=[sp, sp, sp], out_specs=sp,
        out_shape=jax.ShapeDtypeStruct((t, d), F32), compiler_params=_cparams(),
    )(h, glin, pp)


def ple_bwd(dh, glin, pp, name, tr=512):
    t, d = dh.shape
    tr = _row_tile(t, tr)

    def body(dh_ref, gl_ref, pp_ref, dpp_ref, dgl_ref):
        gate = _sigmoid(gl_ref[...])
        dh_ = dh_ref[...]
        dpp_ref[...] = (dh_ * gate).astype(BF16)
        dgl_ref[...] = (dh_ * pp_ref[...] * gate * (1.0 - gate)).astype(BF16)

    sp = _rows(t, d, tr)
    return pl.pallas_call(
        body, name=name, grid=(t // tr,), in_specs=[sp, sp, sp], out_specs=[sp, sp],
        out_shape=[jax.ShapeDtypeStruct((t, d), BF16), jax.ShapeDtypeStruct((t, d), BF16)],
        compiler_params=_cparams(),
    )(dh, glin, pp)


def mix_out_fwd(y_pre, glin, y_gmlp, g_so, g_go, name, tr=512):
    t, d = y_pre.shape
    tr = _row_tile(t, tr)

    def body(yp_ref, gl_ref, yg_ref, gs_ref, gg_ref, o_ref):
        ys = _gelu(yp_ref[...]) * _sigmoid(gl_ref[...])
        r = lax.rsqrt(jnp.mean(ys * ys, axis=-1, keepdims=True) + EPS)
        o_ref[:, 0:d] = (ys * r * gs_ref[...]).astype(BF16)
        yq = yg_ref[...]
        r2 = lax.rsqrt(jnp.mean(yq * yq, axis=-1, keepdims=True) + EPS)
        o_ref[:, d:2 * d] = (yq * r2 * gg_ref[...]).astype(BF16)

    sp = _rows(t, d, tr)
    return pl.pallas_call(
        body, name=name, grid=(t // tr,), in_specs=[sp, sp, sp, _vec(d), _vec(d)],
        out_specs=_rows(t, 2 * d, tr), out_shape=jax.ShapeDtypeStruct((t, 2 * d), BF16),
        compiler_params=_cparams(),
    )(y_pre, glin, y_gmlp, g_so, g_go)


def mix_out_bwd(dycat, y_pre, glin, y_gmlp, g_so, g_go, name, tr=256):
    t, d = y_pre.shape
    tr = _row_tile(t, tr)

    def body(dy_ref, yp_ref, gl_ref, yg_ref, gs_ref, gg_ref, dyg_ref, dl_ref, dyq_ref, dgs_ref, dgg_ref):
        yg = _gelu(yp_ref[...])
        sg = _sigmoid(gl_ref[...])
        dys, dgs = _rms_bwd(dy_ref[:, 0:d], yg * sg, gs_ref[...])
        dyg_ref[...] = dys * sg
        dl_ref[...] = (dys * yg * sg * (1.0 - sg)).astype(BF16)
        dyq, dgg = _rms_bwd(dy_ref[:, d:2 * d], yg_ref[...], gg_ref[...])
        dyq_ref[...] = dyq

        @pl.when(pl.program_id(0) == 0)
        def _():
            dgs_ref[...] = jnp.zeros_like(dgs_ref)
            dgg_ref[...] = jnp.zeros_like(dgg_ref)

        dgs_ref[...] += dgs
        dgg_ref[...] += dgg

    sp = _rows(t, d, tr)
    return pl.pallas_call(
        body, name=name, grid=(t // tr,),
        in_specs=[_rows(t, 2 * d, tr), sp, sp, sp, _vec(d), _vec(d)],
        out_specs=[sp, sp, sp, _vec(d), _vec(d)],
        out_shape=[jax.ShapeDtypeStruct((t, d), F32), jax.ShapeDtypeStruct((t, d), BF16),
                   jax.ShapeDtypeStruct((t, d), F32), jax.ShapeDtypeStruct((1, d), F32),
                   jax.ShapeDtypeStruct((1, d), F32)],
        compiler_params=_cparams(),
    )(dycat, y_pre, glin, y_gmlp, g_so, g_go)


SCAN_COLS = 512


def _scan_tile(xr, xi, const, cr, ci, reverse):
    for lvl, sh in enumerate((1, 2, 4)):
        ar, ai = const(2 * lvl), const(2 * lvl + 1)
        s = (SUBLANE - sh) if reverse else sh
        rr = pltpu.roll(xr, s, 0)
        ri = pltpu.roll(xi, s, 0)
        xr, xi = xr + ar * rr - ai * ri, xi + ar * ri + ai * rr
    pr, pi_ = const(6), const(7)
    xr, xi = xr + pr * cr - pi_ * ci, xi + pr * ci + pi_ * cr
    return xr, xi


def _bcast_row(x, row):
    return jnp.broadcast_to(x[row:row + 1, :], x.shape)


def s5_fwd(z, bc_r, bc_i, cc_r, cc_i, apw, dvec, name, tc=256, tasks=()):
    t = z.shape[0]
    nblk = bc_r.shape[0]
    d = nblk * LANE
    ns = nblk * STATE_BLOCK
    tc = _row_tile(t, tc)
    ntile = tc // SUBLANE

    def body(z_ref, br_ref, bi_ref, cr_ref, ci_ref, apw_ref, d_ref, y_ref, yg_ref, sr_ref, si_ref, carry):
        @pl.when(pl.program_id(0) == 0)
        def _():
            carry[...] = jnp.zeros_like(carry)

        for j in range(nblk):
            uj = z_ref[:, j * LANE:(j + 1) * LANE]
            ub = uj.astype(BF16)
            for q in range(STATE_BLOCK // SCAN_COLS):
                c0 = j * STATE_BLOCK + q * SCAN_COLS
                cs = pl.ds(c0, SCAN_COLS)
                bs = slice(q * SCAN_COLS, (q + 1) * SCAN_COLS)
                sr_ref[:, cs] = _dot(ub, br_ref[j, :, bs])
                si_ref[:, cs] = _dot(ub, bi_ref[j, :, bs])
                const = lambda k, cs=cs: apw_ref[k, :, cs]

                def tile(k, c, cs=cs, const=const):
                    rows = pl.ds(pl.multiple_of(k * SUBLANE, SUBLANE), SUBLANE)
                    xr, xi = _scan_tile(sr_ref[rows, cs], si_ref[rows, cs], const, c[0], c[1], False)
                    sr_ref[rows, cs] = xr
                    si_ref[rows, cs] = xi
                    return _bcast_row(xr, SUBLANE - 1), _bcast_row(xi, SUBLANE - 1)

                c_r, c_i = lax.fori_loop(0, ntile, tile, (carry[0, :, cs], carry[1, :, cs]))
                carry[0, :, cs] = c_r
                carry[1, :, cs] = c_i
            sb = pl.ds(j * STATE_BLOCK, STATE_BLOCK)
            y = (_dot(sr_ref[:, sb].astype(BF16), cr_ref[j]) - _dot(si_ref[:, sb].astype(BF16), ci_ref[j])
                 + d_ref[:, j * LANE:(j + 1) * LANE] * uj)
            y_ref[:, j * LANE:(j + 1) * LANE] = y
            yg_ref[:, j * LANE:(j + 1) * LANE] = _gelu(y).astype(BF16)

    full3 = lambda shp: pl.BlockSpec(shp, lambda i: (0, 0, 0))
    out = _call(
        body, name=name, grid=(t // tc,),
        in_specs=[pl.BlockSpec((tc, d), lambda i: (i, 0)), full3(bc_r.shape), full3(bc_i.shape),
                  full3(cc_r.shape), full3(cc_i.shape), full3(apw.shape), _vec(d)],
        out_specs=[pl.BlockSpec((tc, d), lambda i: (i, 0)), pl.BlockSpec((tc, d), lambda i: (i, 0)),
                   pl.BlockSpec((tc, ns), lambda i: (i, 0)), pl.BlockSpec((tc, ns), lambda i: (i, 0))],
        out_shape=[jax.ShapeDtypeStruct((t, d), F32), jax.ShapeDtypeStruct((t, d), BF16),
                   jax.ShapeDtypeStruct((t, ns), F32), jax.ShapeDtypeStruct((t, ns), F32)],
        args=(z, bc_r, bc_i, cc_r, cc_i, apw, dvec), scratch_shapes=[pltpu.VMEM((2, SUBLANE, ns), F32)], tasks=tasks)
    return (tuple(out[0]), out[1]) if tasks else tuple(out)


def s5_bwd(dyg, y_pre, z, sr, si, bc_r, bc_i, cc_r, cc_i, apw_rev, dvec, name, tc=128, tasks=()):
    t = z.shape[0]
    nblk = bc_r.shape[0]
    d = nblk * LANE
    ns = nblk * STATE_BLOCK
    tc = _row_tile(t, tc)
    ntile = tc // SUBLANE
    nchunk = t // tc
    tiles_per_chunk = tc // SUBLANE

    def body(dyg_ref, yp_ref, z_ref, sr_ref, si_ref, pr_ref, pi_ref, br_ref, bi_ref, cr_ref, ci_ref, apw_ref,
             d_ref, du_ref, gd_ref, gcr_ref, gci_ref, gbr_ref, gbi_ref, gar_ref, gai_ref, lr_ref, li_ref, carry):
        step = pl.program_id(0)

        @pl.when(step == 0)
        def _():
            carry[...] = jnp.zeros_like(carry)
            for ref in (gd_ref, gcr_ref, gci_ref, gbr_ref, gbi_ref, gar_ref, gai_ref):
                ref[...] = jnp.zeros_like(ref)

        first_chunk = (step == nchunk - 1).astype(F32)
        keep_prev = 1.0 - first_chunk
        row0 = lax.broadcasted_iota(jnp.int32, (SUBLANE, SCAN_COLS), 0) == 0

        for j in range(nblk):
            lanes = slice(j * LANE, (j + 1) * LANE)
            uj = z_ref[:, lanes]
            ub = uj.astype(BF16)
            gy = dyg_ref[:, lanes] * _gelu_grad(yp_ref[:, lanes])
            gyb = gy.astype(BF16)
            gd_ref[:, lanes] += jnp.sum(gy * uj, axis=0, keepdims=True)
            for q in range(STATE_BLOCK // SCAN_COLS):
                c0 = j * STATE_BLOCK + q * SCAN_COLS
                cs = pl.ds(c0, SCAN_COLS)
                bs = slice(q * SCAN_COLS, (q + 1) * SCAN_COLS)
                lr_ref[:, cs] = _dot(gyb, cr_ref[j, bs, :], "nt")
                li_ref[:, cs] = -_dot(gyb, ci_ref[j, bs, :], "nt")
                const = lambda k, cs=cs: apw_ref[k, :, cs]

                def one_tile(rows, prev_r, prev_i, c, cs=cs, const=const):
                    cr_, ci_, gar, gai = c
                    xr, xi = _scan_tile(lr_ref[rows, cs], li_ref[rows, cs], const, cr_, ci_, True)
                    lr_ref[rows, cs] = xr
                    li_ref[rows, cs] = xi
                    spr = jnp.where(row0, prev_r, pltpu.roll(sr_ref[rows, cs], 1, 0))
                    spi = jnp.where(row0, prev_i, pltpu.roll(si_ref[rows, cs], 1, 0))
                    gar = gar + xr * spr + xi * spi
                    gai = gai + xi * spr - xr * spi
                    return _bcast_row(xr, 0), _bcast_row(xi, 0), gar, gai

                def tile(k, c, cs=cs, one_tile=one_tile):
                    kk = ntile - 1 - k
                    rows = pl.ds(pl.multiple_of(kk * SUBLANE, SUBLANE), SUBLANE)
                    prow = pl.ds(pl.multiple_of((kk - 1) * SUBLANE, SUBLANE), SUBLANE)
                    prev_r = _bcast_row(sr_ref[prow, cs], SUBLANE - 1)
                    prev_i = _bcast_row(si_ref[prow, cs], SUBLANE - 1)
                    return one_tile(rows, prev_r, prev_i, c)

                zero = jnp.zeros((SUBLANE, SCAN_COLS), F32)
                c = lax.fori_loop(0, ntile - 1, tile, (carry[0, :, cs], carry[1, :, cs], zero, zero))
                prev_r = _bcast_row(pr_ref[:, cs], SUBLANE - 1) * keep_prev
                prev_i = _bcast_row(pi_ref[:, cs], SUBLANE - 1) * keep_prev
                c_r, c_i, gar, gai = one_tile(pl.ds(0, SUBLANE), prev_r, prev_i, c)
                carry[0, :, cs] = c_r
                carry[1, :, cs] = c_i
                gar_ref[:, cs] += gar
                gai_ref[:, cs] += gai
            sb = pl.ds(j * STATE_BLOCK, STATE_BLOCK)
            lrb = lr_ref[:, sb].astype(BF16)
            lib = li_ref[:, sb].astype(BF16)
            gcr_ref[j] += _dot(gyb, sr_ref[:, sb].astype(BF16), "tn")
            gci_ref[j] -= _dot(gyb, si_ref[:, sb].astype(BF16), "tn")
            gbr_ref[j] += _dot(ub, lrb, "tn")
            gbi_ref[j] += _dot(ub, lib, "tn")
            du = _dot(lrb, br_ref[j], "nt") + _dot(lib, bi_ref[j], "nt") + gy * d_ref[:, lanes]
            du_ref[:, lanes] = du.astype(BF16)

    rev = lambda i: (nchunk - 1 - i, 0)
    prev = lambda i: (jnp.maximum((nchunk - 1 - i) * tiles_per_chunk - 1, 0), 0)
    full3 = lambda shp: pl.BlockSpec(shp, lambda i: (0, 0, 0))
    acc3 = pl.BlockSpec((nblk, LANE, STATE_BLOCK), lambda i: (0, 0, 0))
    acc_rows = pl.BlockSpec((SUBLANE, ns), lambda i: (0, 0))
    out = _call(
        body, name=name, grid=(nchunk,),
        in_specs=[pl.BlockSpec((tc, d), rev), pl.BlockSpec((tc, d), rev), pl.BlockSpec((tc, d), rev),
                  pl.BlockSpec((tc, ns), rev), pl.BlockSpec((tc, ns), rev),
                  pl.BlockSpec((SUBLANE, ns), prev), pl.BlockSpec((SUBLANE, ns), prev),
                  full3(bc_r.shape), full3(bc_i.shape), full3(cc_r.shape), full3(cc_i.shape), full3(apw_rev.shape),
                  _vec(d)],
        out_specs=[pl.BlockSpec((tc, d), rev), _vec(d), acc3, acc3, acc3, acc3, acc_rows, acc_rows],
        out_shape=[jax.ShapeDtypeStruct((t, d), BF16), jax.ShapeDtypeStruct((1, d), F32)]
        + [jax.ShapeDtypeStruct((nblk, LANE, STATE_BLOCK), F32)] * 4
        + [jax.ShapeDtypeStruct((SUBLANE, ns), F32)] * 2,
        args=(dyg, y_pre, z, sr, si, sr, si, bc_r, bc_i, cc_r, cc_i, apw_rev, dvec),
        scratch_shapes=[pltpu.VMEM((tc, ns), F32), pltpu.VMEM((tc, ns), F32), pltpu.VMEM((2, SUBLANE, ns), F32)],
        tasks=tasks)
    return (tuple(out[0]), out[1]) if tasks else tuple(out)


def _cmul(a, b):
    return a[0] * b[0] - a[1] * b[1], a[0] * b[1] + a[1] * b[0]


def _scan_constants(abar_r, abar_i, reverse):
    ar = abar_r.reshape(1, -1)
    ai = abar_i.reshape(1, -1)
    if reverse:
        ai = -ai
    pw = [(ar, ai)]
    for _ in range(SUBLANE - 1):
        pw.append(_cmul(pw[-1], (ar, ai)))
    rows = lax.broadcasted_iota(jnp.int32, (SUBLANE, 1), 0)
    out = []
    for sh in (1, 2, 4):
        keep = (rows <= SUBLANE - 1 - sh) if reverse else (rows >= sh)
        for part in pw[sh - 1]:
            out.append(jnp.where(keep, part, 0.0))
    for comp in (0, 1):
        stack = jnp.concatenate([pw[k][comp] for k in range(SUBLANE)], axis=0)
        out.append(stack[::-1] if reverse else stack)
    return jnp.stack(out, axis=0).astype(F32)


def _ssm_discretize(log_dt, a_re, a_im, b_re, b_im):
    dt = jnp.exp(log_dt)[:, None]
    lr = jnp.minimum(a_re, -1e-4)
    li = a_im
    mag = jnp.exp(lr * dt)
    ang = li * dt
    abar_r = mag * jnp.cos(ang)
    abar_i = mag * jnp.sin(ang)
    den = lr * lr + li * li
    xr = abar_r - 1.0
    xi = abar_i
    zr = (xr * lr + xi * li) / den
    zi = (xi * lr - xr * li) / den
    bbar_r = zr[..., None] * b_re - zi[..., None] * b_im
    bbar_i = zr[..., None] * b_im + zi[..., None] * b_re
    return abar_r, abar_i, bbar_r, bbar_i


def _block_diag(w):
    g, a, b = w.shape
    nb = g // GROUPS_PER_BLOCK
    eye = jnp.eye(GROUPS_PER_BLOCK, dtype=w.dtype)
    w5 = w.reshape(nb, GROUPS_PER_BLOCK, a, b)
    out = w5[:, :, :, None, :] * eye[None, :, None, :, None]
    return out.reshape(nb, GROUPS_PER_BLOCK * a, GROUPS_PER_BLOCK * b)


def _block_diag_extract(m, a, b):
    nb = m.shape[0]
    eye = jnp.eye(GROUPS_PER_BLOCK, dtype=m.dtype)
    m5 = m.reshape(nb, GROUPS_PER_BLOCK, a, GROUPS_PER_BLOCK, b)
    out = jnp.sum(m5 * eye[None, :, None, :, None], axis=3)
    return out.reshape(nb * GROUPS_PER_BLOCK, a, b)


def _layer_norm(gv, nv):
    mu = jnp.mean(gv, axis=-1, keepdims=True)
    xc = gv - mu
    r = lax.rsqrt(jnp.mean(xc * xc, axis=-1, keepdims=True) + EPS)
    xhat = xc * r
    return xhat * nv, xhat, r


def gmlp_fwd(z, norm_v, wm, bs, name, tr=256):
    t = z.shape[0]
    nh = wm.shape[0]
    d = nh * GMLP_HEAD
    col0 = (z.shape[1] - 2 * d) // d
    tr = _row_tile(t, tr)

    def body(zu_ref, zv_ref, nv_ref, wm_ref, bs_ref, o_ref):
        v, _, _ = _layer_norm(_gelu(zv_ref[...]), nv_ref[...])
        vb = v.astype(BF16)
        u = _gelu(zu_ref[...])
        for c in range(tr // CHUNK):
            rows = slice(c * CHUNK, (c + 1) * CHUNK)
            for h in range(nh):
                cols = slice(h * GMLP_HEAD, (h + 1) * GMLP_HEAD)
                s = _dot(wm_ref[h], vb[rows, cols]) + bs_ref[h]
                o_ref[rows, cols] = u[rows, cols] * s

    return pl.pallas_call(
        body, name=name, grid=(t // tr,),
        in_specs=[pl.BlockSpec((tr, d), lambda i: (i, col0)), pl.BlockSpec((tr, d), lambda i: (i, col0 + 1)),
                  _vec(d), pl.BlockSpec(wm.shape, lambda i: (0, 0, 0)), pl.BlockSpec(bs.shape, lambda i: (0, 0, 0))],
        out_specs=pl.BlockSpec((tr, d), lambda i: (i, 0)),
        out_shape=jax.ShapeDtypeStruct((t, d), F32), compiler_params=_cparams(),
    )(z, z, norm_v, wm, bs)


def gmlp_bwd(dy, z, norm_v, wm, wmt, bs, name, tr=256):
    t = z.shape[0]
    nh = wm.shape[0]
    d = nh * GMLP_HEAD
    col0 = (z.shape[1] - 2 * d) // d
    tr = _row_tile(t, tr)

    def body(dy_ref, zu_ref, zv_ref, nv_ref, wm_ref, wmt_ref, bs_ref, dzu_ref, dzv_ref, dnv_ref, dwm_ref, dbs_ref,
             dv_ref):
        @pl.when(pl.program_id(0) == 0)
        def _():
            dnv_ref[...] = jnp.zeros_like(dnv_ref)
            dwm_ref[...] = jnp.zeros_like(dwm_ref)
            dbs_ref[...] = jnp.zeros_like(dbs_ref)

        zv = zv_ref[...]
        nv = nv_ref[...]
        v, xhat, r = _layer_norm(_gelu(zv), nv)
        vb = v.astype(BF16)
        zu = zu_ref[...]
        u = _gelu(zu)
        dy_ = dy_ref[...]
        for c in range(tr // CHUNK):
            rows = slice(c * CHUNK, (c + 1) * CHUNK)
            for h in range(nh):
                cols = slice(h * GMLP_HEAD, (h + 1) * GMLP_HEAD)
                vh = vb[rows, cols]
                s = _dot(wm_ref[h], vh) + bs_ref[h]
                dyh = dy_[rows, cols]
                dzu_ref[rows, cols] = (dyh * s * _gelu_grad(zu[rows, cols])).astype(BF16)
                ds = dyh * u[rows, cols]
                dsb = ds.astype(BF16)
                dbs_ref[h] += jnp.sum(ds, axis=1, keepdims=True)
                dwm_ref[h] += _dot(dsb, vh, "nt")
                dv_ref[rows, cols] = _dot(wmt_ref[h], dsb)
        dv = dv_ref[...]
        dnv_ref[...] += jnp.sum(dv * xhat, axis=0, keepdims=True)
        dxh = dv * nv
        dgv = r * (dxh - jnp.mean(dxh, axis=-1, keepdims=True) - xhat * jnp.mean(dxh * xhat, axis=-1, keepdims=True))
        dzv_ref[...] = (dgv * _gelu_grad(zv)).astype(BF16)

    full3 = lambda shp: pl.BlockSpec(shp, lambda i: (0, 0, 0))
    rows_d = pl.BlockSpec((tr, d), lambda i: (i, 0))
    return pl.pallas_call(
        body, name=name, grid=(t // tr,),
        in_specs=[rows_d, pl.BlockSpec((tr, d), lambda i: (i, col0)), pl.BlockSpec((tr, d), lambda i: (i, col0 + 1)),
                  _vec(d), full3(wm.shape), full3(wmt.shape), full3(bs.shape)],
        out_specs=[rows_d, rows_d, _vec(d), full3((nh, CHUNK, CHUNK)), full3((nh, CHUNK, 1))],
        out_shape=[jax.ShapeDtypeStruct((t, d), BF16), jax.ShapeDtypeStruct((t, d), BF16),
                   jax.ShapeDtypeStruct((1, d), F32), jax.ShapeDtypeStruct((nh, CHUNK, CHUNK), F32),
                   jax.ShapeDtypeStruct((nh, CHUNK, 1), F32)],
        scratch_shapes=[pltpu.VMEM((tr, d), F32)], compiler_params=_cparams(),
    )(dy, z, z, norm_v, wm, wmt, bs)


def _block(ref, axis, size, k):
    start = pl.multiple_of(k * size, size)
    if axis == 0:
        return ref.at[pl.ds(start, size), :]
    return ref.at[:, pl.ds(start, size)]


def _place():
    x, y, c = lax.axis_index("x"), lax.axis_index("y"), lax.axis_index("c")
    chips = [(1 - x, y), (x, 1 - y), (1 - x, 1 - y)]
    return x, y, c, chips


def _dev(x, y, c):
    return 4 * x + 2 * y + c


def gather_task(shards, axes):
    n = len(shards)
    sizes = [s.shape[ax] for s, ax in zip(shards, axes)]
    out_shape = [
        jax.ShapeDtypeStruct((s.shape[0] * N_DEV, s.shape[1]) if ax == 0 else (s.shape[0], s.shape[1] * N_DEV), s.dtype)
        for s, ax in zip(shards, axes)
    ]

    def copy(ins, outs, send_sems, recv_sems, t, k, block, to, from_input=False):
        dst = _block(outs[t], axes[t], sizes[t], _dev(*block))
        return pltpu.make_async_remote_copy(
            src_ref=ins[t] if from_input else dst, dst_ref=dst,
            send_sem=send_sems.at[t * 7 + k], recv_sem=recv_sems.at[t * 7 + k],
            device_id=to, device_id_type=MESH_DT)

    def local(ins, outs, local_sems, t, me):
        return pltpu.make_async_copy(ins[t], _block(outs[t], axes[t], sizes[t], _dev(*me)), local_sems.at[t])

    def start(ins, outs, send_sems, recv_sems, local_sems):
        x, y, c, chips = _place()
        me, sibling = (x, y, c), (x, y, 1 - c)
        for t in range(n):
            local(ins, outs, local_sems, t, me).start()
        for t in range(n):
            copy(ins, outs, send_sems, recv_sems, t, 0, me, sibling, True).start()
            for j, chip in enumerate(chips):
                copy(ins, outs, send_sems, recv_sems, t, 1 + j, me, (*chip, c), True).start()

    def late(ins, outs, send_sems, recv_sems, local_sems):
        x, y, c, chips = _place()
        me, sibling = (x, y, c), (x, y, 1 - c)
        for t in range(n):
            for j, chip in enumerate(chips):
                copy(ins, outs, send_sems, recv_sems, t, 1 + j, (*chip, c), me).wait_recv()
                copy(ins, outs, send_sems, recv_sems, t, 4 + j, (*chip, c), sibling).start()

    def finish(ins, outs, send_sems, recv_sems, local_sems):
        x, y, c, chips = _place()
        me, sibling = (x, y, c), (x, y, 1 - c)
        for t in range(n):
            copy(ins, outs, send_sems, recv_sems, t, 0, sibling, me).wait_recv()
            for j, chip in enumerate(chips):
                copy(ins, outs, send_sems, recv_sems, t, 4 + j, (*chip, 1 - c), me).wait_recv()
        for t in range(n):
            copy(ins, outs, send_sems, recv_sems, t, 0, me, sibling, True).wait_send()
            for j, chip in enumerate(chips):
                copy(ins, outs, send_sems, recv_sems, t, 1 + j, me, (*chip, c), True).wait_send()
                copy(ins, outs, send_sems, recv_sems, t, 4 + j, (*chip, c), sibling).wait_send()
            local(ins, outs, local_sems, t, me).wait()

    return CommTask(shards, out_shape, (7 * n, 7 * n, n), start, late, finish)


def _blk3(shape2, axis):
    r, c = shape2
    return (r // N_DEV, c) if axis == 0 else (r, c // N_DEV)


def _no_late(ins, outs, send_sems, recv_sems, local_sems):
    pass


def to_sibling_task(grads, axes):
    n = len(grads)
    blks = [_blk3(g.shape, ax) for g, ax in zip(grads, axes)]
    sizes = [b[ax] for b, ax in zip(blks, axes)]

    def copies(ins, outs, send_sems, recv_sems):
        x, y, c, _ = _place()
        return [pltpu.make_async_remote_copy(
            src_ref=_block(ins[t], axes[t], sizes[t], 2 * i + (1 - c)), dst_ref=outs[t].at[i],
            send_sem=send_sems.at[t * N_CHIP + i], recv_sem=recv_sems.at[t * N_CHIP + i],
            device_id=(x, y, 1 - c), device_id_type=MESH_DT) for t in range(n) for i in range(N_CHIP)]

    def start(ins, outs, send_sems, recv_sems, local_sems):
        for cp in copies(ins, outs, send_sems, recv_sems):
            cp.start()

    def finish(ins, outs, send_sems, recv_sems, local_sems):
        cps = copies(ins, outs, send_sems, recv_sems)
        for cp in cps:
            cp.wait_recv()
        for cp in cps:
            cp.wait_send()

    out_shape = [jax.ShapeDtypeStruct((N_CHIP,) + b, g.dtype) for b, g in zip(blks, grads)]
    return CommTask(grads, out_shape, (N_CHIP * n, N_CHIP * n, 1), start, _no_late, finish)


def across_chips_task(parts):
    n = len(parts)

    def copies(ins, outs, send_sems, recv_sems):
        x, y, c, chips = _place()
        my_chip = 2 * x + y
        return [pltpu.make_async_remote_copy(
            src_ref=ins[t].at[2 * chip[0] + chip[1]], dst_ref=outs[t].at[my_chip],
            send_sem=send_sems.at[t * 3 + j], recv_sem=recv_sems.at[t * 3 + j],
            device_id=(*chip, c), device_id_type=MESH_DT) for t in range(n) for j, chip in enumerate(chips)]

    def mine(ins, outs, local_sems):
        x, y, _, _ = _place()
        my_chip = 2 * x + y
        return [pltpu.make_async_copy(ins[t].at[my_chip], outs[t].at[my_chip], local_sems.at[t]) for t in range(n)]

    def start(ins, outs, send_sems, recv_sems, local_sems):
        for cp in mine(ins, outs, local_sems):
            cp.start()
        for cp in copies(ins, outs, send_sems, recv_sems):
            cp.start()

    def finish(ins, outs, send_sems, recv_sems, local_sems):
        cps = copies(ins, outs, send_sems, recv_sems)
        for cp in cps:
            cp.wait_recv()
        for cp in cps:
            cp.wait_send()
        for cp in mine(ins, outs, local_sems):
            cp.wait()

    out_shape = [jax.ShapeDtypeStruct(p.shape, p.dtype) for p in parts]
    return CommTask(parts, out_shape, (3 * n, 3 * n, n), start, _no_late, finish)


def run_tasks(tasks, name):
    t_in = [len(t.inputs) for t in tasks]
    t_out = [len(t.out_shape) for t in tasks]

    def body(*refs):
        pos, views = 0, []
        for k in t_in:
            views.append([refs[pos:pos + k]])
            pos += k
        for v, k in zip(views, t_out):
            v.append(refs[pos:pos + k])
            pos += k
        for i, v in enumerate(views):
            v.extend(refs[pos + 3 * i:pos + 3 * i + 3])
        for phase in ("start", "late", "finish"):
            for t, v in zip(tasks, views):
                getattr(t, phase)(*v)

    any_spec = pl.BlockSpec(memory_space=pl.ANY)
    res = pl.pallas_call(
        body, name=name, in_specs=[any_spec] * sum(t_in), out_specs=[any_spec] * sum(t_out),
        out_shape=[s for t in tasks for s in t.out_shape],
        scratch_shapes=[pltpu.SemaphoreType.DMA((k,)) for t in tasks for k in t.n_sems],
    )(*[a for t in tasks for a in t.inputs])
    res, out, pos = list(res), [], 0
    for k in t_out:
        out.append(res[pos:pos + k])
        pos += k
    return out


def rs_chip_sum(grad, recv, axis, core, name, tr=512):
    br, bc = _blk3(grad.shape, axis)
    tr = _row_tile(br, tr)
    nrb = br // tr

    if axis == 0:
        g_map = lambda i, r, c_ref: ((2 * i + c_ref[0]) * nrb + r, 0)
    else:
        g_map = lambda i, r, c_ref: (r, 2 * i + c_ref[0])

    def body(c_ref, g_ref, r_ref, o_ref):
        o_ref[...] = (g_ref[...].astype(F32) + r_ref[...].astype(F32)).astype(BF16)

    return pl.pallas_call(
        body, name=name,
        grid_spec=pltpu.PrefetchScalarGridSpec(
            num_scalar_prefetch=1, grid=(N_CHIP, nrb),
            in_specs=[pl.BlockSpec((tr, bc), g_map), pl.BlockSpec((None, tr, bc), lambda i, r, c_ref: (i, r, 0))],
            out_specs=pl.BlockSpec((None, tr, bc), lambda i, r, c_ref: (i, r, 0))),
        out_shape=jax.ShapeDtypeStruct((N_CHIP, br, bc), BF16), compiler_params=_cparams(),
    )(core, grad, recv)


def _adamw(w, g, m, v):
    m = ADAM_B1 * m + (1.0 - ADAM_B1) * g
    v = ADAM_B2 * v + (1.0 - ADAM_B2) * (g * g)
    m_hat = m / (1.0 - ADAM_B1 ** ADAM_STEP)
    v_hat = v / (1.0 - ADAM_B2 ** ADAM_STEP)
    delta = -ADAM_LR * (m_hat / (jnp.sqrt(v_hat) + ADAM_EPS) + ADAM_WD * w)
    return delta, m, v


def _sum_chips(p_ref):
    g = p_ref[0].astype(F32)
    for i in range(1, N_CHIP):
        g = g + p_ref[i].astype(F32)
    return g


def adam_sharded(parts, w, m, v, name, tr=256):
    r, c = w.shape
    assert parts.shape[2] == c
    tr = _row_tile(r, tr)

    def body(p_ref, w_ref, m_ref, v_ref, g_ref, d_ref, nm_ref, nv_ref):
        g = _sum_chips(p_ref)
        delta, nm, nv = _adamw(w_ref[...], g, m_ref[...], v_ref[...])
        g_ref[...] = g
        d_ref[...] = delta
        nm_ref[...] = nm
        nv_ref[...] = nv

    sp = pl.BlockSpec((tr, c), lambda i: (i, 0))
    return pl.pallas_call(
        body, name=name, grid=(r // tr,),
        in_specs=[pl.BlockSpec((N_CHIP, tr, c), lambda i: (0, i, 0)), sp, sp, sp],
        out_specs=[sp, sp, sp, sp], out_shape=[jax.ShapeDtypeStruct((r, c), F32)] * 4,
        compiler_params=_cparams(),
    )(parts, w, m, v)


def adam_small(g, w, m, v, name):
    def body(g_ref, w_ref, m_ref, v_ref, d_ref, nm_ref, nv_ref):
        delta, nm, nv = _adamw(w_ref[...], g_ref[...], m_ref[...], v_ref[...])
        d_ref[...] = delta
        nm_ref[...] = nm
        nv_ref[...] = nv

    return pl.pallas_call(
        body, name=name, out_shape=[jax.ShapeDtypeStruct(w.shape, F32)] * 3, compiler_params=_cparams(),
    )(g, w, m, v)


def sum_devices(gathered, name, tr=512):
    _, r, c = gathered.shape
    tr = _row_tile(r, tr)

    def body(x_ref, o_ref):
        s = x_ref[0]
        for k in range(1, N_DEV):
            s = s + x_ref[k]
        o_ref[...] = s

    return pl.pallas_call(
        body, name=name, grid=(r // tr,), in_specs=[pl.BlockSpec((N_DEV, tr, c), lambda i: (0, i, 0))],
        out_specs=pl.BlockSpec((tr, c), lambda i: (i, 0)), out_shape=jax.ShapeDtypeStruct((r, c), F32),
        compiler_params=_cparams(),
    )(gathered)


def _pad_to(a, axis, mult):
    size = a.shape[axis]
    pad = (-size) % mult
    if pad == 0:
        return a
    cfg = [(0, 0)] * a.ndim
    cfg[axis] = (0, pad)
    return jnp.pad(a, cfg)


def _as2d(a):
    if a.ndim == 1:
        return a.reshape(1, -1)
    return a.reshape(-1, a.shape[-1])


def kernel(x, p, norm_ffn1, w1_gate, w1_up, w1_down, norm_mix, w_in, ssm_log_dt, ssm_a_re, ssm_a_im, ssm_b_re, ssm_b_im, ssm_c_re, ssm_c_im, ssm_d, ssm_w_glu, gmlp_norm_v, gmlp_w_s, gmlp_b_s, norm_ssm_out, norm_gmlp_out, w_out, norm_ffn2, w2_gate, w2_up, w2_down, norm_ple, w_ple_gate, w_ple_proj, norm_final, loss_target, m_norm_ffn1, m_w1_gate, m_w1_up, m_w1_down, m_norm_mix, m_w_in, m_ssm_log_dt, m_ssm_a_re, m_ssm_a_im, m_ssm_b_re, m_ssm_b_im, m_ssm_c_re, m_ssm_c_im, m_ssm_d, m_ssm_w_glu, m_gmlp_norm_v, m_gmlp_w_s, m_gmlp_b_s, m_norm_ssm_out, m_norm_gmlp_out, m_w_out, m_norm_ffn2, m_w2_gate, m_w2_up, m_w2_down, m_norm_ple, m_w_ple_gate, m_w_ple_proj, m_norm_final, v_norm_ffn1, v_w1_gate, v_w1_up, v_w1_down, v_norm_mix, v_w_in, v_ssm_log_dt, v_ssm_a_re, v_ssm_a_im, v_ssm_b_re, v_ssm_b_im, v_ssm_c_re, v_ssm_c_im, v_ssm_d, v_ssm_w_glu, v_gmlp_norm_v, v_gmlp_w_s, v_gmlp_b_s, v_norm_ssm_out, v_norm_gmlp_out, v_w_out, v_norm_ffn2, v_w2_gate, v_w2_up, v_w2_down, v_norm_ple, v_w_ple_gate, v_w_ple_proj, v_norm_final):
    weights = dict(
        norm_ffn1=norm_ffn1, w1_gate=w1_gate, w1_up=w1_up, w1_down=w1_down, norm_mix=norm_mix, w_in=w_in,
        ssm_log_dt=ssm_log_dt, ssm_a_re=ssm_a_re, ssm_a_im=ssm_a_im, ssm_b_re=ssm_b_re, ssm_b_im=ssm_b_im,
        ssm_c_re=ssm_c_re, ssm_c_im=ssm_c_im, ssm_d=ssm_d, ssm_w_glu=ssm_w_glu, gmlp_norm_v=gmlp_norm_v,
        gmlp_w_s=gmlp_w_s, gmlp_b_s=gmlp_b_s, norm_ssm_out=norm_ssm_out, norm_gmlp_out=norm_gmlp_out, w_out=w_out,
        norm_ffn2=norm_ffn2, w2_gate=w2_gate, w2_up=w2_up, w2_down=w2_down, norm_ple=norm_ple,
        w_ple_gate=w_ple_gate, w_ple_proj=w_ple_proj, norm_final=norm_final)
    moments_m = dict(
        norm_ffn1=m_norm_ffn1, w1_gate=m_w1_gate, w1_up=m_w1_up, w1_down=m_w1_down, norm_mix=m_norm_mix, w_in=m_w_in,
        ssm_log_dt=m_ssm_log_dt, ssm_a_re=m_ssm_a_re, ssm_a_im=m_ssm_a_im, ssm_b_re=m_ssm_b_re, ssm_b_im=m_ssm_b_im,
        ssm_c_re=m_ssm_c_re, ssm_c_im=m_ssm_c_im, ssm_d=m_ssm_d, ssm_w_glu=m_ssm_w_glu, gmlp_norm_v=m_gmlp_norm_v,
        gmlp_w_s=m_gmlp_w_s, gmlp_b_s=m_gmlp_b_s, norm_ssm_out=m_norm_ssm_out, norm_gmlp_out=m_norm_gmlp_out,
        w_out=m_w_out, norm_ffn2=m_norm_ffn2, w2_gate=m_w2_gate, w2_up=m_w2_up, w2_down=m_w2_down,
        norm_ple=m_norm_ple, w_ple_gate=m_w_ple_gate, w_ple_proj=m_w_ple_proj, norm_final=m_norm_final)
    moments_v = dict(
        norm_ffn1=v_norm_ffn1, w1_gate=v_w1_gate, w1_up=v_w1_up, w1_down=v_w1_down, norm_mix=v_norm_mix, w_in=v_w_in,
        ssm_log_dt=v_ssm_log_dt, ssm_a_re=v_ssm_a_re, ssm_a_im=v_ssm_a_im, ssm_b_re=v_ssm_b_re, ssm_b_im=v_ssm_b_im,
        ssm_c_re=v_ssm_c_re, ssm_c_im=v_ssm_c_im, ssm_d=v_ssm_d, ssm_w_glu=v_ssm_w_glu, gmlp_norm_v=v_gmlp_norm_v,
        gmlp_w_s=v_gmlp_w_s, gmlp_b_s=v_gmlp_b_s, norm_ssm_out=v_norm_ssm_out, norm_gmlp_out=v_norm_gmlp_out,
        w_out=v_w_out, norm_ffn2=v_norm_ffn2, w2_gate=v_w2_gate, w2_up=v_w2_up, w2_down=v_w2_down,
        norm_ple=v_norm_ple, w_ple_gate=v_w_ple_gate, w_ple_proj=v_w_ple_proj, norm_final=v_norm_final)
    names = list(weights)

    xs = x[0]
    ps = p[0, 0].astype(BF16)
    tgt = loss_target[0]
    d_model = xs.shape[1]
    d_ssm = d_model // 2
    n_groups = d_ssm // SSM_GROUP

    transposed = ("w1_gate", "w1_up", "w2_gate", "w2_up")
    big = {
        "w1_gate": 0, "w1_up": 0, "w1_down": 0, "w_in": 1, "ssm_w_glu": 0, "w_out": 0,
        "w2_gate": 0, "w2_up": 0, "w2_down": 0, "w_ple_gate": 0, "w_ple_proj": 1}
    big_names = list(big)

    def view(a, k):
        return a[0].T if k in transposed else a[0]

    def unview(a, k):
        return a.T[None] if k in transposed else a[None]

    shard = {k: _pad_to(view(weights[k], k).astype(BF16), big[k], LANE) for k in big_names}
    W = {}

    def gather(group):
        return gather_task([shard[k] for k in group], [big[k] for k in group])

    def gathered(group, outs):
        W.update(zip(group, outs))

    abar_r, abar_i, bbar_r, bbar_i = _ssm_discretize(ssm_log_dt[0], ssm_a_re[0], ssm_a_im[0], ssm_b_re[0], ssm_b_im[0])
    bc_r = _block_diag(jnp.swapaxes(bbar_r, 1, 2)).astype(BF16)
    bc_i = _block_diag(jnp.swapaxes(bbar_i, 1, 2)).astype(BF16)
    cc_r = _block_diag(jnp.swapaxes(ssm_c_re[0], 1, 2)).astype(BF16)
    cc_i = _block_diag(jnp.swapaxes(ssm_c_im[0], 1, 2)).astype(BF16)
    apw_f = _scan_constants(abar_r, abar_i, False)
    apw_b = _scan_constants(abar_r, abar_i, True)
    causal = jnp.tril(jnp.ones((CHUNK, CHUNK), dtype=bool))
    wm = jnp.where(causal[None], gmlp_w_s[0], 0.0).astype(BF16)
    wmt = jnp.swapaxes(wm, 1, 2)
    bs = gmlp_b_s[0][:, :, None]

    def riding(fn, *a, group, **kw):
        out, (got,) = fn(*a, tasks=[gather(group)], **kw)
        gathered(group, got)
        return out

    gathered(["w1_gate"], run_tasks([gather(["w1_gate"])], "gather_ffn1_gate")[0])
    xn1 = rmsnorm_fwd(xs, norm_ffn1, "norm_ffn1")
    gate1 = riding(matmul, xn1, W["w1_gate"], "nt", "ffn1_gate", group=["w1_up"])
    up1, act1 = riding(ffn_up, xn1, W["w1_up"], gate1, "ffn1_up", group=["w1_down"])
    h1 = riding(matmul, act1, W["w1_down"], "nn", "ffn1_down", res=xs, scale=0.5, group=["w_in", "ssm_w_glu"])
    xn2 = rmsnorm_fwd(h1, norm_mix, "norm_mix")
    z = riding(matmul, xn2, W["w_in"], "nn", "proj_in", group=["w_out"])
    y_pre, yg, sr, si = riding(s5_fwd, z, bc_r, bc_i, cc_r, cc_i, apw_f, ssm_d, "s5_fwd",
                               group=["w2_gate", "w_ple_proj"])
    glin = matmul(yg, W["ssm_w_glu"], "nn", "ssm_glu")
    y_gmlp = gmlp_fwd(z, gmlp_norm_v, wm, bs, "gmlp_fwd")
    ycat = mix_out_fwd(y_pre, glin, y_gmlp, norm_ssm_out, norm_gmlp_out, "mix_out")
    h2 = matmul(ycat, W["w_out"], "nn", "proj_out", res=h1)
    xn3 = rmsnorm_fwd(h2, norm_ffn2, "norm_ffn2")
    gate2 = riding(matmul, xn3, W["w2_gate"], "nt", "ffn2_gate", group=["w2_up"])
    up2, act2 = riding(ffn_up, xn3, W["w2_up"], gate2, "ffn2_up", group=["w2_down"])
    h3 = riding(matmul, act2, W["w2_down"], "nn", "ffn2_down", res=h2, scale=0.5, group=["w_ple_gate"])
    xn4 = rmsnorm_fwd(h3, norm_ple, "norm_ple")
    pg_lin = matmul(xn4, W["w_ple_gate"], "nn", "ple_gate")
    pp = matmul(ps, W["w_ple_proj"], "nn", "ple_proj")
    h4 = ple_fwd(h3, pg_lin, pp, "ple_fwd")
    dh4, loss_part, g_norm_final = final_loss(h4, tgt, norm_final.reshape(1, -1), "final_loss")
    loss = lax.psum(loss_part[0, 0], ("x", "y", "c"))

    G = {}
    reduced = {}
    chip_part = {}
    wait_sibling, wait_chips = [], []
    core = lax.axis_index("c").astype(jnp.int32).reshape(1)

    def grad(name_, value):
        G[name_] = value
        wait_sibling.append(name_)

    def carry(fn, *a, levels="ab", extra=None, **kw):
        tasks, kinds = [], []
        if extra is not None:
            tasks.append(extra[0])
            kinds.append(("x", extra[1]))
        if "a" in levels and wait_sibling:
            group = list(wait_sibling)
            wait_sibling.clear()
            tasks.append(to_sibling_task([G[k] for k in group], [big[k] for k in group]))
            kinds.append(("a", group))
        if "b" in levels and wait_chips:
            group = list(wait_chips)
            wait_chips.clear()
            tasks.append(across_chips_task([chip_part[k] for k in group]))
            kinds.append(("b", group))
        if not tasks:
            return fn(*a, **kw)
        out, task_outs = fn(*a, tasks=tasks, **kw)
        for (kind, group), outs in zip(kinds, task_outs):
            if kind == "x":
                group(outs)
                continue
            for k, r in zip(group, outs):
                if kind == "a":
                    chip_part[k] = rs_chip_sum(G[k], r, big[k], core, "rs_sum_" + k)
                    wait_chips.append(k)
                else:
                    reduced[k] = r
        return out

    small = {}
    small["norm_final"] = g_norm_final
    dpp, dpg = ple_bwd(dh4, pg_lin, pp, "ple_bwd")
    grad("w_ple_proj", matmul(ps, dpp, "tn", "grad_ple_proj", out_dtype=BF16))
    grad("w_ple_gate", carry(matmul, xn4, dpg, "tn", "grad_ple_gate", out_dtype=BF16))
    dxn4 = carry(matmul, dpg, W["w_ple_gate"], "nt", "ple_gate_bwd")
    dh3, dh3b, small["norm_ple"] = rmsnorm_bwd(dxn4, h3, norm_ple, dh4, "norm_ple_bwd")

    def ffn_bwd(tag, dhb, xn, gate, up, act, wg, wu, wd, extra=None):
        dgate, dup = carry(ffn_bwd_act, dhb, W[wd], gate, up, tag + "_act_bwd", extra=extra)
        grad(wd, carry(matmul, act, dhb, "tn", tag + "_grad_down", out_dtype=BF16, scale=0.5))
        grad(wg, carry(matmul, dgate, xn, "tn", tag + "_grad_gate", out_dtype=BF16))
        grad(wu, carry(matmul, dup, xn, "tn", tag + "_grad_up", out_dtype=BF16))
        dxn = carry(matmul, dgate, W[wg], "nn", tag + "_gate_bwd")
        return carry(matmul, dup, W[wu], "nn", tag + "_up_bwd", res=dxn)

    dxn3 = ffn_bwd("ffn2", dh3b, xn3, gate2, up2, act2, "w2_gate", "w2_up", "w2_down")
    dh2, dh2b, small["norm_ffn2"] = rmsnorm_bwd(dxn3, h2, norm_ffn2, dh3, "norm_ffn2_bwd")

    grad("w_out", matmul(ycat, dh2b, "tn", "grad_out", out_dtype=BF16))
    dycat = carry(matmul, dh2b, W["w_out"], "nt", "proj_out_bwd")
    dyg_direct, dglin, dy_gmlp, small["norm_ssm_out"], small["norm_gmlp_out"] = mix_out_bwd(
        dycat, y_pre, glin, y_gmlp, norm_ssm_out, norm_gmlp_out, "mix_out_bwd")
    grad("ssm_w_glu", matmul(yg, dglin, "tn", "grad_glu", out_dtype=BF16))
    dyg = carry(matmul, dglin, W["ssm_w_glu"], "nt", "ssm_glu_bwd", res=dyg_direct, levels="a")
    du, small["ssm_d"], gc_r, gc_i, gb_r, gb_i, ga_r, ga_i = carry(
        s5_bwd, dyg, y_pre, z, sr, si, bc_r, bc_i, cc_r, cc_i, apw_b, ssm_d, "s5_bwd")
    dzu, dzv, small["gmlp_norm_v"], g_wm, g_bs = gmlp_bwd(dy_gmlp, z, gmlp_norm_v, wm, wmt, bs, "gmlp_bwd")
    small["gmlp_w_s"] = g_wm
    small["gmlp_b_s"] = g_bs
    small["c_re"] = _block_diag_extract(gc_r, SSM_GROUP, SSM_STATE)
    small["c_im"] = _block_diag_extract(gc_i, SSM_GROUP, SSM_STATE)
    small["bbar_r"] = jnp.swapaxes(_block_diag_extract(gb_r, SSM_GROUP, SSM_STATE), 1, 2)
    small["bbar_i"] = jnp.swapaxes(_block_diag_extract(gb_i, SSM_GROUP, SSM_STATE), 1, 2)
    small["abar_r"] = jnp.sum(ga_r, axis=0).reshape(n_groups, SSM_STATE)
    small["abar_i"] = jnp.sum(ga_i, axis=0).reshape(n_groups, SSM_STATE)

    dz = jnp.concatenate([du, dzu, dzv], axis=1)
    grad("w_in", matmul(xn2, dz, "tn", "grad_in", out_dtype=BF16))
    dxn2 = carry(matmul, dz, W["w_in"], "nt", "proj_in_bwd")
    dh1, dh1b, small["norm_mix"] = rmsnorm_bwd(dxn2, h1, norm_mix, dh2, "norm_mix_bwd")

    def pack(parts):
        flat = jnp.concatenate([v.reshape(-1) for v in parts.values()])
        return _pad_to(flat, 0, SUBLANE * LANE).reshape(-1, LANE), flat.shape[0]

    def unpack(everyones, n, parts, tag):
        rows = everyones.shape[0] // N_DEV
        summed = sum_devices(everyones.reshape(N_DEV, rows, LANE), "sum_" + tag).reshape(-1)[:n]
        out, off = {}, 0
        for k, v in parts.items():
            out[k] = summed[off:off + v.size].reshape(v.shape)
            off += v.size
        return out

    early = dict(small)
    flat_early, n_early = pack(early)
    landed = []
    dxn1 = ffn_bwd("ffn1", dh1b, xn1, gate1, up1, act1, "w1_gate", "w1_up", "w1_down",
                   extra=(gather_task([flat_early], [0]), landed.extend))
    tot = unpack(landed[0], n_early, early, "small")
    grad_x, _, g_norm_ffn1 = rmsnorm_bwd(dxn1, xs, norm_ffn1, dh1, "norm_ffn1_bwd")
    assert not wait_sibling and not wait_chips and set(reduced) == set(big_names)
    last = {"norm_ffn1": g_norm_ffn1}
    flat_last, n_last = pack(last)
    ((everyones_last,),) = run_tasks([gather_task([flat_last], [0])], "gather_last")
    tot.update(unpack(everyones_last, n_last, last, "last"))

    out_g, out_d, out_m, out_v = {}, {}, {}, {}
    for k in big_names:
        g, dl, nm, nv = adam_sharded(reduced[k], view(weights[k], k), view(moments_m[k], k), view(moments_v[k], k),
                                     "adam_" + k)
        out_g[k], out_d[k], out_m[k], out_v[k] = unview(g, k), unview(dl, k), unview(nm, k), unview(nv, k)

    _, ssm_vjp = jax.vjp(_ssm_discretize, ssm_log_dt[0], ssm_a_re[0], ssm_a_im[0], ssm_b_re[0], ssm_b_im[0])
    g_log_dt, g_a_re, g_a_im, g_b_re, g_b_im = ssm_vjp((tot["abar_r"], tot["abar_i"], tot["bbar_r"], tot["bbar_i"]))
    small_grads = {
        "norm_ffn1": tot["norm_ffn1"], "norm_mix": tot["norm_mix"], "ssm_log_dt": g_log_dt, "ssm_a_re": g_a_re,
        "ssm_a_im": g_a_im, "ssm_b_re": g_b_re, "ssm_b_im": g_b_im, "ssm_c_re": tot["c_re"], "ssm_c_im": tot["c_im"],
        "ssm_d": tot["ssm_d"], "gmlp_norm_v": tot["gmlp_norm_v"],
        "gmlp_w_s": jnp.where(causal[None], tot["gmlp_w_s"], 0.0), "gmlp_b_s": tot["gmlp_b_s"],
        "norm_ssm_out": tot["norm_ssm_out"], "norm_gmlp_out": tot["norm_gmlp_out"], "norm_ffn2": tot["norm_ffn2"],
        "norm_ple": tot["norm_ple"], "norm_final": tot["norm_final"]}
    for k, g in small_grads.items():
        shp = weights[k].shape
        g2 = _as2d(g.reshape(shp))
        dl, nm, nv = adam_small(g2, _as2d(weights[k]), _as2d(moments_m[k]), _as2d(moments_v[k]), "adam_" + k)
        out_g[k], out_d[k], out_m[k], out_v[k] = g2.reshape(shp), dl.reshape(shp), nm.reshape(shp), nv.reshape(shp)

    return (loss, grad_x[None], *[out_g[k] for k in names], *[out_d[k] for k in names],
            *[out_m[k] for k in names], *[out_v[k] for k in names])
```

```python
import math

import jax
import jax.numpy as jnp
from jax import lax
from jax.experimental import pallas as pl
from jax.experimental.pallas import tpu as pltpu

F32 = jnp.float32
BF16 = jnp.bfloat16
MESH_DT = pl.DeviceIdType.MESH

N_DEV = 8
N_CHIP = 4
LANE = 128
SUBLANE = 8
VMEM_LIMIT = 56 * 1024 * 1024

EPS = 1e-6
SSM_GROUP = 16
SSM_STATE = 64
GROUPS_PER_BLOCK = LANE // SSM_GROUP
STATE_BLOCK = GROUPS_PER_BLOCK * SSM_STATE
GMLP_HEAD = 128
CHUNK = 128

ADAM_LR = 0.001
ADAM_B1 = 0.9
ADAM_B2 = 0.999
ADAM_EPS = 1e-08
ADAM_WD = 0.01
ADAM_STEP = 10

GELU_K = math.sqrt(2.0 / math.pi)
GELU_C = 0.044715


def _cparams():
    return pltpu.CompilerParams(vmem_limit_bytes=VMEM_LIMIT)


def _tile(n, pref):
    if n <= pref:
        return n
    t = (pref // LANE) * LANE
    while t > 0:
        if n % t == 0:
            return t
        t -= LANE
    return n


def _row_tile(n, pref):
    if n <= pref:
        return n
    t = (pref // SUBLANE) * SUBLANE
    while t > 0:
        if n % t == 0:
            return t
        t -= SUBLANE
    return n


def _gelu(x):
    t = jnp.tanh(GELU_K * (x + GELU_C * x * x * x))
    return 0.5 * x * (1.0 + t)


def _gelu_grad(x):
    t = jnp.tanh(GELU_K * (x + GELU_C * x * x * x))
    return 0.5 * (1.0 + t) + 0.5 * x * (1.0 - t * t) * (GELU_K * (1.0 + 3.0 * GELU_C * x * x))


def _sigmoid(x):
    return 1.0 / (1.0 + jnp.exp(-x))


_DN = {
    "nn": (((1,), (0,)), ((), ())),
    "nt": (((1,), (1,)), ((), ())),
    "tn": (((0,), (0,)), ((), ())),
}


def _dot(a, b, mode="nn"):
    return lax.dot_general(a, b, _DN[mode], preferred_element_type=F32)


class CommTask:
    def __init__(self, inputs, out_shape, n_sems, start, late, finish):
        self.inputs, self.out_shape, self.n_sems = list(inputs), list(out_shape), n_sems
        self.start, self.late, self.finish = start, late, finish


def _call(body, *, name, grid, in_specs, out_specs, out_shape, args, scratch_shapes=(), tasks=()):
    in_specs, out_specs, out_shape = list(in_specs), list(out_specs), list(out_shape)
    scratch_shapes = list(scratch_shapes)
    if not tasks:
        return pl.pallas_call(
            body, name=name, grid=grid, in_specs=in_specs, out_specs=out_specs, out_shape=out_shape,
            scratch_shapes=scratch_shapes, compiler_params=_cparams())(*args)
    n_in, n_out, n_scr = len(in_specs), len(out_specs), len(scratch_shapes)
    t_in = [len(t.inputs) for t in tasks]
    t_out = [len(t.out_shape) for t in tasks]
    late_step = grid[0] - max(1, grid[0] // 4)
    has_late = grid[0] >= 2

    def carried(*refs):
        pos = n_in
        task_ins = []
        for k in t_in:
            task_ins.append(refs[pos:pos + k])
            pos += k
        outs = refs[pos:pos + n_out]
        pos += n_out
        task_outs = []
        for k in t_out:
            task_outs.append(refs[pos:pos + k])
            pos += k
        scratch = refs[pos:pos + n_scr]
        pos += n_scr
        sems = [refs[pos + 3 * i:pos + 3 * i + 3] for i in range(len(tasks))]
        ids = [pl.program_id(d) for d in range(len(grid))]
        rest_zero = True
        for d in range(1, len(grid)):
            rest_zero = jnp.logical_and(rest_zero, ids[d] == 0)
        first = jnp.logical_and(ids[0] == 0, rest_zero)
        last = ids[0] == grid[0] - 1
        for d in range(1, len(grid)):
            last = jnp.logical_and(last, ids[d] == grid[d] - 1)

        @pl.when(first)
        def _():
            for t, ti, to, s in zip(tasks, task_ins, task_outs, sems):
                t.start(ti, to, *s)

        if has_late:
            @pl.when(jnp.logical_and(ids[0] == late_step, rest_zero))
            def _():
                for t, ti, to, s in zip(tasks, task_ins, task_outs, sems):
                    t.late(ti, to, *s)

        body(*refs[:n_in], *outs, *scratch)

        @pl.when(last)
        def _():
            for t, ti, to, s in zip(tasks, task_ins, task_outs, sems):
                if not has_late:
                    t.late(ti, to, *s)
                t.finish(ti, to, *s)

    any_spec = pl.BlockSpec(memory_space=pl.ANY)
    sem_shapes = [pltpu.SemaphoreType.DMA((n,)) for t in tasks for n in t.n_sems]
    res = pl.pallas_call(
        carried, name=name, grid=grid,
        in_specs=in_specs + [any_spec] * sum(t_in), out_specs=out_specs + [any_spec] * sum(t_out),
        out_shape=out_shape + [s for t in tasks for s in t.out_shape],
        scratch_shapes=scratch_shapes + sem_shapes, compiler_params=_cparams(),
    )(*args, *[a for t in tasks for a in t.inputs])
    res = list(res)
    task_res, pos = [], n_out
    for k in t_out:
        task_res.append(res[pos:pos + k])
        pos += k
    return res[:n_out], task_res


def _mm_dims(a, b, mode):
    if mode == "nn":
        (m, k), (k2, n) = a.shape, b.shape
    elif mode == "nt":
        (m, k), (n, k2) = a.shape, b.shape
    else:
        (k, m), (k2, n) = a.shape, b.shape
    assert k == k2, (a.shape, b.shape, mode)
    return m, n, k


def _mm_specs(mode, tm, tn, tk):
    if mode == "tn":
        a_spec = pl.BlockSpec((tk, tm), lambda i, j, k: (k, i))
    else:
        a_spec = pl.BlockSpec((tm, tk), lambda i, j, k: (i, k))
    if mode == "nt":
        b_spec = pl.BlockSpec((tn, tk), lambda i, j, k: (j, k))
    else:
        b_spec = pl.BlockSpec((tk, tn), lambda i, j, k: (k, j))
    return a_spec, b_spec


def _accumulate(acc, nk, partial, emit):
    if nk == 1:
        emit(partial)
        return
    kk = pl.program_id(2)

    @pl.when(kk == 0)
    def _():
        acc[...] = partial

    @pl.when(kk > 0)
    def _():
        acc[...] += partial

    @pl.when(kk == nk - 1)
    def _():
        emit(acc[...])


def matmul(a, b, mode, name, out_dtype=F32, res=None, scale=1.0, tm=1024, tn=1024, tk=2048, tasks=()):
    m, n, k = _mm_dims(a, b, mode)
    tm, tn, tk = _tile(m, tm), _tile(n, tn), _tile(k, tk)
    nk = k // tk
    a_spec, b_spec = _mm_specs(mode, tm, tn, tk)
    o_spec = pl.BlockSpec((tm, tn), lambda i, j, k: (i, j))
    has_res = res is not None

    def body(*refs):
        if has_res:
            a_ref, b_ref, r_ref, o_ref, acc = refs
        else:
            a_ref, b_ref, o_ref, acc = refs

        def emit(v):
            if scale != 1.0:
                v = v * scale
            if has_res:
                v = r_ref[...] + v
            o_ref[...] = v.astype(out_dtype)

        _accumulate(acc, nk, _dot(a_ref[...], b_ref[...], mode), emit)

    out = _call(
        body, name=name, grid=(m // tm, n // tn, nk),
        in_specs=[a_spec, b_spec] + ([o_spec] if has_res else []), out_specs=[o_spec],
        out_shape=[jax.ShapeDtypeStruct((m, n), out_dtype)], args=(a, b) + ((res,) if has_res else ()),
        scratch_shapes=[pltpu.VMEM((tm, tn) if nk > 1 else (SUBLANE, LANE), F32)], tasks=tasks)
    return (out[0][0], out[1]) if tasks else out[0]


def ffn_up(xn, wu, gate, name, tm=1024, tn=1024, tk=2048, tasks=()):
    m, n, k = _mm_dims(xn, wu, "nt")
    tm, tn, tk = _tile(m, tm), _tile(n, tn), _tile(k, tk)
    nk = k // tk
    a_spec, b_spec = _mm_specs("nt", tm, tn, tk)
    o_spec = pl.BlockSpec((tm, tn), lambda i, j, k: (i, j))

    def body(a_ref, u_ref, gate_ref, up_ref, act_ref, acc):
        def emit(u):
            g = gate_ref[...]
            up_ref[...] = u
            act_ref[...] = (g * _sigmoid(g) * u).astype(BF16)

        _accumulate(acc, nk, _dot(a_ref[...], u_ref[...], "nt"), emit)

    out = _call(
        body, name=name, grid=(m // tm, n // tn, nk), in_specs=[a_spec, b_spec, o_spec],
        out_specs=[o_spec, o_spec],
        out_shape=[jax.ShapeDtypeStruct((m, n), F32), jax.ShapeDtypeStruct((m, n), BF16)],
        args=(xn, wu, gate), scratch_shapes=[pltpu.VMEM((tm, tn) if nk > 1 else (SUBLANE, LANE), F32)], tasks=tasks)
    return (tuple(out[0]), out[1]) if tasks else tuple(out)


def ffn_bwd_act(dh, wd, gate, up, name, tm=1024, tn=512, tk=2048, tasks=()):
    m, n, k = _mm_dims(dh, wd, "nt")
    tm, tn, tk = _tile(m, tm), _tile(n, tn), _tile(k, tk)
    nk = k // tk
    a_spec, b_spec = _mm_specs("nt", tm, tn, tk)
    o_spec = pl.BlockSpec((tm, tn), lambda i, j, k: (i, j))

    def body(a_ref, b_ref, gate_ref, up_ref, dg_ref, du_ref, acc):
        def emit(total):
            dact = 0.5 * total
            g = gate_ref[...]
            sg = _sigmoid(g)
            du_ref[...] = (dact * (g * sg)).astype(BF16)
            dg_ref[...] = (dact * up_ref[...] * (sg * (1.0 + g * (1.0 - sg)))).astype(BF16)

        _accumulate(acc, nk, _dot(a_ref[...], b_ref[...], "nt"), emit)

    out = _call(
        body, name=name, grid=(m // tm, n // tn, nk), in_specs=[a_spec, b_spec, o_spec, o_spec],
        out_specs=[o_spec, o_spec],
        out_shape=[jax.ShapeDtypeStruct((m, n), BF16), jax.ShapeDtypeStruct((m, n), BF16)],
        args=(dh, wd, gate, up), scratch_shapes=[pltpu.VMEM((tm, tn) if nk > 1 else (SUBLANE, LANE), F32)],
        tasks=tasks)
    return (tuple(out[0]), out[1]) if tasks else tuple(out)


def _rows(t, d, tr):
    return pl.BlockSpec((tr, d), lambda i: (i, 0))


def _vec(d):
    return pl.BlockSpec((1, d), lambda i: (0, 0))


def rmsnorm_fwd(x, g, name, tr=512):
    t, d = x.shape
    tr = _row_tile(t, tr)

    def body(x_ref, g_ref, o_ref):
        xf = x_ref[...]
        r = lax.rsqrt(jnp.mean(xf * xf, axis=-1, keepdims=True) + EPS)
        o_ref[...] = (xf * r * g_ref[...]).astype(BF16)

    return pl.pallas_call(
        body, name=name, grid=(t // tr,), in_specs=[_rows(t, d, tr), _vec(d)], out_specs=_rows(t, d, tr),
        out_shape=jax.ShapeDtypeStruct((t, d), BF16), compiler_params=_cparams(),
    )(x, g)


def _rms_bwd(dxn, xf, g):
    r = lax.rsqrt(jnp.mean(xf * xf, axis=-1, keepdims=True) + EPS)
    xhat = xf * r
    dg = jnp.sum(dxn * xhat, axis=0, keepdims=True)
    dxh = dxn * g
    dx = r * (dxh - xhat * jnp.mean(dxh * xhat, axis=-1, keepdims=True))
    return dx, dg


def rmsnorm_bwd(dxn, x, g, dres, name, tr=256):
    t, d = x.shape
    tr = _row_tile(t, tr)

    def body(dxn_ref, x_ref, g_ref, dres_ref, o_ref, ob_ref, dg_ref):
        dx, dg = _rms_bwd(dxn_ref[...], x_ref[...], g_ref[...])
        out = dres_ref[...] + dx
        o_ref[...] = out
        ob_ref[...] = out.astype(BF16)

        @pl.when(pl.program_id(0) == 0)
        def _():
            dg_ref[...] = jnp.zeros_like(dg_ref)

        dg_ref[...] += dg

    return pl.pallas_call(
        body, name=name, grid=(t // tr,),
        in_specs=[_rows(t, d, tr), _rows(t, d, tr), _vec(d), _rows(t, d, tr)],
        out_specs=[_rows(t, d, tr), _rows(t, d, tr), _vec(d)],
        out_shape=[jax.ShapeDtypeStruct((t, d), F32), jax.ShapeDtypeStruct((t, d), BF16),
                   jax.ShapeDtypeStruct((1, d), F32)],
        compiler_params=_cparams(),
    )(dxn, x, g, dres)


def final_loss(h, target, g, name, tr=256):
    t, d = h.shape
    tr = _row_tile(t, tr)

    def body(h_ref, t_ref, g_ref, dh_ref, loss_ref, dg_ref):
        xf = h_ref[...]
        gg = g_ref[...]
        r = lax.rsqrt(jnp.mean(xf * xf, axis=-1, keepdims=True) + EPS)
        xhat = xf * r
        e = xhat * gg - t_ref[...]
        part = jnp.sum(jnp.sum(e * e, axis=1, keepdims=True), axis=0, keepdims=True) * (0.5 / d)
        dout = e * (1.0 / d)
        dg = jnp.sum(dout * xhat, axis=0, keepdims=True)
        dxh = dout * gg
        dh_ref[...] = r * (dxh - xhat * jnp.mean(dxh * xhat, axis=-1, keepdims=True))

        @pl.when(pl.program_id(0) == 0)
        def _():
            dg_ref[...] = jnp.zeros_like(dg_ref)
            loss_ref[...] = jnp.zeros_like(loss_ref)

        dg_ref[...] += dg
        loss_ref[...] += jnp.broadcast_to(part, loss_ref.shape)

    return pl.pallas_call(
        body, name=name, grid=(t // tr,),
        in_specs=[_rows(t, d, tr), _rows(t, d, tr), _vec(d)],
        out_specs=[_rows(t, d, tr), pl.BlockSpec((SUBLANE, LANE), lambda i: (0, 0)), _vec(d)],
        out_shape=[jax.ShapeDtypeStruct((t, d), F32), jax.ShapeDtypeStruct((SUBLANE, LANE), F32),
                   jax.ShapeDtypeStruct((1, d), F32)],
        compiler_params=_cparams(),
    )(h, target, g)


def ple_fwd(h, glin, pp, name, tr=512):
    t, d = h.shape
    tr = _row_tile(t, tr)

    def body(h_ref, gl_ref, pp_ref, o_ref):
        o_ref[...] = h_ref[...] + _sigmoid(gl_ref[...]) * pp_ref[...]

    sp = _rows(t, d, tr)
    return pl.pallas_call(
        body, name=name, grid=(t // tr,), in_specs=[sp, sp, sp], out_specs=sp,
        out_shape=jax.ShapeDtypeStruct((t, d), F32), compiler_params=_cparams(),
    )(h, glin, pp)


def ple_bwd(dh, glin, pp, name, tr=512):
    t, d = dh.shape
    tr = _row_tile(t, tr)

    def body(dh_ref, gl_ref, pp_ref, dpp_ref, dgl_ref):
        gate = _sigmoid(gl_ref[...])
        dh_ = dh_ref[...]
        dpp_ref[...] = (dh_ * gate).astype(BF16)
        dgl_ref[...] = (dh_ * pp_ref[...] * gate * (1.0 - gate)).astype(BF16)

    sp = _rows(t, d, tr)
    return pl.pallas_call(
        body, name=name, grid=(t // tr,), in_specs=[sp, sp, sp], out_specs=[sp, sp],
        out_shape=[jax.ShapeDtypeStruct((t, d), BF16), jax.ShapeDtypeStruct((t, d), BF16)],
        compiler_params=_cparams(),
    )(dh, glin, pp)


def mix_out_fwd(y_pre, glin, y_gmlp, g_so, g_go, name, tr=512):
    t, d = y_pre.shape
    tr = _row_tile(t, tr)

    def body(yp_ref, gl_ref, yg_ref, gs_ref, gg_ref, o_ref):
        ys = _gelu(yp_ref[...]) * _sigmoid(gl_ref[...])
        r = lax.rsqrt(jnp.mean(ys * ys, axis=-1, keepdims=True) + EPS)
        o_ref[:, 0:d] = (ys * r * gs_ref[...]).astype(BF16)
        yq = yg_ref[...]
        r2 = lax.rsqrt(jnp.mean(yq * yq, axis=-1, keepdims=True) + EPS)
        o_ref[:, d:2 * d] = (yq * r2 * gg_ref[...]).astype(BF16)

    sp = _rows(t, d, tr)
    return pl.pallas_call(
        body, name=name, grid=(t // tr,), in_specs=[sp, sp, sp, _vec(d), _vec(d)],
        out_specs=_rows(t, 2 * d, tr), out_shape=jax.ShapeDtypeStruct((t, 2 * d), BF16),
        compiler_params=_cparams(),
    )(y_pre, glin, y_gmlp, g_so, g_go)


def mix_out_bwd(dycat, y_pre, glin, y_gmlp, g_so, g_go, name, tr=256):
    t, d = y_pre.shape
    tr = _row_tile(t, tr)

    def body(dy_ref, yp_ref, gl_ref, yg_ref, gs_ref, gg_ref, dyg_ref, dl_ref, dyq_ref, dgs_ref, dgg_ref):
        yg = _gelu(yp_ref[...])
        sg = _sigmoid(gl_ref[...])
        dys, dgs = _rms_bwd(dy_ref[:, 0:d], yg * sg, gs_ref[...])
        dyg_ref[...] = dys * sg
        dl_ref[...] = (dys * yg * sg * (1.0 - sg)).astype(BF16)
        dyq, dgg = _rms_bwd(dy_ref[:, d:2 * d], yg_ref[...], gg_ref[...])
        dyq_ref[...] = dyq

        @pl.when(pl.program_id(0) == 0)
        def _():
            dgs_ref[...] = jnp.zeros_like(dgs_ref)
            dgg_ref[...] = jnp.zeros_like(dgg_ref)

        dgs_ref[...] += dgs
        dgg_ref[...] += dgg

    sp = _rows(t, d, tr)
    return pl.pallas_call(
        body, name=name, grid=(t // tr,),
        in_specs=[_rows(t, 2 * d, tr), sp, sp, sp, _vec(d), _vec(d)],
        out_specs=[sp, sp, sp, _vec(d), _vec(d)],
        out_shape=[jax.ShapeDtypeStruct((t, d), F32), jax.ShapeDtypeStruct((t, d), BF16),
                   jax.ShapeDtypeStruct((t, d), F32), jax.ShapeDtypeStruct((1, d), F32),
                   jax.ShapeDtypeStruct((1, d), F32)],
        compiler_params=_cparams(),
    )(dycat, y_pre, glin, y_gmlp, g_so, g_go)


SCAN_COLS = 512


def _scan_tile(xr, xi, const, cr, ci, reverse):
    for lvl, sh in enumerate((1, 2, 4)):
        ar, ai = const(2 * lvl), const(2 * lvl + 1)
        s = (SUBLANE - sh) if reverse else sh
        rr = pltpu.roll(xr, s, 0)
        ri = pltpu.roll(xi, s, 0)
        xr, xi = xr + ar * rr - ai * ri, xi + ar * ri + ai * rr
    pr, pi_ = const(6), const(7)
    xr, xi = xr + pr * cr - pi_ * ci, xi + pr * ci + pi_ * cr
    return xr, xi


def _bcast_row(x, row):
    return jnp.broadcast_to(x[row:row + 1, :], x.shape)


def s5_fwd(z, bc_r, bc_i, cc_r, cc_i, apw, dvec, name, tc=256, tasks=()):
    t = z.shape[0]
    nblk = bc_r.shape[0]
    d = nblk * LANE
    ns = nblk * STATE_BLOCK
    tc = _row_tile(t, tc)
    ntile = tc // SUBLANE

    def body(z_ref, br_ref, bi_ref, cr_ref, ci_ref, apw_ref, d_ref, y_ref, yg_ref, sr_ref, si_ref, carry):
        @pl.when(pl.program_id(0) == 0)
        def _():
            carry[...] = jnp.zeros_like(carry)

        for j in range(nblk):
            uj = z_ref[:, j * LANE:(j + 1) * LANE]
            ub = uj.astype(BF16)
            for q in range(STATE_BLOCK // SCAN_COLS):
                c0 = j * STATE_BLOCK + q * SCAN_COLS
                cs = pl.ds(c0, SCAN_COLS)
                bs = slice(q * SCAN_COLS, (q + 1) * SCAN_COLS)
                sr_ref[:, cs] = _dot(ub, br_ref[j, :, bs])
                si_ref[:, cs] = _dot(ub, bi_ref[j, :, bs])
                const = lambda k, cs=cs: apw_ref[k, :, cs]

                def tile(k, c, cs=cs, const=const):
                    rows = pl.ds(pl.multiple_of(k * SUBLANE, SUBLANE), SUBLANE)
                    xr, xi = _scan_tile(sr_ref[rows, cs], si_ref[rows, cs], const, c[0], c[1], False)
                    sr_ref[rows, cs] = xr
                    si_ref[rows, cs] = xi
                    return _bcast_row(xr, SUBLANE - 1), _bcast_row(xi, SUBLANE - 1)

                c_r, c_i = lax.fori_loop(0, ntile, tile, (carry[0, :, cs], carry[1, :, cs]))
                carry[0, :, cs] = c_r
                carry[1, :, cs] = c_i
            sb = pl.ds(j * STATE_BLOCK, STATE_BLOCK)
            y = (_dot(sr_ref[:, sb].astype(BF16), cr_ref[j]) - _dot(si_ref[:, sb].astype(BF16), ci_ref[j])
                 + d_ref[:, j * LANE:(j + 1) * LANE] * uj)
            y_ref[:, j * LANE:(j + 1) * LANE] = y
            yg_ref[:, j * LANE:(j + 1) * LANE] = _gelu(y).astype(BF16)

    full3 = lambda shp: pl.BlockSpec(shp, lambda i: (0, 0, 0))
    out = _call(
        body, name=name, grid=(t // tc,),
        in_specs=[pl.BlockSpec((tc, d), lambda i: (i, 0)), full3(bc_r.shape), full3(bc_i.shape),
                  full3(cc_r.shape), full3(cc_i.shape), full3(apw.shape), _vec(d)],
        out_specs=[pl.BlockSpec((tc, d), lambda i: (i, 0)), pl.BlockSpec((tc, d), lambda i: (i, 0)),
                   pl.BlockSpec((tc, ns), lambda i: (i, 0)), pl.BlockSpec((tc, ns), lambda i: (i, 0))],
        out_shape=[jax.ShapeDtypeStruct((t, d), F32), jax.ShapeDtypeStruct((t, d), BF16),
                   jax.ShapeDtypeStruct((t, ns), F32), jax.ShapeDtypeStruct((t, ns), F32)],
        args=(z, bc_r, bc_i, cc_r, cc_i, apw, dvec), scratch_shapes=[pltpu.VMEM((2, SUBLANE, ns), F32)], tasks=tasks)
    return (tuple(out[0]), out[1]) if tasks else tuple(out)


def s5_bwd(dyg, y_pre, z, sr, si, bc_r, bc_i, cc_r, cc_i, apw_rev, dvec, name, tc=128, tasks=()):
    t = z.shape[0]
    nblk = bc_r.shape[0]
    d = nblk * LANE
    ns = nblk * STATE_BLOCK
    tc = _row_tile(t, tc)
    ntile = tc // SUBLANE
    nchunk = t // tc
    tiles_per_chunk = tc // SUBLANE

    def body(dyg_ref, yp_ref, z_ref, sr_ref, si_ref, pr_ref, pi_ref, br_ref, bi_ref, cr_ref, ci_ref, apw_ref,
             d_ref, du_ref, gd_ref, gcr_ref, gci_ref, gbr_ref, gbi_ref, gar_ref, gai_ref, lr_ref, li_ref, carry):
        step = pl.program_id(0)

        @pl.when(step == 0)
        def _():
            carry[...] = jnp.zeros_like(carry)
            for ref in (gd_ref, gcr_ref, gci_ref, gbr_ref, gbi_ref, gar_ref, gai_ref):
                ref[...] = jnp.zeros_like(ref)

        first_chunk = (step == nchunk - 1).astype(F32)
        keep_prev = 1.0 - first_chunk
        row0 = lax.broadcasted_iota(jnp.int32, (SUBLANE, SCAN_COLS), 0) == 0

        for j in range(nblk):
            lanes = slice(j * LANE, (j + 1) * LANE)
            uj = z_ref[:, lanes]
            ub = uj.astype(BF16)
            gy = dyg_ref[:, lanes] * _gelu_grad(yp_ref[:, lanes])
            gyb = gy.astype(BF16)
            gd_ref[:, lanes] += jnp.sum(gy * uj, axis=0, keepdims=True)
            for q in range(STATE_BLOCK // SCAN_COLS):
                c0 = j * STATE_BLOCK + q * SCAN_COLS
                cs = pl.ds(c0, SCAN_COLS)
                bs = slice(q * SCAN_COLS, (q + 1) * SCAN_COLS)
                lr_ref[:, cs] = _dot(gyb, cr_ref[j, bs, :], "nt")
                li_ref[:, cs] = -_dot(gyb, ci_ref[j, bs, :], "nt")
                const = lambda k, cs=cs: apw_ref[k, :, cs]

                def one_tile(rows, prev_r, prev_i, c, cs=cs, const=const):
                    cr_, ci_, gar, gai = c
                    xr, xi = _scan_tile(lr_ref[rows, cs], li_ref[rows, cs], const, cr_, ci_, True)
                    lr_ref[rows, cs] = xr
                    li_ref[rows, cs] = xi
                    spr = jnp.where(row0, prev_r, pltpu.roll(sr_ref[rows, cs], 1, 0))
                    spi = jnp.where(row0, prev_i, pltpu.roll(si_ref[rows, cs], 1, 0))
                    gar = gar + xr * spr + xi * spi
                    gai = gai + xi * spr - xr * spi
                    return _bcast_row(xr, 0), _bcast_row(xi, 0), gar, gai

                def tile(k, c, cs=cs, one_tile=one_tile):
                    kk = ntile - 1 - k
                    rows = pl.ds(pl.multiple_of(kk * SUBLANE, SUBLANE), SUBLANE)
                    prow = pl.ds(pl.multiple_of((kk - 1) * SUBLANE, SUBLANE), SUBLANE)
                    prev_r = _bcast_row(sr_ref[prow, cs], SUBLANE - 1)
                    prev_i = _bcast_row(si_ref[prow, cs], SUBLANE - 1)
                    return one_tile(rows, prev_r, prev_i, c)

                zero = jnp.zeros((SUBLANE, SCAN_COLS), F32)
                c = lax.fori_loop(0, ntile - 1, tile, (carry[0, :, cs], carry[1, :, cs], zero, zero))
                prev_r = _bcast_row(pr_ref[:, cs], SUBLANE - 1) * keep_prev
                prev_i = _bcast_row(pi_ref[:, cs], SUBLANE - 1) * keep_prev
                c_r, c_i, gar, gai = one_tile(pl.ds(0, SUBLANE), prev_r, prev_i, c)
                carry[0, :, cs] = c_r
                carry[1, :, cs] = c_i
                gar_ref[:, cs] += gar
                gai_ref[:, cs] += gai
            sb = pl.ds(j * STATE_BLOCK, STATE_BLOCK)
            lrb = lr_ref[:, sb].astype(BF16)
            lib = li_ref[:, sb].astype(BF16)
            gcr_ref[j] += _dot(gyb, sr_ref[:, sb].astype(BF16), "tn")
            gci_ref[j] -= _dot(gyb, si_ref[:, sb].astype(BF16), "tn")
            gbr_ref[j] += _dot(ub, lrb, "tn")
            gbi_ref[j] += _dot(ub, lib, "tn")
            du = _dot(lrb, br_ref[j], "nt") + _dot(lib, bi_ref[j], "nt") + gy * d_ref[:, lanes]
            du_ref[:, lanes] = du.astype(BF16)

    rev = lambda i: (nchunk - 1 - i, 0)
    prev = lambda i: (jnp.maximum((nchunk - 1 - i) * tiles_per_chunk - 1, 0), 0)
    full3 = lambda shp: pl.BlockSpec(shp, lambda i: (0, 0, 0))
    acc3 = pl.BlockSpec((nblk, LANE, STATE_BLOCK), lambda i: (0, 0, 0))
    acc_rows = pl.BlockSpec((SUBLANE, ns), lambda i: (0, 0))
    out = _call(
        body, name=name, grid=(nchunk,),
        in_specs=[pl.BlockSpec((tc, d), rev), pl.BlockSpec((tc, d), rev), pl.BlockSpec((tc, d), rev),
                  pl.BlockSpec((tc, ns), rev), pl.BlockSpec((tc, ns), rev),
                  pl.BlockSpec((SUBLANE, ns), prev), pl.BlockSpec((SUBLANE, ns), prev),
                  full3(bc_r.shape), full3(bc_i.shape), full3(cc_r.shape), full3(cc_i.shape), full3(apw_rev.shape),
                  _vec(d)],
        out_specs=[pl.BlockSpec((tc, d), rev), _vec(d), acc3, acc3, acc3, acc3, acc_rows, acc_rows],
        out_shape=[jax.ShapeDtypeStruct((t, d), BF16), jax.ShapeDtypeStruct((1, d), F32)]
        + [jax.ShapeDtypeStruct((nblk, LANE, STATE_BLOCK), F32)] * 4
        + [jax.ShapeDtypeStruct((SUBLANE, ns), F32)] * 2,
        args=(dyg, y_pre, z, sr, si, sr, si, bc_r, bc_i, cc_r, cc_i, apw_rev, dvec),
        scratch_shapes=[pltpu.VMEM((tc, ns), F32), pltpu.VMEM((tc, ns), F32), pltpu.VMEM((2, SUBLANE, ns), F32)],
        tasks=tasks)
    return (tuple(out[0]), out[1]) if tasks else tuple(out)


def _cmul(a, b):
    return a[0] * b[0] - a[1] * b[1], a[0] * b[1] + a[1] * b[0]


def _scan_constants(abar_r, abar_i, reverse):
    ar = abar_r.reshape(1, -1)
    ai = abar_i.reshape(1, -1)
    if reverse:
        ai = -ai
    pw = [(ar, ai)]
    for _ in range(SUBLANE - 1):
        pw.append(_cmul(pw[-1], (ar, ai)))
    rows = lax.broadcasted_iota(jnp.int32, (SUBLANE, 1), 0)
    out = []
    for sh in (1, 2, 4):
        keep = (rows <= SUBLANE - 1 - sh) if reverse else (rows >= sh)
        for part in pw[sh - 1]:
            out.append(jnp.where(keep, part, 0.0))
    for comp in (0, 1):
        stack = jnp.concatenate([pw[k][comp] for k in range(SUBLANE)], axis=0)
        out.append(stack[::-1] if reverse else stack)
    return jnp.stack(out, axis=0).astype(F32)


def _ssm_discretize(log_dt, a_re, a_im, b_re, b_im):
    dt = jnp.exp(log_dt)[:, None]
    lr = jnp.minimum(a_re, -1e-4)
    li = a_im
    mag = jnp.exp(lr * dt)
    ang = li * dt
    abar_r = mag * jnp.cos(ang)
    abar_i = mag * jnp.sin(ang)
    den = lr * lr + li * li
    xr = abar_r - 1.0
    xi = abar_i
    zr = (xr * lr + xi * li) / den
    zi = (xi * lr - xr * li) / den
    bbar_r = zr[..., None] * b_re - zi[..., None] * b_im
    bbar_i = zr[..., None] * b_im + zi[..., None] * b_re
    return abar_r, abar_i, bbar_r, bbar_i


def _block_diag(w):
    g, a, b = w.shape
    nb = g // GROUPS_PER_BLOCK
    eye = jnp.eye(GROUPS_PER_BLOCK, dtype=w.dtype)
    w5 = w.reshape(nb, GROUPS_PER_BLOCK, a, b)
    out = w5[:, :, :, None, :] * eye[None, :, None, :, None]
    return out.reshape(nb, GROUPS_PER_BLOCK * a, GROUPS_PER_BLOCK * b)


def _block_diag_extract(m, a, b):
    nb = m.shape[0]
    eye = jnp.eye(GROUPS_PER_BLOCK, dtype=m.dtype)
    m5 = m.reshape(nb, GROUPS_PER_BLOCK, a, GROUPS_PER_BLOCK, b)
    out = jnp.sum(m5 * eye[None, :, None, :, None], axis=3)
    return out.reshape(nb * GROUPS_PER_BLOCK, a, b)


def _layer_norm(gv, nv):
    mu = jnp.mean(gv, axis=-1, keepdims=True)
    xc = gv - mu
    r = lax.rsqrt(jnp.mean(xc * xc, axis=-1, keepdims=True) + EPS)
    xhat = xc * r
    return xhat * nv, xhat, r


def gmlp_fwd(z, norm_v, wm, bs, name, tr=256):
    t = z.shape[0]
    nh = wm.shape[0]
    d = nh * GMLP_HEAD
    col0 = (z.shape[1] - 2 * d) // d
    tr = _row_tile(t, tr)

    def body(zu_ref, zv_ref, nv_ref, wm_ref, bs_ref, o_ref):
        v, _, _ = _layer_norm(_gelu(zv_ref[...]), nv_ref[...])
        vb = v.astype(BF16)
        u = _gelu(zu_ref[...])
        for c in range(tr // CHUNK):
            rows = slice(c * CHUNK, (c + 1) * CHUNK)
            for h in range(nh):
                cols = slice(h * GMLP_HEAD, (h + 1) * GMLP_HEAD)
                s = _dot(wm_ref[h], vb[rows, cols]) + bs_ref[h]
                o_ref[rows, cols] = u[rows, cols] * s

    return pl.pallas_call(
        body, name=name, grid=(t // tr,),
        in_specs=[pl.BlockSpec((tr, d), lambda i: (i, col0)), pl.BlockSpec((tr, d), lambda i: (i, col0 + 1)),
                  _vec(d), pl.BlockSpec(wm.shape, lambda i: (0, 0, 0)), pl.BlockSpec(bs.shape, lambda i: (0, 0, 0))],
        out_specs=pl.BlockSpec((tr, d), lambda i: (i, 0)),
        out_shape=jax.ShapeDtypeStruct((t, d), F32), compiler_params=_cparams(),
    )(z, z, norm_v, wm, bs)


def gmlp_bwd(dy, z, norm_v, wm, wmt, bs, name, tr=256):
    t = z.shape[0]
    nh = wm.shape[0]
    d = nh * GMLP_HEAD
    col0 = (z.shape[1] - 2 * d) // d
    tr = _row_tile(t, tr)

    def body(dy_ref, zu_ref, zv_ref, nv_ref, wm_ref, wmt_ref, bs_ref, dzu_ref, dzv_ref, dnv_ref, dwm_ref, dbs_ref,
             dv_ref):
        @pl.when(pl.program_id(0) == 0)
        def _():
            dnv_ref[...] = jnp.zeros_like(dnv_ref)
            dwm_ref[...] = jnp.zeros_like(dwm_ref)
            dbs_ref[...] = jnp.zeros_like(dbs_ref)

        zv = zv_ref[...]
        nv = nv_ref[...]
        v, xhat, r = _layer_norm(_gelu(zv), nv)
        vb = v.astype(BF16)
        zu = zu_ref[...]
        u = _gelu(zu)
        dy_ = dy_ref[...]
        for c in range(tr // CHUNK):
            rows = slice(c * CHUNK, (c + 1) * CHUNK)
            for h in range(nh):
                cols = slice(h * GMLP_HEAD, (h + 1) * GMLP_HEAD)
                vh = vb[rows, cols]
                s = _dot(wm_ref[h], vh) + bs_ref[h]
                dyh = dy_[rows, cols]
                dzu_ref[rows, cols] = (dyh * s * _gelu_grad(zu[rows, cols])).astype(BF16)
                ds = dyh * u[rows, cols]
                dsb = ds.astype(BF16)
                dbs_ref[h] += jnp.sum(ds, axis=1, keepdims=True)
                dwm_ref[h] += _dot(dsb, vh, "nt")
                dv_ref[rows, cols] = _dot(wmt_ref[h], dsb)
        dv = dv_ref[...]
        dnv_ref[...] += jnp.sum(dv * xhat, axis=0, keepdims=True)
        dxh = dv * nv
        dgv = r * (dxh - jnp.mean(dxh, axis=-1, keepdims=True) - xhat * jnp.mean(dxh * xhat, axis=-1, keepdims=True))
        dzv_ref[...] = (dgv * _gelu_grad(zv)).astype(BF16)

    full3 = lambda shp: pl.BlockSpec(shp, lambda i: (0, 0, 0))
    rows_d = pl.BlockSpec((tr, d), lambda i: (i, 0))
    return pl.pallas_call(
        body, name=name, grid=(t // tr,),
        in_specs=[rows_d, pl.BlockSpec((tr, d), lambda i: (i, col0)), pl.BlockSpec((tr, d), lambda i: (i, col0 + 1)),
                  _vec(d), full3(wm.shape), full3(wmt.shape), full3(bs.shape)],
        out_specs=[rows_d, rows_d, _vec(d), full3((nh, CHUNK, CHUNK)), full3((nh, CHUNK, 1))],
        out_shape=[jax.ShapeDtypeStruct((t, d), BF16), jax.ShapeDtypeStruct((t, d), BF16),
                   jax.ShapeDtypeStruct((1, d), F32), jax.ShapeDtypeStruct((nh, CHUNK, CHUNK), F32),
                   jax.ShapeDtypeStruct((nh, CHUNK, 1), F32)],
        scratch_shapes=[pltpu.VMEM((tr, d), F32)], compiler_params=_cparams(),
    )(dy, z, z, norm_v, wm, wmt, bs)


def _block(ref, axis, size, k):
    start = pl.multiple_of(k * size, size)
    if axis == 0:
        return ref.at[pl.ds(start, size), :]
    return ref.at[:, pl.ds(start, size)]


def _place():
    x, y, c = lax.axis_index("x"), lax.axis_index("y"), lax.axis_index("c")
    chips = [(1 - x, y), (x, 1 - y), (1 - x, 1 - y)]
    return x, y, c, chips


def _dev(x, y, c):
    return 4 * x + 2 * y + c


def gather_task(shards, axes):
    n = len(shards)
    sizes = [s.shape[ax] for s, ax in zip(shards, axes)]
    out_shape = [
        jax.ShapeDtypeStruct((s.shape[0] * N_DEV, s.shape[1]) if ax == 0 else (s.shape[0], s.shape[1] * N_DEV), s.dtype)
        for s, ax in zip(shards, axes)
    ]

    def copy(ins, outs, send_sems, recv_sems, t, k, block, to, from_input=False):
        dst = _block(outs[t], axes[t], sizes[t], _dev(*block))
        return pltpu.make_async_remote_copy(
            src_ref=ins[t] if from_input else dst, dst_ref=dst,
            send_sem=send_sems.at[t * 7 + k], recv_sem=recv_sems.at[t * 7 + k],
            device_id=to, device_id_type=MESH_DT)

    def local(ins, outs, local_sems, t, me):
        return pltpu.make_async_copy(ins[t], _block(outs[t], axes[t], sizes[t], _dev(*me)), local_sems.at[t])

    def start(ins, outs, send_sems, recv_sems, local_sems):
        x, y, c, chips = _place()
        me, sibling = (x, y, c), (x, y, 1 - c)
        for t in range(n):
            local(ins, outs, local_sems, t, me).start()
        for t in range(n):
            copy(ins, outs, send_sems, recv_sems, t, 0, me, sibling, True).start()
            for j, chip in enumerate(chips):
                copy(ins, outs, send_sems, recv_sems, t, 1 + j, me, (*chip, c), True).start()

    def late(ins, outs, send_sems, recv_sems, local_sems):
        x, y, c, chips = _place()
        me, sibling = (x, y, c), (x, y, 1 - c)
        for t in range(n):
            for j, chip in enumerate(chips):
                copy(ins, outs, send_sems, recv_sems, t, 1 + j, (*chip, c), me).wait_recv()
                copy(ins, outs, send_sems, recv_sems, t, 4 + j, (*chip, c), sibling).start()

    def finish(ins, outs, send_sems, recv_sems, local_sems):
        x, y, c, chips = _place()
        me, sibling = (x, y, c), (x, y, 1 - c)
        for t in range(n):
            copy(ins, outs, send_sems, recv_sems, t, 0, sibling, me).wait_recv()
            for j, chip in enumerate(chips):
                copy(ins, outs, send_sems, recv_sems, t, 4 + j, (*chip, 1 - c), me).wait_recv()
        for t in range(n):
            copy(ins, outs, send_sems, recv_sems, t, 0, me, sibling, True).wait_send()
            for j, chip in enumerate(chips):
                copy(ins, outs, send_sems, recv_sems, t, 1 + j, me, (*chip, c), True).wait_send()
                copy(ins, outs, send_sems, recv_sems, t, 4 + j, (*chip, c), sibling).wait_send()
            local(ins, outs, local_sems, t, me).wait()

    return CommTask(shards, out_shape, (7 * n, 7 * n, n), start, late, finish)


def _blk3(shape2, axis):
    r, c = shape2
    return (r // N_DEV, c) if axis == 0 else (r, c // N_DEV)


def _no_late(ins, outs, send_sems, recv_sems, local_sems):
    pass


def to_sibling_task(grads, axes):
    n = len(grads)
    blks = [_blk3(g.shape, ax) for g, ax in zip(grads, axes)]
    sizes = [b[ax] for b, ax in zip(blks, axes)]

    def copies(ins, outs, send_sems, recv_sems):
        x, y, c, _ = _place()
        return [pltpu.make_async_remote_copy(
            src_ref=_block(ins[t], axes[t], sizes[t], 2 * i + (1 - c)), dst_ref=outs[t].at[i],
            send_sem=send_sems.at[t * N_CHIP + i], recv_sem=recv_sems.at[t * N_CHIP + i],
            device_id=(x, y, 1 - c), device_id_type=MESH_DT) for t in range(n) for i in range(N_CHIP)]

    def start(ins, outs, send_sems, recv_sems, local_sems):
        for cp in copies(ins, outs, send_sems, recv_sems):
            cp.start()

    def finish(ins, outs, send_sems, recv_sems, local_sems):
        cps = copies(ins, outs, send_sems, recv_sems)
        for cp in cps:
            cp.wait_recv()
        for cp in cps:
            cp.wait_send()

    out_shape = [jax.ShapeDtypeStruct((N_CHIP,) + b, g.dtype) for b, g in zip(blks, grads)]
    return CommTask(grads, out_shape, (N_CHIP * n, N_CHIP * n, 1), start, _no_late, finish)


def across_chips_task(parts):
    n = len(parts)

    def copies(ins, outs, send_sems, recv_sems):
        x, y, c, chips = _place()
        my_chip = 2 * x + y
        return [pltpu.make_async_remote_copy(
            src_ref=ins[t].at[2 * chip[0] + chip[1]], dst_ref=outs[t].at[my_chip],
            send_sem=send_sems.at[t * 3 + j], recv_sem=recv_sems.at[t * 3 + j],
            device_id=(*chip, c), device_id_type=MESH_DT) for t in range(n) for j, chip in enumerate(chips)]

    def mine(ins, outs, local_sems):
        x, y, _, _ = _place()
        my_chip = 2 * x + y
        return [pltpu.make_async_copy(ins[t].at[my_chip], outs[t].at[my_chip], local_sems.at[t]) for t in range(n)]

    def start(ins, outs, send_sems, recv_sems, local_sems):
        for cp in mine(ins, outs, local_sems):
            cp.start()
        for cp in copies(ins, outs, send_sems, recv_sems):
            cp.start()

    def finish(ins, outs, send_sems, recv_sems, local_sems):
        cps = copies(ins, outs, send_sems, recv_sems)
        for cp in cps:
            cp.wait_recv()
        for cp in cps:
            cp.wait_send()
        for cp in mine(ins, outs, local_sems):
            cp.wait()

    out_shape = [jax.ShapeDtypeStruct(p.shape, p.dtype) for p in parts]
    return CommTask(parts, out_shape, (3 * n, 3 * n, n), start, _no_late, finish)


def run_tasks(tasks, name):
    t_in = [len(t.inputs) for t in tasks]
    t_out = [len(t.out_shape) for t in tasks]

    def body(*refs):
        pos, views = 0, []
        for k in t_in:
            views.append([refs[pos:pos + k]])
            pos += k
        for v, k in zip(views, t_out):
            v.append(refs[pos:pos + k])
            pos += k
        for i, v in enumerate(views):
            v.extend(refs[pos + 3 * i:pos + 3 * i + 3])
        for phase in ("start", "late", "finish"):
            for t, v in zip(tasks, views):
                getattr(t, phase)(*v)

    any_spec = pl.BlockSpec(memory_space=pl.ANY)
    res = pl.pallas_call(
        body, name=name, in_specs=[any_spec] * sum(t_in), out_specs=[any_spec] * sum(t_out),
        out_shape=[s for t in tasks for s in t.out_shape],
        scratch_shapes=[pltpu.SemaphoreType.DMA((k,)) for t in tasks for k in t.n_sems],
    )(*[a for t in tasks for a in t.inputs])
    res, out, pos = list(res), [], 0
    for k in t_out:
        out.append(res[pos:pos + k])
        pos += k
    return out


_HBM_SPEC = pl.BlockSpec(memory_space=pltpu.HBM)
_SEM_SPEC = pl.BlockSpec(memory_space=pltpu.SEMAPHORE)
_DATAFLOW = pltpu.SideEffectType.DATAFLOW_SIDE_EFFECTING


def _full_shape(s, ax):
    return (s.shape[0] * N_DEV, s.shape[1]) if ax == 0 else (s.shape[0], s.shape[1] * N_DEV)


def _level1_copy(src, landing, axis, size, send_sems, recv_sems, slot, sender, to):
    dst = _block(landing, axis, size, _dev(*sender))
    return pltpu.make_async_remote_copy(src_ref=src, dst_ref=dst, send_sem=send_sems.at[slot],
                                        recv_sem=recv_sems.at[slot], device_id=to, device_id_type=MESH_DT)


def gather_start(shards, axes, groups, name):
    n = len(shards)
    sizes = [s.shape[ax] for s, ax in zip(shards, axes)]

    def body(*refs):
        srcs, lands = refs[:n], refs[n:2 * n]
        sems = refs[4 * n:]
        x, y, c, chips = _place()
        me = (x, y, c)
        targets = [(x, y, 1 - c)] + [(*chip, c) for chip in chips]
        for g, members in enumerate(groups):
            for m, t in enumerate(members):
                for k, to in enumerate(targets):
                    _level1_copy(srcs[t], lands[t], axes[t], sizes[t], sems[2 * g], sems[2 * g + 1], 4 * m + k,
                                 me, to).start()

    landing = [pltpu.with_memory_space_constraint(lax.empty(_full_shape(s, ax), s.dtype), pltpu.HBM)
               for s, ax in zip(shards, axes)]
    out = pl.pallas_call(
        body, name=name,
        out_shape=[pltpu.HBM(s.shape, s.dtype) for s in shards] + [pltpu.HBM(b.shape, b.dtype) for b in landing]
        + [pltpu.SemaphoreType.DMA((4 * len(members),)) for members in groups for _ in (0, 1)],
        in_specs=[_HBM_SPEC] * (2 * n), out_specs=[_HBM_SPEC] * (2 * n) + [_SEM_SPEC] * (2 * len(groups)),
        input_output_aliases={i: i for i in range(2 * n)},
        compiler_params=pltpu.CompilerParams(has_side_effects=_DATAFLOW),
    )(*[pltpu.with_memory_space_constraint(s, pltpu.HBM) for s in shards], *landing)
    out = list(out)
    sems = out[2 * n:]
    return out[:n], out[n:2 * n], [(sems[2 * g], sems[2 * g + 1]) for g in range(len(groups))]


def gather_wait(shards, landing, axes, send_sems, recv_sems, after, name):
    n = len(shards)
    sizes = [s.shape[ax] for s, ax in zip(shards, axes)]

    def body(*refs):
        srcs, lands = refs[:n], refs[n:2 * n]
        send, recv = refs[2 * n], refs[2 * n + 1]
        x, y, c, chips = _place()
        me = (x, y, c)
        peers = [(x, y, 1 - c)] + [(*chip, c) for chip in chips]
        for t in range(n):
            for k, peer in enumerate(peers):
                _level1_copy(srcs[t], lands[t], axes[t], sizes[t], send, recv, 4 * t + k, me, peer).wait_send()
                _level1_copy(srcs[t], lands[t], axes[t], sizes[t], send, recv, 4 * t + k, peer, me).wait_recv()

    out = pl.pallas_call(
        body, name=name,
        out_shape=[pltpu.HBM(s.shape, s.dtype) for s in shards] + [pltpu.HBM(b.shape, b.dtype) for b in landing],
        in_specs=[_HBM_SPEC] * (2 * n) + [_SEM_SPEC, _SEM_SPEC, pl.BlockSpec(memory_space=pl.ANY)],
        out_specs=[_HBM_SPEC] * (2 * n), input_output_aliases={i: i for i in range(2 * n)},
        compiler_params=pltpu.CompilerParams(has_side_effects=_DATAFLOW),
    )(*shards, *landing, send_sems, recv_sems, after)
    out = list(out)
    return out[:n], out[n:]


def gather_forward(shards, landing, axes, name):
    n = len(shards)
    sizes = [s.shape[ax] for s, ax in zip(shards, axes)]

    def body(*refs):
        srcs, lands = refs[:n], refs[2 * n:3 * n]
        send_sems, recv_sems, local_sems = refs[3 * n:]
        x, y, c, chips = _place()
        sibling = (x, y, 1 - c)

        def forward(t, j, chip_core):
            blk = _block(lands[t], axes[t], sizes[t], _dev(*chip_core))
            return pltpu.make_async_remote_copy(src_ref=blk, dst_ref=blk, send_sem=send_sems.at[3 * t + j],
                                                recv_sem=recv_sems.at[3 * t + j], device_id=sibling,
                                                device_id_type=MESH_DT)

        mine = [pltpu.make_async_copy(srcs[t], _block(lands[t], axes[t], sizes[t], _dev(x, y, c)), local_sems.at[t])
                for t in range(n)]
        sends = [forward(t, j, (*chip, c)) for t in range(n) for j, chip in enumerate(chips)]
        for cp in mine + sends:
            cp.start()
        for t in range(n):
            for j, chip in enumerate(chips):
                forward(t, j, (*chip, 1 - c)).wait_recv()
        for cp in sends:
            cp.wait_send()
        for cp in mine:
            cp.wait()

    any_spec = pl.BlockSpec(memory_space=pl.ANY)
    out = pl.pallas_call(
        body, name=name, in_specs=[any_spec] * (2 * n), out_specs=[any_spec] * n,
        out_shape=[jax.ShapeDtypeStruct(b.shape, b.dtype) for b in landing],
        input_output_aliases={n + i: i for i in range(n)},
        scratch_shapes=[pltpu.SemaphoreType.DMA((3 * n,)), pltpu.SemaphoreType.DMA((3 * n,)),
                        pltpu.SemaphoreType.DMA((n,))],
    )(*shards, *landing)
    return list(out)


def rs_chip_sum(grad, recv, axis, core, name, tr=512):
    br, bc = _blk3(grad.shape, axis)
    tr = _row_tile(br, tr)
    nrb = br // tr

    if axis == 0:
        g_map = lambda i, r, c_ref: ((2 * i + c_ref[0]) * nrb + r, 0)
    else:
        g_map = lambda i, r, c_ref: (r, 2 * i + c_ref[0])

    def body(c_ref, g_ref, r_ref, o_ref):
        o_ref[...] = (g_ref[...].astype(F32) + r_ref[...].astype(F32)).astype(BF16)

    return pl.pallas_call(
        body, name=name,
        grid_spec=pltpu.PrefetchScalarGridSpec(
            num_scalar_prefetch=1, grid=(N_CHIP, nrb),
            in_specs=[pl.BlockSpec((tr, bc), g_map), pl.BlockSpec((None, tr, bc), lambda i, r, c_ref: (i, r, 0))],
            out_specs=pl.BlockSpec((None, tr, bc), lambda i, r, c_ref: (i, r, 0))),
        out_shape=jax.ShapeDtypeStruct((N_CHIP, br, bc), BF16), compiler_params=_cparams(),
    )(core, grad, recv)


def _adamw(w, g, m, v):
    m = ADAM_B1 * m + (1.0 - ADAM_B1) * g
    v = ADAM_B2 * v + (1.0 - ADAM_B2) * (g * g)
    m_hat = m / (1.0 - ADAM_B1 ** ADAM_STEP)
    v_hat = v / (1.0 - ADAM_B2 ** ADAM_STEP)
    delta = -ADAM_LR * (m_hat / (jnp.sqrt(v_hat) + ADAM_EPS) + ADAM_WD * w)
    return delta, m, v


def _sum_chips(p_ref):
    g = p_ref[0].astype(F32)
    for i in range(1, N_CHIP):
        g = g + p_ref[i].astype(F32)
    return g


def adam_sharded(parts, w, m, v, name, tr=256):
    r, c = w.shape
    assert parts.shape[2] == c
    tr = _row_tile(r, tr)

    def body(p_ref, w_ref, m_ref, v_ref, g_ref, d_ref, nm_ref, nv_ref):
        g = _sum_chips(p_ref)
        delta, nm, nv = _adamw(w_ref[...], g, m_ref[...], v_ref[...])
        g_ref[...] = g
        d_ref[...] = delta
        nm_ref[...] = nm
        nv_ref[...] = nv

    sp = pl.BlockSpec((tr, c), lambda i: (i, 0))
    return pl.pallas_call(
        body, name=name, grid=(r // tr,),
        in_specs=[pl.BlockSpec((N_CHIP, tr, c), lambda i: (0, i, 0)), sp, sp, sp],
        out_specs=[sp, sp, sp, sp], out_shape=[jax.ShapeDtypeStruct((r, c), F32)] * 4,
        compiler_params=_cparams(),
    )(parts, w, m, v)


def adam_small(g, w, m, v, name):
    def body(g_ref, w_ref, m_ref, v_ref, d_ref, nm_ref, nv_ref):
        delta, nm, nv = _adamw(w_ref[...], g_ref[...], m_ref[...], v_ref[...])
        d_ref[...] = delta
        nm_ref[...] = nm
        nv_ref[...] = nv

    return pl.pallas_call(
        body, name=name, out_shape=[jax.ShapeDtypeStruct(w.shape, F32)] * 3, compiler_params=_cparams(),
    )(g, w, m, v)


def sum_devices(gathered, name, tr=512):
    _, r, c = gathered.shape
    tr = _row_tile(r, tr)

    def body(x_ref, o_ref):
        s = x_ref[0]
        for k in range(1, N_DEV):
            s = s + x_ref[k]
        o_ref[...] = s

    return pl.pallas_call(
        body, name=name, grid=(r // tr,), in_specs=[pl.BlockSpec((N_DEV, tr, c), lambda i: (0, i, 0))],
        out_specs=pl.BlockSpec((tr, c), lambda i: (i, 0)), out_shape=jax.ShapeDtypeStruct((r, c), F32),
        compiler_params=_cparams(),
    )(gathered)


def _pad_to(a, axis, mult):
    size = a.shape[axis]
    pad = (-size) % mult
    if pad == 0:
        return a
    cfg = [(0, 0)] * a.ndim
    cfg[axis] = (0, pad)
    return jnp.pad(a, cfg)


def _as2d(a):
    if a.ndim == 1:
        return a.reshape(1, -1)
    return a.reshape(-1, a.shape[-1])


def kernel(x, p, norm_ffn1, w1_gate, w1_up, w1_down, norm_mix, w_in, ssm_log_dt, ssm_a_re, ssm_a_im, ssm_b_re, ssm_b_im, ssm_c_re, ssm_c_im, ssm_d, ssm_w_glu, gmlp_norm_v, gmlp_w_s, gmlp_b_s, norm_ssm_out, norm_gmlp_out, w_out, norm_ffn2, w2_gate, w2_up, w2_down, norm_ple, w_ple_gate, w_ple_proj, norm_final, loss_target, m_norm_ffn1, m_w1_gate, m_w1_up, m_w1_down, m_norm_mix, m_w_in, m_ssm_log_dt, m_ssm_a_re, m_ssm_a_im, m_ssm_b_re, m_ssm_b_im, m_ssm_c_re, m_ssm_c_im, m_ssm_d, m_ssm_w_glu, m_gmlp_norm_v, m_gmlp_w_s, m_gmlp_b_s, m_norm_ssm_out, m_norm_gmlp_out, m_w_out, m_norm_ffn2, m_w2_gate, m_w2_up, m_w2_down, m_norm_ple, m_w_ple_gate, m_w_ple_proj, m_norm_final, v_norm_ffn1, v_w1_gate, v_w1_up, v_w1_down, v_norm_mix, v_w_in, v_ssm_log_dt, v_ssm_a_re, v_ssm_a_im, v_ssm_b_re, v_ssm_b_im, v_ssm_c_re, v_ssm_c_im, v_ssm_d, v_ssm_w_glu, v_gmlp_norm_v, v_gmlp_w_s, v_gmlp_b_s, v_norm_ssm_out, v_norm_gmlp_out, v_w_out, v_norm_ffn2, v_w2_gate, v_w2_up, v_w2_down, v_norm_ple, v_w_ple_gate, v_w_ple_proj, v_norm_final):
    weights = dict(
        norm_ffn1=norm_ffn1, w1_gate=w1_gate, w1_up=w1_up, w1_down=w1_down, norm_mix=norm_mix, w_in=w_in,
        ssm_log_dt=ssm_log_dt, ssm_a_re=ssm_a_re, ssm_a_im=ssm_a_im, ssm_b_re=ssm_b_re, ssm_b_im=ssm_b_im,
        ssm_c_re=ssm_c_re, ssm_c_im=ssm_c_im, ssm_d=ssm_d, ssm_w_glu=ssm_w_glu, gmlp_norm_v=gmlp_norm_v,
        gmlp_w_s=gmlp_w_s, gmlp_b_s=gmlp_b_s, norm_ssm_out=norm_ssm_out, norm_gmlp_out=norm_gmlp_out, w_out=w_out,
        norm_ffn2=norm_ffn2, w2_gate=w2_gate, w2_up=w2_up, w2_down=w2_down, norm_ple=norm_ple,
        w_ple_gate=w_ple_gate, w_ple_proj=w_ple_proj, norm_final=norm_final)
    moments_m = dict(
        norm_ffn1=m_norm_ffn1, w1_gate=m_w1_gate, w1_up=m_w1_up, w1_down=m_w1_down, norm_mix=m_norm_mix, w_in=m_w_in,
        ssm_log_dt=m_ssm_log_dt, ssm_a_re=m_ssm_a_re, ssm_a_im=m_ssm_a_im, ssm_b_re=m_ssm_b_re, ssm_b_im=m_ssm_b_im,
        ssm_c_re=m_ssm_c_re, ssm_c_im=m_ssm_c_im, ssm_d=m_ssm_d, ssm_w_glu=m_ssm_w_glu, gmlp_norm_v=m_gmlp_norm_v,
        gmlp_w_s=m_gmlp_w_s, gmlp_b_s=m_gmlp_b_s, norm_ssm_out=m_norm_ssm_out, norm_gmlp_out=m_norm_gmlp_out,
        w_out=m_w_out, norm_ffn2=m_norm_ffn2, w2_gate=m_w2_gate, w2_up=m_w2_up, w2_down=m_w2_down,
        norm_ple=m_norm_ple, w_ple_gate=m_w_ple_gate, w_ple_proj=m_w_ple_proj, norm_final=m_norm_final)
    moments_v = dict(
        norm_ffn1=v_norm_ffn1, w1_gate=v_w1_gate, w1_up=v_w1_up, w1_down=v_w1_down, norm_mix=v_norm_mix, w_in=v_w_in,
        ssm_log_dt=v_ssm_log_dt, ssm_a_re=v_ssm_a_re, ssm_a_im=v_ssm_a_im, ssm_b_re=v_ssm_b_re, ssm_b_im=v_ssm_b_im,
        ssm_c_re=v_ssm_c_re, ssm_c_im=v_ssm_c_im, ssm_d=v_ssm_d, ssm_w_glu=v_ssm_w_glu, gmlp_norm_v=v_gmlp_norm_v,
        gmlp_w_s=v_gmlp_w_s, gmlp_b_s=v_gmlp_b_s, norm_ssm_out=v_norm_ssm_out, norm_gmlp_out=v_norm_gmlp_out,
        w_out=v_w_out, norm_ffn2=v_norm_ffn2, w2_gate=v_w2_gate, w2_up=v_w2_up, w2_down=v_w2_down,
        norm_ple=v_norm_ple, w_ple_gate=v_w_ple_gate, w_ple_proj=v_w_ple_proj, norm_final=v_norm_final)
    names = list(weights)

    xs = x[0]
    ps = p[0, 0].astype(BF16)
    tgt = loss_target[0]
    d_model = xs.shape[1]
    d_ssm = d_model // 2
    n_groups = d_ssm // SSM_GROUP

    transposed = ("w1_gate", "w1_up", "w2_gate", "w2_up")
    big = {
        "w1_gate": 0, "w1_up": 0, "w1_down": 0, "w_in": 1, "ssm_w_glu": 0, "w_out": 0,
        "w2_gate": 0, "w2_up": 0, "w2_down": 0, "w_ple_gate": 0, "w_ple_proj": 1}
    big_names = list(big)

    def view(a, k):
        return a[0].T if k in transposed else a[0]

    def unview(a, k):
        return a.T[None] if k in transposed else a[None]

    shard = {k: _pad_to(view(weights[k], k).astype(BF16), big[k], LANE) for k in big_names}
    W = {}

    abar_r, abar_i, bbar_r, bbar_i = _ssm_discretize(ssm_log_dt[0], ssm_a_re[0], ssm_a_im[0], ssm_b_re[0], ssm_b_im[0])
    bc_r = _block_diag(jnp.swapaxes(bbar_r, 1, 2)).astype(BF16)
    bc_i = _block_diag(jnp.swapaxes(bbar_i, 1, 2)).astype(BF16)
    cc_r = _block_diag(jnp.swapaxes(ssm_c_re[0], 1, 2)).astype(BF16)
    cc_i = _block_diag(jnp.swapaxes(ssm_c_im[0], 1, 2)).astype(BF16)
    apw_f = _scan_constants(abar_r, abar_i, False)
    apw_b = _scan_constants(abar_r, abar_i, True)
    causal = jnp.tril(jnp.ones((CHUNK, CHUNK), dtype=bool))
    wm = jnp.where(causal[None], gmlp_w_s[0], 0.0).astype(BF16)
    wmt = jnp.swapaxes(wm, 1, 2)
    bs = gmlp_b_s[0][:, :, None]

    groups = [["w1_gate"], ["w1_up"], ["w1_down"], ["w_in", "ssm_w_glu", "w_out"], ["w2_gate"], ["w2_up"],
              ["w2_down", "w_ple_gate", "w_ple_proj"]]
    order = [k for g in groups for k in g]
    place = {k: i for i, k in enumerate(order)}
    in_flight, landing, sems = gather_start(
        [shard[k] for k in order], [big[k] for k in order], [[place[k] for k in g] for g in groups], "gather_start")

    def arrive(g, after):
        members = [place[k] for k in groups[g]]
        axes_g = [big[k] for k in groups[g]]
        mine, landed = gather_wait([in_flight[i] for i in members], [landing[i] for i in members], axes_g,
                                   *sems[g], after, "gather_wait_%d" % g)
        W.update(zip(groups[g], gather_forward(mine, landed, axes_g, "gather_forward_%d" % g)))

    xn1 = rmsnorm_fwd(xs, norm_ffn1, "norm_ffn1")
    arrive(0, xn1)
    gate1 = matmul(xn1, W["w1_gate"], "nt", "ffn1_gate")
    arrive(1, gate1)
    up1, act1 = ffn_up(xn1, W["w1_up"], gate1, "ffn1_up")
    arrive(2, act1)
    h1 = matmul(act1, W["w1_down"], "nn", "ffn1_down", res=xs, scale=0.5)
    arrive(3, h1)
    xn2 = rmsnorm_fwd(h1, norm_mix, "norm_mix")
    z = matmul(xn2, W["w_in"], "nn", "proj_in")
    y_pre, yg, sr, si = s5_fwd(z, bc_r, bc_i, cc_r, cc_i, apw_f, ssm_d, "s5_fwd")
    glin = matmul(yg, W["ssm_w_glu"], "nn", "ssm_glu")
    y_gmlp = gmlp_fwd(z, gmlp_norm_v, wm, bs, "gmlp_fwd")
    ycat = mix_out_fwd(y_pre, glin, y_gmlp, norm_ssm_out, norm_gmlp_out, "mix_out")
    h2 = matmul(ycat, W["w_out"], "nn", "proj_out", res=h1)
    arrive(4, h2)
    xn3 = rmsnorm_fwd(h2, norm_ffn2, "norm_ffn2")
    gate2 = matmul(xn3, W["w2_gate"], "nt", "ffn2_gate")
    arrive(5, gate2)
    up2, act2 = ffn_up(xn3, W["w2_up"], gate2, "ffn2_up")
    arrive(6, act2)
    h3 = matmul(act2, W["w2_down"], "nn", "ffn2_down", res=h2, scale=0.5)
    xn4 = rmsnorm_fwd(h3, norm_ple, "norm_ple")
    pg_lin = matmul(xn4, W["w_ple_gate"], "nn", "ple_gate")
    pp = matmul(ps, W["w_ple_proj"], "nn", "ple_proj")
    h4 = ple_fwd(h3, pg_lin, pp, "ple_fwd")
    dh4, loss_part, g_norm_final = final_loss(h4, tgt, norm_final.reshape(1, -1), "final_loss")
    loss = lax.psum(loss_part[0, 0], ("x", "y", "c"))

    G = {}
    reduced = {}
    chip_part = {}
    wait_sibling, wait_chips = [], []
    core = lax.axis_index("c").astype(jnp.int32).reshape(1)

    def grad(name_, value):
        G[name_] = value
        wait_sibling.append(name_)

    def carry(fn, *a, levels="ab", extra=None, **kw):
        tasks, kinds = [], []
        if extra is not None:
            tasks.append(extra[0])
            kinds.append(("x", extra[1]))
        if "a" in levels and wait_sibling:
            group = list(wait_sibling)
            wait_sibling.clear()
            tasks.append(to_sibling_task([G[k] for k in group], [big[k] for k in group]))
            kinds.append(("a", group))
        if "b" in levels and wait_chips:
            group = list(wait_chips)
            wait_chips.clear()
            tasks.append(across_chips_task([chip_part[k] for k in group]))
            kinds.append(("b", group))
        if not tasks:
            return fn(*a, **kw)
        out, task_outs = fn(*a, tasks=tasks, **kw)
        for (kind, group), outs in zip(kinds, task_outs):
            if kind == "x":
                group(outs)
                continue
            for k, r in zip(group, outs):
                if kind == "a":
                    chip_part[k] = rs_chip_sum(G[k], r, big[k], core, "rs_sum_" + k)
                    wait_chips.append(k)
                else:
                    reduced[k] = r
        return out

    small = {}
    small["norm_final"] = g_norm_final
    dpp, dpg = ple_bwd(dh4, pg_lin, pp, "ple_bwd")
    grad("w_ple_proj", matmul(ps, dpp, "tn", "grad_ple_proj", out_dtype=BF16))
    grad("w_ple_gate", carry(matmul, xn4, dpg, "tn", "grad_ple_gate", out_dtype=BF16))
    dxn4 = carry(matmul, dpg, W["w_ple_gate"], "nt", "ple_gate_bwd")
    dh3, dh3b, small["norm_ple"] = rmsnorm_bwd(dxn4, h3, norm_ple, dh4, "norm_ple_bwd")

    def ffn_bwd(tag, dhb, xn, gate, up, act, wg, wu, wd, extra=None):
        dgate, dup = carry(ffn_bwd_act, dhb, W[wd], gate, up, tag + "_act_bwd", extra=extra)
        grad(wd, carry(matmul, act, dhb, "tn", tag + "_grad_down", out_dtype=BF16, scale=0.5))
        grad(wg, carry(matmul, dgate, xn, "tn", tag + "_grad_gate", out_dtype=BF16))
        grad(wu, carry(matmul, dup, xn, "tn", tag + "_grad_up", out_dtype=BF16))
        dxn = carry(matmul, dgate, W[wg], "nn", tag + "_gate_bwd")
        return carry(matmul, dup, W[wu], "nn", tag + "_up_bwd", res=dxn)

    dxn3 = ffn_bwd("ffn2", dh3b, xn3, gate2, up2, act2, "w2_gate", "w2_up", "w2_down")
    dh2, dh2b, small["norm_ffn2"] = rmsnorm_bwd(dxn3, h2, norm_ffn2, dh3, "norm_ffn2_bwd")

    grad("w_out", matmul(ycat, dh2b, "tn", "grad_out", out_dtype=BF16))
    dycat = carry(matmul, dh2b, W["w_out"], "nt", "proj_out_bwd")
    dyg_direct, dglin, dy_gmlp, small["norm_ssm_out"], small["norm_gmlp_out"] = mix_out_bwd(
        dycat, y_pre, glin, y_gmlp, norm_ssm_out, norm_gmlp_out, "mix_out_bwd")
    grad("ssm_w_glu", matmul(yg, dglin, "tn", "grad_glu", out_dtype=BF16))
    dyg = carry(matmul, dglin, W["ssm_w_glu"], "nt", "ssm_glu_bwd", res=dyg_direct, levels="a")
    du, small["ssm_d"], gc_r, gc_i, gb_r, gb_i, ga_r, ga_i = carry(
        s5_bwd, dyg, y_pre, z, sr, si, bc_r, bc_i, cc_r, cc_i, apw_b, ssm_d, "s5_bwd")
    dzu, dzv, small["gmlp_norm_v"], g_wm, g_bs = gmlp_bwd(dy_gmlp, z, gmlp_norm_v, wm, wmt, bs, "gmlp_bwd")
    small["gmlp_w_s"] = g_wm
    small["gmlp_b_s"] = g_bs
    small["c_re"] = _block_diag_extract(gc_r, SSM_GROUP, SSM_STATE)
    small["c_im"] = _block_diag_extract(gc_i, SSM_GROUP, SSM_STATE)
    small["bbar_r"] = jnp.swapaxes(_block_diag_extract(gb_r, SSM_GROUP, SSM_STATE), 1, 2)
    small["bbar_i"] = jnp.swapaxes(_block_diag_extract(gb_i, SSM_GROUP, SSM_STATE), 1, 2)
    small["abar_r"] = jnp.sum(ga_r, axis=0).reshape(n_groups, SSM_STATE)
    small["abar_i"] = jnp.sum(ga_i, axis=0).reshape(n_groups, SSM_STATE)

    dz = jnp.concatenate([du, dzu, dzv], axis=1)
    grad("w_in", matmul(xn2, dz, "tn", "grad_in", out_dtype=BF16))
    dxn2 = carry(matmul, dz, W["w_in"], "nt", "proj_in_bwd")
    dh1, dh1b, small["norm_mix"] = rmsnorm_bwd(dxn2, h1, norm_mix, dh2, "norm_mix_bwd")

    def pack(parts):
        flat = jnp.concatenate([v.reshape(-1) for v in parts.values()])
        return _pad_to(flat, 0, SUBLANE * LANE).reshape(-1, LANE), flat.shape[0]

    def unpack(everyones, n, parts, tag):
        rows = everyones.shape[0] // N_DEV
        summed = sum_devices(everyones.reshape(N_DEV, rows, LANE), "sum_" + tag).reshape(-1)[:n]
        out, off = {}, 0
        for k, v in parts.items():
            out[k] = summed[off:off + v.size].reshape(v.shape)
            off += v.size
        return out

    early = dict(small)
    flat_early, n_early = pack(early)
    landed = []
    dxn1 = ffn_bwd("ffn1", dh1b, xn1, gate1, up1, act1, "w1_gate", "w1_up", "w1_down",
                   extra=(gather_task([flat_early], [0]), landed.extend))
    tot = unpack(landed[0], n_early, early, "small")
    grad_x, _, g_norm_ffn1 = rmsnorm_bwd(dxn1, xs, norm_ffn1, dh1, "norm_ffn1_bwd")
    assert not wait_sibling and not wait_chips and set(reduced) == set(big_names)
    last = {"norm_ffn1": g_norm_ffn1}
    flat_last, n_last = pack(last)
    ((everyones_last,),) = run_tasks([gather_task([flat_last], [0])], "gather_last")
    tot.update(unpack(everyones_last, n_last, last, "last"))

    out_g, out_d, out_m, out_v = {}, {}, {}, {}
    for k in big_names:
        g, dl, nm, nv = adam_sharded(reduced[k], view(weights[k], k), view(moments_m[k], k), view(moments_v[k], k),
                                     "adam_" + k)
        out_g[k], out_d[k], out_m[k], out_v[k] = unview(g, k), unview(dl, k), unview(nm, k), unview(nv, k)

    _, ssm_vjp = jax.vjp(_ssm_discretize, ssm_log_dt[0], ssm_a_re[0], ssm_a_im[0], ssm_b_re[0], ssm_b_im[0])
    g_log_dt, g_a_re, g_a_im, g_b_re, g_b_im = ssm_vjp((tot["abar_r"], tot["abar_i"], tot["bbar_r"], tot["bbar_i"]))
    small_grads = {
        "norm_ffn1": tot["norm_ffn1"], "norm_mix": tot["norm_mix"], "ssm_log_dt": g_log_dt, "ssm_a_re": g_a_re,
        "ssm_a_im": g_a_im, "ssm_b_re": g_b_re, "ssm_b_im": g_b_im, "ssm_c_re": tot["c_re"], "ssm_c_im": tot["c_im"],
        "ssm_d": tot["ssm_d"], "gmlp_norm_v": tot["gmlp_norm_v"],
        "gmlp_w_s": jnp.where(causal[None], tot["gmlp_w_s"], 0.0), "gmlp_b_s": tot["gmlp_b_s"],
        "norm_ssm_out": tot["norm_ssm_out"], "norm_gmlp_out": tot["norm_gmlp_out"], "norm_ffn2": tot["norm_ffn2"],
        "norm_ple": tot["norm_ple"], "norm_final": tot["norm_final"]}
    for k, g in small_grads.items():
        shp = weights[k].shape
        g2 = _as2d(g.reshape(shp))
        dl, nm, nv = adam_small(g2, _as2d(weights[k]), _as2d(moments_m[k]), _as2d(moments_v[k]), "adam_" + k)
        out_g[k], out_d[k], out_m[k], out_v[k] = g2.reshape(shp), dl.reshape(shp), nm.reshape(shp), nv.reshape(shp)

    return (loss, grad_x[None], *[out_g[k] for k in names], *[out_d[k] for k in names],
            *[out_m[k] for k in names], *[out_v[k] for k in names])
```

```python
import math

import jax
import jax.numpy as jnp
from jax import lax
from jax.experimental import pallas as pl
from jax.experimental.pallas import tpu as pltpu

F32 = jnp.float32
BF16 = jnp.bfloat16
MESH_DT = pl.DeviceIdType.MESH

N_DEV = 8
N_CHIP = 4
LANE = 128
SUBLANE = 8
VMEM_LIMIT = 56 * 1024 * 1024

EPS = 1e-6
SSM_GROUP = 16
SSM_STATE = 64
GROUPS_PER_BLOCK = LANE // SSM_GROUP
STATE_BLOCK = GROUPS_PER_BLOCK * SSM_STATE
GMLP_HEAD = 128
CHUNK = 128

ADAM_LR = 0.001
ADAM_B1 = 0.9
ADAM_B2 = 0.999
ADAM_EPS = 1e-08
ADAM_WD = 0.01
ADAM_STEP = 10

GELU_K = math.sqrt(2.0 / math.pi)
GELU_C = 0.044715


def _cparams():
    return pltpu.CompilerParams(vmem_limit_bytes=VMEM_LIMIT)


def _tile(n, pref):
    if n <= pref:
        return n
    t = (pref // LANE) * LANE
    while t > 0:
        if n % t == 0:
            return t
        t -= LANE
    return n


def _row_tile(n, pref):
    if n <= pref:
        return n
    t = (pref // SUBLANE) * SUBLANE
    while t > 0:
        if n % t == 0:
            return t
        t -= SUBLANE
    return n


def _gelu(x):
    t = jnp.tanh(GELU_K * (x + GELU_C * x * x * x))
    return 0.5 * x * (1.0 + t)


def _gelu_grad(x):
    t = jnp.tanh(GELU_K * (x + GELU_C * x * x * x))
    return 0.5 * (1.0 + t) + 0.5 * x * (1.0 - t * t) * (GELU_K * (1.0 + 3.0 * GELU_C * x * x))


def _sigmoid(x):
    return 1.0 / (1.0 + jnp.exp(-x))


_DN = {
    "nn": (((1,), (0,)), ((), ())),
    "nt": (((1,), (1,)), ((), ())),
    "tn": (((0,), (0,)), ((), ())),
}


def _dot(a, b, mode="nn"):
    return lax.dot_general(a, b, _DN[mode], preferred_element_type=F32)


class CommTask:
    def __init__(self, inputs, out_shape, n_sems, start, late, finish):
        self.inputs, self.out_shape, self.n_sems = list(inputs), list(out_shape), n_sems
        self.start, self.late, self.finish = start, late, finish


def _call(body, *, name, grid, in_specs, out_specs, out_shape, args, scratch_shapes=(), tasks=()):
    in_specs, out_specs, out_shape = list(in_specs), list(out_specs), list(out_shape)
    scratch_shapes = list(scratch_shapes)
    if not tasks:
        return pl.pallas_call(
            body, name=name, grid=grid, in_specs=in_specs, out_specs=out_specs, out_shape=out_shape,
            scratch_shapes=scratch_shapes, compiler_params=_cparams())(*args)
    n_in, n_out, n_scr = len(in_specs), len(out_specs), len(scratch_shapes)
    t_in = [len(t.inputs) for t in tasks]
    t_out = [len(t.out_shape) for t in tasks]
    late_step = grid[0] - max(1, grid[0] // 4)
    has_late = grid[0] >= 2

    def carried(*refs):
        pos = n_in
        task_ins = []
        for k in t_in:
            task_ins.append(refs[pos:pos + k])
            pos += k
        outs = refs[pos:pos + n_out]
        pos += n_out
        task_outs = []
        for k in t_out:
            task_outs.append(refs[pos:pos + k])
            pos += k
        scratch = refs[pos:pos + n_scr]
        pos += n_scr
        sems = [refs[pos + 3 * i:pos + 3 * i + 3] for i in range(len(tasks))]
        ids = [pl.program_id(d) for d in range(len(grid))]
        rest_zero = True
        for d in range(1, len(grid)):
            rest_zero = jnp.logical_and(rest_zero, ids[d] == 0)
        first = jnp.logical_and(ids[0] == 0, rest_zero)
        last = ids[0] == grid[0] - 1
        for d in range(1, len(grid)):
            last = jnp.logical_and(last, ids[d] == grid[d] - 1)

        @pl.when(first)
        def _():
            for t, ti, to, s in zip(tasks, task_ins, task_outs, sems):
                t.start(ti, to, *s)

        if has_late:
            @pl.when(jnp.logical_and(ids[0] == late_step, rest_zero))
            def _():
                for t, ti, to, s in zip(tasks, task_ins, task_outs, sems):
                    t.late(ti, to, *s)

        body(*refs[:n_in], *outs, *scratch)

        @pl.when(last)
        def _():
            for t, ti, to, s in zip(tasks, task_ins, task_outs, sems):
                if not has_late:
                    t.late(ti, to, *s)
                t.finish(ti, to, *s)

    any_spec = pl.BlockSpec(memory_space=pl.ANY)
    sem_shapes = [pltpu.SemaphoreType.DMA((n,)) for t in tasks for n in t.n_sems]
    res = pl.pallas_call(
        carried, name=name, grid=grid,
        in_specs=in_specs + [any_spec] * sum(t_in), out_specs=out_specs + [any_spec] * sum(t_out),
        out_shape=out_shape + [s for t in tasks for s in t.out_shape],
        scratch_shapes=scratch_shapes + sem_shapes, compiler_params=_cparams(),
    )(*args, *[a for t in tasks for a in t.inputs])
    res = list(res)
    task_res, pos = [], n_out
    for k in t_out:
        task_res.append(res[pos:pos + k])
        pos += k
    return res[:n_out], task_res


def _mm_dims(a, b, mode):
    if mode == "nn":
        (m, k), (k2, n) = a.shape, b.shape
    elif mode == "nt":
        (m, k), (n, k2) = a.shape, b.shape
    else:
        (k, m), (k2, n) = a.shape, b.shape
    assert k == k2, (a.shape, b.shape, mode)
    return m, n, k


def _mm_specs(mode, tm, tn, tk):
    if mode == "tn":
        a_spec = pl.BlockSpec((tk, tm), lambda i, j, k: (k, i))
    else:
        a_spec = pl.BlockSpec((tm, tk), lambda i, j, k: (i, k))
    if mode == "nt":
        b_spec = pl.BlockSpec((tn, tk), lambda i, j, k: (j, k))
    else:
        b_spec = pl.BlockSpec((tk, tn), lambda i, j, k: (k, j))
    return a_spec, b_spec


def _accumulate(acc, nk, partial, emit):
    if nk == 1:
        emit(partial)
        return
    kk = pl.program_id(2)

    @pl.when(kk == 0)
    def _():
        acc[...] = partial

    @pl.when(kk > 0)
    def _():
        acc[...] += partial

    @pl.when(kk == nk - 1)
    def _():
        emit(acc[...])


def matmul(a, b, mode, name, out_dtype=F32, res=None, scale=1.0, tm=1024, tn=1024, tk=2048, tasks=()):
    m, n, k = _mm_dims(a, b, mode)
    tm, tn, tk = _tile(m, tm), _tile(n, tn), _tile(k, tk)
    nk = k // tk
    a_spec, b_spec = _mm_specs(mode, tm, tn, tk)
    o_spec = pl.BlockSpec((tm, tn), lambda i, j, k: (i, j))
    has_res = res is not None

    def body(*refs):
        if has_res:
            a_ref, b_ref, r_ref, o_ref, acc = refs
        else:
            a_ref, b_ref, o_ref, acc = refs

        def emit(v):
            if scale != 1.0:
                v = v * scale
            if has_res:
                v = r_ref[...] + v
            o_ref[...] = v.astype(out_dtype)

        _accumulate(acc, nk, _dot(a_ref[...], b_ref[...], mode), emit)

    out = _call(
        body, name=name, grid=(m // tm, n // tn, nk),
        in_specs=[a_spec, b_spec] + ([o_spec] if has_res else []), out_specs=[o_spec],
        out_shape=[jax.ShapeDtypeStruct((m, n), out_dtype)], args=(a, b) + ((res,) if has_res else ()),
        scratch_shapes=[pltpu.VMEM((tm, tn) if nk > 1 else (SUBLANE, LANE), F32)], tasks=tasks)
    return (out[0][0], out[1]) if tasks else out[0]


def ffn_up(xn, wu, gate, name, tm=1024, tn=1024, tk=2048, tasks=()):
    m, n, k = _mm_dims(xn, wu, "nt")
    tm, tn, tk = _tile(m, tm), _tile(n, tn), _tile(k, tk)
    nk = k // tk
    a_spec, b_spec = _mm_specs("nt", tm, tn, tk)
    o_spec = pl.BlockSpec((tm, tn), lambda i, j, k: (i, j))

    def body(a_ref, u_ref, gate_ref, up_ref, act_ref, acc):
        def emit(u):
            g = gate_ref[...]
            up_ref[...] = u
            act_ref[...] = (g * _sigmoid(g) * u).astype(BF16)

        _accumulate(acc, nk, _dot(a_ref[...], u_ref[...], "nt"), emit)

    out = _call(
        body, name=name, grid=(m // tm, n // tn, nk), in_specs=[a_spec, b_spec, o_spec],
        out_specs=[o_spec, o_spec],
        out_shape=[jax.ShapeDtypeStruct((m, n), F32), jax.ShapeDtypeStruct((m, n), BF16)],
        args=(xn, wu, gate), scratch_shapes=[pltpu.VMEM((tm, tn) if nk > 1 else (SUBLANE, LANE), F32)], tasks=tasks)
    return (tuple(out[0]), out[1]) if tasks else tuple(out)


def ffn_bwd_act(dh, wd, gate, up, name, tm=1024, tn=512, tk=2048, tasks=()):
    m, n, k = _mm_dims(dh, wd, "nt")
    tm, tn, tk = _tile(m, tm), _tile(n, tn), _tile(k, tk)
    nk = k // tk
    a_spec, b_spec = _mm_specs("nt", tm, tn, tk)
    o_spec = pl.BlockSpec((tm, tn), lambda i, j, k: (i, j))

    def body(a_ref, b_ref, gate_ref, up_ref, dg_ref, du_ref, acc):
        def emit(total):
            dact = 0.5 * total
            g = gate_ref[...]
            sg = _sigmoid(g)
            du_ref[...] = (dact * (g * sg)).astype(BF16)
            dg_ref[...] = (dact * up_ref[...] * (sg * (1.0 + g * (1.0 - sg)))).astype(BF16)

        _accumulate(acc, nk, _dot(a_ref[...], b_ref[...], "nt"), emit)

    out = _call(
        body, name=name, grid=(m // tm, n // tn, nk), in_specs=[a_spec, b_spec, o_spec, o_spec],
        out_specs=[o_spec, o_spec],
        out_shape=[jax.ShapeDtypeStruct((m, n), BF16), jax.ShapeDtypeStruct((m, n), BF16)],
        args=(dh, wd, gate, up), scratch_shapes=[pltpu.VMEM((tm, tn) if nk > 1 else (SUBLANE, LANE), F32)],
        tasks=tasks)
    return (tuple(out[0]), out[1]) if tasks else tuple(out)


def _rows(t, d, tr):
    return pl.BlockSpec((tr, d), lambda i: (i, 0))


def _vec(d):
    return pl.BlockSpec((1, d), lambda i: (0, 0))


def rmsnorm_fwd(x, g, name, tr=512):
    t, d = x.shape
    tr = _row_tile(t, tr)

    def body(x_ref, g_ref, o_ref):
        xf = x_ref[...]
        r = lax.rsqrt(jnp.mean(xf * xf, axis=-1, keepdims=True) + EPS)
        o_ref[...] = (xf * r * g_ref[...]).astype(BF16)

    return pl.pallas_call(
        body, name=name, grid=(t // tr,), in_specs=[_rows(t, d, tr), _vec(d)], out_specs=_rows(t, d, tr),
        out_shape=jax.ShapeDtypeStruct((t, d), BF16), compiler_params=_cparams(),
    )(x, g)


def _rms_bwd(dxn, xf, g):
    r = lax.rsqrt(jnp.mean(xf * xf, axis=-1, keepdims=True) + EPS)
    xhat = xf * r
    dg = jnp.sum(dxn * xhat, axis=0, keepdims=True)
    dxh = dxn * g
    dx = r * (dxh - xhat * jnp.mean(dxh * xhat, axis=-1, keepdims=True))
    return dx, dg


def rmsnorm_bwd(dxn, x, g, dres, name, tr=256):
    t, d = x.shape
    tr = _row_tile(t, tr)

    def body(dxn_ref, x_ref, g_ref, dres_ref, o_ref, ob_ref, dg_ref):
        dx, dg = _rms_bwd(dxn_ref[...], x_ref[...], g_ref[...])
        out = dres_ref[...] + dx
        o_ref[...] = out
        ob_ref[...] = out.astype(BF16)

        @pl.when(pl.program_id(0) == 0)
        def _():
            dg_ref[...] = jnp.zeros_like(dg_ref)

        dg_ref[...] += dg

    return pl.pallas_call(
        body, name=name, grid=(t // tr,),
        in_specs=[_rows(t, d, tr), _rows(t, d, tr), _vec(d), _rows(t, d, tr)],
        out_specs=[_rows(t, d, tr), _rows(t, d, tr), _vec(d)],
        out_shape=[jax.ShapeDtypeStruct((t, d), F32), jax.ShapeDtypeStruct((t, d), BF16),
                   jax.ShapeDtypeStruct((1, d), F32)],
        compiler_params=_cparams(),
    )(dxn, x, g, dres)


def final_loss(h, target, g, name, tr=256):
    t, d = h.shape
    tr = _row_tile(t, tr)

    def body(h_ref, t_ref, g_ref, dh_ref, loss_ref, dg_ref):
        xf = h_ref[...]
        gg = g_ref[...]
        r = lax.rsqrt(jnp.mean(xf * xf, axis=-1, keepdims=True) + EPS)
        xhat = xf * r
        e = xhat * gg - t_ref[...]
        part = jnp.sum(jnp.sum(e * e, axis=1, keepdims=True), axis=0, keepdims=True) * (0.5 / d)
        dout = e * (1.0 / d)
        dg = jnp.sum(dout * xhat, axis=0, keepdims=True)
        dxh = dout * gg
        dh_ref[...] = r * (dxh - xhat * jnp.mean(dxh * xhat, axis=-1, keepdims=True))

        @pl.when(pl.program_id(0) == 0)
        def _():
            dg_ref[...] = jnp.zeros_like(dg_ref)
            loss_ref[...] = jnp.zeros_like(loss_ref)

        dg_ref[...] += dg
        loss_ref[...] += jnp.broadcast_to(part, loss_ref.shape)

    return pl.pallas_call(
        body, name=name, grid=(t // tr,),
        in_specs=[_rows(t, d, tr), _rows(t, d, tr), _vec(d)],
        out_specs=[_rows(t, d, tr), pl.BlockSpec((SUBLANE, LANE), lambda i: (0, 0)), _vec(d)],
        out_shape=[jax.ShapeDtypeStruct((t, d), F32), jax.ShapeDtypeStruct((SUBLANE, LANE), F32),
                   jax.ShapeDtypeStruct((1, d), F32)],
        compiler_params=_cparams(),
    )(h, target, g)


def ple_fwd(h, glin, pp, name, tr=512):
    t, d = h.shape
    tr = _row_tile(t, tr)

    def body(h_ref, gl_ref, pp_ref, o_ref):
        o_ref[...] = h_ref[...] + _sigmoid(gl_ref[...]) * pp_ref[...]

    sp = _rows(t, d, tr)
    return pl.pallas_call(
        body, name=name, grid=(t // tr,), in_specs=[sp, sp, sp], out_specs=sp,
        out_shape=jax.ShapeDtypeStruct((t, d), F32), compiler_params=_cparams(),
    )(h, glin, pp)


def ple_bwd(dh, glin, pp, name, tr=512):
    t, d = dh.shape
    tr = _row_tile(t, tr)

    def body(dh_ref, gl_ref, pp_ref, dpp_ref, dgl_ref):
        gate = _sigmoid(gl_ref[...])
        dh_ = dh_ref[...]
        dpp_ref[...] = (dh_ * gate).astype(BF16)
        dgl_ref[...] = (dh_ * pp_ref[...] * gate * (1.0 - gate)).astype(BF16)

    sp = _rows(t, d, tr)
    return pl.pallas_call(
        body, name=name, grid=(t // tr,), in_specs=[sp, sp, sp], out_specs=[sp, sp],
        out_shape=[jax.ShapeDtypeStruct((t, d), BF16), jax.ShapeDtypeStruct((t, d), BF16)],
        compiler_params=_cparams(),
    )(dh, glin, pp)


def mix_out_fwd(y_pre, glin, y_gmlp, g_so, g_go, name, tr=512):
    t, d = y_pre.shape
    tr = _row_tile(t, tr)

    def body(yp_ref, gl_ref, yg_ref, gs_ref, gg_ref, o_ref):
        ys = _gelu(yp_ref[...]) * _sigmoid(gl_ref[...])
        r = lax.rsqrt(jnp.mean(ys * ys, axis=-1, keepdims=True) + EPS)
        o_ref[:, 0:d] = (ys * r * gs_ref[...]).astype(BF16)
        yq = yg_ref[...]
        r2 = lax.rsqrt(jnp.mean(yq * yq, axis=-1, keepdims=True) + EPS)
        o_ref[:, d:2 * d] = (yq * r2 * gg_ref[...]).astype(BF16)

    sp = _rows(t, d, tr)
    return pl.pallas_call(
        body, name=name, grid=(t // tr,), in_specs=[sp, sp, sp, _vec(d), _vec(d)],
        out_specs=_rows(t, 2 * d, tr), out_shape=jax.ShapeDtypeStruct((t, 2 * d), BF16),
        compiler_params=_cparams(),
    )(y_pre, glin, y_gmlp, g_so, g_go)


def mix_out_bwd(dycat, y_pre, glin, y_gmlp, g_so, g_go, name, tr=256):
    t, d = y_pre.shape
    tr = _row_tile(t, tr)

    def body(dy_ref, yp_ref, gl_ref, yg_ref, gs_ref, gg_ref, dyg_ref, dl_ref, dyq_ref, dgs_ref, dgg_ref):
        yg = _gelu(yp_ref[...])
        sg = _sigmoid(gl_ref[...])
        dys, dgs = _rms_bwd(dy_ref[:, 0:d], yg * sg, gs_ref[...])
        dyg_ref[...] = dys * sg
        dl_ref[...] = (dys * yg * sg * (1.0 - sg)).astype(BF16)
        dyq, dgg = _rms_bwd(dy_ref[:, d:2 * d], yg_ref[...], gg_ref[...])
        dyq_ref[...] = dyq

        @pl.when(pl.program_id(0) == 0)
        def _():
            dgs_ref[...] = jnp.zeros_like(dgs_ref)
            dgg_ref[...] = jnp.zeros_like(dgg_ref)

        dgs_ref[...] += dgs
        dgg_ref[...] += dgg

    sp = _rows(t, d, tr)
    return pl.pallas_call(
        body, name=name, grid=(t // tr,),
        in_specs=[_rows(t, 2 * d, tr), sp, sp, sp, _vec(d), _vec(d)],
        out_specs=[sp, sp, sp, _vec(d), _vec(d)],
        out_shape=[jax.ShapeDtypeStruct((t, d), F32), jax.ShapeDtypeStruct((t, d), BF16),
                   jax.ShapeDtypeStruct((t, d), F32), jax.ShapeDtypeStruct((1, d), F32),
                   jax.ShapeDtypeStruct((1, d), F32)],
        compiler_params=_cparams(),
    )(dycat, y_pre, glin, y_gmlp, g_so, g_go)


SCAN_COLS = 512


def _scan_tile(xr, xi, const, cr, ci, reverse):
    for lvl, sh in enumerate((1, 2, 4)):
        ar, ai = const(2 * lvl), const(2 * lvl + 1)
        s = (SUBLANE - sh) if reverse else sh
        rr = pltpu.roll(xr, s, 0)
        ri = pltpu.roll(xi, s, 0)
        xr, xi = xr + ar * rr - ai * ri, xi + ar * ri + ai * rr
    pr, pi_ = const(6), const(7)
    xr, xi = xr + pr * cr - pi_ * ci, xi + pr * ci + pi_ * cr
    return xr, xi


def _bcast_row(x, row):
    return jnp.broadcast_to(x[row:row + 1, :], x.shape)


def s5_fwd(z, bc_r, bc_i, cc_r, cc_i, apw, dvec, name, tc=256, tasks=()):
    t = z.shape[0]
    nblk = bc_r.shape[0]
    d = nblk * LANE
    ns = nblk * STATE_BLOCK
    tc = _row_tile(t, tc)
    ntile = tc // SUBLANE

    def body(z_ref, br_ref, bi_ref, cr_ref, ci_ref, apw_ref, d_ref, y_ref, yg_ref, sr_ref, si_ref, carry):
        @pl.when(pl.program_id(0) == 0)
        def _():
            carry[...] = jnp.zeros_like(carry)

        for j in range(nblk):
            uj = z_ref[:, j * LANE:(j + 1) * LANE]
            ub = uj.astype(BF16)
            for q in range(STATE_BLOCK // SCAN_COLS):
                c0 = j * STATE_BLOCK + q * SCAN_COLS
                cs = pl.ds(c0, SCAN_COLS)
                bs = slice(q * SCAN_COLS, (q + 1) * SCAN_COLS)
                sr_ref[:, cs] = _dot(ub, br_ref[j, :, bs])
                si_ref[:, cs] = _dot(ub, bi_ref[j, :, bs])
                const = lambda k, cs=cs: apw_ref[k, :, cs]

                def tile(k, c, cs=cs, const=const):
                    rows = pl.ds(pl.multiple_of(k * SUBLANE, SUBLANE), SUBLANE)
                    xr, xi = _scan_tile(sr_ref[rows, cs], si_ref[rows, cs], const, c[0], c[1], False)
                    sr_ref[rows, cs] = xr
                    si_ref[rows, cs] = xi
                    return _bcast_row(xr, SUBLANE - 1), _bcast_row(xi, SUBLANE - 1)

                c_r, c_i = lax.fori_loop(0, ntile, tile, (carry[0, :, cs], carry[1, :, cs]))
                carry[0, :, cs] = c_r
                carry[1, :, cs] = c_i
            sb = pl.ds(j * STATE_BLOCK, STATE_BLOCK)
            y = (_dot(sr_ref[:, sb].astype(BF16), cr_ref[j]) - _dot(si_ref[:, sb].astype(BF16), ci_ref[j])
                 + d_ref[:, j * LANE:(j + 1) * LANE] * uj)
            y_ref[:, j * LANE:(j + 1) * LANE] = y
            yg_ref[:, j * LANE:(j + 1) * LANE] = _gelu(y).astype(BF16)

    full3 = lambda shp: pl.BlockSpec(shp, lambda i: (0, 0, 0))
    out = _call(
        body, name=name, grid=(t // tc,),
        in_specs=[pl.BlockSpec((tc, d), lambda i: (i, 0)), full3(bc_r.shape), full3(bc_i.shape),
                  full3(cc_r.shape), full3(cc_i.shape), full3(apw.shape), _vec(d)],
        out_specs=[pl.BlockSpec((tc, d), lambda i: (i, 0)), pl.BlockSpec((tc, d), lambda i: (i, 0)),
                   pl.BlockSpec((tc, ns), lambda i: (i, 0)), pl.BlockSpec((tc, ns), lambda i: (i, 0))],
        out_shape=[jax.ShapeDtypeStruct((t, d), F32), jax.ShapeDtypeStruct((t, d), BF16),
                   jax.ShapeDtypeStruct((t, ns), F32), jax.ShapeDtypeStruct((t, ns), F32)],
        args=(z, bc_r, bc_i, cc_r, cc_i, apw, dvec), scratch_shapes=[pltpu.VMEM((2, SUBLANE, ns), F32)], tasks=tasks)
    return (tuple(out[0]), out[1]) if tasks else tuple(out)


def s5_bwd(dyg, y_pre, z, sr, si, bc_r, bc_i, cc_r, cc_i, apw_rev, dvec, name, tc=128, tasks=()):
    t = z.shape[0]
    nblk = bc_r.shape[0]
    d = nblk * LANE
    ns = nblk * STATE_BLOCK
    tc = _row_tile(t, tc)
    ntile = tc // SUBLANE
    nchunk = t // tc
    tiles_per_chunk = tc // SUBLANE

    def body(dyg_ref, yp_ref, z_ref, sr_ref, si_ref, pr_ref, pi_ref, br_ref, bi_ref, cr_ref, ci_ref, apw_ref,
             d_ref, du_ref, gd_ref, gcr_ref, gci_ref, gbr_ref, gbi_ref, gar_ref, gai_ref, lr_ref, li_ref, carry):
        step = pl.program_id(0)

        @pl.when(step == 0)
        def _():
            carry[...] = jnp.zeros_like(carry)
            for ref in (gd_ref, gcr_ref, gci_ref, gbr_ref, gbi_ref, gar_ref, gai_ref):
                ref[...] = jnp.zeros_like(ref)

        first_chunk = (step == nchunk - 1).astype(F32)
        keep_prev = 1.0 - first_chunk
        row0 = lax.broadcasted_iota(jnp.int32, (SUBLANE, SCAN_COLS), 0) == 0

        for j in range(nblk):
            lanes = slice(j * LANE, (j + 1) * LANE)
            uj = z_ref[:, lanes]
            ub = uj.astype(BF16)
            gy = dyg_ref[:, lanes] * _gelu_grad(yp_ref[:, lanes])
            gyb = gy.astype(BF16)
            gd_ref[:, lanes] += jnp.sum(gy * uj, axis=0, keepdims=True)
            for q in range(STATE_BLOCK // SCAN_COLS):
                c0 = j * STATE_BLOCK + q * SCAN_COLS
                cs = pl.ds(c0, SCAN_COLS)
                bs = slice(q * SCAN_COLS, (q + 1) * SCAN_COLS)
                lr_ref[:, cs] = _dot(gyb, cr_ref[j, bs, :], "nt")
                li_ref[:, cs] = -_dot(gyb, ci_ref[j, bs, :], "nt")
                const = lambda k, cs=cs: apw_ref[k, :, cs]

                def one_tile(rows, prev_r, prev_i, c, cs=cs, const=const):
                    cr_, ci_, gar, gai = c
                    xr, xi = _scan_tile(lr_ref[rows, cs], li_ref[rows, cs], const, cr_, ci_, True)
                    lr_ref[rows, cs] = xr
                    li_ref[rows, cs] = xi
                    spr = jnp.where(row0, prev_r, pltpu.roll(sr_ref[rows, cs], 1, 0))
                    spi = jnp.where(row0, prev_i, pltpu.roll(si_ref[rows, cs], 1, 0))
                    gar = gar + xr * spr + xi * spi
                    gai = gai + xi * spr - xr * spi
                    return _bcast_row(xr, 0), _bcast_row(xi, 0), gar, gai

                def tile(k, c, cs=cs, one_tile=one_tile):
                    kk = ntile - 1 - k
                    rows = pl.ds(pl.multiple_of(kk * SUBLANE, SUBLANE), SUBLANE)
                    prow = pl.ds(pl.multiple_of((kk - 1) * SUBLANE, SUBLANE), SUBLANE)
                    prev_r = _bcast_row(sr_ref[prow, cs], SUBLANE - 1)
                    prev_i = _bcast_row(si_ref[prow, cs], SUBLANE - 1)
                    return one_tile(rows, prev_r, prev_i, c)

                zero = jnp.zeros((SUBLANE, SCAN_COLS), F32)
                c = lax.fori_loop(0, ntile - 1, tile, (carry[0, :, cs], carry[1, :, cs], zero, zero))
                prev_r = _bcast_row(pr_ref[:, cs], SUBLANE - 1) * keep_prev
                prev_i = _bcast_row(pi_ref[:, cs], SUBLANE - 1) * keep_prev
                c_r, c_i, gar, gai = one_tile(pl.ds(0, SUBLANE), prev_r, prev_i, c)
                carry[0, :, cs] = c_r
                carry[1, :, cs] = c_i
                gar_ref[:, cs] += gar
                gai_ref[:, cs] += gai
            sb = pl.ds(j * STATE_BLOCK, STATE_BLOCK)
            lrb = lr_ref[:, sb].astype(BF16)
            lib = li_ref[:, sb].astype(BF16)
            gcr_ref[j] += _dot(gyb, sr_ref[:, sb].astype(BF16), "tn")
            gci_ref[j] -= _dot(gyb, si_ref[:, sb].astype(BF16), "tn")
            gbr_ref[j] += _dot(ub, lrb, "tn")
            gbi_ref[j] += _dot(ub, lib, "tn")
            du = _dot(lrb, br_ref[j], "nt") + _dot(lib, bi_ref[j], "nt") + gy * d_ref[:, lanes]
            du_ref[:, lanes] = du.astype(BF16)

    rev = lambda i: (nchunk - 1 - i, 0)
    prev = lambda i: (jnp.maximum((nchunk - 1 - i) * tiles_per_chunk - 1, 0), 0)
    full3 = lambda shp: pl.BlockSpec(shp, lambda i: (0, 0, 0))
    acc3 = pl.BlockSpec((nblk, LANE, STATE_BLOCK), lambda i: (0, 0, 0))
    acc_rows = pl.BlockSpec((SUBLANE, ns), lambda i: (0, 0))
    out = _call(
        body, name=name, grid=(nchunk,),
        in_specs=[pl.BlockSpec((tc, d), rev), pl.BlockSpec((tc, d), rev), pl.BlockSpec((tc, d), rev),
                  pl.BlockSpec((tc, ns), rev), pl.BlockSpec((tc, ns), rev),
                  pl.BlockSpec((SUBLANE, ns), prev), pl.BlockSpec((SUBLANE, ns), prev),
                  full3(bc_r.shape), full3(bc_i.shape), full3(cc_r.shape), full3(cc_i.shape), full3(apw_rev.shape),
                  _vec(d)],
        out_specs=[pl.BlockSpec((tc, d), rev), _vec(d), acc3, acc3, acc3, acc3, acc_rows, acc_rows],
        out_shape=[jax.ShapeDtypeStruct((t, d), BF16), jax.ShapeDtypeStruct((1, d), F32)]
        + [jax.ShapeDtypeStruct((nblk, LANE, STATE_BLOCK), F32)] * 4
        + [jax.ShapeDtypeStruct((SUBLANE, ns), F32)] * 2,
        args=(dyg, y_pre, z, sr, si, sr, si, bc_r, bc_i, cc_r, cc_i, apw_rev, dvec),
        scratch_shapes=[pltpu.VMEM((tc, ns), F32), pltpu.VMEM((tc, ns), F32), pltpu.VMEM((2, SUBLANE, ns), F32)],
        tasks=tasks)
    return (tuple(out[0]), out[1]) if tasks else tuple(out)


def _cmul(a, b):
    return a[0] * b[0] - a[1] * b[1], a[0] * b[1] + a[1] * b[0]


def _scan_constants(abar_r, abar_i, reverse):
    ar = abar_r.reshape(1, -1)
    ai = abar_i.reshape(1, -1)
    if reverse:
        ai = -ai
    pw = [(ar, ai)]
    for _ in range(SUBLANE - 1):
        pw.append(_cmul(pw[-1], (ar, ai)))
    rows = lax.broadcasted_iota(jnp.int32, (SUBLANE, 1), 0)
    out = []
    for sh in (1, 2, 4):
        keep = (rows <= SUBLANE - 1 - sh) if reverse else (rows >= sh)
        for part in pw[sh - 1]:
            out.append(jnp.where(keep, part, 0.0))
    for comp in (0, 1):
        stack = jnp.concatenate([pw[k][comp] for k in range(SUBLANE)], axis=0)
        out.append(stack[::-1] if reverse else stack)
    return jnp.stack(out, axis=0).astype(F32)


def _ssm_discretize(log_dt, a_re, a_im, b_re, b_im):
    dt = jnp.exp(log_dt)[:, None]
    lr = jnp.minimum(a_re, -1e-4)
    li = a_im
    mag = jnp.exp(lr * dt)
    ang = li * dt
    abar_r = mag * jnp.cos(ang)
    abar_i = mag * jnp.sin(ang)
    den = lr * lr + li * li
    xr = abar_r - 1.0
    xi = abar_i
    zr = (xr * lr + xi * li) / den
    zi = (xi * lr - xr * li) / den
    bbar_r = zr[..., None] * b_re - zi[..., None] * b_im
    bbar_i = zr[..., None] * b_im + zi[..., None] * b_re
    return abar_r, abar_i, bbar_r, bbar_i


def _block_diag(w):
    g, a, b = w.shape
    nb = g // GROUPS_PER_BLOCK
    eye = jnp.eye(GROUPS_PER_BLOCK, dtype=w.dtype)
    w5 = w.reshape(nb, GROUPS_PER_BLOCK, a, b)
    out = w5[:, :, :, None, :] * eye[None, :, None, :, None]
    return out.reshape(nb, GROUPS_PER_BLOCK * a, GROUPS_PER_BLOCK * b)


def _block_diag_extract(m, a, b):
    nb = m.shape[0]
    eye = jnp.eye(GROUPS_PER_BLOCK, dtype=m.dtype)
    m5 = m.reshape(nb, GROUPS_PER_BLOCK, a, GROUPS_PER_BLOCK, b)
    out = jnp.sum(m5 * eye[None, :, None, :, None], axis=3)
    return out.reshape(nb * GROUPS_PER_BLOCK, a, b)


def _layer_norm(gv, nv):
    mu = jnp.mean(gv, axis=-1, keepdims=True)
    xc = gv - mu
    r = lax.rsqrt(jnp.mean(xc * xc, axis=-1, keepdims=True) + EPS)
    xhat = xc * r
    return xhat * nv, xhat, r


def gmlp_fwd(z, norm_v, wm, bs, name, tr=256):
    t = z.shape[0]
    nh = wm.shape[0]
    d = nh * GMLP_HEAD
    col0 = (z.shape[1] - 2 * d) // d
    tr = _row_tile(t, tr)

    def body(zu_ref, zv_ref, nv_ref, wm_ref, bs_ref, o_ref):
        v, _, _ = _layer_norm(_gelu(zv_ref[...]), nv_ref[...])
        vb = v.astype(BF16)
        u = _gelu(zu_ref[...])
        for c in range(tr // CHUNK):
            rows = slice(c * CHUNK, (c + 1) * CHUNK)
            for h in range(nh):
                cols = slice(h * GMLP_HEAD, (h + 1) * GMLP_HEAD)
                s = _dot(wm_ref[h], vb[rows, cols]) + bs_ref[h]
                o_ref[rows, cols] = u[rows, cols] * s

    return pl.pallas_call(
        body, name=name, grid=(t // tr,),
        in_specs=[pl.BlockSpec((tr, d), lambda i: (i, col0)), pl.BlockSpec((tr, d), lambda i: (i, col0 + 1)),
                  _vec(d), pl.BlockSpec(wm.shape, lambda i: (0, 0, 0)), pl.BlockSpec(bs.shape, lambda i: (0, 0, 0))],
        out_specs=pl.BlockSpec((tr, d), lambda i: (i, 0)),
        out_shape=jax.ShapeDtypeStruct((t, d), F32), compiler_params=_cparams(),
    )(z, z, norm_v, wm, bs)


def gmlp_bwd(dy, z, norm_v, wm, wmt, bs, name, tr=256):
    t = z.shape[0]
    nh = wm.shape[0]
    d = nh * GMLP_HEAD
    col0 = (z.shape[1] - 2 * d) // d
    tr = _row_tile(t, tr)

    def body(dy_ref, zu_ref, zv_ref, nv_ref, wm_ref, wmt_ref, bs_ref, dzu_ref, dzv_ref, dnv_ref, dwm_ref, dbs_ref,
             dv_ref):
        @pl.when(pl.program_id(0) == 0)
        def _():
            dnv_ref[...] = jnp.zeros_like(dnv_ref)
            dwm_ref[...] = jnp.zeros_like(dwm_ref)
            dbs_ref[...] = jnp.zeros_like(dbs_ref)

        zv = zv_ref[...]
        nv = nv_ref[...]
        v, xhat, r = _layer_norm(_gelu(zv), nv)
        vb = v.astype(BF16)
        zu = zu_ref[...]
        u = _gelu(zu)
        dy_ = dy_ref[...]
        for c in range(tr // CHUNK):
            rows = slice(c * CHUNK, (c + 1) * CHUNK)
            for h in range(nh):
                cols = slice(h * GMLP_HEAD, (h + 1) * GMLP_HEAD)
                vh = vb[rows, cols]
                s = _dot(wm_ref[h], vh) + bs_ref[h]
                dyh = dy_[rows, cols]
                dzu_ref[rows, cols] = (dyh * s * _gelu_grad(zu[rows, cols])).astype(BF16)
                ds = dyh * u[rows, cols]
                dsb = ds.astype(BF16)
                dbs_ref[h] += jnp.sum(ds, axis=1, keepdims=True)
                dwm_ref[h] += _dot(dsb, vh, "nt")
                dv_ref[rows, cols] = _dot(wmt_ref[h], dsb)
        dv = dv_ref[...]
        dnv_ref[...] += jnp.sum(dv * xhat, axis=0, keepdims=True)
        dxh = dv * nv
        dgv = r * (dxh - jnp.mean(dxh, axis=-1, keepdims=True) - xhat * jnp.mean(dxh * xhat, axis=-1, keepdims=True))
        dzv_ref[...] = (dgv * _gelu_grad(zv)).astype(BF16)

    full3 = lambda shp: pl.BlockSpec(shp, lambda i: (0, 0, 0))
    rows_d = pl.BlockSpec((tr, d), lambda i: (i, 0))
    return pl.pallas_call(
        body, name=name, grid=(t // tr,),
        in_specs=[rows_d, pl.BlockSpec((tr, d), lambda i: (i, col0)), pl.BlockSpec((tr, d), lambda i: (i, col0 + 1)),
                  _vec(d), full3(wm.shape), full3(wmt.shape), full3(bs.shape)],
        out_specs=[rows_d, rows_d, _vec(d), full3((nh, CHUNK, CHUNK)), full3((nh, CHUNK, 1))],
        out_shape=[jax.ShapeDtypeStruct((t, d), BF16), jax.ShapeDtypeStruct((t, d), BF16),
                   jax.ShapeDtypeStruct((1, d), F32), jax.ShapeDtypeStruct((nh, CHUNK, CHUNK), F32),
                   jax.ShapeDtypeStruct((nh, CHUNK, 1), F32)],
        scratch_shapes=[pltpu.VMEM((tr, d), F32)], compiler_params=_cparams(),
    )(dy, z, z, norm_v, wm, wmt, bs)


def _block(ref, axis, size, k):
    start = pl.multiple_of(k * size, size)
    if axis == 0:
        return ref.at[pl.ds(start, size), :]
    return ref.at[:, pl.ds(start, size)]


def _place():
    x, y, c = lax.axis_index("x"), lax.axis_index("y"), lax.axis_index("c")
    chips = [(1 - x, y), (x, 1 - y), (1 - x, 1 - y)]
    return x, y, c, chips


def _dev(x, y, c):
    return 4 * x + 2 * y + c


def gather_task(shards, axes):
    n = len(shards)
    sizes = [s.shape[ax] for s, ax in zip(shards, axes)]
    out_shape = [
        jax.ShapeDtypeStruct((s.shape[0] * N_DEV, s.shape[1]) if ax == 0 else (s.shape[0], s.shape[1] * N_DEV), s.dtype)
        for s, ax in zip(shards, axes)
    ]

    def copy(ins, outs, send_sems, recv_sems, t, k, block, to, from_input=False):
        dst = _block(outs[t], axes[t], sizes[t], _dev(*block))
        return pltpu.make_async_remote_copy(
            src_ref=ins[t] if from_input else dst, dst_ref=dst,
            send_sem=send_sems.at[t * 7 + k], recv_sem=recv_sems.at[t * 7 + k],
            device_id=to, device_id_type=MESH_DT)

    def local(ins, outs, local_sems, t, me):
        return pltpu.make_async_copy(ins[t], _block(outs[t], axes[t], sizes[t], _dev(*me)), local_sems.at[t])

    def start(ins, outs, send_sems, recv_sems, local_sems):
        x, y, c, chips = _place()
        me, sibling = (x, y, c), (x, y, 1 - c)
        for t in range(n):
            local(ins, outs, local_sems, t, me).start()
        for t in range(n):
            copy(ins, outs, send_sems, recv_sems, t, 0, me, sibling, True).start()
            for j, chip in enumerate(chips):
                copy(ins, outs, send_sems, recv_sems, t, 1 + j, me, (*chip, c), True).start()

    def late(ins, outs, send_sems, recv_sems, local_sems):
        x, y, c, chips = _place()
        me, sibling = (x, y, c), (x, y, 1 - c)
        for t in range(n):
            for j, chip in enumerate(chips):
                copy(ins, outs, send_sems, recv_sems, t, 1 + j, (*chip, c), me).wait_recv()
                copy(ins, outs, send_sems, recv_sems, t, 4 + j, (*chip, c), sibling).start()

    def finish(ins, outs, send_sems, recv_sems, local_sems):
        x, y, c, chips = _place()
        me, sibling = (x, y, c), (x, y, 1 - c)
        for t in range(n):
            copy(ins, outs, send_sems, recv_sems, t, 0, sibling, me).wait_recv()
            for j, chip in enumerate(chips):
                copy(ins, outs, send_sems, recv_sems, t, 4 + j, (*chip, 1 - c), me).wait_recv()
        for t in range(n):
            copy(ins, outs, send_sems, recv_sems, t, 0, me, sibling, True).wait_send()
            for j, chip in enumerate(chips):
                copy(ins, outs, send_sems, recv_sems, t, 1 + j, me, (*chip, c), True).wait_send()
                copy(ins, outs, send_sems, recv_sems, t, 4 + j, (*chip, c), sibling).wait_send()
            local(ins, outs, local_sems, t, me).wait()

    return CommTask(shards, out_shape, (7 * n, 7 * n, n), start, late, finish)


def _blk3(shape2, axis):
    r, c = shape2
    return (r // N_DEV, c) if axis == 0 else (r, c // N_DEV)


def _no_late(ins, outs, send_sems, recv_sems, local_sems):
    pass


def to_sibling_task(grads, axes):
    n = len(grads)
    blks = [_blk3(g.shape, ax) for g, ax in zip(grads, axes)]
    sizes = [b[ax] for b, ax in zip(blks, axes)]

    def copies(ins, outs, send_sems, recv_sems):
        x, y, c, _ = _place()
        return [pltpu.make_async_remote_copy(
            src_ref=_block(ins[t], axes[t], sizes[t], 2 * i + (1 - c)), dst_ref=outs[t].at[i],
            send_sem=send_sems.at[t * N_CHIP + i], recv_sem=recv_sems.at[t * N_CHIP + i],
            device_id=(x, y, 1 - c), device_id_type=MESH_DT) for t in range(n) for i in range(N_CHIP)]

    def start(ins, outs, send_sems, recv_sems, local_sems):
        for cp in copies(ins, outs, send_sems, recv_sems):
            cp.start()

    def finish(ins, outs, send_sems, recv_sems, local_sems):
        cps = copies(ins, outs, send_sems, recv_sems)
        for cp in cps:
            cp.wait_recv()
        for cp in cps:
            cp.wait_send()

    out_shape = [jax.ShapeDtypeStruct((N_CHIP,) + b, g.dtype) for b, g in zip(blks, grads)]
    return CommTask(grads, out_shape, (N_CHIP * n, N_CHIP * n, 1), start, _no_late, finish)


def across_chips_task(parts):
    n = len(parts)

    def copies(ins, outs, send_sems, recv_sems):
        x, y, c, chips = _place()
        my_chip = 2 * x + y
        return [pltpu.make_async_remote_copy(
            src_ref=ins[t].at[2 * chip[0] + chip[1]], dst_ref=outs[t].at[my_chip],
            send_sem=send_sems.at[t * 3 + j], recv_sem=recv_sems.at[t * 3 + j],
            device_id=(*chip, c), device_id_type=MESH_DT) for t in range(n) for j, chip in enumerate(chips)]

    def mine(ins, outs, local_sems):
        x, y, _, _ = _place()
        my_chip = 2 * x + y
        return [pltpu.make_async_copy(ins[t].at[my_chip], outs[t].at[my_chip], local_sems.at[t]) for t in range(n)]

    def start(ins, outs, send_sems, recv_sems, local_sems):
        for cp in mine(ins, outs, local_sems):
            cp.start()
        for cp in copies(ins, outs, send_sems, recv_sems):
            cp.start()

    def finish(ins, outs, send_sems, recv_sems, local_sems):
        cps = copies(ins, outs, send_sems, recv_sems)
        for cp in cps:
            cp.wait_recv()
        for cp in cps:
            cp.wait_send()
        for cp in mine(ins, outs, local_sems):
            cp.wait()

    out_shape = [jax.ShapeDtypeStruct(p.shape, p.dtype) for p in parts]
    return CommTask(parts, out_shape, (3 * n, 3 * n, n), start, _no_late, finish)


def run_tasks(tasks, name):
    t_in = [len(t.inputs) for t in tasks]
    t_out = [len(t.out_shape) for t in tasks]

    def body(*refs):
        pos, views = 0, []
        for k in t_in:
            views.append([refs[pos:pos + k]])
            pos += k
        for v, k in zip(views, t_out):
            v.append(refs[pos:pos + k])
            pos += k
        for i, v in enumerate(views):
            v.extend(refs[pos + 3 * i:pos + 3 * i + 3])
        for phase in ("start", "late", "finish"):
            for t, v in zip(tasks, views):
                getattr(t, phase)(*v)

    any_spec = pl.BlockSpec(memory_space=pl.ANY)
    res = pl.pallas_call(
        body, name=name, in_specs=[any_spec] * sum(t_in), out_specs=[any_spec] * sum(t_out),
        out_shape=[s for t in tasks for s in t.out_shape],
        scratch_shapes=[pltpu.SemaphoreType.DMA((k,)) for t in tasks for k in t.n_sems],
    )(*[a for t in tasks for a in t.inputs])
    res, out, pos = list(res), [], 0
    for k in t_out:
        out.append(res[pos:pos + k])
        pos += k
    return out


_HBM_SPEC = pl.BlockSpec(memory_space=pl.ANY)
_SEM_SPEC = pl.BlockSpec(memory_space=pltpu.SEMAPHORE)
_DATAFLOW = pltpu.SideEffectType.DATAFLOW_SIDE_EFFECTING


def _full_shape(s, ax):
    return (s.shape[0] * N_DEV, s.shape[1]) if ax == 0 else (s.shape[0], s.shape[1] * N_DEV)


def _level1_copy(src, landing, axis, size, send_sems, recv_sems, slot, sender, to):
    dst = _block(landing, axis, size, _dev(*sender))
    return pltpu.make_async_remote_copy(src_ref=src, dst_ref=dst, send_sem=send_sems.at[slot],
                                        recv_sem=recv_sems.at[slot], device_id=to, device_id_type=MESH_DT)


def place_own_block(shard, axis, me, name, tr=256):
    r, c = shard.shape
    tr = _row_tile(r, tr)
    nrb = r // tr
    if axis == 0:
        o_map = lambda i, me_ref: (me_ref[0] * nrb + i, 0)
    else:
        o_map = lambda i, me_ref: (i, me_ref[0])

    def body(me_ref, x_ref, o_ref):
        o_ref[...] = x_ref[...]

    return pl.pallas_call(
        body, name=name,
        grid_spec=pltpu.PrefetchScalarGridSpec(
            num_scalar_prefetch=1, grid=(nrb,), in_specs=[pl.BlockSpec((tr, c), lambda i, me_ref: (i, 0))],
            out_specs=pl.BlockSpec((tr, c), o_map)),
        out_shape=jax.ShapeDtypeStruct(_full_shape(shard, axis), shard.dtype), compiler_params=_cparams(),
    )(me, shard)


def gather_start(landing, axes, sizes, groups, name):
    n = len(landing)

    def body(*refs):
        lands, sems = refs[:n], refs[2 * n:]
        x, y, c, chips = _place()
        me = (x, y, c)
        targets = [(x, y, 1 - c)] + [(*chip, c) for chip in chips]
        for g, members in enumerate(groups):
            for m, t in enumerate(members):
                own = _block(lands[t], axes[t], sizes[t], _dev(*me))
                for k, to in enumerate(targets):
                    _level1_copy(own, lands[t], axes[t], sizes[t], sems[2 * g], sems[2 * g + 1], 4 * m + k,
                                 me, to).start()

    out = pl.pallas_call(
        body, name=name,
        out_shape=[jax.ShapeDtypeStruct(b.shape, b.dtype) for b in landing]
        + [pltpu.SemaphoreType.DMA((4 * len(members),)) for members in groups for _ in (0, 1)],
        in_specs=[_HBM_SPEC] * n, out_specs=[_HBM_SPEC] * n + [_SEM_SPEC] * (2 * len(groups)),
        input_output_aliases={i: i for i in range(n)},
        compiler_params=pltpu.CompilerParams(has_side_effects=_DATAFLOW),
    )(*landing)
    out = list(out)
    sems = out[n:]
    return out[:n], [(sems[2 * g], sems[2 * g + 1]) for g in range(len(groups))]


def gather_wait(landing, axes, sizes, send_sems, recv_sems, after, name):
    n = len(landing)

    def body(*refs):
        lands = refs[:n]
        send, recv = refs[n], refs[n + 1]
        x, y, c, chips = _place()
        me = (x, y, c)
        peers = [(x, y, 1 - c)] + [(*chip, c) for chip in chips]
        for t in range(n):
            own = _block(lands[t], axes[t], sizes[t], _dev(*me))
            for k, peer in enumerate(peers):
                _level1_copy(own, lands[t], axes[t], sizes[t], send, recv, 4 * t + k, me, peer).wait_send()
                _level1_copy(own, lands[t], axes[t], sizes[t], send, recv, 4 * t + k, peer, me).wait_recv()

    out = pl.pallas_call(
        body, name=name, out_shape=[jax.ShapeDtypeStruct(b.shape, b.dtype) for b in landing],
        in_specs=[_HBM_SPEC] * n + [_SEM_SPEC, _SEM_SPEC, pl.BlockSpec(memory_space=pl.ANY)],
        out_specs=[_HBM_SPEC] * n, input_output_aliases={i: i for i in range(n)},
        compiler_params=pltpu.CompilerParams(has_side_effects=_DATAFLOW),
    )(*landing, send_sems, recv_sems, after)
    return list(out)


def gather_forward(landing, axes, sizes, name):
    n = len(landing)

    def body(*refs):
        lands = refs[n:2 * n]
        send_sems, recv_sems = refs[2 * n:]
        x, y, c, chips = _place()
        sibling = (x, y, 1 - c)

        def forward(t, j, chip_core):
            blk = _block(lands[t], axes[t], sizes[t], _dev(*chip_core))
            return pltpu.make_async_remote_copy(src_ref=blk, dst_ref=blk, send_sem=send_sems.at[3 * t + j],
                                                recv_sem=recv_sems.at[3 * t + j], device_id=sibling,
                                                device_id_type=MESH_DT)

        sends = [forward(t, j, (*chip, c)) for t in range(n) for j, chip in enumerate(chips)]
        for cp in sends:
            cp.start()
        for t in range(n):
            for j, chip in enumerate(chips):
                forward(t, j, (*chip, 1 - c)).wait_recv()
        for cp in sends:
            cp.wait_send()

    any_spec = pl.BlockSpec(memory_space=pl.ANY)
    out = pl.pallas_call(
        body, name=name, in_specs=[any_spec] * n, out_specs=[any_spec] * n,
        out_shape=[jax.ShapeDtypeStruct(b.shape, b.dtype) for b in landing],
        input_output_aliases={i: i for i in range(n)},
        scratch_shapes=[pltpu.SemaphoreType.DMA((3 * n,)), pltpu.SemaphoreType.DMA((3 * n,))],
    )(*landing)
    return list(out)


def rs_chip_sum(grad, recv, axis, core, name, tr=512):
    br, bc = _blk3(grad.shape, axis)
    tr = _row_tile(br, tr)
    nrb = br // tr

    if axis == 0:
        g_map = lambda i, r, c_ref: ((2 * i + c_ref[0]) * nrb + r, 0)
    else:
        g_map = lambda i, r, c_ref: (r, 2 * i + c_ref[0])

    def body(c_ref, g_ref, r_ref, o_ref):
        o_ref[...] = (g_ref[...].astype(F32) + r_ref[...].astype(F32)).astype(BF16)

    return pl.pallas_call(
        body, name=name,
        grid_spec=pltpu.PrefetchScalarGridSpec(
            num_scalar_prefetch=1, grid=(N_CHIP, nrb),
            in_specs=[pl.BlockSpec((tr, bc), g_map), pl.BlockSpec((None, tr, bc), lambda i, r, c_ref: (i, r, 0))],
            out_specs=pl.BlockSpec((None, tr, bc), lambda i, r, c_ref: (i, r, 0))),
        out_shape=jax.ShapeDtypeStruct((N_CHIP, br, bc), BF16), compiler_params=_cparams(),
    )(core, grad, recv)


def _adamw(w, g, m, v):
    m = ADAM_B1 * m + (1.0 - ADAM_B1) * g
    v = ADAM_B2 * v + (1.0 - ADAM_B2) * (g * g)
    m_hat = m / (1.0 - ADAM_B1 ** ADAM_STEP)
    v_hat = v / (1.0 - ADAM_B2 ** ADAM_STEP)
    delta = -ADAM_LR * (m_hat / (jnp.sqrt(v_hat) + ADAM_EPS) + ADAM_WD * w)
    return delta, m, v


def _sum_chips(p_ref):
    g = p_ref[0].astype(F32)
    for i in range(1, N_CHIP):
        g = g + p_ref[i].astype(F32)
    return g


def adam_sharded(parts, w, m, v, name, tr=256):
    r, c = w.shape
    assert parts.shape[2] == c
    tr = _row_tile(r, tr)

    def body(p_ref, w_ref, m_ref, v_ref, g_ref, d_ref, nm_ref, nv_ref):
        g = _sum_chips(p_ref)
        delta, nm, nv = _adamw(w_ref[...], g, m_ref[...], v_ref[...])
        g_ref[...] = g
        d_ref[...] = delta
        nm_ref[...] = nm
        nv_ref[...] = nv

    sp = pl.BlockSpec((tr, c), lambda i: (i, 0))
    return pl.pallas_call(
        body, name=name, grid=(r // tr,),
        in_specs=[pl.BlockSpec((N_CHIP, tr, c), lambda i: (0, i, 0)), sp, sp, sp],
        out_specs=[sp, sp, sp, sp], out_shape=[jax.ShapeDtypeStruct((r, c), F32)] * 4,
        compiler_params=_cparams(),
    )(parts, w, m, v)


def adam_small(g, w, m, v, name):
    def body(g_ref, w_ref, m_ref, v_ref, d_ref, nm_ref, nv_ref):
        delta, nm, nv = _adamw(w_ref[...], g_ref[...], m_ref[...], v_ref[...])
        d_ref[...] = delta
        nm_ref[...] = nm
        nv_ref[...] = nv

    return pl.pallas_call(
        body, name=name, out_shape=[jax.ShapeDtypeStruct(w.shape, F32)] * 3, compiler_params=_cparams(),
    )(g, w, m, v)


def sum_devices(gathered, name, tr=512):
    _, r, c = gathered.shape
    tr = _row_tile(r, tr)

    def body(x_ref, o_ref):
        s = x_ref[0]
        for k in range(1, N_DEV):
            s = s + x_ref[k]
        o_ref[...] = s

    return pl.pallas_call(
        body, name=name, grid=(r // tr,), in_specs=[pl.BlockSpec((N_DEV, tr, c), lambda i: (0, i, 0))],
        out_specs=pl.BlockSpec((tr, c), lambda i: (i, 0)), out_shape=jax.ShapeDtypeStruct((r, c), F32),
        compiler_params=_cparams(),
    )(gathered)


def _pad_to(a, axis, mult):
    size = a.shape[axis]
    pad = (-size) % mult
    if pad == 0:
        return a
    cfg = [(0, 0)] * a.ndim
    cfg[axis] = (0, pad)
    return jnp.pad(a, cfg)


def _as2d(a):
    if a.ndim == 1:
        return a.reshape(1, -1)
    return a.reshape(-1, a.shape[-1])


def kernel(x, p, norm_ffn1, w1_gate, w1_up, w1_down, norm_mix, w_in, ssm_log_dt, ssm_a_re, ssm_a_im, ssm_b_re, ssm_b_im, ssm_c_re, ssm_c_im, ssm_d, ssm_w_glu, gmlp_norm_v, gmlp_w_s, gmlp_b_s, norm_ssm_out, norm_gmlp_out, w_out, norm_ffn2, w2_gate, w2_up, w2_down, norm_ple, w_ple_gate, w_ple_proj, norm_final, loss_target, m_norm_ffn1, m_w1_gate, m_w1_up, m_w1_down, m_norm_mix, m_w_in, m_ssm_log_dt, m_ssm_a_re, m_ssm_a_im, m_ssm_b_re, m_ssm_b_im, m_ssm_c_re, m_ssm_c_im, m_ssm_d, m_ssm_w_glu, m_gmlp_norm_v, m_gmlp_w_s, m_gmlp_b_s, m_norm_ssm_out, m_norm_gmlp_out, m_w_out, m_norm_ffn2, m_w2_gate, m_w2_up, m_w2_down, m_norm_ple, m_w_ple_gate, m_w_ple_proj, m_norm_final, v_norm_ffn1, v_w1_gate, v_w1_up, v_w1_down, v_norm_mix, v_w_in, v_ssm_log_dt, v_ssm_a_re, v_ssm_a_im, v_ssm_b_re, v_ssm_b_im, v_ssm_c_re, v_ssm_c_im, v_ssm_d, v_ssm_w_glu, v_gmlp_norm_v, v_gmlp_w_s, v_gmlp_b_s, v_norm_ssm_out, v_norm_gmlp_out, v_w_out, v_norm_ffn2, v_w2_gate, v_w2_up, v_w2_down, v_norm_ple, v_w_ple_gate, v_w_ple_proj, v_norm_final):
    weights = dict(
        norm_ffn1=norm_ffn1, w1_gate=w1_gate, w1_up=w1_up, w1_down=w1_down, norm_mix=norm_mix, w_in=w_in,
        ssm_log_dt=ssm_log_dt, ssm_a_re=ssm_a_re, ssm_a_im=ssm_a_im, ssm_b_re=ssm_b_re, ssm_b_im=ssm_b_im,
        ssm_c_re=ssm_c_re, ssm_c_im=ssm_c_im, ssm_d=ssm_d, ssm_w_glu=ssm_w_glu, gmlp_norm_v=gmlp_norm_v,
        gmlp_w_s=gmlp_w_s, gmlp_b_s=gmlp_b_s, norm_ssm_out=norm_ssm_out, norm_gmlp_out=norm_gmlp_out, w_out=w_out,
        norm_ffn2=norm_ffn2, w2_gate=w2_gate, w2_up=w2_up, w2_down=w2_down, norm_ple=norm_ple,
        w_ple_gate=w_ple_gate, w_ple_proj=w_ple_proj, norm_final=norm_final)
    moments_m = dict(
        norm_ffn1=m_norm_ffn1, w1_gate=m_w1_gate, w1_up=m_w1_up, w1_down=m_w1_down, norm_mix=m_norm_mix, w_in=m_w_in,
        ssm_log_dt=m_ssm_log_dt, ssm_a_re=m_ssm_a_re, ssm_a_im=m_ssm_a_im, ssm_b_re=m_ssm_b_re, ssm_b_im=m_ssm_b_im,
        ssm_c_re=m_ssm_c_re, ssm_c_im=m_ssm_c_im, ssm_d=m_ssm_d, ssm_w_glu=m_ssm_w_glu, gmlp_norm_v=m_gmlp_norm_v,
        gmlp_w_s=m_gmlp_w_s, gmlp_b_s=m_gmlp_b_s, norm_ssm_out=m_norm_ssm_out, norm_gmlp_out=m_norm_gmlp_out,
        w_out=m_w_out, norm_ffn2=m_norm_ffn2, w2_gate=m_w2_gate, w2_up=m_w2_up, w2_down=m_w2_down,
        norm_ple=m_norm_ple, w_ple_gate=m_w_ple_gate, w_ple_proj=m_w_ple_proj, norm_final=m_norm_final)
    moments_v = dict(
        norm_ffn1=v_norm_ffn1, w1_gate=v_w1_gate, w1_up=v_w1_up, w1_down=v_w1_down, norm_mix=v_norm_mix, w_in=v_w_in,
        ssm_log_dt=v_ssm_log_dt, ssm_a_re=v_ssm_a_re, ssm_a_im=v_ssm_a_im, ssm_b_re=v_ssm_b_re, ssm_b_im=v_ssm_b_im,
        ssm_c_re=v_ssm_c_re, ssm_c_im=v_ssm_c_im, ssm_d=v_ssm_d, ssm_w_glu=v_ssm_w_glu, gmlp_norm_v=v_gmlp_norm_v,
        gmlp_w_s=v_gmlp_w_s, gmlp_b_s=v_gmlp_b_s, norm_ssm_out=v_norm_ssm_out, norm_gmlp_out=v_norm_gmlp_out,
        w_out=v_w_out, norm_ffn2=v_norm_ffn2, w2_gate=v_w2_gate, w2_up=v_w2_up, w2_down=v_w2_down,
        norm_ple=v_norm_ple, w_ple_gate=v_w_ple_gate, w_ple_proj=v_w_ple_proj, norm_final=v_norm_final)
    names = list(weights)

    xs = x[0]
    ps = p[0, 0].astype(BF16)
    tgt = loss_target[0]
    d_model = xs.shape[1]
    d_ssm = d_model // 2
    n_groups = d_ssm // SSM_GROUP

    transposed = ("w1_gate", "w1_up", "w2_gate", "w2_up")
    big = {
        "w1_gate": 0, "w1_up": 0, "w1_down": 0, "w_in": 1, "ssm_w_glu": 0, "w_out": 0,
        "w2_gate": 0, "w2_up": 0, "w2_down": 0, "w_ple_gate": 0, "w_ple_proj": 1}
    big_names = list(big)

    def view(a, k):
        return a[0].T if k in transposed else a[0]

    def unview(a, k):
        return a.T[None] if k in transposed else a[None]

    shard = {k: _pad_to(view(weights[k], k).astype(BF16), big[k], LANE) for k in big_names}
    W = {}

    abar_r, abar_i, bbar_r, bbar_i = _ssm_discretize(ssm_log_dt[0], ssm_a_re[0], ssm_a_im[0], ssm_b_re[0], ssm_b_im[0])
    bc_r = _block_diag(jnp.swapaxes(bbar_r, 1, 2)).astype(BF16)
    bc_i = _block_diag(jnp.swapaxes(bbar_i, 1, 2)).astype(BF16)
    cc_r = _block_diag(jnp.swapaxes(ssm_c_re[0], 1, 2)).astype(BF16)
    cc_i = _block_diag(jnp.swapaxes(ssm_c_im[0], 1, 2)).astype(BF16)
    apw_f = _scan_constants(abar_r, abar_i, False)
    apw_b = _scan_constants(abar_r, abar_i, True)
    causal = jnp.tril(jnp.ones((CHUNK, CHUNK), dtype=bool))
    wm = jnp.where(causal[None], gmlp_w_s[0], 0.0).astype(BF16)
    wmt = jnp.swapaxes(wm, 1, 2)
    bs = gmlp_b_s[0][:, :, None]

    groups = [["w1_gate"], ["w1_up"], ["w1_down"], ["w_in", "ssm_w_glu", "w_out"], ["w2_gate"], ["w2_up"],
              ["w2_down", "w_ple_gate", "w_ple_proj"]]
    order = [k for g in groups for k in g]
    place = {k: i for i, k in enumerate(order)}
    me = (4 * lax.axis_index("x") + 2 * lax.axis_index("y") + lax.axis_index("c")).astype(jnp.int32).reshape(1)
    size = {k: shard[k].shape[big[k]] for k in order}
    landing, sems = gather_start([place_own_block(shard[k], big[k], me, "place_" + k) for k in order],
                                 [big[k] for k in order], [size[k] for k in order],
                                 [[place[k] for k in g] for g in groups], "gather_start")

    def arrive(g, after):
        axes_g, sizes_g = [big[k] for k in groups[g]], [size[k] for k in groups[g]]
        landed = gather_wait([landing[place[k]] for k in groups[g]], axes_g, sizes_g, *sems[g], after,
                             "gather_wait_%d" % g)
        W.update(zip(groups[g], gather_forward(landed, axes_g, sizes_g, "gather_forward_%d" % g)))

    xn1 = rmsnorm_fwd(xs, norm_ffn1, "norm_ffn1")
    arrive(0, xn1)
    gate1 = matmul(xn1, W["w1_gate"], "nt", "ffn1_gate")
    arrive(1, gate1)
    up1, act1 = ffn_up(xn1, W["w1_up"], gate1, "ffn1_up")
    arrive(2, act1)
    h1 = matmul(act1, W["w1_down"], "nn", "ffn1_down", res=xs, scale=0.5)
    arrive(3, h1)
    xn2 = rmsnorm_fwd(h1, norm_mix, "norm_mix")
    z = matmul(xn2, W["w_in"], "nn", "proj_in")
    y_pre, yg, sr, si = s5_fwd(z, bc_r, bc_i, cc_r, cc_i, apw_f, ssm_d, "s5_fwd")
    glin = matmul(yg, W["ssm_w_glu"], "nn", "ssm_glu")
    y_gmlp = gmlp_fwd(z, gmlp_norm_v, wm, bs, "gmlp_fwd")
    ycat = mix_out_fwd(y_pre, glin, y_gmlp, norm_ssm_out, norm_gmlp_out, "mix_out")
    h2 = matmul(ycat, W["w_out"], "nn", "proj_out", res=h1)
    arrive(4, h2)
    xn3 = rmsnorm_fwd(h2, norm_ffn2, "norm_ffn2")
    gate2 = matmul(xn3, W["w2_gate"], "nt", "ffn2_gate")
    arrive(5, gate2)
    up2, act2 = ffn_up(xn3, W["w2_up"], gate2, "ffn2_up")
    arrive(6, act2)
    h3 = matmul(act2, W["w2_down"], "nn", "ffn2_down", res=h2, scale=0.5)
    xn4 = rmsnorm_fwd(h3, norm_ple, "norm_ple")
    pg_lin = matmul(xn4, W["w_ple_gate"], "nn", "ple_gate")
    pp = matmul(ps, W["w_ple_proj"], "nn", "ple_proj")
    h4 = ple_fwd(h3, pg_lin, pp, "ple_fwd")
    dh4, loss_part, g_norm_final = final_loss(h4, tgt, norm_final.reshape(1, -1), "final_loss")
    loss = lax.psum(loss_part[0, 0], ("x", "y", "c"))

    G = {}
    reduced = {}
    chip_part = {}
    wait_sibling, wait_chips = [], []
    core = lax.axis_index("c").astype(jnp.int32).reshape(1)

    def grad(name_, value):
        G[name_] = value
        wait_sibling.append(name_)

    def carry(fn, *a, levels="ab", extra=None, **kw):
        tasks, kinds = [], []
        if extra is not None:
            tasks.append(extra[0])
            kinds.append(("x", extra[1]))
        if "a" in levels and wait_sibling:
            group = list(wait_sibling)
            wait_sibling.clear()
            tasks.append(to_sibling_task([G[k] for k in group], [big[k] for k in group]))
            kinds.append(("a", group))
        if "b" in levels and wait_chips:
            group = list(wait_chips)
            wait_chips.clear()
            tasks.append(across_chips_task([chip_part[k] for k in group]))
            kinds.append(("b", group))
        if not tasks:
            return fn(*a, **kw)
        out, task_outs = fn(*a, tasks=tasks, **kw)
        for (kind, group), outs in zip(kinds, task_outs):
            if kind == "x":
                group(outs)
                continue
            for k, r in zip(group, outs):
                if kind == "a":
                    chip_part[k] = rs_chip_sum(G[k], r, big[k], core, "rs_sum_" + k)
                    wait_chips.append(k)
                else:
                    reduced[k] = r
        return out

    small = {}
    small["norm_final"] = g_norm_final
    dpp, dpg = ple_bwd(dh4, pg_lin, pp, "ple_bwd")
    grad("w_ple_proj", matmul(ps, dpp, "tn", "grad_ple_proj", out_dtype=BF16))
    grad("w_ple_gate", carry(matmul, xn4, dpg, "tn", "grad_ple_gate", out_dtype=BF16))
    dxn4 = carry(matmul, dpg, W["w_ple_gate"], "nt", "ple_gate_bwd")
    dh3, dh3b, small["norm_ple"] = rmsnorm_bwd(dxn4, h3, norm_ple, dh4, "norm_ple_bwd")

    def ffn_bwd(tag, dhb, xn, gate, up, act, wg, wu, wd, extra=None):
        dgate, dup = carry(ffn_bwd_act, dhb, W[wd], gate, up, tag + "_act_bwd", extra=extra)
        grad(wd, carry(matmul, act, dhb, "tn", tag + "_grad_down", out_dtype=BF16, scale=0.5))
        grad(wg, carry(matmul, dgate, xn, "tn", tag + "_grad_gate", out_dtype=BF16))
        grad(wu, carry(matmul, dup, xn, "tn", tag + "_grad_up", out_dtype=BF16))
        dxn = carry(matmul, dgate, W[wg], "nn", tag + "_gate_bwd")
        return carry(matmul, dup, W[wu], "nn", tag + "_up_bwd", res=dxn)

    dxn3 = ffn_bwd("ffn2", dh3b, xn3, gate2, up2, act2, "w2_gate", "w2_up", "w2_down")
    dh2, dh2b, small["norm_ffn2"] = rmsnorm_bwd(dxn3, h2, norm_ffn2, dh3, "norm_ffn2_bwd")

    grad("w_out", matmul(ycat, dh2b, "tn", "grad_out", out_dtype=BF16))
    dycat = carry(matmul, dh2b, W["w_out"], "nt", "proj_out_bwd")
    dyg_direct, dglin, dy_gmlp, small["norm_ssm_out"], small["norm_gmlp_out"] = mix_out_bwd(
        dycat, y_pre, glin, y_gmlp, norm_ssm_out, norm_gmlp_out, "mix_out_bwd")
    grad("ssm_w_glu", matmul(yg, dglin, "tn", "grad_glu", out_dtype=BF16))
    dyg = carry(matmul, dglin, W["ssm_w_glu"], "nt", "ssm_glu_bwd", res=dyg_direct, levels="a")
    du, small["ssm_d"], gc_r, gc_i, gb_r, gb_i, ga_r, ga_i = carry(
        s5_bwd, dyg, y_pre, z, sr, si, bc_r, bc_i, cc_r, cc_i, apw_b, ssm_d, "s5_bwd")
    dzu, dzv, small["gmlp_norm_v"], g_wm, g_bs = gmlp_bwd(dy_gmlp, z, gmlp_norm_v, wm, wmt, bs, "gmlp_bwd")
    small["gmlp_w_s"] = g_wm
    small["gmlp_b_s"] = g_bs
    small["c_re"] = _block_diag_extract(gc_r, SSM_GROUP, SSM_STATE)
    small["c_im"] = _block_diag_extract(gc_i, SSM_GROUP, SSM_STATE)
    small["bbar_r"] = jnp.swapaxes(_block_diag_extract(gb_r, SSM_GROUP, SSM_STATE), 1, 2)
    small["bbar_i"] = jnp.swapaxes(_block_diag_extract(gb_i, SSM_GROUP, SSM_STATE), 1, 2)
    small["abar_r"] = jnp.sum(ga_r, axis=0).reshape(n_groups, SSM_STATE)
    small["abar_i"] = jnp.sum(ga_i, axis=0).reshape(n_groups, SSM_STATE)

    dz = jnp.concatenate([du, dzu, dzv], axis=1)
    grad("w_in", matmul(xn2, dz, "tn", "grad_in", out_dtype=BF16))
    dxn2 = carry(matmul, dz, W["w_in"], "nt", "proj_in_bwd")
    dh1, dh1b, small["norm_mix"] = rmsnorm_bwd(dxn2, h1, norm_mix, dh2, "norm_mix_bwd")

    def pack(parts):
        flat = jnp.concatenate([v.reshape(-1) for v in parts.values()])
        return _pad_to(flat, 0, SUBLANE * LANE).reshape(-1, LANE), flat.shape[0]

    def unpack(everyones, n, parts, tag):
        rows = everyones.shape[0] // N_DEV
        summed = sum_devices(everyones.reshape(N_DEV, rows, LANE), "sum_" + tag).reshape(-1)[:n]
        out, off = {}, 0
        for k, v in parts.items():
            out[k] = summed[off:off + v.size].reshape(v.shape)
            off += v.size
        return out

    early = dict(small)
    flat_early, n_early = pack(early)
    landed = []
    dxn1 = ffn_bwd("ffn1", dh1b, xn1, gate1, up1, act1, "w1_gate", "w1_up", "w1_down",
                   extra=(gather_task([flat_early], [0]), landed.extend))
    tot = unpack(landed[0], n_early, early, "small")
    grad_x, _, g_norm_ffn1 = rmsnorm_bwd(dxn1, xs, norm_ffn1, dh1, "norm_ffn1_bwd")
    assert not wait_sibling and not wait_chips and set(reduced) == set(big_names)
    last = {"norm_ffn1": g_norm_ffn1}
    flat_last, n_last = pack(last)
    ((everyones_last,),) = run_tasks([gather_task([flat_last], [0])], "gather_last")
    tot.update(unpack(everyones_last, n_last, last, "last"))

    out_g, out_d, out_m, out_v = {}, {}, {}, {}
    for k in big_names:
        g, dl, nm, nv = adam_sharded(reduced[k], view(weights[k], k), view(moments_m[k], k), view(moments_v[k], k),
                                     "adam_" + k)
        out_g[k], out_d[k], out_m[k], out_v[k] = unview(g, k), unview(dl, k), unview(nm, k), unview(nv, k)

    _, ssm_vjp = jax.vjp(_ssm_discretize, ssm_log_dt[0], ssm_a_re[0], ssm_a_im[0], ssm_b_re[0], ssm_b_im[0])
    g_log_dt, g_a_re, g_a_im, g_b_re, g_b_im = ssm_vjp((tot["abar_r"], tot["abar_i"], tot["bbar_r"], tot["bbar_i"]))
    small_grads = {
        "norm_ffn1": tot["norm_ffn1"], "norm_mix": tot["norm_mix"], "ssm_log_dt": g_log_dt, "ssm_a_re": g_a_re,
        "ssm_a_im": g_a_im, "ssm_b_re": g_b_re, "ssm_b_im": g_b_im, "ssm_c_re": tot["c_re"], "ssm_c_im": tot["c_im"],
        "ssm_d": tot["ssm_d"], "gmlp_norm_v": tot["gmlp_norm_v"],
        "gmlp_w_s": jnp.where(causal[None], tot["gmlp_w_s"], 0.0), "gmlp_b_s": tot["gmlp_b_s"],
        "norm_ssm_out": tot["norm_ssm_out"], "norm_gmlp_out": tot["norm_gmlp_out"], "norm_ffn2": tot["norm_ffn2"],
        "norm_ple": tot["norm_ple"], "norm_final": tot["norm_final"]}
    for k, g in small_grads.items():
        shp = weights[k].shape
        g2 = _as2d(g.reshape(shp))
        dl, nm, nv = adam_small(g2, _as2d(weights[k]), _as2d(moments_m[k]), _as2d(moments_v[k]), "adam_" + k)
        out_g[k], out_d[k], out_m[k], out_v[k] = g2.reshape(shp), dl.reshape(shp), nm.reshape(shp), nv.reshape(shp)

    return (loss, grad_x[None], *[out_g[k] for k in names], *[out_d[k] for k in names],
            *[out_m[k] for k in names], *[out_v[k] for k in names])
```

```python
import math

import jax
import jax.numpy as jnp
from jax import lax
from jax.experimental import pallas as pl
from jax.experimental.pallas import tpu as pltpu

F32 = jnp.float32
BF16 = jnp.bfloat16
MESH_DT = pl.DeviceIdType.MESH

N_DEV = 8
N_CHIP = 4
LANE = 128
SUBLANE = 8
VMEM_LIMIT = 56 * 1024 * 1024

EPS = 1e-6
SSM_GROUP = 16
SSM_STATE = 64
GROUPS_PER_BLOCK = LANE // SSM_GROUP
STATE_BLOCK = GROUPS_PER_BLOCK * SSM_STATE
GMLP_HEAD = 128
CHUNK = 128

ADAM_LR = 0.001
ADAM_B1 = 0.9
ADAM_B2 = 0.999
ADAM_EPS = 1e-08
ADAM_WD = 0.01
ADAM_STEP = 10

GELU_K = math.sqrt(2.0 / math.pi)
GELU_C = 0.044715


def _cparams():
    return pltpu.CompilerParams(vmem_limit_bytes=VMEM_LIMIT)


def _tile(n, pref):
    if n <= pref:
        return n
    t = (pref // LANE) * LANE
    while t > 0:
        if n % t == 0:
            return t
        t -= LANE
    return n


def _row_tile(n, pref):
    if n <= pref:
        return n
    t = (pref // SUBLANE) * SUBLANE
    while t > 0:
        if n % t == 0:
            return t
        t -= SUBLANE
    return n


def _gelu(x):
    t = jnp.tanh(GELU_K * (x + GELU_C * x * x * x))
    return 0.5 * x * (1.0 + t)


def _gelu_grad(x):
    t = jnp.tanh(GELU_K * (x + GELU_C * x * x * x))
    return 0.5 * (1.0 + t) + 0.5 * x * (1.0 - t * t) * (GELU_K * (1.0 + 3.0 * GELU_C * x * x))


def _sigmoid(x):
    return 0.5 * jnp.tanh(0.5 * x) + 0.5


_DN = {
    "nn": (((1,), (0,)), ((), ())),
    "nt": (((1,), (1,)), ((), ())),
    "tn": (((0,), (0,)), ((), ())),
}


def _dot(a, b, mode="nn"):
    return lax.dot_general(a, b, _DN[mode], preferred_element_type=F32)


class CommTask:
    def __init__(self, inputs, out_shape, n_sems, start, late, finish, in_place=False):
        self.inputs, self.out_shape, self.n_sems = list(inputs), list(out_shape), n_sems
        self.start, self.late, self.finish = start, late, finish
        self.in_place = in_place


def _task_aliases(tasks, first_in, first_out):
    aliases = {}
    for t in tasks:
        if t.in_place:
            aliases.update({first_in + i: first_out + i for i in range(len(t.inputs))})
        first_in += len(t.inputs)
        first_out += len(t.out_shape)
    return aliases


def _call(body, *, name, grid, in_specs, out_specs, out_shape, args, scratch_shapes=(), tasks=()):
    in_specs, out_specs, out_shape = list(in_specs), list(out_specs), list(out_shape)
    scratch_shapes = list(scratch_shapes)
    if not tasks:
        return pl.pallas_call(
            body, name=name, grid=grid, in_specs=in_specs, out_specs=out_specs, out_shape=out_shape,
            scratch_shapes=scratch_shapes, compiler_params=_cparams())(*args)
    n_in, n_out, n_scr = len(in_specs), len(out_specs), len(scratch_shapes)
    t_in = [len(t.inputs) for t in tasks]
    t_out = [len(t.out_shape) for t in tasks]
    late_step = grid[0] - max(1, grid[0] // 4)
    has_late = grid[0] >= 2

    def carried(*refs):
        pos = n_in
        task_ins = []
        for k in t_in:
            task_ins.append(refs[pos:pos + k])
            pos += k
        outs = refs[pos:pos + n_out]
        pos += n_out
        task_outs = []
        for k in t_out:
            task_outs.append(refs[pos:pos + k])
            pos += k
        scratch = refs[pos:pos + n_scr]
        pos += n_scr
        sems = [refs[pos + 3 * i:pos + 3 * i + 3] for i in range(len(tasks))]
        ids = [pl.program_id(d) for d in range(len(grid))]
        rest_zero = True
        for d in range(1, len(grid)):
            rest_zero = jnp.logical_and(rest_zero, ids[d] == 0)
        first = jnp.logical_and(ids[0] == 0, rest_zero)
        last = ids[0] == grid[0] - 1
        for d in range(1, len(grid)):
            last = jnp.logical_and(last, ids[d] == grid[d] - 1)

        @pl.when(first)
        def _():
            for t, ti, to, s in zip(tasks, task_ins, task_outs, sems):
                t.start(ti, to, *s)

        if has_late:
            @pl.when(jnp.logical_and(ids[0] == late_step, rest_zero))
            def _():
                for t, ti, to, s in zip(tasks, task_ins, task_outs, sems):
                    t.late(ti, to, *s)

        body(*refs[:n_in], *outs, *scratch)

        @pl.when(last)
        def _():
            for t, ti, to, s in zip(tasks, task_ins, task_outs, sems):
                if not has_late:
                    t.late(ti, to, *s)
                t.finish(ti, to, *s)

    any_spec = pl.BlockSpec(memory_space=pl.ANY)
    sem_shapes = [pltpu.SemaphoreType.DMA((n,)) for t in tasks for n in t.n_sems]
    res = pl.pallas_call(
        carried, name=name, grid=grid,
        in_specs=in_specs + [any_spec] * sum(t_in), out_specs=out_specs + [any_spec] * sum(t_out),
        out_shape=out_shape + [s for t in tasks for s in t.out_shape],
        input_output_aliases=_task_aliases(tasks, n_in, n_out),
        scratch_shapes=scratch_shapes + sem_shapes, compiler_params=_cparams(),
    )(*args, *[a for t in tasks for a in t.inputs])
    res = list(res)
    task_res, pos = [], n_out
    for k in t_out:
        task_res.append(res[pos:pos + k])
        pos += k
    return res[:n_out], task_res


def _mm_dims(a, b, mode):
    if mode == "nn":
        (m, k), (k2, n) = a.shape, b.shape
    elif mode == "nt":
        (m, k), (n, k2) = a.shape, b.shape
    else:
        (k, m), (k2, n) = a.shape, b.shape
    assert k == k2, (a.shape, b.shape, mode)
    return m, n, k


def _mm_specs(mode, tm, tn, tk):
    if mode == "tn":
        a_spec = pl.BlockSpec((tk, tm), lambda i, j, k: (k, i))
    else:
        a_spec = pl.BlockSpec((tm, tk), lambda i, j, k: (i, k))
    if mode == "nt":
        b_spec = pl.BlockSpec((tn, tk), lambda i, j, k: (j, k))
    else:
        b_spec = pl.BlockSpec((tk, tn), lambda i, j, k: (k, j))
    return a_spec, b_spec


def _accumulate(acc, nk, partial, emit):
    if nk == 1:
        emit(partial)
        return
    kk = pl.program_id(2)

    @pl.when(kk == 0)
    def _():
        acc[...] = partial

    @pl.when(kk > 0)
    def _():
        acc[...] += partial

    @pl.when(kk == nk - 1)
    def _():
        emit(acc[...])


def matmul(a, b, mode, name, out_dtype=F32, res=None, scale=1.0, tm=1024, tn=1024, tk=2048, tasks=()):
    m, n, k = _mm_dims(a, b, mode)
    tm, tn, tk = _tile(m, tm), _tile(n, tn), _tile(k, tk)
    nk = k // tk
    a_spec, b_spec = _mm_specs(mode, tm, tn, tk)
    o_spec = pl.BlockSpec((tm, tn), lambda i, j, k: (i, j))
    has_res = res is not None

    def body(*refs):
        if has_res:
            a_ref, b_ref, r_ref, o_ref, acc = refs
        else:
            a_ref, b_ref, o_ref, acc = refs

        def emit(v):
            if scale != 1.0:
                v = v * scale
            if has_res:
                v = r_ref[...] + v
            o_ref[...] = v.astype(out_dtype)

        _accumulate(acc, nk, _dot(a_ref[...], b_ref[...], mode), emit)

    out = _call(
        body, name=name, grid=(m // tm, n // tn, nk),
        in_specs=[a_spec, b_spec] + ([o_spec] if has_res else []), out_specs=[o_spec],
        out_shape=[jax.ShapeDtypeStruct((m, n), out_dtype)], args=(a, b) + ((res,) if has_res else ()),
        scratch_shapes=[pltpu.VMEM((tm, tn) if nk > 1 else (SUBLANE, LANE), F32)], tasks=tasks)
    return (out[0][0], out[1]) if tasks else out[0]


def ffn_up(xn, wu, gate, name, tm=1024, tn=1024, tk=2048, tasks=()):
    m, n, k = _mm_dims(xn, wu, "nt")
    tm, tn, tk = _tile(m, tm), _tile(n, tn), _tile(k, tk)
    nk = k // tk
    a_spec, b_spec = _mm_specs("nt", tm, tn, tk)
    o_spec = pl.BlockSpec((tm, tn), lambda i, j, k: (i, j))

    def body(a_ref, u_ref, gate_ref, up_ref, act_ref, acc):
        def emit(u):
            g = gate_ref[...]
            up_ref[...] = u
            act_ref[...] = (g * _sigmoid(g) * u).astype(BF16)

        _accumulate(acc, nk, _dot(a_ref[...], u_ref[...], "nt"), emit)

    out = _call(
        body, name=name, grid=(m // tm, n // tn, nk), in_specs=[a_spec, b_spec, o_spec],
        out_specs=[o_spec, o_spec],
        out_shape=[jax.ShapeDtypeStruct((m, n), F32), jax.ShapeDtypeStruct((m, n), BF16)],
        args=(xn, wu, gate), scratch_shapes=[pltpu.VMEM((tm, tn) if nk > 1 else (SUBLANE, LANE), F32)], tasks=tasks)
    return (tuple(out[0]), out[1]) if tasks else tuple(out)


def ffn_bwd_act(dh, wd, gate, up, name, tm=1024, tn=512, tk=2048, tasks=()):
    m, n, k = _mm_dims(dh, wd, "nt")
    tm, tn, tk = _tile(m, tm), _tile(n, tn), _tile(k, tk)
    nk = k // tk
    a_spec, b_spec = _mm_specs("nt", tm, tn, tk)
    o_spec = pl.BlockSpec((tm, tn), lambda i, j, k: (i, j))

    def body(a_ref, b_ref, gate_ref, up_ref, dg_ref, du_ref, acc):
        def emit(total):
            dact = 0.5 * total
            g = gate_ref[...]
            sg = _sigmoid(g)
            du_ref[...] = (dact * (g * sg)).astype(BF16)
            dg_ref[...] = (dact * up_ref[...] * (sg * (1.0 + g * (1.0 - sg)))).astype(BF16)

        _accumulate(acc, nk, _dot(a_ref[...], b_ref[...], "nt"), emit)

    out = _call(
        body, name=name, grid=(m // tm, n // tn, nk), in_specs=[a_spec, b_spec, o_spec, o_spec],
        out_specs=[o_spec, o_spec],
        out_shape=[jax.ShapeDtypeStruct((m, n), BF16), jax.ShapeDtypeStruct((m, n), BF16)],
        args=(dh, wd, gate, up), scratch_shapes=[pltpu.VMEM((tm, tn) if nk > 1 else (SUBLANE, LANE), F32)],
        tasks=tasks)
    return (tuple(out[0]), out[1]) if tasks else tuple(out)


def _rows(t, d, tr):
    return pl.BlockSpec((tr, d), lambda i: (i, 0))


def _vec(d):
    return pl.BlockSpec((1, d), lambda i: (0, 0))


def rmsnorm_fwd(x, g, name, tr=512):
    t, d = x.shape
    tr = _row_tile(t, tr)

    def body(x_ref, g_ref, o_ref):
        xf = x_ref[...]
        r = lax.rsqrt(jnp.mean(xf * xf, axis=-1, keepdims=True) + EPS)
        o_ref[...] = (xf * r * g_ref[...]).astype(BF16)

    return pl.pallas_call(
        body, name=name, grid=(t // tr,), in_specs=[_rows(t, d, tr), _vec(d)], out_specs=_rows(t, d, tr),
        out_shape=jax.ShapeDtypeStruct((t, d), BF16), compiler_params=_cparams(),
    )(x, g)


def _rms_bwd(dxn, xf, g):
    r = lax.rsqrt(jnp.mean(xf * xf, axis=-1, keepdims=True) + EPS)
    xhat = xf * r
    dg = jnp.sum(dxn * xhat, axis=0, keepdims=True)
    dxh = dxn * g
    dx = r * (dxh - xhat * jnp.mean(dxh * xhat, axis=-1, keepdims=True))
    return dx, dg


def rmsnorm_bwd(dxn, x, g, dres, name, tr=256):
    t, d = x.shape
    tr = _row_tile(t, tr)

    def body(dxn_ref, x_ref, g_ref, dres_ref, o_ref, ob_ref, dg_ref):
        dx, dg = _rms_bwd(dxn_ref[...], x_ref[...], g_ref[...])
        out = dres_ref[...] + dx
        o_ref[...] = out
        ob_ref[...] = out.astype(BF16)

        @pl.when(pl.program_id(0) == 0)
        def _():
            dg_ref[...] = jnp.zeros_like(dg_ref)

        dg_ref[...] += dg

    return pl.pallas_call(
        body, name=name, grid=(t // tr,),
        in_specs=[_rows(t, d, tr), _rows(t, d, tr), _vec(d), _rows(t, d, tr)],
        out_specs=[_rows(t, d, tr), _rows(t, d, tr), _vec(d)],
        out_shape=[jax.ShapeDtypeStruct((t, d), F32), jax.ShapeDtypeStruct((t, d), BF16),
                   jax.ShapeDtypeStruct((1, d), F32)],
        compiler_params=_cparams(),
    )(dxn, x, g, dres)


def final_loss(h, target, g, name, tr=256):
    t, d = h.shape
    tr = _row_tile(t, tr)

    def body(h_ref, t_ref, g_ref, dh_ref, loss_ref, dg_ref):
        xf = h_ref[...]
        gg = g_ref[...]
        r = lax.rsqrt(jnp.mean(xf * xf, axis=-1, keepdims=True) + EPS)
        xhat = xf * r
        e = xhat * gg - t_ref[...]
        part = jnp.sum(jnp.sum(e * e, axis=1, keepdims=True), axis=0, keepdims=True) * (0.5 / d)
        dout = e * (1.0 / d)
        dg = jnp.sum(dout * xhat, axis=0, keepdims=True)
        dxh = dout * gg
        dh_ref[...] = r * (dxh - xhat * jnp.mean(dxh * xhat, axis=-1, keepdims=True))

        @pl.when(pl.program_id(0) == 0)
        def _():
            dg_ref[...] = jnp.zeros_like(dg_ref)
            loss_ref[...] = jnp.zeros_like(loss_ref)

        dg_ref[...] += dg
        loss_ref[...] += jnp.broadcast_to(part, loss_ref.shape)

    return pl.pallas_call(
        body, name=name, grid=(t // tr,),
        in_specs=[_rows(t, d, tr), _rows(t, d, tr), _vec(d)],
        out_specs=[_rows(t, d, tr), pl.BlockSpec((SUBLANE, LANE), lambda i: (0, 0)), _vec(d)],
        out_shape=[jax.ShapeDtypeStruct((t, d), F32), jax.ShapeDtypeStruct((SUBLANE, LANE), F32),
                   jax.ShapeDtypeStruct((1, d), F32)],
        compiler_params=_cparams(),
    )(h, target, g)


def ple_fwd(h, glin, pp, name, tr=512):
    t, d = h.shape
    tr = _row_tile(t, tr)

    def body(h_ref, gl_ref, pp_ref, o_ref):
        o_ref[...] = h_ref[...] + _sigmoid(gl_ref[...]) * pp_ref[...]

    sp = _rows(t, d, tr)
    return pl.pallas_call(
        body, name=name, grid=(t // tr,), in_specs=[sp, sp, sp], out_specs=sp,
        out_shape=jax.ShapeDtypeStruct((t, d), F32), compiler_params=_cparams(),
    )(h, glin, pp)


def ple_bwd(dh, glin, pp, name, tr=512):
    t, d = dh.shape
    tr = _row_tile(t, tr)

    def body(dh_ref, gl_ref, pp_ref, dpp_ref, dgl_ref):
        gate = _sigmoid(gl_ref[...])
        dh_ = dh_ref[...]
        dpp_ref[...] = (dh_ * gate).astype(BF16)
        dgl_ref[...] = (dh_ * pp_ref[...] * gate * (1.0 - gate)).astype(BF16)

    sp = _rows(t, d, tr)
    return pl.pallas_call(
        body, name=name, grid=(t // tr,), in_specs=[sp, sp, sp], out_specs=[sp, sp],
        out_shape=[jax.ShapeDtypeStruct((t, d), BF16), jax.ShapeDtypeStruct((t, d), BF16)],
        compiler_params=_cparams(),
    )(dh, glin, pp)


def mix_out_fwd(y_pre, glin, y_gmlp, g_so, g_go, name, tr=512):
    t, d = y_pre.shape
    tr = _row_tile(t, tr)

    def body(yp_ref, gl_ref, yg_ref, gs_ref, gg_ref, o_ref):
        ys = _gelu(yp_ref[...]) * _sigmoid(gl_ref[...])
        r = lax.rsqrt(jnp.mean(ys * ys, axis=-1, keepdims=True) + EPS)
        o_ref[:, 0:d] = (ys * r * gs_ref[...]).astype(BF16)
        yq = yg_ref[...]
        r2 = lax.rsqrt(jnp.mean(yq * yq, axis=-1, keepdims=True) + EPS)
        o_ref[:, d:2 * d] = (yq * r2 * gg_ref[...]).astype(BF16)

    sp = _rows(t, d, tr)
    return pl.pallas_call(
        body, name=name, grid=(t // tr,), in_specs=[sp, sp, sp, _vec(d), _vec(d)],
        out_specs=_rows(t, 2 * d, tr), out_shape=jax.ShapeDtypeStruct((t, 2 * d), BF16),
        compiler_params=_cparams(),
    )(y_pre, glin, y_gmlp, g_so, g_go)


def mix_out_bwd(dycat, y_pre, glin, y_gmlp, g_so, g_go, name, tr=256):
    t, d = y_pre.shape
    tr = _row_tile(t, tr)

    def body(dy_ref, yp_ref, gl_ref, yg_ref, gs_ref, gg_ref, dyg_ref, dl_ref, dyq_ref, dgs_ref, dgg_ref):
        yg = _gelu(yp_ref[...])
        sg = _sigmoid(gl_ref[...])
        dys, dgs = _rms_bwd(dy_ref[:, 0:d], yg * sg, gs_ref[...])
        dyg_ref[...] = dys * sg
        dl_ref[...] = (dys * yg * sg * (1.0 - sg)).astype(BF16)
        dyq, dgg = _rms_bwd(dy_ref[:, d:2 * d], yg_ref[...], gg_ref[...])
        dyq_ref[...] = dyq

        @pl.when(pl.program_id(0) == 0)
        def _():
            dgs_ref[...] = jnp.zeros_like(dgs_ref)
            dgg_ref[...] = jnp.zeros_like(dgg_ref)

        dgs_ref[...] += dgs
        dgg_ref[...] += dgg

    sp = _rows(t, d, tr)
    return pl.pallas_call(
        body, name=name, grid=(t // tr,),
        in_specs=[_rows(t, 2 * d, tr), sp, sp, sp, _vec(d), _vec(d)],
        out_specs=[sp, sp, sp, _vec(d), _vec(d)],
        out_shape=[jax.ShapeDtypeStruct((t, d), F32), jax.ShapeDtypeStruct((t, d), BF16),
                   jax.ShapeDtypeStruct((t, d), F32), jax.ShapeDtypeStruct((1, d), F32),
                   jax.ShapeDtypeStruct((1, d), F32)],
        compiler_params=_cparams(),
    )(dycat, y_pre, glin, y_gmlp, g_so, g_go)


SCAN_COLS = 512


def _scan_tile(xr, xi, const, cr, ci, reverse):
    for lvl, sh in enumerate((1, 2, 4)):
        ar, ai = const(2 * lvl), const(2 * lvl + 1)
        s = (SUBLANE - sh) if reverse else sh
        rr = pltpu.roll(xr, s, 0)
        ri = pltpu.roll(xi, s, 0)
        xr, xi = xr + ar * rr - ai * ri, xi + ar * ri + ai * rr
    pr, pi_ = const(6), const(7)
    xr, xi = xr + pr * cr - pi_ * ci, xi + pr * ci + pi_ * cr
    return xr, xi


def _bcast_row(x, row):
    return jnp.broadcast_to(x[row:row + 1, :], x.shape)


def s5_fwd(z, bc_r, bc_i, cc_r, cc_i, apw, dvec, name, tc=256, tasks=()):
    t = z.shape[0]
    nblk = bc_r.shape[0]
    d = nblk * LANE
    ns = nblk * STATE_BLOCK
    tc = _row_tile(t, tc)
    ntile = tc // SUBLANE

    def body(z_ref, br_ref, bi_ref, cr_ref, ci_ref, apw_ref, d_ref, y_ref, yg_ref, sr_ref, si_ref, carry):
        @pl.when(pl.program_id(0) == 0)
        def _():
            carry[...] = jnp.zeros_like(carry)

        for j in range(nblk):
            uj = z_ref[:, j * LANE:(j + 1) * LANE]
            ub = uj.astype(BF16)
            for q in range(STATE_BLOCK // SCAN_COLS):
                c0 = j * STATE_BLOCK + q * SCAN_COLS
                cs = pl.ds(c0, SCAN_COLS)
                bs = slice(q * SCAN_COLS, (q + 1) * SCAN_COLS)
                sr_ref[:, cs] = _dot(ub, br_ref[j, :, bs])
                si_ref[:, cs] = _dot(ub, bi_ref[j, :, bs])
                const = lambda k, cs=cs: apw_ref[k, :, cs]

                def tile(k, c, cs=cs, const=const):
                    rows = pl.ds(pl.multiple_of(k * SUBLANE, SUBLANE), SUBLANE)
                    xr, xi = _scan_tile(sr_ref[rows, cs], si_ref[rows, cs], const, c[0], c[1], False)
                    sr_ref[rows, cs] = xr
                    si_ref[rows, cs] = xi
                    return _bcast_row(xr, SUBLANE - 1), _bcast_row(xi, SUBLANE - 1)

                c_r, c_i = lax.fori_loop(0, ntile, tile, (carry[0, :, cs], carry[1, :, cs]))
                carry[0, :, cs] = c_r
                carry[1, :, cs] = c_i
            sb = pl.ds(j * STATE_BLOCK, STATE_BLOCK)
            y = (_dot(sr_ref[:, sb].astype(BF16), cr_ref[j]) - _dot(si_ref[:, sb].astype(BF16), ci_ref[j])
                 + d_ref[:, j * LANE:(j + 1) * LANE] * uj)
            y_ref[:, j * LANE:(j + 1) * LANE] = y
            yg_ref[:, j * LANE:(j + 1) * LANE] = _gelu(y).astype(BF16)

    full3 = lambda shp: pl.BlockSpec(shp, lambda i: (0, 0, 0))
    out = _call(
        body, name=name, grid=(t // tc,),
        in_specs=[pl.BlockSpec((tc, d), lambda i: (i, 0)), full3(bc_r.shape), full3(bc_i.shape),
                  full3(cc_r.shape), full3(cc_i.shape), full3(apw.shape), _vec(d)],
        out_specs=[pl.BlockSpec((tc, d), lambda i: (i, 0)), pl.BlockSpec((tc, d), lambda i: (i, 0)),
                   pl.BlockSpec((tc, ns), lambda i: (i, 0)), pl.BlockSpec((tc, ns), lambda i: (i, 0))],
        out_shape=[jax.ShapeDtypeStruct((t, d), F32), jax.ShapeDtypeStruct((t, d), BF16),
                   jax.ShapeDtypeStruct((t, ns), F32), jax.ShapeDtypeStruct((t, ns), F32)],
        args=(z, bc_r, bc_i, cc_r, cc_i, apw, dvec), scratch_shapes=[pltpu.VMEM((2, SUBLANE, ns), F32)], tasks=tasks)
    return (tuple(out[0]), out[1]) if tasks else tuple(out)


def s5_bwd(dyg, y_pre, z, sr, si, bc_r, bc_i, cc_r, cc_i, apw_rev, dvec, name, tc=128, tasks=()):
    t = z.shape[0]
    nblk = bc_r.shape[0]
    d = nblk * LANE
    ns = nblk * STATE_BLOCK
    tc = _row_tile(t, tc)
    ntile = tc // SUBLANE
    nchunk = t // tc
    tiles_per_chunk = tc // SUBLANE

    def body(dyg_ref, yp_ref, z_ref, sr_ref, si_ref, pr_ref, pi_ref, br_ref, bi_ref, cr_ref, ci_ref, apw_ref,
             d_ref, du_ref, gd_ref, gcr_ref, gci_ref, gbr_ref, gbi_ref, gar_ref, gai_ref, lr_ref, li_ref, carry):
        step = pl.program_id(0)

        @pl.when(step == 0)
        def _():
            carry[...] = jnp.zeros_like(carry)
            for ref in (gd_ref, gcr_ref, gci_ref, gbr_ref, gbi_ref, gar_ref, gai_ref):
                ref[...] = jnp.zeros_like(ref)

        first_chunk = (step == nchunk - 1).astype(F32)
        keep_prev = 1.0 - first_chunk
        row0 = lax.broadcasted_iota(jnp.int32, (SUBLANE, SCAN_COLS), 0) == 0

        for j in range(nblk):
            lanes = slice(j * LANE, (j + 1) * LANE)
            uj = z_ref[:, lanes]
            ub = uj.astype(BF16)
            gy = dyg_ref[:, lanes] * _gelu_grad(yp_ref[:, lanes])
            gyb = gy.astype(BF16)
            gd_ref[:, lanes] += jnp.sum(gy * uj, axis=0, keepdims=True)
            for q in range(STATE_BLOCK // SCAN_COLS):
                c0 = j * STATE_BLOCK + q * SCAN_COLS
                cs = pl.ds(c0, SCAN_COLS)
                bs = slice(q * SCAN_COLS, (q + 1) * SCAN_COLS)
                lr_ref[:, cs] = _dot(gyb, cr_ref[j, bs, :], "nt")
                li_ref[:, cs] = -_dot(gyb, ci_ref[j, bs, :], "nt")
                const = lambda k, cs=cs: apw_ref[k, :, cs]

                def one_tile(rows, prev_r, prev_i, c, cs=cs, const=const):
                    cr_, ci_, gar, gai = c
                    xr, xi = _scan_tile(lr_ref[rows, cs], li_ref[rows, cs], const, cr_, ci_, True)
                    lr_ref[rows, cs] = xr
                    li_ref[rows, cs] = xi
                    spr = jnp.where(row0, prev_r, pltpu.roll(sr_ref[rows, cs], 1, 0))
                    spi = jnp.where(row0, prev_i, pltpu.roll(si_ref[rows, cs], 1, 0))
                    gar = gar + xr * spr + xi * spi
                    gai = gai + xi * spr - xr * spi
                    return _bcast_row(xr, 0), _bcast_row(xi, 0), gar, gai

                def tile(k, c, cs=cs, one_tile=one_tile):
                    kk = ntile - 1 - k
                    rows = pl.ds(pl.multiple_of(kk * SUBLANE, SUBLANE), SUBLANE)
                    prow = pl.ds(pl.multiple_of((kk - 1) * SUBLANE, SUBLANE), SUBLANE)
                    prev_r = _bcast_row(sr_ref[prow, cs], SUBLANE - 1)
                    prev_i = _bcast_row(si_ref[prow, cs], SUBLANE - 1)
                    return one_tile(rows, prev_r, prev_i, c)

                zero = jnp.zeros((SUBLANE, SCAN_COLS), F32)
                c = lax.fori_loop(0, ntile - 1, tile, (carry[0, :, cs], carry[1, :, cs], zero, zero))
                prev_r = _bcast_row(pr_ref[:, cs], SUBLANE - 1) * keep_prev
                prev_i = _bcast_row(pi_ref[:, cs], SUBLANE - 1) * keep_prev
                c_r, c_i, gar, gai = one_tile(pl.ds(0, SUBLANE), prev_r, prev_i, c)
                carry[0, :, cs] = c_r
                carry[1, :, cs] = c_i
                gar_ref[:, cs] += gar
                gai_ref[:, cs] += gai
            sb = pl.ds(j * STATE_BLOCK, STATE_BLOCK)
            lrb = lr_ref[:, sb].astype(BF16)
            lib = li_ref[:, sb].astype(BF16)
            gcr_ref[j] += _dot(gyb, sr_ref[:, sb].astype(BF16), "tn")
            gci_ref[j] -= _dot(gyb, si_ref[:, sb].astype(BF16), "tn")
            gbr_ref[j] += _dot(ub, lrb, "tn")
            gbi_ref[j] += _dot(ub, lib, "tn")
            du = _dot(lrb, br_ref[j], "nt") + _dot(lib, bi_ref[j], "nt") + gy * d_ref[:, lanes]
            du_ref[:, lanes] = du.astype(BF16)

    rev = lambda i: (nchunk - 1 - i, 0)
    prev = lambda i: (jnp.maximum((nchunk - 1 - i) * tiles_per_chunk - 1, 0), 0)
    full3 = lambda shp: pl.BlockSpec(shp, lambda i: (0, 0, 0))
    acc3 = pl.BlockSpec((nblk, LANE, STATE_BLOCK), lambda i: (0, 0, 0))
    acc_rows = pl.BlockSpec((SUBLANE, ns), lambda i: (0, 0))
    out = _call(
        body, name=name, grid=(nchunk,),
        in_specs=[pl.BlockSpec((tc, d), rev), pl.BlockSpec((tc, d), rev), pl.BlockSpec((tc, d), rev),
                  pl.BlockSpec((tc, ns), rev), pl.BlockSpec((tc, ns), rev),
                  pl.BlockSpec((SUBLANE, ns), prev), pl.BlockSpec((SUBLANE, ns), prev),
                  full3(bc_r.shape), full3(bc_i.shape), full3(cc_r.shape), full3(cc_i.shape), full3(apw_rev.shape),
                  _vec(d)],
        out_specs=[pl.BlockSpec((tc, d), rev), _vec(d), acc3, acc3, acc3, acc3, acc_rows, acc_rows],
        out_shape=[jax.ShapeDtypeStruct((t, d), BF16), jax.ShapeDtypeStruct((1, d), F32)]
        + [jax.ShapeDtypeStruct((nblk, LANE, STATE_BLOCK), F32)] * 4
        + [jax.ShapeDtypeStruct((SUBLANE, ns), F32)] * 2,
        args=(dyg, y_pre, z, sr, si, sr, si, bc_r, bc_i, cc_r, cc_i, apw_rev, dvec),
        scratch_shapes=[pltpu.VMEM((tc, ns), F32), pltpu.VMEM((tc, ns), F32), pltpu.VMEM((2, SUBLANE, ns), F32)],
        tasks=tasks)
    return (tuple(out[0]), out[1]) if tasks else tuple(out)


def _cmul(a, b):
    return a[0] * b[0] - a[1] * b[1], a[0] * b[1] + a[1] * b[0]


def _scan_constants(abar_r, abar_i, reverse):
    ar = abar_r.reshape(1, -1)
    ai = abar_i.reshape(1, -1)
    if reverse:
        ai = -ai
    pw = [(ar, ai)]
    for _ in range(SUBLANE - 1):
        pw.append(_cmul(pw[-1], (ar, ai)))
    rows = lax.broadcasted_iota(jnp.int32, (SUBLANE, 1), 0)
    out = []
    for sh in (1, 2, 4):
        keep = (rows <= SUBLANE - 1 - sh) if reverse else (rows >= sh)
        for part in pw[sh - 1]:
            out.append(jnp.where(keep, part, 0.0))
    for comp in (0, 1):
        stack = jnp.concatenate([pw[k][comp] for k in range(SUBLANE)], axis=0)
        out.append(stack[::-1] if reverse else stack)
    return jnp.stack(out, axis=0).astype(F32)


def _ssm_discretize(log_dt, a_re, a_im, b_re, b_im):
    dt = jnp.exp(log_dt)[:, None]
    lr = jnp.minimum(a_re, -1e-4)
    li = a_im
    mag = jnp.exp(lr * dt)
    ang = li * dt
    abar_r = mag * jnp.cos(ang)
    abar_i = mag * jnp.sin(ang)
    den = lr * lr + li * li
    xr = abar_r - 1.0
    xi = abar_i
    zr = (xr * lr + xi * li) / den
    zi = (xi * lr - xr * li) / den
    bbar_r = zr[..., None] * b_re - zi[..., None] * b_im
    bbar_i = zr[..., None] * b_im + zi[..., None] * b_re
    return abar_r, abar_i, bbar_r, bbar_i


def _block_diag(w):
    g, a, b = w.shape
    nb = g // GROUPS_PER_BLOCK
    eye = jnp.eye(GROUPS_PER_BLOCK, dtype=w.dtype)
    w5 = w.reshape(nb, GROUPS_PER_BLOCK, a, b)
    out = w5[:, :, :, None, :] * eye[None, :, None, :, None]
    return out.reshape(nb, GROUPS_PER_BLOCK * a, GROUPS_PER_BLOCK * b)


def _block_diag_extract(m, a, b):
    nb = m.shape[0]
    eye = jnp.eye(GROUPS_PER_BLOCK, dtype=m.dtype)
    m5 = m.reshape(nb, GROUPS_PER_BLOCK, a, GROUPS_PER_BLOCK, b)
    out = jnp.sum(m5 * eye[None, :, None, :, None], axis=3)
    return out.reshape(nb * GROUPS_PER_BLOCK, a, b)


def _layer_norm(gv, nv):
    mu = jnp.mean(gv, axis=-1, keepdims=True)
    xc = gv - mu
    r = lax.rsqrt(jnp.mean(xc * xc, axis=-1, keepdims=True) + EPS)
    xhat = xc * r
    return xhat * nv, xhat, r


def gmlp_fwd(z, norm_v, wm, bs, name, tr=256):
    t = z.shape[0]
    nh = wm.shape[0]
    d = nh * GMLP_HEAD
    col0 = (z.shape[1] - 2 * d) // d
    tr = _row_tile(t, tr)

    def body(zu_ref, zv_ref, nv_ref, wm_ref, bs_ref, o_ref):
        v, _, _ = _layer_norm(_gelu(zv_ref[...]), nv_ref[...])
        vb = v.astype(BF16)
        u = _gelu(zu_ref[...])
        for c in range(tr // CHUNK):
            rows = slice(c * CHUNK, (c + 1) * CHUNK)
            for h in range(nh):
                cols = slice(h * GMLP_HEAD, (h + 1) * GMLP_HEAD)
                s = _dot(wm_ref[h], vb[rows, cols]) + bs_ref[h]
                o_ref[rows, cols] = u[rows, cols] * s

    return pl.pallas_call(
        body, name=name, grid=(t // tr,),
        in_specs=[pl.BlockSpec((tr, d), lambda i: (i, col0)), pl.BlockSpec((tr, d), lambda i: (i, col0 + 1)),
                  _vec(d), pl.BlockSpec(wm.shape, lambda i: (0, 0, 0)), pl.BlockSpec(bs.shape, lambda i: (0, 0, 0))],
        out_specs=pl.BlockSpec((tr, d), lambda i: (i, 0)),
        out_shape=jax.ShapeDtypeStruct((t, d), F32), compiler_params=_cparams(),
    )(z, z, norm_v, wm, bs)


def gmlp_bwd(dy, z, norm_v, wm, wmt, bs, name, tr=256):
    t = z.shape[0]
    nh = wm.shape[0]
    d = nh * GMLP_HEAD
    col0 = (z.shape[1] - 2 * d) // d
    tr = _row_tile(t, tr)

    def body(dy_ref, zu_ref, zv_ref, nv_ref, wm_ref, wmt_ref, bs_ref, dzu_ref, dzv_ref, dnv_ref, dwm_ref, dbs_ref,
             dv_ref):
        @pl.when(pl.program_id(0) == 0)
        def _():
            dnv_ref[...] = jnp.zeros_like(dnv_ref)
            dwm_ref[...] = jnp.zeros_like(dwm_ref)
            dbs_ref[...] = jnp.zeros_like(dbs_ref)

        zv = zv_ref[...]
        nv = nv_ref[...]
        v, xhat, r = _layer_norm(_gelu(zv), nv)
        vb = v.astype(BF16)
        zu = zu_ref[...]
        u = _gelu(zu)
        dy_ = dy_ref[...]
        for c in range(tr // CHUNK):
            rows = slice(c * CHUNK, (c + 1) * CHUNK)
            for h in range(nh):
                cols = slice(h * GMLP_HEAD, (h + 1) * GMLP_HEAD)
                vh = vb[rows, cols]
                s = _dot(wm_ref[h], vh) + bs_ref[h]
                dyh = dy_[rows, cols]
                dzu_ref[rows, cols] = (dyh * s * _gelu_grad(zu[rows, cols])).astype(BF16)
                ds = dyh * u[rows, cols]
                dsb = ds.astype(BF16)
                dbs_ref[h] += jnp.sum(ds, axis=1, keepdims=True)
                dwm_ref[h] += _dot(dsb, vh, "nt")
                dv_ref[rows, cols] = _dot(wmt_ref[h], dsb)
        dv = dv_ref[...]
        dnv_ref[...] += jnp.sum(dv * xhat, axis=0, keepdims=True)
        dxh = dv * nv
        dgv = r * (dxh - jnp.mean(dxh, axis=-1, keepdims=True) - xhat * jnp.mean(dxh * xhat, axis=-1, keepdims=True))
        dzv_ref[...] = (dgv * _gelu_grad(zv)).astype(BF16)

    full3 = lambda shp: pl.BlockSpec(shp, lambda i: (0, 0, 0))
    rows_d = pl.BlockSpec((tr, d), lambda i: (i, 0))
    return pl.pallas_call(
        body, name=name, grid=(t // tr,),
        in_specs=[rows_d, pl.BlockSpec((tr, d), lambda i: (i, col0)), pl.BlockSpec((tr, d), lambda i: (i, col0 + 1)),
                  _vec(d), full3(wm.shape), full3(wmt.shape), full3(bs.shape)],
        out_specs=[rows_d, rows_d, _vec(d), full3((nh, CHUNK, CHUNK)), full3((nh, CHUNK, 1))],
        out_shape=[jax.ShapeDtypeStruct((t, d), BF16), jax.ShapeDtypeStruct((t, d), BF16),
                   jax.ShapeDtypeStruct((1, d), F32), jax.ShapeDtypeStruct((nh, CHUNK, CHUNK), F32),
                   jax.ShapeDtypeStruct((nh, CHUNK, 1), F32)],
        scratch_shapes=[pltpu.VMEM((tr, d), F32)], compiler_params=_cparams(),
    )(dy, z, z, norm_v, wm, wmt, bs)


def _block(ref, axis, size, k):
    start = pl.multiple_of(k * size, size)
    if axis == 0:
        return ref.at[pl.ds(start, size), :]
    return ref.at[:, pl.ds(start, size)]


def _place():
    x, y, c = lax.axis_index("x"), lax.axis_index("y"), lax.axis_index("c")
    chips = [(1 - x, y), (x, 1 - y), (1 - x, 1 - y)]
    return x, y, c, chips


def _dev(x, y, c):
    return 4 * x + 2 * y + c


def gather_task(shards, axes):
    n = len(shards)
    sizes = [s.shape[ax] for s, ax in zip(shards, axes)]
    out_shape = [
        jax.ShapeDtypeStruct((s.shape[0] * N_DEV, s.shape[1]) if ax == 0 else (s.shape[0], s.shape[1] * N_DEV), s.dtype)
        for s, ax in zip(shards, axes)
    ]

    def copy(ins, outs, send_sems, recv_sems, t, k, block, to, from_input=False):
        dst = _block(outs[t], axes[t], sizes[t], _dev(*block))
        return pltpu.make_async_remote_copy(
            src_ref=ins[t] if from_input else dst, dst_ref=dst,
            send_sem=send_sems.at[t * 7 + k], recv_sem=recv_sems.at[t * 7 + k],
            device_id=to, device_id_type=MESH_DT)

    def local(ins, outs, local_sems, t, me):
        return pltpu.make_async_copy(ins[t], _block(outs[t], axes[t], sizes[t], _dev(*me)), local_sems.at[t])

    def start(ins, outs, send_sems, recv_sems, local_sems):
        x, y, c, chips = _place()
        me, sibling = (x, y, c), (x, y, 1 - c)
        for t in range(n):
            local(ins, outs, local_sems, t, me).start()
        for t in range(n):
            copy(ins, outs, send_sems, recv_sems, t, 0, me, sibling, True).start()
            for j, chip in enumerate(chips):
                copy(ins, outs, send_sems, recv_sems, t, 1 + j, me, (*chip, c), True).start()

    def late(ins, outs, send_sems, recv_sems, local_sems):
        x, y, c, chips = _place()
        me, sibling = (x, y, c), (x, y, 1 - c)
        for t in range(n):
            for j, chip in enumerate(chips):
                copy(ins, outs, send_sems, recv_sems, t, 1 + j, (*chip, c), me).wait_recv()
                copy(ins, outs, send_sems, recv_sems, t, 4 + j, (*chip, c), sibling).start()

    def finish(ins, outs, send_sems, recv_sems, local_sems):
        x, y, c, chips = _place()
        me, sibling = (x, y, c), (x, y, 1 - c)
        for t in range(n):
            copy(ins, outs, send_sems, recv_sems, t, 0, sibling, me).wait_recv()
            for j, chip in enumerate(chips):
                copy(ins, outs, send_sems, recv_sems, t, 4 + j, (*chip, 1 - c), me).wait_recv()
        for t in range(n):
            copy(ins, outs, send_sems, recv_sems, t, 0, me, sibling, True).wait_send()
            for j, chip in enumerate(chips):
                copy(ins, outs, send_sems, recv_sems, t, 1 + j, me, (*chip, c), True).wait_send()
                copy(ins, outs, send_sems, recv_sems, t, 4 + j, (*chip, c), sibling).wait_send()
            local(ins, outs, local_sems, t, me).wait()

    return CommTask(shards, out_shape, (7 * n, 7 * n, n), start, late, finish)


def _blk3(shape2, axis):
    r, c = shape2
    return (r // N_DEV, c) if axis == 0 else (r, c // N_DEV)


def _no_late(ins, outs, send_sems, recv_sems, local_sems):
    pass


def to_sibling_task(grads, axes):
    n = len(grads)
    blks = [_blk3(g.shape, ax) for g, ax in zip(grads, axes)]
    sizes = [b[ax] for b, ax in zip(blks, axes)]

    def copies(ins, outs, send_sems, recv_sems):
        x, y, c, _ = _place()
        return [pltpu.make_async_remote_copy(
            src_ref=_block(ins[t], axes[t], sizes[t], 2 * i + (1 - c)), dst_ref=outs[t].at[i],
            send_sem=send_sems.at[t * N_CHIP + i], recv_sem=recv_sems.at[t * N_CHIP + i],
            device_id=(x, y, 1 - c), device_id_type=MESH_DT) for t in range(n) for i in range(N_CHIP)]

    def start(ins, outs, send_sems, recv_sems, local_sems):
        for cp in copies(ins, outs, send_sems, recv_sems):
            cp.start()

    def finish(ins, outs, send_sems, recv_sems, local_sems):
        cps = copies(ins, outs, send_sems, recv_sems)
        for cp in cps:
            cp.wait_recv()
        for cp in cps:
            cp.wait_send()

    out_shape = [jax.ShapeDtypeStruct((N_CHIP,) + b, g.dtype) for b, g in zip(blks, grads)]
    return CommTask(grads, out_shape, (N_CHIP * n, N_CHIP * n, 1), start, _no_late, finish)


def across_chips_task(parts):
    n = len(parts)

    def copies(ins, outs, send_sems, recv_sems):
        x, y, c, chips = _place()
        my_chip = 2 * x + y
        return [pltpu.make_async_remote_copy(
            src_ref=ins[t].at[2 * chip[0] + chip[1]], dst_ref=outs[t].at[my_chip],
            send_sem=send_sems.at[t * 3 + j], recv_sem=recv_sems.at[t * 3 + j],
            device_id=(*chip, c), device_id_type=MESH_DT) for t in range(n) for j, chip in enumerate(chips)]

    def mine(ins, outs, local_sems):
        x, y, _, _ = _place()
        my_chip = 2 * x + y
        return [pltpu.make_async_copy(ins[t].at[my_chip], outs[t].at[my_chip], local_sems.at[t]) for t in range(n)]

    def start(ins, outs, send_sems, recv_sems, local_sems):
        for cp in mine(ins, outs, local_sems):
            cp.start()
        for cp in copies(ins, outs, send_sems, recv_sems):
            cp.start()

    def finish(ins, outs, send_sems, recv_sems, local_sems):
        cps = copies(ins, outs, send_sems, recv_sems)
        for cp in cps:
            cp.wait_recv()
        for cp in cps:
            cp.wait_send()
        for cp in mine(ins, outs, local_sems):
            cp.wait()

    out_shape = [jax.ShapeDtypeStruct(p.shape, p.dtype) for p in parts]
    return CommTask(parts, out_shape, (3 * n, 3 * n, n), start, _no_late, finish)


def run_tasks(tasks, name):
    t_in = [len(t.inputs) for t in tasks]
    t_out = [len(t.out_shape) for t in tasks]

    def body(*refs):
        pos, views = 0, []
        for k in t_in:
            views.append([refs[pos:pos + k]])
            pos += k
        for v, k in zip(views, t_out):
            v.append(refs[pos:pos + k])
            pos += k
        for i, v in enumerate(views):
            v.extend(refs[pos + 3 * i:pos + 3 * i + 3])
        for phase in ("start", "late", "finish"):
            for t, v in zip(tasks, views):
                getattr(t, phase)(*v)

    any_spec = pl.BlockSpec(memory_space=pl.ANY)
    res = pl.pallas_call(
        body, name=name, in_specs=[any_spec] * sum(t_in), out_specs=[any_spec] * sum(t_out),
        out_shape=[s for t in tasks for s in t.out_shape], input_output_aliases=_task_aliases(tasks, 0, 0),
        scratch_shapes=[pltpu.SemaphoreType.DMA((k,)) for t in tasks for k in t.n_sems],
    )(*[a for t in tasks for a in t.inputs])
    res, out, pos = list(res), [], 0
    for k in t_out:
        out.append(res[pos:pos + k])
        pos += k
    return out


_HBM_SPEC = pl.BlockSpec(memory_space=pl.ANY)
_SEM_SPEC = pl.BlockSpec(memory_space=pltpu.SEMAPHORE)
_DATAFLOW = pltpu.SideEffectType.DATAFLOW_SIDE_EFFECTING


def _full_shape(s, ax):
    return (s.shape[0] * N_DEV, s.shape[1]) if ax == 0 else (s.shape[0], s.shape[1] * N_DEV)


def _level1_copy(src, landing, axis, size, send_sems, recv_sems, slot, sender, to):
    dst = _block(landing, axis, size, _dev(*sender))
    return pltpu.make_async_remote_copy(src_ref=src, dst_ref=dst, send_sem=send_sems.at[slot],
                                        recv_sem=recv_sems.at[slot], device_id=to, device_id_type=MESH_DT)


def place_own_block(shard, axis, me, name, tr=256):
    r, c = shard.shape
    tr = _row_tile(r, tr)
    nrb = r // tr
    if axis == 0:
        o_map = lambda i, me_ref: (me_ref[0] * nrb + i, 0)
    else:
        o_map = lambda i, me_ref: (i, me_ref[0])

    def body(me_ref, x_ref, o_ref):
        o_ref[...] = x_ref[...]

    return pl.pallas_call(
        body, name=name,
        grid_spec=pltpu.PrefetchScalarGridSpec(
            num_scalar_prefetch=1, grid=(nrb,), in_specs=[pl.BlockSpec((tr, c), lambda i, me_ref: (i, 0))],
            out_specs=pl.BlockSpec((tr, c), o_map)),
        out_shape=jax.ShapeDtypeStruct(_full_shape(shard, axis), shard.dtype), compiler_params=_cparams(),
    )(me, shard)


def gather_start(landing, axes, sizes, groups, name):
    n = len(landing)

    def body(*refs):
        lands, sems = refs[:n], refs[2 * n:]
        x, y, c, chips = _place()
        me = (x, y, c)
        targets = [(x, y, 1 - c)] + [(*chip, c) for chip in chips]
        for g, members in enumerate(groups):
            for m, t in enumerate(members):
                own = _block(lands[t], axes[t], sizes[t], _dev(*me))
                for k, to in enumerate(targets):
                    _level1_copy(own, lands[t], axes[t], sizes[t], sems[2 * g], sems[2 * g + 1], 4 * m + k,
                                 me, to).start()

    out = pl.pallas_call(
        body, name=name,
        out_shape=[jax.ShapeDtypeStruct(b.shape, b.dtype) for b in landing]
        + [pltpu.SemaphoreType.DMA((4 * len(members),)) for members in groups for _ in (0, 1)],
        in_specs=[_HBM_SPEC] * n, out_specs=[_HBM_SPEC] * n + [_SEM_SPEC] * (2 * len(groups)),
        input_output_aliases={i: i for i in range(n)},
        compiler_params=pltpu.CompilerParams(has_side_effects=_DATAFLOW),
    )(*landing)
    out = list(out)
    sems = out[n:]
    return out[:n], [(sems[2 * g], sems[2 * g + 1]) for g in range(len(groups))]


def gather_wait(landing, axes, sizes, send_sems, recv_sems, after, name):
    n = len(landing)

    def body(*refs):
        lands = refs[:n]
        send, recv = refs[n], refs[n + 1]
        x, y, c, chips = _place()
        me = (x, y, c)
        peers = [(x, y, 1 - c)] + [(*chip, c) for chip in chips]
        for t in range(n):
            own = _block(lands[t], axes[t], sizes[t], _dev(*me))
            for k, peer in enumerate(peers):
                _level1_copy(own, lands[t], axes[t], sizes[t], send, recv, 4 * t + k, me, peer).wait_send()
                _level1_copy(own, lands[t], axes[t], sizes[t], send, recv, 4 * t + k, peer, me).wait_recv()

    out = pl.pallas_call(
        body, name=name, out_shape=[jax.ShapeDtypeStruct(b.shape, b.dtype) for b in landing],
        in_specs=[_HBM_SPEC] * n + [_SEM_SPEC, _SEM_SPEC, pl.BlockSpec(memory_space=pl.ANY)],
        out_specs=[_HBM_SPEC] * n, input_output_aliases={i: i for i in range(n)},
        compiler_params=pltpu.CompilerParams(has_side_effects=_DATAFLOW),
    )(*landing, send_sems, recv_sems, after)
    return list(out)


def forward_task(landing, axes, sizes):
    n = len(landing)

    def forward(lands, send_sems, recv_sems, t, j, chip_core):
        x, y, c, _ = _place()
        blk = _block(lands[t], axes[t], sizes[t], _dev(*chip_core))
        return pltpu.make_async_remote_copy(src_ref=blk, dst_ref=blk, send_sem=send_sems.at[3 * t + j],
                                            recv_sem=recv_sems.at[3 * t + j], device_id=(x, y, 1 - c),
                                            device_id_type=MESH_DT)

    def start(ins, lands, send_sems, recv_sems, local_sems):
        _, _, c, chips = _place()
        for t in range(n):
            for j, chip in enumerate(chips):
                forward(lands, send_sems, recv_sems, t, j, (*chip, c)).start()

    def finish(ins, lands, send_sems, recv_sems, local_sems):
        _, _, c, chips = _place()
        for t in range(n):
            for j, chip in enumerate(chips):
                forward(lands, send_sems, recv_sems, t, j, (*chip, 1 - c)).wait_recv()
        for t in range(n):
            for j, chip in enumerate(chips):
                forward(lands, send_sems, recv_sems, t, j, (*chip, c)).wait_send()

    out_shape = [jax.ShapeDtypeStruct(b.shape, b.dtype) for b in landing]
    return CommTask(landing, out_shape, (3 * n, 3 * n, 1), start, _no_late, finish, in_place=True)


def rs_chip_sum(grad, recv, axis, core, name, tr=512):
    br, bc = _blk3(grad.shape, axis)
    tr = _row_tile(br, tr)
    nrb = br // tr

    if axis == 0:
        g_map = lambda i, r, c_ref: ((2 * i + c_ref[0]) * nrb + r, 0)
    else:
        g_map = lambda i, r, c_ref: (r, 2 * i + c_ref[0])

    def body(c_ref, g_ref, r_ref, o_ref):
        o_ref[...] = (g_ref[...].astype(F32) + r_ref[...].astype(F32)).astype(BF16)

    return pl.pallas_call(
        body, name=name,
        grid_spec=pltpu.PrefetchScalarGridSpec(
            num_scalar_prefetch=1, grid=(N_CHIP, nrb),
            in_specs=[pl.BlockSpec((tr, bc), g_map), pl.BlockSpec((None, tr, bc), lambda i, r, c_ref: (i, r, 0))],
            out_specs=pl.BlockSpec((None, tr, bc), lambda i, r, c_ref: (i, r, 0))),
        out_shape=jax.ShapeDtypeStruct((N_CHIP, br, bc), BF16), compiler_params=_cparams(),
    )(core, grad, recv)


def _adamw(w, g, m, v):
    m = ADAM_B1 * m + (1.0 - ADAM_B1) * g
    v = ADAM_B2 * v + (1.0 - ADAM_B2) * (g * g)
    m_hat = m / (1.0 - ADAM_B1 ** ADAM_STEP)
    v_hat = v / (1.0 - ADAM_B2 ** ADAM_STEP)
    delta = -ADAM_LR * (m_hat / (jnp.sqrt(v_hat) + ADAM_EPS) + ADAM_WD * w)
    return delta, m, v


def _sum_chips(p_ref):
    g = p_ref[0].astype(F32)
    for i in range(1, N_CHIP):
        g = g + p_ref[i].astype(F32)
    return g


def adam_sharded(parts, w, m, v, name, tr=256):
    r, c = w.shape
    assert parts.shape[2] == c
    tr = _row_tile(r, tr)

    def body(p_ref, w_ref, m_ref, v_ref, g_ref, d_ref, nm_ref, nv_ref):
        g = _sum_chips(p_ref)
        delta, nm, nv = _adamw(w_ref[...], g, m_ref[...], v_ref[...])
        g_ref[...] = g
        d_ref[...] = delta
        nm_ref[...] = nm
        nv_ref[...] = nv

    sp = pl.BlockSpec((tr, c), lambda i: (i, 0))
    return pl.pallas_call(
        body, name=name, grid=(r // tr,),
        in_specs=[pl.BlockSpec((N_CHIP, tr, c), lambda i: (0, i, 0)), sp, sp, sp],
        out_specs=[sp, sp, sp, sp], out_shape=[jax.ShapeDtypeStruct((r, c), F32)] * 4,
        compiler_params=_cparams(),
    )(parts, w, m, v)


def adam_small(g, w, m, v, name):
    def body(g_ref, w_ref, m_ref, v_ref, d_ref, nm_ref, nv_ref):
        delta, nm, nv = _adamw(w_ref[...], g_ref[...], m_ref[...], v_ref[...])
        d_ref[...] = delta
        nm_ref[...] = nm
        nv_ref[...] = nv

    return pl.pallas_call(
        body, name=name, out_shape=[jax.ShapeDtypeStruct(w.shape, F32)] * 3, compiler_params=_cparams(),
    )(g, w, m, v)


def sum_devices(gathered, name, tr=512):
    _, r, c = gathered.shape
    tr = _row_tile(r, tr)

    def body(x_ref, o_ref):
        s = x_ref[0]
        for k in range(1, N_DEV):
            s = s + x_ref[k]
        o_ref[...] = s

    return pl.pallas_call(
        body, name=name, grid=(r // tr,), in_specs=[pl.BlockSpec((N_DEV, tr, c), lambda i: (0, i, 0))],
        out_specs=pl.BlockSpec((tr, c), lambda i: (i, 0)), out_shape=jax.ShapeDtypeStruct((r, c), F32),
        compiler_params=_cparams(),
    )(gathered)


def _pad_to(a, axis, mult):
    size = a.shape[axis]
    pad = (-size) % mult
    if pad == 0:
        return a
    cfg = [(0, 0)] * a.ndim
    cfg[axis] = (0, pad)
    return jnp.pad(a, cfg)


def _as2d(a):
    if a.ndim == 1:
        return a.reshape(1, -1)
    return a.reshape(-1, a.shape[-1])


def kernel(x, p, norm_ffn1, w1_gate, w1_up, w1_down, norm_mix, w_in, ssm_log_dt, ssm_a_re, ssm_a_im, ssm_b_re, ssm_b_im, ssm_c_re, ssm_c_im, ssm_d, ssm_w_glu, gmlp_norm_v, gmlp_w_s, gmlp_b_s, norm_ssm_out, norm_gmlp_out, w_out, norm_ffn2, w2_gate, w2_up, w2_down, norm_ple, w_ple_gate, w_ple_proj, norm_final, loss_target, m_norm_ffn1, m_w1_gate, m_w1_up, m_w1_down, m_norm_mix, m_w_in, m_ssm_log_dt, m_ssm_a_re, m_ssm_a_im, m_ssm_b_re, m_ssm_b_im, m_ssm_c_re, m_ssm_c_im, m_ssm_d, m_ssm_w_glu, m_gmlp_norm_v, m_gmlp_w_s, m_gmlp_b_s, m_norm_ssm_out, m_norm_gmlp_out, m_w_out, m_norm_ffn2, m_w2_gate, m_w2_up, m_w2_down, m_norm_ple, m_w_ple_gate, m_w_ple_proj, m_norm_final, v_norm_ffn1, v_w1_gate, v_w1_up, v_w1_down, v_norm_mix, v_w_in, v_ssm_log_dt, v_ssm_a_re, v_ssm_a_im, v_ssm_b_re, v_ssm_b_im, v_ssm_c_re, v_ssm_c_im, v_ssm_d, v_ssm_w_glu, v_gmlp_norm_v, v_gmlp_w_s, v_gmlp_b_s, v_norm_ssm_out, v_norm_gmlp_out, v_w_out, v_norm_ffn2, v_w2_gate, v_w2_up, v_w2_down, v_norm_ple, v_w_ple_gate, v_w_ple_proj, v_norm_final):
    weights = dict(
        norm_ffn1=norm_ffn1, w1_gate=w1_gate, w1_up=w1_up, w1_down=w1_down, norm_mix=norm_mix, w_in=w_in,
        ssm_log_dt=ssm_log_dt, ssm_a_re=ssm_a_re, ssm_a_im=ssm_a_im, ssm_b_re=ssm_b_re, ssm_b_im=ssm_b_im,
        ssm_c_re=ssm_c_re, ssm_c_im=ssm_c_im, ssm_d=ssm_d, ssm_w_glu=ssm_w_glu, gmlp_norm_v=gmlp_norm_v,
        gmlp_w_s=gmlp_w_s, gmlp_b_s=gmlp_b_s, norm_ssm_out=norm_ssm_out, norm_gmlp_out=norm_gmlp_out, w_out=w_out,
        norm_ffn2=norm_ffn2, w2_gate=w2_gate, w2_up=w2_up, w2_down=w2_down, norm_ple=norm_ple,
        w_ple_gate=w_ple_gate, w_ple_proj=w_ple_proj, norm_final=norm_final)
    moments_m = dict(
        norm_ffn1=m_norm_ffn1, w1_gate=m_w1_gate, w1_up=m_w1_up, w1_down=m_w1_down, norm_mix=m_norm_mix, w_in=m_w_in,
        ssm_log_dt=m_ssm_log_dt, ssm_a_re=m_ssm_a_re, ssm_a_im=m_ssm_a_im, ssm_b_re=m_ssm_b_re, ssm_b_im=m_ssm_b_im,
        ssm_c_re=m_ssm_c_re, ssm_c_im=m_ssm_c_im, ssm_d=m_ssm_d, ssm_w_glu=m_ssm_w_glu, gmlp_norm_v=m_gmlp_norm_v,
        gmlp_w_s=m_gmlp_w_s, gmlp_b_s=m_gmlp_b_s, norm_ssm_out=m_norm_ssm_out, norm_gmlp_out=m_norm_gmlp_out,
        w_out=m_w_out, norm_ffn2=m_norm_ffn2, w2_gate=m_w2_gate, w2_up=m_w2_up, w2_down=m_w2_down,
        norm_ple=m_norm_ple, w_ple_gate=m_w_ple_gate, w_ple_proj=m_w_ple_proj, norm_final=m_norm_final)
    moments_v = dict(
        norm_ffn1=v_norm_ffn1, w1_gate=v_w1_gate, w1_up=v_w1_up, w1_down=v_w1_down, norm_mix=v_norm_mix, w_in=v_w_in,
        ssm_log_dt=v_ssm_log_dt, ssm_a_re=v_ssm_a_re, ssm_a_im=v_ssm_a_im, ssm_b_re=v_ssm_b_re, ssm_b_im=v_ssm_b_im,
        ssm_c_re=v_ssm_c_re, ssm_c_im=v_ssm_c_im, ssm_d=v_ssm_d, ssm_w_glu=v_ssm_w_glu, gmlp_norm_v=v_gmlp_norm_v,
        gmlp_w_s=v_gmlp_w_s, gmlp_b_s=v_gmlp_b_s, norm_ssm_out=v_norm_ssm_out, norm_gmlp_out=v_norm_gmlp_out,
        w_out=v_w_out, norm_ffn2=v_norm_ffn2, w2_gate=v_w2_gate, w2_up=v_w2_up, w2_down=v_w2_down,
        norm_ple=v_norm_ple, w_ple_gate=v_w_ple_gate, w_ple_proj=v_w_ple_proj, norm_final=v_norm_final)
    names = list(weights)

    xs = x[0]
    ps = p[0, 0].astype(BF16)
    tgt = loss_target[0]
    d_model = xs.shape[1]
    d_ssm = d_model // 2
    n_groups = d_ssm // SSM_GROUP

    transposed = ("w1_gate", "w1_up", "w2_gate", "w2_up")
    big = {
        "w1_gate": 0, "w1_up": 0, "w1_down": 0, "w_in": 1, "ssm_w_glu": 0, "w_out": 0,
        "w2_gate": 0, "w2_up": 0, "w2_down": 0, "w_ple_gate": 0, "w_ple_proj": 1}
    big_names = list(big)

    def view(a, k):
        return a[0].T if k in transposed else a[0]

    def unview(a, k):
        return a.T[None] if k in transposed else a[None]

    shard = {k: _pad_to(view(weights[k], k).astype(BF16), big[k], LANE) for k in big_names}
    W = {}

    abar_r, abar_i, bbar_r, bbar_i = _ssm_discretize(ssm_log_dt[0], ssm_a_re[0], ssm_a_im[0], ssm_b_re[0], ssm_b_im[0])
    bc_r = _block_diag(jnp.swapaxes(bbar_r, 1, 2)).astype(BF16)
    bc_i = _block_diag(jnp.swapaxes(bbar_i, 1, 2)).astype(BF16)
    cc_r = _block_diag(jnp.swapaxes(ssm_c_re[0], 1, 2)).astype(BF16)
    cc_i = _block_diag(jnp.swapaxes(ssm_c_im[0], 1, 2)).astype(BF16)
    apw_f = _scan_constants(abar_r, abar_i, False)
    apw_b = _scan_constants(abar_r, abar_i, True)
    causal = jnp.tril(jnp.ones((CHUNK, CHUNK), dtype=bool))
    wm = jnp.where(causal[None], gmlp_w_s[0], 0.0).astype(BF16)
    wmt = jnp.swapaxes(wm, 1, 2)
    bs = gmlp_b_s[0][:, :, None]

    groups = [["w1_gate"], ["w1_up"], ["w1_down"], ["w_in", "ssm_w_glu", "w_out"], ["w2_gate"], ["w2_up"],
              ["w2_down", "w_ple_gate", "w_ple_proj"]]
    order = [k for g in groups for k in g]
    place = {k: i for i, k in enumerate(order)}
    me = (4 * lax.axis_index("x") + 2 * lax.axis_index("y") + lax.axis_index("c")).astype(jnp.int32).reshape(1)
    size = {k: shard[k].shape[big[k]] for k in order}
    landing, sems = gather_start([place_own_block(shard[k], big[k], me, "place_" + k) for k in order],
                                 [big[k] for k in order], [size[k] for k in order],
                                 [[place[k] for k in g] for g in groups], "gather_start")

    def landed(g, after):
        axes_g, sizes_g = [big[k] for k in groups[g]], [size[k] for k in groups[g]]
        bufs = gather_wait([landing[place[k]] for k in groups[g]], axes_g, sizes_g, *sems[g], after,
                           "gather_wait_%d" % g)
        return forward_task(bufs, axes_g, sizes_g)

    def arrive(g, after):
        W.update(zip(groups[g], run_tasks([landed(g, after)], "gather_forward_%d" % g)[0]))

    def arrive_during(g, after, fn, *a, **kw):
        out, (got,) = fn(*a, tasks=[landed(g, after)], **kw)
        W.update(zip(groups[g], got))
        return out

    xn1 = rmsnorm_fwd(xs, norm_ffn1, "norm_ffn1")
    arrive(0, xn1)
    gate1 = matmul(xn1, W["w1_gate"], "nt", "ffn1_gate")
    arrive(1, gate1)
    up1, act1 = ffn_up(xn1, W["w1_up"], gate1, "ffn1_up")
    arrive(2, act1)
    h1 = matmul(act1, W["w1_down"], "nn", "ffn1_down", res=xs, scale=0.5)
    arrive(3, h1)
    xn2 = rmsnorm_fwd(h1, norm_mix, "norm_mix")
    z = matmul(xn2, W["w_in"], "nn", "proj_in")
    y_pre, yg, sr, si = s5_fwd(z, bc_r, bc_i, cc_r, cc_i, apw_f, ssm_d, "s5_fwd")
    glin = matmul(yg, W["ssm_w_glu"], "nn", "ssm_glu")
    y_gmlp = gmlp_fwd(z, gmlp_norm_v, wm, bs, "gmlp_fwd")
    ycat = mix_out_fwd(y_pre, glin, y_gmlp, norm_ssm_out, norm_gmlp_out, "mix_out")
    h2 = arrive_during(4, ycat, matmul, ycat, W["w_out"], "nn", "proj_out", res=h1)
    xn3 = rmsnorm_fwd(h2, norm_ffn2, "norm_ffn2")
    gate2 = arrive_during(5, xn3, matmul, xn3, W["w2_gate"], "nt", "ffn2_gate")
    up2, act2 = arrive_during(6, gate2, ffn_up, xn3, W["w2_up"], gate2, "ffn2_up")
    h3 = matmul(act2, W["w2_down"], "nn", "ffn2_down", res=h2, scale=0.5)
    xn4 = rmsnorm_fwd(h3, norm_ple, "norm_ple")
    pg_lin = matmul(xn4, W["w_ple_gate"], "nn", "ple_gate")
    pp = matmul(ps, W["w_ple_proj"], "nn", "ple_proj")
    h4 = ple_fwd(h3, pg_lin, pp, "ple_fwd")
    dh4, loss_part, g_norm_final = final_loss(h4, tgt, norm_final.reshape(1, -1), "final_loss")
    loss = lax.psum(loss_part[0, 0], ("x", "y", "c"))

    G = {}
    reduced = {}
    chip_part = {}
    wait_sibling, wait_chips = [], []
    core = lax.axis_index("c").astype(jnp.int32).reshape(1)

    def grad(name_, value):
        G[name_] = value
        wait_sibling.append(name_)

    def carry(fn, *a, levels="ab", extra=None, **kw):
        tasks, kinds = [], []
        if extra is not None:
            tasks.append(extra[0])
            kinds.append(("x", extra[1]))
        if "a" in levels and wait_sibling:
            group = list(wait_sibling)
            wait_sibling.clear()
            tasks.append(to_sibling_task([G[k] for k in group], [big[k] for k in group]))
            kinds.append(("a", group))
        if "b" in levels and wait_chips:
            group = list(wait_chips)
            wait_chips.clear()
            tasks.append(across_chips_task([chip_part[k] for k in group]))
            kinds.append(("b", group))
        if not tasks:
            return fn(*a, **kw)
        out, task_outs = fn(*a, tasks=tasks, **kw)
        for (kind, group), outs in zip(kinds, task_outs):
            if kind == "x":
                group(outs)
                continue
            for k, r in zip(group, outs):
                if kind == "a":
                    chip_part[k] = rs_chip_sum(G[k], r, big[k], core, "rs_sum_" + k)
                    wait_chips.append(k)
                else:
                    reduced[k] = r
        return out

    small = {}
    small["norm_final"] = g_norm_final
    dpp, dpg = ple_bwd(dh4, pg_lin, pp, "ple_bwd")
    grad("w_ple_proj", matmul(ps, dpp, "tn", "grad_ple_proj", out_dtype=BF16))
    grad("w_ple_gate", carry(matmul, xn4, dpg, "tn", "grad_ple_gate", out_dtype=BF16))
    dxn4 = carry(matmul, dpg, W["w_ple_gate"], "nt", "ple_gate_bwd")
    dh3, dh3b, small["norm_ple"] = rmsnorm_bwd(dxn4, h3, norm_ple, dh4, "norm_ple_bwd")

    def ffn_bwd(tag, dhb, xn, gate, up, act, wg, wu, wd, extra=None):
        dgate, dup = carry(ffn_bwd_act, dhb, W[wd], gate, up, tag + "_act_bwd", extra=extra)
        grad(wd, carry(matmul, act, dhb, "tn", tag + "_grad_down", out_dtype=BF16, scale=0.5))
        grad(wg, carry(matmul, dgate, xn, "tn", tag + "_grad_gate", out_dtype=BF16))
        grad(wu, carry(matmul, dup, xn, "tn", tag + "_grad_up", out_dtype=BF16))
        dxn = carry(matmul, dgate, W[wg], "nn", tag + "_gate_bwd")
        return carry(matmul, dup, W[wu], "nn", tag + "_up_bwd", res=dxn)

    dxn3 = ffn_bwd("ffn2", dh3b, xn3, gate2, up2, act2, "w2_gate", "w2_up", "w2_down")
    dh2, dh2b, small["norm_ffn2"] = rmsnorm_bwd(dxn3, h2, norm_ffn2, dh3, "norm_ffn2_bwd")

    grad("w_out", matmul(ycat, dh2b, "tn", "grad_out", out_dtype=BF16))
    dycat = carry(matmul, dh2b, W["w_out"], "nt", "proj_out_bwd")
    dyg_direct, dglin, dy_gmlp, small["norm_ssm_out"], small["norm_gmlp_out"] = mix_out_bwd(
        dycat, y_pre, glin, y_gmlp, norm_ssm_out, norm_gmlp_out, "mix_out_bwd")
    grad("ssm_w_glu", matmul(yg, dglin, "tn", "grad_glu", out_dtype=BF16))
    dyg = carry(matmul, dglin, W["ssm_w_glu"], "nt", "ssm_glu_bwd", res=dyg_direct, levels="a")
    du, small["ssm_d"], gc_r, gc_i, gb_r, gb_i, ga_r, ga_i = carry(
        s5_bwd, dyg, y_pre, z, sr, si, bc_r, bc_i, cc_r, cc_i, apw_b, ssm_d, "s5_bwd")
    dzu, dzv, small["gmlp_norm_v"], g_wm, g_bs = gmlp_bwd(dy_gmlp, z, gmlp_norm_v, wm, wmt, bs, "gmlp_bwd")
    small["gmlp_w_s"] = g_wm
    small["gmlp_b_s"] = g_bs
    small["c_re"] = _block_diag_extract(gc_r, SSM_GROUP, SSM_STATE)
    small["c_im"] = _block_diag_extract(gc_i, SSM_GROUP, SSM_STATE)
    small["bbar_r"] = jnp.swapaxes(_block_diag_extract(gb_r, SSM_GROUP, SSM_STATE), 1, 2)
    small["bbar_i"] = jnp.swapaxes(_block_diag_extract(gb_i, SSM_GROUP, SSM_STATE), 1, 2)
    small["abar_r"] = jnp.sum(ga_r, axis=0).reshape(n_groups, SSM_STATE)
    small["abar_i"] = jnp.sum(ga_i, axis=0).reshape(n_groups, SSM_STATE)

    dz = jnp.concatenate([du, dzu, dzv], axis=1)
    grad("w_in", matmul(xn2, dz, "tn", "grad_in", out_dtype=BF16))
    dxn2 = carry(matmul, dz, W["w_in"], "nt", "proj_in_bwd")
    dh1, dh1b, small["norm_mix"] = rmsnorm_bwd(dxn2, h1, norm_mix, dh2, "norm_mix_bwd")

    def pack(parts):
        flat = jnp.concatenate([v.reshape(-1) for v in parts.values()])
        return _pad_to(flat, 0, SUBLANE * LANE).reshape(-1, LANE), flat.shape[0]

    def unpack(everyones, n, parts, tag):
        rows = everyones.shape[0] // N_DEV
        summed = sum_devices(everyones.reshape(N_DEV, rows, LANE), "sum_" + tag).reshape(-1)[:n]
        out, off = {}, 0
        for k, v in parts.items():
            out[k] = summed[off:off + v.size].reshape(v.shape)
            off += v.size
        return out

    early = dict(small)
    flat_early, n_early = pack(early)
    landed = []
    dxn1 = ffn_bwd("ffn1", dh1b, xn1, gate1, up1, act1, "w1_gate", "w1_up", "w1_down",
                   extra=(gather_task([flat_early], [0]), landed.extend))
    tot = unpack(landed[0], n_early, early, "small")
    grad_x, _, g_norm_ffn1 = rmsnorm_bwd(dxn1, xs, norm_ffn1, dh1, "norm_ffn1_bwd")
    assert not wait_sibling and not wait_chips and set(reduced) == set(big_names)
    last = {"norm_ffn1": g_norm_ffn1}
    flat_last, n_last = pack(last)
    ((everyones_last,),) = run_tasks([gather_task([flat_last], [0])], "gather_last")
    tot.update(unpack(everyones_last, n_last, last, "last"))

    out_g, out_d, out_m, out_v = {}, {}, {}, {}
    for k in big_names:
        g, dl, nm, nv = adam_sharded(reduced[k], view(weights[k], k), view(moments_m[k], k), view(moments_v[k], k),
                                     "adam_" + k)
        out_g[k], out_d[k], out_m[k], out_v[k] = unview(g, k), unview(dl, k), unview(nm, k), unview(nv, k)

    _, ssm_vjp = jax.vjp(_ssm_discretize, ssm_log_dt[0], ssm_a_re[0], ssm_a_im[0], ssm_b_re[0], ssm_b_im[0])
    g_log_dt, g_a_re, g_a_im, g_b_re, g_b_im = ssm_vjp((tot["abar_r"], tot["abar_i"], tot["bbar_r"], tot["bbar_i"]))
    small_grads = {
        "norm_ffn1": tot["norm_ffn1"], "norm_mix": tot["norm_mix"], "ssm_log_dt": g_log_dt, "ssm_a_re": g_a_re,
        "ssm_a_im": g_a_im, "ssm_b_re": g_b_re, "ssm_b_im": g_b_im, "ssm_c_re": tot["c_re"], "ssm_c_im": tot["c_im"],
        "ssm_d": tot["ssm_d"], "gmlp_norm_v": tot["gmlp_norm_v"],
        "gmlp_w_s": jnp.where(causal[None], tot["gmlp_w_s"], 0.0), "gmlp_b_s": tot["gmlp_b_s"],
        "norm_ssm_out": tot["norm_ssm_out"], "norm_gmlp_out": tot["norm_gmlp_out"], "norm_ffn2": tot["norm_ffn2"],
        "norm_ple": tot["norm_ple"], "norm_final": tot["norm_final"]}
    for k, g in small_grads.items():
        shp = weights[k].shape
        g2 = _as2d(g.reshape(shp))
        dl, nm, nv = adam_small(g2, _as2d(weights[k]), _as2d(moments_m[k]), _as2d(moments_v[k]), "adam_" + k)
        out_g[k], out_d[k], out_m[k], out_v[k] = g2.reshape(shp), dl.reshape(shp), nm.reshape(shp), nv.reshape(shp)

    return (loss, grad_x[None], *[out_g[k] for k in names], *[out_d[k] for k in names],
            *[out_m[k] for k in names], *[out_v[k] for k in names])
```

```python
import math

import jax
import jax.numpy as jnp
from jax import lax
from jax.experimental import pallas as pl
from jax.experimental.pallas import tpu as pltpu

F32 = jnp.float32
BF16 = jnp.bfloat16
MESH_DT = pl.DeviceIdType.MESH

N_DEV = 8
N_CHIP = 4
LANE = 128
SUBLANE = 8
VMEM_LIMIT = 56 * 1024 * 1024

EPS = 1e-6
SSM_GROUP = 16
SSM_STATE = 64
GROUPS_PER_BLOCK = LANE // SSM_GROUP
STATE_BLOCK = GROUPS_PER_BLOCK * SSM_STATE
GMLP_HEAD = 128
CHUNK = 128

ADAM_LR = 0.001
ADAM_B1 = 0.9
ADAM_B2 = 0.999
ADAM_EPS = 1e-08
ADAM_WD = 0.01
ADAM_STEP = 10

GELU_K = math.sqrt(2.0 / math.pi)
GELU_C = 0.044715


def _cparams():
    return pltpu.CompilerParams(vmem_limit_bytes=VMEM_LIMIT)


def _tile(n, pref):
    if n <= pref:
        return n
    t = (pref // LANE) * LANE
    while t > 0:
        if n % t == 0:
            return t
        t -= LANE
    return n


def _row_tile(n, pref):
    if n <= pref:
        return n
    t = (pref // SUBLANE) * SUBLANE
    while t > 0:
        if n % t == 0:
            return t
        t -= SUBLANE
    return n


def _gelu(x):
    t = jnp.tanh(GELU_K * (x + GELU_C * x * x * x))
    return 0.5 * x * (1.0 + t)


def _gelu_grad(x):
    t = jnp.tanh(GELU_K * (x + GELU_C * x * x * x))
    return 0.5 * (1.0 + t) + 0.5 * x * (1.0 - t * t) * (GELU_K * (1.0 + 3.0 * GELU_C * x * x))


def _sigmoid(x):
    return 0.5 * jnp.tanh(0.5 * x) + 0.5


_DN = {
    "nn": (((1,), (0,)), ((), ())),
    "nt": (((1,), (1,)), ((), ())),
    "tn": (((0,), (0,)), ((), ())),
}


def _dot(a, b, mode="nn"):
    return lax.dot_general(a, b, _DN[mode], preferred_element_type=F32)


class CommTask:
    def __init__(self, inputs, out_shape, n_sems, start, late, finish, in_place=False):
        self.inputs, self.out_shape, self.n_sems = list(inputs), list(out_shape), n_sems
        self.start, self.late, self.finish = start, late, finish
        self.in_place = in_place


def _task_aliases(tasks, first_in, first_out):
    aliases = {}
    for t in tasks:
        if t.in_place:
            aliases.update({first_in + i: first_out + i for i in range(len(t.inputs))})
        first_in += len(t.inputs)
        first_out += len(t.out_shape)
    return aliases


def _call(body, *, name, grid, in_specs, out_specs, out_shape, args, scratch_shapes=(), tasks=()):
    in_specs, out_specs, out_shape = list(in_specs), list(out_specs), list(out_shape)
    scratch_shapes = list(scratch_shapes)
    if not tasks:
        return pl.pallas_call(
            body, name=name, grid=grid, in_specs=in_specs, out_specs=out_specs, out_shape=out_shape,
            scratch_shapes=scratch_shapes, compiler_params=_cparams())(*args)
    n_in, n_out, n_scr = len(in_specs), len(out_specs), len(scratch_shapes)
    t_in = [len(t.inputs) for t in tasks]
    t_out = [len(t.out_shape) for t in tasks]
    late_step = grid[0] - max(1, grid[0] // 4)
    has_late = grid[0] >= 2

    def carried(*refs):
        pos = n_in
        task_ins = []
        for k in t_in:
            task_ins.append(refs[pos:pos + k])
            pos += k
        outs = refs[pos:pos + n_out]
        pos += n_out
        task_outs = []
        for k in t_out:
            task_outs.append(refs[pos:pos + k])
            pos += k
        scratch = refs[pos:pos + n_scr]
        pos += n_scr
        sems = [refs[pos + 3 * i:pos + 3 * i + 3] for i in range(len(tasks))]
        ids = [pl.program_id(d) for d in range(len(grid))]
        rest_zero = True
        for d in range(1, len(grid)):
            rest_zero = jnp.logical_and(rest_zero, ids[d] == 0)
        first = jnp.logical_and(ids[0] == 0, rest_zero)
        last = ids[0] == grid[0] - 1
        for d in range(1, len(grid)):
            last = jnp.logical_and(last, ids[d] == grid[d] - 1)

        @pl.when(first)
        def _():
            for t, ti, to, s in zip(tasks, task_ins, task_outs, sems):
                t.start(ti, to, *s)

        if has_late:
            @pl.when(jnp.logical_and(ids[0] == late_step, rest_zero))
            def _():
                for t, ti, to, s in zip(tasks, task_ins, task_outs, sems):
                    t.late(ti, to, *s)

        body(*refs[:n_in], *outs, *scratch)

        @pl.when(last)
        def _():
            for t, ti, to, s in zip(tasks, task_ins, task_outs, sems):
                if not has_late:
                    t.late(ti, to, *s)
                t.finish(ti, to, *s)

    any_spec = pl.BlockSpec(memory_space=pl.ANY)
    sem_shapes = [pltpu.SemaphoreType.DMA((n,)) for t in tasks for n in t.n_sems]
    res = pl.pallas_call(
        carried, name=name, grid=grid,
        in_specs=in_specs + [any_spec] * sum(t_in), out_specs=out_specs + [any_spec] * sum(t_out),
        out_shape=out_shape + [s for t in tasks for s in t.out_shape],
        input_output_aliases=_task_aliases(tasks, n_in, n_out),
        scratch_shapes=scratch_shapes + sem_shapes, compiler_params=_cparams(),
    )(*args, *[a for t in tasks for a in t.inputs])
    res = list(res)
    task_res, pos = [], n_out
    for k in t_out:
        task_res.append(res[pos:pos + k])
        pos += k
    return res[:n_out], task_res


def _mm_dims(a, b, mode):
    if mode == "nn":
        (m, k), (k2, n) = a.shape, b.shape
    elif mode == "nt":
        (m, k), (n, k2) = a.shape, b.shape
    else:
        (k, m), (k2, n) = a.shape, b.shape
    assert k == k2, (a.shape, b.shape, mode)
    return m, n, k


def _mm_specs(mode, tm, tn, tk):
    if mode == "tn":
        a_spec = pl.BlockSpec((tk, tm), lambda i, j, k: (k, i))
    else:
        a_spec = pl.BlockSpec((tm, tk), lambda i, j, k: (i, k))
    if mode == "nt":
        b_spec = pl.BlockSpec((tn, tk), lambda i, j, k: (j, k))
    else:
        b_spec = pl.BlockSpec((tk, tn), lambda i, j, k: (k, j))
    return a_spec, b_spec


def _accumulate(acc, nk, partial, emit):
    if nk == 1:
        emit(partial)
        return
    kk = pl.program_id(2)

    @pl.when(kk == 0)
    def _():
        acc[...] = partial

    @pl.when(kk > 0)
    def _():
        acc[...] += partial

    @pl.when(kk == nk - 1)
    def _():
        emit(acc[...])


def matmul(a, b, mode, name, out_dtype=F32, res=None, scale=1.0, tm=1024, tn=1024, tk=2048, tasks=()):
    m, n, k = _mm_dims(a, b, mode)
    tm, tn, tk = _tile(m, tm), _tile(n, tn), _tile(k, tk)
    nk = k // tk
    a_spec, b_spec = _mm_specs(mode, tm, tn, tk)
    o_spec = pl.BlockSpec((tm, tn), lambda i, j, k: (i, j))
    has_res = res is not None

    def body(*refs):
        if has_res:
            a_ref, b_ref, r_ref, o_ref, acc = refs
        else:
            a_ref, b_ref, o_ref, acc = refs

        def emit(v):
            if scale != 1.0:
                v = v * scale
            if has_res:
                v = r_ref[...] + v
            o_ref[...] = v.astype(out_dtype)

        _accumulate(acc, nk, _dot(a_ref[...], b_ref[...], mode), emit)

    out = _call(
        body, name=name, grid=(m // tm, n // tn, nk),
        in_specs=[a_spec, b_spec] + ([o_spec] if has_res else []), out_specs=[o_spec],
        out_shape=[jax.ShapeDtypeStruct((m, n), out_dtype)], args=(a, b) + ((res,) if has_res else ()),
        scratch_shapes=[pltpu.VMEM((tm, tn) if nk > 1 else (SUBLANE, LANE), F32)], tasks=tasks)
    return (out[0][0], out[1]) if tasks else out[0]


def ffn_up(xn, wu, gate, name, tm=1024, tn=1024, tk=2048, tasks=()):
    m, n, k = _mm_dims(xn, wu, "nt")
    tm, tn, tk = _tile(m, tm), _tile(n, tn), _tile(k, tk)
    nk = k // tk
    a_spec, b_spec = _mm_specs("nt", tm, tn, tk)
    o_spec = pl.BlockSpec((tm, tn), lambda i, j, k: (i, j))

    def body(a_ref, u_ref, gate_ref, up_ref, act_ref, acc):
        def emit(u):
            g = gate_ref[...]
            up_ref[...] = u
            act_ref[...] = (g * _sigmoid(g) * u).astype(BF16)

        _accumulate(acc, nk, _dot(a_ref[...], u_ref[...], "nt"), emit)

    out = _call(
        body, name=name, grid=(m // tm, n // tn, nk), in_specs=[a_spec, b_spec, o_spec],
        out_specs=[o_spec, o_spec],
        out_shape=[jax.ShapeDtypeStruct((m, n), F32), jax.ShapeDtypeStruct((m, n), BF16)],
        args=(xn, wu, gate), scratch_shapes=[pltpu.VMEM((tm, tn) if nk > 1 else (SUBLANE, LANE), F32)], tasks=tasks)
    return (tuple(out[0]), out[1]) if tasks else tuple(out)


def ffn_bwd_act(dh, wd, gate, up, name, tm=1024, tn=512, tk=2048, tasks=()):
    m, n, k = _mm_dims(dh, wd, "nt")
    tm, tn, tk = _tile(m, tm), _tile(n, tn), _tile(k, tk)
    nk = k // tk
    a_spec, b_spec = _mm_specs("nt", tm, tn, tk)
    o_spec = pl.BlockSpec((tm, tn), lambda i, j, k: (i, j))

    def body(a_ref, b_ref, gate_ref, up_ref, dg_ref, du_ref, acc):
        def emit(total):
            dact = 0.5 * total
            g = gate_ref[...]
            sg = _sigmoid(g)
            du_ref[...] = (dact * (g * sg)).astype(BF16)
            dg_ref[...] = (dact * up_ref[...] * (sg * (1.0 + g * (1.0 - sg)))).astype(BF16)

        _accumulate(acc, nk, _dot(a_ref[...], b_ref[...], "nt"), emit)

    out = _call(
        body, name=name, grid=(m // tm, n // tn, nk), in_specs=[a_spec, b_spec, o_spec, o_spec],
        out_specs=[o_spec, o_spec],
        out_shape=[jax.ShapeDtypeStruct((m, n), BF16), jax.ShapeDtypeStruct((m, n), BF16)],
        args=(dh, wd, gate, up), scratch_shapes=[pltpu.VMEM((tm, tn) if nk > 1 else (SUBLANE, LANE), F32)],
        tasks=tasks)
    return (tuple(out[0]), out[1]) if tasks else tuple(out)


def _rows(t, d, tr):
    return pl.BlockSpec((tr, d), lambda i: (i, 0))


def _vec(d):
    return pl.BlockSpec((1, d), lambda i: (0, 0))


def rmsnorm_fwd(x, g, name, tr=512):
    t, d = x.shape
    tr = _row_tile(t, tr)

    def body(x_ref, g_ref, o_ref):
        xf = x_ref[...]
        r = lax.rsqrt(jnp.mean(xf * xf, axis=-1, keepdims=True) + EPS)
        o_ref[...] = (xf * r * g_ref[...]).astype(BF16)

    return pl.pallas_call(
        body, name=name, grid=(t // tr,), in_specs=[_rows(t, d, tr), _vec(d)], out_specs=_rows(t, d, tr),
        out_shape=jax.ShapeDtypeStruct((t, d), BF16), compiler_params=_cparams(),
    )(x, g)


def _rms_bwd(dxn, xf, g):
    r = lax.rsqrt(jnp.mean(xf * xf, axis=-1, keepdims=True) + EPS)
    xhat = xf * r
    dg = jnp.sum(dxn * xhat, axis=0, keepdims=True)
    dxh = dxn * g
    dx = r * (dxh - xhat * jnp.mean(dxh * xhat, axis=-1, keepdims=True))
    return dx, dg


def rmsnorm_bwd(dxn, x, g, dres, name, tr=256):
    t, d = x.shape
    tr = _row_tile(t, tr)

    def body(dxn_ref, x_ref, g_ref, dres_ref, o_ref, ob_ref, dg_ref):
        dx, dg = _rms_bwd(dxn_ref[...], x_ref[...], g_ref[...])
        out = dres_ref[...] + dx
        o_ref[...] = out
        ob_ref[...] = out.astype(BF16)

        @pl.when(pl.program_id(0) == 0)
        def _():
            dg_ref[...] = jnp.zeros_like(dg_ref)

        dg_ref[...] += dg

    return pl.pallas_call(
        body, name=name, grid=(t // tr,),
        in_specs=[_rows(t, d, tr), _rows(t, d, tr), _vec(d), _rows(t, d, tr)],
        out_specs=[_rows(t, d, tr), _rows(t, d, tr), _vec(d)],
        out_shape=[jax.ShapeDtypeStruct((t, d), F32), jax.ShapeDtypeStruct((t, d), BF16),
                   jax.ShapeDtypeStruct((1, d), F32)],
        compiler_params=_cparams(),
    )(dxn, x, g, dres)


def final_loss(h, target, g, name, tr=256):
    t, d = h.shape
    tr = _row_tile(t, tr)

    def body(h_ref, t_ref, g_ref, dh_ref, loss_ref, dg_ref):
        xf = h_ref[...]
        gg = g_ref[...]
        r = lax.rsqrt(jnp.mean(xf * xf, axis=-1, keepdims=True) + EPS)
        xhat = xf * r
        e = xhat * gg - t_ref[...]
        part = jnp.sum(jnp.sum(e * e, axis=1, keepdims=True), axis=0, keepdims=True) * (0.5 / d)
        dout = e * (1.0 / d)
        dg = jnp.sum(dout * xhat, axis=0, keepdims=True)
        dxh = dout * gg
        dh_ref[...] = r * (dxh - xhat * jnp.mean(dxh * xhat, axis=-1, keepdims=True))

        @pl.when(pl.program_id(0) == 0)
        def _():
            dg_ref[...] = jnp.zeros_like(dg_ref)
            loss_ref[...] = jnp.zeros_like(loss_ref)

        dg_ref[...] += dg
        loss_ref[...] += jnp.broadcast_to(part, loss_ref.shape)

    return pl.pallas_call(
        body, name=name, grid=(t // tr,),
        in_specs=[_rows(t, d, tr), _rows(t, d, tr), _vec(d)],
        out_specs=[_rows(t, d, tr), pl.BlockSpec((SUBLANE, LANE), lambda i: (0, 0)), _vec(d)],
        out_shape=[jax.ShapeDtypeStruct((t, d), F32), jax.ShapeDtypeStruct((SUBLANE, LANE), F32),
                   jax.ShapeDtypeStruct((1, d), F32)],
        compiler_params=_cparams(),
    )(h, target, g)


def ple_fwd(h, glin, pp, name, tr=512):
    t, d = h.shape
    tr = _row_tile(t, tr)

    def body(h_ref, gl_ref, pp_ref, o_ref):
        o_ref[...] = h_ref[...] + _sigmoid(gl_ref[...]) * pp_ref[...]

    sp = _rows(t, d, tr)
    return pl.pallas_call(
        body, name=name, grid=(t // tr,), in_specs=[sp, sp, sp], out_specs=sp,
        out_shape=jax.ShapeDtypeStruct((t, d), F32), compiler_params=_cparams(),
    )(h, glin, pp)


def ple_bwd(dh, glin, pp, name, tr=512):
    t, d = dh.shape
    tr = _row_tile(t, tr)

    def body(dh_ref, gl_ref, pp_ref, dpp_ref, dgl_ref):
        gate = _sigmoid(gl_ref[...])
        dh_ = dh_ref[...]
        dpp_ref[...] = (dh_ * gate).astype(BF16)
        dgl_ref[...] = (dh_ * pp_ref[...] * gate * (1.0 - gate)).astype(BF16)

    sp = _rows(t, d, tr)
    return pl.pallas_call(
        body, name=name, grid=(t // tr,), in_specs=[sp, sp, sp], out_specs=[sp, sp],
        out_shape=[jax.ShapeDtypeStruct((t, d), BF16), jax.ShapeDtypeStruct((t, d), BF16)],
        compiler_params=_cparams(),
    )(dh, glin, pp)


def mix_out_fwd(y_pre, glin, y_gmlp, g_so, g_go, name, tr=512):
    t, d = y_pre.shape
    tr = _row_tile(t, tr)

    def body(yp_ref, gl_ref, yg_ref, gs_ref, gg_ref, o_ref):
        ys = _gelu(yp_ref[...]) * _sigmoid(gl_ref[...])
        r = lax.rsqrt(jnp.mean(ys * ys, axis=-1, keepdims=True) + EPS)
        o_ref[:, 0:d] = (ys * r * gs_ref[...]).astype(BF16)
        yq = yg_ref[...]
        r2 = lax.rsqrt(jnp.mean(yq * yq, axis=-1, keepdims=True) + EPS)
        o_ref[:, d:2 * d] = (yq * r2 * gg_ref[...]).astype(BF16)

    sp = _rows(t, d, tr)
    return pl.pallas_call(
        body, name=name, grid=(t // tr,), in_specs=[sp, sp, sp, _vec(d), _vec(d)],
        out_specs=_rows(t, 2 * d, tr), out_shape=jax.ShapeDtypeStruct((t, 2 * d), BF16),
        compiler_params=_cparams(),
    )(y_pre, glin, y_gmlp, g_so, g_go)


def mix_out_bwd(dycat, y_pre, glin, y_gmlp, g_so, g_go, name, tr=256):
    t, d = y_pre.shape
    tr = _row_tile(t, tr)

    def body(dy_ref, yp_ref, gl_ref, yg_ref, gs_ref, gg_ref, dyg_ref, dl_ref, dyq_ref, dgs_ref, dgg_ref):
        yg = _gelu(yp_ref[...])
        sg = _sigmoid(gl_ref[...])
        dys, dgs = _rms_bwd(dy_ref[:, 0:d], yg * sg, gs_ref[...])
        dyg_ref[...] = dys * sg
        dl_ref[...] = (dys * yg * sg * (1.0 - sg)).astype(BF16)
        dyq, dgg = _rms_bwd(dy_ref[:, d:2 * d], yg_ref[...], gg_ref[...])
        dyq_ref[...] = dyq

        @pl.when(pl.program_id(0) == 0)
        def _():
            dgs_ref[...] = jnp.zeros_like(dgs_ref)
            dgg_ref[...] = jnp.zeros_like(dgg_ref)

        dgs_ref[...] += dgs
        dgg_ref[...] += dgg

    sp = _rows(t, d, tr)
    return pl.pallas_call(
        body, name=name, grid=(t // tr,),
        in_specs=[_rows(t, 2 * d, tr), sp, sp, sp, _vec(d), _vec(d)],
        out_specs=[sp, sp, sp, _vec(d), _vec(d)],
        out_shape=[jax.ShapeDtypeStruct((t, d), F32), jax.ShapeDtypeStruct((t, d), BF16),
                   jax.ShapeDtypeStruct((t, d), F32), jax.ShapeDtypeStruct((1, d), F32),
                   jax.ShapeDtypeStruct((1, d), F32)],
        compiler_params=_cparams(),
    )(dycat, y_pre, glin, y_gmlp, g_so, g_go)


SCAN_COLS = 512


def _scan_tile(xr, xi, const, cr, ci, reverse):
    for lvl, sh in enumerate((1, 2, 4)):
        ar, ai = const(2 * lvl), const(2 * lvl + 1)
        s = (SUBLANE - sh) if reverse else sh
        rr = pltpu.roll(xr, s, 0)
        ri = pltpu.roll(xi, s, 0)
        xr, xi = xr + ar * rr - ai * ri, xi + ar * ri + ai * rr
    pr, pi_ = const(6), const(7)
    xr, xi = xr + pr * cr - pi_ * ci, xi + pr * ci + pi_ * cr
    return xr, xi


def _bcast_row(x, row):
    return jnp.broadcast_to(x[row:row + 1, :], x.shape)


def s5_fwd(z, bc_r, bc_i, cc_r, cc_i, apw, dvec, name, tc=256, tasks=()):
    t = z.shape[0]
    nblk = bc_r.shape[0]
    d = nblk * LANE
    ns = nblk * STATE_BLOCK
    tc = _row_tile(t, tc)
    ntile = tc // SUBLANE

    def body(z_ref, br_ref, bi_ref, cr_ref, ci_ref, apw_ref, d_ref, y_ref, yg_ref, sr_ref, si_ref, carry):
        @pl.when(pl.program_id(0) == 0)
        def _():
            carry[...] = jnp.zeros_like(carry)

        for j in range(nblk):
            uj = z_ref[:, j * LANE:(j + 1) * LANE]
            ub = uj.astype(BF16)
            for q in range(STATE_BLOCK // SCAN_COLS):
                c0 = j * STATE_BLOCK + q * SCAN_COLS
                cs = pl.ds(c0, SCAN_COLS)
                bs = slice(q * SCAN_COLS, (q + 1) * SCAN_COLS)
                sr_ref[:, cs] = _dot(ub, br_ref[j, :, bs])
                si_ref[:, cs] = _dot(ub, bi_ref[j, :, bs])
                const = lambda k, cs=cs: apw_ref[k, :, cs]

                def tile(k, c, cs=cs, const=const):
                    rows = pl.ds(pl.multiple_of(k * SUBLANE, SUBLANE), SUBLANE)
                    xr, xi = _scan_tile(sr_ref[rows, cs], si_ref[rows, cs], const, c[0], c[1], False)
                    sr_ref[rows, cs] = xr
                    si_ref[rows, cs] = xi
                    return _bcast_row(xr, SUBLANE - 1), _bcast_row(xi, SUBLANE - 1)

                c_r, c_i = lax.fori_loop(0, ntile, tile, (carry[0, :, cs], carry[1, :, cs]))
                carry[0, :, cs] = c_r
                carry[1, :, cs] = c_i
            sb = pl.ds(j * STATE_BLOCK, STATE_BLOCK)
            y = (_dot(sr_ref[:, sb].astype(BF16), cr_ref[j]) - _dot(si_ref[:, sb].astype(BF16), ci_ref[j])
                 + d_ref[:, j * LANE:(j + 1) * LANE] * uj)
            y_ref[:, j * LANE:(j + 1) * LANE] = y
            yg_ref[:, j * LANE:(j + 1) * LANE] = _gelu(y).astype(BF16)

    full3 = lambda shp: pl.BlockSpec(shp, lambda i: (0, 0, 0))
    out = _call(
        body, name=name, grid=(t // tc,),
        in_specs=[pl.BlockSpec((tc, d), lambda i: (i, 0)), full3(bc_r.shape), full3(bc_i.shape),
                  full3(cc_r.shape), full3(cc_i.shape), full3(apw.shape), _vec(d)],
        out_specs=[pl.BlockSpec((tc, d), lambda i: (i, 0)), pl.BlockSpec((tc, d), lambda i: (i, 0)),
                   pl.BlockSpec((tc, ns), lambda i: (i, 0)), pl.BlockSpec((tc, ns), lambda i: (i, 0))],
        out_shape=[jax.ShapeDtypeStruct((t, d), F32), jax.ShapeDtypeStruct((t, d), BF16),
                   jax.ShapeDtypeStruct((t, ns), F32), jax.ShapeDtypeStruct((t, ns), F32)],
        args=(z, bc_r, bc_i, cc_r, cc_i, apw, dvec), scratch_shapes=[pltpu.VMEM((2, SUBLANE, ns), F32)], tasks=tasks)
    return (tuple(out[0]), out[1]) if tasks else tuple(out)


def s5_bwd(dyg, y_pre, z, sr, si, bc_r, bc_i, cc_r, cc_i, apw_rev, dvec, name, tc=128, tasks=()):
    t = z.shape[0]
    nblk = bc_r.shape[0]
    d = nblk * LANE
    ns = nblk * STATE_BLOCK
    tc = _row_tile(t, tc)
    ntile = tc // SUBLANE
    nchunk = t // tc
    tiles_per_chunk = tc // SUBLANE

    def body(dyg_ref, yp_ref, z_ref, sr_ref, si_ref, pr_ref, pi_ref, br_ref, bi_ref, cr_ref, ci_ref, apw_ref,
             d_ref, du_ref, gd_ref, gcr_ref, gci_ref, gbr_ref, gbi_ref, gar_ref, gai_ref, lr_ref, li_ref, carry):
        step = pl.program_id(0)

        @pl.when(step == 0)
        def _():
            carry[...] = jnp.zeros_like(carry)
            for ref in (gd_ref, gcr_ref, gci_ref, gbr_ref, gbi_ref, gar_ref, gai_ref):
                ref[...] = jnp.zeros_like(ref)

        first_chunk = (step == nchunk - 1).astype(F32)
        keep_prev = 1.0 - first_chunk
        row0 = lax.broadcasted_iota(jnp.int32, (SUBLANE, SCAN_COLS), 0) == 0

        for j in range(nblk):
            lanes = slice(j * LANE, (j + 1) * LANE)
            uj = z_ref[:, lanes]
            ub = uj.astype(BF16)
            gy = dyg_ref[:, lanes] * _gelu_grad(yp_ref[:, lanes])
            gyb = gy.astype(BF16)
            gd_ref[:, lanes] += jnp.sum(gy * uj, axis=0, keepdims=True)
            for q in range(STATE_BLOCK // SCAN_COLS):
                c0 = j * STATE_BLOCK + q * SCAN_COLS
                cs = pl.ds(c0, SCAN_COLS)
                bs = slice(q * SCAN_COLS, (q + 1) * SCAN_COLS)
                lr_ref[:, cs] = _dot(gyb, cr_ref[j, bs, :], "nt")
                li_ref[:, cs] = -_dot(gyb, ci_ref[j, bs, :], "nt")
                const = lambda k, cs=cs: apw_ref[k, :, cs]

                def one_tile(rows, prev_r, prev_i, c, cs=cs, const=const):
                    cr_, ci_, gar, gai = c
                    xr, xi = _scan_tile(lr_ref[rows, cs], li_ref[rows, cs], const, cr_, ci_, True)
                    lr_ref[rows, cs] = xr
                    li_ref[rows, cs] = xi
                    spr = jnp.where(row0, prev_r, pltpu.roll(sr_ref[rows, cs], 1, 0))
                    spi = jnp.where(row0, prev_i, pltpu.roll(si_ref[rows, cs], 1, 0))
                    gar = gar + xr * spr + xi * spi
                    gai = gai + xi * spr - xr * spi
                    return _bcast_row(xr, 0), _bcast_row(xi, 0), gar, gai

                def tile(k, c, cs=cs, one_tile=one_tile):
                    kk = ntile - 1 - k
                    rows = pl.ds(pl.multiple_of(kk * SUBLANE, SUBLANE), SUBLANE)
                    prow = pl.ds(pl.multiple_of((kk - 1) * SUBLANE, SUBLANE), SUBLANE)
                    prev_r = _bcast_row(sr_ref[prow, cs], SUBLANE - 1)
                    prev_i = _bcast_row(si_ref[prow, cs], SUBLANE - 1)
                    return one_tile(rows, prev_r, prev_i, c)

                zero = jnp.zeros((SUBLANE, SCAN_COLS), F32)
                c = lax.fori_loop(0, ntile - 1, tile, (carry[0, :, cs], carry[1, :, cs], zero, zero))
                prev_r = _bcast_row(pr_ref[:, cs], SUBLANE - 1) * keep_prev
                prev_i = _bcast_row(pi_ref[:, cs], SUBLANE - 1) * keep_prev
                c_r, c_i, gar, gai = one_tile(pl.ds(0, SUBLANE), prev_r, prev_i, c)
                carry[0, :, cs] = c_r
                carry[1, :, cs] = c_i
                gar_ref[:, cs] += gar
                gai_ref[:, cs] += gai
            sb = pl.ds(j * STATE_BLOCK, STATE_BLOCK)
            lrb = lr_ref[:, sb].astype(BF16)
            lib = li_ref[:, sb].astype(BF16)
            gcr_ref[j] += _dot(gyb, sr_ref[:, sb].astype(BF16), "tn")
            gci_ref[j] -= _dot(gyb, si_ref[:, sb].astype(BF16), "tn")
            gbr_ref[j] += _dot(ub, lrb, "tn")
            gbi_ref[j] += _dot(ub, lib, "tn")
            du = _dot(lrb, br_ref[j], "nt") + _dot(lib, bi_ref[j], "nt") + gy * d_ref[:, lanes]
            du_ref[:, lanes] = du.astype(BF16)

    rev = lambda i: (nchunk - 1 - i, 0)
    prev = lambda i: (jnp.maximum((nchunk - 1 - i) * tiles_per_chunk - 1, 0), 0)
    full3 = lambda shp: pl.BlockSpec(shp, lambda i: (0, 0, 0))
    acc3 = pl.BlockSpec((nblk, LANE, STATE_BLOCK), lambda i: (0, 0, 0))
    acc_rows = pl.BlockSpec((SUBLANE, ns), lambda i: (0, 0))
    out = _call(
        body, name=name, grid=(nchunk,),
        in_specs=[pl.BlockSpec((tc, d), rev), pl.BlockSpec((tc, d), rev), pl.BlockSpec((tc, d), rev),
                  pl.BlockSpec((tc, ns), rev), pl.BlockSpec((tc, ns), rev),
                  pl.BlockSpec((SUBLANE, ns), prev), pl.BlockSpec((SUBLANE, ns), prev),
                  full3(bc_r.shape), full3(bc_i.shape), full3(cc_r.shape), full3(cc_i.shape), full3(apw_rev.shape),
                  _vec(d)],
        out_specs=[pl.BlockSpec((tc, d), rev), _vec(d), acc3, acc3, acc3, acc3, acc_rows, acc_rows],
        out_shape=[jax.ShapeDtypeStruct((t, d), BF16), jax.ShapeDtypeStruct((1, d), F32)]
        + [jax.ShapeDtypeStruct((nblk, LANE, STATE_BLOCK), F32)] * 4
        + [jax.ShapeDtypeStruct((SUBLANE, ns), F32)] * 2,
        args=(dyg, y_pre, z, sr, si, sr, si, bc_r, bc_i, cc_r, cc_i, apw_rev, dvec),
        scratch_shapes=[pltpu.VMEM((tc, ns), F32), pltpu.VMEM((tc, ns), F32), pltpu.VMEM((2, SUBLANE, ns), F32)],
        tasks=tasks)
    return (tuple(out[0]), out[1]) if tasks else tuple(out)


def _cmul(a, b):
    return a[0] * b[0] - a[1] * b[1], a[0] * b[1] + a[1] * b[0]


def _scan_constants(abar_r, abar_i, reverse):
    ar = abar_r.reshape(1, -1)
    ai = abar_i.reshape(1, -1)
    if reverse:
        ai = -ai
    pw = [(ar, ai)]
    for _ in range(SUBLANE - 1):
        pw.append(_cmul(pw[-1], (ar, ai)))
    rows = lax.broadcasted_iota(jnp.int32, (SUBLANE, 1), 0)
    out = []
    for sh in (1, 2, 4):
        keep = (rows <= SUBLANE - 1 - sh) if reverse else (rows >= sh)
        for part in pw[sh - 1]:
            out.append(jnp.where(keep, part, 0.0))
    for comp in (0, 1):
        stack = jnp.concatenate([pw[k][comp] for k in range(SUBLANE)], axis=0)
        out.append(stack[::-1] if reverse else stack)
    return jnp.stack(out, axis=0).astype(F32)


def _ssm_discretize(log_dt, a_re, a_im, b_re, b_im):
    dt = jnp.exp(log_dt)[:, None]
    lr = jnp.minimum(a_re, -1e-4)
    li = a_im
    mag = jnp.exp(lr * dt)
    ang = li * dt
    abar_r = mag * jnp.cos(ang)
    abar_i = mag * jnp.sin(ang)
    den = lr * lr + li * li
    xr = abar_r - 1.0
    xi = abar_i
    zr = (xr * lr + xi * li) / den
    zi = (xi * lr - xr * li) / den
    bbar_r = zr[..., None] * b_re - zi[..., None] * b_im
    bbar_i = zr[..., None] * b_im + zi[..., None] * b_re
    return abar_r, abar_i, bbar_r, bbar_i


def _block_diag(w):
    g, a, b = w.shape
    nb = g // GROUPS_PER_BLOCK
    eye = jnp.eye(GROUPS_PER_BLOCK, dtype=w.dtype)
    w5 = w.reshape(nb, GROUPS_PER_BLOCK, a, b)
    out = w5[:, :, :, None, :] * eye[None, :, None, :, None]
    return out.reshape(nb, GROUPS_PER_BLOCK * a, GROUPS_PER_BLOCK * b)


def _block_diag_extract(m, a, b):
    nb = m.shape[0]
    eye = jnp.eye(GROUPS_PER_BLOCK, dtype=m.dtype)
    m5 = m.reshape(nb, GROUPS_PER_BLOCK, a, GROUPS_PER_BLOCK, b)
    out = jnp.sum(m5 * eye[None, :, None, :, None], axis=3)
    return out.reshape(nb * GROUPS_PER_BLOCK, a, b)


SSD_L = 16
PAIR = 2


def _pair_diag(w):
    g, a, b = w.shape
    eye = jnp.eye(PAIR, dtype=w.dtype)
    out = w.reshape(g // PAIR, PAIR, a, b)[:, :, :, None, :] * eye[None, :, None, :, None]
    return out.reshape(g // PAIR, PAIR * a, PAIR * b)


def _ssd_matrices(abar_r, abar_i, bbar_r, bbar_i, c_re, c_im):
    g, n = abar_r.shape
    p = bbar_r.shape[2]
    ell = SSD_L
    pw = [(jnp.ones_like(abar_r), jnp.zeros_like(abar_i))]
    for _ in range(ell):
        pw.append(_cmul(pw[-1], (abar_r, abar_i)))
    pr = jnp.stack([q[0] for q in pw])
    pi = jnp.stack([q[1] for q in pw])

    def c_times(kr, ki):
        return (c_re[None] * kr[:, :, None, :] - c_im[None] * ki[:, :, None, :],
                c_re[None] * ki[:, :, None, :] + c_im[None] * kr[:, :, None, :])

    car, cai = c_times(pr[:ell], pi[:ell])
    taps = jnp.einsum("kgpn,gnq->gkpq", car, bbar_r) - jnp.einsum("kgpn,gnq->gkpq", cai, bbar_i)
    lag = jnp.arange(ell)[None, :] - jnp.arange(ell)[:, None]
    onehot = (lag[None] == jnp.arange(ell)[:, None, None]).astype(F32)
    m1 = jnp.einsum("kst,gkpq->gsqtp", onehot, taps).reshape(g, ell * p, ell * p)

    rev_r, rev_i = pr[:ell][::-1], pi[:ell][::-1]
    m2r = rev_r[:, :, :, None] * bbar_r[None] - rev_i[:, :, :, None] * bbar_i[None]
    m2i = rev_r[:, :, :, None] * bbar_i[None] + rev_i[:, :, :, None] * bbar_r[None]
    m2r = jnp.transpose(m2r, (1, 0, 3, 2)).reshape(g, ell * p, n)
    m2i = jnp.transpose(m2i, (1, 0, 3, 2)).reshape(g, ell * p, n)

    car1, cai1 = c_times(pr[1:], pi[1:])
    m3r = jnp.transpose(car1, (1, 3, 0, 2)).reshape(g, n, ell * p)
    m3i = -jnp.transpose(cai1, (1, 3, 0, 2)).reshape(g, n, ell * p)
    m3 = jnp.concatenate([_pair_diag(m3r), _pair_diag(m3i)], axis=1)
    return m1, _pair_diag(m2r), _pair_diag(m2i), m3, pw[ell][0], pw[ell][1]


def _to_pairs(a, ell=SSD_L):
    t, d = a.shape
    nq = d // (PAIR * SSM_GROUP)
    return a.reshape(t // ell, ell, nq, PAIR, SSM_GROUP).transpose(2, 0, 3, 1, 4).reshape(nq, t // ell, -1)


def _from_pairs(a, ell=SSD_L):
    nq, nc, _ = a.shape
    return a.reshape(nq, nc, PAIR, ell, SSM_GROUP).transpose(1, 3, 0, 2, 4).reshape(nc * ell, -1)


def s5c_fwd(u, u_prev, m1, m2r, m2i, m3, apw, name):
    nq, nc, w = u.shape
    half = w // PAIR
    ns = m2r.shape[2]
    ntile = nc // SUBLANE

    def body(u_ref, up_ref, m1_ref, m2r_ref, m2i_ref, m3_ref, apw_ref, y_ref, sr_ref, si_ref):
        up = up_ref[...]
        sr_ref[...] = _dot(up, m2r_ref[...])
        si_ref[...] = _dot(up, m2i_ref[...])
        const = lambda k: apw_ref[k]

        def tile(k, c):
            rows = pl.ds(pl.multiple_of(k * SUBLANE, SUBLANE), SUBLANE)
            xr, xi = _scan_tile(sr_ref[rows, :], si_ref[rows, :], const, c[0], c[1], False)
            sr_ref[rows, :] = xr
            si_ref[rows, :] = xi
            return _bcast_row(xr, SUBLANE - 1), _bcast_row(xi, SUBLANE - 1)

        zero = jnp.zeros((SUBLANE, ns), F32)
        lax.fori_loop(0, ntile, tile, (zero, zero))
        state = jnp.concatenate([sr_ref[...].astype(BF16), si_ref[...].astype(BF16)], axis=1)
        carried = _dot(state, m3_ref[...])
        uu = u_ref[...]
        for h in range(PAIR):
            cols = slice(h * half, (h + 1) * half)
            y_ref[:, cols] = _dot(uu[:, cols], m1_ref[h]) + carried[:, cols]

    per_pair = lambda shp: pl.BlockSpec((None,) + shp, lambda q: (q, 0, 0))
    return pl.pallas_call(
        body, name=name, grid=(nq,),
        in_specs=[per_pair((nc, w)), per_pair((nc, w)), pl.BlockSpec((PAIR, half, half), lambda q: (q, 0, 0)),
                  per_pair(m2r.shape[1:]), per_pair(m2i.shape[1:]), per_pair(m3.shape[1:]),
                  pl.BlockSpec((8, SUBLANE, ns), lambda q: (0, 0, q))],
        out_specs=[per_pair((nc, w)), per_pair((nc, ns)), per_pair((nc, ns))],
        out_shape=[jax.ShapeDtypeStruct((nq, nc, w), F32), jax.ShapeDtypeStruct((nq, nc, ns), F32),
                   jax.ShapeDtypeStruct((nq, nc, ns), F32)],
        compiler_params=_cparams(),
    )(u, u_prev, m1, m2r, m2i, m3, apw)


def s5c_bwd(u, dy, dy_next, sr, si, m1, m2r, m2i, m3, apw_rev, name):
    nq, nc, w = u.shape
    half = w // PAIR
    ns = m2r.shape[2]
    ntile = nc // SUBLANE

    def body(u_ref, dy_ref, dyn_ref, sr_ref, si_ref, m1_ref, m2r_ref, m2i_ref, m3_ref, apw_ref,
             du_ref, dm1_ref, dm2r_ref, dm2i_ref, dm3_ref, dar_ref, dai_ref, lr_ref, li_ref):
        back = _dot(dyn_ref[...], m3_ref[...], "nt")
        lr_ref[...] = back[:, 0:ns]
        li_ref[...] = back[:, ns:2 * ns]
        const = lambda k: apw_ref[k]

        def tile(k, c):
            rows = pl.ds(pl.multiple_of((ntile - 1 - k) * SUBLANE, SUBLANE), SUBLANE)
            xr, xi = _scan_tile(lr_ref[rows, :], li_ref[rows, :], const, c[0], c[1], True)
            lr_ref[rows, :] = xr
            li_ref[rows, :] = xi
            s_r, s_i = sr_ref[rows, :], si_ref[rows, :]
            return (_bcast_row(xr, 0), _bcast_row(xi, 0), c[2] + xr * s_r + xi * s_i, c[3] + xi * s_r - xr * s_i)

        zero = jnp.zeros((SUBLANE, ns), F32)
        _, _, dar, dai = lax.fori_loop(0, ntile, tile, (zero, zero, zero, zero))
        dar_ref[...] = dar
        dai_ref[...] = dai
        lrb, lib = lr_ref[...].astype(BF16), li_ref[...].astype(BF16)
        uu, dyy = u_ref[...], dy_ref[...]
        from_state = _dot(lrb, m2r_ref[...], "nt") + _dot(lib, m2i_ref[...], "nt")
        for h in range(PAIR):
            cols = slice(h * half, (h + 1) * half)
            du_ref[:, cols] = _dot(dyy[:, cols], m1_ref[h], "nt") + from_state[:, cols]
            dm1_ref[h] = _dot(uu[:, cols], dyy[:, cols], "tn")
        dm2r_ref[...] = _dot(uu, lrb, "tn")
        dm2i_ref[...] = _dot(uu, lib, "tn")
        state = jnp.concatenate([sr_ref[...].astype(BF16), si_ref[...].astype(BF16)], axis=1)
        dm3_ref[...] = _dot(state, dyy, "tn")

    per_pair = lambda shp: pl.BlockSpec((None,) + shp, lambda q: (q, 0, 0))
    m1_spec = pl.BlockSpec((PAIR, half, half), lambda q: (q, 0, 0))
    return pl.pallas_call(
        body, name=name, grid=(nq,),
        in_specs=[per_pair((nc, w)), per_pair((nc, w)), per_pair((nc, w)), per_pair((nc, ns)), per_pair((nc, ns)),
                  m1_spec, per_pair(m2r.shape[1:]), per_pair(m2i.shape[1:]), per_pair(m3.shape[1:]),
                  pl.BlockSpec((8, SUBLANE, ns), lambda q: (0, 0, q))],
        out_specs=[per_pair((nc, w)), m1_spec, per_pair(m2r.shape[1:]), per_pair(m2i.shape[1:]),
                   per_pair(m3.shape[1:]), per_pair((SUBLANE, ns)), per_pair((SUBLANE, ns))],
        out_shape=[jax.ShapeDtypeStruct((nq, nc, w), F32), jax.ShapeDtypeStruct(m1.shape, F32),
                   jax.ShapeDtypeStruct(m2r.shape, F32), jax.ShapeDtypeStruct(m2i.shape, F32),
                   jax.ShapeDtypeStruct(m3.shape, F32), jax.ShapeDtypeStruct((nq, SUBLANE, ns), F32),
                   jax.ShapeDtypeStruct((nq, SUBLANE, ns), F32)],
        scratch_shapes=[pltpu.VMEM((nc, ns), F32), pltpu.VMEM((nc, ns), F32)], compiler_params=_cparams(),
    )(u, dy, dy_next, sr, si, m1, m2r, m2i, m3, apw_rev)


def s5_skip_gelu(y_lin, z, dvec, name, tr=512):
    t, d = y_lin.shape
    tr = _row_tile(t, tr)

    def body(y_ref, u_ref, d_ref, yp_ref, yg_ref):
        y = y_ref[...] + d_ref[...] * u_ref[...]
        yp_ref[...] = y
        yg_ref[...] = _gelu(y).astype(BF16)

    sp = _rows(t, d, tr)
    return pl.pallas_call(
        body, name=name, grid=(t // tr,), in_specs=[sp, sp, _vec(d)], out_specs=[sp, sp],
        out_shape=[jax.ShapeDtypeStruct((t, d), F32), jax.ShapeDtypeStruct((t, d), BF16)],
        compiler_params=_cparams(),
    )(y_lin, z, dvec)


def s5_skip_gelu_bwd(dyg, y_pre, z, dvec, name, tr=512):
    t, d = dyg.shape
    tr = _row_tile(t, tr)

    def body(dyg_ref, yp_ref, u_ref, d_ref, gy_ref, skip_ref, gd_ref):
        gy = dyg_ref[...] * _gelu_grad(yp_ref[...])
        gy_ref[...] = gy.astype(BF16)
        skip_ref[...] = gy * d_ref[...]

        @pl.when(pl.program_id(0) == 0)
        def _():
            gd_ref[...] = jnp.zeros_like(gd_ref)

        gd_ref[...] += jnp.sum(gy * u_ref[...], axis=0, keepdims=True)

    sp = _rows(t, d, tr)
    return pl.pallas_call(
        body, name=name, grid=(t // tr,), in_specs=[sp, sp, sp, _vec(d)], out_specs=[sp, sp, _vec(d)],
        out_shape=[jax.ShapeDtypeStruct((t, d), BF16), jax.ShapeDtypeStruct((t, d), F32),
                   jax.ShapeDtypeStruct((1, d), F32)],
        compiler_params=_cparams(),
    )(dyg, y_pre, z, dvec)


def _layer_norm(gv, nv):
    mu = jnp.mean(gv, axis=-1, keepdims=True)
    xc = gv - mu
    r = lax.rsqrt(jnp.mean(xc * xc, axis=-1, keepdims=True) + EPS)
    xhat = xc * r
    return xhat * nv, xhat, r


def gmlp_fwd(z, norm_v, wm, bs, name, tr=256):
    t = z.shape[0]
    nh = wm.shape[0]
    d = nh * GMLP_HEAD
    col0 = (z.shape[1] - 2 * d) // d
    tr = _row_tile(t, tr)

    def body(zu_ref, zv_ref, nv_ref, wm_ref, bs_ref, o_ref):
        v, _, _ = _layer_norm(_gelu(zv_ref[...]), nv_ref[...])
        vb = v.astype(BF16)
        u = _gelu(zu_ref[...])
        for c in range(tr // CHUNK):
            rows = slice(c * CHUNK, (c + 1) * CHUNK)
            for h in range(nh):
                cols = slice(h * GMLP_HEAD, (h + 1) * GMLP_HEAD)
                s = _dot(wm_ref[h], vb[rows, cols]) + bs_ref[h]
                o_ref[rows, cols] = u[rows, cols] * s

    return pl.pallas_call(
        body, name=name, grid=(t // tr,),
        in_specs=[pl.BlockSpec((tr, d), lambda i: (i, col0)), pl.BlockSpec((tr, d), lambda i: (i, col0 + 1)),
                  _vec(d), pl.BlockSpec(wm.shape, lambda i: (0, 0, 0)), pl.BlockSpec(bs.shape, lambda i: (0, 0, 0))],
        out_specs=pl.BlockSpec((tr, d), lambda i: (i, 0)),
        out_shape=jax.ShapeDtypeStruct((t, d), F32), compiler_params=_cparams(),
    )(z, z, norm_v, wm, bs)


def gmlp_bwd(dy, z, norm_v, wm, wmt, bs, name, tr=256):
    t = z.shape[0]
    nh = wm.shape[0]
    d = nh * GMLP_HEAD
    col0 = (z.shape[1] - 2 * d) // d
    tr = _row_tile(t, tr)

    def body(dy_ref, zu_ref, zv_ref, nv_ref, wm_ref, wmt_ref, bs_ref, dzu_ref, dzv_ref, dnv_ref, dwm_ref, dbs_ref,
             dv_ref):
        @pl.when(pl.program_id(0) == 0)
        def _():
            dnv_ref[...] = jnp.zeros_like(dnv_ref)
            dwm_ref[...] = jnp.zeros_like(dwm_ref)
            dbs_ref[...] = jnp.zeros_like(dbs_ref)

        zv = zv_ref[...]
        nv = nv_ref[...]
        v, xhat, r = _layer_norm(_gelu(zv), nv)
        vb = v.astype(BF16)
        zu = zu_ref[...]
        u = _gelu(zu)
        dy_ = dy_ref[...]
        for c in range(tr // CHUNK):
            rows = slice(c * CHUNK, (c + 1) * CHUNK)
            for h in range(nh):
                cols = slice(h * GMLP_HEAD, (h + 1) * GMLP_HEAD)
                vh = vb[rows, cols]
                s = _dot(wm_ref[h], vh) + bs_ref[h]
                dyh = dy_[rows, cols]
                dzu_ref[rows, cols] = (dyh * s * _gelu_grad(zu[rows, cols])).astype(BF16)
                ds = dyh * u[rows, cols]
                dsb = ds.astype(BF16)
                dbs_ref[h] += jnp.sum(ds, axis=1, keepdims=True)
                dwm_ref[h] += _dot(dsb, vh, "nt")
                dv_ref[rows, cols] = _dot(wmt_ref[h], dsb)
        dv = dv_ref[...]
        dnv_ref[...] += jnp.sum(dv * xhat, axis=0, keepdims=True)
        dxh = dv * nv
        dgv = r * (dxh - jnp.mean(dxh, axis=-1, keepdims=True) - xhat * jnp.mean(dxh * xhat, axis=-1, keepdims=True))
        dzv_ref[...] = (dgv * _gelu_grad(zv)).astype(BF16)

    full3 = lambda shp: pl.BlockSpec(shp, lambda i: (0, 0, 0))
    rows_d = pl.BlockSpec((tr, d), lambda i: (i, 0))
    return pl.pallas_call(
        body, name=name, grid=(t // tr,),
        in_specs=[rows_d, pl.BlockSpec((tr, d), lambda i: (i, col0)), pl.BlockSpec((tr, d), lambda i: (i, col0 + 1)),
                  _vec(d), full3(wm.shape), full3(wmt.shape), full3(bs.shape)],
        out_specs=[rows_d, rows_d, _vec(d), full3((nh, CHUNK, CHUNK)), full3((nh, CHUNK, 1))],
        out_shape=[jax.ShapeDtypeStruct((t, d), BF16), jax.ShapeDtypeStruct((t, d), BF16),
                   jax.ShapeDtypeStruct((1, d), F32), jax.ShapeDtypeStruct((nh, CHUNK, CHUNK), F32),
                   jax.ShapeDtypeStruct((nh, CHUNK, 1), F32)],
        scratch_shapes=[pltpu.VMEM((tr, d), F32)], compiler_params=_cparams(),
    )(dy, z, z, norm_v, wm, wmt, bs)


def _block(ref, axis, size, k):
    start = pl.multiple_of(k * size, size)
    if axis == 0:
        return ref.at[pl.ds(start, size), :]
    return ref.at[:, pl.ds(start, size)]


def _place():
    x, y, c = lax.axis_index("x"), lax.axis_index("y"), lax.axis_index("c")
    chips = [(1 - x, y), (x, 1 - y), (1 - x, 1 - y)]
    return x, y, c, chips


def _dev(x, y, c):
    return 4 * x + 2 * y + c


def gather_task(shards, axes):
    n = len(shards)
    sizes = [s.shape[ax] for s, ax in zip(shards, axes)]
    out_shape = [
        jax.ShapeDtypeStruct((s.shape[0] * N_DEV, s.shape[1]) if ax == 0 else (s.shape[0], s.shape[1] * N_DEV), s.dtype)
        for s, ax in zip(shards, axes)
    ]

    def copy(ins, outs, send_sems, recv_sems, t, k, block, to, from_input=False):
        dst = _block(outs[t], axes[t], sizes[t], _dev(*block))
        return pltpu.make_async_remote_copy(
            src_ref=ins[t] if from_input else dst, dst_ref=dst,
            send_sem=send_sems.at[t * 7 + k], recv_sem=recv_sems.at[t * 7 + k],
            device_id=to, device_id_type=MESH_DT)

    def local(ins, outs, local_sems, t, me):
        return pltpu.make_async_copy(ins[t], _block(outs[t], axes[t], sizes[t], _dev(*me)), local_sems.at[t])

    def start(ins, outs, send_sems, recv_sems, local_sems):
        x, y, c, chips = _place()
        me, sibling = (x, y, c), (x, y, 1 - c)
        for t in range(n):
            local(ins, outs, local_sems, t, me).start()
        for t in range(n):
            copy(ins, outs, send_sems, recv_sems, t, 0, me, sibling, True).start()
            for j, chip in enumerate(chips):
                copy(ins, outs, send_sems, recv_sems, t, 1 + j, me, (*chip, c), True).start()

    def late(ins, outs, send_sems, recv_sems, local_sems):
        x, y, c, chips = _place()
        me, sibling = (x, y, c), (x, y, 1 - c)
        for t in range(n):
            for j, chip in enumerate(chips):
                copy(ins, outs, send_sems, recv_sems, t, 1 + j, (*chip, c), me).wait_recv()
                copy(ins, outs, send_sems, recv_sems, t, 4 + j, (*chip, c), sibling).start()

    def finish(ins, outs, send_sems, recv_sems, local_sems):
        x, y, c, chips = _place()
        me, sibling = (x, y, c), (x, y, 1 - c)
        for t in range(n):
            copy(ins, outs, send_sems, recv_sems, t, 0, sibling, me).wait_recv()
            for j, chip in enumerate(chips):
                copy(ins, outs, send_sems, recv_sems, t, 4 + j, (*chip, 1 - c), me).wait_recv()
        for t in range(n):
            copy(ins, outs, send_sems, recv_sems, t, 0, me, sibling, True).wait_send()
            for j, chip in enumerate(chips):
                copy(ins, outs, send_sems, recv_sems, t, 1 + j, me, (*chip, c), True).wait_send()
                copy(ins, outs, send_sems, recv_sems, t, 4 + j, (*chip, c), sibling).wait_send()
            local(ins, outs, local_sems, t, me).wait()

    return CommTask(shards, out_shape, (7 * n, 7 * n, n), start, late, finish)


def _blk3(shape2, axis):
    r, c = shape2
    return (r // N_DEV, c) if axis == 0 else (r, c // N_DEV)


def _no_late(ins, outs, send_sems, recv_sems, local_sems):
    pass


def to_sibling_task(grads, axes):
    n = len(grads)
    blks = [_blk3(g.shape, ax) for g, ax in zip(grads, axes)]
    sizes = [b[ax] for b, ax in zip(blks, axes)]

    def copies(ins, outs, send_sems, recv_sems):
        x, y, c, _ = _place()
        return [pltpu.make_async_remote_copy(
            src_ref=_block(ins[t], axes[t], sizes[t], 2 * i + (1 - c)), dst_ref=outs[t].at[i],
            send_sem=send_sems.at[t * N_CHIP + i], recv_sem=recv_sems.at[t * N_CHIP + i],
            device_id=(x, y, 1 - c), device_id_type=MESH_DT) for t in range(n) for i in range(N_CHIP)]

    def start(ins, outs, send_sems, recv_sems, local_sems):
        for cp in copies(ins, outs, send_sems, recv_sems):
            cp.start()

    def finish(ins, outs, send_sems, recv_sems, local_sems):
        cps = copies(ins, outs, send_sems, recv_sems)
        for cp in cps:
            cp.wait_recv()
        for cp in cps:
            cp.wait_send()

    out_shape = [jax.ShapeDtypeStruct((N_CHIP,) + b, g.dtype) for b, g in zip(blks, grads)]
    return CommTask(grads, out_shape, (N_CHIP * n, N_CHIP * n, 1), start, _no_late, finish)


def across_chips_task(parts):
    n = len(parts)

    def copies(ins, outs, send_sems, recv_sems):
        x, y, c, chips = _place()
        my_chip = 2 * x + y
        return [pltpu.make_async_remote_copy(
            src_ref=ins[t].at[2 * chip[0] + chip[1]], dst_ref=outs[t].at[my_chip],
            send_sem=send_sems.at[t * 3 + j], recv_sem=recv_sems.at[t * 3 + j],
            device_id=(*chip, c), device_id_type=MESH_DT) for t in range(n) for j, chip in enumerate(chips)]

    def mine(ins, outs, local_sems):
        x, y, _, _ = _place()
        my_chip = 2 * x + y
        return [pltpu.make_async_copy(ins[t].at[my_chip], outs[t].at[my_chip], local_sems.at[t]) for t in range(n)]

    def start(ins, outs, send_sems, recv_sems, local_sems):
        for cp in mine(ins, outs, local_sems):
            cp.start()
        for cp in copies(ins, outs, send_sems, recv_sems):
            cp.start()

    def finish(ins, outs, send_sems, recv_sems, local_sems):
        cps = copies(ins, outs, send_sems, recv_sems)
        for cp in cps:
            cp.wait_recv()
        for cp in cps:
            cp.wait_send()
        for cp in mine(ins, outs, local_sems):
            cp.wait()

    out_shape = [jax.ShapeDtypeStruct(p.shape, p.dtype) for p in parts]
    return CommTask(parts, out_shape, (3 * n, 3 * n, n), start, _no_late, finish)


def run_tasks(tasks, name):
    t_in = [len(t.inputs) for t in tasks]
    t_out = [len(t.out_shape) for t in tasks]

    def body(*refs):
        pos, views = 0, []
        for k in t_in:
            views.append([refs[pos:pos + k]])
            pos += k
        for v, k in zip(views, t_out):
            v.append(refs[pos:pos + k])
            pos += k
        for i, v in enumerate(views):
            v.extend(refs[pos + 3 * i:pos + 3 * i + 3])
        for phase in ("start", "late", "finish"):
            for t, v in zip(tasks, views):
                getattr(t, phase)(*v)

    any_spec = pl.BlockSpec(memory_space=pl.ANY)
    res = pl.pallas_call(
        body, name=name, in_specs=[any_spec] * sum(t_in), out_specs=[any_spec] * sum(t_out),
        out_shape=[s for t in tasks for s in t.out_shape], input_output_aliases=_task_aliases(tasks, 0, 0),
        scratch_shapes=[pltpu.SemaphoreType.DMA((k,)) for t in tasks for k in t.n_sems],
    )(*[a for t in tasks for a in t.inputs])
    res, out, pos = list(res), [], 0
    for k in t_out:
        out.append(res[pos:pos + k])
        pos += k
    return out


_HBM_SPEC = pl.BlockSpec(memory_space=pl.ANY)
_SEM_SPEC = pl.BlockSpec(memory_space=pltpu.SEMAPHORE)
_DATAFLOW = pltpu.SideEffectType.DATAFLOW_SIDE_EFFECTING


def _full_shape(s, ax):
    return (s.shape[0] * N_DEV, s.shape[1]) if ax == 0 else (s.shape[0], s.shape[1] * N_DEV)


def _level1_copy(src, landing, axis, size, send_sems, recv_sems, slot, sender, to):
    dst = _block(landing, axis, size, _dev(*sender))
    return pltpu.make_async_remote_copy(src_ref=src, dst_ref=dst, send_sem=send_sems.at[slot],
                                        recv_sem=recv_sems.at[slot], device_id=to, device_id_type=MESH_DT)


def place_own_block(shard, axis, me, name, tr=256):
    r, c = shard.shape
    tr = _row_tile(r, tr)
    nrb = r // tr
    if axis == 0:
        o_map = lambda i, me_ref: (me_ref[0] * nrb + i, 0)
    else:
        o_map = lambda i, me_ref: (i, me_ref[0])

    def body(me_ref, x_ref, o_ref):
        o_ref[...] = x_ref[...]

    return pl.pallas_call(
        body, name=name,
        grid_spec=pltpu.PrefetchScalarGridSpec(
            num_scalar_prefetch=1, grid=(nrb,), in_specs=[pl.BlockSpec((tr, c), lambda i, me_ref: (i, 0))],
            out_specs=pl.BlockSpec((tr, c), o_map)),
        out_shape=jax.ShapeDtypeStruct(_full_shape(shard, axis), shard.dtype), compiler_params=_cparams(),
    )(me, shard)


def gather_start(landing, axes, sizes, groups, name):
    n = len(landing)

    def body(*refs):
        lands, sems = refs[:n], refs[2 * n:]
        x, y, c, chips = _place()
        me = (x, y, c)
        targets = [(x, y, 1 - c)] + [(*chip, c) for chip in chips]
        for g, members in enumerate(groups):
            for m, t in enumerate(members):
                own = _block(lands[t], axes[t], sizes[t], _dev(*me))
                for k, to in enumerate(targets):
                    _level1_copy(own, lands[t], axes[t], sizes[t], sems[2 * g], sems[2 * g + 1], 4 * m + k,
                                 me, to).start()

    out = pl.pallas_call(
        body, name=name,
        out_shape=[jax.ShapeDtypeStruct(b.shape, b.dtype) for b in landing]
        + [pltpu.SemaphoreType.DMA((4 * len(members),)) for members in groups for _ in (0, 1)],
        in_specs=[_HBM_SPEC] * n, out_specs=[_HBM_SPEC] * n + [_SEM_SPEC] * (2 * len(groups)),
        input_output_aliases={i: i for i in range(n)},
        compiler_params=pltpu.CompilerParams(has_side_effects=_DATAFLOW),
    )(*landing)
    out = list(out)
    sems = out[n:]
    return out[:n], [(sems[2 * g], sems[2 * g + 1]) for g in range(len(groups))]


def gather_wait(landing, axes, sizes, send_sems, recv_sems, after, name):
    n = len(landing)

    def body(*refs):
        lands = refs[:n]
        send, recv = refs[n], refs[n + 1]
        x, y, c, chips = _place()
        me = (x, y, c)
        peers = [(x, y, 1 - c)] + [(*chip, c) for chip in chips]
        for t in range(n):
            own = _block(lands[t], axes[t], sizes[t], _dev(*me))
            for k, peer in enumerate(peers):
                _level1_copy(own, lands[t], axes[t], sizes[t], send, recv, 4 * t + k, me, peer).wait_send()
                _level1_copy(own, lands[t], axes[t], sizes[t], send, recv, 4 * t + k, peer, me).wait_recv()

    out = pl.pallas_call(
        body, name=name, out_shape=[jax.ShapeDtypeStruct(b.shape, b.dtype) for b in landing],
        in_specs=[_HBM_SPEC] * n + [_SEM_SPEC, _SEM_SPEC, pl.BlockSpec(memory_space=pl.ANY)],
        out_specs=[_HBM_SPEC] * n, input_output_aliases={i: i for i in range(n)},
        compiler_params=pltpu.CompilerParams(has_side_effects=_DATAFLOW),
    )(*landing, send_sems, recv_sems, after)
    return list(out)


def forward_task(landing, axes, sizes):
    n = len(landing)

    def forward(lands, send_sems, recv_sems, t, j, chip_core):
        x, y, c, _ = _place()
        blk = _block(lands[t], axes[t], sizes[t], _dev(*chip_core))
        return pltpu.make_async_remote_copy(src_ref=blk, dst_ref=blk, send_sem=send_sems.at[3 * t + j],
                                            recv_sem=recv_sems.at[3 * t + j], device_id=(x, y, 1 - c),
                                            device_id_type=MESH_DT)

    def start(ins, lands, send_sems, recv_sems, local_sems):
        _, _, c, chips = _place()
        for t in range(n):
            for j, chip in enumerate(chips):
                forward(lands, send_sems, recv_sems, t, j, (*chip, c)).start()

    def finish(ins, lands, send_sems, recv_sems, local_sems):
        _, _, c, chips = _place()
        for t in range(n):
            for j, chip in enumerate(chips):
                forward(lands, send_sems, recv_sems, t, j, (*chip, 1 - c)).wait_recv()
        for t in range(n):
            for j, chip in enumerate(chips):
                forward(lands, send_sems, recv_sems, t, j, (*chip, c)).wait_send()

    out_shape = [jax.ShapeDtypeStruct(b.shape, b.dtype) for b in landing]
    return CommTask(landing, out_shape, (3 * n, 3 * n, 1), start, _no_late, finish, in_place=True)


def rs_chip_sum(grad, recv, axis, core, name, tr=512):
    br, bc = _blk3(grad.shape, axis)
    tr = _row_tile(br, tr)
    nrb = br // tr

    if axis == 0:
        g_map = lambda i, r, c_ref: ((2 * i + c_ref[0]) * nrb + r, 0)
    else:
        g_map = lambda i, r, c_ref: (r, 2 * i + c_ref[0])

    def body(c_ref, g_ref, r_ref, o_ref):
        o_ref[...] = (g_ref[...].astype(F32) + r_ref[...].astype(F32)).astype(BF16)

    return pl.pallas_call(
        body, name=name,
        grid_spec=pltpu.PrefetchScalarGridSpec(
            num_scalar_prefetch=1, grid=(N_CHIP, nrb),
            in_specs=[pl.BlockSpec((tr, bc), g_map), pl.BlockSpec((None, tr, bc), lambda i, r, c_ref: (i, r, 0))],
            out_specs=pl.BlockSpec((None, tr, bc), lambda i, r, c_ref: (i, r, 0))),
        out_shape=jax.ShapeDtypeStruct((N_CHIP, br, bc), BF16), compiler_params=_cparams(),
    )(core, grad, recv)


def _adamw(w, g, m, v):
    m = ADAM_B1 * m + (1.0 - ADAM_B1) * g
    v = ADAM_B2 * v + (1.0 - ADAM_B2) * (g * g)
    m_hat = m / (1.0 - ADAM_B1 ** ADAM_STEP)
    v_hat = v / (1.0 - ADAM_B2 ** ADAM_STEP)
    delta = -ADAM_LR * (m_hat / (jnp.sqrt(v_hat) + ADAM_EPS) + ADAM_WD * w)
    return delta, m, v


def _sum_chips(p_ref):
    g = p_ref[0].astype(F32)
    for i in range(1, N_CHIP):
        g = g + p_ref[i].astype(F32)
    return g


def adam_sharded(parts, w, m, v, name, tr=256):
    r, c = w.shape
    assert parts.shape[2] == c
    tr = _row_tile(r, tr)

    def body(p_ref, w_ref, m_ref, v_ref, g_ref, d_ref, nm_ref, nv_ref):
        g = _sum_chips(p_ref)
        delta, nm, nv = _adamw(w_ref[...], g, m_ref[...], v_ref[...])
        g_ref[...] = g
        d_ref[...] = delta
        nm_ref[...] = nm
        nv_ref[...] = nv

    sp = pl.BlockSpec((tr, c), lambda i: (i, 0))
    return pl.pallas_call(
        body, name=name, grid=(r // tr,),
        in_specs=[pl.BlockSpec((N_CHIP, tr, c), lambda i: (0, i, 0)), sp, sp, sp],
        out_specs=[sp, sp, sp, sp], out_shape=[jax.ShapeDtypeStruct((r, c), F32)] * 4,
        compiler_params=_cparams(),
    )(parts, w, m, v)


def adam_small(g, w, m, v, name):
    def body(g_ref, w_ref, m_ref, v_ref, d_ref, nm_ref, nv_ref):
        delta, nm, nv = _adamw(w_ref[...], g_ref[...], m_ref[...], v_ref[...])
        d_ref[...] = delta
        nm_ref[...] = nm
        nv_ref[...] = nv

    return pl.pallas_call(
        body, name=name, out_shape=[jax.ShapeDtypeStruct(w.shape, F32)] * 3, compiler_params=_cparams(),
    )(g, w, m, v)


def sum_devices(gathered, name, tr=512):
    _, r, c = gathered.shape
    tr = _row_tile(r, tr)

    def body(x_ref, o_ref):
        s = x_ref[0]
        for k in range(1, N_DEV):
            s = s + x_ref[k]
        o_ref[...] = s

    return pl.pallas_call(
        body, name=name, grid=(r // tr,), in_specs=[pl.BlockSpec((N_DEV, tr, c), lambda i: (0, i, 0))],
        out_specs=pl.BlockSpec((tr, c), lambda i: (i, 0)), out_shape=jax.ShapeDtypeStruct((r, c), F32),
        compiler_params=_cparams(),
    )(gathered)


def _pad_to(a, axis, mult):
    size = a.shape[axis]
    pad = (-size) % mult
    if pad == 0:
        return a
    cfg = [(0, 0)] * a.ndim
    cfg[axis] = (0, pad)
    return jnp.pad(a, cfg)


def _as2d(a):
    if a.ndim == 1:
        return a.reshape(1, -1)
    return a.reshape(-1, a.shape[-1])


def kernel(x, p, norm_ffn1, w1_gate, w1_up, w1_down, norm_mix, w_in, ssm_log_dt, ssm_a_re, ssm_a_im, ssm_b_re, ssm_b_im, ssm_c_re, ssm_c_im, ssm_d, ssm_w_glu, gmlp_norm_v, gmlp_w_s, gmlp_b_s, norm_ssm_out, norm_gmlp_out, w_out, norm_ffn2, w2_gate, w2_up, w2_down, norm_ple, w_ple_gate, w_ple_proj, norm_final, loss_target, m_norm_ffn1, m_w1_gate, m_w1_up, m_w1_down, m_norm_mix, m_w_in, m_ssm_log_dt, m_ssm_a_re, m_ssm_a_im, m_ssm_b_re, m_ssm_b_im, m_ssm_c_re, m_ssm_c_im, m_ssm_d, m_ssm_w_glu, m_gmlp_norm_v, m_gmlp_w_s, m_gmlp_b_s, m_norm_ssm_out, m_norm_gmlp_out, m_w_out, m_norm_ffn2, m_w2_gate, m_w2_up, m_w2_down, m_norm_ple, m_w_ple_gate, m_w_ple_proj, m_norm_final, v_norm_ffn1, v_w1_gate, v_w1_up, v_w1_down, v_norm_mix, v_w_in, v_ssm_log_dt, v_ssm_a_re, v_ssm_a_im, v_ssm_b_re, v_ssm_b_im, v_ssm_c_re, v_ssm_c_im, v_ssm_d, v_ssm_w_glu, v_gmlp_norm_v, v_gmlp_w_s, v_gmlp_b_s, v_norm_ssm_out, v_norm_gmlp_out, v_w_out, v_norm_ffn2, v_w2_gate, v_w2_up, v_w2_down, v_norm_ple, v_w_ple_gate, v_w_ple_proj, v_norm_final):
    weights = dict(
        norm_ffn1=norm_ffn1, w1_gate=w1_gate, w1_up=w1_up, w1_down=w1_down, norm_mix=norm_mix, w_in=w_in,
        ssm_log_dt=ssm_log_dt, ssm_a_re=ssm_a_re, ssm_a_im=ssm_a_im, ssm_b_re=ssm_b_re, ssm_b_im=ssm_b_im,
        ssm_c_re=ssm_c_re, ssm_c_im=ssm_c_im, ssm_d=ssm_d, ssm_w_glu=ssm_w_glu, gmlp_norm_v=gmlp_norm_v,
        gmlp_w_s=gmlp_w_s, gmlp_b_s=gmlp_b_s, norm_ssm_out=norm_ssm_out, norm_gmlp_out=norm_gmlp_out, w_out=w_out,
        norm_ffn2=norm_ffn2, w2_gate=w2_gate, w2_up=w2_up, w2_down=w2_down, norm_ple=norm_ple,
        w_ple_gate=w_ple_gate, w_ple_proj=w_ple_proj, norm_final=norm_final)
    moments_m = dict(
        norm_ffn1=m_norm_ffn1, w1_gate=m_w1_gate, w1_up=m_w1_up, w1_down=m_w1_down, norm_mix=m_norm_mix, w_in=m_w_in,
        ssm_log_dt=m_ssm_log_dt, ssm_a_re=m_ssm_a_re, ssm_a_im=m_ssm_a_im, ssm_b_re=m_ssm_b_re, ssm_b_im=m_ssm_b_im,
        ssm_c_re=m_ssm_c_re, ssm_c_im=m_ssm_c_im, ssm_d=m_ssm_d, ssm_w_glu=m_ssm_w_glu, gmlp_norm_v=m_gmlp_norm_v,
        gmlp_w_s=m_gmlp_w_s, gmlp_b_s=m_gmlp_b_s, norm_ssm_out=m_norm_ssm_out, norm_gmlp_out=m_norm_gmlp_out,
        w_out=m_w_out, norm_ffn2=m_norm_ffn2, w2_gate=m_w2_gate, w2_up=m_w2_up, w2_down=m_w2_down,
        norm_ple=m_norm_ple, w_ple_gate=m_w_ple_gate, w_ple_proj=m_w_ple_proj, norm_final=m_norm_final)
    moments_v = dict(
        norm_ffn1=v_norm_ffn1, w1_gate=v_w1_gate, w1_up=v_w1_up, w1_down=v_w1_down, norm_mix=v_norm_mix, w_in=v_w_in,
        ssm_log_dt=v_ssm_log_dt, ssm_a_re=v_ssm_a_re, ssm_a_im=v_ssm_a_im, ssm_b_re=v_ssm_b_re, ssm_b_im=v_ssm_b_im,
        ssm_c_re=v_ssm_c_re, ssm_c_im=v_ssm_c_im, ssm_d=v_ssm_d, ssm_w_glu=v_ssm_w_glu, gmlp_norm_v=v_gmlp_norm_v,
        gmlp_w_s=v_gmlp_w_s, gmlp_b_s=v_gmlp_b_s, norm_ssm_out=v_norm_ssm_out, norm_gmlp_out=v_norm_gmlp_out,
        w_out=v_w_out, norm_ffn2=v_norm_ffn2, w2_gate=v_w2_gate, w2_up=v_w2_up, w2_down=v_w2_down,
        norm_ple=v_norm_ple, w_ple_gate=v_w_ple_gate, w_ple_proj=v_w_ple_proj, norm_final=v_norm_final)
    names = list(weights)

    xs = x[0]
    ps = p[0, 0].astype(BF16)
    tgt = loss_target[0]
    d_model = xs.shape[1]
    d_ssm = d_model // 2
    n_groups = d_ssm // SSM_GROUP

    transposed = ("w1_gate", "w1_up", "w2_gate", "w2_up")
    big = {
        "w1_gate": 0, "w1_up": 0, "w1_down": 0, "w_in": 1, "ssm_w_glu": 0, "w_out": 0,
        "w2_gate": 0, "w2_up": 0, "w2_down": 0, "w_ple_gate": 0, "w_ple_proj": 1}
    big_names = list(big)

    def view(a, k):
        return a[0].T if k in transposed else a[0]

    def unview(a, k):
        return a.T[None] if k in transposed else a[None]

    shard = {k: _pad_to(view(weights[k], k).astype(BF16), big[k], LANE) for k in big_names}
    W = {}

    abar_r, abar_i, bbar_r, bbar_i = _ssm_discretize(ssm_log_dt[0], ssm_a_re[0], ssm_a_im[0], ssm_b_re[0], ssm_b_im[0])
    ssd_params = (abar_r, abar_i, bbar_r, bbar_i, ssm_c_re[0], ssm_c_im[0])
    (m1, m2r, m2i, m3, a16_r, a16_i), ssd_vjp = jax.vjp(_ssd_matrices, *ssd_params)
    ssd_mats = [m.astype(BF16) for m in (m1, m2r, m2i, m3)]
    apw_f = _scan_constants(a16_r, a16_i, False)
    apw_b = _scan_constants(a16_r, a16_i, True)
    causal = jnp.tril(jnp.ones((CHUNK, CHUNK), dtype=bool))
    wm = jnp.where(causal[None], gmlp_w_s[0], 0.0).astype(BF16)
    wmt = jnp.swapaxes(wm, 1, 2)
    bs = gmlp_b_s[0][:, :, None]

    groups = [["w1_gate"], ["w1_up"], ["w1_down"], ["w_in", "ssm_w_glu", "w_out"], ["w2_gate"], ["w2_up"],
              ["w2_down", "w_ple_gate", "w_ple_proj"]]
    order = [k for g in groups for k in g]
    place = {k: i for i, k in enumerate(order)}
    me = (4 * lax.axis_index("x") + 2 * lax.axis_index("y") + lax.axis_index("c")).astype(jnp.int32).reshape(1)
    size = {k: shard[k].shape[big[k]] for k in order}
    landing, sems = gather_start([place_own_block(shard[k], big[k], me, "place_" + k) for k in order],
                                 [big[k] for k in order], [size[k] for k in order],
                                 [[place[k] for k in g] for g in groups], "gather_start")

    def landed(g, after):
        axes_g, sizes_g = [big[k] for k in groups[g]], [size[k] for k in groups[g]]
        bufs = gather_wait([landing[place[k]] for k in groups[g]], axes_g, sizes_g, *sems[g], after,
                           "gather_wait_%d" % g)
        return forward_task(bufs, axes_g, sizes_g)

    def arrive(g, after):
        W.update(zip(groups[g], run_tasks([landed(g, after)], "gather_forward_%d" % g)[0]))

    def arrive_during(g, after, fn, *a, **kw):
        out, (got,) = fn(*a, tasks=[landed(g, after)], **kw)
        W.update(zip(groups[g], got))
        return out

    xn1 = rmsnorm_fwd(xs, norm_ffn1, "norm_ffn1")
    arrive(0, xn1)
    gate1 = matmul(xn1, W["w1_gate"], "nt", "ffn1_gate")
    arrive(1, gate1)
    up1, act1 = ffn_up(xn1, W["w1_up"], gate1, "ffn1_up")
    arrive(2, act1)
    h1 = matmul(act1, W["w1_down"], "nn", "ffn1_down", res=xs, scale=0.5)
    arrive(3, h1)
    xn2 = rmsnorm_fwd(h1, norm_mix, "norm_mix")
    z = matmul(xn2, W["w_in"], "nn", "proj_in")
    u_pairs = _to_pairs(z[:, :d_ssm].astype(BF16))
    u_prev = jnp.concatenate([jnp.zeros_like(u_pairs[:, :1]), u_pairs[:, :-1]], axis=1)
    y_pairs, state_r, state_i = s5c_fwd(u_pairs, u_prev, *ssd_mats, apw_f, "s5_fwd")
    y_pre, yg = s5_skip_gelu(_from_pairs(y_pairs), z, ssm_d, "s5_skip_gelu")
    glin = matmul(yg, W["ssm_w_glu"], "nn", "ssm_glu")
    y_gmlp = gmlp_fwd(z, gmlp_norm_v, wm, bs, "gmlp_fwd")
    ycat = mix_out_fwd(y_pre, glin, y_gmlp, norm_ssm_out, norm_gmlp_out, "mix_out")
    h2 = arrive_during(4, ycat, matmul, ycat, W["w_out"], "nn", "proj_out", res=h1)
    xn3 = rmsnorm_fwd(h2, norm_ffn2, "norm_ffn2")
    gate2 = arrive_during(5, xn3, matmul, xn3, W["w2_gate"], "nt", "ffn2_gate")
    up2, act2 = arrive_during(6, gate2, ffn_up, xn3, W["w2_up"], gate2, "ffn2_up")
    h3 = matmul(act2, W["w2_down"], "nn", "ffn2_down", res=h2, scale=0.5)
    xn4 = rmsnorm_fwd(h3, norm_ple, "norm_ple")
    pg_lin = matmul(xn4, W["w_ple_gate"], "nn", "ple_gate")
    pp = matmul(ps, W["w_ple_proj"], "nn", "ple_proj")
    h4 = ple_fwd(h3, pg_lin, pp, "ple_fwd")
    dh4, loss_part, g_norm_final = final_loss(h4, tgt, norm_final.reshape(1, -1), "final_loss")
    loss = lax.psum(loss_part[0, 0], ("x", "y", "c"))

    G = {}
    reduced = {}
    chip_part = {}
    wait_sibling, wait_chips = [], []
    core = lax.axis_index("c").astype(jnp.int32).reshape(1)

    def grad(name_, value):
        G[name_] = value
        wait_sibling.append(name_)

    def carry(fn, *a, levels="ab", extra=None, **kw):
        tasks, kinds = [], []
        if extra is not None:
            tasks.append(extra[0])
            kinds.append(("x", extra[1]))
        if "a" in levels and wait_sibling:
            group = list(wait_sibling)
            wait_sibling.clear()
            tasks.append(to_sibling_task([G[k] for k in group], [big[k] for k in group]))
            kinds.append(("a", group))
        if "b" in levels and wait_chips:
            group = list(wait_chips)
            wait_chips.clear()
            tasks.append(across_chips_task([chip_part[k] for k in group]))
            kinds.append(("b", group))
        if not tasks:
            return fn(*a, **kw)
        out, task_outs = fn(*a, tasks=tasks, **kw)
        for (kind, group), outs in zip(kinds, task_outs):
            if kind == "x":
                group(outs)
                continue
            for k, r in zip(group, outs):
                if kind == "a":
                    chip_part[k] = rs_chip_sum(G[k], r, big[k], core, "rs_sum_" + k)
                    wait_chips.append(k)
                else:
                    reduced[k] = r
        return out

    small = {}
    small["norm_final"] = g_norm_final
    dpp, dpg = ple_bwd(dh4, pg_lin, pp, "ple_bwd")
    grad("w_ple_proj", matmul(ps, dpp, "tn", "grad_ple_proj", out_dtype=BF16))
    grad("w_ple_gate", carry(matmul, xn4, dpg, "tn", "grad_ple_gate", out_dtype=BF16))
    dxn4 = carry(matmul, dpg, W["w_ple_gate"], "nt", "ple_gate_bwd")
    dh3, dh3b, small["norm_ple"] = rmsnorm_bwd(dxn4, h3, norm_ple, dh4, "norm_ple_bwd")

    def ffn_bwd(tag, dhb, xn, gate, up, act, wg, wu, wd, extra=None):
        dgate, dup = carry(ffn_bwd_act, dhb, W[wd], gate, up, tag + "_act_bwd", extra=extra)
        grad(wd, carry(matmul, act, dhb, "tn", tag + "_grad_down", out_dtype=BF16, scale=0.5))
        grad(wg, carry(matmul, dgate, xn, "tn", tag + "_grad_gate", out_dtype=BF16))
        grad(wu, carry(matmul, dup, xn, "tn", tag + "_grad_up", out_dtype=BF16))
        dxn = carry(matmul, dgate, W[wg], "nn", tag + "_gate_bwd")
        return carry(matmul, dup, W[wu], "nn", tag + "_up_bwd", res=dxn)

    dxn3 = ffn_bwd("ffn2", dh3b, xn3, gate2, up2, act2, "w2_gate", "w2_up", "w2_down")
    dh2, dh2b, small["norm_ffn2"] = rmsnorm_bwd(dxn3, h2, norm_ffn2, dh3, "norm_ffn2_bwd")

    grad("w_out", matmul(ycat, dh2b, "tn", "grad_out", out_dtype=BF16))
    dycat = carry(matmul, dh2b, W["w_out"], "nt", "proj_out_bwd")
    dyg_direct, dglin, dy_gmlp, small["norm_ssm_out"], small["norm_gmlp_out"] = mix_out_bwd(
        dycat, y_pre, glin, y_gmlp, norm_ssm_out, norm_gmlp_out, "mix_out_bwd")
    grad("ssm_w_glu", matmul(yg, dglin, "tn", "grad_glu", out_dtype=BF16))
    dyg = carry(matmul, dglin, W["ssm_w_glu"], "nt", "ssm_glu_bwd", res=dyg_direct, levels="a")
    gy, du_skip, small["ssm_d"] = s5_skip_gelu_bwd(dyg, y_pre, z, ssm_d, "s5_skip_gelu_bwd")
    dy_pairs = _to_pairs(gy)
    dy_next = jnp.concatenate([dy_pairs[:, 1:], jnp.zeros_like(dy_pairs[:, :1])], axis=1)
    du_pairs, d_m1, d_m2r, d_m2i, d_m3, d_a16_r, d_a16_i = s5c_bwd(
        u_pairs, dy_pairs, dy_next, state_r, state_i, *ssd_mats, apw_b, "s5_bwd")
    du = (_from_pairs(du_pairs) + du_skip).astype(BF16)
    (small["abar_r"], small["abar_i"], small["bbar_r"], small["bbar_i"], small["c_re"], small["c_im"]) = ssd_vjp(
        (d_m1, d_m2r, d_m2i, d_m3, jnp.sum(d_a16_r, axis=1).reshape(n_groups, SSM_STATE),
         jnp.sum(d_a16_i, axis=1).reshape(n_groups, SSM_STATE)))
    dzu, dzv, small["gmlp_norm_v"], g_wm, g_bs = gmlp_bwd(dy_gmlp, z, gmlp_norm_v, wm, wmt, bs, "gmlp_bwd")
    small["gmlp_w_s"] = g_wm
    small["gmlp_b_s"] = g_bs

    dz = jnp.concatenate([du, dzu, dzv], axis=1)
    grad("w_in", matmul(xn2, dz, "tn", "grad_in", out_dtype=BF16))
    dxn2 = carry(matmul, dz, W["w_in"], "nt", "proj_in_bwd")
    dh1, dh1b, small["norm_mix"] = rmsnorm_bwd(dxn2, h1, norm_mix, dh2, "norm_mix_bwd")

    def pack(parts):
        flat = jnp.concatenate([v.reshape(-1) for v in parts.values()])
        return _pad_to(flat, 0, SUBLANE * LANE).reshape(-1, LANE), flat.shape[0]

    def unpack(everyones, n, parts, tag):
        rows = everyones.shape[0] // N_DEV
        summed = sum_devices(everyones.reshape(N_DEV, rows, LANE), "sum_" + tag).reshape(-1)[:n]
        out, off = {}, 0
        for k, v in parts.items():
            out[k] = summed[off:off + v.size].reshape(v.shape)
            off += v.size
        return out

    early = dict(small)
    flat_early, n_early = pack(early)
    landed = []
    dxn1 = ffn_bwd("ffn1", dh1b, xn1, gate1, up1, act1, "w1_gate", "w1_up", "w1_down",
                   extra=(gather_task([flat_early], [0]), landed.extend))
    tot = unpack(landed[0], n_early, early, "small")
    grad_x, _, g_norm_ffn1 = rmsnorm_bwd(dxn1, xs, norm_ffn1, dh1, "norm_ffn1_bwd")
    assert not wait_sibling and not wait_chips and set(reduced) == set(big_names)
    last = {"norm_ffn1": g_norm_ffn1}
    flat_last, n_last = pack(last)
    ((everyones_last,),) = run_tasks([gather_task([flat_last], [0])], "gather_last")
    tot.update(unpack(everyones_last, n_last, last, "last"))

    out_g, out_d, out_m, out_v = {}, {}, {}, {}
    for k in big_names:
        g, dl, nm, nv = adam_sharded(reduced[k], view(weights[k], k), view(moments_m[k], k), view(moments_v[k], k),
                                     "adam_" + k)
        out_g[k], out_d[k], out_m[k], out_v[k] = unview(g, k), unview(dl, k), unview(nm, k), unview(nv, k)

    _, ssm_vjp = jax.vjp(_ssm_discretize, ssm_log_dt[0], ssm_a_re[0], ssm_a_im[0], ssm_b_re[0], ssm_b_im[0])
    g_log_dt, g_a_re, g_a_im, g_b_re, g_b_im = ssm_vjp((tot["abar_r"], tot["abar_i"], tot["bbar_r"], tot["bbar_i"]))
    small_grads = {
        "norm_ffn1": tot["norm_ffn1"], "norm_mix": tot["norm_mix"], "ssm_log_dt": g_log_dt, "ssm_a_re": g_a_re,
        "ssm_a_im": g_a_im, "ssm_b_re": g_b_re, "ssm_b_im": g_b_im, "ssm_c_re": tot["c_re"], "ssm_c_im": tot["c_im"],
        "ssm_d": tot["ssm_d"], "gmlp_norm_v": tot["gmlp_norm_v"],
        "gmlp_w_s": jnp.where(causal[None], tot["gmlp_w_s"], 0.0), "gmlp_b_s": tot["gmlp_b_s"],
        "norm_ssm_out": tot["norm_ssm_out"], "norm_gmlp_out": tot["norm_gmlp_out"], "norm_ffn2": tot["norm_ffn2"],
        "norm_ple": tot["norm_ple"], "norm_final": tot["norm_final"]}
    for k, g in small_grads.items():
        shp = weights[k].shape
        g2 = _as2d(g.reshape(shp))
        dl, nm, nv = adam_small(g2, _as2d(weights[k]), _as2d(moments_m[k]), _as2d(moments_v[k]), "adam_" + k)
        out_g[k], out_d[k], out_m[k], out_v[k] = g2.reshape(shp), dl.reshape(shp), nm.reshape(shp), nv.reshape(shp)

    return (loss, grad_x[None], *[out_g[k] for k in names], *[out_d[k] for k in names],
            *[out_m[k] for k in names], *[out_v[k] for k in names])
```

```python
import math

import jax
import jax.numpy as jnp
from jax import lax
from jax.experimental import pallas as pl
from jax.experimental.pallas import tpu as pltpu

F32 = jnp.float32
BF16 = jnp.bfloat16
MESH_DT = pl.DeviceIdType.MESH

N_DEV = 8
N_CHIP = 4
LANE = 128
SUBLANE = 8
VMEM_LIMIT = 56 * 1024 * 1024

EPS = 1e-6
SSM_GROUP = 16
SSM_STATE = 64
GROUPS_PER_BLOCK = LANE // SSM_GROUP
STATE_BLOCK = GROUPS_PER_BLOCK * SSM_STATE
GMLP_HEAD = 128
CHUNK = 128

ADAM_LR = 0.001
ADAM_B1 = 0.9
ADAM_B2 = 0.999
ADAM_EPS = 1e-08
ADAM_WD = 0.01
ADAM_STEP = 10

GELU_K = math.sqrt(2.0 / math.pi)
GELU_C = 0.044715


def _cparams():
    return pltpu.CompilerParams(vmem_limit_bytes=VMEM_LIMIT)


def _tile(n, pref):
    if n <= pref:
        return n
    t = (pref // LANE) * LANE
    while t > 0:
        if n % t == 0:
            return t
        t -= LANE
    return n


def _row_tile(n, pref):
    if n <= pref:
        return n
    t = (pref // SUBLANE) * SUBLANE
    while t > 0:
        if n % t == 0:
            return t
        t -= SUBLANE
    return n


def _gelu(x):
    t = jnp.tanh(GELU_K * (x + GELU_C * x * x * x))
    return 0.5 * x * (1.0 + t)


def _gelu_grad(x):
    t = jnp.tanh(GELU_K * (x + GELU_C * x * x * x))
    return 0.5 * (1.0 + t) + 0.5 * x * (1.0 - t * t) * (GELU_K * (1.0 + 3.0 * GELU_C * x * x))


def _sigmoid(x):
    return 0.5 * jnp.tanh(0.5 * x) + 0.5


_DN = {
    "nn": (((1,), (0,)), ((), ())),
    "nt": (((1,), (1,)), ((), ())),
    "tn": (((0,), (0,)), ((), ())),
}


def _dot(a, b, mode="nn"):
    return lax.dot_general(a, b, _DN[mode], preferred_element_type=F32)


class CommTask:
    def __init__(self, inputs, out_shape, n_sems, start, late, finish, in_place=False):
        self.inputs, self.out_shape, self.n_sems = list(inputs), list(out_shape), n_sems
        self.start, self.late, self.finish = start, late, finish
        self.in_place = in_place


def _task_aliases(tasks, first_in, first_out):
    aliases = {}
    for t in tasks:
        if t.in_place:
            aliases.update({first_in + i: first_out + i for i in range(len(t.inputs))})
        first_in += len(t.inputs)
        first_out += len(t.out_shape)
    return aliases


def _call(body, *, name, grid, in_specs, out_specs, out_shape, args, scratch_shapes=(), tasks=()):
    in_specs, out_specs, out_shape = list(in_specs), list(out_specs), list(out_shape)
    scratch_shapes = list(scratch_shapes)
    if not tasks:
        return pl.pallas_call(
            body, name=name, grid=grid, in_specs=in_specs, out_specs=out_specs, out_shape=out_shape,
            scratch_shapes=scratch_shapes, compiler_params=_cparams())(*args)
    n_in, n_out, n_scr = len(in_specs), len(out_specs), len(scratch_shapes)
    t_in = [len(t.inputs) for t in tasks]
    t_out = [len(t.out_shape) for t in tasks]
    late_step = grid[0] - max(1, grid[0] // 4)
    has_late = grid[0] >= 2

    def carried(*refs):
        pos = n_in
        task_ins = []
        for k in t_in:
            task_ins.append(refs[pos:pos + k])
            pos += k
        outs = refs[pos:pos + n_out]
        pos += n_out
        task_outs = []
        for k in t_out:
            task_outs.append(refs[pos:pos + k])
            pos += k
        scratch = refs[pos:pos + n_scr]
        pos += n_scr
        sems = [refs[pos + 3 * i:pos + 3 * i + 3] for i in range(len(tasks))]
        ids = [pl.program_id(d) for d in range(len(grid))]
        rest_zero = True
        for d in range(1, len(grid)):
            rest_zero = jnp.logical_and(rest_zero, ids[d] == 0)
        first = jnp.logical_and(ids[0] == 0, rest_zero)
        last = ids[0] == grid[0] - 1
        for d in range(1, len(grid)):
            last = jnp.logical_and(last, ids[d] == grid[d] - 1)

        @pl.when(first)
        def _():
            for t, ti, to, s in zip(tasks, task_ins, task_outs, sems):
                t.start(ti, to, *s)

        if has_late:
            @pl.when(jnp.logical_and(ids[0] == late_step, rest_zero))
            def _():
                for t, ti, to, s in zip(tasks, task_ins, task_outs, sems):
                    t.late(ti, to, *s)

        body(*refs[:n_in], *outs, *scratch)

        @pl.when(last)
        def _():
            for t, ti, to, s in zip(tasks, task_ins, task_outs, sems):
                if not has_late:
                    t.late(ti, to, *s)
                t.finish(ti, to, *s)

    any_spec = pl.BlockSpec(memory_space=pl.ANY)
    sem_shapes = [pltpu.SemaphoreType.DMA((n,)) for t in tasks for n in t.n_sems]
    res = pl.pallas_call(
        carried, name=name, grid=grid,
        in_specs=in_specs + [any_spec] * sum(t_in), out_specs=out_specs + [any_spec] * sum(t_out),
        out_shape=out_shape + [s for t in tasks for s in t.out_shape],
        input_output_aliases=_task_aliases(tasks, n_in, n_out),
        scratch_shapes=scratch_shapes + sem_shapes, compiler_params=_cparams(),
    )(*args, *[a for t in tasks for a in t.inputs])
    res = list(res)
    task_res, pos = [], n_out
    for k in t_out:
        task_res.append(res[pos:pos + k])
        pos += k
    return res[:n_out], task_res


def _mm_dims(a, b, mode):
    if mode == "nn":
        (m, k), (k2, n) = a.shape, b.shape
    elif mode == "nt":
        (m, k), (n, k2) = a.shape, b.shape
    else:
        (k, m), (k2, n) = a.shape, b.shape
    assert k == k2, (a.shape, b.shape, mode)
    return m, n, k


def _mm_specs(mode, tm, tn, tk):
    if mode == "tn":
        a_spec = pl.BlockSpec((tk, tm), lambda i, j, k: (k, i))
    else:
        a_spec = pl.BlockSpec((tm, tk), lambda i, j, k: (i, k))
    if mode == "nt":
        b_spec = pl.BlockSpec((tn, tk), lambda i, j, k: (j, k))
    else:
        b_spec = pl.BlockSpec((tk, tn), lambda i, j, k: (k, j))
    return a_spec, b_spec


def _accumulate(acc, nk, partial, emit):
    if nk == 1:
        emit(partial)
        return
    kk = pl.program_id(2)

    @pl.when(kk == 0)
    def _():
        acc[...] = partial

    @pl.when(kk > 0)
    def _():
        acc[...] += partial

    @pl.when(kk == nk - 1)
    def _():
        emit(acc[...])


def matmul(a, b, mode, name, out_dtype=F32, res=None, scale=1.0, tm=1024, tn=1024, tk=2048, tasks=()):
    m, n, k = _mm_dims(a, b, mode)
    tm, tn, tk = _tile(m, tm), _tile(n, tn), _tile(k, tk)
    nk = k // tk
    a_spec, b_spec = _mm_specs(mode, tm, tn, tk)
    o_spec = pl.BlockSpec((tm, tn), lambda i, j, k: (i, j))
    has_res = res is not None

    def body(*refs):
        if has_res:
            a_ref, b_ref, r_ref, o_ref, acc = refs
        else:
            a_ref, b_ref, o_ref, acc = refs

        def emit(v):
            if scale != 1.0:
                v = v * scale
            if has_res:
                v = r_ref[...] + v
            o_ref[...] = v.astype(out_dtype)

        _accumulate(acc, nk, _dot(a_ref[...], b_ref[...], mode), emit)

    out = _call(
        body, name=name, grid=(m // tm, n // tn, nk),
        in_specs=[a_spec, b_spec] + ([o_spec] if has_res else []), out_specs=[o_spec],
        out_shape=[jax.ShapeDtypeStruct((m, n), out_dtype)], args=(a, b) + ((res,) if has_res else ()),
        scratch_shapes=[pltpu.VMEM((tm, tn) if nk > 1 else (SUBLANE, LANE), F32)], tasks=tasks)
    return (out[0][0], out[1]) if tasks else out[0]


def ffn_up(xn, wu, gate, name, tm=1024, tn=1024, tk=2048, tasks=()):
    m, n, k = _mm_dims(xn, wu, "nt")
    tm, tn, tk = _tile(m, tm), _tile(n, tn), _tile(k, tk)
    nk = k // tk
    a_spec, b_spec = _mm_specs("nt", tm, tn, tk)
    o_spec = pl.BlockSpec((tm, tn), lambda i, j, k: (i, j))

    def body(a_ref, u_ref, gate_ref, gate_b_ref, up_b_ref, act_ref, acc):
        def emit(u):
            g = gate_ref[...]
            gate_b_ref[...] = g.astype(BF16)
            up_b_ref[...] = u.astype(BF16)
            act_ref[...] = (g * _sigmoid(g) * u).astype(BF16)

        _accumulate(acc, nk, _dot(a_ref[...], u_ref[...], "nt"), emit)

    out = _call(
        body, name=name, grid=(m // tm, n // tn, nk), in_specs=[a_spec, b_spec, o_spec],
        out_specs=[o_spec, o_spec, o_spec],
        out_shape=[jax.ShapeDtypeStruct((m, n), BF16), jax.ShapeDtypeStruct((m, n), BF16),
                   jax.ShapeDtypeStruct((m, n), BF16)],
        args=(xn, wu, gate), scratch_shapes=[pltpu.VMEM((tm, tn) if nk > 1 else (SUBLANE, LANE), F32)], tasks=tasks)
    return (tuple(out[0]), out[1]) if tasks else tuple(out)


def ffn_bwd_act(dh, wd, gate, up, name, tm=1024, tn=1024, tk=2048, tasks=()):
    m, n, k = _mm_dims(dh, wd, "nt")
    tm, tn, tk = _tile(m, tm), _tile(n, tn), _tile(k, tk)
    nk = k // tk
    a_spec, b_spec = _mm_specs("nt", tm, tn, tk)
    o_spec = pl.BlockSpec((tm, tn), lambda i, j, k: (i, j))

    def body(a_ref, b_ref, gate_ref, up_ref, dg_ref, du_ref, acc):
        def emit(total):
            dact = 0.5 * total
            g = gate_ref[...].astype(F32)
            sg = _sigmoid(g)
            du_ref[...] = (dact * (g * sg)).astype(BF16)
            dg_ref[...] = (dact * up_ref[...].astype(F32) * (sg * (1.0 + g * (1.0 - sg)))).astype(BF16)

        _accumulate(acc, nk, _dot(a_ref[...], b_ref[...], "nt"), emit)

    out = _call(
        body, name=name, grid=(m // tm, n // tn, nk), in_specs=[a_spec, b_spec, o_spec, o_spec],
        out_specs=[o_spec, o_spec],
        out_shape=[jax.ShapeDtypeStruct((m, n), BF16), jax.ShapeDtypeStruct((m, n), BF16)],
        args=(dh, wd, gate, up), scratch_shapes=[pltpu.VMEM((tm, tn) if nk > 1 else (SUBLANE, LANE), F32)],
        tasks=tasks)
    return (tuple(out[0]), out[1]) if tasks else tuple(out)


def _rows(t, d, tr):
    return pl.BlockSpec((tr, d), lambda i: (i, 0))


def _vec(d):
    return pl.BlockSpec((1, d), lambda i: (0, 0))


def rmsnorm_fwd(x, g, name, tr=512):
    t, d = x.shape
    tr = _row_tile(t, tr)

    def body(x_ref, g_ref, o_ref):
        xf = x_ref[...]
        r = lax.rsqrt(jnp.mean(xf * xf, axis=-1, keepdims=True) + EPS)
        o_ref[...] = (xf * r * g_ref[...]).astype(BF16)

    return pl.pallas_call(
        body, name=name, grid=(t // tr,), in_specs=[_rows(t, d, tr), _vec(d)], out_specs=_rows(t, d, tr),
        out_shape=jax.ShapeDtypeStruct((t, d), BF16), compiler_params=_cparams(),
    )(x, g)


def _rms_bwd(dxn, xf, g):
    r = lax.rsqrt(jnp.mean(xf * xf, axis=-1, keepdims=True) + EPS)
    xhat = xf * r
    dg = jnp.sum(dxn * xhat, axis=0, keepdims=True)
    dxh = dxn * g
    dx = r * (dxh - xhat * jnp.mean(dxh * xhat, axis=-1, keepdims=True))
    return dx, dg


def rmsnorm_bwd(dxn, x, g, dres, name, tr=256):
    t, d = x.shape
    tr = _row_tile(t, tr)

    def body(dxn_ref, x_ref, g_ref, dres_ref, o_ref, ob_ref, dg_ref):
        dx, dg = _rms_bwd(dxn_ref[...], x_ref[...], g_ref[...])
        out = dres_ref[...] + dx
        o_ref[...] = out
        ob_ref[...] = out.astype(BF16)

        @pl.when(pl.program_id(0) == 0)
        def _():
            dg_ref[...] = jnp.zeros_like(dg_ref)

        dg_ref[...] += dg

    return pl.pallas_call(
        body, name=name, grid=(t // tr,),
        in_specs=[_rows(t, d, tr), _rows(t, d, tr), _vec(d), _rows(t, d, tr)],
        out_specs=[_rows(t, d, tr), _rows(t, d, tr), _vec(d)],
        out_shape=[jax.ShapeDtypeStruct((t, d), F32), jax.ShapeDtypeStruct((t, d), BF16),
                   jax.ShapeDtypeStruct((1, d), F32)],
        compiler_params=_cparams(),
    )(dxn, x, g, dres)


def final_loss(h, target, g, name, tr=256):
    t, d = h.shape
    tr = _row_tile(t, tr)

    def body(h_ref, t_ref, g_ref, dh_ref, loss_ref, dg_ref):
        xf = h_ref[...]
        gg = g_ref[...]
        r = lax.rsqrt(jnp.mean(xf * xf, axis=-1, keepdims=True) + EPS)
        xhat = xf * r
        e = xhat * gg - t_ref[...]
        part = jnp.sum(jnp.sum(e * e, axis=1, keepdims=True), axis=0, keepdims=True) * (0.5 / d)
        dout = e * (1.0 / d)
        dg = jnp.sum(dout * xhat, axis=0, keepdims=True)
        dxh = dout * gg
        dh_ref[...] = r * (dxh - xhat * jnp.mean(dxh * xhat, axis=-1, keepdims=True))

        @pl.when(pl.program_id(0) == 0)
        def _():
            dg_ref[...] = jnp.zeros_like(dg_ref)
            loss_ref[...] = jnp.zeros_like(loss_ref)

        dg_ref[...] += dg
        loss_ref[...] += jnp.broadcast_to(part, loss_ref.shape)

    return pl.pallas_call(
        body, name=name, grid=(t // tr,),
        in_specs=[_rows(t, d, tr), _rows(t, d, tr), _vec(d)],
        out_specs=[_rows(t, d, tr), pl.BlockSpec((SUBLANE, LANE), lambda i: (0, 0)), _vec(d)],
        out_shape=[jax.ShapeDtypeStruct((t, d), F32), jax.ShapeDtypeStruct((SUBLANE, LANE), F32),
                   jax.ShapeDtypeStruct((1, d), F32)],
        compiler_params=_cparams(),
    )(h, target, g)


def ple_fwd(h, glin, pp, name, tr=512):
    t, d = h.shape
    tr = _row_tile(t, tr)

    def body(h_ref, gl_ref, pp_ref, o_ref):
        o_ref[...] = h_ref[...] + _sigmoid(gl_ref[...]) * pp_ref[...]

    sp = _rows(t, d, tr)
    return pl.pallas_call(
        body, name=name, grid=(t // tr,), in_specs=[sp, sp, sp], out_specs=sp,
        out_shape=jax.ShapeDtypeStruct((t, d), F32), compiler_params=_cparams(),
    )(h, glin, pp)


def ple_bwd(dh, glin, pp, name, tr=512):
    t, d = dh.shape
    tr = _row_tile(t, tr)

    def body(dh_ref, gl_ref, pp_ref, dpp_ref, dgl_ref):
        gate = _sigmoid(gl_ref[...])
        dh_ = dh_ref[...]
        dpp_ref[...] = (dh_ * gate).astype(BF16)
        dgl_ref[...] = (dh_ * pp_ref[...] * gate * (1.0 - gate)).astype(BF16)

    sp = _rows(t, d, tr)
    return pl.pallas_call(
        body, name=name, grid=(t // tr,), in_specs=[sp, sp, sp], out_specs=[sp, sp],
        out_shape=[jax.ShapeDtypeStruct((t, d), BF16), jax.ShapeDtypeStruct((t, d), BF16)],
        compiler_params=_cparams(),
    )(dh, glin, pp)


def mix_out_fwd(y_pre, glin, y_gmlp, g_so, g_go, name, tr=512):
    t, d = y_pre.shape
    tr = _row_tile(t, tr)

    def body(yp_ref, gl_ref, yg_ref, gs_ref, gg_ref, o_ref):
        ys = _gelu(yp_ref[...]) * _sigmoid(gl_ref[...])
        r = lax.rsqrt(jnp.mean(ys * ys, axis=-1, keepdims=True) + EPS)
        o_ref[:, 0:d] = (ys * r * gs_ref[...]).astype(BF16)
        yq = yg_ref[...]
        r2 = lax.rsqrt(jnp.mean(yq * yq, axis=-1, keepdims=True) + EPS)
        o_ref[:, d:2 * d] = (yq * r2 * gg_ref[...]).astype(BF16)

    sp = _rows(t, d, tr)
    return pl.pallas_call(
        body, name=name, grid=(t // tr,), in_specs=[sp, sp, sp, _vec(d), _vec(d)],
        out_specs=_rows(t, 2 * d, tr), out_shape=jax.ShapeDtypeStruct((t, 2 * d), BF16),
        compiler_params=_cparams(),
    )(y_pre, glin, y_gmlp, g_so, g_go)


def mix_out_bwd(dycat, y_pre, glin, y_gmlp, g_so, g_go, name, tr=256):
    t, d = y_pre.shape
    tr = _row_tile(t, tr)

    def body(dy_ref, yp_ref, gl_ref, yg_ref, gs_ref, gg_ref, dyg_ref, dl_ref, dyq_ref, dgs_ref, dgg_ref):
        yg = _gelu(yp_ref[...])
        sg = _sigmoid(gl_ref[...])
        dys, dgs = _rms_bwd(dy_ref[:, 0:d], yg * sg, gs_ref[...])
        dyg_ref[...] = dys * sg
        dl_ref[...] = (dys * yg * sg * (1.0 - sg)).astype(BF16)
        dyq, dgg = _rms_bwd(dy_ref[:, d:2 * d], yg_ref[...], gg_ref[...])
        dyq_ref[...] = dyq

        @pl.when(pl.program_id(0) == 0)
        def _():
            dgs_ref[...] = jnp.zeros_like(dgs_ref)
            dgg_ref[...] = jnp.zeros_like(dgg_ref)

        dgs_ref[...] += dgs
        dgg_ref[...] += dgg

    sp = _rows(t, d, tr)
    return pl.pallas_call(
        body, name=name, grid=(t // tr,),
        in_specs=[_rows(t, 2 * d, tr), sp, sp, sp, _vec(d), _vec(d)],
        out_specs=[sp, sp, sp, _vec(d), _vec(d)],
        out_shape=[jax.ShapeDtypeStruct((t, d), F32), jax.ShapeDtypeStruct((t, d), BF16),
                   jax.ShapeDtypeStruct((t, d), F32), jax.ShapeDtypeStruct((1, d), F32),
                   jax.ShapeDtypeStruct((1, d), F32)],
        compiler_params=_cparams(),
    )(dycat, y_pre, glin, y_gmlp, g_so, g_go)


SCAN_COLS = 512


def _scan_tile(xr, xi, const, cr, ci, reverse):
    for lvl, sh in enumerate((1, 2, 4)):
        ar, ai = const(2 * lvl), const(2 * lvl + 1)
        s = (SUBLANE - sh) if reverse else sh
        rr = pltpu.roll(xr, s, 0)
        ri = pltpu.roll(xi, s, 0)
        xr, xi = xr + ar * rr - ai * ri, xi + ar * ri + ai * rr
    pr, pi_ = const(6), const(7)
    xr, xi = xr + pr * cr - pi_ * ci, xi + pr * ci + pi_ * cr
    return xr, xi


def _bcast_row(x, row):
    return jnp.broadcast_to(x[row:row + 1, :], x.shape)


def s5_fwd(z, bc_r, bc_i, cc_r, cc_i, apw, dvec, name, tc=256, tasks=()):
    t = z.shape[0]
    nblk = bc_r.shape[0]
    d = nblk * LANE
    ns = nblk * STATE_BLOCK
    tc = _row_tile(t, tc)
    ntile = tc // SUBLANE

    def body(z_ref, br_ref, bi_ref, cr_ref, ci_ref, apw_ref, d_ref, y_ref, yg_ref, sr_ref, si_ref, carry):
        @pl.when(pl.program_id(0) == 0)
        def _():
            carry[...] = jnp.zeros_like(carry)

        for j in range(nblk):
            uj = z_ref[:, j * LANE:(j + 1) * LANE]
            ub = uj.astype(BF16)
            for q in range(STATE_BLOCK // SCAN_COLS):
                c0 = j * STATE_BLOCK + q * SCAN_COLS
                cs = pl.ds(c0, SCAN_COLS)
                bs = slice(q * SCAN_COLS, (q + 1) * SCAN_COLS)
                sr_ref[:, cs] = _dot(ub, br_ref[j, :, bs])
                si_ref[:, cs] = _dot(ub, bi_ref[j, :, bs])
                const = lambda k, cs=cs: apw_ref[k, :, cs]

                def tile(k, c, cs=cs, const=const):
                    rows = pl.ds(pl.multiple_of(k * SUBLANE, SUBLANE), SUBLANE)
                    xr, xi = _scan_tile(sr_ref[rows, cs], si_ref[rows, cs], const, c[0], c[1], False)
                    sr_ref[rows, cs] = xr
                    si_ref[rows, cs] = xi
                    return _bcast_row(xr, SUBLANE - 1), _bcast_row(xi, SUBLANE - 1)

                c_r, c_i = lax.fori_loop(0, ntile, tile, (carry[0, :, cs], carry[1, :, cs]))
                carry[0, :, cs] = c_r
                carry[1, :, cs] = c_i
            sb = pl.ds(j * STATE_BLOCK, STATE_BLOCK)
            y = (_dot(sr_ref[:, sb].astype(BF16), cr_ref[j]) - _dot(si_ref[:, sb].astype(BF16), ci_ref[j])
                 + d_ref[:, j * LANE:(j + 1) * LANE] * uj)
            y_ref[:, j * LANE:(j + 1) * LANE] = y
            yg_ref[:, j * LANE:(j + 1) * LANE] = _gelu(y).astype(BF16)

    full3 = lambda shp: pl.BlockSpec(shp, lambda i: (0, 0, 0))
    out = _call(
        body, name=name, grid=(t // tc,),
        in_specs=[pl.BlockSpec((tc, d), lambda i: (i, 0)), full3(bc_r.shape), full3(bc_i.shape),
                  full3(cc_r.shape), full3(cc_i.shape), full3(apw.shape), _vec(d)],
        out_specs=[pl.BlockSpec((tc, d), lambda i: (i, 0)), pl.BlockSpec((tc, d), lambda i: (i, 0)),
                   pl.BlockSpec((tc, ns), lambda i: (i, 0)), pl.BlockSpec((tc, ns), lambda i: (i, 0))],
        out_shape=[jax.ShapeDtypeStruct((t, d), F32), jax.ShapeDtypeStruct((t, d), BF16),
                   jax.ShapeDtypeStruct((t, ns), F32), jax.ShapeDtypeStruct((t, ns), F32)],
        args=(z, bc_r, bc_i, cc_r, cc_i, apw, dvec), scratch_shapes=[pltpu.VMEM((2, SUBLANE, ns), F32)], tasks=tasks)
    return (tuple(out[0]), out[1]) if tasks else tuple(out)


def s5_bwd(dyg, y_pre, z, sr, si, bc_r, bc_i, cc_r, cc_i, apw_rev, dvec, name, tc=128, tasks=()):
    t = z.shape[0]
    nblk = bc_r.shape[0]
    d = nblk * LANE
    ns = nblk * STATE_BLOCK
    tc = _row_tile(t, tc)
    ntile = tc // SUBLANE
    nchunk = t // tc
    tiles_per_chunk = tc // SUBLANE

    def body(dyg_ref, yp_ref, z_ref, sr_ref, si_ref, pr_ref, pi_ref, br_ref, bi_ref, cr_ref, ci_ref, apw_ref,
             d_ref, du_ref, gd_ref, gcr_ref, gci_ref, gbr_ref, gbi_ref, gar_ref, gai_ref, lr_ref, li_ref, carry):
        step = pl.program_id(0)

        @pl.when(step == 0)
        def _():
            carry[...] = jnp.zeros_like(carry)
            for ref in (gd_ref, gcr_ref, gci_ref, gbr_ref, gbi_ref, gar_ref, gai_ref):
                ref[...] = jnp.zeros_like(ref)

        first_chunk = (step == nchunk - 1).astype(F32)
        keep_prev = 1.0 - first_chunk
        row0 = lax.broadcasted_iota(jnp.int32, (SUBLANE, SCAN_COLS), 0) == 0

        for j in range(nblk):
            lanes = slice(j * LANE, (j + 1) * LANE)
            uj = z_ref[:, lanes]
            ub = uj.astype(BF16)
            gy = dyg_ref[:, lanes] * _gelu_grad(yp_ref[:, lanes])
            gyb = gy.astype(BF16)
            gd_ref[:, lanes] += jnp.sum(gy * uj, axis=0, keepdims=True)
            for q in range(STATE_BLOCK // SCAN_COLS):
                c0 = j * STATE_BLOCK + q * SCAN_COLS
                cs = pl.ds(c0, SCAN_COLS)
                bs = slice(q * SCAN_COLS, (q + 1) * SCAN_COLS)
                lr_ref[:, cs] = _dot(gyb, cr_ref[j, bs, :], "nt")
                li_ref[:, cs] = -_dot(gyb, ci_ref[j, bs, :], "nt")
                const = lambda k, cs=cs: apw_ref[k, :, cs]

                def one_tile(rows, prev_r, prev_i, c, cs=cs, const=const):
                    cr_, ci_, gar, gai = c
                    xr, xi = _scan_tile(lr_ref[rows, cs], li_ref[rows, cs], const, cr_, ci_, True)
                    lr_ref[rows, cs] = xr
                    li_ref[rows, cs] = xi
                    spr = jnp.where(row0, prev_r, pltpu.roll(sr_ref[rows, cs], 1, 0))
                    spi = jnp.where(row0, prev_i, pltpu.roll(si_ref[rows, cs], 1, 0))
                    gar = gar + xr * spr + xi * spi
                    gai = gai + xi * spr - xr * spi
                    return _bcast_row(xr, 0), _bcast_row(xi, 0), gar, gai

                def tile(k, c, cs=cs, one_tile=one_tile):
                    kk = ntile - 1 - k
                    rows = pl.ds(pl.multiple_of(kk * SUBLANE, SUBLANE), SUBLANE)
                    prow = pl.ds(pl.multiple_of((kk - 1) * SUBLANE, SUBLANE), SUBLANE)
                    prev_r = _bcast_row(sr_ref[prow, cs], SUBLANE - 1)
                    prev_i = _bcast_row(si_ref[prow, cs], SUBLANE - 1)
                    return one_tile(rows, prev_r, prev_i, c)

                zero = jnp.zeros((SUBLANE, SCAN_COLS), F32)
                c = lax.fori_loop(0, ntile - 1, tile, (carry[0, :, cs], carry[1, :, cs], zero, zero))
                prev_r = _bcast_row(pr_ref[:, cs], SUBLANE - 1) * keep_prev
                prev_i = _bcast_row(pi_ref[:, cs], SUBLANE - 1) * keep_prev
                c_r, c_i, gar, gai = one_tile(pl.ds(0, SUBLANE), prev_r, prev_i, c)
                carry[0, :, cs] = c_r
                carry[1, :, cs] = c_i
                gar_ref[:, cs] += gar
                gai_ref[:, cs] += gai
            sb = pl.ds(j * STATE_BLOCK, STATE_BLOCK)
            lrb = lr_ref[:, sb].astype(BF16)
            lib = li_ref[:, sb].astype(BF16)
            gcr_ref[j] += _dot(gyb, sr_ref[:, sb].astype(BF16), "tn")
            gci_ref[j] -= _dot(gyb, si_ref[:, sb].astype(BF16), "tn")
            gbr_ref[j] += _dot(ub, lrb, "tn")
            gbi_ref[j] += _dot(ub, lib, "tn")
            du = _dot(lrb, br_ref[j], "nt") + _dot(lib, bi_ref[j], "nt") + gy * d_ref[:, lanes]
            du_ref[:, lanes] = du.astype(BF16)

    rev = lambda i: (nchunk - 1 - i, 0)
    prev = lambda i: (jnp.maximum((nchunk - 1 - i) * tiles_per_chunk - 1, 0), 0)
    full3 = lambda shp: pl.BlockSpec(shp, lambda i: (0, 0, 0))
    acc3 = pl.BlockSpec((nblk, LANE, STATE_BLOCK), lambda i: (0, 0, 0))
    acc_rows = pl.BlockSpec((SUBLANE, ns), lambda i: (0, 0))
    out = _call(
        body, name=name, grid=(nchunk,),
        in_specs=[pl.BlockSpec((tc, d), rev), pl.BlockSpec((tc, d), rev), pl.BlockSpec((tc, d), rev),
                  pl.BlockSpec((tc, ns), rev), pl.BlockSpec((tc, ns), rev),
                  pl.BlockSpec((SUBLANE, ns), prev), pl.BlockSpec((SUBLANE, ns), prev),
                  full3(bc_r.shape), full3(bc_i.shape), full3(cc_r.shape), full3(cc_i.shape), full3(apw_rev.shape),
                  _vec(d)],
        out_specs=[pl.BlockSpec((tc, d), rev), _vec(d), acc3, acc3, acc3, acc3, acc_rows, acc_rows],
        out_shape=[jax.ShapeDtypeStruct((t, d), BF16), jax.ShapeDtypeStruct((1, d), F32)]
        + [jax.ShapeDtypeStruct((nblk, LANE, STATE_BLOCK), F32)] * 4
        + [jax.ShapeDtypeStruct((SUBLANE, ns), F32)] * 2,
        args=(dyg, y_pre, z, sr, si, sr, si, bc_r, bc_i, cc_r, cc_i, apw_rev, dvec),
        scratch_shapes=[pltpu.VMEM((tc, ns), F32), pltpu.VMEM((tc, ns), F32), pltpu.VMEM((2, SUBLANE, ns), F32)],
        tasks=tasks)
    return (tuple(out[0]), out[1]) if tasks else tuple(out)


def _cmul(a, b):
    return a[0] * b[0] - a[1] * b[1], a[0] * b[1] + a[1] * b[0]


def _scan_constants(abar_r, abar_i, reverse):
    ar = abar_r.reshape(1, -1)
    ai = abar_i.reshape(1, -1)
    if reverse:
        ai = -ai
    pw = [(ar, ai)]
    for _ in range(SUBLANE - 1):
        pw.append(_cmul(pw[-1], (ar, ai)))
    rows = lax.broadcasted_iota(jnp.int32, (SUBLANE, 1), 0)
    out = []
    for sh in (1, 2, 4):
        keep = (rows <= SUBLANE - 1 - sh) if reverse else (rows >= sh)
        for part in pw[sh - 1]:
            out.append(jnp.where(keep, part, 0.0))
    for comp in (0, 1):
        stack = jnp.concatenate([pw[k][comp] for k in range(SUBLANE)], axis=0)
        out.append(stack[::-1] if reverse else stack)
    return jnp.stack(out, axis=0).astype(F32)


def _ssm_discretize(log_dt, a_re, a_im, b_re, b_im):
    dt = jnp.exp(log_dt)[:, None]
    lr = jnp.minimum(a_re, -1e-4)
    li = a_im
    mag = jnp.exp(lr * dt)
    ang = li * dt
    abar_r = mag * jnp.cos(ang)
    abar_i = mag * jnp.sin(ang)
    den = lr * lr + li * li
    xr = abar_r - 1.0
    xi = abar_i
    zr = (xr * lr + xi * li) / den
    zi = (xi * lr - xr * li) / den
    bbar_r = zr[..., None] * b_re - zi[..., None] * b_im
    bbar_i = zr[..., None] * b_im + zi[..., None] * b_re
    return abar_r, abar_i, bbar_r, bbar_i


def _block_diag(w):
    g, a, b = w.shape
    nb = g // GROUPS_PER_BLOCK
    eye = jnp.eye(GROUPS_PER_BLOCK, dtype=w.dtype)
    w5 = w.reshape(nb, GROUPS_PER_BLOCK, a, b)
    out = w5[:, :, :, None, :] * eye[None, :, None, :, None]
    return out.reshape(nb, GROUPS_PER_BLOCK * a, GROUPS_PER_BLOCK * b)


def _block_diag_extract(m, a, b):
    nb = m.shape[0]
    eye = jnp.eye(GROUPS_PER_BLOCK, dtype=m.dtype)
    m5 = m.reshape(nb, GROUPS_PER_BLOCK, a, GROUPS_PER_BLOCK, b)
    out = jnp.sum(m5 * eye[None, :, None, :, None], axis=3)
    return out.reshape(nb * GROUPS_PER_BLOCK, a, b)


SSD_L = 16
PAIR = 2


def _pair_diag(w):
    g, a, b = w.shape
    eye = jnp.eye(PAIR, dtype=w.dtype)
    out = w.reshape(g // PAIR, PAIR, a, b)[:, :, :, None, :] * eye[None, :, None, :, None]
    return out.reshape(g // PAIR, PAIR * a, PAIR * b)


def _ssd_matrices(abar_r, abar_i, bbar_r, bbar_i, c_re, c_im):
    g, n = abar_r.shape
    p = bbar_r.shape[2]
    ell = SSD_L
    pw = [(jnp.ones_like(abar_r), jnp.zeros_like(abar_i))]
    for _ in range(ell):
        pw.append(_cmul(pw[-1], (abar_r, abar_i)))
    pr = jnp.stack([q[0] for q in pw])
    pi = jnp.stack([q[1] for q in pw])

    def c_times(kr, ki):
        return (c_re[None] * kr[:, :, None, :] - c_im[None] * ki[:, :, None, :],
                c_re[None] * ki[:, :, None, :] + c_im[None] * kr[:, :, None, :])

    car, cai = c_times(pr[:ell], pi[:ell])
    taps = jnp.einsum("kgpn,gnq->gkpq", car, bbar_r) - jnp.einsum("kgpn,gnq->gkpq", cai, bbar_i)
    lag = jnp.arange(ell)[None, :] - jnp.arange(ell)[:, None]
    onehot = (lag[None] == jnp.arange(ell)[:, None, None]).astype(F32)
    m1 = jnp.einsum("kst,gkpq->gsqtp", onehot, taps).reshape(g, ell * p, ell * p)

    rev_r, rev_i = pr[:ell][::-1], pi[:ell][::-1]
    m2r = rev_r[:, :, :, None] * bbar_r[None] - rev_i[:, :, :, None] * bbar_i[None]
    m2i = rev_r[:, :, :, None] * bbar_i[None] + rev_i[:, :, :, None] * bbar_r[None]
    m2r = jnp.transpose(m2r, (1, 0, 3, 2)).reshape(g, ell * p, n)
    m2i = jnp.transpose(m2i, (1, 0, 3, 2)).reshape(g, ell * p, n)

    car1, cai1 = c_times(pr[1:], pi[1:])
    m3r = jnp.transpose(car1, (1, 3, 0, 2)).reshape(g, n, ell * p)
    m3i = -jnp.transpose(cai1, (1, 3, 0, 2)).reshape(g, n, ell * p)
    m3 = jnp.concatenate([_pair_diag(m3r), _pair_diag(m3i)], axis=1)
    return m1, _pair_diag(m2r), _pair_diag(m2i), m3, pw[ell][0], pw[ell][1]


def _to_pairs(a, ell=SSD_L):
    t, d = a.shape
    nq = d // (PAIR * SSM_GROUP)
    return a.reshape(t // ell, ell, nq, PAIR, SSM_GROUP).transpose(2, 0, 3, 1, 4).reshape(nq, t // ell, -1)


def _from_pairs(a, ell=SSD_L):
    nq, nc, _ = a.shape
    return a.reshape(nq, nc, PAIR, ell, SSM_GROUP).transpose(1, 3, 0, 2, 4).reshape(nc * ell, -1)


def s5c_fwd(u, u_prev, m1, m2r, m2i, m3, apw, name):
    nq, nc, w = u.shape
    half = w // PAIR
    ns = m2r.shape[2]
    ntile = nc // SUBLANE

    def body(u_ref, up_ref, m1_ref, m2r_ref, m2i_ref, m3_ref, apw_ref, y_ref, sr_ref, si_ref):
        up = up_ref[...]
        sr_ref[...] = _dot(up, m2r_ref[...])
        si_ref[...] = _dot(up, m2i_ref[...])
        const = lambda k: apw_ref[k]

        def tile(k, c):
            rows = pl.ds(pl.multiple_of(k * SUBLANE, SUBLANE), SUBLANE)
            xr, xi = _scan_tile(sr_ref[rows, :], si_ref[rows, :], const, c[0], c[1], False)
            sr_ref[rows, :] = xr
            si_ref[rows, :] = xi
            return _bcast_row(xr, SUBLANE - 1), _bcast_row(xi, SUBLANE - 1)

        zero = jnp.zeros((SUBLANE, ns), F32)
        lax.fori_loop(0, ntile, tile, (zero, zero))
        state = jnp.concatenate([sr_ref[...].astype(BF16), si_ref[...].astype(BF16)], axis=1)
        carried = _dot(state, m3_ref[...])
        uu = u_ref[...]
        for h in range(PAIR):
            cols = slice(h * half, (h + 1) * half)
            y_ref[:, cols] = _dot(uu[:, cols], m1_ref[h]) + carried[:, cols]

    per_pair = lambda shp: pl.BlockSpec((None,) + shp, lambda q: (q, 0, 0))
    return pl.pallas_call(
        body, name=name, grid=(nq,),
        in_specs=[per_pair((nc, w)), per_pair((nc, w)), pl.BlockSpec((PAIR, half, half), lambda q: (q, 0, 0)),
                  per_pair(m2r.shape[1:]), per_pair(m2i.shape[1:]), per_pair(m3.shape[1:]),
                  pl.BlockSpec((8, SUBLANE, ns), lambda q: (0, 0, q))],
        out_specs=[per_pair((nc, w)), per_pair((nc, ns)), per_pair((nc, ns))],
        out_shape=[jax.ShapeDtypeStruct((nq, nc, w), F32), jax.ShapeDtypeStruct((nq, nc, ns), F32),
                   jax.ShapeDtypeStruct((nq, nc, ns), F32)],
        compiler_params=_cparams(),
    )(u, u_prev, m1, m2r, m2i, m3, apw)


def s5c_bwd(u, dy, dy_next, sr, si, m1, m2r, m2i, m3, apw_rev, name):
    nq, nc, w = u.shape
    half = w // PAIR
    ns = m2r.shape[2]
    ntile = nc // SUBLANE

    def body(u_ref, dy_ref, dyn_ref, sr_ref, si_ref, m1_ref, m2r_ref, m2i_ref, m3_ref, apw_ref,
             du_ref, dm1_ref, dm2r_ref, dm2i_ref, dm3_ref, dar_ref, dai_ref, lr_ref, li_ref):
        back = _dot(dyn_ref[...], m3_ref[...], "nt")
        lr_ref[...] = back[:, 0:ns]
        li_ref[...] = back[:, ns:2 * ns]
        const = lambda k: apw_ref[k]

        def tile(k, c):
            rows = pl.ds(pl.multiple_of((ntile - 1 - k) * SUBLANE, SUBLANE), SUBLANE)
            xr, xi = _scan_tile(lr_ref[rows, :], li_ref[rows, :], const, c[0], c[1], True)
            lr_ref[rows, :] = xr
            li_ref[rows, :] = xi
            s_r, s_i = sr_ref[rows, :], si_ref[rows, :]
            return (_bcast_row(xr, 0), _bcast_row(xi, 0), c[2] + xr * s_r + xi * s_i, c[3] + xi * s_r - xr * s_i)

        zero = jnp.zeros((SUBLANE, ns), F32)
        _, _, dar, dai = lax.fori_loop(0, ntile, tile, (zero, zero, zero, zero))
        dar_ref[...] = dar
        dai_ref[...] = dai
        lrb, lib = lr_ref[...].astype(BF16), li_ref[...].astype(BF16)
        uu, dyy = u_ref[...], dy_ref[...]
        from_state = _dot(lrb, m2r_ref[...], "nt") + _dot(lib, m2i_ref[...], "nt")
        for h in range(PAIR):
            cols = slice(h * half, (h + 1) * half)
            du_ref[:, cols] = _dot(dyy[:, cols], m1_ref[h], "nt") + from_state[:, cols]
            dm1_ref[h] = _dot(uu[:, cols], dyy[:, cols], "tn")
        dm2r_ref[...] = _dot(uu, lrb, "tn")
        dm2i_ref[...] = _dot(uu, lib, "tn")
        state = jnp.concatenate([sr_ref[...].astype(BF16), si_ref[...].astype(BF16)], axis=1)
        dm3_ref[...] = _dot(state, dyy, "tn")

    per_pair = lambda shp: pl.BlockSpec((None,) + shp, lambda q: (q, 0, 0))
    m1_spec = pl.BlockSpec((PAIR, half, half), lambda q: (q, 0, 0))
    return pl.pallas_call(
        body, name=name, grid=(nq,),
        in_specs=[per_pair((nc, w)), per_pair((nc, w)), per_pair((nc, w)), per_pair((nc, ns)), per_pair((nc, ns)),
                  m1_spec, per_pair(m2r.shape[1:]), per_pair(m2i.shape[1:]), per_pair(m3.shape[1:]),
                  pl.BlockSpec((8, SUBLANE, ns), lambda q: (0, 0, q))],
        out_specs=[per_pair((nc, w)), m1_spec, per_pair(m2r.shape[1:]), per_pair(m2i.shape[1:]),
                   per_pair(m3.shape[1:]), per_pair((SUBLANE, ns)), per_pair((SUBLANE, ns))],
        out_shape=[jax.ShapeDtypeStruct((nq, nc, w), F32), jax.ShapeDtypeStruct(m1.shape, F32),
                   jax.ShapeDtypeStruct(m2r.shape, F32), jax.ShapeDtypeStruct(m2i.shape, F32),
                   jax.ShapeDtypeStruct(m3.shape, F32), jax.ShapeDtypeStruct((nq, SUBLANE, ns), F32),
                   jax.ShapeDtypeStruct((nq, SUBLANE, ns), F32)],
        scratch_shapes=[pltpu.VMEM((nc, ns), F32), pltpu.VMEM((nc, ns), F32)], compiler_params=_cparams(),
    )(u, dy, dy_next, sr, si, m1, m2r, m2i, m3, apw_rev)


PAIRS_PER_BLOCK = LANE // (PAIR * SSM_GROUP)


def _lane_range(shape, start, size):
    lane = lax.broadcasted_iota(jnp.int32, shape, 1)
    return (lane >= start) & (lane < start + size)


def _pairs_from_rows(rows, qq):
    tiles = []
    for g2 in range(PAIR):
        src = (qq * PAIR + g2) * SSM_GROUP
        for half in range(SSD_L // SUBLANE):
            acc = jnp.zeros_like(rows[0])
            for tt in range(SUBLANE):
                dst = tt * SSM_GROUP
                moved = pltpu.roll(rows[half * SUBLANE + tt], (dst - src) % LANE, 1)
                acc = jnp.where(_lane_range(acc.shape, dst, SSM_GROUP), moved, acc)
            tiles.append(acc)
    return jnp.concatenate(tiles, axis=1)


def _rows_from_pairs(pairs, t):
    acc = jnp.zeros((pairs[0].shape[0], LANE), F32)
    src = (t % SUBLANE) * SSM_GROUP
    for qq in range(PAIRS_PER_BLOCK):
        for g2 in range(PAIR):
            k = g2 * (SSD_L // SUBLANE) + t // SUBLANE
            dst = (qq * PAIR + g2) * SSM_GROUP
            moved = pltpu.roll(pairs[qq][:, k * LANE:(k + 1) * LANE], (dst - src) % LANE, 1)
            acc = jnp.where(_lane_range(acc.shape, dst, SSM_GROUP), moved, acc)
    return acc


def _chunk_rows(ref, nc):
    return [ref[pl.ds(t, nc, stride=SSD_L), :] for t in range(SSD_L)]


def s5r_fwd(z, shift, m1, m2r, m2i, m3, apw, name):
    t = z.shape[0]
    nc = t // SSD_L
    nq, _, ns = m2r.shape
    nblk = nq // PAIRS_PER_BLOCK
    d = nblk * LANE
    half = m1.shape[1]
    ntile = nc // SUBLANE

    def body(u_ref, sh_ref, m1_ref, m2r_ref, m2i_ref, m3_ref, apw_ref, y_ref, sr_ref, si_ref, ys_ref):
        rows = _chunk_rows(u_ref, nc)
        for qq in range(PAIRS_PER_BLOCK):
            uu = _pairs_from_rows(rows, qq).astype(BF16)
            up = _dot(sh_ref[...], uu).astype(BF16)
            sr_ref[qq] = _dot(up, m2r_ref[qq])
            si_ref[qq] = _dot(up, m2i_ref[qq])
            const = lambda k, qq=qq: apw_ref[k, :, qq * ns:(qq + 1) * ns]

            def tile(k, c, qq=qq, const=const):
                r8 = pl.ds(pl.multiple_of(k * SUBLANE, SUBLANE), SUBLANE)
                xr, xi = _scan_tile(sr_ref[qq, r8, :], si_ref[qq, r8, :], const, c[0], c[1], False)
                sr_ref[qq, r8, :] = xr
                si_ref[qq, r8, :] = xi
                return _bcast_row(xr, SUBLANE - 1), _bcast_row(xi, SUBLANE - 1)

            zero = jnp.zeros((SUBLANE, ns), F32)
            lax.fori_loop(0, ntile, tile, (zero, zero))
            state = jnp.concatenate([sr_ref[qq].astype(BF16), si_ref[qq].astype(BF16)], axis=1)
            carried = _dot(state, m3_ref[qq])
            for h in range(PAIR):
                cols = slice(h * half, (h + 1) * half)
                ys_ref[qq, :, cols] = _dot(uu[:, cols], m1_ref[qq * PAIR + h]) + carried[:, cols]
        pairs = [ys_ref[qq] for qq in range(PAIRS_PER_BLOCK)]
        for tpos in range(SSD_L):
            y_ref[pl.ds(tpos, nc, stride=SSD_L), :] = _rows_from_pairs(pairs, tpos)

    blk3 = lambda a, n: pl.BlockSpec((n,) + a.shape[1:], lambda j: (j, 0, 0))
    return pl.pallas_call(
        body, name=name, grid=(nblk,),
        in_specs=[pl.BlockSpec((t, LANE), lambda j: (0, j)), pl.BlockSpec((nc, nc), lambda j: (0, 0)),
                  blk3(m1, PAIRS_PER_BLOCK * PAIR), blk3(m2r, PAIRS_PER_BLOCK), blk3(m2i, PAIRS_PER_BLOCK),
                  blk3(m3, PAIRS_PER_BLOCK), pl.BlockSpec((8, SUBLANE, PAIRS_PER_BLOCK * ns), lambda j: (0, 0, j))],
        out_specs=[pl.BlockSpec((t, LANE), lambda j: (0, j)),
                   pl.BlockSpec((PAIRS_PER_BLOCK, nc, ns), lambda j: (j, 0, 0)),
                   pl.BlockSpec((PAIRS_PER_BLOCK, nc, ns), lambda j: (j, 0, 0))],
        out_shape=[jax.ShapeDtypeStruct((t, d), F32), jax.ShapeDtypeStruct((nq, nc, ns), F32),
                   jax.ShapeDtypeStruct((nq, nc, ns), F32)],
        scratch_shapes=[pltpu.VMEM((PAIRS_PER_BLOCK, nc, PAIR * half), F32)], compiler_params=_cparams(),
    )(z, shift, m1, m2r, m2i, m3, apw)


def s5r_bwd(z, gy, shift, sr, si, m1, m2r, m2i, m3, apw_rev, name):
    t = z.shape[0]
    nc = t // SSD_L
    nq, _, ns = m2r.shape
    nblk = nq // PAIRS_PER_BLOCK
    d = nblk * LANE
    half = m1.shape[1]
    ntile = nc // SUBLANE

    def body(u_ref, gy_ref, sh_ref, sr_ref, si_ref, m1_ref, m2r_ref, m2i_ref, m3_ref, apw_ref,
             du_ref, dm1_ref, dm2r_ref, dm2i_ref, dm3_ref, dar_ref, dai_ref, lr_ref, li_ref, dus_ref):
        u_rows = _chunk_rows(u_ref, nc)
        gy_rows = _chunk_rows(gy_ref, nc)
        for qq in range(PAIRS_PER_BLOCK):
            uu = _pairs_from_rows(u_rows, qq).astype(BF16)
            dyy = _pairs_from_rows(gy_rows, qq).astype(BF16)
            dyn = _dot(sh_ref[...], dyy, "tn").astype(BF16)
            back = _dot(dyn, m3_ref[qq], "nt")
            lr_ref[...] = back[:, 0:ns]
            li_ref[...] = back[:, ns:2 * ns]
            const = lambda k, qq=qq: apw_ref[k, :, qq * ns:(qq + 1) * ns]

            def tile(k, c, qq=qq, const=const):
                r8 = pl.ds(pl.multiple_of((ntile - 1 - k) * SUBLANE, SUBLANE), SUBLANE)
                xr, xi = _scan_tile(lr_ref[r8, :], li_ref[r8, :], const, c[0], c[1], True)
                lr_ref[r8, :] = xr
                li_ref[r8, :] = xi
                s_r, s_i = sr_ref[qq, r8, :], si_ref[qq, r8, :]
                return (_bcast_row(xr, 0), _bcast_row(xi, 0), c[2] + xr * s_r + xi * s_i, c[3] + xi * s_r - xr * s_i)

            zero = jnp.zeros((SUBLANE, ns), F32)
            _, _, dar, dai = lax.fori_loop(0, ntile, tile, (zero, zero, zero, zero))
            dar_ref[qq] = dar
            dai_ref[qq] = dai
            lrb, lib = lr_ref[...].astype(BF16), li_ref[...].astype(BF16)
            from_state = _dot(lrb, m2r_ref[qq], "nt") + _dot(lib, m2i_ref[qq], "nt")
            for h in range(PAIR):
                cols = slice(h * half, (h + 1) * half)
                dus_ref[qq, :, cols] = _dot(dyy[:, cols], m1_ref[qq * PAIR + h], "nt") + from_state[:, cols]
                dm1_ref[qq * PAIR + h] = _dot(uu[:, cols], dyy[:, cols], "tn")
            dm2r_ref[qq] = _dot(uu, lrb, "tn")
            dm2i_ref[qq] = _dot(uu, lib, "tn")
            state = jnp.concatenate([sr_ref[qq].astype(BF16), si_ref[qq].astype(BF16)], axis=1)
            dm3_ref[qq] = _dot(state, dyy, "tn")
        pairs = [dus_ref[qq] for qq in range(PAIRS_PER_BLOCK)]
        for tpos in range(SSD_L):
            du_ref[pl.ds(tpos, nc, stride=SSD_L), :] = _rows_from_pairs(pairs, tpos)

    blk3 = lambda a, n: pl.BlockSpec((n,) + a.shape[1:], lambda j: (j, 0, 0))
    cols_j = pl.BlockSpec((t, LANE), lambda j: (0, j))
    states = pl.BlockSpec((PAIRS_PER_BLOCK, nc, ns), lambda j: (j, 0, 0))
    partial = pl.BlockSpec((PAIRS_PER_BLOCK, SUBLANE, ns), lambda j: (j, 0, 0))
    return pl.pallas_call(
        body, name=name, grid=(nblk,),
        in_specs=[cols_j, cols_j, pl.BlockSpec((nc, nc), lambda j: (0, 0)), states, states,
                  blk3(m1, PAIRS_PER_BLOCK * PAIR), blk3(m2r, PAIRS_PER_BLOCK), blk3(m2i, PAIRS_PER_BLOCK),
                  blk3(m3, PAIRS_PER_BLOCK), pl.BlockSpec((8, SUBLANE, PAIRS_PER_BLOCK * ns), lambda j: (0, 0, j))],
        out_specs=[cols_j, blk3(m1, PAIRS_PER_BLOCK * PAIR), blk3(m2r, PAIRS_PER_BLOCK), blk3(m2i, PAIRS_PER_BLOCK),
                   blk3(m3, PAIRS_PER_BLOCK), partial, partial],
        out_shape=[jax.ShapeDtypeStruct((t, d), F32), jax.ShapeDtypeStruct(m1.shape, F32),
                   jax.ShapeDtypeStruct(m2r.shape, F32), jax.ShapeDtypeStruct(m2i.shape, F32),
                   jax.ShapeDtypeStruct(m3.shape, F32), jax.ShapeDtypeStruct((nq, SUBLANE, ns), F32),
                   jax.ShapeDtypeStruct((nq, SUBLANE, ns), F32)],
        scratch_shapes=[pltpu.VMEM((nc, ns), F32), pltpu.VMEM((nc, ns), F32),
                        pltpu.VMEM((PAIRS_PER_BLOCK, nc, PAIR * half), F32)],
        compiler_params=_cparams(),
    )(z, gy, shift, sr, si, m1, m2r, m2i, m3, apw_rev)


def s5_skip_gelu(y_lin, z, dvec, name, tr=512):
    t, d = y_lin.shape
    tr = _row_tile(t, tr)

    def body(y_ref, u_ref, d_ref, yp_ref, yg_ref):
        y = y_ref[...] + d_ref[...] * u_ref[...]
        yp_ref[...] = y
        yg_ref[...] = _gelu(y).astype(BF16)

    sp = _rows(t, d, tr)
    return pl.pallas_call(
        body, name=name, grid=(t // tr,), in_specs=[sp, sp, _vec(d)], out_specs=[sp, sp],
        out_shape=[jax.ShapeDtypeStruct((t, d), F32), jax.ShapeDtypeStruct((t, d), BF16)],
        compiler_params=_cparams(),
    )(y_lin, z, dvec)


def s5_skip_gelu_bwd(dyg, y_pre, z, dvec, name, tr=512):
    t, d = dyg.shape
    tr = _row_tile(t, tr)

    def body(dyg_ref, yp_ref, u_ref, d_ref, gy_ref, skip_ref, gd_ref):
        gy = dyg_ref[...] * _gelu_grad(yp_ref[...])
        gy_ref[...] = gy
        skip_ref[...] = gy * d_ref[...]

        @pl.when(pl.program_id(0) == 0)
        def _():
            gd_ref[...] = jnp.zeros_like(gd_ref)

        gd_ref[...] += jnp.sum(gy * u_ref[...], axis=0, keepdims=True)

    sp = _rows(t, d, tr)
    return pl.pallas_call(
        body, name=name, grid=(t // tr,), in_specs=[sp, sp, sp, _vec(d)], out_specs=[sp, sp, _vec(d)],
        out_shape=[jax.ShapeDtypeStruct((t, d), F32), jax.ShapeDtypeStruct((t, d), F32),
                   jax.ShapeDtypeStruct((1, d), F32)],
        compiler_params=_cparams(),
    )(dyg, y_pre, z, dvec)


def _layer_norm(gv, nv):
    mu = jnp.mean(gv, axis=-1, keepdims=True)
    xc = gv - mu
    r = lax.rsqrt(jnp.mean(xc * xc, axis=-1, keepdims=True) + EPS)
    xhat = xc * r
    return xhat * nv, xhat, r


def gmlp_fwd(z, norm_v, wm, bs, name, tr=256):
    t = z.shape[0]
    nh = wm.shape[0]
    d = nh * GMLP_HEAD
    col0 = (z.shape[1] - 2 * d) // d
    tr = _row_tile(t, tr)

    def body(zu_ref, zv_ref, nv_ref, wm_ref, bs_ref, o_ref):
        v, _, _ = _layer_norm(_gelu(zv_ref[...]), nv_ref[...])
        vb = v.astype(BF16)
        u = _gelu(zu_ref[...])
        for c in range(tr // CHUNK):
            rows = slice(c * CHUNK, (c + 1) * CHUNK)
            for h in range(nh):
                cols = slice(h * GMLP_HEAD, (h + 1) * GMLP_HEAD)
                s = _dot(wm_ref[h], vb[rows, cols]) + bs_ref[h]
                o_ref[rows, cols] = u[rows, cols] * s

    return pl.pallas_call(
        body, name=name, grid=(t // tr,),
        in_specs=[pl.BlockSpec((tr, d), lambda i: (i, col0)), pl.BlockSpec((tr, d), lambda i: (i, col0 + 1)),
                  _vec(d), pl.BlockSpec(wm.shape, lambda i: (0, 0, 0)), pl.BlockSpec(bs.shape, lambda i: (0, 0, 0))],
        out_specs=pl.BlockSpec((tr, d), lambda i: (i, 0)),
        out_shape=jax.ShapeDtypeStruct((t, d), F32), compiler_params=_cparams(),
    )(z, z, norm_v, wm, bs)


def gmlp_bwd(dy, z, norm_v, wm, wmt, bs, name, tr=256):
    t = z.shape[0]
    nh = wm.shape[0]
    d = nh * GMLP_HEAD
    col0 = (z.shape[1] - 2 * d) // d
    tr = _row_tile(t, tr)

    def body(dy_ref, zu_ref, zv_ref, nv_ref, wm_ref, wmt_ref, bs_ref, dzu_ref, dzv_ref, dnv_ref, dwm_ref, dbs_ref,
             dv_ref):
        @pl.when(pl.program_id(0) == 0)
        def _():
            dnv_ref[...] = jnp.zeros_like(dnv_ref)
            dwm_ref[...] = jnp.zeros_like(dwm_ref)
            dbs_ref[...] = jnp.zeros_like(dbs_ref)

        zv = zv_ref[...]
        nv = nv_ref[...]
        v, xhat, r = _layer_norm(_gelu(zv), nv)
        vb = v.astype(BF16)
        zu = zu_ref[...]
        u = _gelu(zu)
        dy_ = dy_ref[...]
        for c in range(tr // CHUNK):
            rows = slice(c * CHUNK, (c + 1) * CHUNK)
            for h in range(nh):
                cols = slice(h * GMLP_HEAD, (h + 1) * GMLP_HEAD)
                vh = vb[rows, cols]
                s = _dot(wm_ref[h], vh) + bs_ref[h]
                dyh = dy_[rows, cols]
                dzu_ref[rows, cols] = (dyh * s * _gelu_grad(zu[rows, cols])).astype(BF16)
                ds = dyh * u[rows, cols]
                dsb = ds.astype(BF16)
                dbs_ref[h] += jnp.sum(ds, axis=1, keepdims=True)
                dwm_ref[h] += _dot(dsb, vh, "nt")
                dv_ref[rows, cols] = _dot(wmt_ref[h], dsb)
        dv = dv_ref[...]
        dnv_ref[...] += jnp.sum(dv * xhat, axis=0, keepdims=True)
        dxh = dv * nv
        dgv = r * (dxh - jnp.mean(dxh, axis=-1, keepdims=True) - xhat * jnp.mean(dxh * xhat, axis=-1, keepdims=True))
        dzv_ref[...] = (dgv * _gelu_grad(zv)).astype(BF16)

    full3 = lambda shp: pl.BlockSpec(shp, lambda i: (0, 0, 0))
    rows_d = pl.BlockSpec((tr, d), lambda i: (i, 0))
    return pl.pallas_call(
        body, name=name, grid=(t // tr,),
        in_specs=[rows_d, pl.BlockSpec((tr, d), lambda i: (i, col0)), pl.BlockSpec((tr, d), lambda i: (i, col0 + 1)),
                  _vec(d), full3(wm.shape), full3(wmt.shape), full3(bs.shape)],
        out_specs=[rows_d, rows_d, _vec(d), full3((nh, CHUNK, CHUNK)), full3((nh, CHUNK, 1))],
        out_shape=[jax.ShapeDtypeStruct((t, d), BF16), jax.ShapeDtypeStruct((t, d), BF16),
                   jax.ShapeDtypeStruct((1, d), F32), jax.ShapeDtypeStruct((nh, CHUNK, CHUNK), F32),
                   jax.ShapeDtypeStruct((nh, CHUNK, 1), F32)],
        scratch_shapes=[pltpu.VMEM((tr, d), F32)], compiler_params=_cparams(),
    )(dy, z, z, norm_v, wm, wmt, bs)


def _block(ref, axis, size, k):
    start = pl.multiple_of(k * size, size)
    if axis == 0:
        return ref.at[pl.ds(start, size), :]
    return ref.at[:, pl.ds(start, size)]


def _place():
    x, y, c = lax.axis_index("x"), lax.axis_index("y"), lax.axis_index("c")
    chips = [(1 - x, y), (x, 1 - y), (1 - x, 1 - y)]
    return x, y, c, chips


def _dev(x, y, c):
    return 4 * x + 2 * y + c


def gather_task(shards, axes):
    n = len(shards)
    sizes = [s.shape[ax] for s, ax in zip(shards, axes)]
    out_shape = [
        jax.ShapeDtypeStruct((s.shape[0] * N_DEV, s.shape[1]) if ax == 0 else (s.shape[0], s.shape[1] * N_DEV), s.dtype)
        for s, ax in zip(shards, axes)
    ]

    def copy(ins, outs, send_sems, recv_sems, t, k, block, to, from_input=False):
        dst = _block(outs[t], axes[t], sizes[t], _dev(*block))
        return pltpu.make_async_remote_copy(
            src_ref=ins[t] if from_input else dst, dst_ref=dst,
            send_sem=send_sems.at[t * 7 + k], recv_sem=recv_sems.at[t * 7 + k],
            device_id=to, device_id_type=MESH_DT)

    def local(ins, outs, local_sems, t, me):
        return pltpu.make_async_copy(ins[t], _block(outs[t], axes[t], sizes[t], _dev(*me)), local_sems.at[t])

    def start(ins, outs, send_sems, recv_sems, local_sems):
        x, y, c, chips = _place()
        me, sibling = (x, y, c), (x, y, 1 - c)
        for t in range(n):
            local(ins, outs, local_sems, t, me).start()
        for t in range(n):
            copy(ins, outs, send_sems, recv_sems, t, 0, me, sibling, True).start()
            for j, chip in enumerate(chips):
                copy(ins, outs, send_sems, recv_sems, t, 1 + j, me, (*chip, c), True).start()

    def late(ins, outs, send_sems, recv_sems, local_sems):
        x, y, c, chips = _place()
        me, sibling = (x, y, c), (x, y, 1 - c)
        for t in range(n):
            for j, chip in enumerate(chips):
                copy(ins, outs, send_sems, recv_sems, t, 1 + j, (*chip, c), me).wait_recv()
                copy(ins, outs, send_sems, recv_sems, t, 4 + j, (*chip, c), sibling).start()

    def finish(ins, outs, send_sems, recv_sems, local_sems):
        x, y, c, chips = _place()
        me, sibling = (x, y, c), (x, y, 1 - c)
        for t in range(n):
            copy(ins, outs, send_sems, recv_sems, t, 0, sibling, me).wait_recv()
            for j, chip in enumerate(chips):
                copy(ins, outs, send_sems, recv_sems, t, 4 + j, (*chip, 1 - c), me).wait_recv()
        for t in range(n):
            copy(ins, outs, send_sems, recv_sems, t, 0, me, sibling, True).wait_send()
            for j, chip in enumerate(chips):
                copy(ins, outs, send_sems, recv_sems, t, 1 + j, me, (*chip, c), True).wait_send()
                copy(ins, outs, send_sems, recv_sems, t, 4 + j, (*chip, c), sibling).wait_send()
            local(ins, outs, local_sems, t, me).wait()

    return CommTask(shards, out_shape, (7 * n, 7 * n, n), start, late, finish)


def _blk3(shape2, axis):
    r, c = shape2
    return (r // N_DEV, c) if axis == 0 else (r, c // N_DEV)


def _no_late(ins, outs, send_sems, recv_sems, local_sems):
    pass


def to_sibling_task(grads, axes):
    n = len(grads)
    blks = [_blk3(g.shape, ax) for g, ax in zip(grads, axes)]
    sizes = [b[ax] for b, ax in zip(blks, axes)]

    def copies(ins, outs, send_sems, recv_sems):
        x, y, c, _ = _place()
        return [pltpu.make_async_remote_copy(
            src_ref=_block(ins[t], axes[t], sizes[t], 2 * i + (1 - c)), dst_ref=outs[t].at[i],
            send_sem=send_sems.at[t * N_CHIP + i], recv_sem=recv_sems.at[t * N_CHIP + i],
            device_id=(x, y, 1 - c), device_id_type=MESH_DT) for t in range(n) for i in range(N_CHIP)]

    def start(ins, outs, send_sems, recv_sems, local_sems):
        for cp in copies(ins, outs, send_sems, recv_sems):
            cp.start()

    def finish(ins, outs, send_sems, recv_sems, local_sems):
        cps = copies(ins, outs, send_sems, recv_sems)
        for cp in cps:
            cp.wait_recv()
        for cp in cps:
            cp.wait_send()

    out_shape = [jax.ShapeDtypeStruct((N_CHIP,) + b, g.dtype) for b, g in zip(blks, grads)]
    return CommTask(grads, out_shape, (N_CHIP * n, N_CHIP * n, 1), start, _no_late, finish)


def across_chips_task(parts):
    n = len(parts)

    def copies(ins, outs, send_sems, recv_sems):
        x, y, c, chips = _place()
        my_chip = 2 * x + y
        return [pltpu.make_async_remote_copy(
            src_ref=ins[t].at[2 * chip[0] + chip[1]], dst_ref=outs[t].at[my_chip],
            send_sem=send_sems.at[t * 3 + j], recv_sem=recv_sems.at[t * 3 + j],
            device_id=(*chip, c), device_id_type=MESH_DT) for t in range(n) for j, chip in enumerate(chips)]

    def mine(ins, outs, local_sems):
        x, y, _, _ = _place()
        my_chip = 2 * x + y
        return [pltpu.make_async_copy(ins[t].at[my_chip], outs[t].at[my_chip], local_sems.at[t]) for t in range(n)]

    def start(ins, outs, send_sems, recv_sems, local_sems):
        for cp in mine(ins, outs, local_sems):
            cp.start()
        for cp in copies(ins, outs, send_sems, recv_sems):
            cp.start()

    def finish(ins, outs, send_sems, recv_sems, local_sems):
        cps = copies(ins, outs, send_sems, recv_sems)
        for cp in cps:
            cp.wait_recv()
        for cp in cps:
            cp.wait_send()
        for cp in mine(ins, outs, local_sems):
            cp.wait()

    out_shape = [jax.ShapeDtypeStruct(p.shape, p.dtype) for p in parts]
    return CommTask(parts, out_shape, (3 * n, 3 * n, n), start, _no_late, finish)


def run_tasks(tasks, name):
    t_in = [len(t.inputs) for t in tasks]
    t_out = [len(t.out_shape) for t in tasks]

    def body(*refs):
        pos, views = 0, []
        for k in t_in:
            views.append([refs[pos:pos + k]])
            pos += k
        for v, k in zip(views, t_out):
            v.append(refs[pos:pos + k])
            pos += k
        for i, v in enumerate(views):
            v.extend(refs[pos + 3 * i:pos + 3 * i + 3])
        for phase in ("start", "late", "finish"):
            for t, v in zip(tasks, views):
                getattr(t, phase)(*v)

    any_spec = pl.BlockSpec(memory_space=pl.ANY)
    res = pl.pallas_call(
        body, name=name, in_specs=[any_spec] * sum(t_in), out_specs=[any_spec] * sum(t_out),
        out_shape=[s for t in tasks for s in t.out_shape], input_output_aliases=_task_aliases(tasks, 0, 0),
        scratch_shapes=[pltpu.SemaphoreType.DMA((k,)) for t in tasks for k in t.n_sems],
    )(*[a for t in tasks for a in t.inputs])
    res, out, pos = list(res), [], 0
    for k in t_out:
        out.append(res[pos:pos + k])
        pos += k
    return out


_HBM_SPEC = pl.BlockSpec(memory_space=pl.ANY)
_SEM_SPEC = pl.BlockSpec(memory_space=pltpu.SEMAPHORE)
_DATAFLOW = pltpu.SideEffectType.DATAFLOW_SIDE_EFFECTING


def _full_shape(s, ax):
    return (s.shape[0] * N_DEV, s.shape[1]) if ax == 0 else (s.shape[0], s.shape[1] * N_DEV)


def _level1_copy(src, landing, axis, size, send_sems, recv_sems, slot, sender, to):
    dst = _block(landing, axis, size, _dev(*sender))
    return pltpu.make_async_remote_copy(src_ref=src, dst_ref=dst, send_sem=send_sems.at[slot],
                                        recv_sem=recv_sems.at[slot], device_id=to, device_id_type=MESH_DT)


def place_own_block(shard, axis, me, name, tr=256):
    r, c = shard.shape
    tr = _row_tile(r, tr)
    nrb = r // tr
    if axis == 0:
        o_map = lambda i, me_ref: (me_ref[0] * nrb + i, 0)
    else:
        o_map = lambda i, me_ref: (i, me_ref[0])

    def body(me_ref, x_ref, o_ref):
        o_ref[...] = x_ref[...]

    return pl.pallas_call(
        body, name=name,
        grid_spec=pltpu.PrefetchScalarGridSpec(
            num_scalar_prefetch=1, grid=(nrb,), in_specs=[pl.BlockSpec((tr, c), lambda i, me_ref: (i, 0))],
            out_specs=pl.BlockSpec((tr, c), o_map)),
        out_shape=jax.ShapeDtypeStruct(_full_shape(shard, axis), shard.dtype), compiler_params=_cparams(),
    )(me, shard)


def gather_start(landing, axes, sizes, groups, name):
    n = len(landing)

    def body(*refs):
        lands, sems = refs[:n], refs[2 * n:]
        x, y, c, chips = _place()
        me = (x, y, c)
        targets = [(x, y, 1 - c)] + [(*chip, c) for chip in chips]
        for g, members in enumerate(groups):
            for m, t in enumerate(members):
                own = _block(lands[t], axes[t], sizes[t], _dev(*me))
                for k, to in enumerate(targets):
                    _level1_copy(own, lands[t], axes[t], sizes[t], sems[2 * g], sems[2 * g + 1], 4 * m + k,
                                 me, to).start()

    out = pl.pallas_call(
        body, name=name,
        out_shape=[jax.ShapeDtypeStruct(b.shape, b.dtype) for b in landing]
        + [pltpu.SemaphoreType.DMA((4 * len(members),)) for members in groups for _ in (0, 1)],
        in_specs=[_HBM_SPEC] * n, out_specs=[_HBM_SPEC] * n + [_SEM_SPEC] * (2 * len(groups)),
        input_output_aliases={i: i for i in range(n)},
        compiler_params=pltpu.CompilerParams(has_side_effects=_DATAFLOW),
    )(*landing)
    out = list(out)
    sems = out[n:]
    return out[:n], [(sems[2 * g], sems[2 * g + 1]) for g in range(len(groups))]


def gather_wait(landing, axes, sizes, send_sems, recv_sems, after, name):
    n = len(landing)

    def body(*refs):
        lands = refs[:n]
        send, recv = refs[n], refs[n + 1]
        x, y, c, chips = _place()
        me = (x, y, c)
        peers = [(x, y, 1 - c)] + [(*chip, c) for chip in chips]
        for t in range(n):
            own = _block(lands[t], axes[t], sizes[t], _dev(*me))
            for k, peer in enumerate(peers):
                _level1_copy(own, lands[t], axes[t], sizes[t], send, recv, 4 * t + k, me, peer).wait_send()
                _level1_copy(own, lands[t], axes[t], sizes[t], send, recv, 4 * t + k, peer, me).wait_recv()

    out = pl.pallas_call(
        body, name=name, out_shape=[jax.ShapeDtypeStruct(b.shape, b.dtype) for b in landing],
        in_specs=[_HBM_SPEC] * n + [_SEM_SPEC, _SEM_SPEC, pl.BlockSpec(memory_space=pl.ANY)],
        out_specs=[_HBM_SPEC] * n, input_output_aliases={i: i for i in range(n)},
        compiler_params=pltpu.CompilerParams(has_side_effects=_DATAFLOW),
    )(*landing, send_sems, recv_sems, after)
    return list(out)


def forward_task(landing, axes, sizes):
    n = len(landing)

    def forward(lands, send_sems, recv_sems, t, j, chip_core):
        x, y, c, _ = _place()
        blk = _block(lands[t], axes[t], sizes[t], _dev(*chip_core))
        return pltpu.make_async_remote_copy(src_ref=blk, dst_ref=blk, send_sem=send_sems.at[3 * t + j],
                                            recv_sem=recv_sems.at[3 * t + j], device_id=(x, y, 1 - c),
                                            device_id_type=MESH_DT)

    def start(ins, lands, send_sems, recv_sems, local_sems):
        _, _, c, chips = _place()
        for t in range(n):
            for j, chip in enumerate(chips):
                forward(lands, send_sems, recv_sems, t, j, (*chip, c)).start()

    def finish(ins, lands, send_sems, recv_sems, local_sems):
        _, _, c, chips = _place()
        for t in range(n):
            for j, chip in enumerate(chips):
                forward(lands, send_sems, recv_sems, t, j, (*chip, 1 - c)).wait_recv()
        for t in range(n):
            for j, chip in enumerate(chips):
                forward(lands, send_sems, recv_sems, t, j, (*chip, c)).wait_send()

    out_shape = [jax.ShapeDtypeStruct(b.shape, b.dtype) for b in landing]
    return CommTask(landing, out_shape, (3 * n, 3 * n, 1), start, _no_late, finish, in_place=True)


def rs_chip_sum(grad, recv, axis, core, name, tr=512):
    br, bc = _blk3(grad.shape, axis)
    tr = _row_tile(br, tr)
    nrb = br // tr

    if axis == 0:
        g_map = lambda i, r, c_ref: ((2 * i + c_ref[0]) * nrb + r, 0)
    else:
        g_map = lambda i, r, c_ref: (r, 2 * i + c_ref[0])

    def body(c_ref, g_ref, r_ref, o_ref):
        o_ref[...] = (g_ref[...].astype(F32) + r_ref[...].astype(F32)).astype(BF16)

    return pl.pallas_call(
        body, name=name,
        grid_spec=pltpu.PrefetchScalarGridSpec(
            num_scalar_prefetch=1, grid=(N_CHIP, nrb),
            in_specs=[pl.BlockSpec((tr, bc), g_map), pl.BlockSpec((None, tr, bc), lambda i, r, c_ref: (i, r, 0))],
            out_specs=pl.BlockSpec((None, tr, bc), lambda i, r, c_ref: (i, r, 0))),
        out_shape=jax.ShapeDtypeStruct((N_CHIP, br, bc), BF16), compiler_params=_cparams(),
    )(core, grad, recv)


def _adamw(w, g, m, v):
    m = ADAM_B1 * m + (1.0 - ADAM_B1) * g
    v = ADAM_B2 * v + (1.0 - ADAM_B2) * (g * g)
    m_hat = m / (1.0 - ADAM_B1 ** ADAM_STEP)
    v_hat = v / (1.0 - ADAM_B2 ** ADAM_STEP)
    delta = -ADAM_LR * (m_hat / (jnp.sqrt(v_hat) + ADAM_EPS) + ADAM_WD * w)
    return delta, m, v


def _sum_chips(p_ref):
    g = p_ref[0].astype(F32)
    for i in range(1, N_CHIP):
        g = g + p_ref[i].astype(F32)
    return g


def adam_sharded(parts, w, m, v, name, tr=256):
    r, c = w.shape
    assert parts.shape[2] == c
    tr = _row_tile(r, tr)

    def body(p_ref, w_ref, m_ref, v_ref, g_ref, d_ref, nm_ref, nv_ref):
        g = _sum_chips(p_ref)
        delta, nm, nv = _adamw(w_ref[...], g, m_ref[...], v_ref[...])
        g_ref[...] = g
        d_ref[...] = delta
        nm_ref[...] = nm
        nv_ref[...] = nv

    sp = pl.BlockSpec((tr, c), lambda i: (i, 0))
    return pl.pallas_call(
        body, name=name, grid=(r // tr,),
        in_specs=[pl.BlockSpec((N_CHIP, tr, c), lambda i: (0, i, 0)), sp, sp, sp],
        out_specs=[sp, sp, sp, sp], out_shape=[jax.ShapeDtypeStruct((r, c), F32)] * 4,
        compiler_params=_cparams(),
    )(parts, w, m, v)


def adam_small(g, w, m, v, name):
    def body(g_ref, w_ref, m_ref, v_ref, d_ref, nm_ref, nv_ref):
        delta, nm, nv = _adamw(w_ref[...], g_ref[...], m_ref[...], v_ref[...])
        d_ref[...] = delta
        nm_ref[...] = nm
        nv_ref[...] = nv

    return pl.pallas_call(
        body, name=name, out_shape=[jax.ShapeDtypeStruct(w.shape, F32)] * 3, compiler_params=_cparams(),
    )(g, w, m, v)


def sum_devices(gathered, name, tr=512):
    _, r, c = gathered.shape
    tr = _row_tile(r, tr)

    def body(x_ref, o_ref):
        s = x_ref[0]
        for k in range(1, N_DEV):
            s = s + x_ref[k]
        o_ref[...] = s

    return pl.pallas_call(
        body, name=name, grid=(r // tr,), in_specs=[pl.BlockSpec((N_DEV, tr, c), lambda i: (0, i, 0))],
        out_specs=pl.BlockSpec((tr, c), lambda i: (i, 0)), out_shape=jax.ShapeDtypeStruct((r, c), F32),
        compiler_params=_cparams(),
    )(gathered)


def _pad_to(a, axis, mult):
    size = a.shape[axis]
    pad = (-size) % mult
    if pad == 0:
        return a
    cfg = [(0, 0)] * a.ndim
    cfg[axis] = (0, pad)
    return jnp.pad(a, cfg)


def _as2d(a):
    if a.ndim == 1:
        return a.reshape(1, -1)
    return a.reshape(-1, a.shape[-1])


def kernel(x, p, norm_ffn1, w1_gate, w1_up, w1_down, norm_mix, w_in, ssm_log_dt, ssm_a_re, ssm_a_im, ssm_b_re, ssm_b_im, ssm_c_re, ssm_c_im, ssm_d, ssm_w_glu, gmlp_norm_v, gmlp_w_s, gmlp_b_s, norm_ssm_out, norm_gmlp_out, w_out, norm_ffn2, w2_gate, w2_up, w2_down, norm_ple, w_ple_gate, w_ple_proj, norm_final, loss_target, m_norm_ffn1, m_w1_gate, m_w1_up, m_w1_down, m_norm_mix, m_w_in, m_ssm_log_dt, m_ssm_a_re, m_ssm_a_im, m_ssm_b_re, m_ssm_b_im, m_ssm_c_re, m_ssm_c_im, m_ssm_d, m_ssm_w_glu, m_gmlp_norm_v, m_gmlp_w_s, m_gmlp_b_s, m_norm_ssm_out, m_norm_gmlp_out, m_w_out, m_norm_ffn2, m_w2_gate, m_w2_up, m_w2_down, m_norm_ple, m_w_ple_gate, m_w_ple_proj, m_norm_final, v_norm_ffn1, v_w1_gate, v_w1_up, v_w1_down, v_norm_mix, v_w_in, v_ssm_log_dt, v_ssm_a_re, v_ssm_a_im, v_ssm_b_re, v_ssm_b_im, v_ssm_c_re, v_ssm_c_im, v_ssm_d, v_ssm_w_glu, v_gmlp_norm_v, v_gmlp_w_s, v_gmlp_b_s, v_norm_ssm_out, v_norm_gmlp_out, v_w_out, v_norm_ffn2, v_w2_gate, v_w2_up, v_w2_down, v_norm_ple, v_w_ple_gate, v_w_ple_proj, v_norm_final):
    weights = dict(
        norm_ffn1=norm_ffn1, w1_gate=w1_gate, w1_up=w1_up, w1_down=w1_down, norm_mix=norm_mix, w_in=w_in,
        ssm_log_dt=ssm_log_dt, ssm_a_re=ssm_a_re, ssm_a_im=ssm_a_im, ssm_b_re=ssm_b_re, ssm_b_im=ssm_b_im,
        ssm_c_re=ssm_c_re, ssm_c_im=ssm_c_im, ssm_d=ssm_d, ssm_w_glu=ssm_w_glu, gmlp_norm_v=gmlp_norm_v,
        gmlp_w_s=gmlp_w_s, gmlp_b_s=gmlp_b_s, norm_ssm_out=norm_ssm_out, norm_gmlp_out=norm_gmlp_out, w_out=w_out,
        norm_ffn2=norm_ffn2, w2_gate=w2_gate, w2_up=w2_up, w2_down=w2_down, norm_ple=norm_ple,
        w_ple_gate=w_ple_gate, w_ple_proj=w_ple_proj, norm_final=norm_final)
    moments_m = dict(
        norm_ffn1=m_norm_ffn1, w1_gate=m_w1_gate, w1_up=m_w1_up, w1_down=m_w1_down, norm_mix=m_norm_mix, w_in=m_w_in,
        ssm_log_dt=m_ssm_log_dt, ssm_a_re=m_ssm_a_re, ssm_a_im=m_ssm_a_im, ssm_b_re=m_ssm_b_re, ssm_b_im=m_ssm_b_im,
        ssm_c_re=m_ssm_c_re, ssm_c_im=m_ssm_c_im, ssm_d=m_ssm_d, ssm_w_glu=m_ssm_w_glu, gmlp_norm_v=m_gmlp_norm_v,
        gmlp_w_s=m_gmlp_w_s, gmlp_b_s=m_gmlp_b_s, norm_ssm_out=m_norm_ssm_out, norm_gmlp_out=m_norm_gmlp_out,
        w_out=m_w_out, norm_ffn2=m_norm_ffn2, w2_gate=m_w2_gate, w2_up=m_w2_up, w2_down=m_w2_down,
        norm_ple=m_norm_ple, w_ple_gate=m_w_ple_gate, w_ple_proj=m_w_ple_proj, norm_final=m_norm_final)
    moments_v = dict(
        norm_ffn1=v_norm_ffn1, w1_gate=v_w1_gate, w1_up=v_w1_up, w1_down=v_w1_down, norm_mix=v_norm_mix, w_in=v_w_in,
        ssm_log_dt=v_ssm_log_dt, ssm_a_re=v_ssm_a_re, ssm_a_im=v_ssm_a_im, ssm_b_re=v_ssm_b_re, ssm_b_im=v_ssm_b_im,
        ssm_c_re=v_ssm_c_re, ssm_c_im=v_ssm_c_im, ssm_d=v_ssm_d, ssm_w_glu=v_ssm_w_glu, gmlp_norm_v=v_gmlp_norm_v,
        gmlp_w_s=v_gmlp_w_s, gmlp_b_s=v_gmlp_b_s, norm_ssm_out=v_norm_ssm_out, norm_gmlp_out=v_norm_gmlp_out,
        w_out=v_w_out, norm_ffn2=v_norm_ffn2, w2_gate=v_w2_gate, w2_up=v_w2_up, w2_down=v_w2_down,
        norm_ple=v_norm_ple, w_ple_gate=v_w_ple_gate, w_ple_proj=v_w_ple_proj, norm_final=v_norm_final)
    names = list(weights)

    xs = x[0]
    ps = p[0, 0].astype(BF16)
    tgt = loss_target[0]
    d_model = xs.shape[1]
    d_ssm = d_model // 2
    n_groups = d_ssm // SSM_GROUP

    transposed = ("w1_gate", "w1_up", "w2_gate", "w2_up")
    big = {
        "w1_gate": 0, "w1_up": 0, "w1_down": 0, "w_in": 1, "ssm_w_glu": 0, "w_out": 0,
        "w2_gate": 0, "w2_up": 0, "w2_down": 0, "w_ple_gate": 0, "w_ple_proj": 1}
    big_names = list(big)

    def view(a, k):
        return a[0].T if k in transposed else a[0]

    def unview(a, k):
        return a.T[None] if k in transposed else a[None]

    shard = {k: _pad_to(view(weights[k], k).astype(BF16), big[k], LANE) for k in big_names}
    W = {}

    abar_r, abar_i, bbar_r, bbar_i = _ssm_discretize(ssm_log_dt[0], ssm_a_re[0], ssm_a_im[0], ssm_b_re[0], ssm_b_im[0])
    bc_r = _block_diag(jnp.swapaxes(bbar_r, 1, 2)).astype(BF16)
    bc_i = _block_diag(jnp.swapaxes(bbar_i, 1, 2)).astype(BF16)
    cc_r = _block_diag(jnp.swapaxes(ssm_c_re[0], 1, 2)).astype(BF16)
    cc_i = _block_diag(jnp.swapaxes(ssm_c_im[0], 1, 2)).astype(BF16)
    apw_f = _scan_constants(abar_r, abar_i, False)
    apw_b = _scan_constants(abar_r, abar_i, True)
    causal = jnp.tril(jnp.ones((CHUNK, CHUNK), dtype=bool))
    wm = jnp.where(causal[None], gmlp_w_s[0], 0.0).astype(BF16)
    wmt = jnp.swapaxes(wm, 1, 2)
    bs = gmlp_b_s[0][:, :, None]

    groups = [["w1_gate"], ["w1_up"], ["w1_down"], ["w_in", "ssm_w_glu", "w_out"], ["w2_gate"], ["w2_up"],
              ["w2_down", "w_ple_gate", "w_ple_proj"]]
    order = [k for g in groups for k in g]
    place = {k: i for i, k in enumerate(order)}
    me = (4 * lax.axis_index("x") + 2 * lax.axis_index("y") + lax.axis_index("c")).astype(jnp.int32).reshape(1)
    size = {k: shard[k].shape[big[k]] for k in order}
    landing, sems = gather_start([place_own_block(shard[k], big[k], me, "place_" + k) for k in order],
                                 [big[k] for k in order], [size[k] for k in order],
                                 [[place[k] for k in g] for g in groups], "gather_start")

    def landed(g, after):
        axes_g, sizes_g = [big[k] for k in groups[g]], [size[k] for k in groups[g]]
        bufs = gather_wait([landing[place[k]] for k in groups[g]], axes_g, sizes_g, *sems[g], after,
                           "gather_wait_%d" % g)
        return forward_task(bufs, axes_g, sizes_g)

    def arrive(g, after):
        W.update(zip(groups[g], run_tasks([landed(g, after)], "gather_forward_%d" % g)[0]))

    def arrive_during(g, after, fn, *a, **kw):
        out, (got,) = fn(*a, tasks=[landed(g, after)], **kw)
        W.update(zip(groups[g], got))
        return out

    xn1 = rmsnorm_fwd(xs, norm_ffn1, "norm_ffn1")
    arrive(0, xn1)
    gate1 = matmul(xn1, W["w1_gate"], "nt", "ffn1_gate")
    arrive(1, gate1)
    gate1, up1, act1 = ffn_up(xn1, W["w1_up"], gate1, "ffn1_up")
    arrive(2, act1)
    h1 = matmul(act1, W["w1_down"], "nn", "ffn1_down", res=xs, scale=0.5)
    arrive(3, h1)
    xn2 = rmsnorm_fwd(h1, norm_mix, "norm_mix")
    z = matmul(xn2, W["w_in"], "nn", "proj_in")
    y_pre, yg, sr, si = s5_fwd(z, bc_r, bc_i, cc_r, cc_i, apw_f, ssm_d, "s5_fwd")
    glin = matmul(yg, W["ssm_w_glu"], "nn", "ssm_glu")
    y_gmlp = gmlp_fwd(z, gmlp_norm_v, wm, bs, "gmlp_fwd")
    ycat = mix_out_fwd(y_pre, glin, y_gmlp, norm_ssm_out, norm_gmlp_out, "mix_out")
    h2 = arrive_during(4, ycat, matmul, ycat, W["w_out"], "nn", "proj_out", res=h1)
    xn3 = rmsnorm_fwd(h2, norm_ffn2, "norm_ffn2")
    gate2 = arrive_during(5, xn3, matmul, xn3, W["w2_gate"], "nt", "ffn2_gate")
    gate2, up2, act2 = arrive_during(6, gate2, ffn_up, xn3, W["w2_up"], gate2, "ffn2_up")
    h3 = matmul(act2, W["w2_down"], "nn", "ffn2_down", res=h2, scale=0.5)
    xn4 = rmsnorm_fwd(h3, norm_ple, "norm_ple")
    pg_lin = matmul(xn4, W["w_ple_gate"], "nn", "ple_gate")
    pp = matmul(ps, W["w_ple_proj"], "nn", "ple_proj")
    h4 = ple_fwd(h3, pg_lin, pp, "ple_fwd")
    dh4, loss_part, g_norm_final = final_loss(h4, tgt, norm_final.reshape(1, -1), "final_loss")
    loss = lax.psum(loss_part[0, 0], ("x", "y", "c"))

    G = {}
    reduced = {}
    chip_part = {}
    wait_sibling, wait_chips = [], []
    core = lax.axis_index("c").astype(jnp.int32).reshape(1)

    def grad(name_, value):
        G[name_] = value
        wait_sibling.append(name_)

    def carry(fn, *a, levels="ab", extra=None, **kw):
        tasks, kinds = [], []
        if extra is not None:
            tasks.append(extra[0])
            kinds.append(("x", extra[1]))
        if "a" in levels and wait_sibling:
            group = list(wait_sibling)
            wait_sibling.clear()
            tasks.append(to_sibling_task([G[k] for k in group], [big[k] for k in group]))
            kinds.append(("a", group))
        if "b" in levels and wait_chips:
            group = list(wait_chips)
            wait_chips.clear()
            tasks.append(across_chips_task([chip_part[k] for k in group]))
            kinds.append(("b", group))
        if not tasks:
            return fn(*a, **kw)
        out, task_outs = fn(*a, tasks=tasks, **kw)
        for (kind, group), outs in zip(kinds, task_outs):
            if kind == "x":
                group(outs)
                continue
            for k, r in zip(group, outs):
                if kind == "a":
                    chip_part[k] = rs_chip_sum(G[k], r, big[k], core, "rs_sum_" + k)
                    wait_chips.append(k)
                else:
                    reduced[k] = r
        return out

    small = {}
    small["norm_final"] = g_norm_final
    dpp, dpg = ple_bwd(dh4, pg_lin, pp, "ple_bwd")
    grad("w_ple_proj", matmul(ps, dpp, "tn", "grad_ple_proj", out_dtype=BF16))
    grad("w_ple_gate", carry(matmul, xn4, dpg, "tn", "grad_ple_gate", out_dtype=BF16))
    dxn4 = carry(matmul, dpg, W["w_ple_gate"], "nt", "ple_gate_bwd")
    dh3, dh3b, small["norm_ple"] = rmsnorm_bwd(dxn4, h3, norm_ple, dh4, "norm_ple_bwd")

    def ffn_bwd(tag, dhb, xn, gate, up, act, wg, wu, wd, extra=None):
        dgate, dup = carry(ffn_bwd_act, dhb, W[wd], gate, up, tag + "_act_bwd", extra=extra)
        grad(wd, carry(matmul, act, dhb, "tn", tag + "_grad_down", out_dtype=BF16, scale=0.5))
        grad(wg, carry(matmul, dgate, xn, "tn", tag + "_grad_gate", out_dtype=BF16))
        grad(wu, carry(matmul, dup, xn, "tn", tag + "_grad_up", out_dtype=BF16))
        dxn = carry(matmul, dgate, W[wg], "nn", tag + "_gate_bwd")
        return carry(matmul, dup, W[wu], "nn", tag + "_up_bwd", res=dxn)

    dxn3 = ffn_bwd("ffn2", dh3b, xn3, gate2, up2, act2, "w2_gate", "w2_up", "w2_down")
    dh2, dh2b, small["norm_ffn2"] = rmsnorm_bwd(dxn3, h2, norm_ffn2, dh3, "norm_ffn2_bwd")

    grad("w_out", matmul(ycat, dh2b, "tn", "grad_out", out_dtype=BF16))
    dycat = carry(matmul, dh2b, W["w_out"], "nt", "proj_out_bwd")
    dyg_direct, dglin, dy_gmlp, small["norm_ssm_out"], small["norm_gmlp_out"] = mix_out_bwd(
        dycat, y_pre, glin, y_gmlp, norm_ssm_out, norm_gmlp_out, "mix_out_bwd")
    grad("ssm_w_glu", matmul(yg, dglin, "tn", "grad_glu", out_dtype=BF16))
    dyg = carry(matmul, dglin, W["ssm_w_glu"], "nt", "ssm_glu_bwd", res=dyg_direct, levels="a")
    du, small["ssm_d"], gc_r, gc_i, gb_r, gb_i, ga_r, ga_i = carry(
        s5_bwd, dyg, y_pre, z, sr, si, bc_r, bc_i, cc_r, cc_i, apw_b, ssm_d, "s5_bwd")
    dzu, dzv, small["gmlp_norm_v"], g_wm, g_bs = gmlp_bwd(dy_gmlp, z, gmlp_norm_v, wm, wmt, bs, "gmlp_bwd")
    small["gmlp_w_s"] = g_wm
    small["gmlp_b_s"] = g_bs
    small["c_re"] = _block_diag_extract(gc_r, SSM_GROUP, SSM_STATE)
    small["c_im"] = _block_diag_extract(gc_i, SSM_GROUP, SSM_STATE)
    small["bbar_r"] = jnp.swapaxes(_block_diag_extract(gb_r, SSM_GROUP, SSM_STATE), 1, 2)
    small["bbar_i"] = jnp.swapaxes(_block_diag_extract(gb_i, SSM_GROUP, SSM_STATE), 1, 2)
    small["abar_r"] = jnp.sum(ga_r, axis=0).reshape(n_groups, SSM_STATE)
    small["abar_i"] = jnp.sum(ga_i, axis=0).reshape(n_groups, SSM_STATE)

    dz = jnp.concatenate([du, dzu, dzv], axis=1)
    grad("w_in", matmul(xn2, dz, "tn", "grad_in", out_dtype=BF16))
    dxn2 = carry(matmul, dz, W["w_in"], "nt", "proj_in_bwd")
    dh1, dh1b, small["norm_mix"] = rmsnorm_bwd(dxn2, h1, norm_mix, dh2, "norm_mix_bwd")

    def pack(parts):
        flat = jnp.concatenate([v.reshape(-1) for v in parts.values()])
        return _pad_to(flat, 0, SUBLANE * LANE).reshape(-1, LANE), flat.shape[0]

    def unpack(everyones, n, parts, tag):
        rows = everyones.shape[0] // N_DEV
        summed = sum_devices(everyones.reshape(N_DEV, rows, LANE), "sum_" + tag).reshape(-1)[:n]
        out, off = {}, 0
        for k, v in parts.items():
            out[k] = summed[off:off + v.size].reshape(v.shape)
            off += v.size
        return out

    early = dict(small)
    flat_early, n_early = pack(early)
    landed = []
    dxn1 = ffn_bwd("ffn1", dh1b, xn1, gate1, up1, act1, "w1_gate", "w1_up", "w1_down",
                   extra=(gather_task([flat_early], [0]), landed.extend))
    tot = unpack(landed[0], n_early, early, "small")
    grad_x, _, g_norm_ffn1 = rmsnorm_bwd(dxn1, xs, norm_ffn1, dh1, "norm_ffn1_bwd")
    assert not wait_sibling and not wait_chips and set(reduced) == set(big_names)
    last = {"norm_ffn1": g_norm_ffn1}
    flat_last, n_last = pack(last)
    ((everyones_last,),) = run_tasks([gather_task([flat_last], [0])], "gather_last")
    tot.update(unpack(everyones_last, n_last, last, "last"))

    out_g, out_d, out_m, out_v = {}, {}, {}, {}
    for k in big_names:
        g, dl, nm, nv = adam_sharded(reduced[k], view(weights[k], k), view(moments_m[k], k), view(moments_v[k], k),
                                     "adam_" + k)
        out_g[k], out_d[k], out_m[k], out_v[k] = unview(g, k), unview(dl, k), unview(nm, k), unview(nv, k)

    _, ssm_vjp = jax.vjp(_ssm_discretize, ssm_log_dt[0], ssm_a_re[0], ssm_a_im[0], ssm_b_re[0], ssm_b_im[0])
    g_log_dt, g_a_re, g_a_im, g_b_re, g_b_im = ssm_vjp((tot["abar_r"], tot["abar_i"], tot["bbar_r"], tot["bbar_i"]))
    small_grads = {
        "norm_ffn1": tot["norm_ffn1"], "norm_mix": tot["norm_mix"], "ssm_log_dt": g_log_dt, "ssm_a_re": g_a_re,
        "ssm_a_im": g_a_im, "ssm_b_re": g_b_re, "ssm_b_im": g_b_im, "ssm_c_re": tot["c_re"], "ssm_c_im": tot["c_im"],
        "ssm_d": tot["ssm_d"], "gmlp_norm_v": tot["gmlp_norm_v"],
        "gmlp_w_s": jnp.where(causal[None], tot["gmlp_w_s"], 0.0), "gmlp_b_s": tot["gmlp_b_s"],
        "norm_ssm_out": tot["norm_ssm_out"], "norm_gmlp_out": tot["norm_gmlp_out"], "norm_ffn2": tot["norm_ffn2"],
        "norm_ple": tot["norm_ple"], "norm_final": tot["norm_final"]}
    for k, g in small_grads.items():
        shp = weights[k].shape
        g2 = _as2d(g.reshape(shp))
        dl, nm, nv = adam_small(g2, _as2d(weights[k]), _as2d(moments_m[k]), _as2d(moments_v[k]), "adam_" + k)
        out_g[k], out_d[k], out_m[k], out_v[k] = g2.reshape(shp), dl.reshape(shp), nm.reshape(shp), nv.reshape(shp)

    return (loss, grad_x[None], *[out_g[k] for k in names], *[out_d[k] for k in names],
            *[out_m[k] for k in names], *[out_v[k] for k in names])
```

```python
import math

import jax
import jax.numpy as jnp
from jax import lax
from jax.experimental import pallas as pl
from jax.experimental.pallas import tpu as pltpu

F32 = jnp.float32
BF16 = jnp.bfloat16
MESH_DT = pl.DeviceIdType.MESH

N_DEV = 8
N_CHIP = 4
LANE = 128
SUBLANE = 8
VMEM_LIMIT = 56 * 1024 * 1024

EPS = 1e-6
SSM_GROUP = 16
SSM_STATE = 64
GROUPS_PER_BLOCK = LANE // SSM_GROUP
STATE_BLOCK = GROUPS_PER_BLOCK * SSM_STATE
GMLP_HEAD = 128
CHUNK = 128

ADAM_LR = 0.001
ADAM_B1 = 0.9
ADAM_B2 = 0.999
ADAM_EPS = 1e-08
ADAM_WD = 0.01
ADAM_STEP = 10

GELU_K = math.sqrt(2.0 / math.pi)
GELU_C = 0.044715


def _cparams():
    return pltpu.CompilerParams(vmem_limit_bytes=VMEM_LIMIT)


def _tile(n, pref):
    if n <= pref:
        return n
    t = (pref // LANE) * LANE
    while t > 0:
        if n % t == 0:
            return t
        t -= LANE
    return n


def _row_tile(n, pref):
    if n <= pref:
        return n
    t = (pref // SUBLANE) * SUBLANE
    while t > 0:
        if n % t == 0:
            return t
        t -= SUBLANE
    return n


def _gelu(x):
    t = jnp.tanh(GELU_K * (x + GELU_C * x * x * x))
    return 0.5 * x * (1.0 + t)


def _gelu_grad(x):
    t = jnp.tanh(GELU_K * (x + GELU_C * x * x * x))
    return 0.5 * (1.0 + t) + 0.5 * x * (1.0 - t * t) * (GELU_K * (1.0 + 3.0 * GELU_C * x * x))


def _sigmoid(x):
    return 0.5 * jnp.tanh(0.5 * x) + 0.5


_DN = {
    "nn": (((1,), (0,)), ((), ())),
    "nt": (((1,), (1,)), ((), ())),
    "tn": (((0,), (0,)), ((), ())),
}


def _dot(a, b, mode="nn"):
    return lax.dot_general(a, b, _DN[mode], preferred_element_type=F32)


class CommTask:
    def __init__(self, inputs, out_shape, n_sems, start, late, finish, in_place=False):
        self.inputs, self.out_shape, self.n_sems = list(inputs), list(out_shape), n_sems
        self.start, self.late, self.finish = start, late, finish
        self.in_place = in_place


def _task_aliases(tasks, first_in, first_out):
    aliases = {}
    for t in tasks:
        if t.in_place:
            aliases.update({first_in + i: first_out + i for i in range(len(t.inputs))})
        first_in += len(t.inputs)
        first_out += len(t.out_shape)
    return aliases


def _call(body, *, name, grid, in_specs, out_specs, out_shape, args, scratch_shapes=(), tasks=()):
    in_specs, out_specs, out_shape = list(in_specs), list(out_specs), list(out_shape)
    scratch_shapes = list(scratch_shapes)
    if not tasks:
        return pl.pallas_call(
            body, name=name, grid=grid, in_specs=in_specs, out_specs=out_specs, out_shape=out_shape,
            scratch_shapes=scratch_shapes, compiler_params=_cparams())(*args)
    n_in, n_out, n_scr = len(in_specs), len(out_specs), len(scratch_shapes)
    t_in = [len(t.inputs) for t in tasks]
    t_out = [len(t.out_shape) for t in tasks]
    late_step = grid[0] - max(1, grid[0] // 4)
    has_late = grid[0] >= 2

    def carried(*refs):
        pos = n_in
        task_ins = []
        for k in t_in:
            task_ins.append(refs[pos:pos + k])
            pos += k
        outs = refs[pos:pos + n_out]
        pos += n_out
        task_outs = []
        for k in t_out:
            task_outs.append(refs[pos:pos + k])
            pos += k
        scratch = refs[pos:pos + n_scr]
        pos += n_scr
        sems = [refs[pos + 3 * i:pos + 3 * i + 3] for i in range(len(tasks))]
        ids = [pl.program_id(d) for d in range(len(grid))]
        rest_zero = True
        for d in range(1, len(grid)):
            rest_zero = jnp.logical_and(rest_zero, ids[d] == 0)
        first = jnp.logical_and(ids[0] == 0, rest_zero)
        last = ids[0] == grid[0] - 1
        for d in range(1, len(grid)):
            last = jnp.logical_and(last, ids[d] == grid[d] - 1)

        @pl.when(first)
        def _():
            for t, ti, to, s in zip(tasks, task_ins, task_outs, sems):
                t.start(ti, to, *s)

        if has_late:
            @pl.when(jnp.logical_and(ids[0] == late_step, rest_zero))
            def _():
                for t, ti, to, s in zip(tasks, task_ins, task_outs, sems):
                    t.late(ti, to, *s)

        body(*refs[:n_in], *outs, *scratch)

        @pl.when(last)
        def _():
            for t, ti, to, s in zip(tasks, task_ins, task_outs, sems):
                if not has_late:
                    t.late(ti, to, *s)
                t.finish(ti, to, *s)

    any_spec = pl.BlockSpec(memory_space=pl.ANY)
    sem_shapes = [pltpu.SemaphoreType.DMA((n,)) for t in tasks for n in t.n_sems]
    res = pl.pallas_call(
        carried, name=name, grid=grid,
        in_specs=in_specs + [any_spec] * sum(t_in), out_specs=out_specs + [any_spec] * sum(t_out),
        out_shape=out_shape + [s for t in tasks for s in t.out_shape],
        input_output_aliases=_task_aliases(tasks, n_in, n_out),
        scratch_shapes=scratch_shapes + sem_shapes, compiler_params=_cparams(),
    )(*args, *[a for t in tasks for a in t.inputs])
    res = list(res)
    task_res, pos = [], n_out
    for k in t_out:
        task_res.append(res[pos:pos + k])
        pos += k
    return res[:n_out], task_res


def _mm_dims(a, b, mode):
    if mode == "nn":
        (m, k), (k2, n) = a.shape, b.shape
    elif mode == "nt":
        (m, k), (n, k2) = a.shape, b.shape
    else:
        (k, m), (k2, n) = a.shape, b.shape
    assert k == k2, (a.shape, b.shape, mode)
    return m, n, k


def _mm_specs(mode, tm, tn, tk):
    if mode == "tn":
        a_spec = pl.BlockSpec((tk, tm), lambda i, j, k: (k, i))
    else:
        a_spec = pl.BlockSpec((tm, tk), lambda i, j, k: (i, k))
    if mode == "nt":
        b_spec = pl.BlockSpec((tn, tk), lambda i, j, k: (j, k))
    else:
        b_spec = pl.BlockSpec((tk, tn), lambda i, j, k: (k, j))
    return a_spec, b_spec


def _accumulate(acc, nk, partial, emit):
    if nk == 1:
        emit(partial)
        return
    kk = pl.program_id(2)

    @pl.when(kk == 0)
    def _():
        acc[...] = partial

    @pl.when(kk > 0)
    def _():
        acc[...] += partial

    @pl.when(kk == nk - 1)
    def _():
        emit(acc[...])


def matmul(a, b, mode, name, out_dtype=F32, res=None, scale=1.0, tm=1024, tn=1024, tk=2048, tasks=()):
    m, n, k = _mm_dims(a, b, mode)
    tm, tn, tk = _tile(m, tm), _tile(n, tn), _tile(k, tk)
    nk = k // tk
    a_spec, b_spec = _mm_specs(mode, tm, tn, tk)
    o_spec = pl.BlockSpec((tm, tn), lambda i, j, k: (i, j))
    has_res = res is not None

    def body(*refs):
        if has_res:
            a_ref, b_ref, r_ref, o_ref, acc = refs
        else:
            a_ref, b_ref, o_ref, acc = refs

        def emit(v):
            if scale != 1.0:
                v = v * scale
            if has_res:
                v = r_ref[...] + v
            o_ref[...] = v.astype(out_dtype)

        _accumulate(acc, nk, _dot(a_ref[...], b_ref[...], mode), emit)

    out = _call(
        body, name=name, grid=(m // tm, n // tn, nk),
        in_specs=[a_spec, b_spec] + ([o_spec] if has_res else []), out_specs=[o_spec],
        out_shape=[jax.ShapeDtypeStruct((m, n), out_dtype)], args=(a, b) + ((res,) if has_res else ()),
        scratch_shapes=[pltpu.VMEM((tm, tn) if nk > 1 else (SUBLANE, LANE), F32)], tasks=tasks)
    return (out[0][0], out[1]) if tasks else out[0]


def ffn_up(xn, wu, gate, name, tm=1024, tn=1024, tk=2048, tasks=()):
    m, n, k = _mm_dims(xn, wu, "nt")
    tm, tn, tk = _tile(m, tm), _tile(n, tn), _tile(k, tk)
    nk = k // tk
    a_spec, b_spec = _mm_specs("nt", tm, tn, tk)
    o_spec = pl.BlockSpec((tm, tn), lambda i, j, k: (i, j))

    def body(a_ref, u_ref, gate_ref, gate_b_ref, up_b_ref, act_ref, acc):
        def emit(u):
            g = gate_ref[...]
            gate_b_ref[...] = g.astype(BF16)
            up_b_ref[...] = u.astype(BF16)
            act_ref[...] = (g * _sigmoid(g) * u).astype(BF16)

        _accumulate(acc, nk, _dot(a_ref[...], u_ref[...], "nt"), emit)

    out = _call(
        body, name=name, grid=(m // tm, n // tn, nk), in_specs=[a_spec, b_spec, o_spec],
        out_specs=[o_spec, o_spec, o_spec],
        out_shape=[jax.ShapeDtypeStruct((m, n), BF16), jax.ShapeDtypeStruct((m, n), BF16),
                   jax.ShapeDtypeStruct((m, n), BF16)],
        args=(xn, wu, gate), scratch_shapes=[pltpu.VMEM((tm, tn) if nk > 1 else (SUBLANE, LANE), F32)], tasks=tasks)
    return (tuple(out[0]), out[1]) if tasks else tuple(out)


def ffn_bwd_act(dh, wd, gate, up, name, tm=1024, tn=1024, tk=2048, tasks=()):
    m, n, k = _mm_dims(dh, wd, "nt")
    tm, tn, tk = _tile(m, tm), _tile(n, tn), _tile(k, tk)
    nk = k // tk
    a_spec, b_spec = _mm_specs("nt", tm, tn, tk)
    o_spec = pl.BlockSpec((tm, tn), lambda i, j, k: (i, j))

    def body(a_ref, b_ref, gate_ref, up_ref, dg_ref, du_ref, acc):
        def emit(total):
            dact = 0.5 * total
            g = gate_ref[...].astype(F32)
            sg = _sigmoid(g)
            du_ref[...] = (dact * (g * sg)).astype(BF16)
            dg_ref[...] = (dact * up_ref[...].astype(F32) * (sg * (1.0 + g * (1.0 - sg)))).astype(BF16)

        _accumulate(acc, nk, _dot(a_ref[...], b_ref[...], "nt"), emit)

    out = _call(
        body, name=name, grid=(m // tm, n // tn, nk), in_specs=[a_spec, b_spec, o_spec, o_spec],
        out_specs=[o_spec, o_spec],
        out_shape=[jax.ShapeDtypeStruct((m, n), BF16), jax.ShapeDtypeStruct((m, n), BF16)],
        args=(dh, wd, gate, up), scratch_shapes=[pltpu.VMEM((tm, tn) if nk > 1 else (SUBLANE, LANE), F32)],
        tasks=tasks)
    return (tuple(out[0]), out[1]) if tasks else tuple(out)


def _rows(t, d, tr):
    return pl.BlockSpec((tr, d), lambda i: (i, 0))


def _vec(d):
    return pl.BlockSpec((1, d), lambda i: (0, 0))


def rmsnorm_fwd(x, g, name, tr=512):
    t, d = x.shape
    tr = _row_tile(t, tr)

    def body(x_ref, g_ref, o_ref):
        xf = x_ref[...]
        r = lax.rsqrt(jnp.mean(xf * xf, axis=-1, keepdims=True) + EPS)
        o_ref[...] = (xf * r * g_ref[...]).astype(BF16)

    return pl.pallas_call(
        body, name=name, grid=(t // tr,), in_specs=[_rows(t, d, tr), _vec(d)], out_specs=_rows(t, d, tr),
        out_shape=jax.ShapeDtypeStruct((t, d), BF16), compiler_params=_cparams(),
    )(x, g)


def _rms_bwd(dxn, xf, g):
    r = lax.rsqrt(jnp.mean(xf * xf, axis=-1, keepdims=True) + EPS)
    xhat = xf * r
    dg = jnp.sum(dxn * xhat, axis=0, keepdims=True)
    dxh = dxn * g
    dx = r * (dxh - xhat * jnp.mean(dxh * xhat, axis=-1, keepdims=True))
    return dx, dg


def rmsnorm_bwd(dxn, x, g, dres, name, tr=256):
    t, d = x.shape
    tr = _row_tile(t, tr)

    def body(dxn_ref, x_ref, g_ref, dres_ref, o_ref, ob_ref, dg_ref):
        dx, dg = _rms_bwd(dxn_ref[...], x_ref[...], g_ref[...])
        out = dres_ref[...] + dx
        o_ref[...] = out
        ob_ref[...] = out.astype(BF16)

        @pl.when(pl.program_id(0) == 0)
        def _():
            dg_ref[...] = jnp.zeros_like(dg_ref)

        dg_ref[...] += dg

    return pl.pallas_call(
        body, name=name, grid=(t // tr,),
        in_specs=[_rows(t, d, tr), _rows(t, d, tr), _vec(d), _rows(t, d, tr)],
        out_specs=[_rows(t, d, tr), _rows(t, d, tr), _vec(d)],
        out_shape=[jax.ShapeDtypeStruct((t, d), F32), jax.ShapeDtypeStruct((t, d), BF16),
                   jax.ShapeDtypeStruct((1, d), F32)],
        compiler_params=_cparams(),
    )(dxn, x, g, dres)


def final_loss(h, target, g, name, tr=256):
    t, d = h.shape
    tr = _row_tile(t, tr)

    def body(h_ref, t_ref, g_ref, dh_ref, loss_ref, dg_ref):
        xf = h_ref[...]
        gg = g_ref[...]
        r = lax.rsqrt(jnp.mean(xf * xf, axis=-1, keepdims=True) + EPS)
        xhat = xf * r
        e = xhat * gg - t_ref[...]
        part = jnp.sum(jnp.sum(e * e, axis=1, keepdims=True), axis=0, keepdims=True) * (0.5 / d)
        dout = e * (1.0 / d)
        dg = jnp.sum(dout * xhat, axis=0, keepdims=True)
        dxh = dout * gg
        dh_ref[...] = r * (dxh - xhat * jnp.mean(dxh * xhat, axis=-1, keepdims=True))

        @pl.when(pl.program_id(0) == 0)
        def _():
            dg_ref[...] = jnp.zeros_like(dg_ref)
            loss_ref[...] = jnp.zeros_like(loss_ref)

        dg_ref[...] += dg
        loss_ref[...] += jnp.broadcast_to(part, loss_ref.shape)

    return pl.pallas_call(
        body, name=name, grid=(t // tr,),
        in_specs=[_rows(t, d, tr), _rows(t, d, tr), _vec(d)],
        out_specs=[_rows(t, d, tr), pl.BlockSpec((SUBLANE, LANE), lambda i: (0, 0)), _vec(d)],
        out_shape=[jax.ShapeDtypeStruct((t, d), F32), jax.ShapeDtypeStruct((SUBLANE, LANE), F32),
                   jax.ShapeDtypeStruct((1, d), F32)],
        compiler_params=_cparams(),
    )(h, target, g)


def ple_fwd(h, glin, pp, name, tr=512):
    t, d = h.shape
    tr = _row_tile(t, tr)

    def body(h_ref, gl_ref, pp_ref, o_ref):
        o_ref[...] = h_ref[...] + _sigmoid(gl_ref[...]) * pp_ref[...]

    sp = _rows(t, d, tr)
    return pl.pallas_call(
        body, name=name, grid=(t // tr,), in_specs=[sp, sp, sp], out_specs=sp,
        out_shape=jax.ShapeDtypeStruct((t, d), F32), compiler_params=_cparams(),
    )(h, glin, pp)


def ple_bwd(dh, glin, pp, name, tr=512):
    t, d = dh.shape
    tr = _row_tile(t, tr)

    def body(dh_ref, gl_ref, pp_ref, dpp_ref, dgl_ref):
        gate = _sigmoid(gl_ref[...])
        dh_ = dh_ref[...]
        dpp_ref[...] = (dh_ * gate).astype(BF16)
        dgl_ref[...] = (dh_ * pp_ref[...] * gate * (1.0 - gate)).astype(BF16)

    sp = _rows(t, d, tr)
    return pl.pallas_call(
        body, name=name, grid=(t // tr,), in_specs=[sp, sp, sp], out_specs=[sp, sp],
        out_shape=[jax.ShapeDtypeStruct((t, d), BF16), jax.ShapeDtypeStruct((t, d), BF16)],
        compiler_params=_cparams(),
    )(dh, glin, pp)


def mix_out_fwd(y_pre, glin, y_gmlp, g_so, g_go, name, tr=512):
    t, d = y_pre.shape
    tr = _row_tile(t, tr)

    def body(yp_ref, gl_ref, yg_ref, gs_ref, gg_ref, o_ref):
        ys = _gelu(yp_ref[...]) * _sigmoid(gl_ref[...])
        r = lax.rsqrt(jnp.mean(ys * ys, axis=-1, keepdims=True) + EPS)
        o_ref[:, 0:d] = (ys * r * gs_ref[...]).astype(BF16)
        yq = yg_ref[...]
        r2 = lax.rsqrt(jnp.mean(yq * yq, axis=-1, keepdims=True) + EPS)
        o_ref[:, d:2 * d] = (yq * r2 * gg_ref[...]).astype(BF16)

    sp = _rows(t, d, tr)
    return pl.pallas_call(
        body, name=name, grid=(t // tr,), in_specs=[sp, sp, sp, _vec(d), _vec(d)],
        out_specs=_rows(t, 2 * d, tr), out_shape=jax.ShapeDtypeStruct((t, 2 * d), BF16),
        compiler_params=_cparams(),
    )(y_pre, glin, y_gmlp, g_so, g_go)


def mix_out_bwd(dycat, y_pre, glin, y_gmlp, g_so, g_go, name, tr=256):
    t, d = y_pre.shape
    tr = _row_tile(t, tr)

    def body(dy_ref, yp_ref, gl_ref, yg_ref, gs_ref, gg_ref, dyg_ref, dl_ref, dyq_ref, dgs_ref, dgg_ref):
        yg = _gelu(yp_ref[...])
        sg = _sigmoid(gl_ref[...])
        dys, dgs = _rms_bwd(dy_ref[:, 0:d], yg * sg, gs_ref[...])
        dyg_ref[...] = dys * sg
        dl_ref[...] = (dys * yg * sg * (1.0 - sg)).astype(BF16)
        dyq, dgg = _rms_bwd(dy_ref[:, d:2 * d], yg_ref[...], gg_ref[...])
        dyq_ref[...] = dyq

        @pl.when(pl.program_id(0) == 0)
        def _():
            dgs_ref[...] = jnp.zeros_like(dgs_ref)
            dgg_ref[...] = jnp.zeros_like(dgg_ref)

        dgs_ref[...] += dgs
        dgg_ref[...] += dgg

    sp = _rows(t, d, tr)
    return pl.pallas_call(
        body, name=name, grid=(t // tr,),
        in_specs=[_rows(t, 2 * d, tr), sp, sp, sp, _vec(d), _vec(d)],
        out_specs=[sp, sp, sp, _vec(d), _vec(d)],
        out_shape=[jax.ShapeDtypeStruct((t, d), F32), jax.ShapeDtypeStruct((t, d), BF16),
                   jax.ShapeDtypeStruct((t, d), F32), jax.ShapeDtypeStruct((1, d), F32),
                   jax.ShapeDtypeStruct((1, d), F32)],
        compiler_params=_cparams(),
    )(dycat, y_pre, glin, y_gmlp, g_so, g_go)


SCAN_COLS = 512


def _scan_tile(xr, xi, const, cr, ci, reverse):
    for lvl, sh in enumerate((1, 2, 4)):
        ar, ai = const(2 * lvl), const(2 * lvl + 1)
        s = (SUBLANE - sh) if reverse else sh
        rr = pltpu.roll(xr, s, 0)
        ri = pltpu.roll(xi, s, 0)
        xr, xi = xr + ar * rr - ai * ri, xi + ar * ri + ai * rr
    pr, pi_ = const(6), const(7)
    xr, xi = xr + pr * cr - pi_ * ci, xi + pr * ci + pi_ * cr
    return xr, xi


def _bcast_row(x, row):
    return jnp.broadcast_to(x[row:row + 1, :], x.shape)


def s5_fwd(z, bc_r, bc_i, cc_r, cc_i, apw, dvec, name, tc=256, tasks=()):
    t = z.shape[0]
    nblk = bc_r.shape[0]
    d = nblk * LANE
    ns = nblk * STATE_BLOCK
    tc = _row_tile(t, tc)
    ntile = tc // SUBLANE

    def body(z_ref, br_ref, bi_ref, cr_ref, ci_ref, apw_ref, d_ref, y_ref, yg_ref, sr_ref, si_ref, carry):
        @pl.when(pl.program_id(0) == 0)
        def _():
            carry[...] = jnp.zeros_like(carry)

        for j in range(nblk):
            uj = z_ref[:, j * LANE:(j + 1) * LANE]
            ub = uj.astype(BF16)
            for q in range(STATE_BLOCK // SCAN_COLS):
                c0 = j * STATE_BLOCK + q * SCAN_COLS
                cs = pl.ds(c0, SCAN_COLS)
                bs = slice(q * SCAN_COLS, (q + 1) * SCAN_COLS)
                sr_ref[:, cs] = _dot(ub, br_ref[j, :, bs])
                si_ref[:, cs] = _dot(ub, bi_ref[j, :, bs])
                const = lambda k, cs=cs: apw_ref[k, :, cs]

                def tile(k, c, cs=cs, const=const):
                    rows = pl.ds(pl.multiple_of(k * SUBLANE, SUBLANE), SUBLANE)
                    xr, xi = _scan_tile(sr_ref[rows, cs], si_ref[rows, cs], const, c[0], c[1], False)
                    sr_ref[rows, cs] = xr
                    si_ref[rows, cs] = xi
                    return _bcast_row(xr, SUBLANE - 1), _bcast_row(xi, SUBLANE - 1)

                c_r, c_i = lax.fori_loop(0, ntile, tile, (carry[0, :, cs], carry[1, :, cs]))
                carry[0, :, cs] = c_r
                carry[1, :, cs] = c_i
            sb = pl.ds(j * STATE_BLOCK, STATE_BLOCK)
            y = (_dot(sr_ref[:, sb].astype(BF16), cr_ref[j]) - _dot(si_ref[:, sb].astype(BF16), ci_ref[j])
                 + d_ref[:, j * LANE:(j + 1) * LANE] * uj)
            y_ref[:, j * LANE:(j + 1) * LANE] = y
            yg_ref[:, j * LANE:(j + 1) * LANE] = _gelu(y).astype(BF16)

    full3 = lambda shp: pl.BlockSpec(shp, lambda i: (0, 0, 0))
    out = _call(
        body, name=name, grid=(t // tc,),
        in_specs=[pl.BlockSpec((tc, d), lambda i: (i, 0)), full3(bc_r.shape), full3(bc_i.shape),
                  full3(cc_r.shape), full3(cc_i.shape), full3(apw.shape), _vec(d)],
        out_specs=[pl.BlockSpec((tc, d), lambda i: (i, 0)), pl.BlockSpec((tc, d), lambda i: (i, 0)),
                   pl.BlockSpec((tc, ns), lambda i: (i, 0)), pl.BlockSpec((tc, ns), lambda i: (i, 0))],
        out_shape=[jax.ShapeDtypeStruct((t, d), F32), jax.ShapeDtypeStruct((t, d), BF16),
                   jax.ShapeDtypeStruct((t, ns), F32), jax.ShapeDtypeStruct((t, ns), F32)],
        args=(z, bc_r, bc_i, cc_r, cc_i, apw, dvec), scratch_shapes=[pltpu.VMEM((2, SUBLANE, ns), F32)], tasks=tasks)
    return (tuple(out[0]), out[1]) if tasks else tuple(out)


def s5_bwd(dyg, y_pre, z, sr, si, bc_r, bc_i, cc_r, cc_i, apw_rev, dvec, name, tc=128, tasks=()):
    t = z.shape[0]
    nblk = bc_r.shape[0]
    d = nblk * LANE
    ns = nblk * STATE_BLOCK
    tc = _row_tile(t, tc)
    ntile = tc // SUBLANE
    nchunk = t // tc
    tiles_per_chunk = tc // SUBLANE

    def body(dyg_ref, yp_ref, z_ref, sr_ref, si_ref, pr_ref, pi_ref, br_ref, bi_ref, cr_ref, ci_ref, apw_ref,
             d_ref, du_ref, gd_ref, gcr_ref, gci_ref, gbr_ref, gbi_ref, gar_ref, gai_ref, lr_ref, li_ref, carry):
        step = pl.program_id(0)

        @pl.when(step == 0)
        def _():
            carry[...] = jnp.zeros_like(carry)
            for ref in (gd_ref, gcr_ref, gci_ref, gbr_ref, gbi_ref, gar_ref, gai_ref):
                ref[...] = jnp.zeros_like(ref)

        first_chunk = (step == nchunk - 1).astype(F32)
        keep_prev = 1.0 - first_chunk
        row0 = lax.broadcasted_iota(jnp.int32, (SUBLANE, SCAN_COLS), 0) == 0

        for j in range(nblk):
            lanes = slice(j * LANE, (j + 1) * LANE)
            uj = z_ref[:, lanes]
            ub = uj.astype(BF16)
            gy = dyg_ref[:, lanes] * _gelu_grad(yp_ref[:, lanes])
            gyb = gy.astype(BF16)
            gd_ref[:, lanes] += jnp.sum(gy * uj, axis=0, keepdims=True)
            for q in range(STATE_BLOCK // SCAN_COLS):
                c0 = j * STATE_BLOCK + q * SCAN_COLS
                cs = pl.ds(c0, SCAN_COLS)
                bs = slice(q * SCAN_COLS, (q + 1) * SCAN_COLS)
                lr_ref[:, cs] = _dot(gyb, cr_ref[j, bs, :], "nt")
                li_ref[:, cs] = -_dot(gyb, ci_ref[j, bs, :], "nt")
                const = lambda k, cs=cs: apw_ref[k, :, cs]

                def one_tile(rows, prev_r, prev_i, c, cs=cs, const=const):
                    cr_, ci_, gar, gai = c
                    xr, xi = _scan_tile(lr_ref[rows, cs], li_ref[rows, cs], const, cr_, ci_, True)
                    lr_ref[rows, cs] = xr
                    li_ref[rows, cs] = xi
                    spr = jnp.where(row0, prev_r, pltpu.roll(sr_ref[rows, cs], 1, 0))
                    spi = jnp.where(row0, prev_i, pltpu.roll(si_ref[rows, cs], 1, 0))
                    gar = gar + xr * spr + xi * spi
                    gai = gai + xi * spr - xr * spi
                    return _bcast_row(xr, 0), _bcast_row(xi, 0), gar, gai

                def tile(k, c, cs=cs, one_tile=one_tile):
                    kk = ntile - 1 - k
                    rows = pl.ds(pl.multiple_of(kk * SUBLANE, SUBLANE), SUBLANE)
                    prow = pl.ds(pl.multiple_of((kk - 1) * SUBLANE, SUBLANE), SUBLANE)
                    prev_r = _bcast_row(sr_ref[prow, cs], SUBLANE - 1)
                    prev_i = _bcast_row(si_ref[prow, cs], SUBLANE - 1)
                    return one_tile(rows, prev_r, prev_i, c)

                zero = jnp.zeros((SUBLANE, SCAN_COLS), F32)
                c = lax.fori_loop(0, ntile - 1, tile, (carry[0, :, cs], carry[1, :, cs], zero, zero))
                prev_r = _bcast_row(pr_ref[:, cs], SUBLANE - 1) * keep_prev
                prev_i = _bcast_row(pi_ref[:, cs], SUBLANE - 1) * keep_prev
                c_r, c_i, gar, gai = one_tile(pl.ds(0, SUBLANE), prev_r, prev_i, c)
                carry[0, :, cs] = c_r
                carry[1, :, cs] = c_i
                gar_ref[:, cs] += gar
                gai_ref[:, cs] += gai
            sb = pl.ds(j * STATE_BLOCK, STATE_BLOCK)
            lrb = lr_ref[:, sb].astype(BF16)
            lib = li_ref[:, sb].astype(BF16)
            gcr_ref[j] += _dot(gyb, sr_ref[:, sb].astype(BF16), "tn")
            gci_ref[j] -= _dot(gyb, si_ref[:, sb].astype(BF16), "tn")
            gbr_ref[j] += _dot(ub, lrb, "tn")
            gbi_ref[j] += _dot(ub, lib, "tn")
            du = _dot(lrb, br_ref[j], "nt") + _dot(lib, bi_ref[j], "nt") + gy * d_ref[:, lanes]
            du_ref[:, lanes] = du.astype(BF16)

    rev = lambda i: (nchunk - 1 - i, 0)
    prev = lambda i: (jnp.maximum((nchunk - 1 - i) * tiles_per_chunk - 1, 0), 0)
    full3 = lambda shp: pl.BlockSpec(shp, lambda i: (0, 0, 0))
    acc3 = pl.BlockSpec((nblk, LANE, STATE_BLOCK), lambda i: (0, 0, 0))
    acc_rows = pl.BlockSpec((SUBLANE, ns), lambda i: (0, 0))
    out = _call(
        body, name=name, grid=(nchunk,),
        in_specs=[pl.BlockSpec((tc, d), rev), pl.BlockSpec((tc, d), rev), pl.BlockSpec((tc, d), rev),
                  pl.BlockSpec((tc, ns), rev), pl.BlockSpec((tc, ns), rev),
                  pl.BlockSpec((SUBLANE, ns), prev), pl.BlockSpec((SUBLANE, ns), prev),
                  full3(bc_r.shape), full3(bc_i.shape), full3(cc_r.shape), full3(cc_i.shape), full3(apw_rev.shape),
                  _vec(d)],
        out_specs=[pl.BlockSpec((tc, d), rev), _vec(d), acc3, acc3, acc3, acc3, acc_rows, acc_rows],
        out_shape=[jax.ShapeDtypeStruct((t, d), BF16), jax.ShapeDtypeStruct((1, d), F32)]
        + [jax.ShapeDtypeStruct((nblk, LANE, STATE_BLOCK), F32)] * 4
        + [jax.ShapeDtypeStruct((SUBLANE, ns), F32)] * 2,
        args=(dyg, y_pre, z, sr, si, sr, si, bc_r, bc_i, cc_r, cc_i, apw_rev, dvec),
        scratch_shapes=[pltpu.VMEM((tc, ns), F32), pltpu.VMEM((tc, ns), F32), pltpu.VMEM((2, SUBLANE, ns), F32)],
        tasks=tasks)
    return (tuple(out[0]), out[1]) if tasks else tuple(out)


def _cmul(a, b):
    return a[0] * b[0] - a[1] * b[1], a[0] * b[1] + a[1] * b[0]


def _scan_constants(abar_r, abar_i, reverse):
    ar = abar_r.reshape(1, -1)
    ai = abar_i.reshape(1, -1)
    if reverse:
        ai = -ai
    pw = [(ar, ai)]
    for _ in range(SUBLANE - 1):
        pw.append(_cmul(pw[-1], (ar, ai)))
    rows = lax.broadcasted_iota(jnp.int32, (SUBLANE, 1), 0)
    out = []
    for sh in (1, 2, 4):
        keep = (rows <= SUBLANE - 1 - sh) if reverse else (rows >= sh)
        for part in pw[sh - 1]:
            out.append(jnp.where(keep, part, 0.0))
    for comp in (0, 1):
        stack = jnp.concatenate([pw[k][comp] for k in range(SUBLANE)], axis=0)
        out.append(stack[::-1] if reverse else stack)
    return jnp.stack(out, axis=0).astype(F32)


def _ssm_discretize(log_dt, a_re, a_im, b_re, b_im):
    dt = jnp.exp(log_dt)[:, None]
    lr = jnp.minimum(a_re, -1e-4)
    li = a_im
    mag = jnp.exp(lr * dt)
    ang = li * dt
    abar_r = mag * jnp.cos(ang)
    abar_i = mag * jnp.sin(ang)
    den = lr * lr + li * li
    xr = abar_r - 1.0
    xi = abar_i
    zr = (xr * lr + xi * li) / den
    zi = (xi * lr - xr * li) / den
    bbar_r = zr[..., None] * b_re - zi[..., None] * b_im
    bbar_i = zr[..., None] * b_im + zi[..., None] * b_re
    return abar_r, abar_i, bbar_r, bbar_i


def _block_diag(w):
    g, a, b = w.shape
    nb = g // GROUPS_PER_BLOCK
    eye = jnp.eye(GROUPS_PER_BLOCK, dtype=w.dtype)
    w5 = w.reshape(nb, GROUPS_PER_BLOCK, a, b)
    out = w5[:, :, :, None, :] * eye[None, :, None, :, None]
    return out.reshape(nb, GROUPS_PER_BLOCK * a, GROUPS_PER_BLOCK * b)


def _block_diag_extract(m, a, b):
    nb = m.shape[0]
    eye = jnp.eye(GROUPS_PER_BLOCK, dtype=m.dtype)
    m5 = m.reshape(nb, GROUPS_PER_BLOCK, a, GROUPS_PER_BLOCK, b)
    out = jnp.sum(m5 * eye[None, :, None, :, None], axis=3)
    return out.reshape(nb * GROUPS_PER_BLOCK, a, b)


SSD_L = 16
PAIR = 2


def _pair_diag(w):
    g, a, b = w.shape
    eye = jnp.eye(PAIR, dtype=w.dtype)
    out = w.reshape(g // PAIR, PAIR, a, b)[:, :, :, None, :] * eye[None, :, None, :, None]
    return out.reshape(g // PAIR, PAIR * a, PAIR * b)


def _ssd_matrices(abar_r, abar_i, bbar_r, bbar_i, c_re, c_im):
    g, n = abar_r.shape
    p = bbar_r.shape[2]
    ell = SSD_L
    pw = [(jnp.ones_like(abar_r), jnp.zeros_like(abar_i))]
    for _ in range(ell):
        pw.append(_cmul(pw[-1], (abar_r, abar_i)))
    pr = jnp.stack([q[0] for q in pw])
    pi = jnp.stack([q[1] for q in pw])

    def c_times(kr, ki):
        return (c_re[None] * kr[:, :, None, :] - c_im[None] * ki[:, :, None, :],
                c_re[None] * ki[:, :, None, :] + c_im[None] * kr[:, :, None, :])

    car, cai = c_times(pr[:ell], pi[:ell])
    taps = jnp.einsum("kgpn,gnq->gkpq", car, bbar_r) - jnp.einsum("kgpn,gnq->gkpq", cai, bbar_i)
    lag = jnp.arange(ell)[None, :] - jnp.arange(ell)[:, None]
    onehot = (lag[None] == jnp.arange(ell)[:, None, None]).astype(F32)
    m1 = jnp.einsum("kst,gkpq->gsqtp", onehot, taps).reshape(g, ell * p, ell * p)

    rev_r, rev_i = pr[:ell][::-1], pi[:ell][::-1]
    m2r = rev_r[:, :, :, None] * bbar_r[None] - rev_i[:, :, :, None] * bbar_i[None]
    m2i = rev_r[:, :, :, None] * bbar_i[None] + rev_i[:, :, :, None] * bbar_r[None]
    m2r = jnp.transpose(m2r, (1, 0, 3, 2)).reshape(g, ell * p, n)
    m2i = jnp.transpose(m2i, (1, 0, 3, 2)).reshape(g, ell * p, n)

    car1, cai1 = c_times(pr[1:], pi[1:])
    m3r = jnp.transpose(car1, (1, 3, 0, 2)).reshape(g, n, ell * p)
    m3i = -jnp.transpose(cai1, (1, 3, 0, 2)).reshape(g, n, ell * p)
    m3 = jnp.concatenate([_pair_diag(m3r), _pair_diag(m3i)], axis=1)
    return m1, _pair_diag(m2r), _pair_diag(m2i), m3, pw[ell][0], pw[ell][1]


def _to_pairs(a, ell=SSD_L):
    t, d = a.shape
    nq = d // (PAIR * SSM_GROUP)
    return a.reshape(t // ell, ell, nq, PAIR, SSM_GROUP).transpose(2, 0, 3, 1, 4).reshape(nq, t // ell, -1)


def _from_pairs(a, ell=SSD_L):
    nq, nc, _ = a.shape
    return a.reshape(nq, nc, PAIR, ell, SSM_GROUP).transpose(1, 3, 0, 2, 4).reshape(nc * ell, -1)


def s5c_fwd(u, u_prev, m1, m2r, m2i, m3, apw, name):
    nq, nc, w = u.shape
    half = w // PAIR
    ns = m2r.shape[2]
    ntile = nc // SUBLANE

    def body(u_ref, up_ref, m1_ref, m2r_ref, m2i_ref, m3_ref, apw_ref, y_ref, sr_ref, si_ref):
        up = up_ref[...]
        sr_ref[...] = _dot(up, m2r_ref[...])
        si_ref[...] = _dot(up, m2i_ref[...])
        const = lambda k: apw_ref[k]

        def tile(k, c):
            rows = pl.ds(pl.multiple_of(k * SUBLANE, SUBLANE), SUBLANE)
            xr, xi = _scan_tile(sr_ref[rows, :], si_ref[rows, :], const, c[0], c[1], False)
            sr_ref[rows, :] = xr
            si_ref[rows, :] = xi
            return _bcast_row(xr, SUBLANE - 1), _bcast_row(xi, SUBLANE - 1)

        zero = jnp.zeros((SUBLANE, ns), F32)
        lax.fori_loop(0, ntile, tile, (zero, zero))
        state = jnp.concatenate([sr_ref[...].astype(BF16), si_ref[...].astype(BF16)], axis=1)
        carried = _dot(state, m3_ref[...])
        uu = u_ref[...]
        for h in range(PAIR):
            cols = slice(h * half, (h + 1) * half)
            y_ref[:, cols] = _dot(uu[:, cols], m1_ref[h]) + carried[:, cols]

    per_pair = lambda shp: pl.BlockSpec((None,) + shp, lambda q: (q, 0, 0))
    return pl.pallas_call(
        body, name=name, grid=(nq,),
        in_specs=[per_pair((nc, w)), per_pair((nc, w)), pl.BlockSpec((PAIR, half, half), lambda q: (q, 0, 0)),
                  per_pair(m2r.shape[1:]), per_pair(m2i.shape[1:]), per_pair(m3.shape[1:]),
                  pl.BlockSpec((8, SUBLANE, ns), lambda q: (0, 0, q))],
        out_specs=[per_pair((nc, w)), per_pair((nc, ns)), per_pair((nc, ns))],
        out_shape=[jax.ShapeDtypeStruct((nq, nc, w), F32), jax.ShapeDtypeStruct((nq, nc, ns), F32),
                   jax.ShapeDtypeStruct((nq, nc, ns), F32)],
        compiler_params=_cparams(),
    )(u, u_prev, m1, m2r, m2i, m3, apw)


def s5c_bwd(u, dy, dy_next, sr, si, m1, m2r, m2i, m3, apw_rev, name):
    nq, nc, w = u.shape
    half = w // PAIR
    ns = m2r.shape[2]
    ntile = nc // SUBLANE

    def body(u_ref, dy_ref, dyn_ref, sr_ref, si_ref, m1_ref, m2r_ref, m2i_ref, m3_ref, apw_ref,
             du_ref, dm1_ref, dm2r_ref, dm2i_ref, dm3_ref, dar_ref, dai_ref, lr_ref, li_ref):
        back = _dot(dyn_ref[...], m3_ref[...], "nt")
        lr_ref[...] = back[:, 0:ns]
        li_ref[...] = back[:, ns:2 * ns]
        const = lambda k: apw_ref[k]

        def tile(k, c):
            rows = pl.ds(pl.multiple_of((ntile - 1 - k) * SUBLANE, SUBLANE), SUBLANE)
            xr, xi = _scan_tile(lr_ref[rows, :], li_ref[rows, :], const, c[0], c[1], True)
            lr_ref[rows, :] = xr
            li_ref[rows, :] = xi
            s_r, s_i = sr_ref[rows, :], si_ref[rows, :]
            return (_bcast_row(xr, 0), _bcast_row(xi, 0), c[2] + xr * s_r + xi * s_i, c[3] + xi * s_r - xr * s_i)

        zero = jnp.zeros((SUBLANE, ns), F32)
        _, _, dar, dai = lax.fori_loop(0, ntile, tile, (zero, zero, zero, zero))
        dar_ref[...] = dar
        dai_ref[...] = dai
        lrb, lib = lr_ref[...].astype(BF16), li_ref[...].astype(BF16)
        uu, dyy = u_ref[...], dy_ref[...]
        from_state = _dot(lrb, m2r_ref[...], "nt") + _dot(lib, m2i_ref[...], "nt")
        for h in range(PAIR):
            cols = slice(h * half, (h + 1) * half)
            du_ref[:, cols] = _dot(dyy[:, cols], m1_ref[h], "nt") + from_state[:, cols]
            dm1_ref[h] = _dot(uu[:, cols], dyy[:, cols], "tn")
        dm2r_ref[...] = _dot(uu, lrb, "tn")
        dm2i_ref[...] = _dot(uu, lib, "tn")
        state = jnp.concatenate([sr_ref[...].astype(BF16), si_ref[...].astype(BF16)], axis=1)
        dm3_ref[...] = _dot(state, dyy, "tn")

    per_pair = lambda shp: pl.BlockSpec((None,) + shp, lambda q: (q, 0, 0))
    m1_spec = pl.BlockSpec((PAIR, half, half), lambda q: (q, 0, 0))
    return pl.pallas_call(
        body, name=name, grid=(nq,),
        in_specs=[per_pair((nc, w)), per_pair((nc, w)), per_pair((nc, w)), per_pair((nc, ns)), per_pair((nc, ns)),
                  m1_spec, per_pair(m2r.shape[1:]), per_pair(m2i.shape[1:]), per_pair(m3.shape[1:]),
                  pl.BlockSpec((8, SUBLANE, ns), lambda q: (0, 0, q))],
        out_specs=[per_pair((nc, w)), m1_spec, per_pair(m2r.shape[1:]), per_pair(m2i.shape[1:]),
                   per_pair(m3.shape[1:]), per_pair((SUBLANE, ns)), per_pair((SUBLANE, ns))],
        out_shape=[jax.ShapeDtypeStruct((nq, nc, w), F32), jax.ShapeDtypeStruct(m1.shape, F32),
                   jax.ShapeDtypeStruct(m2r.shape, F32), jax.ShapeDtypeStruct(m2i.shape, F32),
                   jax.ShapeDtypeStruct(m3.shape, F32), jax.ShapeDtypeStruct((nq, SUBLANE, ns), F32),
                   jax.ShapeDtypeStruct((nq, SUBLANE, ns), F32)],
        scratch_shapes=[pltpu.VMEM((nc, ns), F32), pltpu.VMEM((nc, ns), F32)], compiler_params=_cparams(),
    )(u, dy, dy_next, sr, si, m1, m2r, m2i, m3, apw_rev)


PAIRS_PER_BLOCK = LANE // (PAIR * SSM_GROUP)


def _lane_range(shape, start, size):
    lane = lax.broadcasted_iota(jnp.int32, shape, 1)
    return (lane >= start) & (lane < start + size)


def _pairs_from_rows(rows, qq):
    tiles = []
    for g2 in range(PAIR):
        src = (qq * PAIR + g2) * SSM_GROUP
        for half in range(SSD_L // SUBLANE):
            acc = jnp.zeros_like(rows[0])
            for tt in range(SUBLANE):
                dst = tt * SSM_GROUP
                moved = pltpu.roll(rows[half * SUBLANE + tt], (dst - src) % LANE, 1)
                acc = jnp.where(_lane_range(acc.shape, dst, SSM_GROUP), moved, acc)
            tiles.append(acc)
    return jnp.concatenate(tiles, axis=1)


def _rows_from_pairs(pairs, t):
    acc = jnp.zeros((pairs[0].shape[0], LANE), F32)
    src = (t % SUBLANE) * SSM_GROUP
    for qq in range(PAIRS_PER_BLOCK):
        for g2 in range(PAIR):
            k = g2 * (SSD_L // SUBLANE) + t // SUBLANE
            dst = (qq * PAIR + g2) * SSM_GROUP
            moved = pltpu.roll(pairs[qq][:, k * LANE:(k + 1) * LANE], (dst - src) % LANE, 1)
            acc = jnp.where(_lane_range(acc.shape, dst, SSM_GROUP), moved, acc)
    return acc


def _chunk_rows(ref, nc):
    return [ref[pl.ds(t, nc, stride=SSD_L), :] for t in range(SSD_L)]


def s5r_fwd(z, shift, m1, m2r, m2i, m3, apw, name):
    t = z.shape[0]
    nc = t // SSD_L
    nq, _, ns = m2r.shape
    nblk = nq // PAIRS_PER_BLOCK
    d = nblk * LANE
    half = m1.shape[1]
    ntile = nc // SUBLANE

    def body(u_ref, sh_ref, m1_ref, m2r_ref, m2i_ref, m3_ref, apw_ref, y_ref, sr_ref, si_ref, ys_ref):
        rows = _chunk_rows(u_ref, nc)
        for qq in range(PAIRS_PER_BLOCK):
            uu = _pairs_from_rows(rows, qq).astype(BF16)
            up = _dot(sh_ref[...], uu).astype(BF16)
            sr_ref[qq] = _dot(up, m2r_ref[qq])
            si_ref[qq] = _dot(up, m2i_ref[qq])
            const = lambda k, qq=qq: apw_ref[k, :, qq * ns:(qq + 1) * ns]

            def tile(k, c, qq=qq, const=const):
                r8 = pl.ds(pl.multiple_of(k * SUBLANE, SUBLANE), SUBLANE)
                xr, xi = _scan_tile(sr_ref[qq, r8, :], si_ref[qq, r8, :], const, c[0], c[1], False)
                sr_ref[qq, r8, :] = xr
                si_ref[qq, r8, :] = xi
                return _bcast_row(xr, SUBLANE - 1), _bcast_row(xi, SUBLANE - 1)

            zero = jnp.zeros((SUBLANE, ns), F32)
            lax.fori_loop(0, ntile, tile, (zero, zero))
            state = jnp.concatenate([sr_ref[qq].astype(BF16), si_ref[qq].astype(BF16)], axis=1)
            carried = _dot(state, m3_ref[qq])
            for h in range(PAIR):
                cols = slice(h * half, (h + 1) * half)
                ys_ref[qq, :, cols] = _dot(uu[:, cols], m1_ref[qq * PAIR + h]) + carried[:, cols]
        pairs = [ys_ref[qq] for qq in range(PAIRS_PER_BLOCK)]
        for tpos in range(SSD_L):
            y_ref[pl.ds(tpos, nc, stride=SSD_L), :] = _rows_from_pairs(pairs, tpos)

    blk3 = lambda a, n: pl.BlockSpec((n,) + a.shape[1:], lambda j: (j, 0, 0))
    return pl.pallas_call(
        body, name=name, grid=(nblk,),
        in_specs=[pl.BlockSpec((t, LANE), lambda j: (0, j)), pl.BlockSpec((nc, nc), lambda j: (0, 0)),
                  blk3(m1, PAIRS_PER_BLOCK * PAIR), blk3(m2r, PAIRS_PER_BLOCK), blk3(m2i, PAIRS_PER_BLOCK),
                  blk3(m3, PAIRS_PER_BLOCK), pl.BlockSpec((8, SUBLANE, PAIRS_PER_BLOCK * ns), lambda j: (0, 0, j))],
        out_specs=[pl.BlockSpec((t, LANE), lambda j: (0, j)),
                   pl.BlockSpec((PAIRS_PER_BLOCK, nc, ns), lambda j: (j, 0, 0)),
                   pl.BlockSpec((PAIRS_PER_BLOCK, nc, ns), lambda j: (j, 0, 0))],
        out_shape=[jax.ShapeDtypeStruct((t, d), F32), jax.ShapeDtypeStruct((nq, nc, ns), F32),
                   jax.ShapeDtypeStruct((nq, nc, ns), F32)],
        scratch_shapes=[pltpu.VMEM((PAIRS_PER_BLOCK, nc, PAIR * half), F32)], compiler_params=_cparams(),
    )(z, shift, m1, m2r, m2i, m3, apw)


def s5r_bwd(z, gy, shift, sr, si, m1, m2r, m2i, m3, apw_rev, name):
    t = z.shape[0]
    nc = t // SSD_L
    nq, _, ns = m2r.shape
    nblk = nq // PAIRS_PER_BLOCK
    d = nblk * LANE
    half = m1.shape[1]
    ntile = nc // SUBLANE

    def body(u_ref, gy_ref, sh_ref, sr_ref, si_ref, m1_ref, m2r_ref, m2i_ref, m3_ref, apw_ref,
             du_ref, dm1_ref, dm2r_ref, dm2i_ref, dm3_ref, dar_ref, dai_ref, lr_ref, li_ref, dus_ref):
        u_rows = _chunk_rows(u_ref, nc)
        gy_rows = _chunk_rows(gy_ref, nc)
        for qq in range(PAIRS_PER_BLOCK):
            uu = _pairs_from_rows(u_rows, qq).astype(BF16)
            dyy = _pairs_from_rows(gy_rows, qq).astype(BF16)
            dyn = _dot(sh_ref[...], dyy, "tn").astype(BF16)
            back = _dot(dyn, m3_ref[qq], "nt")
            lr_ref[...] = back[:, 0:ns]
            li_ref[...] = back[:, ns:2 * ns]
            const = lambda k, qq=qq: apw_ref[k, :, qq * ns:(qq + 1) * ns]

            def tile(k, c, qq=qq, const=const):
                r8 = pl.ds(pl.multiple_of((ntile - 1 - k) * SUBLANE, SUBLANE), SUBLANE)
                xr, xi = _scan_tile(lr_ref[r8, :], li_ref[r8, :], const, c[0], c[1], True)
                lr_ref[r8, :] = xr
                li_ref[r8, :] = xi
                s_r, s_i = sr_ref[qq, r8, :], si_ref[qq, r8, :]
                return (_bcast_row(xr, 0), _bcast_row(xi, 0), c[2] + xr * s_r + xi * s_i, c[3] + xi * s_r - xr * s_i)

            zero = jnp.zeros((SUBLANE, ns), F32)
            _, _, dar, dai = lax.fori_loop(0, ntile, tile, (zero, zero, zero, zero))
            dar_ref[qq] = dar
            dai_ref[qq] = dai
            lrb, lib = lr_ref[...].astype(BF16), li_ref[...].astype(BF16)
            from_state = _dot(lrb, m2r_ref[qq], "nt") + _dot(lib, m2i_ref[qq], "nt")
            for h in range(PAIR):
                cols = slice(h * half, (h + 1) * half)
                dus_ref[qq, :, cols] = _dot(dyy[:, cols], m1_ref[qq * PAIR + h], "nt") + from_state[:, cols]
                dm1_ref[qq * PAIR + h] = _dot(uu[:, cols], dyy[:, cols], "tn")
            dm2r_ref[qq] = _dot(uu, lrb, "tn")
            dm2i_ref[qq] = _dot(uu, lib, "tn")
            state = jnp.concatenate([sr_ref[qq].astype(BF16), si_ref[qq].astype(BF16)], axis=1)
            dm3_ref[qq] = _dot(state, dyy, "tn")
        pairs = [dus_ref[qq] for qq in range(PAIRS_PER_BLOCK)]
        for tpos in range(SSD_L):
            du_ref[pl.ds(tpos, nc, stride=SSD_L), :] = _rows_from_pairs(pairs, tpos)

    blk3 = lambda a, n: pl.BlockSpec((n,) + a.shape[1:], lambda j: (j, 0, 0))
    cols_j = pl.BlockSpec((t, LANE), lambda j: (0, j))
    states = pl.BlockSpec((PAIRS_PER_BLOCK, nc, ns), lambda j: (j, 0, 0))
    partial = pl.BlockSpec((PAIRS_PER_BLOCK, SUBLANE, ns), lambda j: (j, 0, 0))
    return pl.pallas_call(
        body, name=name, grid=(nblk,),
        in_specs=[cols_j, cols_j, pl.BlockSpec((nc, nc), lambda j: (0, 0)), states, states,
                  blk3(m1, PAIRS_PER_BLOCK * PAIR), blk3(m2r, PAIRS_PER_BLOCK), blk3(m2i, PAIRS_PER_BLOCK),
                  blk3(m3, PAIRS_PER_BLOCK), pl.BlockSpec((8, SUBLANE, PAIRS_PER_BLOCK * ns), lambda j: (0, 0, j))],
        out_specs=[cols_j, blk3(m1, PAIRS_PER_BLOCK * PAIR), blk3(m2r, PAIRS_PER_BLOCK), blk3(m2i, PAIRS_PER_BLOCK),
                   blk3(m3, PAIRS_PER_BLOCK), partial, partial],
        out_shape=[jax.ShapeDtypeStruct((t, d), F32), jax.ShapeDtypeStruct(m1.shape, F32),
                   jax.ShapeDtypeStruct(m2r.shape, F32), jax.ShapeDtypeStruct(m2i.shape, F32),
                   jax.ShapeDtypeStruct(m3.shape, F32), jax.ShapeDtypeStruct((nq, SUBLANE, ns), F32),
                   jax.ShapeDtypeStruct((nq, SUBLANE, ns), F32)],
        scratch_shapes=[pltpu.VMEM((nc, ns), F32), pltpu.VMEM((nc, ns), F32),
                        pltpu.VMEM((PAIRS_PER_BLOCK, nc, PAIR * half), F32)],
        compiler_params=_cparams(),
    )(z, gy, shift, sr, si, m1, m2r, m2i, m3, apw_rev)


def s5_skip_gelu(y_lin, z, dvec, name, tr=512):
    t, d = y_lin.shape
    tr = _row_tile(t, tr)

    def body(y_ref, u_ref, d_ref, yp_ref, yg_ref):
        y = y_ref[...] + d_ref[...] * u_ref[...]
        yp_ref[...] = y
        yg_ref[...] = _gelu(y).astype(BF16)

    sp = _rows(t, d, tr)
    return pl.pallas_call(
        body, name=name, grid=(t // tr,), in_specs=[sp, sp, _vec(d)], out_specs=[sp, sp],
        out_shape=[jax.ShapeDtypeStruct((t, d), F32), jax.ShapeDtypeStruct((t, d), BF16)],
        compiler_params=_cparams(),
    )(y_lin, z, dvec)


def s5_skip_gelu_bwd(dyg, y_pre, z, dvec, name, tr=512):
    t, d = dyg.shape
    tr = _row_tile(t, tr)

    def body(dyg_ref, yp_ref, u_ref, d_ref, gy_ref, skip_ref, gd_ref):
        gy = dyg_ref[...] * _gelu_grad(yp_ref[...])
        gy_ref[...] = gy
        skip_ref[...] = gy * d_ref[...]

        @pl.when(pl.program_id(0) == 0)
        def _():
            gd_ref[...] = jnp.zeros_like(gd_ref)

        gd_ref[...] += jnp.sum(gy * u_ref[...], axis=0, keepdims=True)

    sp = _rows(t, d, tr)
    return pl.pallas_call(
        body, name=name, grid=(t // tr,), in_specs=[sp, sp, sp, _vec(d)], out_specs=[sp, sp, _vec(d)],
        out_shape=[jax.ShapeDtypeStruct((t, d), F32), jax.ShapeDtypeStruct((t, d), F32),
                   jax.ShapeDtypeStruct((1, d), F32)],
        compiler_params=_cparams(),
    )(dyg, y_pre, z, dvec)


def _layer_norm(gv, nv):
    mu = jnp.mean(gv, axis=-1, keepdims=True)
    xc = gv - mu
    r = lax.rsqrt(jnp.mean(xc * xc, axis=-1, keepdims=True) + EPS)
    xhat = xc * r
    return xhat * nv, xhat, r


def gmlp_fwd(z, norm_v, wm, bs, name, tr=256):
    t = z.shape[0]
    nh = wm.shape[0]
    d = nh * GMLP_HEAD
    col0 = (z.shape[1] - 2 * d) // d
    tr = _row_tile(t, tr)

    def body(zu_ref, zv_ref, nv_ref, wm_ref, bs_ref, o_ref):
        v, _, _ = _layer_norm(_gelu(zv_ref[...]), nv_ref[...])
        vb = v.astype(BF16)
        u = _gelu(zu_ref[...])
        for c in range(tr // CHUNK):
            rows = slice(c * CHUNK, (c + 1) * CHUNK)
            for h in range(nh):
                cols = slice(h * GMLP_HEAD, (h + 1) * GMLP_HEAD)
                s = _dot(wm_ref[h], vb[rows, cols]) + bs_ref[h]
                o_ref[rows, cols] = u[rows, cols] * s

    return pl.pallas_call(
        body, name=name, grid=(t // tr,),
        in_specs=[pl.BlockSpec((tr, d), lambda i: (i, col0)), pl.BlockSpec((tr, d), lambda i: (i, col0 + 1)),
                  _vec(d), pl.BlockSpec(wm.shape, lambda i: (0, 0, 0)), pl.BlockSpec(bs.shape, lambda i: (0, 0, 0))],
        out_specs=pl.BlockSpec((tr, d), lambda i: (i, 0)),
        out_shape=jax.ShapeDtypeStruct((t, d), F32), compiler_params=_cparams(),
    )(z, z, norm_v, wm, bs)


def gmlp_bwd(dy, z, norm_v, wm, wmt, bs, name, tr=256):
    t = z.shape[0]
    nh = wm.shape[0]
    d = nh * GMLP_HEAD
    col0 = (z.shape[1] - 2 * d) // d
    tr = _row_tile(t, tr)

    def body(dy_ref, zu_ref, zv_ref, nv_ref, wm_ref, wmt_ref, bs_ref, dzu_ref, dzv_ref, dnv_ref, dwm_ref, dbs_ref,
             dv_ref):
        @pl.when(pl.program_id(0) == 0)
        def _():
            dnv_ref[...] = jnp.zeros_like(dnv_ref)
            dwm_ref[...] = jnp.zeros_like(dwm_ref)
            dbs_ref[...] = jnp.zeros_like(dbs_ref)

        zv = zv_ref[...]
        nv = nv_ref[...]
        v, xhat, r = _layer_norm(_gelu(zv), nv)
        vb = v.astype(BF16)
        zu = zu_ref[...]
        u = _gelu(zu)
        dy_ = dy_ref[...]
        for c in range(tr // CHUNK):
            rows = slice(c * CHUNK, (c + 1) * CHUNK)
            for h in range(nh):
                cols = slice(h * GMLP_HEAD, (h + 1) * GMLP_HEAD)
                vh = vb[rows, cols]
                s = _dot(wm_ref[h], vh) + bs_ref[h]
                dyh = dy_[rows, cols]
                dzu_ref[rows, cols] = (dyh * s * _gelu_grad(zu[rows, cols])).astype(BF16)
                ds = dyh * u[rows, cols]
                dsb = ds.astype(BF16)
                dbs_ref[h] += jnp.sum(ds, axis=1, keepdims=True)
                dwm_ref[h] += _dot(dsb, vh, "nt")
                dv_ref[rows, cols] = _dot(wmt_ref[h], dsb)
        dv = dv_ref[...]
        dnv_ref[...] += jnp.sum(dv * xhat, axis=0, keepdims=True)
        dxh = dv * nv
        dgv = r * (dxh - jnp.mean(dxh, axis=-1, keepdims=True) - xhat * jnp.mean(dxh * xhat, axis=-1, keepdims=True))
        dzv_ref[...] = (dgv * _gelu_grad(zv)).astype(BF16)

    full3 = lambda shp: pl.BlockSpec(shp, lambda i: (0, 0, 0))
    rows_d = pl.BlockSpec((tr, d), lambda i: (i, 0))
    return pl.pallas_call(
        body, name=name, grid=(t // tr,),
        in_specs=[rows_d, pl.BlockSpec((tr, d), lambda i: (i, col0)), pl.BlockSpec((tr, d), lambda i: (i, col0 + 1)),
                  _vec(d), full3(wm.shape), full3(wmt.shape), full3(bs.shape)],
        out_specs=[rows_d, rows_d, _vec(d), full3((nh, CHUNK, CHUNK)), full3((nh, CHUNK, 1))],
        out_shape=[jax.ShapeDtypeStruct((t, d), BF16), jax.ShapeDtypeStruct((t, d), BF16),
                   jax.ShapeDtypeStruct((1, d), F32), jax.ShapeDtypeStruct((nh, CHUNK, CHUNK), F32),
                   jax.ShapeDtypeStruct((nh, CHUNK, 1), F32)],
        scratch_shapes=[pltpu.VMEM((tr, d), F32)], compiler_params=_cparams(),
    )(dy, z, z, norm_v, wm, wmt, bs)


def _block(ref, axis, size, k):
    start = pl.multiple_of(k * size, size)
    if axis == 0:
        return ref.at[pl.ds(start, size), :]
    return ref.at[:, pl.ds(start, size)]


def _place():
    x, y, c = lax.axis_index("x"), lax.axis_index("y"), lax.axis_index("c")
    chips = [(1 - x, y), (x, 1 - y), (1 - x, 1 - y)]
    return x, y, c, chips


def _dev(x, y, c):
    return 4 * x + 2 * y + c


def gather_task(shards, axes):
    n = len(shards)
    sizes = [s.shape[ax] for s, ax in zip(shards, axes)]
    out_shape = [
        jax.ShapeDtypeStruct((s.shape[0] * N_DEV, s.shape[1]) if ax == 0 else (s.shape[0], s.shape[1] * N_DEV), s.dtype)
        for s, ax in zip(shards, axes)
    ]

    def copy(ins, outs, send_sems, recv_sems, t, k, block, to, from_input=False):
        dst = _block(outs[t], axes[t], sizes[t], _dev(*block))
        return pltpu.make_async_remote_copy(
            src_ref=ins[t] if from_input else dst, dst_ref=dst,
            send_sem=send_sems.at[t * 7 + k], recv_sem=recv_sems.at[t * 7 + k],
            device_id=to, device_id_type=MESH_DT)

    def local(ins, outs, local_sems, t, me):
        return pltpu.make_async_copy(ins[t], _block(outs[t], axes[t], sizes[t], _dev(*me)), local_sems.at[t])

    def start(ins, outs, send_sems, recv_sems, local_sems):
        x, y, c, chips = _place()
        me, sibling = (x, y, c), (x, y, 1 - c)
        for t in range(n):
            local(ins, outs, local_sems, t, me).start()
        for t in range(n):
            copy(ins, outs, send_sems, recv_sems, t, 0, me, sibling, True).start()
            for j, chip in enumerate(chips):
                copy(ins, outs, send_sems, recv_sems, t, 1 + j, me, (*chip, c), True).start()

    def late(ins, outs, send_sems, recv_sems, local_sems):
        x, y, c, chips = _place()
        me, sibling = (x, y, c), (x, y, 1 - c)
        for t in range(n):
            for j, chip in enumerate(chips):
                copy(ins, outs, send_sems, recv_sems, t, 1 + j, (*chip, c), me).wait_recv()
                copy(ins, outs, send_sems, recv_sems, t, 4 + j, (*chip, c), sibling).start()

    def finish(ins, outs, send_sems, recv_sems, local_sems):
        x, y, c, chips = _place()
        me, sibling = (x, y, c), (x, y, 1 - c)
        for t in range(n):
            copy(ins, outs, send_sems, recv_sems, t, 0, sibling, me).wait_recv()
            for j, chip in enumerate(chips):
                copy(ins, outs, send_sems, recv_sems, t, 4 + j, (*chip, 1 - c), me).wait_recv()
        for t in range(n):
            copy(ins, outs, send_sems, recv_sems, t, 0, me, sibling, True).wait_send()
            for j, chip in enumerate(chips):
                copy(ins, outs, send_sems, recv_sems, t, 1 + j, me, (*chip, c), True).wait_send()
                copy(ins, outs, send_sems, recv_sems, t, 4 + j, (*chip, c), sibling).wait_send()
            local(ins, outs, local_sems, t, me).wait()

    return CommTask(shards, out_shape, (7 * n, 7 * n, n), start, late, finish)


def _blk3(shape2, axis):
    r, c = shape2
    return (r // N_DEV, c) if axis == 0 else (r, c // N_DEV)


def _no_late(ins, outs, send_sems, recv_sems, local_sems):
    pass


def to_sibling_task(grads, axes):
    n = len(grads)
    blks = [_blk3(g.shape, ax) for g, ax in zip(grads, axes)]
    sizes = [b[ax] for b, ax in zip(blks, axes)]

    def copies(ins, outs, send_sems, recv_sems):
        x, y, c, _ = _place()
        return [pltpu.make_async_remote_copy(
            src_ref=_block(ins[t], axes[t], sizes[t], 2 * i + (1 - c)), dst_ref=outs[t].at[i],
            send_sem=send_sems.at[t * N_CHIP + i], recv_sem=recv_sems.at[t * N_CHIP + i],
            device_id=(x, y, 1 - c), device_id_type=MESH_DT) for t in range(n) for i in range(N_CHIP)]

    def start(ins, outs, send_sems, recv_sems, local_sems):
        for cp in copies(ins, outs, send_sems, recv_sems):
            cp.start()

    def finish(ins, outs, send_sems, recv_sems, local_sems):
        cps = copies(ins, outs, send_sems, recv_sems)
        for cp in cps:
            cp.wait_recv()
        for cp in cps:
            cp.wait_send()

    out_shape = [jax.ShapeDtypeStruct((N_CHIP,) + b, g.dtype) for b, g in zip(blks, grads)]
    return CommTask(grads, out_shape, (N_CHIP * n, N_CHIP * n, 1), start, _no_late, finish)


def across_chips_task(parts):
    n = len(parts)

    def copies(ins, outs, send_sems, recv_sems):
        x, y, c, chips = _place()
        my_chip = 2 * x + y
        return [pltpu.make_async_remote_copy(
            src_ref=ins[t].at[2 * chip[0] + chip[1]], dst_ref=outs[t].at[my_chip],
            send_sem=send_sems.at[t * 3 + j], recv_sem=recv_sems.at[t * 3 + j],
            device_id=(*chip, c), device_id_type=MESH_DT) for t in range(n) for j, chip in enumerate(chips)]

    def mine(ins, outs, local_sems):
        x, y, _, _ = _place()
        my_chip = 2 * x + y
        return [pltpu.make_async_copy(ins[t].at[my_chip], outs[t].at[my_chip], local_sems.at[t]) for t in range(n)]

    def start(ins, outs, send_sems, recv_sems, local_sems):
        for cp in mine(ins, outs, local_sems):
            cp.start()
        for cp in copies(ins, outs, send_sems, recv_sems):
            cp.start()

    def finish(ins, outs, send_sems, recv_sems, local_sems):
        cps = copies(ins, outs, send_sems, recv_sems)
        for cp in cps:
            cp.wait_recv()
        for cp in cps:
            cp.wait_send()
        for cp in mine(ins, outs, local_sems):
            cp.wait()

    out_shape = [jax.ShapeDtypeStruct(p.shape, p.dtype) for p in parts]
    return CommTask(parts, out_shape, (3 * n, 3 * n, n), start, _no_late, finish)


def run_tasks(tasks, name):
    t_in = [len(t.inputs) for t in tasks]
    t_out = [len(t.out_shape) for t in tasks]

    def body(*refs):
        pos, views = 0, []
        for k in t_in:
            views.append([refs[pos:pos + k]])
            pos += k
        for v, k in zip(views, t_out):
            v.append(refs[pos:pos + k])
            pos += k
        for i, v in enumerate(views):
            v.extend(refs[pos + 3 * i:pos + 3 * i + 3])
        for phase in ("start", "late", "finish"):
            for t, v in zip(tasks, views):
                getattr(t, phase)(*v)

    any_spec = pl.BlockSpec(memory_space=pl.ANY)
    res = pl.pallas_call(
        body, name=name, in_specs=[any_spec] * sum(t_in), out_specs=[any_spec] * sum(t_out),
        out_shape=[s for t in tasks for s in t.out_shape], input_output_aliases=_task_aliases(tasks, 0, 0),
        scratch_shapes=[pltpu.SemaphoreType.DMA((k,)) for t in tasks for k in t.n_sems],
    )(*[a for t in tasks for a in t.inputs])
    res, out, pos = list(res), [], 0
    for k in t_out:
        out.append(res[pos:pos + k])
        pos += k
    return out


_HBM_SPEC = pl.BlockSpec(memory_space=pl.ANY)
_SEM_SPEC = pl.BlockSpec(memory_space=pltpu.SEMAPHORE)
_DATAFLOW = pltpu.SideEffectType.DATAFLOW_SIDE_EFFECTING


def _full_shape(s, ax):
    return (s.shape[0] * N_DEV, s.shape[1]) if ax == 0 else (s.shape[0], s.shape[1] * N_DEV)


def _level1_copy(src, landing, axis, size, send_sems, recv_sems, slot, sender, to):
    dst = _block(landing, axis, size, _dev(*sender))
    return pltpu.make_async_remote_copy(src_ref=src, dst_ref=dst, send_sem=send_sems.at[slot],
                                        recv_sem=recv_sems.at[slot], device_id=to, device_id_type=MESH_DT)


def place_own_block(shard, axis, me, name, tr=256):
    r, c = shard.shape
    tr = _row_tile(r, tr)
    nrb = r // tr
    if axis == 0:
        o_map = lambda i, me_ref: (me_ref[0] * nrb + i, 0)
    else:
        o_map = lambda i, me_ref: (i, me_ref[0])

    def body(me_ref, x_ref, o_ref):
        o_ref[...] = x_ref[...]

    return pl.pallas_call(
        body, name=name,
        grid_spec=pltpu.PrefetchScalarGridSpec(
            num_scalar_prefetch=1, grid=(nrb,), in_specs=[pl.BlockSpec((tr, c), lambda i, me_ref: (i, 0))],
            out_specs=pl.BlockSpec((tr, c), o_map)),
        out_shape=jax.ShapeDtypeStruct(_full_shape(shard, axis), shard.dtype), compiler_params=_cparams(),
    )(me, shard)


def gather_start(landing, axes, sizes, groups, name):
    n = len(landing)

    def body(*refs):
        lands, sems = refs[:n], refs[2 * n:]
        x, y, c, chips = _place()
        me = (x, y, c)
        targets = [(x, y, 1 - c)] + [(*chip, c) for chip in chips]
        for g, members in enumerate(groups):
            for m, t in enumerate(members):
                own = _block(lands[t], axes[t], sizes[t], _dev(*me))
                for k, to in enumerate(targets):
                    _level1_copy(own, lands[t], axes[t], sizes[t], sems[2 * g], sems[2 * g + 1], 4 * m + k,
                                 me, to).start()

    out = pl.pallas_call(
        body, name=name,
        out_shape=[jax.ShapeDtypeStruct(b.shape, b.dtype) for b in landing]
        + [pltpu.SemaphoreType.DMA((4 * len(members),)) for members in groups for _ in (0, 1)],
        in_specs=[_HBM_SPEC] * n, out_specs=[_HBM_SPEC] * n + [_SEM_SPEC] * (2 * len(groups)),
        input_output_aliases={i: i for i in range(n)},
        compiler_params=pltpu.CompilerParams(has_side_effects=_DATAFLOW),
    )(*landing)
    out = list(out)
    sems = out[n:]
    return out[:n], [(sems[2 * g], sems[2 * g + 1]) for g in range(len(groups))]


def gather_wait(landing, axes, sizes, send_sems, recv_sems, after, name):
    n = len(landing)

    def body(*refs):
        lands = refs[:n]
        send, recv = refs[n], refs[n + 1]
        x, y, c, chips = _place()
        me = (x, y, c)
        peers = [(x, y, 1 - c)] + [(*chip, c) for chip in chips]
        for t in range(n):
            own = _block(lands[t], axes[t], sizes[t], _dev(*me))
            for k, peer in enumerate(peers):
                _level1_copy(own, lands[t], axes[t], sizes[t], send, recv, 4 * t + k, me, peer).wait_send()
                _level1_copy(own, lands[t], axes[t], sizes[t], send, recv, 4 * t + k, peer, me).wait_recv()

    out = pl.pallas_call(
        body, name=name, out_shape=[jax.ShapeDtypeStruct(b.shape, b.dtype) for b in landing],
        in_specs=[_HBM_SPEC] * n + [_SEM_SPEC, _SEM_SPEC, pl.BlockSpec(memory_space=pl.ANY)],
        out_specs=[_HBM_SPEC] * n, input_output_aliases={i: i for i in range(n)},
        compiler_params=pltpu.CompilerParams(has_side_effects=_DATAFLOW),
    )(*landing, send_sems, recv_sems, after)
    return list(out)


def forward_task(landing, axes, sizes):
    n = len(landing)

    def forward(lands, send_sems, recv_sems, t, j, chip_core):
        x, y, c, _ = _place()
        blk = _block(lands[t], axes[t], sizes[t], _dev(*chip_core))
        return pltpu.make_async_remote_copy(src_ref=blk, dst_ref=blk, send_sem=send_sems.at[3 * t + j],
                                            recv_sem=recv_sems.at[3 * t + j], device_id=(x, y, 1 - c),
                                            device_id_type=MESH_DT)

    def start(ins, lands, send_sems, recv_sems, local_sems):
        _, _, c, chips = _place()
        for t in range(n):
            for j, chip in enumerate(chips):
                forward(lands, send_sems, recv_sems, t, j, (*chip, c)).start()

    def finish(ins, lands, send_sems, recv_sems, local_sems):
        _, _, c, chips = _place()
        for t in range(n):
            for j, chip in enumerate(chips):
                forward(lands, send_sems, recv_sems, t, j, (*chip, 1 - c)).wait_recv()
        for t in range(n):
            for j, chip in enumerate(chips):
                forward(lands, send_sems, recv_sems, t, j, (*chip, c)).wait_send()

    out_shape = [jax.ShapeDtypeStruct(b.shape, b.dtype) for b in landing]
    return CommTask(landing, out_shape, (3 * n, 3 * n, 1), start, _no_late, finish, in_place=True)


def rs_chip_sum(grad, recv, axis, core, name, tr=512):
    br, bc = _blk3(grad.shape, axis)
    tr = _row_tile(br, tr)
    nrb = br // tr

    if axis == 0:
        g_map = lambda i, r, c_ref: ((2 * i + c_ref[0]) * nrb + r, 0)
    else:
        g_map = lambda i, r, c_ref: (r, 2 * i + c_ref[0])

    def body(c_ref, g_ref, r_ref, o_ref):
        o_ref[...] = (g_ref[...].astype(F32) + r_ref[...].astype(F32)).astype(BF16)

    return pl.pallas_call(
        body, name=name,
        grid_spec=pltpu.PrefetchScalarGridSpec(
            num_scalar_prefetch=1, grid=(N_CHIP, nrb),
            in_specs=[pl.BlockSpec((tr, bc), g_map), pl.BlockSpec((None, tr, bc), lambda i, r, c_ref: (i, r, 0))],
            out_specs=pl.BlockSpec((None, tr, bc), lambda i, r, c_ref: (i, r, 0))),
        out_shape=jax.ShapeDtypeStruct((N_CHIP, br, bc), BF16), compiler_params=_cparams(),
    )(core, grad, recv)


def _adamw(w, g, m, v):
    m = ADAM_B1 * m + (1.0 - ADAM_B1) * g
    v = ADAM_B2 * v + (1.0 - ADAM_B2) * (g * g)
    m_hat = m / (1.0 - ADAM_B1 ** ADAM_STEP)
    v_hat = v / (1.0 - ADAM_B2 ** ADAM_STEP)
    delta = -ADAM_LR * (m_hat / (jnp.sqrt(v_hat) + ADAM_EPS) + ADAM_WD * w)
    return delta, m, v


def _sum_chips(p_ref):
    g = p_ref[0].astype(F32)
    for i in range(1, N_CHIP):
        g = g + p_ref[i].astype(F32)
    return g


def _chip_copy(part, landing, send_sems, recv_sems, slot, to_chip, from_chip, to):
    return pltpu.make_async_remote_copy(src_ref=part.at[to_chip], dst_ref=landing.at[from_chip],
                                        send_sem=send_sems.at[slot], recv_sem=recv_sems.at[slot],
                                        device_id=to, device_id_type=MESH_DT)


def scatter_start(parts, name):
    n = len(parts)

    def body(*refs):
        srcs, lands = refs[:n], refs[n:2 * n]
        send_sems, recv_sems = refs[4 * n], refs[4 * n + 1]
        x, y, c, chips = _place()
        my_chip = 2 * x + y
        for t in range(n):
            for j, chip in enumerate(chips):
                _chip_copy(srcs[t], lands[t], send_sems, recv_sems, 3 * t + j, 2 * chip[0] + chip[1], my_chip,
                           (*chip, c)).start()

    landing = [lax.empty(p.shape, p.dtype) for p in parts]
    out = pl.pallas_call(
        body, name=name,
        out_shape=[jax.ShapeDtypeStruct(p.shape, p.dtype) for p in parts + landing]
        + [pltpu.SemaphoreType.DMA((3 * n,)), pltpu.SemaphoreType.DMA((3 * n,))],
        in_specs=[_HBM_SPEC] * (2 * n), out_specs=[_HBM_SPEC] * (2 * n) + [_SEM_SPEC, _SEM_SPEC],
        input_output_aliases={i: i for i in range(2 * n)},
        compiler_params=pltpu.CompilerParams(has_side_effects=_DATAFLOW),
    )(*parts, *landing)
    out = list(out)
    return out[:n], out[n:2 * n], out[2 * n], out[2 * n + 1]


def scatter_wait(started, after, name):
    sizes = [len(s[0]) for s in started]
    n_all = sum(sizes)

    def body(*refs):
        x, y, c, chips = _place()
        my_chip = 2 * x + y
        pos, sem_pos = 0, 2 * n_all
        for n in sizes:
            srcs, lands = refs[pos:pos + n], refs[n_all + pos:n_all + pos + n]
            send_sems, recv_sems = refs[sem_pos], refs[sem_pos + 1]
            for t in range(n):
                for j, chip in enumerate(chips):
                    other = 2 * chip[0] + chip[1]
                    _chip_copy(srcs[t], lands[t], send_sems, recv_sems, 3 * t + j, other, my_chip, (*chip, c)).wait_send()
                    _chip_copy(srcs[t], lands[t], send_sems, recv_sems, 3 * t + j, my_chip, other, (*chip, c)).wait_recv()
            pos += n
            sem_pos += 2

    parts = [p for s in started for p in s[0]]
    landing = [b for s in started for b in s[1]]
    sems = [q for s in started for q in (s[2], s[3])]
    out = pl.pallas_call(
        body, name=name, out_shape=[jax.ShapeDtypeStruct(a.shape, a.dtype) for a in parts + landing],
        in_specs=[_HBM_SPEC] * (2 * n_all) + [_SEM_SPEC] * len(sems) + [pl.BlockSpec(memory_space=pl.ANY)],
        out_specs=[_HBM_SPEC] * (2 * n_all), input_output_aliases={i: i for i in range(2 * n_all)},
        compiler_params=pltpu.CompilerParams(has_side_effects=_DATAFLOW),
    )(*parts, *landing, *sems, after)
    out = list(out)
    res, pos = [], 0
    for n in sizes:
        res.append((out[pos:pos + n], out[n_all + pos:n_all + pos + n]))
        pos += n
    return res


def adam_sharded(part, landed, my_chip, w, m, v, name, tr=256):
    r, c = w.shape
    assert part.shape[2] == c
    tr = _row_tile(r, tr)

    def body(chip_ref, own_ref, p_ref, w_ref, m_ref, v_ref, g_ref, d_ref, nm_ref, nv_ref):
        own = own_ref[...].astype(F32)
        g = None
        for i in range(N_CHIP):
            term = jnp.where(chip_ref[0] == i, own, p_ref[i].astype(F32))
            g = term if g is None else g + term
        delta, nm, nv = _adamw(w_ref[...], g, m_ref[...], v_ref[...])
        g_ref[...] = g
        d_ref[...] = delta
        nm_ref[...] = nm
        nv_ref[...] = nv

    sp = pl.BlockSpec((tr, c), lambda i, chip_ref: (i, 0))
    return pl.pallas_call(
        body, name=name,
        grid_spec=pltpu.PrefetchScalarGridSpec(
            num_scalar_prefetch=1, grid=(r // tr,),
            in_specs=[pl.BlockSpec((None, tr, c), lambda i, chip_ref: (chip_ref[0], i, 0)),
                      pl.BlockSpec((N_CHIP, tr, c), lambda i, chip_ref: (0, i, 0)), sp, sp, sp],
            out_specs=[sp, sp, sp, sp]),
        out_shape=[jax.ShapeDtypeStruct((r, c), F32)] * 4, compiler_params=_cparams(),
    )(my_chip, part, landed, w, m, v)


def adam_small(g, w, m, v, name):
    def body(g_ref, w_ref, m_ref, v_ref, d_ref, nm_ref, nv_ref):
        delta, nm, nv = _adamw(w_ref[...], g_ref[...], m_ref[...], v_ref[...])
        d_ref[...] = delta
        nm_ref[...] = nm
        nv_ref[...] = nv

    return pl.pallas_call(
        body, name=name, out_shape=[jax.ShapeDtypeStruct(w.shape, F32)] * 3, compiler_params=_cparams(),
    )(g, w, m, v)


def sum_devices(gathered, name, tr=512):
    _, r, c = gathered.shape
    tr = _row_tile(r, tr)

    def body(x_ref, o_ref):
        s = x_ref[0]
        for k in range(1, N_DEV):
            s = s + x_ref[k]
        o_ref[...] = s

    return pl.pallas_call(
        body, name=name, grid=(r // tr,), in_specs=[pl.BlockSpec((N_DEV, tr, c), lambda i: (0, i, 0))],
        out_specs=pl.BlockSpec((tr, c), lambda i: (i, 0)), out_shape=jax.ShapeDtypeStruct((r, c), F32),
        compiler_params=_cparams(),
    )(gathered)


def _pad_to(a, axis, mult):
    size = a.shape[axis]
    pad = (-size) % mult
    if pad == 0:
        return a
    cfg = [(0, 0)] * a.ndim
    cfg[axis] = (0, pad)
    return jnp.pad(a, cfg)


def _as2d(a):
    if a.ndim == 1:
        return a.reshape(1, -1)
    return a.reshape(-1, a.shape[-1])


def kernel(x, p, norm_ffn1, w1_gate, w1_up, w1_down, norm_mix, w_in, ssm_log_dt, ssm_a_re, ssm_a_im, ssm_b_re, ssm_b_im, ssm_c_re, ssm_c_im, ssm_d, ssm_w_glu, gmlp_norm_v, gmlp_w_s, gmlp_b_s, norm_ssm_out, norm_gmlp_out, w_out, norm_ffn2, w2_gate, w2_up, w2_down, norm_ple, w_ple_gate, w_ple_proj, norm_final, loss_target, m_norm_ffn1, m_w1_gate, m_w1_up, m_w1_down, m_norm_mix, m_w_in, m_ssm_log_dt, m_ssm_a_re, m_ssm_a_im, m_ssm_b_re, m_ssm_b_im, m_ssm_c_re, m_ssm_c_im, m_ssm_d, m_ssm_w_glu, m_gmlp_norm_v, m_gmlp_w_s, m_gmlp_b_s, m_norm_ssm_out, m_norm_gmlp_out, m_w_out, m_norm_ffn2, m_w2_gate, m_w2_up, m_w2_down, m_norm_ple, m_w_ple_gate, m_w_ple_proj, m_norm_final, v_norm_ffn1, v_w1_gate, v_w1_up, v_w1_down, v_norm_mix, v_w_in, v_ssm_log_dt, v_ssm_a_re, v_ssm_a_im, v_ssm_b_re, v_ssm_b_im, v_ssm_c_re, v_ssm_c_im, v_ssm_d, v_ssm_w_glu, v_gmlp_norm_v, v_gmlp_w_s, v_gmlp_b_s, v_norm_ssm_out, v_norm_gmlp_out, v_w_out, v_norm_ffn2, v_w2_gate, v_w2_up, v_w2_down, v_norm_ple, v_w_ple_gate, v_w_ple_proj, v_norm_final):
    weights = dict(
        norm_ffn1=norm_ffn1, w1_gate=w1_gate, w1_up=w1_up, w1_down=w1_down, norm_mix=norm_mix, w_in=w_in,
        ssm_log_dt=ssm_log_dt, ssm_a_re=ssm_a_re, ssm_a_im=ssm_a_im, ssm_b_re=ssm_b_re, ssm_b_im=ssm_b_im,
        ssm_c_re=ssm_c_re, ssm_c_im=ssm_c_im, ssm_d=ssm_d, ssm_w_glu=ssm_w_glu, gmlp_norm_v=gmlp_norm_v,
        gmlp_w_s=gmlp_w_s, gmlp_b_s=gmlp_b_s, norm_ssm_out=norm_ssm_out, norm_gmlp_out=norm_gmlp_out, w_out=w_out,
        norm_ffn2=norm_ffn2, w2_gate=w2_gate, w2_up=w2_up, w2_down=w2_down, norm_ple=norm_ple,
        w_ple_gate=w_ple_gate, w_ple_proj=w_ple_proj, norm_final=norm_final)
    moments_m = dict(
        norm_ffn1=m_norm_ffn1, w1_gate=m_w1_gate, w1_up=m_w1_up, w1_down=m_w1_down, norm_mix=m_norm_mix, w_in=m_w_in,
        ssm_log_dt=m_ssm_log_dt, ssm_a_re=m_ssm_a_re, ssm_a_im=m_ssm_a_im, ssm_b_re=m_ssm_b_re, ssm_b_im=m_ssm_b_im,
        ssm_c_re=m_ssm_c_re, ssm_c_im=m_ssm_c_im, ssm_d=m_ssm_d, ssm_w_glu=m_ssm_w_glu, gmlp_norm_v=m_gmlp_norm_v,
        gmlp_w_s=m_gmlp_w_s, gmlp_b_s=m_gmlp_b_s, norm_ssm_out=m_norm_ssm_out, norm_gmlp_out=m_norm_gmlp_out,
        w_out=m_w_out, norm_ffn2=m_norm_ffn2, w2_gate=m_w2_gate, w2_up=m_w2_up, w2_down=m_w2_down,
        norm_ple=m_norm_ple, w_ple_gate=m_w_ple_gate, w_ple_proj=m_w_ple_proj, norm_final=m_norm_final)
    moments_v = dict(
        norm_ffn1=v_norm_ffn1, w1_gate=v_w1_gate, w1_up=v_w1_up, w1_down=v_w1_down, norm_mix=v_norm_mix, w_in=v_w_in,
        ssm_log_dt=v_ssm_log_dt, ssm_a_re=v_ssm_a_re, ssm_a_im=v_ssm_a_im, ssm_b_re=v_ssm_b_re, ssm_b_im=v_ssm_b_im,
        ssm_c_re=v_ssm_c_re, ssm_c_im=v_ssm_c_im, ssm_d=v_ssm_d, ssm_w_glu=v_ssm_w_glu, gmlp_norm_v=v_gmlp_norm_v,
        gmlp_w_s=v_gmlp_w_s, gmlp_b_s=v_gmlp_b_s, norm_ssm_out=v_norm_ssm_out, norm_gmlp_out=v_norm_gmlp_out,
        w_out=v_w_out, norm_ffn2=v_norm_ffn2, w2_gate=v_w2_gate, w2_up=v_w2_up, w2_down=v_w2_down,
        norm_ple=v_norm_ple, w_ple_gate=v_w_ple_gate, w_ple_proj=v_w_ple_proj, norm_final=v_norm_final)
    names = list(weights)

    xs = x[0]
    ps = p[0, 0].astype(BF16)
    tgt = loss_target[0]
    d_model = xs.shape[1]
    d_ssm = d_model // 2
    n_groups = d_ssm // SSM_GROUP

    transposed = ("w1_gate", "w1_up", "w2_gate", "w2_up")
    big = {
        "w1_gate": 0, "w1_up": 0, "w1_down": 0, "w_in": 1, "ssm_w_glu": 0, "w_out": 0,
        "w2_gate": 0, "w2_up": 0, "w2_down": 0, "w_ple_gate": 0, "w_ple_proj": 1}
    big_names = list(big)

    def view(a, k):
        return a[0].T if k in transposed else a[0]

    def unview(a, k):
        return a.T[None] if k in transposed else a[None]

    shard = {k: _pad_to(view(weights[k], k).astype(BF16), big[k], LANE) for k in big_names}
    W = {}

    abar_r, abar_i, bbar_r, bbar_i = _ssm_discretize(ssm_log_dt[0], ssm_a_re[0], ssm_a_im[0], ssm_b_re[0], ssm_b_im[0])
    bc_r = _block_diag(jnp.swapaxes(bbar_r, 1, 2)).astype(BF16)
    bc_i = _block_diag(jnp.swapaxes(bbar_i, 1, 2)).astype(BF16)
    cc_r = _block_diag(jnp.swapaxes(ssm_c_re[0], 1, 2)).astype(BF16)
    cc_i = _block_diag(jnp.swapaxes(ssm_c_im[0], 1, 2)).astype(BF16)
    apw_f = _scan_constants(abar_r, abar_i, False)
    apw_b = _scan_constants(abar_r, abar_i, True)
    causal = jnp.tril(jnp.ones((CHUNK, CHUNK), dtype=bool))
    wm = jnp.where(causal[None], gmlp_w_s[0], 0.0).astype(BF16)
    wmt = jnp.swapaxes(wm, 1, 2)
    bs = gmlp_b_s[0][:, :, None]

    groups = [["w1_gate"], ["w1_up"], ["w1_down"], ["w_in", "ssm_w_glu", "w_out"], ["w2_gate"], ["w2_up"],
              ["w2_down", "w_ple_gate", "w_ple_proj"]]
    order = [k for g in groups for k in g]
    place = {k: i for i, k in enumerate(order)}
    me = (4 * lax.axis_index("x") + 2 * lax.axis_index("y") + lax.axis_index("c")).astype(jnp.int32).reshape(1)
    size = {k: shard[k].shape[big[k]] for k in order}
    landing, sems = gather_start([place_own_block(shard[k], big[k], me, "place_" + k) for k in order],
                                 [big[k] for k in order], [size[k] for k in order],
                                 [[place[k] for k in g] for g in groups], "gather_start")

    def landed(g, after):
        axes_g, sizes_g = [big[k] for k in groups[g]], [size[k] for k in groups[g]]
        bufs = gather_wait([landing[place[k]] for k in groups[g]], axes_g, sizes_g, *sems[g], after,
                           "gather_wait_%d" % g)
        return forward_task(bufs, axes_g, sizes_g)

    def arrive(g, after):
        W.update(zip(groups[g], run_tasks([landed(g, after)], "gather_forward_%d" % g)[0]))

    def arrive_during(g, after, fn, *a, **kw):
        out, (got,) = fn(*a, tasks=[landed(g, after)], **kw)
        W.update(zip(groups[g], got))
        return out

    xn1 = rmsnorm_fwd(xs, norm_ffn1, "norm_ffn1")
    arrive(0, xn1)
    gate1 = matmul(xn1, W["w1_gate"], "nt", "ffn1_gate")
    arrive(1, gate1)
    gate1, up1, act1 = ffn_up(xn1, W["w1_up"], gate1, "ffn1_up")
    arrive(2, act1)
    h1 = matmul(act1, W["w1_down"], "nn", "ffn1_down", res=xs, scale=0.5)
    arrive(3, h1)
    xn2 = rmsnorm_fwd(h1, norm_mix, "norm_mix")
    z = matmul(xn2, W["w_in"], "nn", "proj_in")
    y_pre, yg, sr, si = s5_fwd(z, bc_r, bc_i, cc_r, cc_i, apw_f, ssm_d, "s5_fwd")
    glin = matmul(yg, W["ssm_w_glu"], "nn", "ssm_glu")
    y_gmlp = gmlp_fwd(z, gmlp_norm_v, wm, bs, "gmlp_fwd")
    ycat = mix_out_fwd(y_pre, glin, y_gmlp, norm_ssm_out, norm_gmlp_out, "mix_out")
    h2 = arrive_during(4, ycat, matmul, ycat, W["w_out"], "nn", "proj_out", res=h1)
    xn3 = rmsnorm_fwd(h2, norm_ffn2, "norm_ffn2")
    gate2 = arrive_during(5, xn3, matmul, xn3, W["w2_gate"], "nt", "ffn2_gate")
    gate2, up2, act2 = arrive_during(6, gate2, ffn_up, xn3, W["w2_up"], gate2, "ffn2_up")
    h3 = matmul(act2, W["w2_down"], "nn", "ffn2_down", res=h2, scale=0.5)
    xn4 = rmsnorm_fwd(h3, norm_ple, "norm_ple")
    pg_lin = matmul(xn4, W["w_ple_gate"], "nn", "ple_gate")
    pp = matmul(ps, W["w_ple_proj"], "nn", "ple_proj")
    h4 = ple_fwd(h3, pg_lin, pp, "ple_fwd")
    dh4, loss_part, g_norm_final = final_loss(h4, tgt, norm_final.reshape(1, -1), "final_loss")
    loss = lax.psum(loss_part[0, 0], ("x", "y", "c"))

    G = {}
    wait_sibling, crossing = [], []
    core = lax.axis_index("c").astype(jnp.int32).reshape(1)
    my_chip = (2 * lax.axis_index("x") + lax.axis_index("y")).astype(jnp.int32).reshape(1)

    def grad(name_, value):
        G[name_] = value
        wait_sibling.append(name_)

    def carry(fn, *a, extra=None, **kw):
        tasks, kinds = [], []
        if extra is not None:
            tasks.append(extra[0])
            kinds.append(("x", extra[1]))
        if wait_sibling:
            group = list(wait_sibling)
            wait_sibling.clear()
            tasks.append(to_sibling_task([G[k] for k in group], [big[k] for k in group]))
            kinds.append(("a", group))
        if not tasks:
            return fn(*a, **kw)
        out, task_outs = fn(*a, tasks=tasks, **kw)
        for (kind, group), outs in zip(kinds, task_outs):
            if kind == "x":
                group(outs)
                continue
            parts = [rs_chip_sum(G[k], r, big[k], core, "rs_sum_" + k) for k, r in zip(group, outs)]
            crossing.append((group, scatter_start(parts, "scatter_start_" + group[0])))
        return out

    small = {}
    small["norm_final"] = g_norm_final
    dpp, dpg = ple_bwd(dh4, pg_lin, pp, "ple_bwd")
    grad("w_ple_proj", matmul(ps, dpp, "tn", "grad_ple_proj", out_dtype=BF16))
    grad("w_ple_gate", carry(matmul, xn4, dpg, "tn", "grad_ple_gate", out_dtype=BF16))
    dxn4 = carry(matmul, dpg, W["w_ple_gate"], "nt", "ple_gate_bwd")
    dh3, dh3b, small["norm_ple"] = rmsnorm_bwd(dxn4, h3, norm_ple, dh4, "norm_ple_bwd")

    def ffn_bwd(tag, dhb, xn, gate, up, act, wg, wu, wd, extra=None):
        dgate, dup = carry(ffn_bwd_act, dhb, W[wd], gate, up, tag + "_act_bwd", extra=extra)
        grad(wd, carry(matmul, act, dhb, "tn", tag + "_grad_down", out_dtype=BF16, scale=0.5))
        grad(wg, carry(matmul, dgate, xn, "tn", tag + "_grad_gate", out_dtype=BF16))
        grad(wu, carry(matmul, dup, xn, "tn", tag + "_grad_up", out_dtype=BF16))
        dxn = carry(matmul, dgate, W[wg], "nn", tag + "_gate_bwd")
        return carry(matmul, dup, W[wu], "nn", tag + "_up_bwd", res=dxn)

    dxn3 = ffn_bwd("ffn2", dh3b, xn3, gate2, up2, act2, "w2_gate", "w2_up", "w2_down")
    dh2, dh2b, small["norm_ffn2"] = rmsnorm_bwd(dxn3, h2, norm_ffn2, dh3, "norm_ffn2_bwd")

    grad("w_out", matmul(ycat, dh2b, "tn", "grad_out", out_dtype=BF16))
    dycat = carry(matmul, dh2b, W["w_out"], "nt", "proj_out_bwd")
    dyg_direct, dglin, dy_gmlp, small["norm_ssm_out"], small["norm_gmlp_out"] = mix_out_bwd(
        dycat, y_pre, glin, y_gmlp, norm_ssm_out, norm_gmlp_out, "mix_out_bwd")
    grad("ssm_w_glu", matmul(yg, dglin, "tn", "grad_glu", out_dtype=BF16))
    dyg = carry(matmul, dglin, W["ssm_w_glu"], "nt", "ssm_glu_bwd", res=dyg_direct)
    du, small["ssm_d"], gc_r, gc_i, gb_r, gb_i, ga_r, ga_i = carry(
        s5_bwd, dyg, y_pre, z, sr, si, bc_r, bc_i, cc_r, cc_i, apw_b, ssm_d, "s5_bwd")
    dzu, dzv, small["gmlp_norm_v"], g_wm, g_bs = gmlp_bwd(dy_gmlp, z, gmlp_norm_v, wm, wmt, bs, "gmlp_bwd")
    small["gmlp_w_s"] = g_wm
    small["gmlp_b_s"] = g_bs
    small["c_re"] = _block_diag_extract(gc_r, SSM_GROUP, SSM_STATE)
    small["c_im"] = _block_diag_extract(gc_i, SSM_GROUP, SSM_STATE)
    small["bbar_r"] = jnp.swapaxes(_block_diag_extract(gb_r, SSM_GROUP, SSM_STATE), 1, 2)
    small["bbar_i"] = jnp.swapaxes(_block_diag_extract(gb_i, SSM_GROUP, SSM_STATE), 1, 2)
    small["abar_r"] = jnp.sum(ga_r, axis=0).reshape(n_groups, SSM_STATE)
    small["abar_i"] = jnp.sum(ga_i, axis=0).reshape(n_groups, SSM_STATE)

    dz = jnp.concatenate([du, dzu, dzv], axis=1)
    grad("w_in", matmul(xn2, dz, "tn", "grad_in", out_dtype=BF16))
    dxn2 = carry(matmul, dz, W["w_in"], "nt", "proj_in_bwd")
    dh1, dh1b, small["norm_mix"] = rmsnorm_bwd(dxn2, h1, norm_mix, dh2, "norm_mix_bwd")

    def pack(parts):
        flat = jnp.concatenate([v.reshape(-1) for v in parts.values()])
        return _pad_to(flat, 0, SUBLANE * LANE).reshape(-1, LANE), flat.shape[0]

    def unpack(everyones, n, parts, tag):
        rows = everyones.shape[0] // N_DEV
        summed = sum_devices(everyones.reshape(N_DEV, rows, LANE), "sum_" + tag).reshape(-1)[:n]
        out, off = {}, 0
        for k, v in parts.items():
            out[k] = summed[off:off + v.size].reshape(v.shape)
            off += v.size
        return out

    early = dict(small)
    flat_early, n_early = pack(early)
    small_landed = []
    dxn1 = ffn_bwd("ffn1", dh1b, xn1, gate1, up1, act1, "w1_gate", "w1_up", "w1_down",
                   extra=(gather_task([flat_early], [0]), small_landed.extend))
    tot = unpack(small_landed[0], n_early, early, "small")
    grad_x, _, g_norm_ffn1 = rmsnorm_bwd(dxn1, xs, norm_ffn1, dh1, "norm_ffn1_bwd")
    assert not wait_sibling and sorted(k for group, _ in crossing for k in group) == sorted(big_names)
    last = {"norm_ffn1": g_norm_ffn1}
    flat_last, n_last = pack(last)
    ((everyones_last,),) = run_tasks([gather_task([flat_last], [0])], "gather_last")
    tot.update(unpack(everyones_last, n_last, last, "last"))

    arrived = scatter_wait([started for _, started in crossing], grad_x, "scatter_wait")
    out_g, out_d, out_m, out_v = {}, {}, {}, {}
    for (group, _), (parts, recv) in zip(crossing, arrived):
        for k, part, landed_k in zip(group, parts, recv):
            g, dl, nm, nv = adam_sharded(part, landed_k, my_chip, view(weights[k], k), view(moments_m[k], k),
                                         view(moments_v[k], k), "adam_" + k)
            out_g[k], out_d[k], out_m[k], out_v[k] = unview(g, k), unview(dl, k), unview(nm, k), unview(nv, k)

    _, ssm_vjp = jax.vjp(_ssm_discretize, ssm_log_dt[0], ssm_a_re[0], ssm_a_im[0], ssm_b_re[0], ssm_b_im[0])
    g_log_dt, g_a_re, g_a_im, g_b_re, g_b_im = ssm_vjp((tot["abar_r"], tot["abar_i"], tot["bbar_r"], tot["bbar_i"]))
    small_grads = {
        "norm_ffn1": tot["norm_ffn1"], "norm_mix": tot["norm_mix"], "ssm_log_dt": g_log_dt, "ssm_a_re": g_a_re,
        "ssm_a_im": g_a_im, "ssm_b_re": g_b_re, "ssm_b_im": g_b_im, "ssm_c_re": tot["c_re"], "ssm_c_im": tot["c_im"],
        "ssm_d": tot["ssm_d"], "gmlp_norm_v": tot["gmlp_norm_v"],
        "gmlp_w_s": jnp.where(causal[None], tot["gmlp_w_s"], 0.0), "gmlp_b_s": tot["gmlp_b_s"],
        "norm_ssm_out": tot["norm_ssm_out"], "norm_gmlp_out": tot["norm_gmlp_out"], "norm_ffn2": tot["norm_ffn2"],
        "norm_ple": tot["norm_ple"], "norm_final": tot["norm_final"]}
    for k, g in small_grads.items():
        shp = weights[k].shape
        g2 = _as2d(g.reshape(shp))
        dl, nm, nv = adam_small(g2, _as2d(weights[k]), _as2d(moments_m[k]), _as2d(moments_v[k]), "adam_" + k)
        out_g[k], out_d[k], out_m[k], out_v[k] = g2.reshape(shp), dl.reshape(shp), nm.reshape(shp), nv.reshape(shp)

    return (loss, grad_x[None], *[out_g[k] for k in names], *[out_d[k] for k in names],
            *[out_m[k] for k in names], *[out_v[k] for k in names])
```

```python
import math

import jax
import jax.numpy as jnp
from jax import lax
from jax.experimental import pallas as pl
from jax.experimental.pallas import tpu as pltpu

F32 = jnp.float32
BF16 = jnp.bfloat16
MESH_DT = pl.DeviceIdType.MESH

N_DEV = 8
N_CHIP = 4
LANE = 128
SUBLANE = 8
VMEM_LIMIT = 60 * 1024 * 1024

EPS = 1e-6
SSM_GROUP = 16
SSM_STATE = 64
GROUPS_PER_BLOCK = LANE // SSM_GROUP
STATE_BLOCK = GROUPS_PER_BLOCK * SSM_STATE
GMLP_HEAD = 128
CHUNK = 128

ADAM_LR = 0.001
ADAM_B1 = 0.9
ADAM_B2 = 0.999
ADAM_EPS = 1e-08
ADAM_WD = 0.01
ADAM_STEP = 10

GELU_K = math.sqrt(2.0 / math.pi)
GELU_C = 0.044715


def _cparams():
    return pltpu.CompilerParams(vmem_limit_bytes=VMEM_LIMIT)


def _tile(n, pref):
    if n <= pref:
        return n
    t = (pref // LANE) * LANE
    while t > 0:
        if n % t == 0:
            return t
        t -= LANE
    return n


def _row_tile(n, pref):
    if n <= pref:
        return n
    t = (pref // SUBLANE) * SUBLANE
    while t > 0:
        if n % t == 0:
            return t
        t -= SUBLANE
    return n


def _gelu(x):
    t = jnp.tanh(GELU_K * (x + GELU_C * x * x * x))
    return 0.5 * x * (1.0 + t)


def _gelu_grad(x):
    t = jnp.tanh(GELU_K * (x + GELU_C * x * x * x))
    return 0.5 * (1.0 + t) + 0.5 * x * (1.0 - t * t) * (GELU_K * (1.0 + 3.0 * GELU_C * x * x))


def _sigmoid(x):
    return 0.5 * jnp.tanh(0.5 * x) + 0.5


_DN = {
    "nn": (((1,), (0,)), ((), ())),
    "nt": (((1,), (1,)), ((), ())),
    "tn": (((0,), (0,)), ((), ())),
}


def _dot(a, b, mode="nn"):
    return lax.dot_general(a, b, _DN[mode], preferred_element_type=F32)


class CommTask:
    def __init__(self, inputs, out_shape, n_sems, start, late, finish, in_place=False):
        self.inputs, self.out_shape, self.n_sems = list(inputs), list(out_shape), n_sems
        self.start, self.late, self.finish = start, late, finish
        self.in_place = in_place


def _task_aliases(tasks, first_in, first_out):
    aliases = {}
    for t in tasks:
        if t.in_place:
            aliases.update({first_in + i: first_out + i for i in range(len(t.inputs))})
        first_in += len(t.inputs)
        first_out += len(t.out_shape)
    return aliases


def _call(body, *, name, grid, in_specs, out_specs, out_shape, args, scratch_shapes=(), tasks=()):
    in_specs, out_specs, out_shape = list(in_specs), list(out_specs), list(out_shape)
    scratch_shapes = list(scratch_shapes)
    if not tasks:
        return pl.pallas_call(
            body, name=name, grid=grid, in_specs=in_specs, out_specs=out_specs, out_shape=out_shape,
            scratch_shapes=scratch_shapes, compiler_params=_cparams())(*args)
    n_in, n_out, n_scr = len(in_specs), len(out_specs), len(scratch_shapes)
    t_in = [len(t.inputs) for t in tasks]
    t_out = [len(t.out_shape) for t in tasks]
    late_step = grid[0] - max(1, grid[0] // 4)
    has_late = grid[0] >= 2

    def carried(*refs):
        pos = n_in
        task_ins = []
        for k in t_in:
            task_ins.append(refs[pos:pos + k])
            pos += k
        outs = refs[pos:pos + n_out]
        pos += n_out
        task_outs = []
        for k in t_out:
            task_outs.append(refs[pos:pos + k])
            pos += k
        scratch = refs[pos:pos + n_scr]
        pos += n_scr
        sems = [refs[pos + 3 * i:pos + 3 * i + 3] for i in range(len(tasks))]
        ids = [pl.program_id(d) for d in range(len(grid))]
        rest_zero = True
        for d in range(1, len(grid)):
            rest_zero = jnp.logical_and(rest_zero, ids[d] == 0)
        first = jnp.logical_and(ids[0] == 0, rest_zero)
        last = ids[0] == grid[0] - 1
        for d in range(1, len(grid)):
            last = jnp.logical_and(last, ids[d] == grid[d] - 1)

        @pl.when(first)
        def _():
            for t, ti, to, s in zip(tasks, task_ins, task_outs, sems):
                t.start(ti, to, *s)

        if has_late:
            @pl.when(jnp.logical_and(ids[0] == late_step, rest_zero))
            def _():
                for t, ti, to, s in zip(tasks, task_ins, task_outs, sems):
                    t.late(ti, to, *s)

        body(*refs[:n_in], *outs, *scratch)

        @pl.when(last)
        def _():
            for t, ti, to, s in zip(tasks, task_ins, task_outs, sems):
                if not has_late:
                    t.late(ti, to, *s)
                t.finish(ti, to, *s)

    any_spec = pl.BlockSpec(memory_space=pl.ANY)
    sem_shapes = [pltpu.SemaphoreType.DMA((n,)) for t in tasks for n in t.n_sems]
    res = pl.pallas_call(
        carried, name=name, grid=grid,
        in_specs=in_specs + [any_spec] * sum(t_in), out_specs=out_specs + [any_spec] * sum(t_out),
        out_shape=out_shape + [s for t in tasks for s in t.out_shape],
        input_output_aliases=_task_aliases(tasks, n_in, n_out),
        scratch_shapes=scratch_shapes + sem_shapes, compiler_params=_cparams(),
    )(*args, *[a for t in tasks for a in t.inputs])
    res = list(res)
    task_res, pos = [], n_out
    for k in t_out:
        task_res.append(res[pos:pos + k])
        pos += k
    return res[:n_out], task_res


def _mm_dims(a, b, mode):
    if mode == "nn":
        (m, k), (k2, n) = a.shape, b.shape
    elif mode == "nt":
        (m, k), (n, k2) = a.shape, b.shape
    else:
        (k, m), (k2, n) = a.shape, b.shape
    assert k == k2, (a.shape, b.shape, mode)
    return m, n, k


def _mm_specs(mode, tm, tn, tk):
    if mode == "tn":
        a_spec = pl.BlockSpec((tk, tm), lambda i, j, k: (k, i))
    else:
        a_spec = pl.BlockSpec((tm, tk), lambda i, j, k: (i, k))
    if mode == "nt":
        b_spec = pl.BlockSpec((tn, tk), lambda i, j, k: (j, k))
    else:
        b_spec = pl.BlockSpec((tk, tn), lambda i, j, k: (k, j))
    return a_spec, b_spec


def _accumulate(acc, nk, partial, emit):
    if nk == 1:
        emit(partial)
        return
    kk = pl.program_id(2)

    @pl.when(kk == 0)
    def _():
        acc[...] = partial

    @pl.when(kk > 0)
    def _():
        acc[...] += partial

    @pl.when(kk == nk - 1)
    def _():
        emit(acc[...])


def matmul(a, b, mode, name, out_dtype=F32, res=None, scale=1.0, tm=1024, tn=1024, tk=2048, tasks=()):
    m, n, k = _mm_dims(a, b, mode)
    tm, tn, tk = _tile(m, tm), _tile(n, tn), _tile(k, tk)
    nk = k // tk
    a_spec, b_spec = _mm_specs(mode, tm, tn, tk)
    o_spec = pl.BlockSpec((tm, tn), lambda i, j, k: (i, j))
    has_res = res is not None

    def body(*refs):
        if has_res:
            a_ref, b_ref, r_ref, o_ref, acc = refs
        else:
            a_ref, b_ref, o_ref, acc = refs

        def emit(v):
            if scale != 1.0:
                v = v * scale
            if has_res:
                v = r_ref[...] + v
            o_ref[...] = v.astype(out_dtype)

        _accumulate(acc, nk, _dot(a_ref[...], b_ref[...], mode), emit)

    out = _call(
        body, name=name, grid=(m // tm, n // tn, nk),
        in_specs=[a_spec, b_spec] + ([o_spec] if has_res else []), out_specs=[o_spec],
        out_shape=[jax.ShapeDtypeStruct((m, n), out_dtype)], args=(a, b) + ((res,) if has_res else ()),
        scratch_shapes=[pltpu.VMEM((tm, tn) if nk > 1 else (SUBLANE, LANE), F32)], tasks=tasks)
    return (out[0][0], out[1]) if tasks else out[0]


def ffn_up(xn, wu, gate, name, tm=1024, tn=1024, tk=2048, tasks=()):
    m, n, k = _mm_dims(xn, wu, "nt")
    tm, tn, tk = _tile(m, tm), _tile(n, tn), _tile(k, tk)
    nk = k // tk
    a_spec, b_spec = _mm_specs("nt", tm, tn, tk)
    o_spec = pl.BlockSpec((tm, tn), lambda i, j, k: (i, j))

    def body(a_ref, u_ref, gate_ref, gate_b_ref, up_b_ref, act_ref, acc):
        def emit(u):
            g = gate_ref[...]
            gate_b_ref[...] = g.astype(BF16)
            up_b_ref[...] = u.astype(BF16)
            act_ref[...] = (g * _sigmoid(g) * u).astype(BF16)

        _accumulate(acc, nk, _dot(a_ref[...], u_ref[...], "nt"), emit)

    out = _call(
        body, name=name, grid=(m // tm, n // tn, nk), in_specs=[a_spec, b_spec, o_spec],
        out_specs=[o_spec, o_spec, o_spec],
        out_shape=[jax.ShapeDtypeStruct((m, n), BF16), jax.ShapeDtypeStruct((m, n), BF16),
                   jax.ShapeDtypeStruct((m, n), BF16)],
        args=(xn, wu, gate), scratch_shapes=[pltpu.VMEM((tm, tn) if nk > 1 else (SUBLANE, LANE), F32)], tasks=tasks)
    return (tuple(out[0]), out[1]) if tasks else tuple(out)


def ffn_bwd_act(dh, wd, gate, up, name, tm=1024, tn=1024, tk=2048, tasks=()):
    m, n, k = _mm_dims(dh, wd, "nt")
    tm, tn, tk = _tile(m, tm), _tile(n, tn), _tile(k, tk)
    nk = k // tk
    a_spec, b_spec = _mm_specs("nt", tm, tn, tk)
    o_spec = pl.BlockSpec((tm, tn), lambda i, j, k: (i, j))

    def body(a_ref, b_ref, gate_ref, up_ref, dg_ref, du_ref, acc):
        def emit(total):
            dact = 0.5 * total
            g = gate_ref[...].astype(F32)
            sg = _sigmoid(g)
            du_ref[...] = (dact * (g * sg)).astype(BF16)
            dg_ref[...] = (dact * up_ref[...].astype(F32) * (sg * (1.0 + g * (1.0 - sg)))).astype(BF16)

        _accumulate(acc, nk, _dot(a_ref[...], b_ref[...], "nt"), emit)

    out = _call(
        body, name=name, grid=(m // tm, n // tn, nk), in_specs=[a_spec, b_spec, o_spec, o_spec],
        out_specs=[o_spec, o_spec],
        out_shape=[jax.ShapeDtypeStruct((m, n), BF16), jax.ShapeDtypeStruct((m, n), BF16)],
        args=(dh, wd, gate, up), scratch_shapes=[pltpu.VMEM((tm, tn) if nk > 1 else (SUBLANE, LANE), F32)],
        tasks=tasks)
    return (tuple(out[0]), out[1]) if tasks else tuple(out)


def _rows(t, d, tr):
    return pl.BlockSpec((tr, d), lambda i: (i, 0))


def _vec(d):
    return pl.BlockSpec((1, d), lambda i: (0, 0))


def rmsnorm_fwd(x, g, name, tr=512):
    t, d = x.shape
    tr = _row_tile(t, tr)

    def body(x_ref, g_ref, o_ref):
        xf = x_ref[...]
        r = lax.rsqrt(jnp.mean(xf * xf, axis=-1, keepdims=True) + EPS)
        o_ref[...] = (xf * r * g_ref[...]).astype(BF16)

    return pl.pallas_call(
        body, name=name, grid=(t // tr,), in_specs=[_rows(t, d, tr), _vec(d)], out_specs=_rows(t, d, tr),
        out_shape=jax.ShapeDtypeStruct((t, d), BF16), compiler_params=_cparams(),
    )(x, g)


def _rms_bwd(dxn, xf, g):
    r = lax.rsqrt(jnp.mean(xf * xf, axis=-1, keepdims=True) + EPS)
    xhat = xf * r
    dg = jnp.sum(dxn * xhat, axis=0, keepdims=True)
    dxh = dxn * g
    dx = r * (dxh - xhat * jnp.mean(dxh * xhat, axis=-1, keepdims=True))
    return dx, dg


def rmsnorm_bwd(dxn, x, g, dres, name, tr=256):
    t, d = x.shape
    tr = _row_tile(t, tr)

    def body(dxn_ref, x_ref, g_ref, dres_ref, o_ref, ob_ref, dg_ref):
        dx, dg = _rms_bwd(dxn_ref[...], x_ref[...], g_ref[...])
        out = dres_ref[...] + dx
        o_ref[...] = out
        ob_ref[...] = out.astype(BF16)

        @pl.when(pl.program_id(0) == 0)
        def _():
            dg_ref[...] = jnp.zeros_like(dg_ref)

        dg_ref[...] += dg

    return pl.pallas_call(
        body, name=name, grid=(t // tr,),
        in_specs=[_rows(t, d, tr), _rows(t, d, tr), _vec(d), _rows(t, d, tr)],
        out_specs=[_rows(t, d, tr), _rows(t, d, tr), _vec(d)],
        out_shape=[jax.ShapeDtypeStruct((t, d), F32), jax.ShapeDtypeStruct((t, d), BF16),
                   jax.ShapeDtypeStruct((1, d), F32)],
        compiler_params=_cparams(),
    )(dxn, x, g, dres)


def final_loss(h, target, g, name, tr=256):
    t, d = h.shape
    tr = _row_tile(t, tr)

    def body(h_ref, t_ref, g_ref, dh_ref, loss_ref, dg_ref):
        xf = h_ref[...]
        gg = g_ref[...]
        r = lax.rsqrt(jnp.mean(xf * xf, axis=-1, keepdims=True) + EPS)
        xhat = xf * r
        e = xhat * gg - t_ref[...]
        part = jnp.sum(jnp.sum(e * e, axis=1, keepdims=True), axis=0, keepdims=True) * (0.5 / d)
        dout = e * (1.0 / d)
        dg = jnp.sum(dout * xhat, axis=0, keepdims=True)
        dxh = dout * gg
        dh_ref[...] = r * (dxh - xhat * jnp.mean(dxh * xhat, axis=-1, keepdims=True))

        @pl.when(pl.program_id(0) == 0)
        def _():
            dg_ref[...] = jnp.zeros_like(dg_ref)
            loss_ref[...] = jnp.zeros_like(loss_ref)

        dg_ref[...] += dg
        loss_ref[...] += jnp.broadcast_to(part, loss_ref.shape)

    return pl.pallas_call(
        body, name=name, grid=(t // tr,),
        in_specs=[_rows(t, d, tr), _rows(t, d, tr), _vec(d)],
        out_specs=[_rows(t, d, tr), pl.BlockSpec((SUBLANE, LANE), lambda i: (0, 0)), _vec(d)],
        out_shape=[jax.ShapeDtypeStruct((t, d), F32), jax.ShapeDtypeStruct((SUBLANE, LANE), F32),
                   jax.ShapeDtypeStruct((1, d), F32)],
        compiler_params=_cparams(),
    )(h, target, g)


def ple_fwd(h, glin, pp, name, tr=512):
    t, d = h.shape
    tr = _row_tile(t, tr)

    def body(h_ref, gl_ref, pp_ref, o_ref):
        o_ref[...] = h_ref[...] + _sigmoid(gl_ref[...]) * pp_ref[...]

    sp = _rows(t, d, tr)
    return pl.pallas_call(
        body, name=name, grid=(t // tr,), in_specs=[sp, sp, sp], out_specs=sp,
        out_shape=jax.ShapeDtypeStruct((t, d), F32), compiler_params=_cparams(),
    )(h, glin, pp)


def ple_bwd(dh, glin, pp, name, tr=512):
    t, d = dh.shape
    tr = _row_tile(t, tr)

    def body(dh_ref, gl_ref, pp_ref, dpp_ref, dgl_ref):
        gate = _sigmoid(gl_ref[...])
        dh_ = dh_ref[...]
        dpp_ref[...] = (dh_ * gate).astype(BF16)
        dgl_ref[...] = (dh_ * pp_ref[...] * gate * (1.0 - gate)).astype(BF16)

    sp = _rows(t, d, tr)
    return pl.pallas_call(
        body, name=name, grid=(t // tr,), in_specs=[sp, sp, sp], out_specs=[sp, sp],
        out_shape=[jax.ShapeDtypeStruct((t, d), BF16), jax.ShapeDtypeStruct((t, d), BF16)],
        compiler_params=_cparams(),
    )(dh, glin, pp)


def mix_out_fwd(y_pre, glin, y_gmlp, g_so, g_go, name, tr=512):
    t, d = y_pre.shape
    tr = _row_tile(t, tr)

    def body(yp_ref, gl_ref, yg_ref, gs_ref, gg_ref, o_ref):
        ys = _gelu(yp_ref[...]) * _sigmoid(gl_ref[...])
        r = lax.rsqrt(jnp.mean(ys * ys, axis=-1, keepdims=True) + EPS)
        o_ref[:, 0:d] = (ys * r * gs_ref[...]).astype(BF16)
        yq = yg_ref[...]
        r2 = lax.rsqrt(jnp.mean(yq * yq, axis=-1, keepdims=True) + EPS)
        o_ref[:, d:2 * d] = (yq * r2 * gg_ref[...]).astype(BF16)

    sp = _rows(t, d, tr)
    return pl.pallas_call(
        body, name=name, grid=(t // tr,), in_specs=[sp, sp, sp, _vec(d), _vec(d)],
        out_specs=_rows(t, 2 * d, tr), out_shape=jax.ShapeDtypeStruct((t, 2 * d), BF16),
        compiler_params=_cparams(),
    )(y_pre, glin, y_gmlp, g_so, g_go)


def mix_out_bwd(dycat, y_pre, glin, y_gmlp, g_so, g_go, name, tr=256):
    t, d = y_pre.shape
    tr = _row_tile(t, tr)

    def body(dy_ref, yp_ref, gl_ref, yg_ref, gs_ref, gg_ref, dyg_ref, dl_ref, dyq_ref, dgs_ref, dgg_ref):
        yg = _gelu(yp_ref[...])
        sg = _sigmoid(gl_ref[...])
        dys, dgs = _rms_bwd(dy_ref[:, 0:d], yg * sg, gs_ref[...])
        dyg_ref[...] = dys * sg
        dl_ref[...] = (dys * yg * sg * (1.0 - sg)).astype(BF16)
        dyq, dgg = _rms_bwd(dy_ref[:, d:2 * d], yg_ref[...], gg_ref[...])
        dyq_ref[...] = dyq

        @pl.when(pl.program_id(0) == 0)
        def _():
            dgs_ref[...] = jnp.zeros_like(dgs_ref)
            dgg_ref[...] = jnp.zeros_like(dgg_ref)

        dgs_ref[...] += dgs
        dgg_ref[...] += dgg

    sp = _rows(t, d, tr)
    return pl.pallas_call(
        body, name=name, grid=(t // tr,),
        in_specs=[_rows(t, 2 * d, tr), sp, sp, sp, _vec(d), _vec(d)],
        out_specs=[sp, sp, sp, _vec(d), _vec(d)],
        out_shape=[jax.ShapeDtypeStruct((t, d), F32), jax.ShapeDtypeStruct((t, d), BF16),
                   jax.ShapeDtypeStruct((t, d), F32), jax.ShapeDtypeStruct((1, d), F32),
                   jax.ShapeDtypeStruct((1, d), F32)],
        compiler_params=_cparams(),
    )(dycat, y_pre, glin, y_gmlp, g_so, g_go)


SCAN_COLS = 512


def _scan_tile(xr, xi, const, cr, ci, reverse):
    for lvl, sh in enumerate((1, 2, 4)):
        ar, ai = const(2 * lvl), const(2 * lvl + 1)
        s = (SUBLANE - sh) if reverse else sh
        rr = pltpu.roll(xr, s, 0)
        ri = pltpu.roll(xi, s, 0)
        xr, xi = xr + ar * rr - ai * ri, xi + ar * ri + ai * rr
    pr, pi_ = const(6), const(7)
    xr, xi = xr + pr * cr - pi_ * ci, xi + pr * ci + pi_ * cr
    return xr, xi


def _bcast_row(x, row):
    return jnp.broadcast_to(x[row:row + 1, :], x.shape)


def s5_fwd(z, bc_r, bc_i, cc_r, cc_i, apw, dvec, name, tc=256, tasks=()):
    t = z.shape[0]
    nblk = bc_r.shape[0]
    d = nblk * LANE
    ns = nblk * STATE_BLOCK
    tc = _row_tile(t, tc)
    ntile = tc // SUBLANE

    def body(z_ref, br_ref, bi_ref, cr_ref, ci_ref, apw_ref, d_ref, y_ref, yg_ref, sr_ref, si_ref, carry):
        @pl.when(pl.program_id(0) == 0)
        def _():
            carry[...] = jnp.zeros_like(carry)

        for j in range(nblk):
            uj = z_ref[:, j * LANE:(j + 1) * LANE]
            ub = uj.astype(BF16)
            for q in range(STATE_BLOCK // SCAN_COLS):
                c0 = j * STATE_BLOCK + q * SCAN_COLS
                cs = pl.ds(c0, SCAN_COLS)
                bs = slice(q * SCAN_COLS, (q + 1) * SCAN_COLS)
                sr_ref[:, cs] = _dot(ub, br_ref[j, :, bs])
                si_ref[:, cs] = _dot(ub, bi_ref[j, :, bs])
                const = lambda k, cs=cs: apw_ref[k, :, cs]

                def tile(k, c, cs=cs, const=const):
                    rows = pl.ds(pl.multiple_of(k * SUBLANE, SUBLANE), SUBLANE)
                    xr, xi = _scan_tile(sr_ref[rows, cs], si_ref[rows, cs], const, c[0], c[1], False)
                    sr_ref[rows, cs] = xr
                    si_ref[rows, cs] = xi
                    return _bcast_row(xr, SUBLANE - 1), _bcast_row(xi, SUBLANE - 1)

                c_r, c_i = lax.fori_loop(0, ntile, tile, (carry[0, :, cs], carry[1, :, cs]))
                carry[0, :, cs] = c_r
                carry[1, :, cs] = c_i
            sb = pl.ds(j * STATE_BLOCK, STATE_BLOCK)
            y = (_dot(sr_ref[:, sb].astype(BF16), cr_ref[j]) - _dot(si_ref[:, sb].astype(BF16), ci_ref[j])
                 + d_ref[:, j * LANE:(j + 1) * LANE] * uj)
            y_ref[:, j * LANE:(j + 1) * LANE] = y
            yg_ref[:, j * LANE:(j + 1) * LANE] = _gelu(y).astype(BF16)

    full3 = lambda shp: pl.BlockSpec(shp, lambda i: (0, 0, 0))
    out = _call(
        body, name=name, grid=(t // tc,),
        in_specs=[pl.BlockSpec((tc, d), lambda i: (i, 0)), full3(bc_r.shape), full3(bc_i.shape),
                  full3(cc_r.shape), full3(cc_i.shape), full3(apw.shape), _vec(d)],
        out_specs=[pl.BlockSpec((tc, d), lambda i: (i, 0)), pl.BlockSpec((tc, d), lambda i: (i, 0)),
                   pl.BlockSpec((tc, ns), lambda i: (i, 0)), pl.BlockSpec((tc, ns), lambda i: (i, 0))],
        out_shape=[jax.ShapeDtypeStruct((t, d), F32), jax.ShapeDtypeStruct((t, d), BF16),
                   jax.ShapeDtypeStruct((t, ns), F32), jax.ShapeDtypeStruct((t, ns), F32)],
        args=(z, bc_r, bc_i, cc_r, cc_i, apw, dvec), scratch_shapes=[pltpu.VMEM((2, SUBLANE, ns), F32)], tasks=tasks)
    return (tuple(out[0]), out[1]) if tasks else tuple(out)


def s5_bwd(dyg, y_pre, z, sr, si, bc_r, bc_i, cc_r, cc_i, apw_rev, dvec, name, tc=256, tasks=()):
    t = z.shape[0]
    nblk = bc_r.shape[0]
    d = nblk * LANE
    ns = nblk * STATE_BLOCK
    tc = _row_tile(t, tc)
    ntile = tc // SUBLANE
    nchunk = t // tc
    tiles_per_chunk = tc // SUBLANE

    def body(dyg_ref, yp_ref, z_ref, sr_ref, si_ref, pr_ref, pi_ref, br_ref, bi_ref, cr_ref, ci_ref, apw_ref,
             d_ref, du_ref, gd_ref, gcr_ref, gci_ref, gbr_ref, gbi_ref, gar_ref, gai_ref, lr_ref, li_ref, carry):
        step = pl.program_id(0)

        @pl.when(step == 0)
        def _():
            carry[...] = jnp.zeros_like(carry)
            for ref in (gd_ref, gcr_ref, gci_ref, gbr_ref, gbi_ref, gar_ref, gai_ref):
                ref[...] = jnp.zeros_like(ref)

        first_chunk = (step == nchunk - 1).astype(F32)
        keep_prev = 1.0 - first_chunk
        row0 = lax.broadcasted_iota(jnp.int32, (SUBLANE, SCAN_COLS), 0) == 0

        for j in range(nblk):
            lanes = slice(j * LANE, (j + 1) * LANE)
            uj = z_ref[:, lanes]
            ub = uj.astype(BF16)
            gy = dyg_ref[:, lanes] * _gelu_grad(yp_ref[:, lanes])
            gyb = gy.astype(BF16)
            gd_ref[:, lanes] += jnp.sum(gy * uj, axis=0, keepdims=True)
            for q in range(STATE_BLOCK // SCAN_COLS):
                c0 = j * STATE_BLOCK + q * SCAN_COLS
                cs = pl.ds(c0, SCAN_COLS)
                bs = slice(q * SCAN_COLS, (q + 1) * SCAN_COLS)
                lr_ref[:, cs] = _dot(gyb, cr_ref[j, bs, :], "nt")
                li_ref[:, cs] = -_dot(gyb, ci_ref[j, bs, :], "nt")
                const = lambda k, cs=cs: apw_ref[k, :, cs]

                def one_tile(rows, prev_r, prev_i, c, cs=cs, const=const):
                    cr_, ci_, gar, gai = c
                    xr, xi = _scan_tile(lr_ref[rows, cs], li_ref[rows, cs], const, cr_, ci_, True)
                    lr_ref[rows, cs] = xr
                    li_ref[rows, cs] = xi
                    spr = jnp.where(row0, prev_r, pltpu.roll(sr_ref[rows, cs], 1, 0))
                    spi = jnp.where(row0, prev_i, pltpu.roll(si_ref[rows, cs], 1, 0))
                    gar = gar + xr * spr + xi * spi
                    gai = gai + xi * spr - xr * spi
                    return _bcast_row(xr, 0), _bcast_row(xi, 0), gar, gai

                def tile(k, c, cs=cs, one_tile=one_tile):
                    kk = ntile - 1 - k
                    rows = pl.ds(pl.multiple_of(kk * SUBLANE, SUBLANE), SUBLANE)
                    prow = pl.ds(pl.multiple_of((kk - 1) * SUBLANE, SUBLANE), SUBLANE)
                    prev_r = _bcast_row(sr_ref[prow, cs], SUBLANE - 1)
                    prev_i = _bcast_row(si_ref[prow, cs], SUBLANE - 1)
                    return one_tile(rows, prev_r, prev_i, c)

                zero = jnp.zeros((SUBLANE, SCAN_COLS), F32)
                c = lax.fori_loop(0, ntile - 1, tile, (carry[0, :, cs], carry[1, :, cs], zero, zero))
                prev_r = _bcast_row(pr_ref[:, cs], SUBLANE - 1) * keep_prev
                prev_i = _bcast_row(pi_ref[:, cs], SUBLANE - 1) * keep_prev
                c_r, c_i, gar, gai = one_tile(pl.ds(0, SUBLANE), prev_r, prev_i, c)
                carry[0, :, cs] = c_r
                carry[1, :, cs] = c_i
                gar_ref[:, cs] += gar
                gai_ref[:, cs] += gai
            sb = pl.ds(j * STATE_BLOCK, STATE_BLOCK)
            lrb = lr_ref[:, sb].astype(BF16)
            lib = li_ref[:, sb].astype(BF16)
            gcr_ref[j] += _dot(gyb, sr_ref[:, sb].astype(BF16), "tn")
            gci_ref[j] -= _dot(gyb, si_ref[:, sb].astype(BF16), "tn")
            gbr_ref[j] += _dot(ub, lrb, "tn")
            gbi_ref[j] += _dot(ub, lib, "tn")
            du = _dot(lrb, br_ref[j], "nt") + _dot(lib, bi_ref[j], "nt") + gy * d_ref[:, lanes]
            du_ref[:, lanes] = du.astype(BF16)

    rev = lambda i: (nchunk - 1 - i, 0)
    prev = lambda i: (jnp.maximum((nchunk - 1 - i) * tiles_per_chunk - 1, 0), 0)
    full3 = lambda shp: pl.BlockSpec(shp, lambda i: (0, 0, 0))
    acc3 = pl.BlockSpec((nblk, LANE, STATE_BLOCK), lambda i: (0, 0, 0))
    acc_rows = pl.BlockSpec((SUBLANE, ns), lambda i: (0, 0))
    out = _call(
        body, name=name, grid=(nchunk,),
        in_specs=[pl.BlockSpec((tc, d), rev), pl.BlockSpec((tc, d), rev), pl.BlockSpec((tc, d), rev),
                  pl.BlockSpec((tc, ns), rev), pl.BlockSpec((tc, ns), rev),
                  pl.BlockSpec((SUBLANE, ns), prev), pl.BlockSpec((SUBLANE, ns), prev),
                  full3(bc_r.shape), full3(bc_i.shape), full3(cc_r.shape), full3(cc_i.shape), full3(apw_rev.shape),
                  _vec(d)],
        out_specs=[pl.BlockSpec((tc, d), rev), _vec(d), acc3, acc3, acc3, acc3, acc_rows, acc_rows],
        out_shape=[jax.ShapeDtypeStruct((t, d), BF16), jax.ShapeDtypeStruct((1, d), F32)]
        + [jax.ShapeDtypeStruct((nblk, LANE, STATE_BLOCK), F32)] * 4
        + [jax.ShapeDtypeStruct((SUBLANE, ns), F32)] * 2,
        args=(dyg, y_pre, z, sr, si, sr, si, bc_r, bc_i, cc_r, cc_i, apw_rev, dvec),
        scratch_shapes=[pltpu.VMEM((tc, ns), F32), pltpu.VMEM((tc, ns), F32), pltpu.VMEM((2, SUBLANE, ns), F32)],
        tasks=tasks)
    return (tuple(out[0]), out[1]) if tasks else tuple(out)


def _cmul(a, b):
    return a[0] * b[0] - a[1] * b[1], a[0] * b[1] + a[1] * b[0]


def _scan_constants(abar_r, abar_i, reverse):
    ar = abar_r.reshape(1, -1)
    ai = abar_i.reshape(1, -1)
    if reverse:
        ai = -ai
    pw = [(ar, ai)]
    for _ in range(SUBLANE - 1):
        pw.append(_cmul(pw[-1], (ar, ai)))
    rows = lax.broadcasted_iota(jnp.int32, (SUBLANE, 1), 0)
    out = []
    for sh in (1, 2, 4):
        keep = (rows <= SUBLANE - 1 - sh) if reverse else (rows >= sh)
        for part in pw[sh - 1]:
            out.append(jnp.where(keep, part, 0.0))
    for comp in (0, 1):
        stack = jnp.concatenate([pw[k][comp] for k in range(SUBLANE)], axis=0)
        out.append(stack[::-1] if reverse else stack)
    return jnp.stack(out, axis=0).astype(F32)


def _ssm_discretize(log_dt, a_re, a_im, b_re, b_im):
    dt = jnp.exp(log_dt)[:, None]
    lr = jnp.minimum(a_re, -1e-4)
    li = a_im
    mag = jnp.exp(lr * dt)
    ang = li * dt
    abar_r = mag * jnp.cos(ang)
    abar_i = mag * jnp.sin(ang)
    den = lr * lr + li * li
    xr = abar_r - 1.0
    xi = abar_i
    zr = (xr * lr + xi * li) / den
    zi = (xi * lr - xr * li) / den
    bbar_r = zr[..., None] * b_re - zi[..., None] * b_im
    bbar_i = zr[..., None] * b_im + zi[..., None] * b_re
    return abar_r, abar_i, bbar_r, bbar_i


def _block_diag(w):
    g, a, b = w.shape
    nb = g // GROUPS_PER_BLOCK
    eye = jnp.eye(GROUPS_PER_BLOCK, dtype=w.dtype)
    w5 = w.reshape(nb, GROUPS_PER_BLOCK, a, b)
    out = w5[:, :, :, None, :] * eye[None, :, None, :, None]
    return out.reshape(nb, GROUPS_PER_BLOCK * a, GROUPS_PER_BLOCK * b)


def _block_diag_extract(m, a, b):
    nb = m.shape[0]
    eye = jnp.eye(GROUPS_PER_BLOCK, dtype=m.dtype)
    m5 = m.reshape(nb, GROUPS_PER_BLOCK, a, GROUPS_PER_BLOCK, b)
    out = jnp.sum(m5 * eye[None, :, None, :, None], axis=3)
    return out.reshape(nb * GROUPS_PER_BLOCK, a, b)


SSD_L = 16
PAIR = 2


def _pair_diag(w):
    g, a, b = w.shape
    eye = jnp.eye(PAIR, dtype=w.dtype)
    out = w.reshape(g // PAIR, PAIR, a, b)[:, :, :, None, :] * eye[None, :, None, :, None]
    return out.reshape(g // PAIR, PAIR * a, PAIR * b)


def _ssd_matrices(abar_r, abar_i, bbar_r, bbar_i, c_re, c_im):
    g, n = abar_r.shape
    p = bbar_r.shape[2]
    ell = SSD_L
    pw = [(jnp.ones_like(abar_r), jnp.zeros_like(abar_i))]
    for _ in range(ell):
        pw.append(_cmul(pw[-1], (abar_r, abar_i)))
    pr = jnp.stack([q[0] for q in pw])
    pi = jnp.stack([q[1] for q in pw])

    def c_times(kr, ki):
        return (c_re[None] * kr[:, :, None, :] - c_im[None] * ki[:, :, None, :],
                c_re[None] * ki[:, :, None, :] + c_im[None] * kr[:, :, None, :])

    car, cai = c_times(pr[:ell], pi[:ell])
    taps = jnp.einsum("kgpn,gnq->gkpq", car, bbar_r) - jnp.einsum("kgpn,gnq->gkpq", cai, bbar_i)
    lag = jnp.arange(ell)[None, :] - jnp.arange(ell)[:, None]
    onehot = (lag[None] == jnp.arange(ell)[:, None, None]).astype(F32)
    m1 = jnp.einsum("kst,gkpq->gsqtp", onehot, taps).reshape(g, ell * p, ell * p)

    rev_r, rev_i = pr[:ell][::-1], pi[:ell][::-1]
    m2r = rev_r[:, :, :, None] * bbar_r[None] - rev_i[:, :, :, None] * bbar_i[None]
    m2i = rev_r[:, :, :, None] * bbar_i[None] + rev_i[:, :, :, None] * bbar_r[None]
    m2r = jnp.transpose(m2r, (1, 0, 3, 2)).reshape(g, ell * p, n)
    m2i = jnp.transpose(m2i, (1, 0, 3, 2)).reshape(g, ell * p, n)

    car1, cai1 = c_times(pr[1:], pi[1:])
    m3r = jnp.transpose(car1, (1, 3, 0, 2)).reshape(g, n, ell * p)
    m3i = -jnp.transpose(cai1, (1, 3, 0, 2)).reshape(g, n, ell * p)
    m3 = jnp.concatenate([_pair_diag(m3r), _pair_diag(m3i)], axis=1)
    return m1, _pair_diag(m2r), _pair_diag(m2i), m3, pw[ell][0], pw[ell][1]


def _to_pairs(a, ell=SSD_L):
    t, d = a.shape
    nq = d // (PAIR * SSM_GROUP)
    return a.reshape(t // ell, ell, nq, PAIR, SSM_GROUP).transpose(2, 0, 3, 1, 4).reshape(nq, t // ell, -1)


def _from_pairs(a, ell=SSD_L):
    nq, nc, _ = a.shape
    return a.reshape(nq, nc, PAIR, ell, SSM_GROUP).transpose(1, 3, 0, 2, 4).reshape(nc * ell, -1)


def s5c_fwd(u, u_prev, m1, m2r, m2i, m3, apw, name):
    nq, nc, w = u.shape
    half = w // PAIR
    ns = m2r.shape[2]
    ntile = nc // SUBLANE

    def body(u_ref, up_ref, m1_ref, m2r_ref, m2i_ref, m3_ref, apw_ref, y_ref, sr_ref, si_ref):
        up = up_ref[...]
        sr_ref[...] = _dot(up, m2r_ref[...])
        si_ref[...] = _dot(up, m2i_ref[...])
        const = lambda k: apw_ref[k]

        def tile(k, c):
            rows = pl.ds(pl.multiple_of(k * SUBLANE, SUBLANE), SUBLANE)
            xr, xi = _scan_tile(sr_ref[rows, :], si_ref[rows, :], const, c[0], c[1], False)
            sr_ref[rows, :] = xr
            si_ref[rows, :] = xi
            return _bcast_row(xr, SUBLANE - 1), _bcast_row(xi, SUBLANE - 1)

        zero = jnp.zeros((SUBLANE, ns), F32)
        lax.fori_loop(0, ntile, tile, (zero, zero))
        state = jnp.concatenate([sr_ref[...].astype(BF16), si_ref[...].astype(BF16)], axis=1)
        carried = _dot(state, m3_ref[...])
        uu = u_ref[...]
        for h in range(PAIR):
            cols = slice(h * half, (h + 1) * half)
            y_ref[:, cols] = _dot(uu[:, cols], m1_ref[h]) + carried[:, cols]

    per_pair = lambda shp: pl.BlockSpec((None,) + shp, lambda q: (q, 0, 0))
    return pl.pallas_call(
        body, name=name, grid=(nq,),
        in_specs=[per_pair((nc, w)), per_pair((nc, w)), pl.BlockSpec((PAIR, half, half), lambda q: (q, 0, 0)),
                  per_pair(m2r.shape[1:]), per_pair(m2i.shape[1:]), per_pair(m3.shape[1:]),
                  pl.BlockSpec((8, SUBLANE, ns), lambda q: (0, 0, q))],
        out_specs=[per_pair((nc, w)), per_pair((nc, ns)), per_pair((nc, ns))],
        out_shape=[jax.ShapeDtypeStruct((nq, nc, w), F32), jax.ShapeDtypeStruct((nq, nc, ns), F32),
                   jax.ShapeDtypeStruct((nq, nc, ns), F32)],
        compiler_params=_cparams(),
    )(u, u_prev, m1, m2r, m2i, m3, apw)


def s5c_bwd(u, dy, dy_next, sr, si, m1, m2r, m2i, m3, apw_rev, name):
    nq, nc, w = u.shape
    half = w // PAIR
    ns = m2r.shape[2]
    ntile = nc // SUBLANE

    def body(u_ref, dy_ref, dyn_ref, sr_ref, si_ref, m1_ref, m2r_ref, m2i_ref, m3_ref, apw_ref,
             du_ref, dm1_ref, dm2r_ref, dm2i_ref, dm3_ref, dar_ref, dai_ref, lr_ref, li_ref):
        back = _dot(dyn_ref[...], m3_ref[...], "nt")
        lr_ref[...] = back[:, 0:ns]
        li_ref[...] = back[:, ns:2 * ns]
        const = lambda k: apw_ref[k]

        def tile(k, c):
            rows = pl.ds(pl.multiple_of((ntile - 1 - k) * SUBLANE, SUBLANE), SUBLANE)
            xr, xi = _scan_tile(lr_ref[rows, :], li_ref[rows, :], const, c[0], c[1], True)
            lr_ref[rows, :] = xr
            li_ref[rows, :] = xi
            s_r, s_i = sr_ref[rows, :], si_ref[rows, :]
            return (_bcast_row(xr, 0), _bcast_row(xi, 0), c[2] + xr * s_r + xi * s_i, c[3] + xi * s_r - xr * s_i)

        zero = jnp.zeros((SUBLANE, ns), F32)
        _, _, dar, dai = lax.fori_loop(0, ntile, tile, (zero, zero, zero, zero))
        dar_ref[...] = dar
        dai_ref[...] = dai
        lrb, lib = lr_ref[...].astype(BF16), li_ref[...].astype(BF16)
        uu, dyy = u_ref[...], dy_ref[...]
        from_state = _dot(lrb, m2r_ref[...], "nt") + _dot(lib, m2i_ref[...], "nt")
        for h in range(PAIR):
            cols = slice(h * half, (h + 1) * half)
            du_ref[:, cols] = _dot(dyy[:, cols], m1_ref[h], "nt") + from_state[:, cols]
            dm1_ref[h] = _dot(uu[:, cols], dyy[:, cols], "tn")
        dm2r_ref[...] = _dot(uu, lrb, "tn")
        dm2i_ref[...] = _dot(uu, lib, "tn")
        state = jnp.concatenate([sr_ref[...].astype(BF16), si_ref[...].astype(BF16)], axis=1)
        dm3_ref[...] = _dot(state, dyy, "tn")

    per_pair = lambda shp: pl.BlockSpec((None,) + shp, lambda q: (q, 0, 0))
    m1_spec = pl.BlockSpec((PAIR, half, half), lambda q: (q, 0, 0))
    return pl.pallas_call(
        body, name=name, grid=(nq,),
        in_specs=[per_pair((nc, w)), per_pair((nc, w)), per_pair((nc, w)), per_pair((nc, ns)), per_pair((nc, ns)),
                  m1_spec, per_pair(m2r.shape[1:]), per_pair(m2i.shape[1:]), per_pair(m3.shape[1:]),
                  pl.BlockSpec((8, SUBLANE, ns), lambda q: (0, 0, q))],
        out_specs=[per_pair((nc, w)), m1_spec, per_pair(m2r.shape[1:]), per_pair(m2i.shape[1:]),
                   per_pair(m3.shape[1:]), per_pair((SUBLANE, ns)), per_pair((SUBLANE, ns))],
        out_shape=[jax.ShapeDtypeStruct((nq, nc, w), F32), jax.ShapeDtypeStruct(m1.shape, F32),
                   jax.ShapeDtypeStruct(m2r.shape, F32), jax.ShapeDtypeStruct(m2i.shape, F32),
                   jax.ShapeDtypeStruct(m3.shape, F32), jax.ShapeDtypeStruct((nq, SUBLANE, ns), F32),
                   jax.ShapeDtypeStruct((nq, SUBLANE, ns), F32)],
        scratch_shapes=[pltpu.VMEM((nc, ns), F32), pltpu.VMEM((nc, ns), F32)], compiler_params=_cparams(),
    )(u, dy, dy_next, sr, si, m1, m2r, m2i, m3, apw_rev)


PAIRS_PER_BLOCK = LANE // (PAIR * SSM_GROUP)


def _lane_range(shape, start, size):
    lane = lax.broadcasted_iota(jnp.int32, shape, 1)
    return (lane >= start) & (lane < start + size)


def _pairs_from_rows(rows, qq):
    tiles = []
    for g2 in range(PAIR):
        src = (qq * PAIR + g2) * SSM_GROUP
        for half in range(SSD_L // SUBLANE):
            acc = jnp.zeros_like(rows[0])
            for tt in range(SUBLANE):
                dst = tt * SSM_GROUP
                moved = pltpu.roll(rows[half * SUBLANE + tt], (dst - src) % LANE, 1)
                acc = jnp.where(_lane_range(acc.shape, dst, SSM_GROUP), moved, acc)
            tiles.append(acc)
    return jnp.concatenate(tiles, axis=1)


def _rows_from_pairs(pairs, t):
    acc = jnp.zeros((pairs[0].shape[0], LANE), F32)
    src = (t % SUBLANE) * SSM_GROUP
    for qq in range(PAIRS_PER_BLOCK):
        for g2 in range(PAIR):
            k = g2 * (SSD_L // SUBLANE) + t // SUBLANE
            dst = (qq * PAIR + g2) * SSM_GROUP
            moved = pltpu.roll(pairs[qq][:, k * LANE:(k + 1) * LANE], (dst - src) % LANE, 1)
            acc = jnp.where(_lane_range(acc.shape, dst, SSM_GROUP), moved, acc)
    return acc


def _chunk_rows(ref, nc):
    return [ref[pl.ds(t, nc, stride=SSD_L), :] for t in range(SSD_L)]


def s5r_fwd(z, shift, m1, m2r, m2i, m3, apw, name):
    t = z.shape[0]
    nc = t // SSD_L
    nq, _, ns = m2r.shape
    nblk = nq // PAIRS_PER_BLOCK
    d = nblk * LANE
    half = m1.shape[1]
    ntile = nc // SUBLANE

    def body(u_ref, sh_ref, m1_ref, m2r_ref, m2i_ref, m3_ref, apw_ref, y_ref, sr_ref, si_ref, ys_ref):
        rows = _chunk_rows(u_ref, nc)
        for qq in range(PAIRS_PER_BLOCK):
            uu = _pairs_from_rows(rows, qq).astype(BF16)
            up = _dot(sh_ref[...], uu).astype(BF16)
            sr_ref[qq] = _dot(up, m2r_ref[qq])
            si_ref[qq] = _dot(up, m2i_ref[qq])
            const = lambda k, qq=qq: apw_ref[k, :, qq * ns:(qq + 1) * ns]

            def tile(k, c, qq=qq, const=const):
                r8 = pl.ds(pl.multiple_of(k * SUBLANE, SUBLANE), SUBLANE)
                xr, xi = _scan_tile(sr_ref[qq, r8, :], si_ref[qq, r8, :], const, c[0], c[1], False)
                sr_ref[qq, r8, :] = xr
                si_ref[qq, r8, :] = xi
                return _bcast_row(xr, SUBLANE - 1), _bcast_row(xi, SUBLANE - 1)

            zero = jnp.zeros((SUBLANE, ns), F32)
            lax.fori_loop(0, ntile, tile, (zero, zero))
            state = jnp.concatenate([sr_ref[qq].astype(BF16), si_ref[qq].astype(BF16)], axis=1)
            carried = _dot(state, m3_ref[qq])
            for h in range(PAIR):
                cols = slice(h * half, (h + 1) * half)
                ys_ref[qq, :, cols] = _dot(uu[:, cols], m1_ref[qq * PAIR + h]) + carried[:, cols]
        pairs = [ys_ref[qq] for qq in range(PAIRS_PER_BLOCK)]
        for tpos in range(SSD_L):
            y_ref[pl.ds(tpos, nc, stride=SSD_L), :] = _rows_from_pairs(pairs, tpos)

    blk3 = lambda a, n: pl.BlockSpec((n,) + a.shape[1:], lambda j: (j, 0, 0))
    return pl.pallas_call(
        body, name=name, grid=(nblk,),
        in_specs=[pl.BlockSpec((t, LANE), lambda j: (0, j)), pl.BlockSpec((nc, nc), lambda j: (0, 0)),
                  blk3(m1, PAIRS_PER_BLOCK * PAIR), blk3(m2r, PAIRS_PER_BLOCK), blk3(m2i, PAIRS_PER_BLOCK),
                  blk3(m3, PAIRS_PER_BLOCK), pl.BlockSpec((8, SUBLANE, PAIRS_PER_BLOCK * ns), lambda j: (0, 0, j))],
        out_specs=[pl.BlockSpec((t, LANE), lambda j: (0, j)),
                   pl.BlockSpec((PAIRS_PER_BLOCK, nc, ns), lambda j: (j, 0, 0)),
                   pl.BlockSpec((PAIRS_PER_BLOCK, nc, ns), lambda j: (j, 0, 0))],
        out_shape=[jax.ShapeDtypeStruct((t, d), F32), jax.ShapeDtypeStruct((nq, nc, ns), F32),
                   jax.ShapeDtypeStruct((nq, nc, ns), F32)],
        scratch_shapes=[pltpu.VMEM((PAIRS_PER_BLOCK, nc, PAIR * half), F32)], compiler_params=_cparams(),
    )(z, shift, m1, m2r, m2i, m3, apw)


def s5r_bwd(z, gy, shift, sr, si, m1, m2r, m2i, m3, apw_rev, name):
    t = z.shape[0]
    nc = t // SSD_L
    nq, _, ns = m2r.shape
    nblk = nq // PAIRS_PER_BLOCK
    d = nblk * LANE
    half = m1.shape[1]
    ntile = nc // SUBLANE

    def body(u_ref, gy_ref, sh_ref, sr_ref, si_ref, m1_ref, m2r_ref, m2i_ref, m3_ref, apw_ref,
             du_ref, dm1_ref, dm2r_ref, dm2i_ref, dm3_ref, dar_ref, dai_ref, lr_ref, li_ref, dus_ref):
        u_rows = _chunk_rows(u_ref, nc)
        gy_rows = _chunk_rows(gy_ref, nc)
        for qq in range(PAIRS_PER_BLOCK):
            uu = _pairs_from_rows(u_rows, qq).astype(BF16)
            dyy = _pairs_from_rows(gy_rows, qq).astype(BF16)
            dyn = _dot(sh_ref[...], dyy, "tn").astype(BF16)
            back = _dot(dyn, m3_ref[qq], "nt")
            lr_ref[...] = back[:, 0:ns]
            li_ref[...] = back[:, ns:2 * ns]
            const = lambda k, qq=qq: apw_ref[k, :, qq * ns:(qq + 1) * ns]

            def tile(k, c, qq=qq, const=const):
                r8 = pl.ds(pl.multiple_of((ntile - 1 - k) * SUBLANE, SUBLANE), SUBLANE)
                xr, xi = _scan_tile(lr_ref[r8, :], li_ref[r8, :], const, c[0], c[1], True)
                lr_ref[r8, :] = xr
                li_ref[r8, :] = xi
                s_r, s_i = sr_ref[qq, r8, :], si_ref[qq, r8, :]
                return (_bcast_row(xr, 0), _bcast_row(xi, 0), c[2] + xr * s_r + xi * s_i, c[3] + xi * s_r - xr * s_i)

            zero = jnp.zeros((SUBLANE, ns), F32)
            _, _, dar, dai = lax.fori_loop(0, ntile, tile, (zero, zero, zero, zero))
            dar_ref[qq] = dar
            dai_ref[qq] = dai
            lrb, lib = lr_ref[...].astype(BF16), li_ref[...].astype(BF16)
            from_state = _dot(lrb, m2r_ref[qq], "nt") + _dot(lib, m2i_ref[qq], "nt")
            for h in range(PAIR):
                cols = slice(h * half, (h + 1) * half)
                dus_ref[qq, :, cols] = _dot(dyy[:, cols], m1_ref[qq * PAIR + h], "nt") + from_state[:, cols]
                dm1_ref[qq * PAIR + h] = _dot(uu[:, cols], dyy[:, cols], "tn")
            dm2r_ref[qq] = _dot(uu, lrb, "tn")
            dm2i_ref[qq] = _dot(uu, lib, "tn")
            state = jnp.concatenate([sr_ref[qq].astype(BF16), si_ref[qq].astype(BF16)], axis=1)
            dm3_ref[qq] = _dot(state, dyy, "tn")
        pairs = [dus_ref[qq] for qq in range(PAIRS_PER_BLOCK)]
        for tpos in range(SSD_L):
            du_ref[pl.ds(tpos, nc, stride=SSD_L), :] = _rows_from_pairs(pairs, tpos)

    blk3 = lambda a, n: pl.BlockSpec((n,) + a.shape[1:], lambda j: (j, 0, 0))
    cols_j = pl.BlockSpec((t, LANE), lambda j: (0, j))
    states = pl.BlockSpec((PAIRS_PER_BLOCK, nc, ns), lambda j: (j, 0, 0))
    partial = pl.BlockSpec((PAIRS_PER_BLOCK, SUBLANE, ns), lambda j: (j, 0, 0))
    return pl.pallas_call(
        body, name=name, grid=(nblk,),
        in_specs=[cols_j, cols_j, pl.BlockSpec((nc, nc), lambda j: (0, 0)), states, states,
                  blk3(m1, PAIRS_PER_BLOCK * PAIR), blk3(m2r, PAIRS_PER_BLOCK), blk3(m2i, PAIRS_PER_BLOCK),
                  blk3(m3, PAIRS_PER_BLOCK), pl.BlockSpec((8, SUBLANE, PAIRS_PER_BLOCK * ns), lambda j: (0, 0, j))],
        out_specs=[cols_j, blk3(m1, PAIRS_PER_BLOCK * PAIR), blk3(m2r, PAIRS_PER_BLOCK), blk3(m2i, PAIRS_PER_BLOCK),
                   blk3(m3, PAIRS_PER_BLOCK), partial, partial],
        out_shape=[jax.ShapeDtypeStruct((t, d), F32), jax.ShapeDtypeStruct(m1.shape, F32),
                   jax.ShapeDtypeStruct(m2r.shape, F32), jax.ShapeDtypeStruct(m2i.shape, F32),
                   jax.ShapeDtypeStruct(m3.shape, F32), jax.ShapeDtypeStruct((nq, SUBLANE, ns), F32),
                   jax.ShapeDtypeStruct((nq, SUBLANE, ns), F32)],
        scratch_shapes=[pltpu.VMEM((nc, ns), F32), pltpu.VMEM((nc, ns), F32),
                        pltpu.VMEM((PAIRS_PER_BLOCK, nc, PAIR * half), F32)],
        compiler_params=_cparams(),
    )(z, gy, shift, sr, si, m1, m2r, m2i, m3, apw_rev)


def s5_skip_gelu(y_lin, z, dvec, name, tr=512):
    t, d = y_lin.shape
    tr = _row_tile(t, tr)

    def body(y_ref, u_ref, d_ref, yp_ref, yg_ref):
        y = y_ref[...] + d_ref[...] * u_ref[...]
        yp_ref[...] = y
        yg_ref[...] = _gelu(y).astype(BF16)

    sp = _rows(t, d, tr)
    return pl.pallas_call(
        body, name=name, grid=(t // tr,), in_specs=[sp, sp, _vec(d)], out_specs=[sp, sp],
        out_shape=[jax.ShapeDtypeStruct((t, d), F32), jax.ShapeDtypeStruct((t, d), BF16)],
        compiler_params=_cparams(),
    )(y_lin, z, dvec)


def s5_skip_gelu_bwd(dyg, y_pre, z, dvec, name, tr=512):
    t, d = dyg.shape
    tr = _row_tile(t, tr)

    def body(dyg_ref, yp_ref, u_ref, d_ref, gy_ref, skip_ref, gd_ref):
        gy = dyg_ref[...] * _gelu_grad(yp_ref[...])
        gy_ref[...] = gy
        skip_ref[...] = gy * d_ref[...]

        @pl.when(pl.program_id(0) == 0)
        def _():
            gd_ref[...] = jnp.zeros_like(gd_ref)

        gd_ref[...] += jnp.sum(gy * u_ref[...], axis=0, keepdims=True)

    sp = _rows(t, d, tr)
    return pl.pallas_call(
        body, name=name, grid=(t // tr,), in_specs=[sp, sp, sp, _vec(d)], out_specs=[sp, sp, _vec(d)],
        out_shape=[jax.ShapeDtypeStruct((t, d), F32), jax.ShapeDtypeStruct((t, d), F32),
                   jax.ShapeDtypeStruct((1, d), F32)],
        compiler_params=_cparams(),
    )(dyg, y_pre, z, dvec)


def _layer_norm(gv, nv):
    mu = jnp.mean(gv, axis=-1, keepdims=True)
    xc = gv - mu
    r = lax.rsqrt(jnp.mean(xc * xc, axis=-1, keepdims=True) + EPS)
    xhat = xc * r
    return xhat * nv, xhat, r


def gmlp_fwd(z, norm_v, wm, bs, name, tr=256):
    t = z.shape[0]
    nh = wm.shape[0]
    d = nh * GMLP_HEAD
    col0 = (z.shape[1] - 2 * d) // d
    tr = _row_tile(t, tr)

    def body(zu_ref, zv_ref, nv_ref, wm_ref, bs_ref, o_ref):
        v, _, _ = _layer_norm(_gelu(zv_ref[...]), nv_ref[...])
        vb = v.astype(BF16)
        u = _gelu(zu_ref[...])
        for c in range(tr // CHUNK):
            rows = slice(c * CHUNK, (c + 1) * CHUNK)
            for h in range(nh):
                cols = slice(h * GMLP_HEAD, (h + 1) * GMLP_HEAD)
                s = _dot(wm_ref[h], vb[rows, cols]) + bs_ref[h]
                o_ref[rows, cols] = u[rows, cols] * s

    return pl.pallas_call(
        body, name=name, grid=(t // tr,),
        in_specs=[pl.BlockSpec((tr, d), lambda i: (i, col0)), pl.BlockSpec((tr, d), lambda i: (i, col0 + 1)),
                  _vec(d), pl.BlockSpec(wm.shape, lambda i: (0, 0, 0)), pl.BlockSpec(bs.shape, lambda i: (0, 0, 0))],
        out_specs=pl.BlockSpec((tr, d), lambda i: (i, 0)),
        out_shape=jax.ShapeDtypeStruct((t, d), F32), compiler_params=_cparams(),
    )(z, z, norm_v, wm, bs)


def gmlp_bwd(dy, z, norm_v, wm, wmt, bs, name, tr=256):
    t = z.shape[0]
    nh = wm.shape[0]
    d = nh * GMLP_HEAD
    col0 = (z.shape[1] - 2 * d) // d
    tr = _row_tile(t, tr)

    def body(dy_ref, zu_ref, zv_ref, nv_ref, wm_ref, wmt_ref, bs_ref, dzu_ref, dzv_ref, dnv_ref, dwm_ref, dbs_ref,
             dv_ref):
        @pl.when(pl.program_id(0) == 0)
        def _():
            dnv_ref[...] = jnp.zeros_like(dnv_ref)
            dwm_ref[...] = jnp.zeros_like(dwm_ref)
            dbs_ref[...] = jnp.zeros_like(dbs_ref)

        zv = zv_ref[...]
        nv = nv_ref[...]
        v, xhat, r = _layer_norm(_gelu(zv), nv)
        vb = v.astype(BF16)
        zu = zu_ref[...]
        u = _gelu(zu)
        dy_ = dy_ref[...]
        for c in range(tr // CHUNK):
            rows = slice(c * CHUNK, (c + 1) * CHUNK)
            for h in range(nh):
                cols = slice(h * GMLP_HEAD, (h + 1) * GMLP_HEAD)
                vh = vb[rows, cols]
                s = _dot(wm_ref[h], vh) + bs_ref[h]
                dyh = dy_[rows, cols]
                dzu_ref[rows, cols] = (dyh * s * _gelu_grad(zu[rows, cols])).astype(BF16)
                ds = dyh * u[rows, cols]
                dsb = ds.astype(BF16)
                dbs_ref[h] += jnp.sum(ds, axis=1, keepdims=True)
                dwm_ref[h] += _dot(dsb, vh, "nt")
                dv_ref[rows, cols] = _dot(wmt_ref[h], dsb)
        dv = dv_ref[...]
        dnv_ref[...] += jnp.sum(dv * xhat, axis=0, keepdims=True)
        dxh = dv * nv
        dgv = r * (dxh - jnp.mean(dxh, axis=-1, keepdims=True) - xhat * jnp.mean(dxh * xhat, axis=-1, keepdims=True))
        dzv_ref[...] = (dgv * _gelu_grad(zv)).astype(BF16)

    full3 = lambda shp: pl.BlockSpec(shp, lambda i: (0, 0, 0))
    rows_d = pl.BlockSpec((tr, d), lambda i: (i, 0))
    return pl.pallas_call(
        body, name=name, grid=(t // tr,),
        in_specs=[rows_d, pl.BlockSpec((tr, d), lambda i: (i, col0)), pl.BlockSpec((tr, d), lambda i: (i, col0 + 1)),
                  _vec(d), full3(wm.shape), full3(wmt.shape), full3(bs.shape)],
        out_specs=[rows_d, rows_d, _vec(d), full3((nh, CHUNK, CHUNK)), full3((nh, CHUNK, 1))],
        out_shape=[jax.ShapeDtypeStruct((t, d), BF16), jax.ShapeDtypeStruct((t, d), BF16),
                   jax.ShapeDtypeStruct((1, d), F32), jax.ShapeDtypeStruct((nh, CHUNK, CHUNK), F32),
                   jax.ShapeDtypeStruct((nh, CHUNK, 1), F32)],
        scratch_shapes=[pltpu.VMEM((tr, d), F32)], compiler_params=_cparams(),
    )(dy, z, z, norm_v, wm, wmt, bs)


def _block(ref, axis, size, k):
    start = pl.multiple_of(k * size, size)
    if axis == 0:
        return ref.at[pl.ds(start, size), :]
    return ref.at[:, pl.ds(start, size)]


def _place():
    x, y, c = lax.axis_index("x"), lax.axis_index("y"), lax.axis_index("c")
    chips = [(1 - x, y), (x, 1 - y), (1 - x, 1 - y)]
    return x, y, c, chips


def _dev(x, y, c):
    return 4 * x + 2 * y + c


def gather_task(shards, axes):
    n = len(shards)
    sizes = [s.shape[ax] for s, ax in zip(shards, axes)]
    out_shape = [
        jax.ShapeDtypeStruct((s.shape[0] * N_DEV, s.shape[1]) if ax == 0 else (s.shape[0], s.shape[1] * N_DEV), s.dtype)
        for s, ax in zip(shards, axes)
    ]

    def copy(ins, outs, send_sems, recv_sems, t, k, block, to, from_input=False):
        dst = _block(outs[t], axes[t], sizes[t], _dev(*block))
        return pltpu.make_async_remote_copy(
            src_ref=ins[t] if from_input else dst, dst_ref=dst,
            send_sem=send_sems.at[t * 7 + k], recv_sem=recv_sems.at[t * 7 + k],
            device_id=to, device_id_type=MESH_DT)

    def local(ins, outs, local_sems, t, me):
        return pltpu.make_async_copy(ins[t], _block(outs[t], axes[t], sizes[t], _dev(*me)), local_sems.at[t])

    def start(ins, outs, send_sems, recv_sems, local_sems):
        x, y, c, chips = _place()
        me, sibling = (x, y, c), (x, y, 1 - c)
        for t in range(n):
            local(ins, outs, local_sems, t, me).start()
        for t in range(n):
            copy(ins, outs, send_sems, recv_sems, t, 0, me, sibling, True).start()
            for j, chip in enumerate(chips):
                copy(ins, outs, send_sems, recv_sems, t, 1 + j, me, (*chip, c), True).start()

    def late(ins, outs, send_sems, recv_sems, local_sems):
        x, y, c, chips = _place()
        me, sibling = (x, y, c), (x, y, 1 - c)
        for t in range(n):
            for j, chip in enumerate(chips):
                copy(ins, outs, send_sems, recv_sems, t, 1 + j, (*chip, c), me).wait_recv()
                copy(ins, outs, send_sems, recv_sems, t, 4 + j, (*chip, c), sibling).start()

    def finish(ins, outs, send_sems, recv_sems, local_sems):
        x, y, c, chips = _place()
        me, sibling = (x, y, c), (x, y, 1 - c)
        for t in range(n):
            copy(ins, outs, send_sems, recv_sems, t, 0, sibling, me).wait_recv()
            for j, chip in enumerate(chips):
                copy(ins, outs, send_sems, recv_sems, t, 4 + j, (*chip, 1 - c), me).wait_recv()
        for t in range(n):
            copy(ins, outs, send_sems, recv_sems, t, 0, me, sibling, True).wait_send()
            for j, chip in enumerate(chips):
                copy(ins, outs, send_sems, recv_sems, t, 1 + j, me, (*chip, c), True).wait_send()
                copy(ins, outs, send_sems, recv_sems, t, 4 + j, (*chip, c), sibling).wait_send()
            local(ins, outs, local_sems, t, me).wait()

    return CommTask(shards, out_shape, (7 * n, 7 * n, n), start, late, finish)


def _blk3(shape2, axis):
    r, c = shape2
    return (r // N_DEV, c) if axis == 0 else (r, c // N_DEV)


def _no_late(ins, outs, send_sems, recv_sems, local_sems):
    pass


def to_sibling_task(grads, axes):
    n = len(grads)
    blks = [_blk3(g.shape, ax) for g, ax in zip(grads, axes)]
    sizes = [b[ax] for b, ax in zip(blks, axes)]

    def copies(ins, outs, send_sems, recv_sems):
        x, y, c, _ = _place()
        return [pltpu.make_async_remote_copy(
            src_ref=_block(ins[t], axes[t], sizes[t], 2 * i + (1 - c)), dst_ref=outs[t].at[i],
            send_sem=send_sems.at[t * N_CHIP + i], recv_sem=recv_sems.at[t * N_CHIP + i],
            device_id=(x, y, 1 - c), device_id_type=MESH_DT) for t in range(n) for i in range(N_CHIP)]

    def start(ins, outs, send_sems, recv_sems, local_sems):
        for cp in copies(ins, outs, send_sems, recv_sems):
            cp.start()

    def finish(ins, outs, send_sems, recv_sems, local_sems):
        cps = copies(ins, outs, send_sems, recv_sems)
        for cp in cps:
            cp.wait_recv()
        for cp in cps:
            cp.wait_send()

    out_shape = [jax.ShapeDtypeStruct((N_CHIP,) + b, g.dtype) for b, g in zip(blks, grads)]
    return CommTask(grads, out_shape, (N_CHIP * n, N_CHIP * n, 1), start, _no_late, finish)


def across_chips_task(parts):
    n = len(parts)

    def copies(ins, outs, send_sems, recv_sems):
        x, y, c, chips = _place()
        my_chip = 2 * x + y
        return [pltpu.make_async_remote_copy(
            src_ref=ins[t].at[2 * chip[0] + chip[1]], dst_ref=outs[t].at[my_chip],
            send_sem=send_sems.at[t * 3 + j], recv_sem=recv_sems.at[t * 3 + j],
            device_id=(*chip, c), device_id_type=MESH_DT) for t in range(n) for j, chip in enumerate(chips)]

    def mine(ins, outs, local_sems):
        x, y, _, _ = _place()
        my_chip = 2 * x + y
        return [pltpu.make_async_copy(ins[t].at[my_chip], outs[t].at[my_chip], local_sems.at[t]) for t in range(n)]

    def start(ins, outs, send_sems, recv_sems, local_sems):
        for cp in mine(ins, outs, local_sems):
            cp.start()
        for cp in copies(ins, outs, send_sems, recv_sems):
            cp.start()

    def finish(ins, outs, send_sems, recv_sems, local_sems):
        cps = copies(ins, outs, send_sems, recv_sems)
        for cp in cps:
            cp.wait_recv()
        for cp in cps:
            cp.wait_send()
        for cp in mine(ins, outs, local_sems):
            cp.wait()

    out_shape = [jax.ShapeDtypeStruct(p.shape, p.dtype) for p in parts]
    return CommTask(parts, out_shape, (3 * n, 3 * n, n), start, _no_late, finish)


def run_tasks(tasks, name):
    t_in = [len(t.inputs) for t in tasks]
    t_out = [len(t.out_shape) for t in tasks]

    def body(*refs):
        pos, views = 0, []
        for k in t_in:
            views.append([refs[pos:pos + k]])
            pos += k
        for v, k in zip(views, t_out):
            v.append(refs[pos:pos + k])
            pos += k
        for i, v in enumerate(views):
            v.extend(refs[pos + 3 * i:pos + 3 * i + 3])
        for phase in ("start", "late", "finish"):
            for t, v in zip(tasks, views):
                getattr(t, phase)(*v)

    any_spec = pl.BlockSpec(memory_space=pl.ANY)
    res = pl.pallas_call(
        body, name=name, in_specs=[any_spec] * sum(t_in), out_specs=[any_spec] * sum(t_out),
        out_shape=[s for t in tasks for s in t.out_shape], input_output_aliases=_task_aliases(tasks, 0, 0),
        scratch_shapes=[pltpu.SemaphoreType.DMA((k,)) for t in tasks for k in t.n_sems],
    )(*[a for t in tasks for a in t.inputs])
    res, out, pos = list(res), [], 0
    for k in t_out:
        out.append(res[pos:pos + k])
        pos += k
    return out


_HBM_SPEC = pl.BlockSpec(memory_space=pl.ANY)
_SEM_SPEC = pl.BlockSpec(memory_space=pltpu.SEMAPHORE)
_DATAFLOW = pltpu.SideEffectType.DATAFLOW_SIDE_EFFECTING


def _full_shape(s, ax):
    return (s.shape[0] * N_DEV, s.shape[1]) if ax == 0 else (s.shape[0], s.shape[1] * N_DEV)


def _level1_copy(src, landing, axis, size, send_sems, recv_sems, slot, sender, to):
    dst = _block(landing, axis, size, _dev(*sender))
    return pltpu.make_async_remote_copy(src_ref=src, dst_ref=dst, send_sem=send_sems.at[slot],
                                        recv_sem=recv_sems.at[slot], device_id=to, device_id_type=MESH_DT)


def place_own_block(shard, axis, me, name, tr=256):
    r, c = shard.shape
    tr = _row_tile(r, tr)
    nrb = r // tr
    if axis == 0:
        o_map = lambda i, me_ref: (me_ref[0] * nrb + i, 0)
    else:
        o_map = lambda i, me_ref: (i, me_ref[0])

    def body(me_ref, x_ref, o_ref):
        o_ref[...] = x_ref[...]

    return pl.pallas_call(
        body, name=name,
        grid_spec=pltpu.PrefetchScalarGridSpec(
            num_scalar_prefetch=1, grid=(nrb,), in_specs=[pl.BlockSpec((tr, c), lambda i, me_ref: (i, 0))],
            out_specs=pl.BlockSpec((tr, c), o_map)),
        out_shape=jax.ShapeDtypeStruct(_full_shape(shard, axis), shard.dtype), compiler_params=_cparams(),
    )(me, shard)


def gather_start(landing, axes, sizes, groups, name):
    n = len(landing)

    def body(*refs):
        lands, sems = refs[:n], refs[2 * n:]
        x, y, c, chips = _place()
        me = (x, y, c)
        targets = [(x, y, 1 - c)] + [(*chip, c) for chip in chips]
        for g, members in enumerate(groups):
            for m, t in enumerate(members):
                own = _block(lands[t], axes[t], sizes[t], _dev(*me))
                for k, to in enumerate(targets):
                    _level1_copy(own, lands[t], axes[t], sizes[t], sems[2 * g], sems[2 * g + 1], 4 * m + k,
                                 me, to).start()

    out = pl.pallas_call(
        body, name=name,
        out_shape=[jax.ShapeDtypeStruct(b.shape, b.dtype) for b in landing]
        + [pltpu.SemaphoreType.DMA((4 * len(members),)) for members in groups for _ in (0, 1)],
        in_specs=[_HBM_SPEC] * n, out_specs=[_HBM_SPEC] * n + [_SEM_SPEC] * (2 * len(groups)),
        input_output_aliases={i: i for i in range(n)},
        compiler_params=pltpu.CompilerParams(has_side_effects=_DATAFLOW),
    )(*landing)
    out = list(out)
    sems = out[n:]
    return out[:n], [(sems[2 * g], sems[2 * g + 1]) for g in range(len(groups))]


def gather_wait(landing, axes, sizes, send_sems, recv_sems, after, name):
    n = len(landing)

    def body(*refs):
        lands = refs[:n]
        send, recv = refs[n], refs[n + 1]
        x, y, c, chips = _place()
        me = (x, y, c)
        peers = [(x, y, 1 - c)] + [(*chip, c) for chip in chips]
        for t in range(n):
            own = _block(lands[t], axes[t], sizes[t], _dev(*me))
            for k, peer in enumerate(peers):
                _level1_copy(own, lands[t], axes[t], sizes[t], send, recv, 4 * t + k, me, peer).wait_send()
                _level1_copy(own, lands[t], axes[t], sizes[t], send, recv, 4 * t + k, peer, me).wait_recv()

    out = pl.pallas_call(
        body, name=name, out_shape=[jax.ShapeDtypeStruct(b.shape, b.dtype) for b in landing],
        in_specs=[_HBM_SPEC] * n + [_SEM_SPEC, _SEM_SPEC, pl.BlockSpec(memory_space=pl.ANY)],
        out_specs=[_HBM_SPEC] * n, input_output_aliases={i: i for i in range(n)},
        compiler_params=pltpu.CompilerParams(has_side_effects=_DATAFLOW),
    )(*landing, send_sems, recv_sems, after)
    return list(out)


def forward_task(landing, axes, sizes):
    n = len(landing)

    def forward(lands, send_sems, recv_sems, t, j, chip_core):
        x, y, c, _ = _place()
        blk = _block(lands[t], axes[t], sizes[t], _dev(*chip_core))
        return pltpu.make_async_remote_copy(src_ref=blk, dst_ref=blk, send_sem=send_sems.at[3 * t + j],
                                            recv_sem=recv_sems.at[3 * t + j], device_id=(x, y, 1 - c),
                                            device_id_type=MESH_DT)

    def start(ins, lands, send_sems, recv_sems, local_sems):
        _, _, c, chips = _place()
        for t in range(n):
            for j, chip in enumerate(chips):
                forward(lands, send_sems, recv_sems, t, j, (*chip, c)).start()

    def finish(ins, lands, send_sems, recv_sems, local_sems):
        _, _, c, chips = _place()
        for t in range(n):
            for j, chip in enumerate(chips):
                forward(lands, send_sems, recv_sems, t, j, (*chip, 1 - c)).wait_recv()
        for t in range(n):
            for j, chip in enumerate(chips):
                forward(lands, send_sems, recv_sems, t, j, (*chip, c)).wait_send()

    out_shape = [jax.ShapeDtypeStruct(b.shape, b.dtype) for b in landing]
    return CommTask(landing, out_shape, (3 * n, 3 * n, 1), start, _no_late, finish, in_place=True)


def rs_chip_sum(grad, recv, axis, core, name, tr=512):
    br, bc = _blk3(grad.shape, axis)
    tr = _row_tile(br, tr)
    nrb = br // tr

    if axis == 0:
        g_map = lambda i, r, c_ref: ((2 * i + c_ref[0]) * nrb + r, 0)
    else:
        g_map = lambda i, r, c_ref: (r, 2 * i + c_ref[0])

    def body(c_ref, g_ref, r_ref, o_ref):
        o_ref[...] = (g_ref[...].astype(F32) + r_ref[...].astype(F32)).astype(BF16)

    return pl.pallas_call(
        body, name=name,
        grid_spec=pltpu.PrefetchScalarGridSpec(
            num_scalar_prefetch=1, grid=(N_CHIP, nrb),
            in_specs=[pl.BlockSpec((tr, bc), g_map), pl.BlockSpec((None, tr, bc), lambda i, r, c_ref: (i, r, 0))],
            out_specs=pl.BlockSpec((None, tr, bc), lambda i, r, c_ref: (i, r, 0))),
        out_shape=jax.ShapeDtypeStruct((N_CHIP, br, bc), BF16), compiler_params=_cparams(),
    )(core, grad, recv)


def _adamw(w, g, m, v):
    m = ADAM_B1 * m + (1.0 - ADAM_B1) * g
    v = ADAM_B2 * v + (1.0 - ADAM_B2) * (g * g)
    m_hat = m / (1.0 - ADAM_B1 ** ADAM_STEP)
    v_hat = v / (1.0 - ADAM_B2 ** ADAM_STEP)
    delta = -ADAM_LR * (m_hat / (jnp.sqrt(v_hat) + ADAM_EPS) + ADAM_WD * w)
    return delta, m, v


def _sum_chips(p_ref):
    g = p_ref[0].astype(F32)
    for i in range(1, N_CHIP):
        g = g + p_ref[i].astype(F32)
    return g


def _chip_copy(part, landing, send_sems, recv_sems, slot, to_chip, from_chip, to):
    return pltpu.make_async_remote_copy(src_ref=part.at[to_chip], dst_ref=landing.at[from_chip],
                                        send_sem=send_sems.at[slot], recv_sem=recv_sems.at[slot],
                                        device_id=to, device_id_type=MESH_DT)


def scatter_start(parts, name):
    n = len(parts)

    def body(*refs):
        srcs, lands = refs[:n], refs[n:2 * n]
        send_sems, recv_sems = refs[4 * n], refs[4 * n + 1]
        x, y, c, chips = _place()
        my_chip = 2 * x + y
        for t in range(n):
            for j, chip in enumerate(chips):
                _chip_copy(srcs[t], lands[t], send_sems, recv_sems, 3 * t + j, 2 * chip[0] + chip[1], my_chip,
                           (*chip, c)).start()

    landing = [lax.empty(p.shape, p.dtype) for p in parts]
    out = pl.pallas_call(
        body, name=name,
        out_shape=[jax.ShapeDtypeStruct(p.shape, p.dtype) for p in parts + landing]
        + [pltpu.SemaphoreType.DMA((3 * n,)), pltpu.SemaphoreType.DMA((3 * n,))],
        in_specs=[_HBM_SPEC] * (2 * n), out_specs=[_HBM_SPEC] * (2 * n) + [_SEM_SPEC, _SEM_SPEC],
        input_output_aliases={i: i for i in range(2 * n)},
        compiler_params=pltpu.CompilerParams(has_side_effects=_DATAFLOW),
    )(*parts, *landing)
    out = list(out)
    return out[:n], out[n:2 * n], out[2 * n], out[2 * n + 1]


def scatter_wait(started, after, name):
    sizes = [len(s[0]) for s in started]
    n_all = sum(sizes)

    def body(*refs):
        x, y, c, chips = _place()
        my_chip = 2 * x + y
        pos, sem_pos = 0, 2 * n_all
        for n in sizes:
            srcs, lands = refs[pos:pos + n], refs[n_all + pos:n_all + pos + n]
            send_sems, recv_sems = refs[sem_pos], refs[sem_pos + 1]
            for t in range(n):
                for j, chip in enumerate(chips):
                    other = 2 * chip[0] + chip[1]
                    _chip_copy(srcs[t], lands[t], send_sems, recv_sems, 3 * t + j, other, my_chip, (*chip, c)).wait_send()
                    _chip_copy(srcs[t], lands[t], send_sems, recv_sems, 3 * t + j, my_chip, other, (*chip, c)).wait_recv()
            pos += n
            sem_pos += 2

    parts = [p for s in started for p in s[0]]
    landing = [b for s in started for b in s[1]]
    sems = [q for s in started for q in (s[2], s[3])]
    out = pl.pallas_call(
        body, name=name, out_shape=[jax.ShapeDtypeStruct(a.shape, a.dtype) for a in parts + landing],
        in_specs=[_HBM_SPEC] * (2 * n_all) + [_SEM_SPEC] * len(sems) + [pl.BlockSpec(memory_space=pl.ANY)],
        out_specs=[_HBM_SPEC] * (2 * n_all), input_output_aliases={i: i for i in range(2 * n_all)},
        compiler_params=pltpu.CompilerParams(has_side_effects=_DATAFLOW),
    )(*parts, *landing, *sems, after)
    out = list(out)
    res, pos = [], 0
    for n in sizes:
        res.append((out[pos:pos + n], out[n_all + pos:n_all + pos + n]))
        pos += n
    return res


def adam_sharded(parts, w, m, v, name, tr=256):
    r, c = w.shape
    assert parts.shape[2] == c
    tr = _row_tile(r, tr)

    def body(p_ref, w_ref, m_ref, v_ref, g_ref, d_ref, nm_ref, nv_ref):
        g = _sum_chips(p_ref)
        delta, nm, nv = _adamw(w_ref[...], g, m_ref[...], v_ref[...])
        g_ref[...] = g
        d_ref[...] = delta
        nm_ref[...] = nm
        nv_ref[...] = nv

    sp = pl.BlockSpec((tr, c), lambda i: (i, 0))
    return pl.pallas_call(
        body, name=name, grid=(r // tr,),
        in_specs=[pl.BlockSpec((N_CHIP, tr, c), lambda i: (0, i, 0)), sp, sp, sp],
        out_specs=[sp, sp, sp, sp], out_shape=[jax.ShapeDtypeStruct((r, c), F32)] * 4,
        compiler_params=_cparams(),
    )(parts, w, m, v)


def adam_small(g, w, m, v, name):
    def body(g_ref, w_ref, m_ref, v_ref, d_ref, nm_ref, nv_ref):
        delta, nm, nv = _adamw(w_ref[...], g_ref[...], m_ref[...], v_ref[...])
        d_ref[...] = delta
        nm_ref[...] = nm
        nv_ref[...] = nv

    return pl.pallas_call(
        body, name=name, out_shape=[jax.ShapeDtypeStruct(w.shape, F32)] * 3, compiler_params=_cparams(),
    )(g, w, m, v)


def sum_devices(gathered, name, tr=512):
    _, r, c = gathered.shape
    tr = _row_tile(r, tr)

    def body(x_ref, o_ref):
        s = x_ref[0]
        for k in range(1, N_DEV):
            s = s + x_ref[k]
        o_ref[...] = s

    return pl.pallas_call(
        body, name=name, grid=(r // tr,), in_specs=[pl.BlockSpec((N_DEV, tr, c), lambda i: (0, i, 0))],
        out_specs=pl.BlockSpec((tr, c), lambda i: (i, 0)), out_shape=jax.ShapeDtypeStruct((r, c), F32),
        compiler_params=_cparams(),
    )(gathered)


def _pad_to(a, axis, mult):
    size = a.shape[axis]
    pad = (-size) % mult
    if pad == 0:
        return a
    cfg = [(0, 0)] * a.ndim
    cfg[axis] = (0, pad)
    return jnp.pad(a, cfg)


def _as2d(a):
    if a.ndim == 1:
        return a.reshape(1, -1)
    return a.reshape(-1, a.shape[-1])


def kernel(x, p, norm_ffn1, w1_gate, w1_up, w1_down, norm_mix, w_in, ssm_log_dt, ssm_a_re, ssm_a_im, ssm_b_re, ssm_b_im, ssm_c_re, ssm_c_im, ssm_d, ssm_w_glu, gmlp_norm_v, gmlp_w_s, gmlp_b_s, norm_ssm_out, norm_gmlp_out, w_out, norm_ffn2, w2_gate, w2_up, w2_down, norm_ple, w_ple_gate, w_ple_proj, norm_final, loss_target, m_norm_ffn1, m_w1_gate, m_w1_up, m_w1_down, m_norm_mix, m_w_in, m_ssm_log_dt, m_ssm_a_re, m_ssm_a_im, m_ssm_b_re, m_ssm_b_im, m_ssm_c_re, m_ssm_c_im, m_ssm_d, m_ssm_w_glu, m_gmlp_norm_v, m_gmlp_w_s, m_gmlp_b_s, m_norm_ssm_out, m_norm_gmlp_out, m_w_out, m_norm_ffn2, m_w2_gate, m_w2_up, m_w2_down, m_norm_ple, m_w_ple_gate, m_w_ple_proj, m_norm_final, v_norm_ffn1, v_w1_gate, v_w1_up, v_w1_down, v_norm_mix, v_w_in, v_ssm_log_dt, v_ssm_a_re, v_ssm_a_im, v_ssm_b_re, v_ssm_b_im, v_ssm_c_re, v_ssm_c_im, v_ssm_d, v_ssm_w_glu, v_gmlp_norm_v, v_gmlp_w_s, v_gmlp_b_s, v_norm_ssm_out, v_norm_gmlp_out, v_w_out, v_norm_ffn2, v_w2_gate, v_w2_up, v_w2_down, v_norm_ple, v_w_ple_gate, v_w_ple_proj, v_norm_final):
    weights = dict(
        norm_ffn1=norm_ffn1, w1_gate=w1_gate, w1_up=w1_up, w1_down=w1_down, norm_mix=norm_mix, w_in=w_in,
        ssm_log_dt=ssm_log_dt, ssm_a_re=ssm_a_re, ssm_a_im=ssm_a_im, ssm_b_re=ssm_b_re, ssm_b_im=ssm_b_im,
        ssm_c_re=ssm_c_re, ssm_c_im=ssm_c_im, ssm_d=ssm_d, ssm_w_glu=ssm_w_glu, gmlp_norm_v=gmlp_norm_v,
        gmlp_w_s=gmlp_w_s, gmlp_b_s=gmlp_b_s, norm_ssm_out=norm_ssm_out, norm_gmlp_out=norm_gmlp_out, w_out=w_out,
        norm_ffn2=norm_ffn2, w2_gate=w2_gate, w2_up=w2_up, w2_down=w2_down, norm_ple=norm_ple,
        w_ple_gate=w_ple_gate, w_ple_proj=w_ple_proj, norm_final=norm_final)
    moments_m = dict(
        norm_ffn1=m_norm_ffn1, w1_gate=m_w1_gate, w1_up=m_w1_up, w1_down=m_w1_down, norm_mix=m_norm_mix, w_in=m_w_in,
        ssm_log_dt=m_ssm_log_dt, ssm_a_re=m_ssm_a_re, ssm_a_im=m_ssm_a_im, ssm_b_re=m_ssm_b_re, ssm_b_im=m_ssm_b_im,
        ssm_c_re=m_ssm_c_re, ssm_c_im=m_ssm_c_im, ssm_d=m_ssm_d, ssm_w_glu=m_ssm_w_glu, gmlp_norm_v=m_gmlp_norm_v,
        gmlp_w_s=m_gmlp_w_s, gmlp_b_s=m_gmlp_b_s, norm_ssm_out=m_norm_ssm_out, norm_gmlp_out=m_norm_gmlp_out,
        w_out=m_w_out, norm_ffn2=m_norm_ffn2, w2_gate=m_w2_gate, w2_up=m_w2_up, w2_down=m_w2_down,
        norm_ple=m_norm_ple, w_ple_gate=m_w_ple_gate, w_ple_proj=m_w_ple_proj, norm_final=m_norm_final)
    moments_v = dict(
        norm_ffn1=v_norm_ffn1, w1_gate=v_w1_gate, w1_up=v_w1_up, w1_down=v_w1_down, norm_mix=v_norm_mix, w_in=v_w_in,
        ssm_log_dt=v_ssm_log_dt, ssm_a_re=v_ssm_a_re, ssm_a_im=v_ssm_a_im, ssm_b_re=v_ssm_b_re, ssm_b_im=v_ssm_b_im,
        ssm_c_re=v_ssm_c_re, ssm_c_im=v_ssm_c_im, ssm_d=v_ssm_d, ssm_w_glu=v_ssm_w_glu, gmlp_norm_v=v_gmlp_norm_v,
        gmlp_w_s=v_gmlp_w_s, gmlp_b_s=v_gmlp_b_s, norm_ssm_out=v_norm_ssm_out, norm_gmlp_out=v_norm_gmlp_out,
        w_out=v_w_out, norm_ffn2=v_norm_ffn2, w2_gate=v_w2_gate, w2_up=v_w2_up, w2_down=v_w2_down,
        norm_ple=v_norm_ple, w_ple_gate=v_w_ple_gate, w_ple_proj=v_w_ple_proj, norm_final=v_norm_final)
    names = list(weights)

    xs = x[0]
    ps = p[0, 0].astype(BF16)
    tgt = loss_target[0]
    d_model = xs.shape[1]
    d_ssm = d_model // 2
    n_groups = d_ssm // SSM_GROUP

    transposed = ("w1_gate", "w1_up", "w2_gate", "w2_up")
    big = {
        "w1_gate": 0, "w1_up": 0, "w1_down": 0, "w_in": 1, "ssm_w_glu": 0, "w_out": 0,
        "w2_gate": 0, "w2_up": 0, "w2_down": 0, "w_ple_gate": 0, "w_ple_proj": 1}
    big_names = list(big)

    def view(a, k):
        return a[0].T if k in transposed else a[0]

    def unview(a, k):
        return a.T[None] if k in transposed else a[None]

    shard = {k: _pad_to(view(weights[k], k).astype(BF16), big[k], LANE) for k in big_names}
    W = {}

    abar_r, abar_i, bbar_r, bbar_i = _ssm_discretize(ssm_log_dt[0], ssm_a_re[0], ssm_a_im[0], ssm_b_re[0], ssm_b_im[0])
    bc_r = _block_diag(jnp.swapaxes(bbar_r, 1, 2)).astype(BF16)
    bc_i = _block_diag(jnp.swapaxes(bbar_i, 1, 2)).astype(BF16)
    cc_r = _block_diag(jnp.swapaxes(ssm_c_re[0], 1, 2)).astype(BF16)
    cc_i = _block_diag(jnp.swapaxes(ssm_c_im[0], 1, 2)).astype(BF16)
    apw_f = _scan_constants(abar_r, abar_i, False)
    apw_b = _scan_constants(abar_r, abar_i, True)
    causal = jnp.tril(jnp.ones((CHUNK, CHUNK), dtype=bool))
    wm = jnp.where(causal[None], gmlp_w_s[0], 0.0).astype(BF16)
    wmt = jnp.swapaxes(wm, 1, 2)
    bs = gmlp_b_s[0][:, :, None]

    groups = [["w1_gate"], ["w1_up"], ["w1_down"], ["w_in", "ssm_w_glu", "w_out"], ["w2_gate"], ["w2_up"],
              ["w2_down", "w_ple_gate", "w_ple_proj"]]
    order = [k for g in groups for k in g]
    place = {k: i for i, k in enumerate(order)}
    me = (4 * lax.axis_index("x") + 2 * lax.axis_index("y") + lax.axis_index("c")).astype(jnp.int32).reshape(1)
    size = {k: shard[k].shape[big[k]] for k in order}
    landing, sems = gather_start([place_own_block(shard[k], big[k], me, "place_" + k) for k in order],
                                 [big[k] for k in order], [size[k] for k in order],
                                 [[place[k] for k in g] for g in groups], "gather_start")

    def landed(g, after):
        axes_g, sizes_g = [big[k] for k in groups[g]], [size[k] for k in groups[g]]
        bufs = gather_wait([landing[place[k]] for k in groups[g]], axes_g, sizes_g, *sems[g], after,
                           "gather_wait_%d" % g)
        return forward_task(bufs, axes_g, sizes_g)

    def arrive(g, after):
        W.update(zip(groups[g], run_tasks([landed(g, after)], "gather_forward_%d" % g)[0]))

    def arrive_during(g, after, fn, *a, **kw):
        out, (got,) = fn(*a, tasks=[landed(g, after)], **kw)
        W.update(zip(groups[g], got))
        return out

    xn1 = rmsnorm_fwd(xs, norm_ffn1, "norm_ffn1")
    arrive(0, xn1)
    gate1 = matmul(xn1, W["w1_gate"], "nt", "ffn1_gate")
    arrive(1, gate1)
    gate1, up1, act1 = ffn_up(xn1, W["w1_up"], gate1, "ffn1_up")
    arrive(2, act1)
    h1 = matmul(act1, W["w1_down"], "nn", "ffn1_down", res=xs, scale=0.5)
    arrive(3, h1)
    xn2 = rmsnorm_fwd(h1, norm_mix, "norm_mix")
    z = matmul(xn2, W["w_in"], "nn", "proj_in")
    y_pre, yg, sr, si = s5_fwd(z, bc_r, bc_i, cc_r, cc_i, apw_f, ssm_d, "s5_fwd")
    glin = matmul(yg, W["ssm_w_glu"], "nn", "ssm_glu")
    y_gmlp = gmlp_fwd(z, gmlp_norm_v, wm, bs, "gmlp_fwd")
    ycat = mix_out_fwd(y_pre, glin, y_gmlp, norm_ssm_out, norm_gmlp_out, "mix_out")
    h2 = arrive_during(4, ycat, matmul, ycat, W["w_out"], "nn", "proj_out", res=h1)
    xn3 = rmsnorm_fwd(h2, norm_ffn2, "norm_ffn2")
    gate2 = arrive_during(5, xn3, matmul, xn3, W["w2_gate"], "nt", "ffn2_gate")
    gate2, up2, act2 = arrive_during(6, gate2, ffn_up, xn3, W["w2_up"], gate2, "ffn2_up")
    h3 = matmul(act2, W["w2_down"], "nn", "ffn2_down", res=h2, scale=0.5)
    xn4 = rmsnorm_fwd(h3, norm_ple, "norm_ple")
    pg_lin = matmul(xn4, W["w_ple_gate"], "nn", "ple_gate")
    pp = matmul(ps, W["w_ple_proj"], "nn", "ple_proj")
    h4 = ple_fwd(h3, pg_lin, pp, "ple_fwd")
    dh4, loss_part, g_norm_final = final_loss(h4, tgt, norm_final.reshape(1, -1), "final_loss")
    loss = lax.psum(loss_part[0, 0], ("x", "y", "c"))

    G = {}
    reduced = {}
    chip_part = {}
    wait_sibling, wait_chips = [], []
    core = lax.axis_index("c").astype(jnp.int32).reshape(1)

    def grad(name_, value):
        G[name_] = value
        wait_sibling.append(name_)

    def carry(fn, *a, levels="ab", extra=None, **kw):
        tasks, kinds = [], []
        if extra is not None:
            tasks.append(extra[0])
            kinds.append(("x", extra[1]))
        if "a" in levels and wait_sibling:
            group = list(wait_sibling)
            wait_sibling.clear()
            tasks.append(to_sibling_task([G[k] for k in group], [big[k] for k in group]))
            kinds.append(("a", group))
        if "b" in levels and wait_chips:
            group = list(wait_chips)
            wait_chips.clear()
            tasks.append(across_chips_task([chip_part[k] for k in group]))
            kinds.append(("b", group))
        if not tasks:
            return fn(*a, **kw)
        out, task_outs = fn(*a, tasks=tasks, **kw)
        for (kind, group), outs in zip(kinds, task_outs):
            if kind == "x":
                group(outs)
                continue
            for k, r in zip(group, outs):
                if kind == "a":
                    chip_part[k] = rs_chip_sum(G[k], r, big[k], core, "rs_sum_" + k)
                    wait_chips.append(k)
                else:
                    reduced[k] = r
        return out

    small = {}
    small["norm_final"] = g_norm_final
    dpp, dpg = ple_bwd(dh4, pg_lin, pp, "ple_bwd")
    grad("w_ple_proj", matmul(ps, dpp, "tn", "grad_ple_proj", out_dtype=BF16))
    grad("w_ple_gate", carry(matmul, xn4, dpg, "tn", "grad_ple_gate", out_dtype=BF16))
    dxn4 = carry(matmul, dpg, W["w_ple_gate"], "nt", "ple_gate_bwd")
    dh3, dh3b, small["norm_ple"] = rmsnorm_bwd(dxn4, h3, norm_ple, dh4, "norm_ple_bwd")

    def ffn_bwd(tag, dhb, xn, gate, up, act, wg, wu, wd, extra=None, last_levels="ab"):
        dgate, dup = carry(ffn_bwd_act, dhb, W[wd], gate, up, tag + "_act_bwd", extra=extra)
        grad(wd, carry(matmul, act, dhb, "tn", tag + "_grad_down", out_dtype=BF16, scale=0.5))
        grad(wg, carry(matmul, dgate, xn, "tn", tag + "_grad_gate", out_dtype=BF16))
        grad(wu, carry(matmul, dup, xn, "tn", tag + "_grad_up", out_dtype=BF16))
        dxn = carry(matmul, dgate, W[wg], "nn", tag + "_gate_bwd")
        return carry(matmul, dup, W[wu], "nn", tag + "_up_bwd", res=dxn, levels=last_levels)

    dxn3 = ffn_bwd("ffn2", dh3b, xn3, gate2, up2, act2, "w2_gate", "w2_up", "w2_down", last_levels="a")
    dh2, dh2b, small["norm_ffn2"] = rmsnorm_bwd(dxn3, h2, norm_ffn2, dh3, "norm_ffn2_bwd")

    grad("w_out", matmul(ycat, dh2b, "tn", "grad_out", out_dtype=BF16))
    dycat = carry(matmul, dh2b, W["w_out"], "nt", "proj_out_bwd", levels="a")
    dyg_direct, dglin, dy_gmlp, small["norm_ssm_out"], small["norm_gmlp_out"] = mix_out_bwd(
        dycat, y_pre, glin, y_gmlp, norm_ssm_out, norm_gmlp_out, "mix_out_bwd")
    grad("ssm_w_glu", matmul(yg, dglin, "tn", "grad_glu", out_dtype=BF16))
    dyg = carry(matmul, dglin, W["ssm_w_glu"], "nt", "ssm_glu_bwd", res=dyg_direct, levels="a")
    du, small["ssm_d"], gc_r, gc_i, gb_r, gb_i, ga_r, ga_i = carry(
        s5_bwd, dyg, y_pre, z, sr, si, bc_r, bc_i, cc_r, cc_i, apw_b, ssm_d, "s5_bwd")
    dzu, dzv, small["gmlp_norm_v"], g_wm, g_bs = gmlp_bwd(dy_gmlp, z, gmlp_norm_v, wm, wmt, bs, "gmlp_bwd")
    small["gmlp_w_s"] = g_wm
    small["gmlp_b_s"] = g_bs
    small["c_re"] = _block_diag_extract(gc_r, SSM_GROUP, SSM_STATE)
    small["c_im"] = _block_diag_extract(gc_i, SSM_GROUP, SSM_STATE)
    small["bbar_r"] = jnp.swapaxes(_block_diag_extract(gb_r, SSM_GROUP, SSM_STATE), 1, 2)
    small["bbar_i"] = jnp.swapaxes(_block_diag_extract(gb_i, SSM_GROUP, SSM_STATE), 1, 2)
    small["abar_r"] = jnp.sum(ga_r, axis=0).reshape(n_groups, SSM_STATE)
    small["abar_i"] = jnp.sum(ga_i, axis=0).reshape(n_groups, SSM_STATE)

    dz = jnp.concatenate([du, dzu, dzv], axis=1)
    grad("w_in", matmul(xn2, dz, "tn", "grad_in", out_dtype=BF16))
    dxn2 = carry(matmul, dz, W["w_in"], "nt", "proj_in_bwd")
    dh1, dh1b, small["norm_mix"] = rmsnorm_bwd(dxn2, h1, norm_mix, dh2, "norm_mix_bwd")

    def pack(parts):
        flat = jnp.concatenate([v.reshape(-1) for v in parts.values()])
        return _pad_to(flat, 0, SUBLANE * LANE).reshape(-1, LANE), flat.shape[0]

    def unpack(everyones, n, parts, tag):
        rows = everyones.shape[0] // N_DEV
        summed = sum_devices(everyones.reshape(N_DEV, rows, LANE), "sum_" + tag).reshape(-1)[:n]
        out, off = {}, 0
        for k, v in parts.items():
            out[k] = summed[off:off + v.size].reshape(v.shape)
            off += v.size
        return out

    early = dict(small)
    flat_early, n_early = pack(early)
    small_landed = []
    dxn1 = ffn_bwd("ffn1", dh1b, xn1, gate1, up1, act1, "w1_gate", "w1_up", "w1_down",
                   extra=(gather_task([flat_early], [0]), small_landed.extend))
    tot = unpack(small_landed[0], n_early, early, "small")
    grad_x, _, g_norm_ffn1 = rmsnorm_bwd(dxn1, xs, norm_ffn1, dh1, "norm_ffn1_bwd")
    assert not wait_sibling and not wait_chips and set(reduced) == set(big_names)
    last = {"norm_ffn1": g_norm_ffn1}
    flat_last, n_last = pack(last)
    ((everyones_last,),) = run_tasks([gather_task([flat_last], [0])], "gather_last")
    tot.update(unpack(everyones_last, n_last, last, "last"))

    out_g, out_d, out_m, out_v = {}, {}, {}, {}
    for k in big_names:
        g, dl, nm, nv = adam_sharded(reduced[k], view(weights[k], k), view(moments_m[k], k), view(moments_v[k], k),
                                     "adam_" + k)
        out_g[k], out_d[k], out_m[k], out_v[k] = unview(g, k), unview(dl, k), unview(nm, k), unview(nv, k)

    _, ssm_vjp = jax.vjp(_ssm_discretize, ssm_log_dt[0], ssm_a_re[0], ssm_a_im[0], ssm_b_re[0], ssm_b_im[0])
    g_log_dt, g_a_re, g_a_im, g_b_re, g_b_im = ssm_vjp((tot["abar_r"], tot["abar_i"], tot["bbar_r"], tot["bbar_i"]))
    small_grads = {
        "norm_ffn1": tot["norm_ffn1"], "norm_mix": tot["norm_mix"], "ssm_log_dt": g_log_dt, "ssm_a_re": g_a_re,
        "ssm_a_im": g_a_im, "ssm_b_re": g_b_re, "ssm_b_im": g_b_im, "ssm_c_re": tot["c_re"], "ssm_c_im": tot["c_im"],
        "ssm_d": tot["ssm_d"], "gmlp_norm_v": tot["gmlp_norm_v"],
        "gmlp_w_s": jnp.where(causal[None], tot["gmlp_w_s"], 0.0), "gmlp_b_s": tot["gmlp_b_s"],
        "norm_ssm_out": tot["norm_ssm_out"], "norm_gmlp_out": tot["norm_gmlp_out"], "norm_ffn2": tot["norm_ffn2"],
        "norm_ple": tot["norm_ple"], "norm_final": tot["norm_final"]}
    for k, g in small_grads.items():
        shp = weights[k].shape
        g2 = _as2d(g.reshape(shp))
        dl, nm, nv = adam_small(g2, _as2d(weights[k]), _as2d(moments_m[k]), _as2d(moments_v[k]), "adam_" + k)
        out_g[k], out_d[k], out_m[k], out_v[k] = g2.reshape(shp), dl.reshape(shp), nm.reshape(shp), nv.reshape(shp)

    return (loss, grad_x[None], *[out_g[k] for k in names], *[out_d[k] for k in names],
            *[out_m[k] for k in names], *[out_v[k] for k in names])
```

```python
import math

import jax
import jax.numpy as jnp
from jax import lax
from jax.experimental import pallas as pl
from jax.experimental.pallas import tpu as pltpu

F32 = jnp.float32
BF16 = jnp.bfloat16
MESH_DT = pl.DeviceIdType.MESH

N_DEV = 8
N_CHIP = 4
LANE = 128
SUBLANE = 8
VMEM_LIMIT = 60 * 1024 * 1024

EPS = 1e-6
SSM_GROUP = 16
SSM_STATE = 64
GROUPS_PER_BLOCK = LANE // SSM_GROUP
STATE_BLOCK = GROUPS_PER_BLOCK * SSM_STATE
GMLP_HEAD = 128
CHUNK = 128

ADAM_LR = 0.001
ADAM_B1 = 0.9
ADAM_B2 = 0.999
ADAM_EPS = 1e-08
ADAM_WD = 0.01
ADAM_STEP = 10

GELU_K = math.sqrt(2.0 / math.pi)
GELU_C = 0.044715


def _cparams():
    return pltpu.CompilerParams(vmem_limit_bytes=VMEM_LIMIT)


def _tile(n, pref):
    if n <= pref:
        return n
    t = (pref // LANE) * LANE
    while t > 0:
        if n % t == 0:
            return t
        t -= LANE
    return n


def _row_tile(n, pref):
    if n <= pref:
        return n
    t = (pref // SUBLANE) * SUBLANE
    while t > 0:
        if n % t == 0:
            return t
        t -= SUBLANE
    return n


def _gelu(x):
    t = jnp.tanh(GELU_K * (x + GELU_C * x * x * x))
    return 0.5 * x * (1.0 + t)


def _gelu_grad(x):
    t = jnp.tanh(GELU_K * (x + GELU_C * x * x * x))
    return 0.5 * (1.0 + t) + 0.5 * x * (1.0 - t * t) * (GELU_K * (1.0 + 3.0 * GELU_C * x * x))


def _sigmoid(x):
    return 0.5 * jnp.tanh(0.5 * x) + 0.5


_DN = {
    "nn": (((1,), (0,)), ((), ())),
    "nt": (((1,), (1,)), ((), ())),
    "tn": (((0,), (0,)), ((), ())),
}


def _dot(a, b, mode="nn"):
    return lax.dot_general(a, b, _DN[mode], preferred_element_type=F32)


class CommTask:
    def __init__(self, inputs, out_shape, n_sems, start, late, finish, in_place=False):
        self.inputs, self.out_shape, self.n_sems = list(inputs), list(out_shape), n_sems
        self.start, self.late, self.finish = start, late, finish
        self.in_place = in_place


def _task_aliases(tasks, first_in, first_out):
    aliases = {}
    for t in tasks:
        if t.in_place:
            aliases.update({first_in + i: first_out + i for i in range(len(t.inputs))})
        first_in += len(t.inputs)
        first_out += len(t.out_shape)
    return aliases


def _call(body, *, name, grid, in_specs, out_specs, out_shape, args, scratch_shapes=(), tasks=()):
    in_specs, out_specs, out_shape = list(in_specs), list(out_specs), list(out_shape)
    scratch_shapes = list(scratch_shapes)
    if not tasks:
        return pl.pallas_call(
            body, name=name, grid=grid, in_specs=in_specs, out_specs=out_specs, out_shape=out_shape,
            scratch_shapes=scratch_shapes, compiler_params=_cparams())(*args)
    n_in, n_out, n_scr = len(in_specs), len(out_specs), len(scratch_shapes)
    t_in = [len(t.inputs) for t in tasks]
    t_out = [len(t.out_shape) for t in tasks]
    late_step = grid[0] - max(1, grid[0] // 4)
    has_late = grid[0] >= 2

    def carried(*refs):
        pos = n_in
        task_ins = []
        for k in t_in:
            task_ins.append(refs[pos:pos + k])
            pos += k
        outs = refs[pos:pos + n_out]
        pos += n_out
        task_outs = []
        for k in t_out:
            task_outs.append(refs[pos:pos + k])
            pos += k
        scratch = refs[pos:pos + n_scr]
        pos += n_scr
        sems = [refs[pos + 3 * i:pos + 3 * i + 3] for i in range(len(tasks))]
        ids = [pl.program_id(d) for d in range(len(grid))]
        rest_zero = True
        for d in range(1, len(grid)):
            rest_zero = jnp.logical_and(rest_zero, ids[d] == 0)
        first = jnp.logical_and(ids[0] == 0, rest_zero)
        last = ids[0] == grid[0] - 1
        for d in range(1, len(grid)):
            last = jnp.logical_and(last, ids[d] == grid[d] - 1)

        @pl.when(first)
        def _():
            for t, ti, to, s in zip(tasks, task_ins, task_outs, sems):
                t.start(ti, to, *s)

        if has_late:
            @pl.when(jnp.logical_and(ids[0] == late_step, rest_zero))
            def _():
                for t, ti, to, s in zip(tasks, task_ins, task_outs, sems):
                    t.late(ti, to, *s)

        body(*refs[:n_in], *outs, *scratch)

        @pl.when(last)
        def _():
            for t, ti, to, s in zip(tasks, task_ins, task_outs, sems):
                if not has_late:
                    t.late(ti, to, *s)
                t.finish(ti, to, *s)

    any_spec = pl.BlockSpec(memory_space=pl.ANY)
    sem_shapes = [pltpu.SemaphoreType.DMA((n,)) for t in tasks for n in t.n_sems]
    res = pl.pallas_call(
        carried, name=name, grid=grid,
        in_specs=in_specs + [any_spec] * sum(t_in), out_specs=out_specs + [any_spec] * sum(t_out),
        out_shape=out_shape + [s for t in tasks for s in t.out_shape],
        input_output_aliases=_task_aliases(tasks, n_in, n_out),
        scratch_shapes=scratch_shapes + sem_shapes, compiler_params=_cparams(),
    )(*args, *[a for t in tasks for a in t.inputs])
    res = list(res)
    task_res, pos = [], n_out
    for k in t_out:
        task_res.append(res[pos:pos + k])
        pos += k
    return res[:n_out], task_res


def _mm_dims(a, b, mode):
    if mode == "nn":
        (m, k), (k2, n) = a.shape, b.shape
    elif mode == "nt":
        (m, k), (n, k2) = a.shape, b.shape
    else:
        (k, m), (k2, n) = a.shape, b.shape
    assert k == k2, (a.shape, b.shape, mode)
    return m, n, k


def _mm_specs(mode, tm, tn, tk):
    if mode == "tn":
        a_spec = pl.BlockSpec((tk, tm), lambda i, j, k: (k, i))
    else:
        a_spec = pl.BlockSpec((tm, tk), lambda i, j, k: (i, k))
    if mode == "nt":
        b_spec = pl.BlockSpec((tn, tk), lambda i, j, k: (j, k))
    else:
        b_spec = pl.BlockSpec((tk, tn), lambda i, j, k: (k, j))
    return a_spec, b_spec


def _accumulate(acc, nk, partial, emit):
    if nk == 1:
        emit(partial)
        return
    kk = pl.program_id(2)

    @pl.when(kk == 0)
    def _():
        acc[...] = partial

    @pl.when(kk > 0)
    def _():
        acc[...] += partial

    @pl.when(kk == nk - 1)
    def _():
        emit(acc[...])


def matmul(a, b, mode, name, out_dtype=F32, res=None, scale=1.0, tm=1024, tn=1024, tk=2048, tasks=()):
    m, n, k = _mm_dims(a, b, mode)
    tm, tn, tk = _tile(m, tm), _tile(n, tn), _tile(k, tk)
    nk = k // tk
    a_spec, b_spec = _mm_specs(mode, tm, tn, tk)
    o_spec = pl.BlockSpec((tm, tn), lambda i, j, k: (i, j))
    has_res = res is not None

    def body(*refs):
        if has_res:
            a_ref, b_ref, r_ref, o_ref, acc = refs
        else:
            a_ref, b_ref, o_ref, acc = refs

        def emit(v):
            if scale != 1.0:
                v = v * scale
            if has_res:
                v = r_ref[...] + v
            o_ref[...] = v.astype(out_dtype)

        _accumulate(acc, nk, _dot(a_ref[...], b_ref[...], mode), emit)

    out = _call(
        body, name=name, grid=(m // tm, n // tn, nk),
        in_specs=[a_spec, b_spec] + ([o_spec] if has_res else []), out_specs=[o_spec],
        out_shape=[jax.ShapeDtypeStruct((m, n), out_dtype)], args=(a, b) + ((res,) if has_res else ()),
        scratch_shapes=[pltpu.VMEM((tm, tn) if nk > 1 else (SUBLANE, LANE), F32)], tasks=tasks)
    return (out[0][0], out[1]) if tasks else out[0]


def ffn_up(xn, wu, gate, name, tm=1024, tn=1024, tk=2048, tasks=()):
    m, n, k = _mm_dims(xn, wu, "nt")
    tm, tn, tk = _tile(m, tm), _tile(n, tn), _tile(k, tk)
    nk = k // tk
    a_spec, b_spec = _mm_specs("nt", tm, tn, tk)
    o_spec = pl.BlockSpec((tm, tn), lambda i, j, k: (i, j))

    def body(a_ref, u_ref, gate_ref, gate_b_ref, up_b_ref, act_ref, acc):
        def emit(u):
            g = gate_ref[...]
            gate_b_ref[...] = g.astype(BF16)
            up_b_ref[...] = u.astype(BF16)
            act_ref[...] = (g * _sigmoid(g) * u).astype(BF16)

        _accumulate(acc, nk, _dot(a_ref[...], u_ref[...], "nt"), emit)

    out = _call(
        body, name=name, grid=(m // tm, n // tn, nk), in_specs=[a_spec, b_spec, o_spec],
        out_specs=[o_spec, o_spec, o_spec],
        out_shape=[jax.ShapeDtypeStruct((m, n), BF16), jax.ShapeDtypeStruct((m, n), BF16),
                   jax.ShapeDtypeStruct((m, n), BF16)],
        args=(xn, wu, gate), scratch_shapes=[pltpu.VMEM((tm, tn) if nk > 1 else (SUBLANE, LANE), F32)], tasks=tasks)
    return (tuple(out[0]), out[1]) if tasks else tuple(out)


def ffn_bwd_act(dh, wd, gate, up, name, tm=1024, tn=1024, tk=2048, tasks=()):
    m, n, k = _mm_dims(dh, wd, "nt")
    tm, tn, tk = _tile(m, tm), _tile(n, tn), _tile(k, tk)
    nk = k // tk
    a_spec, b_spec = _mm_specs("nt", tm, tn, tk)
    o_spec = pl.BlockSpec((tm, tn), lambda i, j, k: (i, j))

    def body(a_ref, b_ref, gate_ref, up_ref, dg_ref, du_ref, acc):
        def emit(total):
            dact = 0.5 * total
            g = gate_ref[...].astype(F32)
            sg = _sigmoid(g)
            du_ref[...] = (dact * (g * sg)).astype(BF16)
            dg_ref[...] = (dact * up_ref[...].astype(F32) * (sg * (1.0 + g * (1.0 - sg)))).astype(BF16)

        _accumulate(acc, nk, _dot(a_ref[...], b_ref[...], "nt"), emit)

    out = _call(
        body, name=name, grid=(m // tm, n // tn, nk), in_specs=[a_spec, b_spec, o_spec, o_spec],
        out_specs=[o_spec, o_spec],
        out_shape=[jax.ShapeDtypeStruct((m, n), BF16), jax.ShapeDtypeStruct((m, n), BF16)],
        args=(dh, wd, gate, up), scratch_shapes=[pltpu.VMEM((tm, tn) if nk > 1 else (SUBLANE, LANE), F32)],
        tasks=tasks)
    return (tuple(out[0]), out[1]) if tasks else tuple(out)


def _rows(t, d, tr):
    return pl.BlockSpec((tr, d), lambda i: (i, 0))


def _vec(d):
    return pl.BlockSpec((1, d), lambda i: (0, 0))


def rmsnorm_fwd(x, g, name, tr=512):
    t, d = x.shape
    tr = _row_tile(t, tr)

    def body(x_ref, g_ref, o_ref):
        xf = x_ref[...]
        r = lax.rsqrt(jnp.mean(xf * xf, axis=-1, keepdims=True) + EPS)
        o_ref[...] = (xf * r * g_ref[...]).astype(BF16)

    return pl.pallas_call(
        body, name=name, grid=(t // tr,), in_specs=[_rows(t, d, tr), _vec(d)], out_specs=_rows(t, d, tr),
        out_shape=jax.ShapeDtypeStruct((t, d), BF16), compiler_params=_cparams(),
    )(x, g)


def _rms_bwd(dxn, xf, g):
    r = lax.rsqrt(jnp.mean(xf * xf, axis=-1, keepdims=True) + EPS)
    xhat = xf * r
    dg = jnp.sum(dxn * xhat, axis=0, keepdims=True)
    dxh = dxn * g
    dx = r * (dxh - xhat * jnp.mean(dxh * xhat, axis=-1, keepdims=True))
    return dx, dg


def rmsnorm_bwd(dxn, x, g, dres, name, tr=256):
    t, d = x.shape
    tr = _row_tile(t, tr)

    def body(dxn_ref, x_ref, g_ref, dres_ref, o_ref, ob_ref, dg_ref):
        dx, dg = _rms_bwd(dxn_ref[...], x_ref[...], g_ref[...])
        out = dres_ref[...] + dx
        o_ref[...] = out
        ob_ref[...] = out.astype(BF16)

        @pl.when(pl.program_id(0) == 0)
        def _():
            dg_ref[...] = jnp.zeros_like(dg_ref)

        dg_ref[...] += dg

    return pl.pallas_call(
        body, name=name, grid=(t // tr,),
        in_specs=[_rows(t, d, tr), _rows(t, d, tr), _vec(d), _rows(t, d, tr)],
        out_specs=[_rows(t, d, tr), _rows(t, d, tr), _vec(d)],
        out_shape=[jax.ShapeDtypeStruct((t, d), F32), jax.ShapeDtypeStruct((t, d), BF16),
                   jax.ShapeDtypeStruct((1, d), F32)],
        compiler_params=_cparams(),
    )(dxn, x, g, dres)


def final_loss(h, target, g, name, tr=256):
    t, d = h.shape
    tr = _row_tile(t, tr)

    def body(h_ref, t_ref, g_ref, dh_ref, loss_ref, dg_ref):
        xf = h_ref[...]
        gg = g_ref[...]
        r = lax.rsqrt(jnp.mean(xf * xf, axis=-1, keepdims=True) + EPS)
        xhat = xf * r
        e = xhat * gg - t_ref[...]
        part = jnp.sum(jnp.sum(e * e, axis=1, keepdims=True), axis=0, keepdims=True) * (0.5 / d)
        dout = e * (1.0 / d)
        dg = jnp.sum(dout * xhat, axis=0, keepdims=True)
        dxh = dout * gg
        dh_ref[...] = r * (dxh - xhat * jnp.mean(dxh * xhat, axis=-1, keepdims=True))

        @pl.when(pl.program_id(0) == 0)
        def _():
            dg_ref[...] = jnp.zeros_like(dg_ref)
            loss_ref[...] = jnp.zeros_like(loss_ref)

        dg_ref[...] += dg
        loss_ref[...] += jnp.broadcast_to(part, loss_ref.shape)

    return pl.pallas_call(
        body, name=name, grid=(t // tr,),
        in_specs=[_rows(t, d, tr), _rows(t, d, tr), _vec(d)],
        out_specs=[_rows(t, d, tr), pl.BlockSpec((SUBLANE, LANE), lambda i: (0, 0)), _vec(d)],
        out_shape=[jax.ShapeDtypeStruct((t, d), F32), jax.ShapeDtypeStruct((SUBLANE, LANE), F32),
                   jax.ShapeDtypeStruct((1, d), F32)],
        compiler_params=_cparams(),
    )(h, target, g)


def ple_fwd(h, glin, pp, name, tr=512):
    t, d = h.shape
    tr = _row_tile(t, tr)

    def body(h_ref, gl_ref, pp_ref, o_ref):
        o_ref[...] = h_ref[...] + _sigmoid(gl_ref[...]) * pp_ref[...]

    sp = _rows(t, d, tr)
    return pl.pallas_call(
        body, name=name, grid=(t // tr,), in_specs=[sp, sp, sp], out_specs=sp,
        out_shape=jax.ShapeDtypeStruct((t, d), F32), compiler_params=_cparams(),
    )(h, glin, pp)


def ple_bwd(dh, glin, pp, name, tr=512):
    t, d = dh.shape
    tr = _row_tile(t, tr)

    def body(dh_ref, gl_ref, pp_ref, dpp_ref, dgl_ref):
        gate = _sigmoid(gl_ref[...])
        dh_ = dh_ref[...]
        dpp_ref[...] = (dh_ * gate).astype(BF16)
        dgl_ref[...] = (dh_ * pp_ref[...] * gate * (1.0 - gate)).astype(BF16)

    sp = _rows(t, d, tr)
    return pl.pallas_call(
        body, name=name, grid=(t // tr,), in_specs=[sp, sp, sp], out_specs=[sp, sp],
        out_shape=[jax.ShapeDtypeStruct((t, d), BF16), jax.ShapeDtypeStruct((t, d), BF16)],
        compiler_params=_cparams(),
    )(dh, glin, pp)


def mix_out_fwd(y_pre, glin, y_gmlp, g_so, g_go, name, tr=512):
    t, d = y_pre.shape
    tr = _row_tile(t, tr)

    def body(yp_ref, gl_ref, yg_ref, gs_ref, gg_ref, o_ref):
        ys = _gelu(yp_ref[...]) * _sigmoid(gl_ref[...])
        r = lax.rsqrt(jnp.mean(ys * ys, axis=-1, keepdims=True) + EPS)
        o_ref[:, 0:d] = (ys * r * gs_ref[...]).astype(BF16)
        yq = yg_ref[...]
        r2 = lax.rsqrt(jnp.mean(yq * yq, axis=-1, keepdims=True) + EPS)
        o_ref[:, d:2 * d] = (yq * r2 * gg_ref[...]).astype(BF16)

    sp = _rows(t, d, tr)
    return pl.pallas_call(
        body, name=name, grid=(t // tr,), in_specs=[sp, sp, sp, _vec(d), _vec(d)],
        out_specs=_rows(t, 2 * d, tr), out_shape=jax.ShapeDtypeStruct((t, 2 * d), BF16),
        compiler_params=_cparams(),
    )(y_pre, glin, y_gmlp, g_so, g_go)


def mix_out_bwd(dycat, y_pre, glin, y_gmlp, g_so, g_go, name, tr=256):
    t, d = y_pre.shape
    tr = _row_tile(t, tr)

    def body(dy_ref, yp_ref, gl_ref, yg_ref, gs_ref, gg_ref, dyg_ref, dl_ref, dyq_ref, dgs_ref, dgg_ref):
        yg = _gelu(yp_ref[...])
        sg = _sigmoid(gl_ref[...])
        dys, dgs = _rms_bwd(dy_ref[:, 0:d], yg * sg, gs_ref[...])
        dyg_ref[...] = dys * sg
        dl_ref[...] = (dys * yg * sg * (1.0 - sg)).astype(BF16)
        dyq, dgg = _rms_bwd(dy_ref[:, d:2 * d], yg_ref[...], gg_ref[...])
        dyq_ref[...] = dyq

        @pl.when(pl.program_id(0) == 0)
        def _():
            dgs_ref[...] = jnp.zeros_like(dgs_ref)
            dgg_ref[...] = jnp.zeros_like(dgg_ref)

        dgs_ref[...] += dgs
        dgg_ref[...] += dgg

    sp = _rows(t, d, tr)
    return pl.pallas_call(
        body, name=name, grid=(t // tr,),
        in_specs=[_rows(t, 2 * d, tr), sp, sp, sp, _vec(d), _vec(d)],
        out_specs=[sp, sp, sp, _vec(d), _vec(d)],
        out_shape=[jax.ShapeDtypeStruct((t, d), F32), jax.ShapeDtypeStruct((t, d), BF16),
                   jax.ShapeDtypeStruct((t, d), F32), jax.ShapeDtypeStruct((1, d), F32),
                   jax.ShapeDtypeStruct((1, d), F32)],
        compiler_params=_cparams(),
    )(dycat, y_pre, glin, y_gmlp, g_so, g_go)


SCAN_COLS = 512


def _scan_tile(xr, xi, const, cr, ci, reverse):
    for lvl, sh in enumerate((1, 2, 4)):
        ar, ai = const(2 * lvl), const(2 * lvl + 1)
        s = (SUBLANE - sh) if reverse else sh
        rr = pltpu.roll(xr, s, 0)
        ri = pltpu.roll(xi, s, 0)
        xr, xi = xr + ar * rr - ai * ri, xi + ar * ri + ai * rr
    pr, pi_ = const(6), const(7)
    xr, xi = xr + pr * cr - pi_ * ci, xi + pr * ci + pi_ * cr
    return xr, xi


def _bcast_row(x, row):
    return jnp.broadcast_to(x[row:row + 1, :], x.shape)


def s5_fwd(z, bc_r, bc_i, cc_r, cc_i, apw, dvec, name, tc=512, tasks=()):
    t = z.shape[0]
    nblk = bc_r.shape[0]
    d = nblk * LANE
    ns = nblk * STATE_BLOCK
    tc = _row_tile(t, tc)
    ntile = tc // SUBLANE

    def body(z_ref, br_ref, bi_ref, cr_ref, ci_ref, apw_ref, d_ref, y_ref, yg_ref, sr_ref, si_ref, carry):
        @pl.when(pl.program_id(0) == 0)
        def _():
            carry[...] = jnp.zeros_like(carry)

        for j in range(nblk):
            uj = z_ref[:, j * LANE:(j + 1) * LANE]
            ub = uj.astype(BF16)
            for q in range(STATE_BLOCK // SCAN_COLS):
                c0 = j * STATE_BLOCK + q * SCAN_COLS
                cs = pl.ds(c0, SCAN_COLS)
                bs = slice(q * SCAN_COLS, (q + 1) * SCAN_COLS)
                sr_ref[:, cs] = _dot(ub, br_ref[j, :, bs])
                si_ref[:, cs] = _dot(ub, bi_ref[j, :, bs])
                const = lambda k, cs=cs: apw_ref[k, :, cs]

                def tile(k, c, cs=cs, const=const):
                    rows = pl.ds(pl.multiple_of(k * SUBLANE, SUBLANE), SUBLANE)
                    xr, xi = _scan_tile(sr_ref[rows, cs], si_ref[rows, cs], const, c[0], c[1], False)
                    sr_ref[rows, cs] = xr
                    si_ref[rows, cs] = xi
                    return _bcast_row(xr, SUBLANE - 1), _bcast_row(xi, SUBLANE - 1)

                c_r, c_i = lax.fori_loop(0, ntile, tile, (carry[0, :, cs], carry[1, :, cs]))
                carry[0, :, cs] = c_r
                carry[1, :, cs] = c_i
            sb = pl.ds(j * STATE_BLOCK, STATE_BLOCK)
            y = (_dot(sr_ref[:, sb].astype(BF16), cr_ref[j]) - _dot(si_ref[:, sb].astype(BF16), ci_ref[j])
                 + d_ref[:, j * LANE:(j + 1) * LANE] * uj)
            y_ref[:, j * LANE:(j + 1) * LANE] = y
            yg_ref[:, j * LANE:(j + 1) * LANE] = _gelu(y).astype(BF16)

    full3 = lambda shp: pl.BlockSpec(shp, lambda i: (0, 0, 0))
    out = _call(
        body, name=name, grid=(t // tc,),
        in_specs=[pl.BlockSpec((tc, d), lambda i: (i, 0)), full3(bc_r.shape), full3(bc_i.shape),
                  full3(cc_r.shape), full3(cc_i.shape), full3(apw.shape), _vec(d)],
        out_specs=[pl.BlockSpec((tc, d), lambda i: (i, 0)), pl.BlockSpec((tc, d), lambda i: (i, 0)),
                   pl.BlockSpec((tc, ns), lambda i: (i, 0)), pl.BlockSpec((tc, ns), lambda i: (i, 0))],
        out_shape=[jax.ShapeDtypeStruct((t, d), F32), jax.ShapeDtypeStruct((t, d), BF16),
                   jax.ShapeDtypeStruct((t, ns), F32), jax.ShapeDtypeStruct((t, ns), F32)],
        args=(z, bc_r, bc_i, cc_r, cc_i, apw, dvec), scratch_shapes=[pltpu.VMEM((2, SUBLANE, ns), F32)], tasks=tasks)
    return (tuple(out[0]), out[1]) if tasks else tuple(out)


def s5_bwd(dyg, y_pre, z, sr, si, bc_r, bc_i, cc_r, cc_i, apw_rev, dvec, name, tc=256, tasks=()):
    t = z.shape[0]
    nblk = bc_r.shape[0]
    d = nblk * LANE
    ns = nblk * STATE_BLOCK
    tc = _row_tile(t, tc)
    ntile = tc // SUBLANE
    nchunk = t // tc
    tiles_per_chunk = tc // SUBLANE

    def body(dyg_ref, yp_ref, z_ref, sr_ref, si_ref, pr_ref, pi_ref, br_ref, bi_ref, cr_ref, ci_ref, apw_ref,
             d_ref, du_ref, gd_ref, gcr_ref, gci_ref, gbr_ref, gbi_ref, gar_ref, gai_ref, lr_ref, li_ref, carry):
        step = pl.program_id(0)

        @pl.when(step == 0)
        def _():
            carry[...] = jnp.zeros_like(carry)
            for ref in (gd_ref, gcr_ref, gci_ref, gbr_ref, gbi_ref, gar_ref, gai_ref):
                ref[...] = jnp.zeros_like(ref)

        first_chunk = (step == nchunk - 1).astype(F32)
        keep_prev = 1.0 - first_chunk
        row0 = lax.broadcasted_iota(jnp.int32, (SUBLANE, SCAN_COLS), 0) == 0

        for j in range(nblk):
            lanes = slice(j * LANE, (j + 1) * LANE)
            uj = z_ref[:, lanes]
            ub = uj.astype(BF16)
            gy = dyg_ref[:, lanes] * _gelu_grad(yp_ref[:, lanes])
            gyb = gy.astype(BF16)
            gd_ref[:, lanes] += jnp.sum(gy * uj, axis=0, keepdims=True)
            for q in range(STATE_BLOCK // SCAN_COLS):
                c0 = j * STATE_BLOCK + q * SCAN_COLS
                cs = pl.ds(c0, SCAN_COLS)
                bs = slice(q * SCAN_COLS, (q + 1) * SCAN_COLS)
                lr_ref[:, cs] = _dot(gyb, cr_ref[j, bs, :], "nt")
                li_ref[:, cs] = -_dot(gyb, ci_ref[j, bs, :], "nt")
                const = lambda k, cs=cs: apw_ref[k, :, cs]

                def one_tile(rows, prev_r, prev_i, c, cs=cs, const=const):
                    cr_, ci_, gar, gai = c
                    xr, xi = _scan_tile(lr_ref[rows, cs], li_ref[rows, cs], const, cr_, ci_, True)
                    lr_ref[rows, cs] = xr
                    li_ref[rows, cs] = xi
                    spr = jnp.where(row0, prev_r, pltpu.roll(sr_ref[rows, cs], 1, 0))
                    spi = jnp.where(row0, prev_i, pltpu.roll(si_ref[rows, cs], 1, 0))
                    gar = gar + xr * spr + xi * spi
                    gai = gai + xi * spr - xr * spi
                    return _bcast_row(xr, 0), _bcast_row(xi, 0), gar, gai

                def tile(k, c, cs=cs, one_tile=one_tile):
                    kk = ntile - 1 - k
                    rows = pl.ds(pl.multiple_of(kk * SUBLANE, SUBLANE), SUBLANE)
                    prow = pl.ds(pl.multiple_of((kk - 1) * SUBLANE, SUBLANE), SUBLANE)
                    prev_r = _bcast_row(sr_ref[prow, cs], SUBLANE - 1)
                    prev_i = _bcast_row(si_ref[prow, cs], SUBLANE - 1)
                    return one_tile(rows, prev_r, prev_i, c)

                zero = jnp.zeros((SUBLANE, SCAN_COLS), F32)
                c = lax.fori_loop(0, ntile - 1, tile, (carry[0, :, cs], carry[1, :, cs], zero, zero))
                prev_r = _bcast_row(pr_ref[:, cs], SUBLANE - 1) * keep_prev
                prev_i = _bcast_row(pi_ref[:, cs], SUBLANE - 1) * keep_prev
                c_r, c_i, gar, gai = one_tile(pl.ds(0, SUBLANE), prev_r, prev_i, c)
                carry[0, :, cs] = c_r
                carry[1, :, cs] = c_i
                gar_ref[:, cs] += gar
                gai_ref[:, cs] += gai
            sb = pl.ds(j * STATE_BLOCK, STATE_BLOCK)
            lrb = lr_ref[:, sb].astype(BF16)
            lib = li_ref[:, sb].astype(BF16)
            gcr_ref[j] += _dot(gyb, sr_ref[:, sb].astype(BF16), "tn")
            gci_ref[j] -= _dot(gyb, si_ref[:, sb].astype(BF16), "tn")
            gbr_ref[j] += _dot(ub, lrb, "tn")
            gbi_ref[j] += _dot(ub, lib, "tn")
            du = _dot(lrb, br_ref[j], "nt") + _dot(lib, bi_ref[j], "nt") + gy * d_ref[:, lanes]
            du_ref[:, lanes] = du.astype(BF16)

    rev = lambda i: (nchunk - 1 - i, 0)
    prev = lambda i: (jnp.maximum((nchunk - 1 - i) * tiles_per_chunk - 1, 0), 0)
    full3 = lambda shp: pl.BlockSpec(shp, lambda i: (0, 0, 0))
    acc3 = pl.BlockSpec((nblk, LANE, STATE_BLOCK), lambda i: (0, 0, 0))
    acc_rows = pl.BlockSpec((SUBLANE, ns), lambda i: (0, 0))
    out = _call(
        body, name=name, grid=(nchunk,),
        in_specs=[pl.BlockSpec((tc, d), rev), pl.BlockSpec((tc, d), rev), pl.BlockSpec((tc, d), rev),
                  pl.BlockSpec((tc, ns), rev), pl.BlockSpec((tc, ns), rev),
                  pl.BlockSpec((SUBLANE, ns), prev), pl.BlockSpec((SUBLANE, ns), prev),
                  full3(bc_r.shape), full3(bc_i.shape), full3(cc_r.shape), full3(cc_i.shape), full3(apw_rev.shape),
                  _vec(d)],
        out_specs=[pl.BlockSpec((tc, d), rev), _vec(d), acc3, acc3, acc3, acc3, acc_rows, acc_rows],
        out_shape=[jax.ShapeDtypeStruct((t, d), BF16), jax.ShapeDtypeStruct((1, d), F32)]
        + [jax.ShapeDtypeStruct((nblk, LANE, STATE_BLOCK), F32)] * 4
        + [jax.ShapeDtypeStruct((SUBLANE, ns), F32)] * 2,
        args=(dyg, y_pre, z, sr, si, sr, si, bc_r, bc_i, cc_r, cc_i, apw_rev, dvec),
        scratch_shapes=[pltpu.VMEM((tc, ns), F32), pltpu.VMEM((tc, ns), F32), pltpu.VMEM((2, SUBLANE, ns), F32)],
        tasks=tasks)
    return (tuple(out[0]), out[1]) if tasks else tuple(out)


def _cmul(a, b):
    return a[0] * b[0] - a[1] * b[1], a[0] * b[1] + a[1] * b[0]


def _scan_constants(abar_r, abar_i, reverse):
    ar = abar_r.reshape(1, -1)
    ai = abar_i.reshape(1, -1)
    if reverse:
        ai = -ai
    pw = [(ar, ai)]
    for _ in range(SUBLANE - 1):
        pw.append(_cmul(pw[-1], (ar, ai)))
    rows = lax.broadcasted_iota(jnp.int32, (SUBLANE, 1), 0)
    out = []
    for sh in (1, 2, 4):
        keep = (rows <= SUBLANE - 1 - sh) if reverse else (rows >= sh)
        for part in pw[sh - 1]:
            out.append(jnp.where(keep, part, 0.0))
    for comp in (0, 1):
        stack = jnp.concatenate([pw[k][comp] for k in range(SUBLANE)], axis=0)
        out.append(stack[::-1] if reverse else stack)
    return jnp.stack(out, axis=0).astype(F32)


def _ssm_discretize(log_dt, a_re, a_im, b_re, b_im):
    dt = jnp.exp(log_dt)[:, None]
    lr = jnp.minimum(a_re, -1e-4)
    li = a_im
    mag = jnp.exp(lr * dt)
    ang = li * dt
    abar_r = mag * jnp.cos(ang)
    abar_i = mag * jnp.sin(ang)
    den = lr * lr + li * li
    xr = abar_r - 1.0
    xi = abar_i
    zr = (xr * lr + xi * li) / den
    zi = (xi * lr - xr * li) / den
    bbar_r = zr[..., None] * b_re - zi[..., None] * b_im
    bbar_i = zr[..., None] * b_im + zi[..., None] * b_re
    return abar_r, abar_i, bbar_r, bbar_i


def _block_diag(w):
    g, a, b = w.shape
    nb = g // GROUPS_PER_BLOCK
    eye = jnp.eye(GROUPS_PER_BLOCK, dtype=w.dtype)
    w5 = w.reshape(nb, GROUPS_PER_BLOCK, a, b)
    out = w5[:, :, :, None, :] * eye[None, :, None, :, None]
    return out.reshape(nb, GROUPS_PER_BLOCK * a, GROUPS_PER_BLOCK * b)


def _block_diag_extract(m, a, b):
    nb = m.shape[0]
    eye = jnp.eye(GROUPS_PER_BLOCK, dtype=m.dtype)
    m5 = m.reshape(nb, GROUPS_PER_BLOCK, a, GROUPS_PER_BLOCK, b)
    out = jnp.sum(m5 * eye[None, :, None, :, None], axis=3)
    return out.reshape(nb * GROUPS_PER_BLOCK, a, b)


def _layer_norm(gv, nv):
    mu = jnp.mean(gv, axis=-1, keepdims=True)
    xc = gv - mu
    r = lax.rsqrt(jnp.mean(xc * xc, axis=-1, keepdims=True) + EPS)
    xhat = xc * r
    return xhat * nv, xhat, r


def gmlp_fwd(z, norm_v, wm, bs, name, tr=256):
    t = z.shape[0]
    nh = wm.shape[0]
    d = nh * GMLP_HEAD
    col0 = (z.shape[1] - 2 * d) // d
    tr = _row_tile(t, tr)

    def body(zu_ref, zv_ref, nv_ref, wm_ref, bs_ref, o_ref):
        v, _, _ = _layer_norm(_gelu(zv_ref[...]), nv_ref[...])
        vb = v.astype(BF16)
        u = _gelu(zu_ref[...])
        for c in range(tr // CHUNK):
            rows = slice(c * CHUNK, (c + 1) * CHUNK)
            for h in range(nh):
                cols = slice(h * GMLP_HEAD, (h + 1) * GMLP_HEAD)
                s = _dot(wm_ref[h], vb[rows, cols]) + bs_ref[h]
                o_ref[rows, cols] = u[rows, cols] * s

    return pl.pallas_call(
        body, name=name, grid=(t // tr,),
        in_specs=[pl.BlockSpec((tr, d), lambda i: (i, col0)), pl.BlockSpec((tr, d), lambda i: (i, col0 + 1)),
                  _vec(d), pl.BlockSpec(wm.shape, lambda i: (0, 0, 0)), pl.BlockSpec(bs.shape, lambda i: (0, 0, 0))],
        out_specs=pl.BlockSpec((tr, d), lambda i: (i, 0)),
        out_shape=jax.ShapeDtypeStruct((t, d), F32), compiler_params=_cparams(),
    )(z, z, norm_v, wm, bs)


def gmlp_bwd(dy, z, norm_v, wm, wmt, bs, name, tr=256):
    t = z.shape[0]
    nh = wm.shape[0]
    d = nh * GMLP_HEAD
    col0 = (z.shape[1] - 2 * d) // d
    tr = _row_tile(t, tr)

    def body(dy_ref, zu_ref, zv_ref, nv_ref, wm_ref, wmt_ref, bs_ref, dzu_ref, dzv_ref, dnv_ref, dwm_ref, dbs_ref,
             dv_ref):
        @pl.when(pl.program_id(0) == 0)
        def _():
            dnv_ref[...] = jnp.zeros_like(dnv_ref)
            dwm_ref[...] = jnp.zeros_like(dwm_ref)
            dbs_ref[...] = jnp.zeros_like(dbs_ref)

        zv = zv_ref[...]
        nv = nv_ref[...]
        v, xhat, r = _layer_norm(_gelu(zv), nv)
        vb = v.astype(BF16)
        zu = zu_ref[...]
        u = _gelu(zu)
        dy_ = dy_ref[...]
        for c in range(tr // CHUNK):
            rows = slice(c * CHUNK, (c + 1) * CHUNK)
            for h in range(nh):
                cols = slice(h * GMLP_HEAD, (h + 1) * GMLP_HEAD)
                vh = vb[rows, cols]
                s = _dot(wm_ref[h], vh) + bs_ref[h]
                dyh = dy_[rows, cols]
                dzu_ref[rows, cols] = (dyh * s * _gelu_grad(zu[rows, cols])).astype(BF16)
                ds = dyh * u[rows, cols]
                dsb = ds.astype(BF16)
                dbs_ref[h] += jnp.sum(ds, axis=1, keepdims=True)
                dwm_ref[h] += _dot(dsb, vh, "nt")
                dv_ref[rows, cols] = _dot(wmt_ref[h], dsb)
        dv = dv_ref[...]
        dnv_ref[...] += jnp.sum(dv * xhat, axis=0, keepdims=True)
        dxh = dv * nv
        dgv = r * (dxh - jnp.mean(dxh, axis=-1, keepdims=True) - xhat * jnp.mean(dxh * xhat, axis=-1, keepdims=True))
        dzv_ref[...] = (dgv * _gelu_grad(zv)).astype(BF16)

    full3 = lambda shp: pl.BlockSpec(shp, lambda i: (0, 0, 0))
    rows_d = pl.BlockSpec((tr, d), lambda i: (i, 0))
    return pl.pallas_call(
        body, name=name, grid=(t // tr,),
        in_specs=[rows_d, pl.BlockSpec((tr, d), lambda i: (i, col0)), pl.BlockSpec((tr, d), lambda i: (i, col0 + 1)),
                  _vec(d), full3(wm.shape), full3(wmt.shape), full3(bs.shape)],
        out_specs=[rows_d, rows_d, _vec(d), full3((nh, CHUNK, CHUNK)), full3((nh, CHUNK, 1))],
        out_shape=[jax.ShapeDtypeStruct((t, d), BF16), jax.ShapeDtypeStruct((t, d), BF16),
                   jax.ShapeDtypeStruct((1, d), F32), jax.ShapeDtypeStruct((nh, CHUNK, CHUNK), F32),
                   jax.ShapeDtypeStruct((nh, CHUNK, 1), F32)],
        scratch_shapes=[pltpu.VMEM((tr, d), F32)], compiler_params=_cparams(),
    )(dy, z, z, norm_v, wm, wmt, bs)


def _block(ref, axis, size, k):
    start = pl.multiple_of(k * size, size)
    if axis == 0:
        return ref.at[pl.ds(start, size), :]
    return ref.at[:, pl.ds(start, size)]


def _place():
    x, y, c = lax.axis_index("x"), lax.axis_index("y"), lax.axis_index("c")
    chips = [(1 - x, y), (x, 1 - y), (1 - x, 1 - y)]
    return x, y, c, chips


def _dev(x, y, c):
    return 4 * x + 2 * y + c


def gather_task(shards, axes):
    n = len(shards)
    sizes = [s.shape[ax] for s, ax in zip(shards, axes)]
    out_shape = [
        jax.ShapeDtypeStruct((s.shape[0] * N_DEV, s.shape[1]) if ax == 0 else (s.shape[0], s.shape[1] * N_DEV), s.dtype)
        for s, ax in zip(shards, axes)
    ]

    def copy(ins, outs, send_sems, recv_sems, t, k, block, to, from_input=False):
        dst = _block(outs[t], axes[t], sizes[t], _dev(*block))
        return pltpu.make_async_remote_copy(
            src_ref=ins[t] if from_input else dst, dst_ref=dst,
            send_sem=send_sems.at[t * 7 + k], recv_sem=recv_sems.at[t * 7 + k],
            device_id=to, device_id_type=MESH_DT)

    def local(ins, outs, local_sems, t, me):
        return pltpu.make_async_copy(ins[t], _block(outs[t], axes[t], sizes[t], _dev(*me)), local_sems.at[t])

    def start(ins, outs, send_sems, recv_sems, local_sems):
        x, y, c, chips = _place()
        me, sibling = (x, y, c), (x, y, 1 - c)
        for t in range(n):
            local(ins, outs, local_sems, t, me).start()
        for t in range(n):
            copy(ins, outs, send_sems, recv_sems, t, 0, me, sibling, True).start()
            for j, chip in enumerate(chips):
                copy(ins, outs, send_sems, recv_sems, t, 1 + j, me, (*chip, c), True).start()

    def late(ins, outs, send_sems, recv_sems, local_sems):
        x, y, c, chips = _place()
        me, sibling = (x, y, c), (x, y, 1 - c)
        for t in range(n):
            for j, chip in enumerate(chips):
                copy(ins, outs, send_sems, recv_sems, t, 1 + j, (*chip, c), me).wait_recv()
                copy(ins, outs, send_sems, recv_sems, t, 4 + j, (*chip, c), sibling).start()

    def finish(ins, outs, send_sems, recv_sems, local_sems):
        x, y, c, chips = _place()
        me, sibling = (x, y, c), (x, y, 1 - c)
        for t in range(n):
            copy(ins, outs, send_sems, recv_sems, t, 0, sibling, me).wait_recv()
            for j, chip in enumerate(chips):
                copy(ins, outs, send_sems, recv_sems, t, 4 + j, (*chip, 1 - c), me).wait_recv()
        for t in range(n):
            copy(ins, outs, send_sems, recv_sems, t, 0, me, sibling, True).wait_send()
            for j, chip in enumerate(chips):
                copy(ins, outs, send_sems, recv_sems, t, 1 + j, me, (*chip, c), True).wait_send()
                copy(ins, outs, send_sems, recv_sems, t, 4 + j, (*chip, c), sibling).wait_send()
            local(ins, outs, local_sems, t, me).wait()

    return CommTask(shards, out_shape, (7 * n, 7 * n, n), start, late, finish)


def _blk3(shape2, axis):
    r, c = shape2
    return (r // N_DEV, c) if axis == 0 else (r, c // N_DEV)


def _no_late(ins, outs, send_sems, recv_sems, local_sems):
    pass


def to_sibling_task(grads, axes):
    n = len(grads)
    blks = [_blk3(g.shape, ax) for g, ax in zip(grads, axes)]
    sizes = [b[ax] for b, ax in zip(blks, axes)]

    def copies(ins, outs, send_sems, recv_sems):
        x, y, c, _ = _place()
        return [pltpu.make_async_remote_copy(
            src_ref=_block(ins[t], axes[t], sizes[t], 2 * i + (1 - c)), dst_ref=outs[t].at[i],
            send_sem=send_sems.at[t * N_CHIP + i], recv_sem=recv_sems.at[t * N_CHIP + i],
            device_id=(x, y, 1 - c), device_id_type=MESH_DT) for t in range(n) for i in range(N_CHIP)]

    def start(ins, outs, send_sems, recv_sems, local_sems):
        for cp in copies(ins, outs, send_sems, recv_sems):
            cp.start()

    def finish(ins, outs, send_sems, recv_sems, local_sems):
        cps = copies(ins, outs, send_sems, recv_sems)
        for cp in cps:
            cp.wait_recv()
        for cp in cps:
            cp.wait_send()

    out_shape = [jax.ShapeDtypeStruct((N_CHIP,) + b, g.dtype) for b, g in zip(blks, grads)]
    return CommTask(grads, out_shape, (N_CHIP * n, N_CHIP * n, 1), start, _no_late, finish)


def across_chips_task(parts):
    n = len(parts)

    def copies(ins, outs, send_sems, recv_sems):
        x, y, c, chips = _place()
        my_chip = 2 * x + y
        return [pltpu.make_async_remote_copy(
            src_ref=ins[t].at[2 * chip[0] + chip[1]], dst_ref=outs[t].at[my_chip],
            send_sem=send_sems.at[t * 3 + j], recv_sem=recv_sems.at[t * 3 + j],
            device_id=(*chip, c), device_id_type=MESH_DT) for t in range(n) for j, chip in enumerate(chips)]

    def mine(ins, outs, local_sems):
        x, y, _, _ = _place()
        my_chip = 2 * x + y
        return [pltpu.make_async_copy(ins[t].at[my_chip], outs[t].at[my_chip], local_sems.at[t]) for t in range(n)]

    def start(ins, outs, send_sems, recv_sems, local_sems):
        for cp in mine(ins, outs, local_sems):
            cp.start()
        for cp in copies(ins, outs, send_sems, recv_sems):
            cp.start()

    def finish(ins, outs, send_sems, recv_sems, local_sems):
        cps = copies(ins, outs, send_sems, recv_sems)
        for cp in cps:
            cp.wait_recv()
        for cp in cps:
            cp.wait_send()
        for cp in mine(ins, outs, local_sems):
            cp.wait()

    out_shape = [jax.ShapeDtypeStruct(p.shape, p.dtype) for p in parts]
    return CommTask(parts, out_shape, (3 * n, 3 * n, n), start, _no_late, finish)


def run_tasks(tasks, name):
    t_in = [len(t.inputs) for t in tasks]
    t_out = [len(t.out_shape) for t in tasks]

    def body(*refs):
        pos, views = 0, []
        for k in t_in:
            views.append([refs[pos:pos + k]])
            pos += k
        for v, k in zip(views, t_out):
            v.append(refs[pos:pos + k])
            pos += k
        for i, v in enumerate(views):
            v.extend(refs[pos + 3 * i:pos + 3 * i + 3])
        for phase in ("start", "late", "finish"):
            for t, v in zip(tasks, views):
                getattr(t, phase)(*v)

    any_spec = pl.BlockSpec(memory_space=pl.ANY)
    res = pl.pallas_call(
        body, name=name, in_specs=[any_spec] * sum(t_in), out_specs=[any_spec] * sum(t_out),
        out_shape=[s for t in tasks for s in t.out_shape], input_output_aliases=_task_aliases(tasks, 0, 0),
        scratch_shapes=[pltpu.SemaphoreType.DMA((k,)) for t in tasks for k in t.n_sems],
    )(*[a for t in tasks for a in t.inputs])
    res, out, pos = list(res), [], 0
    for k in t_out:
        out.append(res[pos:pos + k])
        pos += k
    return out


_HBM_SPEC = pl.BlockSpec(memory_space=pl.ANY)
_SEM_SPEC = pl.BlockSpec(memory_space=pltpu.SEMAPHORE)
_DATAFLOW = pltpu.SideEffectType.DATAFLOW_SIDE_EFFECTING


def _full_shape(s, ax):
    return (s.shape[0] * N_DEV, s.shape[1]) if ax == 0 else (s.shape[0], s.shape[1] * N_DEV)


def _level1_copy(src, landing, axis, size, send_sems, recv_sems, slot, sender, to):
    dst = _block(landing, axis, size, _dev(*sender))
    return pltpu.make_async_remote_copy(src_ref=src, dst_ref=dst, send_sem=send_sems.at[slot],
                                        recv_sem=recv_sems.at[slot], device_id=to, device_id_type=MESH_DT)


def place_own_block(shard, axis, me, name, tr=256):
    r, c = shard.shape
    tr = _row_tile(r, tr)
    nrb = r // tr
    if axis == 0:
        o_map = lambda i, me_ref: (me_ref[0] * nrb + i, 0)
    else:
        o_map = lambda i, me_ref: (i, me_ref[0])

    def body(me_ref, x_ref, o_ref):
        o_ref[...] = x_ref[...]

    return pl.pallas_call(
        body, name=name,
        grid_spec=pltpu.PrefetchScalarGridSpec(
            num_scalar_prefetch=1, grid=(nrb,), in_specs=[pl.BlockSpec((tr, c), lambda i, me_ref: (i, 0))],
            out_specs=pl.BlockSpec((tr, c), o_map)),
        out_shape=jax.ShapeDtypeStruct(_full_shape(shard, axis), shard.dtype), compiler_params=_cparams(),
    )(me, shard)


def gather_start(landing, axes, sizes, groups, name):
    n = len(landing)

    def body(*refs):
        lands, sems = refs[:n], refs[2 * n:]
        x, y, c, chips = _place()
        me = (x, y, c)
        targets = [(x, y, 1 - c)] + [(*chip, c) for chip in chips]
        for g, members in enumerate(groups):
            for m, t in enumerate(members):
                own = _block(lands[t], axes[t], sizes[t], _dev(*me))
                for k, to in enumerate(targets):
                    _level1_copy(own, lands[t], axes[t], sizes[t], sems[2 * g], sems[2 * g + 1], 4 * m + k,
                                 me, to).start()

    out = pl.pallas_call(
        body, name=name,
        out_shape=[jax.ShapeDtypeStruct(b.shape, b.dtype) for b in landing]
        + [pltpu.SemaphoreType.DMA((4 * len(members),)) for members in groups for _ in (0, 1)],
        in_specs=[_HBM_SPEC] * n, out_specs=[_HBM_SPEC] * n + [_SEM_SPEC] * (2 * len(groups)),
        input_output_aliases={i: i for i in range(n)},
        compiler_params=pltpu.CompilerParams(has_side_effects=_DATAFLOW),
    )(*landing)
    out = list(out)
    sems = out[n:]
    return out[:n], [(sems[2 * g], sems[2 * g + 1]) for g in range(len(groups))]


def gather_wait(landing, axes, sizes, send_sems, recv_sems, after, name):
    n = len(landing)

    def body(*refs):
        lands = refs[:n]
        send, recv = refs[n], refs[n + 1]
        x, y, c, chips = _place()
        me = (x, y, c)
        peers = [(x, y, 1 - c)] + [(*chip, c) for chip in chips]
        for t in range(n):
            own = _block(lands[t], axes[t], sizes[t], _dev(*me))
            for k, peer in enumerate(peers):
                _level1_copy(own, lands[t], axes[t], sizes[t], send, recv, 4 * t + k, me, peer).wait_send()
                _level1_copy(own, lands[t], axes[t], sizes[t], send, recv, 4 * t + k, peer, me).wait_recv()

    out = pl.pallas_call(
        body, name=name, out_shape=[jax.ShapeDtypeStruct(b.shape, b.dtype) for b in landing],
        in_specs=[_HBM_SPEC] * n + [_SEM_SPEC, _SEM_SPEC, pl.BlockSpec(memory_space=pl.ANY)],
        out_specs=[_HBM_SPEC] * n, input_output_aliases={i: i for i in range(n)},
        compiler_params=pltpu.CompilerParams(has_side_effects=_DATAFLOW),
    )(*landing, send_sems, recv_sems, after)
    return list(out)


def forward_task(landing, axes, sizes):
    n = len(landing)

    def forward(lands, send_sems, recv_sems, t, j, chip_core):
        x, y, c, _ = _place()
        blk = _block(lands[t], axes[t], sizes[t], _dev(*chip_core))
        return pltpu.make_async_remote_copy(src_ref=blk, dst_ref=blk, send_sem=send_sems.at[3 * t + j],
                                            recv_sem=recv_sems.at[3 * t + j], device_id=(x, y, 1 - c),
                                            device_id_type=MESH_DT)

    def start(ins, lands, send_sems, recv_sems, local_sems):
        _, _, c, chips = _place()
        for t in range(n):
            for j, chip in enumerate(chips):
                forward(lands, send_sems, recv_sems, t, j, (*chip, c)).start()

    def finish(ins, lands, send_sems, recv_sems, local_sems):
        _, _, c, chips = _place()
        for t in range(n):
            for j, chip in enumerate(chips):
                forward(lands, send_sems, recv_sems, t, j, (*chip, 1 - c)).wait_recv()
        for t in range(n):
            for j, chip in enumerate(chips):
                forward(lands, send_sems, recv_sems, t, j, (*chip, c)).wait_send()

    out_shape = [jax.ShapeDtypeStruct(b.shape, b.dtype) for b in landing]
    return CommTask(landing, out_shape, (3 * n, 3 * n, 1), start, _no_late, finish, in_place=True)


def rs_chip_sum(grad, recv, axis, core, name, tr=512):
    br, bc = _blk3(grad.shape, axis)
    tr = _row_tile(br, tr)
    nrb = br // tr

    if axis == 0:
        g_map = lambda i, r, c_ref: ((2 * i + c_ref[0]) * nrb + r, 0)
    else:
        g_map = lambda i, r, c_ref: (r, 2 * i + c_ref[0])

    def body(c_ref, g_ref, r_ref, o_ref):
        o_ref[...] = (g_ref[...].astype(F32) + r_ref[...].astype(F32)).astype(BF16)

    return pl.pallas_call(
        body, name=name,
        grid_spec=pltpu.PrefetchScalarGridSpec(
            num_scalar_prefetch=1, grid=(N_CHIP, nrb),
            in_specs=[pl.BlockSpec((tr, bc), g_map), pl.BlockSpec((None, tr, bc), lambda i, r, c_ref: (i, r, 0))],
            out_specs=pl.BlockSpec((None, tr, bc), lambda i, r, c_ref: (i, r, 0))),
        out_shape=jax.ShapeDtypeStruct((N_CHIP, br, bc), BF16), compiler_params=_cparams(),
    )(core, grad, recv)


def _adamw(w, g, m, v):
    m = ADAM_B1 * m + (1.0 - ADAM_B1) * g
    v = ADAM_B2 * v + (1.0 - ADAM_B2) * (g * g)
    m_hat = m / (1.0 - ADAM_B1 ** ADAM_STEP)
    v_hat = v / (1.0 - ADAM_B2 ** ADAM_STEP)
    delta = -ADAM_LR * (m_hat / (jnp.sqrt(v_hat) + ADAM_EPS) + ADAM_WD * w)
    return delta, m, v


def _sum_chips(p_ref):
    g = p_ref[0].astype(F32)
    for i in range(1, N_CHIP):
        g = g + p_ref[i].astype(F32)
    return g


def adam_sharded(parts, w, m, v, name, tr=256):
    r, c = w.shape
    assert parts.shape[2] == c
    tr = _row_tile(r, tr)

    def body(p_ref, w_ref, m_ref, v_ref, g_ref, d_ref, nm_ref, nv_ref):
        g = _sum_chips(p_ref)
        delta, nm, nv = _adamw(w_ref[...], g, m_ref[...], v_ref[...])
        g_ref[...] = g
        d_ref[...] = delta
        nm_ref[...] = nm
        nv_ref[...] = nv

    sp = pl.BlockSpec((tr, c), lambda i: (i, 0))
    return pl.pallas_call(
        body, name=name, grid=(r // tr,),
        in_specs=[pl.BlockSpec((N_CHIP, tr, c), lambda i: (0, i, 0)), sp, sp, sp],
        out_specs=[sp, sp, sp, sp], out_shape=[jax.ShapeDtypeStruct((r, c), F32)] * 4,
        compiler_params=_cparams(),
    )(parts, w, m, v)


def adam_small(g, w, m, v, name):
    def body(g_ref, w_ref, m_ref, v_ref, d_ref, nm_ref, nv_ref):
        delta, nm, nv = _adamw(w_ref[...], g_ref[...], m_ref[...], v_ref[...])
        d_ref[...] = delta
        nm_ref[...] = nm
        nv_ref[...] = nv

    return pl.pallas_call(
        body, name=name, out_shape=[jax.ShapeDtypeStruct(w.shape, F32)] * 3, compiler_params=_cparams(),
    )(g, w, m, v)


def sum_devices(gathered, name, tr=512):
    _, r, c = gathered.shape
    tr = _row_tile(r, tr)

    def body(x_ref, o_ref):
        s = x_ref[0]
        for k in range(1, N_DEV):
            s = s + x_ref[k]
        o_ref[...] = s

    return pl.pallas_call(
        body, name=name, grid=(r // tr,), in_specs=[pl.BlockSpec((N_DEV, tr, c), lambda i: (0, i, 0))],
        out_specs=pl.BlockSpec((tr, c), lambda i: (i, 0)), out_shape=jax.ShapeDtypeStruct((r, c), F32),
        compiler_params=_cparams(),
    )(gathered)


def _pad_to(a, axis, mult):
    size = a.shape[axis]
    pad = (-size) % mult
    if pad == 0:
        return a
    cfg = [(0, 0)] * a.ndim
    cfg[axis] = (0, pad)
    return jnp.pad(a, cfg)


def _as2d(a):
    if a.ndim == 1:
        return a.reshape(1, -1)
    return a.reshape(-1, a.shape[-1])


def kernel(x, p, norm_ffn1, w1_gate, w1_up, w1_down, norm_mix, w_in, ssm_log_dt, ssm_a_re, ssm_a_im, ssm_b_re, ssm_b_im, ssm_c_re, ssm_c_im, ssm_d, ssm_w_glu, gmlp_norm_v, gmlp_w_s, gmlp_b_s, norm_ssm_out, norm_gmlp_out, w_out, norm_ffn2, w2_gate, w2_up, w2_down, norm_ple, w_ple_gate, w_ple_proj, norm_final, loss_target, m_norm_ffn1, m_w1_gate, m_w1_up, m_w1_down, m_norm_mix, m_w_in, m_ssm_log_dt, m_ssm_a_re, m_ssm_a_im, m_ssm_b_re, m_ssm_b_im, m_ssm_c_re, m_ssm_c_im, m_ssm_d, m_ssm_w_glu, m_gmlp_norm_v, m_gmlp_w_s, m_gmlp_b_s, m_norm_ssm_out, m_norm_gmlp_out, m_w_out, m_norm_ffn2, m_w2_gate, m_w2_up, m_w2_down, m_norm_ple, m_w_ple_gate, m_w_ple_proj, m_norm_final, v_norm_ffn1, v_w1_gate, v_w1_up, v_w1_down, v_norm_mix, v_w_in, v_ssm_log_dt, v_ssm_a_re, v_ssm_a_im, v_ssm_b_re, v_ssm_b_im, v_ssm_c_re, v_ssm_c_im, v_ssm_d, v_ssm_w_glu, v_gmlp_norm_v, v_gmlp_w_s, v_gmlp_b_s, v_norm_ssm_out, v_norm_gmlp_out, v_w_out, v_norm_ffn2, v_w2_gate, v_w2_up, v_w2_down, v_norm_ple, v_w_ple_gate, v_w_ple_proj, v_norm_final):
    weights = dict(
        norm_ffn1=norm_ffn1, w1_gate=w1_gate, w1_up=w1_up, w1_down=w1_down, norm_mix=norm_mix, w_in=w_in,
        ssm_log_dt=ssm_log_dt, ssm_a_re=ssm_a_re, ssm_a_im=ssm_a_im, ssm_b_re=ssm_b_re, ssm_b_im=ssm_b_im,
        ssm_c_re=ssm_c_re, ssm_c_im=ssm_c_im, ssm_d=ssm_d, ssm_w_glu=ssm_w_glu, gmlp_norm_v=gmlp_norm_v,
        gmlp_w_s=gmlp_w_s, gmlp_b_s=gmlp_b_s, norm_ssm_out=norm_ssm_out, norm_gmlp_out=norm_gmlp_out, w_out=w_out,
        norm_ffn2=norm_ffn2, w2_gate=w2_gate, w2_up=w2_up, w2_down=w2_down, norm_ple=norm_ple,
        w_ple_gate=w_ple_gate, w_ple_proj=w_ple_proj, norm_final=norm_final)
    moments_m = dict(
        norm_ffn1=m_norm_ffn1, w1_gate=m_w1_gate, w1_up=m_w1_up, w1_down=m_w1_down, norm_mix=m_norm_mix, w_in=m_w_in,
        ssm_log_dt=m_ssm_log_dt, ssm_a_re=m_ssm_a_re, ssm_a_im=m_ssm_a_im, ssm_b_re=m_ssm_b_re, ssm_b_im=m_ssm_b_im,
        ssm_c_re=m_ssm_c_re, ssm_c_im=m_ssm_c_im, ssm_d=m_ssm_d, ssm_w_glu=m_ssm_w_glu, gmlp_norm_v=m_gmlp_norm_v,
        gmlp_w_s=m_gmlp_w_s, gmlp_b_s=m_gmlp_b_s, norm_ssm_out=m_norm_ssm_out, norm_gmlp_out=m_norm_gmlp_out,
        w_out=m_w_out, norm_ffn2=m_norm_ffn2, w2_gate=m_w2_gate, w2_up=m_w2_up, w2_down=m_w2_down,
        norm_ple=m_norm_ple, w_ple_gate=m_w_ple_gate, w_ple_proj=m_w_ple_proj, norm_final=m_norm_final)
    moments_v = dict(
        norm_ffn1=v_norm_ffn1, w1_gate=v_w1_gate, w1_up=v_w1_up, w1_down=v_w1_down, norm_mix=v_norm_mix, w_in=v_w_in,
        ssm_log_dt=v_ssm_log_dt, ssm_a_re=v_ssm_a_re, ssm_a_im=v_ssm_a_im, ssm_b_re=v_ssm_b_re, ssm_b_im=v_ssm_b_im,
        ssm_c_re=v_ssm_c_re, ssm_c_im=v_ssm_c_im, ssm_d=v_ssm_d, ssm_w_glu=v_ssm_w_glu, gmlp_norm_v=v_gmlp_norm_v,
        gmlp_w_s=v_gmlp_w_s, gmlp_b_s=v_gmlp_b_s, norm_ssm_out=v_norm_ssm_out, norm_gmlp_out=v_norm_gmlp_out,
        w_out=v_w_out, norm_ffn2=v_norm_ffn2, w2_gate=v_w2_gate, w2_up=v_w2_up, w2_down=v_w2_down,
        norm_ple=v_norm_ple, w_ple_gate=v_w_ple_gate, w_ple_proj=v_w_ple_proj, norm_final=v_norm_final)
    names = list(weights)

    xs = x[0]
    ps = p[0, 0].astype(BF16)
    tgt = loss_target[0]
    d_model = xs.shape[1]
    d_ssm = d_model // 2
    n_groups = d_ssm // SSM_GROUP

    transposed = ("w1_gate", "w1_up", "w2_gate", "w2_up")
    big = {
        "w1_gate": 0, "w1_up": 0, "w1_down": 0, "w_in": 1, "ssm_w_glu": 0, "w_out": 0,
        "w2_gate": 0, "w2_up": 0, "w2_down": 0, "w_ple_gate": 0, "w_ple_proj": 1}
    big_names = list(big)

    def view(a, k):
        return a[0].T if k in transposed else a[0]

    def unview(a, k):
        return a.T[None] if k in transposed else a[None]

    shard = {k: _pad_to(view(weights[k], k).astype(BF16), big[k], LANE) for k in big_names}
    W = {}

    abar_r, abar_i, bbar_r, bbar_i = _ssm_discretize(ssm_log_dt[0], ssm_a_re[0], ssm_a_im[0], ssm_b_re[0], ssm_b_im[0])
    bc_r = _block_diag(jnp.swapaxes(bbar_r, 1, 2)).astype(BF16)
    bc_i = _block_diag(jnp.swapaxes(bbar_i, 1, 2)).astype(BF16)
    cc_r = _block_diag(jnp.swapaxes(ssm_c_re[0], 1, 2)).astype(BF16)
    cc_i = _block_diag(jnp.swapaxes(ssm_c_im[0], 1, 2)).astype(BF16)
    apw_f = _scan_constants(abar_r, abar_i, False)
    apw_b = _scan_constants(abar_r, abar_i, True)
    causal = jnp.tril(jnp.ones((CHUNK, CHUNK), dtype=bool))
    wm = jnp.where(causal[None], gmlp_w_s[0], 0.0).astype(BF16)
    wmt = jnp.swapaxes(wm, 1, 2)
    bs = gmlp_b_s[0][:, :, None]

    groups = [["w1_gate"], ["w1_up"], ["w1_down"], ["w_in", "ssm_w_glu", "w_out"], ["w2_gate"], ["w2_up"],
              ["w2_down", "w_ple_gate", "w_ple_proj"]]
    order = [k for g in groups for k in g]
    place = {k: i for i, k in enumerate(order)}
    me = (4 * lax.axis_index("x") + 2 * lax.axis_index("y") + lax.axis_index("c")).astype(jnp.int32).reshape(1)
    size = {k: shard[k].shape[big[k]] for k in order}
    landing, sems = gather_start([place_own_block(shard[k], big[k], me, "place_" + k) for k in order],
                                 [big[k] for k in order], [size[k] for k in order],
                                 [[place[k] for k in g] for g in groups], "gather_start")

    def landed(g, after):
        axes_g, sizes_g = [big[k] for k in groups[g]], [size[k] for k in groups[g]]
        bufs = gather_wait([landing[place[k]] for k in groups[g]], axes_g, sizes_g, *sems[g], after,
                           "gather_wait_%d" % g)
        return forward_task(bufs, axes_g, sizes_g)

    def arrive(g, after):
        W.update(zip(groups[g], run_tasks([landed(g, after)], "gather_forward_%d" % g)[0]))

    def arrive_during(g, after, fn, *a, **kw):
        out, (got,) = fn(*a, tasks=[landed(g, after)], **kw)
        W.update(zip(groups[g], got))
        return out

    xn1 = rmsnorm_fwd(xs, norm_ffn1, "norm_ffn1")
    arrive(0, xn1)
    gate1 = matmul(xn1, W["w1_gate"], "nt", "ffn1_gate")
    arrive(1, gate1)
    gate1, up1, act1 = ffn_up(xn1, W["w1_up"], gate1, "ffn1_up")
    arrive(2, act1)
    h1 = matmul(act1, W["w1_down"], "nn", "ffn1_down", res=xs, scale=0.5)
    arrive(3, h1)
    xn2 = rmsnorm_fwd(h1, norm_mix, "norm_mix")
    z = matmul(xn2, W["w_in"], "nn", "proj_in")
    y_pre, yg, sr, si = s5_fwd(z, bc_r, bc_i, cc_r, cc_i, apw_f, ssm_d, "s5_fwd")
    glin = matmul(yg, W["ssm_w_glu"], "nn", "ssm_glu")
    y_gmlp = gmlp_fwd(z, gmlp_norm_v, wm, bs, "gmlp_fwd")
    ycat = mix_out_fwd(y_pre, glin, y_gmlp, norm_ssm_out, norm_gmlp_out, "mix_out")
    h2 = arrive_during(4, ycat, matmul, ycat, W["w_out"], "nn", "proj_out", res=h1)
    xn3 = rmsnorm_fwd(h2, norm_ffn2, "norm_ffn2")
    gate2 = arrive_during(5, xn3, matmul, xn3, W["w2_gate"], "nt", "ffn2_gate")
    gate2, up2, act2 = arrive_during(6, gate2, ffn_up, xn3, W["w2_up"], gate2, "ffn2_up")
    h3 = matmul(act2, W["w2_down"], "nn", "ffn2_down", res=h2, scale=0.5)
    xn4 = rmsnorm_fwd(h3, norm_ple, "norm_ple")
    pg_lin = matmul(xn4, W["w_ple_gate"], "nn", "ple_gate")
    pp = matmul(ps, W["w_ple_proj"], "nn", "ple_proj")
    h4 = ple_fwd(h3, pg_lin, pp, "ple_fwd")
    dh4, loss_part, g_norm_final = final_loss(h4, tgt, norm_final.reshape(1, -1), "final_loss")
    loss = lax.psum(loss_part[0, 0], ("x", "y", "c"))

    G = {}
    reduced = {}
    chip_part = {}
    wait_sibling, wait_chips = [], []
    core = lax.axis_index("c").astype(jnp.int32).reshape(1)

    def grad(name_, value):
        G[name_] = value
        wait_sibling.append(name_)

    def carry(fn, *a, levels="ab", extra=None, **kw):
        tasks, kinds = [], []
        if extra is not None:
            tasks.append(extra[0])
            kinds.append(("x", extra[1]))
        if "a" in levels and wait_sibling:
            group = list(wait_sibling)
            wait_sibling.clear()
            tasks.append(to_sibling_task([G[k] for k in group], [big[k] for k in group]))
            kinds.append(("a", group))
        if "b" in levels and wait_chips:
            group = list(wait_chips)
            wait_chips.clear()
            tasks.append(across_chips_task([chip_part[k] for k in group]))
            kinds.append(("b", group))
        if not tasks:
            return fn(*a, **kw)
        out, task_outs = fn(*a, tasks=tasks, **kw)
        for (kind, group), outs in zip(kinds, task_outs):
            if kind == "x":
                group(outs)
                continue
            for k, r in zip(group, outs):
                if kind == "a":
                    chip_part[k] = rs_chip_sum(G[k], r, big[k], core, "rs_sum_" + k)
                    wait_chips.append(k)
                else:
                    reduced[k] = r
        return out

    small = {}
    small["norm_final"] = g_norm_final
    dpp, dpg = ple_bwd(dh4, pg_lin, pp, "ple_bwd")
    grad("w_ple_proj", matmul(ps, dpp, "tn", "grad_ple_proj", out_dtype=BF16))
    grad("w_ple_gate", carry(matmul, xn4, dpg, "tn", "grad_ple_gate", out_dtype=BF16))
    dxn4 = carry(matmul, dpg, W["w_ple_gate"], "nt", "ple_gate_bwd")
    dh3, dh3b, small["norm_ple"] = rmsnorm_bwd(dxn4, h3, norm_ple, dh4, "norm_ple_bwd")

    def ffn_bwd(tag, dhb, xn, gate, up, act, wg, wu, wd, extra=None, last_levels="ab"):
        dgate, dup = carry(ffn_bwd_act, dhb, W[wd], gate, up, tag + "_act_bwd", extra=extra)
        grad(wd, carry(matmul, act, dhb, "tn", tag + "_grad_down", out_dtype=BF16, scale=0.5))
        grad(wg, carry(matmul, dgate, xn, "tn", tag + "_grad_gate", out_dtype=BF16))
        grad(wu, carry(matmul, dup, xn, "tn", tag + "_grad_up", out_dtype=BF16))
        dxn = carry(matmul, dgate, W[wg], "nn", tag + "_gate_bwd")
        return carry(matmul, dup, W[wu], "nn", tag + "_up_bwd", res=dxn, levels=last_levels)

    dxn3 = ffn_bwd("ffn2", dh3b, xn3, gate2, up2, act2, "w2_gate", "w2_up", "w2_down", last_levels="a")
    dh2, dh2b, small["norm_ffn2"] = rmsnorm_bwd(dxn3, h2, norm_ffn2, dh3, "norm_ffn2_bwd")

    grad("w_out", matmul(ycat, dh2b, "tn", "grad_out", out_dtype=BF16))
    dycat = carry(matmul, dh2b, W["w_out"], "nt", "proj_out_bwd", levels="a")
    dyg_direct, dglin, dy_gmlp, small["norm_ssm_out"], small["norm_gmlp_out"] = mix_out_bwd(
        dycat, y_pre, glin, y_gmlp, norm_ssm_out, norm_gmlp_out, "mix_out_bwd")
    grad("ssm_w_glu", matmul(yg, dglin, "tn", "grad_glu", out_dtype=BF16))
    dyg = carry(matmul, dglin, W["ssm_w_glu"], "nt", "ssm_glu_bwd", res=dyg_direct, levels="a")
    du, small["ssm_d"], gc_r, gc_i, gb_r, gb_i, ga_r, ga_i = carry(
        s5_bwd, dyg, y_pre, z, sr, si, bc_r, bc_i, cc_r, cc_i, apw_b, ssm_d, "s5_bwd")
    dzu, dzv, small["gmlp_norm_v"], g_wm, g_bs = gmlp_bwd(dy_gmlp, z, gmlp_norm_v, wm, wmt, bs, "gmlp_bwd")
    small["gmlp_w_s"] = g_wm
    small["gmlp_b_s"] = g_bs
    small["c_re"] = _block_diag_extract(gc_r, SSM_GROUP, SSM_STATE)
    small["c_im"] = _block_diag_extract(gc_i, SSM_GROUP, SSM_STATE)
    small["bbar_r"] = jnp.swapaxes(_block_diag_extract(gb_r, SSM_GROUP, SSM_STATE), 1, 2)
    small["bbar_i"] = jnp.swapaxes(_block_diag_extract(gb_i, SSM_GROUP, SSM_STATE), 1, 2)
    small["abar_r"] = jnp.sum(ga_r, axis=0).reshape(n_groups, SSM_STATE)
    small["abar_i"] = jnp.sum(ga_i, axis=0).reshape(n_groups, SSM_STATE)

    dz = jnp.concatenate([du, dzu, dzv], axis=1)
    grad("w_in", matmul(xn2, dz, "tn", "grad_in", out_dtype=BF16))
    dxn2 = carry(matmul, dz, W["w_in"], "nt", "proj_in_bwd")
    dh1, dh1b, small["norm_mix"] = rmsnorm_bwd(dxn2, h1, norm_mix, dh2, "norm_mix_bwd")

    def pack(parts):
        flat = jnp.concatenate([v.reshape(-1) for v in parts.values()])
        return _pad_to(flat, 0, SUBLANE * LANE).reshape(-1, LANE), flat.shape[0]

    def unpack(everyones, n, parts, tag):
        rows = everyones.shape[0] // N_DEV
        summed = sum_devices(everyones.reshape(N_DEV, rows, LANE), "sum_" + tag).reshape(-1)[:n]
        out, off = {}, 0
        for k, v in parts.items():
            out[k] = summed[off:off + v.size].reshape(v.shape)
            off += v.size
        return out

    early = dict(small)
    flat_early, n_early = pack(early)
    small_landed = []
    dxn1 = ffn_bwd("ffn1", dh1b, xn1, gate1, up1, act1, "w1_gate", "w1_up", "w1_down",
                   extra=(gather_task([flat_early], [0]), small_landed.extend))
    tot = unpack(small_landed[0], n_early, early, "small")
    grad_x, _, g_norm_ffn1 = rmsnorm_bwd(dxn1, xs, norm_ffn1, dh1, "norm_ffn1_bwd")
    assert not wait_sibling and not wait_chips and set(reduced) == set(big_names)
    last = {"norm_ffn1": g_norm_ffn1}
    flat_last, n_last = pack(last)
    ((everyones_last,),) = run_tasks([gather_task([flat_last], [0])], "gather_last")
    tot.update(unpack(everyones_last, n_last, last, "last"))

    out_g, out_d, out_m, out_v = {}, {}, {}, {}
    for k in big_names:
        g, dl, nm, nv = adam_sharded(reduced[k], view(weights[k], k), view(moments_m[k], k), view(moments_v[k], k),
                                     "adam_" + k)
        out_g[k], out_d[k], out_m[k], out_v[k] = unview(g, k), unview(dl, k), unview(nm, k), unview(nv, k)

    _, ssm_vjp = jax.vjp(_ssm_discretize, ssm_log_dt[0], ssm_a_re[0], ssm_a_im[0], ssm_b_re[0], ssm_b_im[0])
    g_log_dt, g_a_re, g_a_im, g_b_re, g_b_im = ssm_vjp((tot["abar_r"], tot["abar_i"], tot["bbar_r"], tot["bbar_i"]))
    small_grads = {
        "norm_ffn1": tot["norm_ffn1"], "norm_mix": tot["norm_mix"], "ssm_log_dt": g_log_dt, "ssm_a_re": g_a_re,
        "ssm_a_im": g_a_im, "ssm_b_re": g_b_re, "ssm_b_im": g_b_im, "ssm_c_re": tot["c_re"], "ssm_c_im": tot["c_im"],
        "ssm_d": tot["ssm_d"], "gmlp_norm_v": tot["gmlp_norm_v"],
        "gmlp_w_s": jnp.where(causal[None], tot["gmlp_w_s"], 0.0), "gmlp_b_s": tot["gmlp_b_s"],
        "norm_ssm_out": tot["norm_ssm_out"], "norm_gmlp_out": tot["norm_gmlp_out"], "norm_ffn2": tot["norm_ffn2"],
        "norm_ple": tot["norm_ple"], "norm_final": tot["norm_final"]}
    for k, g in small_grads.items():
        shp = weights[k].shape
        g2 = _as2d(g.reshape(shp))
        dl, nm, nv = adam_small(g2, _as2d(weights[k]), _as2d(moments_m[k]), _as2d(moments_v[k]), "adam_" + k)
        out_g[k], out_d[k], out_m[k], out_v[k] = g2.reshape(shp), dl.reshape(shp), nm.reshape(shp), nv.reshape(shp)

    return (loss, grad_x[None], *[out_g[k] for k in names], *[out_d[k] for k in names],
            *[out_m[k] for k in names], *[out_v[k] for k in names])
```

```python
import math

import jax
import jax.numpy as jnp
from jax import lax
from jax.experimental import pallas as pl
from jax.experimental.pallas import tpu as pltpu

F32 = jnp.float32
BF16 = jnp.bfloat16
MESH_DT = pl.DeviceIdType.MESH

N_DEV = 8
N_CHIP = 4
LANE = 128
SUBLANE = 8
VMEM_LIMIT = 60 * 1024 * 1024

EPS = 1e-6
SSM_GROUP = 16
SSM_STATE = 64
GROUPS_PER_BLOCK = LANE // SSM_GROUP
STATE_BLOCK = GROUPS_PER_BLOCK * SSM_STATE
GMLP_HEAD = 128
CHUNK = 128

ADAM_LR = 0.001
ADAM_B1 = 0.9
ADAM_B2 = 0.999
ADAM_EPS = 1e-08
ADAM_WD = 0.01
ADAM_STEP = 10

GELU_K = math.sqrt(2.0 / math.pi)
GELU_C = 0.044715


def _cparams():
    return pltpu.CompilerParams(vmem_limit_bytes=VMEM_LIMIT)


def _tile(n, pref):
    if n <= pref:
        return n
    t = (pref // LANE) * LANE
    while t > 0:
        if n % t == 0:
            return t
        t -= LANE
    return n


def _row_tile(n, pref):
    if n <= pref:
        return n
    t = (pref // SUBLANE) * SUBLANE
    while t > 0:
        if n % t == 0:
            return t
        t -= SUBLANE
    return n


def _gelu(x):
    t = jnp.tanh(GELU_K * (x + GELU_C * x * x * x))
    return 0.5 * x * (1.0 + t)


def _gelu_grad(x):
    t = jnp.tanh(GELU_K * (x + GELU_C * x * x * x))
    return 0.5 * (1.0 + t) + 0.5 * x * (1.0 - t * t) * (GELU_K * (1.0 + 3.0 * GELU_C * x * x))


def _sigmoid(x):
    return 0.5 * jnp.tanh(0.5 * x) + 0.5


_DN = {
    "nn": (((1,), (0,)), ((), ())),
    "nt": (((1,), (1,)), ((), ())),
    "tn": (((0,), (0,)), ((), ())),
}


def _dot(a, b, mode="nn"):
    return lax.dot_general(a, b, _DN[mode], preferred_element_type=F32)


class CommTask:
    def __init__(self, inputs, out_shape, n_sems, start, late, finish, in_place=False):
        self.inputs, self.out_shape, self.n_sems = list(inputs), list(out_shape), n_sems
        self.start, self.late, self.finish = start, late, finish
        self.in_place = in_place


def _task_aliases(tasks, first_in, first_out):
    aliases = {}
    for t in tasks:
        if t.in_place:
            aliases.update({first_in + i: first_out + i for i in range(len(t.inputs))})
        first_in += len(t.inputs)
        first_out += len(t.out_shape)
    return aliases


def _call(body, *, name, grid, in_specs, out_specs, out_shape, args, scratch_shapes=(), tasks=()):
    in_specs, out_specs, out_shape = list(in_specs), list(out_specs), list(out_shape)
    scratch_shapes = list(scratch_shapes)
    if not tasks:
        return pl.pallas_call(
            body, name=name, grid=grid, in_specs=in_specs, out_specs=out_specs, out_shape=out_shape,
            scratch_shapes=scratch_shapes, compiler_params=_cparams())(*args)
    n_in, n_out, n_scr = len(in_specs), len(out_specs), len(scratch_shapes)
    t_in = [len(t.inputs) for t in tasks]
    t_out = [len(t.out_shape) for t in tasks]
    late_step = grid[0] - max(1, grid[0] // 4)
    has_late = grid[0] >= 2

    def carried(*refs):
        pos = n_in
        task_ins = []
        for k in t_in:
            task_ins.append(refs[pos:pos + k])
            pos += k
        outs = refs[pos:pos + n_out]
        pos += n_out
        task_outs = []
        for k in t_out:
            task_outs.append(refs[pos:pos + k])
            pos += k
        scratch = refs[pos:pos + n_scr]
        pos += n_scr
        sems = [refs[pos + 3 * i:pos + 3 * i + 3] for i in range(len(tasks))]
        ids = [pl.program_id(d) for d in range(len(grid))]
        rest_zero = True
        for d in range(1, len(grid)):
            rest_zero = jnp.logical_and(rest_zero, ids[d] == 0)
        first = jnp.logical_and(ids[0] == 0, rest_zero)
        last = ids[0] == grid[0] - 1
        for d in range(1, len(grid)):
            last = jnp.logical_and(last, ids[d] == grid[d] - 1)

        @pl.when(first)
        def _():
            for t, ti, to, s in zip(tasks, task_ins, task_outs, sems):
                t.start(ti, to, *s)

        if has_late:
            @pl.when(jnp.logical_and(ids[0] == late_step, rest_zero))
            def _():
                for t, ti, to, s in zip(tasks, task_ins, task_outs, sems):
                    t.late(ti, to, *s)

        body(*refs[:n_in], *outs, *scratch)

        @pl.when(last)
        def _():
            for t, ti, to, s in zip(tasks, task_ins, task_outs, sems):
                if not has_late:
                    t.late(ti, to, *s)
                t.finish(ti, to, *s)

    any_spec = pl.BlockSpec(memory_space=pl.ANY)
    sem_shapes = [pltpu.SemaphoreType.DMA((n,)) for t in tasks for n in t.n_sems]
    res = pl.pallas_call(
        carried, name=name, grid=grid,
        in_specs=in_specs + [any_spec] * sum(t_in), out_specs=out_specs + [any_spec] * sum(t_out),
        out_shape=out_shape + [s for t in tasks for s in t.out_shape],
        input_output_aliases=_task_aliases(tasks, n_in, n_out),
        scratch_shapes=scratch_shapes + sem_shapes, compiler_params=_cparams(),
    )(*args, *[a for t in tasks for a in t.inputs])
    res = list(res)
    task_res, pos = [], n_out
    for k in t_out:
        task_res.append(res[pos:pos + k])
        pos += k
    return res[:n_out], task_res


def _mm_dims(a, b, mode):
    if mode == "nn":
        (m, k), (k2, n) = a.shape, b.shape
    elif mode == "nt":
        (m, k), (n, k2) = a.shape, b.shape
    else:
        (k, m), (k2, n) = a.shape, b.shape
    assert k == k2, (a.shape, b.shape, mode)
    return m, n, k


def _mm_specs(mode, tm, tn, tk):
    if mode == "tn":
        a_spec = pl.BlockSpec((tk, tm), lambda i, j, k: (k, i))
    else:
        a_spec = pl.BlockSpec((tm, tk), lambda i, j, k: (i, k))
    if mode == "nt":
        b_spec = pl.BlockSpec((tn, tk), lambda i, j, k: (j, k))
    else:
        b_spec = pl.BlockSpec((tk, tn), lambda i, j, k: (k, j))
    return a_spec, b_spec


def _accumulate(acc, nk, partial, emit):
    if nk == 1:
        emit(partial)
        return
    kk = pl.program_id(2)

    @pl.when(kk == 0)
    def _():
        acc[...] = partial

    @pl.when(kk > 0)
    def _():
        acc[...] += partial

    @pl.when(kk == nk - 1)
    def _():
        emit(acc[...])


def matmul(a, b, mode, name, out_dtype=F32, res=None, scale=1.0, tm=1024, tn=1024, tk=2048, tasks=()):
    m, n, k = _mm_dims(a, b, mode)
    tm, tn, tk = _tile(m, tm), _tile(n, tn), _tile(k, tk)
    nk = k // tk
    a_spec, b_spec = _mm_specs(mode, tm, tn, tk)
    o_spec = pl.BlockSpec((tm, tn), lambda i, j, k: (i, j))
    has_res = res is not None

    def body(*refs):
        if has_res:
            a_ref, b_ref, r_ref, o_ref, acc = refs
        else:
            a_ref, b_ref, o_ref, acc = refs

        def emit(v):
            if scale != 1.0:
                v = v * scale
            if has_res:
                v = r_ref[...] + v
            o_ref[...] = v.astype(out_dtype)

        _accumulate(acc, nk, _dot(a_ref[...], b_ref[...], mode), emit)

    out = _call(
        body, name=name, grid=(m // tm, n // tn, nk),
        in_specs=[a_spec, b_spec] + ([o_spec] if has_res else []), out_specs=[o_spec],
        out_shape=[jax.ShapeDtypeStruct((m, n), out_dtype)], args=(a, b) + ((res,) if has_res else ()),
        scratch_shapes=[pltpu.VMEM((tm, tn) if nk > 1 else (SUBLANE, LANE), F32)], tasks=tasks)
    return (out[0][0], out[1]) if tasks else out[0]


def ffn_up(xn, wu, gate, name, tm=1024, tn=1024, tk=2048, tasks=()):
    m, n, k = _mm_dims(xn, wu, "nt")
    tm, tn, tk = _tile(m, tm), _tile(n, tn), _tile(k, tk)
    nk = k // tk
    a_spec, b_spec = _mm_specs("nt", tm, tn, tk)
    o_spec = pl.BlockSpec((tm, tn), lambda i, j, k: (i, j))

    def body(a_ref, u_ref, gate_ref, gate_b_ref, up_b_ref, act_ref, acc):
        def emit(u):
            g = gate_ref[...]
            gate_b_ref[...] = g.astype(BF16)
            up_b_ref[...] = u.astype(BF16)
            act_ref[...] = (g * _sigmoid(g) * u).astype(BF16)

        _accumulate(acc, nk, _dot(a_ref[...], u_ref[...], "nt"), emit)

    out = _call(
        body, name=name, grid=(m // tm, n // tn, nk), in_specs=[a_spec, b_spec, o_spec],
        out_specs=[o_spec, o_spec, o_spec],
        out_shape=[jax.ShapeDtypeStruct((m, n), BF16), jax.ShapeDtypeStruct((m, n), BF16),
                   jax.ShapeDtypeStruct((m, n), BF16)],
        args=(xn, wu, gate), scratch_shapes=[pltpu.VMEM((tm, tn) if nk > 1 else (SUBLANE, LANE), F32)], tasks=tasks)
    return (tuple(out[0]), out[1]) if tasks else tuple(out)


def ffn_bwd_act(dh, wd, gate, up, name, tm=1024, tn=1024, tk=2048, tasks=()):
    m, n, k = _mm_dims(dh, wd, "nt")
    tm, tn, tk = _tile(m, tm), _tile(n, tn), _tile(k, tk)
    nk = k // tk
    a_spec, b_spec = _mm_specs("nt", tm, tn, tk)
    o_spec = pl.BlockSpec((tm, tn), lambda i, j, k: (i, j))

    def body(a_ref, b_ref, gate_ref, up_ref, dg_ref, du_ref, acc):
        def emit(total):
            dact = 0.5 * total
            g = gate_ref[...].astype(F32)
            sg = _sigmoid(g)
            du_ref[...] = (dact * (g * sg)).astype(BF16)
            dg_ref[...] = (dact * up_ref[...].astype(F32) * (sg * (1.0 + g * (1.0 - sg)))).astype(BF16)

        _accumulate(acc, nk, _dot(a_ref[...], b_ref[...], "nt"), emit)

    out = _call(
        body, name=name, grid=(m // tm, n // tn, nk), in_specs=[a_spec, b_spec, o_spec, o_spec],
        out_specs=[o_spec, o_spec],
        out_shape=[jax.ShapeDtypeStruct((m, n), BF16), jax.ShapeDtypeStruct((m, n), BF16)],
        args=(dh, wd, gate, up), scratch_shapes=[pltpu.VMEM((tm, tn) if nk > 1 else (SUBLANE, LANE), F32)],
        tasks=tasks)
    return (tuple(out[0]), out[1]) if tasks else tuple(out)


def _rows(t, d, tr):
    return pl.BlockSpec((tr, d), lambda i: (i, 0))


def _vec(d):
    return pl.BlockSpec((1, d), lambda i: (0, 0))


def rmsnorm_fwd(x, g, name, tr=512):
    t, d = x.shape
    tr = _row_tile(t, tr)

    def body(x_ref, g_ref, o_ref):
        xf = x_ref[...]
        r = lax.rsqrt(jnp.mean(xf * xf, axis=-1, keepdims=True) + EPS)
        o_ref[...] = (xf * r * g_ref[...]).astype(BF16)

    return pl.pallas_call(
        body, name=name, grid=(t // tr,), in_specs=[_rows(t, d, tr), _vec(d)], out_specs=_rows(t, d, tr),
        out_shape=jax.ShapeDtypeStruct((t, d), BF16), compiler_params=_cparams(),
    )(x, g)


def _rms_bwd(dxn, xf, g):
    r = lax.rsqrt(jnp.mean(xf * xf, axis=-1, keepdims=True) + EPS)
    xhat = xf * r
    dg = jnp.sum(dxn * xhat, axis=0, keepdims=True)
    dxh = dxn * g
    dx = r * (dxh - xhat * jnp.mean(dxh * xhat, axis=-1, keepdims=True))
    return dx, dg


def rmsnorm_bwd(dxn, x, g, dres, name, tr=256):
    t, d = x.shape
    tr = _row_tile(t, tr)

    def body(dxn_ref, x_ref, g_ref, dres_ref, o_ref, ob_ref, dg_ref):
        dx, dg = _rms_bwd(dxn_ref[...], x_ref[...], g_ref[...])
        out = dres_ref[...] + dx
        o_ref[...] = out
        ob_ref[...] = out.astype(BF16)

        @pl.when(pl.program_id(0) == 0)
        def _():
            dg_ref[...] = jnp.zeros_like(dg_ref)

        dg_ref[...] += dg

    return pl.pallas_call(
        body, name=name, grid=(t // tr,),
        in_specs=[_rows(t, d, tr), _rows(t, d, tr), _vec(d), _rows(t, d, tr)],
        out_specs=[_rows(t, d, tr), _rows(t, d, tr), _vec(d)],
        out_shape=[jax.ShapeDtypeStruct((t, d), F32), jax.ShapeDtypeStruct((t, d), BF16),
                   jax.ShapeDtypeStruct((1, d), F32)],
        compiler_params=_cparams(),
    )(dxn, x, g, dres)


def final_loss(h, target, g, name, tr=256):
    t, d = h.shape
    tr = _row_tile(t, tr)

    def body(h_ref, t_ref, g_ref, dh_ref, loss_ref, dg_ref):
        xf = h_ref[...]
        gg = g_ref[...]
        r = lax.rsqrt(jnp.mean(xf * xf, axis=-1, keepdims=True) + EPS)
        xhat = xf * r
        e = xhat * gg - t_ref[...]
        part = jnp.sum(jnp.sum(e * e, axis=1, keepdims=True), axis=0, keepdims=True) * (0.5 / d)
        dout = e * (1.0 / d)
        dg = jnp.sum(dout * xhat, axis=0, keepdims=True)
        dxh = dout * gg
        dh_ref[...] = r * (dxh - xhat * jnp.mean(dxh * xhat, axis=-1, keepdims=True))

        @pl.when(pl.program_id(0) == 0)
        def _():
            dg_ref[...] = jnp.zeros_like(dg_ref)
            loss_ref[...] = jnp.zeros_like(loss_ref)

        dg_ref[...] += dg
        loss_ref[...] += jnp.broadcast_to(part, loss_ref.shape)

    return pl.pallas_call(
        body, name=name, grid=(t // tr,),
        in_specs=[_rows(t, d, tr), _rows(t, d, tr), _vec(d)],
        out_specs=[_rows(t, d, tr), pl.BlockSpec((SUBLANE, LANE), lambda i: (0, 0)), _vec(d)],
        out_shape=[jax.ShapeDtypeStruct((t, d), F32), jax.ShapeDtypeStruct((SUBLANE, LANE), F32),
                   jax.ShapeDtypeStruct((1, d), F32)],
        compiler_params=_cparams(),
    )(h, target, g)


def ple_fwd(h, glin, pp, name, tr=512):
    t, d = h.shape
    tr = _row_tile(t, tr)

    def body(h_ref, gl_ref, pp_ref, o_ref):
        o_ref[...] = h_ref[...] + _sigmoid(gl_ref[...]) * pp_ref[...]

    sp = _rows(t, d, tr)
    return pl.pallas_call(
        body, name=name, grid=(t // tr,), in_specs=[sp, sp, sp], out_specs=sp,
        out_shape=jax.ShapeDtypeStruct((t, d), F32), compiler_params=_cparams(),
    )(h, glin, pp)


def ple_bwd(dh, glin, pp, name, tr=512):
    t, d = dh.shape
    tr = _row_tile(t, tr)

    def body(dh_ref, gl_ref, pp_ref, dpp_ref, dgl_ref):
        gate = _sigmoid(gl_ref[...])
        dh_ = dh_ref[...]
        dpp_ref[...] = (dh_ * gate).astype(BF16)
        dgl_ref[...] = (dh_ * pp_ref[...] * gate * (1.0 - gate)).astype(BF16)

    sp = _rows(t, d, tr)
    return pl.pallas_call(
        body, name=name, grid=(t // tr,), in_specs=[sp, sp, sp], out_specs=[sp, sp],
        out_shape=[jax.ShapeDtypeStruct((t, d), BF16), jax.ShapeDtypeStruct((t, d), BF16)],
        compiler_params=_cparams(),
    )(dh, glin, pp)


def mix_out_fwd(y_pre, glin, y_gmlp, g_so, g_go, name, tr=512):
    t, d = y_pre.shape
    tr = _row_tile(t, tr)

    def body(yp_ref, gl_ref, yg_ref, gs_ref, gg_ref, o_ref):
        ys = _gelu(yp_ref[...]) * _sigmoid(gl_ref[...])
        r = lax.rsqrt(jnp.mean(ys * ys, axis=-1, keepdims=True) + EPS)
        o_ref[:, 0:d] = (ys * r * gs_ref[...]).astype(BF16)
        yq = yg_ref[...]
        r2 = lax.rsqrt(jnp.mean(yq * yq, axis=-1, keepdims=True) + EPS)
        o_ref[:, d:2 * d] = (yq * r2 * gg_ref[...]).astype(BF16)

    sp = _rows(t, d, tr)
    return pl.pallas_call(
        body, name=name, grid=(t // tr,), in_specs=[sp, sp, sp, _vec(d), _vec(d)],
        out_specs=_rows(t, 2 * d, tr), out_shape=jax.ShapeDtypeStruct((t, 2 * d), BF16),
        compiler_params=_cparams(),
    )(y_pre, glin, y_gmlp, g_so, g_go)


def mix_out_bwd(dycat, y_pre, glin, y_gmlp, g_so, g_go, name, tr=256):
    t, d = y_pre.shape
    tr = _row_tile(t, tr)

    def body(dy_ref, yp_ref, gl_ref, yg_ref, gs_ref, gg_ref, dyg_ref, dl_ref, dyq_ref, dgs_ref, dgg_ref):
        yg = _gelu(yp_ref[...])
        sg = _sigmoid(gl_ref[...])
        dys, dgs = _rms_bwd(dy_ref[:, 0:d], yg * sg, gs_ref[...])
        dyg_ref[...] = dys * sg
        dl_ref[...] = (dys * yg * sg * (1.0 - sg)).astype(BF16)
        dyq, dgg = _rms_bwd(dy_ref[:, d:2 * d], yg_ref[...], gg_ref[...])
        dyq_ref[...] = dyq

        @pl.when(pl.program_id(0) == 0)
        def _():
            dgs_ref[...] = jnp.zeros_like(dgs_ref)
            dgg_ref[...] = jnp.zeros_like(dgg_ref)

        dgs_ref[...] += dgs
        dgg_ref[...] += dgg

    sp = _rows(t, d, tr)
    return pl.pallas_call(
        body, name=name, grid=(t // tr,),
        in_specs=[_rows(t, 2 * d, tr), sp, sp, sp, _vec(d), _vec(d)],
        out_specs=[sp, sp, sp, _vec(d), _vec(d)],
        out_shape=[jax.ShapeDtypeStruct((t, d), F32), jax.ShapeDtypeStruct((t, d), BF16),
                   jax.ShapeDtypeStruct((t, d), F32), jax.ShapeDtypeStruct((1, d), F32),
                   jax.ShapeDtypeStruct((1, d), F32)],
        compiler_params=_cparams(),
    )(dycat, y_pre, glin, y_gmlp, g_so, g_go)


SCAN_COLS = 512


def _scan_tile(xr, xi, const, cr, ci, reverse):
    for lvl, sh in enumerate((1, 2, 4)):
        ar, ai = const(2 * lvl), const(2 * lvl + 1)
        s = (SUBLANE - sh) if reverse else sh
        rr = pltpu.roll(xr, s, 0)
        ri = pltpu.roll(xi, s, 0)
        xr, xi = xr + ar * rr - ai * ri, xi + ar * ri + ai * rr
    pr, pi_ = const(6), const(7)
    xr, xi = xr + pr * cr - pi_ * ci, xi + pr * ci + pi_ * cr
    return xr, xi


def _bcast_row(x, row):
    return jnp.broadcast_to(x[row:row + 1, :], x.shape)


def s5_fwd(z, bc_r, bc_i, cc_r, cc_i, apw, dvec, name, tc=512, tasks=()):
    t = z.shape[0]
    nblk = bc_r.shape[0]
    d = nblk * LANE
    ns = nblk * STATE_BLOCK
    tc = _row_tile(t, tc)
    ntile = tc // SUBLANE

    def body(z_ref, br_ref, bi_ref, cr_ref, ci_ref, apw_ref, d_ref, y_ref, yg_ref, sr_ref, si_ref, carry):
        @pl.when(pl.program_id(0) == 0)
        def _():
            carry[...] = jnp.zeros_like(carry)

        for j in range(nblk):
            uj = z_ref[:, j * LANE:(j + 1) * LANE]
            ub = uj.astype(BF16)
            for q in range(STATE_BLOCK // SCAN_COLS):
                c0 = j * STATE_BLOCK + q * SCAN_COLS
                cs = pl.ds(c0, SCAN_COLS)
                bs = slice(q * SCAN_COLS, (q + 1) * SCAN_COLS)
                sr_ref[:, cs] = _dot(ub, br_ref[j, :, bs])
                si_ref[:, cs] = _dot(ub, bi_ref[j, :, bs])
                const = lambda k, cs=cs: apw_ref[k, :, cs]

                def tile(k, c, cs=cs, const=const):
                    rows = pl.ds(pl.multiple_of(k * SUBLANE, SUBLANE), SUBLANE)
                    xr, xi = _scan_tile(sr_ref[rows, cs], si_ref[rows, cs], const, c[0], c[1], False)
                    sr_ref[rows, cs] = xr
                    si_ref[rows, cs] = xi
                    return _bcast_row(xr, SUBLANE - 1), _bcast_row(xi, SUBLANE - 1)

                c_r, c_i = lax.fori_loop(0, ntile, tile, (carry[0, :, cs], carry[1, :, cs]))
                carry[0, :, cs] = c_r
                carry[1, :, cs] = c_i
            sb = pl.ds(j * STATE_BLOCK, STATE_BLOCK)
            y = (_dot(sr_ref[:, sb].astype(BF16), cr_ref[j]) - _dot(si_ref[:, sb].astype(BF16), ci_ref[j])
                 + d_ref[:, j * LANE:(j + 1) * LANE] * uj)
            y_ref[:, j * LANE:(j + 1) * LANE] = y
            yg_ref[:, j * LANE:(j + 1) * LANE] = _gelu(y).astype(BF16)

    full3 = lambda shp: pl.BlockSpec(shp, lambda i: (0, 0, 0))
    out = _call(
        body, name=name, grid=(t // tc,),
        in_specs=[pl.BlockSpec((tc, d), lambda i: (i, 0)), full3(bc_r.shape), full3(bc_i.shape),
                  full3(cc_r.shape), full3(cc_i.shape), full3(apw.shape), _vec(d)],
        out_specs=[pl.BlockSpec((tc, d), lambda i: (i, 0)), pl.BlockSpec((tc, d), lambda i: (i, 0)),
                   pl.BlockSpec((tc, ns), lambda i: (i, 0)), pl.BlockSpec((tc, ns), lambda i: (i, 0))],
        out_shape=[jax.ShapeDtypeStruct((t, d), F32), jax.ShapeDtypeStruct((t, d), BF16),
                   jax.ShapeDtypeStruct((t, ns), F32), jax.ShapeDtypeStruct((t, ns), F32)],
        args=(z, bc_r, bc_i, cc_r, cc_i, apw, dvec), scratch_shapes=[pltpu.VMEM((2, SUBLANE, ns), F32)], tasks=tasks)
    return (tuple(out[0]), out[1]) if tasks else tuple(out)


def s5_bwd(dyg, y_pre, z, sr, si, bc_r, bc_i, cc_r, cc_i, apw_rev, dvec, name, tc=256, tasks=()):
    t = z.shape[0]
    nblk = bc_r.shape[0]
    d = nblk * LANE
    ns = nblk * STATE_BLOCK
    tc = _row_tile(t, tc)
    ntile = tc // SUBLANE
    nchunk = t // tc
    tiles_per_chunk = tc // SUBLANE

    def body(dyg_ref, yp_ref, z_ref, sr_ref, si_ref, pr_ref, pi_ref, br_ref, bi_ref, cr_ref, ci_ref, apw_ref,
             d_ref, du_ref, gd_ref, gcr_ref, gci_ref, gbr_ref, gbi_ref, gar_ref, gai_ref, lr_ref, li_ref, carry):
        step = pl.program_id(0)

        @pl.when(step == 0)
        def _():
            carry[...] = jnp.zeros_like(carry)
            for ref in (gd_ref, gcr_ref, gci_ref, gbr_ref, gbi_ref, gar_ref, gai_ref):
                ref[...] = jnp.zeros_like(ref)

        first_chunk = (step == nchunk - 1).astype(F32)
        keep_prev = 1.0 - first_chunk
        row0 = lax.broadcasted_iota(jnp.int32, (SUBLANE, SCAN_COLS), 0) == 0

        for j in range(nblk):
            lanes = slice(j * LANE, (j + 1) * LANE)
            uj = z_ref[:, lanes]
            ub = uj.astype(BF16)
            gy = dyg_ref[:, lanes] * _gelu_grad(yp_ref[:, lanes])
            gyb = gy.astype(BF16)
            gd_ref[:, lanes] += jnp.sum(gy * uj, axis=0, keepdims=True)
            for q in range(STATE_BLOCK // SCAN_COLS):
                c0 = j * STATE_BLOCK + q * SCAN_COLS
                cs = pl.ds(c0, SCAN_COLS)
                bs = slice(q * SCAN_COLS, (q + 1) * SCAN_COLS)
                lr_ref[:, cs] = _dot(gyb, cr_ref[j, bs, :], "nt")
                li_ref[:, cs] = -_dot(gyb, ci_ref[j, bs, :], "nt")
                const = lambda k, cs=cs: apw_ref[k, :, cs]

                def one_tile(rows, prev_r, prev_i, c, cs=cs, const=const):
                    cr_, ci_, gar, gai = c
                    xr, xi = _scan_tile(lr_ref[rows, cs], li_ref[rows, cs], const, cr_, ci_, True)
                    lr_ref[rows, cs] = xr
                    li_ref[rows, cs] = xi
                    spr = jnp.where(row0, prev_r, pltpu.roll(sr_ref[rows, cs], 1, 0))
                    spi = jnp.where(row0, prev_i, pltpu.roll(si_ref[rows, cs], 1, 0))
                    gar = gar + xr * spr + xi * spi
                    gai = gai + xi * spr - xr * spi
                    return _bcast_row(xr, 0), _bcast_row(xi, 0), gar, gai

                def tile(k, c, cs=cs, one_tile=one_tile):
                    kk = ntile - 1 - k
                    rows = pl.ds(pl.multiple_of(kk * SUBLANE, SUBLANE), SUBLANE)
                    prow = pl.ds(pl.multiple_of((kk - 1) * SUBLANE, SUBLANE), SUBLANE)
                    prev_r = _bcast_row(sr_ref[prow, cs], SUBLANE - 1)
                    prev_i = _bcast_row(si_ref[prow, cs], SUBLANE - 1)
                    return one_tile(rows, prev_r, prev_i, c)

                zero = jnp.zeros((SUBLANE, SCAN_COLS), F32)
                c = lax.fori_loop(0, ntile - 1, tile, (carry[0, :, cs], carry[1, :, cs], zero, zero))
                prev_r = _bcast_row(pr_ref[:, cs], SUBLANE - 1) * keep_prev
                prev_i = _bcast_row(pi_ref[:, cs], SUBLANE - 1) * keep_prev
                c_r, c_i, gar, gai = one_tile(pl.ds(0, SUBLANE), prev_r, prev_i, c)
                carry[0, :, cs] = c_r
                carry[1, :, cs] = c_i
                gar_ref[:, cs] += gar
                gai_ref[:, cs] += gai
            sb = pl.ds(j * STATE_BLOCK, STATE_BLOCK)
            lrb = lr_ref[:, sb].astype(BF16)
            lib = li_ref[:, sb].astype(BF16)
            gcr_ref[j] += _dot(gyb, sr_ref[:, sb].astype(BF16), "tn")
            gci_ref[j] -= _dot(gyb, si_ref[:, sb].astype(BF16), "tn")
            gbr_ref[j] += _dot(ub, lrb, "tn")
            gbi_ref[j] += _dot(ub, lib, "tn")
            du = _dot(lrb, br_ref[j], "nt") + _dot(lib, bi_ref[j], "nt") + gy * d_ref[:, lanes]
            du_ref[:, lanes] = du.astype(BF16)

    rev = lambda i: (nchunk - 1 - i, 0)
    prev = lambda i: (jnp.maximum((nchunk - 1 - i) * tiles_per_chunk - 1, 0), 0)
    full3 = lambda shp: pl.BlockSpec(shp, lambda i: (0, 0, 0))
    acc3 = pl.BlockSpec((nblk, LANE, STATE_BLOCK), lambda i: (0, 0, 0))
    acc_rows = pl.BlockSpec((SUBLANE, ns), lambda i: (0, 0))
    out = _call(
        body, name=name, grid=(nchunk,),
        in_specs=[pl.BlockSpec((tc, d), rev), pl.BlockSpec((tc, d), rev), pl.BlockSpec((tc, d), rev),
                  pl.BlockSpec((tc, ns), rev), pl.BlockSpec((tc, ns), rev),
                  pl.BlockSpec((SUBLANE, ns), prev), pl.BlockSpec((SUBLANE, ns), prev),
                  full3(bc_r.shape), full3(bc_i.shape), full3(cc_r.shape), full3(cc_i.shape), full3(apw_rev.shape),
                  _vec(d)],
        out_specs=[pl.BlockSpec((tc, d), rev), _vec(d), acc3, acc3, acc3, acc3, acc_rows, acc_rows],
        out_shape=[jax.ShapeDtypeStruct((t, d), BF16), jax.ShapeDtypeStruct((1, d), F32)]
        + [jax.ShapeDtypeStruct((nblk, LANE, STATE_BLOCK), F32)] * 4
        + [jax.ShapeDtypeStruct((SUBLANE, ns), F32)] * 2,
        args=(dyg, y_pre, z, sr, si, sr, si, bc_r, bc_i, cc_r, cc_i, apw_rev, dvec),
        scratch_shapes=[pltpu.VMEM((tc, ns), F32), pltpu.VMEM((tc, ns), F32), pltpu.VMEM((2, SUBLANE, ns), F32)],
        tasks=tasks)
    return (tuple(out[0]), out[1]) if tasks else tuple(out)


def _cmul(a, b):
    return a[0] * b[0] - a[1] * b[1], a[0] * b[1] + a[1] * b[0]


def _scan_constants(abar_r, abar_i, reverse):
    ar = abar_r.reshape(1, -1)
    ai = abar_i.reshape(1, -1)
    if reverse:
        ai = -ai
    pw = [(ar, ai)]
    for _ in range(SUBLANE - 1):
        pw.append(_cmul(pw[-1], (ar, ai)))
    rows = lax.broadcasted_iota(jnp.int32, (SUBLANE, 1), 0)
    out = []
    for sh in (1, 2, 4):
        keep = (rows <= SUBLANE - 1 - sh) if reverse else (rows >= sh)
        for part in pw[sh - 1]:
            out.append(jnp.where(keep, part, 0.0))
    for comp in (0, 1):
        stack = jnp.concatenate([pw[k][comp] for k in range(SUBLANE)], axis=0)
        out.append(stack[::-1] if reverse else stack)
    return jnp.stack(out, axis=0).astype(F32)


def _ssm_discretize(log_dt, a_re, a_im, b_re, b_im):
    dt = jnp.exp(log_dt)[:, None]
    lr = jnp.minimum(a_re, -1e-4)
    li = a_im
    mag = jnp.exp(lr * dt)
    ang = li * dt
    abar_r = mag * jnp.cos(ang)
    abar_i = mag * jnp.sin(ang)
    den = lr * lr + li * li
    xr = abar_r - 1.0
    xi = abar_i
    zr = (xr * lr + xi * li) / den
    zi = (xi * lr - xr * li) / den
    bbar_r = zr[..., None] * b_re - zi[..., None] * b_im
    bbar_i = zr[..., None] * b_im + zi[..., None] * b_re
    return abar_r, abar_i, bbar_r, bbar_i


def _block_diag(w):
    g, a, b = w.shape
    nb = g // GROUPS_PER_BLOCK
    eye = jnp.eye(GROUPS_PER_BLOCK, dtype=w.dtype)
    w5 = w.reshape(nb, GROUPS_PER_BLOCK, a, b)
    out = w5[:, :, :, None, :] * eye[None, :, None, :, None]
    return out.reshape(nb, GROUPS_PER_BLOCK * a, GROUPS_PER_BLOCK * b)


def _block_diag_extract(m, a, b):
    nb = m.shape[0]
    eye = jnp.eye(GROUPS_PER_BLOCK, dtype=m.dtype)
    m5 = m.reshape(nb, GROUPS_PER_BLOCK, a, GROUPS_PER_BLOCK, b)
    out = jnp.sum(m5 * eye[None, :, None, :, None], axis=3)
    return out.reshape(nb * GROUPS_PER_BLOCK, a, b)


def _layer_norm(gv, nv):
    mu = jnp.mean(gv, axis=-1, keepdims=True)
    xc = gv - mu
    r = lax.rsqrt(jnp.mean(xc * xc, axis=-1, keepdims=True) + EPS)
    xhat = xc * r
    return xhat * nv, xhat, r


def gmlp_fwd(z, norm_v, wm, bs, name, tr=256):
    t = z.shape[0]
    nh = wm.shape[0]
    d = nh * GMLP_HEAD
    col0 = (z.shape[1] - 2 * d) // d
    tr = _row_tile(t, tr)

    def body(zu_ref, zv_ref, nv_ref, wm_ref, bs_ref, o_ref):
        v, _, _ = _layer_norm(_gelu(zv_ref[...]), nv_ref[...])
        vb = v.astype(BF16)
        u = _gelu(zu_ref[...])
        for c in range(tr // CHUNK):
            rows = slice(c * CHUNK, (c + 1) * CHUNK)
            for h in range(nh):
                cols = slice(h * GMLP_HEAD, (h + 1) * GMLP_HEAD)
                s = _dot(wm_ref[h], vb[rows, cols]) + bs_ref[h]
                o_ref[rows, cols] = u[rows, cols] * s

    return pl.pallas_call(
        body, name=name, grid=(t // tr,),
        in_specs=[pl.BlockSpec((tr, d), lambda i: (i, col0)), pl.BlockSpec((tr, d), lambda i: (i, col0 + 1)),
                  _vec(d), pl.BlockSpec(wm.shape, lambda i: (0, 0, 0)), pl.BlockSpec(bs.shape, lambda i: (0, 0, 0))],
        out_specs=pl.BlockSpec((tr, d), lambda i: (i, 0)),
        out_shape=jax.ShapeDtypeStruct((t, d), F32), compiler_params=_cparams(),
    )(z, z, norm_v, wm, bs)


def gmlp_bwd(dy, z, norm_v, wm, wmt, bs, name, tr=256):
    t = z.shape[0]
    nh = wm.shape[0]
    d = nh * GMLP_HEAD
    col0 = (z.shape[1] - 2 * d) // d
    tr = _row_tile(t, tr)

    def body(dy_ref, zu_ref, zv_ref, nv_ref, wm_ref, wmt_ref, bs_ref, dzu_ref, dzv_ref, dnv_ref, dwm_ref, dbs_ref,
             dv_ref):
        @pl.when(pl.program_id(0) == 0)
        def _():
            dnv_ref[...] = jnp.zeros_like(dnv_ref)
            dwm_ref[...] = jnp.zeros_like(dwm_ref)
            dbs_ref[...] = jnp.zeros_like(dbs_ref)

        zv = zv_ref[...]
        nv = nv_ref[...]
        v, xhat, r = _layer_norm(_gelu(zv), nv)
        vb = v.astype(BF16)
        zu = zu_ref[...]
        u = _gelu(zu)
        dy_ = dy_ref[...]
        for c in range(tr // CHUNK):
            rows = slice(c * CHUNK, (c + 1) * CHUNK)
            for h in range(nh):
                cols = slice(h * GMLP_HEAD, (h + 1) * GMLP_HEAD)
                vh = vb[rows, cols]
                s = _dot(wm_ref[h], vh) + bs_ref[h]
                dyh = dy_[rows, cols]
                dzu_ref[rows, cols] = (dyh * s * _gelu_grad(zu[rows, cols])).astype(BF16)
                ds = dyh * u[rows, cols]
                dsb = ds.astype(BF16)
                dbs_ref[h] += jnp.sum(ds, axis=1, keepdims=True)
                dwm_ref[h] += _dot(dsb, vh, "nt")
                dv_ref[rows, cols] = _dot(wmt_ref[h], dsb)
        dv = dv_ref[...]
        dnv_ref[...] += jnp.sum(dv * xhat, axis=0, keepdims=True)
        dxh = dv * nv
        dgv = r * (dxh - jnp.mean(dxh, axis=-1, keepdims=True) - xhat * jnp.mean(dxh * xhat, axis=-1, keepdims=True))
        dzv_ref[...] = (dgv * _gelu_grad(zv)).astype(BF16)

    full3 = lambda shp: pl.BlockSpec(shp, lambda i: (0, 0, 0))
    rows_d = pl.BlockSpec((tr, d), lambda i: (i, 0))
    return pl.pallas_call(
        body, name=name, grid=(t // tr,),
        in_specs=[rows_d, pl.BlockSpec((tr, d), lambda i: (i, col0)), pl.BlockSpec((tr, d), lambda i: (i, col0 + 1)),
                  _vec(d), full3(wm.shape), full3(wmt.shape), full3(bs.shape)],
        out_specs=[rows_d, rows_d, _vec(d), full3((nh, CHUNK, CHUNK)), full3((nh, CHUNK, 1))],
        out_shape=[jax.ShapeDtypeStruct((t, d), BF16), jax.ShapeDtypeStruct((t, d), BF16),
                   jax.ShapeDtypeStruct((1, d), F32), jax.ShapeDtypeStruct((nh, CHUNK, CHUNK), F32),
                   jax.ShapeDtypeStruct((nh, CHUNK, 1), F32)],
        scratch_shapes=[pltpu.VMEM((tr, d), F32)], compiler_params=_cparams(),
    )(dy, z, z, norm_v, wm, wmt, bs)


def _block(ref, axis, size, k):
    start = pl.multiple_of(k * size, size)
    if axis == 0:
        return ref.at[pl.ds(start, size), :]
    return ref.at[:, pl.ds(start, size)]


def _place():
    x, y, c = lax.axis_index("x"), lax.axis_index("y"), lax.axis_index("c")
    chips = [(1 - x, y), (x, 1 - y), (1 - x, 1 - y)]
    return x, y, c, chips


def _dev(x, y, c):
    return 4 * x + 2 * y + c


def gather_task(shards, axes):
    n = len(shards)
    sizes = [s.shape[ax] for s, ax in zip(shards, axes)]
    out_shape = [
        jax.ShapeDtypeStruct((s.shape[0] * N_DEV, s.shape[1]) if ax == 0 else (s.shape[0], s.shape[1] * N_DEV), s.dtype)
        for s, ax in zip(shards, axes)
    ]

    def copy(ins, outs, send_sems, recv_sems, t, k, block, to, from_input=False):
        dst = _block(outs[t], axes[t], sizes[t], _dev(*block))
        return pltpu.make_async_remote_copy(
            src_ref=ins[t] if from_input else dst, dst_ref=dst,
            send_sem=send_sems.at[t * 7 + k], recv_sem=recv_sems.at[t * 7 + k],
            device_id=to, device_id_type=MESH_DT)

    def local(ins, outs, local_sems, t, me):
        return pltpu.make_async_copy(ins[t], _block(outs[t], axes[t], sizes[t], _dev(*me)), local_sems.at[t])

    def start(ins, outs, send_sems, recv_sems, local_sems):
        x, y, c, chips = _place()
        me, sibling = (x, y, c), (x, y, 1 - c)
        for t in range(n):
            local(ins, outs, local_sems, t, me).start()
        for t in range(n):
            copy(ins, outs, send_sems, recv_sems, t, 0, me, sibling, True).start()
            for j, chip in enumerate(chips):
                copy(ins, outs, send_sems, recv_sems, t, 1 + j, me, (*chip, c), True).start()

    def late(ins, outs, send_sems, recv_sems, local_sems):
        x, y, c, chips = _place()
        me, sibling = (x, y, c), (x, y, 1 - c)
        for t in range(n):
            for j, chip in enumerate(chips):
                copy(ins, outs, send_sems, recv_sems, t, 1 + j, (*chip, c), me).wait_recv()
                copy(ins, outs, send_sems, recv_sems, t, 4 + j, (*chip, c), sibling).start()

    def finish(ins, outs, send_sems, recv_sems, local_sems):
        x, y, c, chips = _place()
        me, sibling = (x, y, c), (x, y, 1 - c)
        for t in range(n):
            copy(ins, outs, send_sems, recv_sems, t, 0, sibling, me).wait_recv()
            for j, chip in enumerate(chips):
                copy(ins, outs, send_sems, recv_sems, t, 4 + j, (*chip, 1 - c), me).wait_recv()
        for t in range(n):
            copy(ins, outs, send_sems, recv_sems, t, 0, me, sibling, True).wait_send()
            for j, chip in enumerate(chips):
                copy(ins, outs, send_sems, recv_sems, t, 1 + j, me, (*chip, c), True).wait_send()
                copy(ins, outs, send_sems, recv_sems, t, 4 + j, (*chip, c), sibling).wait_send()
            local(ins, outs, local_sems, t, me).wait()

    return CommTask(shards, out_shape, (7 * n, 7 * n, n), start, late, finish)


def _blk3(shape2, axis):
    r, c = shape2
    return (r // N_DEV, c) if axis == 0 else (r, c // N_DEV)


def _no_late(ins, outs, send_sems, recv_sems, local_sems):
    pass


def to_sibling_task(grads, axes):
    n = len(grads)
    blks = [_blk3(g.shape, ax) for g, ax in zip(grads, axes)]
    sizes = [b[ax] for b, ax in zip(blks, axes)]

    def copies(ins, outs, send_sems, recv_sems):
        x, y, c, _ = _place()
        return [pltpu.make_async_remote_copy(
            src_ref=_block(ins[t], axes[t], sizes[t], 2 * i + (1 - c)), dst_ref=outs[t].at[i],
            send_sem=send_sems.at[t * N_CHIP + i], recv_sem=recv_sems.at[t * N_CHIP + i],
            device_id=(x, y, 1 - c), device_id_type=MESH_DT) for t in range(n) for i in range(N_CHIP)]

    def start(ins, outs, send_sems, recv_sems, local_sems):
        for cp in copies(ins, outs, send_sems, recv_sems):
            cp.start()

    def finish(ins, outs, send_sems, recv_sems, local_sems):
        cps = copies(ins, outs, send_sems, recv_sems)
        for cp in cps:
            cp.wait_recv()
        for cp in cps:
            cp.wait_send()

    out_shape = [jax.ShapeDtypeStruct((N_CHIP,) + b, g.dtype) for b, g in zip(blks, grads)]
    return CommTask(grads, out_shape, (N_CHIP * n, N_CHIP * n, 1), start, _no_late, finish)


def across_chips_task(parts):
    n = len(parts)

    def copies(ins, outs, send_sems, recv_sems):
        x, y, c, chips = _place()
        my_chip = 2 * x + y
        return [pltpu.make_async_remote_copy(
            src_ref=ins[t].at[2 * chip[0] + chip[1]], dst_ref=outs[t].at[my_chip],
            send_sem=send_sems.at[t * 3 + j], recv_sem=recv_sems.at[t * 3 + j],
            device_id=(*chip, c), device_id_type=MESH_DT) for t in range(n) for j, chip in enumerate(chips)]

    def mine(ins, outs, local_sems):
        x, y, _, _ = _place()
        my_chip = 2 * x + y
        return [pltpu.make_async_copy(ins[t].at[my_chip], outs[t].at[my_chip], local_sems.at[t]) for t in range(n)]

    def start(ins, outs, send_sems, recv_sems, local_sems):
        for cp in mine(ins, outs, local_sems):
            cp.start()
        for cp in copies(ins, outs, send_sems, recv_sems):
            cp.start()

    def finish(ins, outs, send_sems, recv_sems, local_sems):
        cps = copies(ins, outs, send_sems, recv_sems)
        for cp in cps:
            cp.wait_recv()
        for cp in cps:
            cp.wait_send()
        for cp in mine(ins, outs, local_sems):
            cp.wait()

    out_shape = [jax.ShapeDtypeStruct(p.shape, p.dtype) for p in parts]
    return CommTask(parts, out_shape, (3 * n, 3 * n, n), start, _no_late, finish)


def run_tasks(tasks, name):
    t_in = [len(t.inputs) for t in tasks]
    t_out = [len(t.out_shape) for t in tasks]

    def body(*refs):
        pos, views = 0, []
        for k in t_in:
            views.append([refs[pos:pos + k]])
            pos += k
        for v, k in zip(views, t_out):
            v.append(refs[pos:pos + k])
            pos += k
        for i, v in enumerate(views):
            v.extend(refs[pos + 3 * i:pos + 3 * i + 3])
        for phase in ("start", "late", "finish"):
            for t, v in zip(tasks, views):
                getattr(t, phase)(*v)

    any_spec = pl.BlockSpec(memory_space=pl.ANY)
    res = pl.pallas_call(
        body, name=name, in_specs=[any_spec] * sum(t_in), out_specs=[any_spec] * sum(t_out),
        out_shape=[s for t in tasks for s in t.out_shape], input_output_aliases=_task_aliases(tasks, 0, 0),
        scratch_shapes=[pltpu.SemaphoreType.DMA((k,)) for t in tasks for k in t.n_sems],
    )(*[a for t in tasks for a in t.inputs])
    res, out, pos = list(res), [], 0
    for k in t_out:
        out.append(res[pos:pos + k])
        pos += k
    return out


_HBM_SPEC = pl.BlockSpec(memory_space=pl.ANY)
_SEM_SPEC = pl.BlockSpec(memory_space=pltpu.SEMAPHORE)
_DATAFLOW = pltpu.SideEffectType.DATAFLOW_SIDE_EFFECTING


def _full_shape(s, ax):
    return (s.shape[0] * N_DEV, s.shape[1]) if ax == 0 else (s.shape[0], s.shape[1] * N_DEV)


def _level1_copy(src, landing, axis, size, send_sems, recv_sems, slot, sender, to):
    dst = _block(landing, axis, size, _dev(*sender))
    return pltpu.make_async_remote_copy(src_ref=src, dst_ref=dst, send_sem=send_sems.at[slot],
                                        recv_sem=recv_sems.at[slot], device_id=to, device_id_type=MESH_DT)


def place_own_block(shard, axis, me, name, tr=256):
    r, c = shard.shape
    tr = _row_tile(r, tr)
    nrb = r // tr
    if axis == 0:
        o_map = lambda i, me_ref: (me_ref[0] * nrb + i, 0)
    else:
        o_map = lambda i, me_ref: (i, me_ref[0])

    def body(me_ref, x_ref, o_ref):
        o_ref[...] = x_ref[...]

    return pl.pallas_call(
        body, name=name,
        grid_spec=pltpu.PrefetchScalarGridSpec(
            num_scalar_prefetch=1, grid=(nrb,), in_specs=[pl.BlockSpec((tr, c), lambda i, me_ref: (i, 0))],
            out_specs=pl.BlockSpec((tr, c), o_map)),
        out_shape=jax.ShapeDtypeStruct(_full_shape(shard, axis), shard.dtype), compiler_params=_cparams(),
    )(me, shard)


def gather_start(landing, axes, sizes, groups, name):
    n = len(landing)

    def body(*refs):
        lands, sems = refs[:n], refs[2 * n:]
        x, y, c, chips = _place()
        me = (x, y, c)
        targets = [(x, y, 1 - c)] + [(*chip, c) for chip in chips]
        for g, members in enumerate(groups):
            for m, t in enumerate(members):
                own = _block(lands[t], axes[t], sizes[t], _dev(*me))
                for k, to in enumerate(targets):
                    _level1_copy(own, lands[t], axes[t], sizes[t], sems[2 * g], sems[2 * g + 1], 4 * m + k,
                                 me, to).start()

    out = pl.pallas_call(
        body, name=name,
        out_shape=[jax.ShapeDtypeStruct(b.shape, b.dtype) for b in landing]
        + [pltpu.SemaphoreType.DMA((4 * len(members),)) for members in groups for _ in (0, 1)],
        in_specs=[_HBM_SPEC] * n, out_specs=[_HBM_SPEC] * n + [_SEM_SPEC] * (2 * len(groups)),
        input_output_aliases={i: i for i in range(n)},
        compiler_params=pltpu.CompilerParams(has_side_effects=_DATAFLOW),
    )(*landing)
    out = list(out)
    sems = out[n:]
    return out[:n], [(sems[2 * g], sems[2 * g + 1]) for g in range(len(groups))]


def gather_wait(landing, axes, sizes, send_sems, recv_sems, after, name):
    n = len(landing)

    def body(*refs):
        lands = refs[:n]
        send, recv = refs[n], refs[n + 1]
        x, y, c, chips = _place()
        me = (x, y, c)
        peers = [(x, y, 1 - c)] + [(*chip, c) for chip in chips]
        for t in range(n):
            own = _block(lands[t], axes[t], sizes[t], _dev(*me))
            for k, peer in enumerate(peers):
                _level1_copy(own, lands[t], axes[t], sizes[t], send, recv, 4 * t + k, me, peer).wait_send()
                _level1_copy(own, lands[t], axes[t], sizes[t], send, recv, 4 * t + k, peer, me).wait_recv()

    out = pl.pallas_call(
        body, name=name, out_shape=[jax.ShapeDtypeStruct(b.shape, b.dtype) for b in landing],
        in_specs=[_HBM_SPEC] * n + [_SEM_SPEC, _SEM_SPEC, pl.BlockSpec(memory_space=pl.ANY)],
        out_specs=[_HBM_SPEC] * n, input_output_aliases={i: i for i in range(n)},
        compiler_params=pltpu.CompilerParams(has_side_effects=_DATAFLOW),
    )(*landing, send_sems, recv_sems, after)
    return list(out)


def forward_task(landing, axes, sizes):
    n = len(landing)

    def forward(lands, send_sems, recv_sems, t, j, chip_core):
        x, y, c, _ = _place()
        blk = _block(lands[t], axes[t], sizes[t], _dev(*chip_core))
        return pltpu.make_async_remote_copy(src_ref=blk, dst_ref=blk, send_sem=send_sems.at[3 * t + j],
                                            recv_sem=recv_sems.at[3 * t + j], device_id=(x, y, 1 - c),
                                            device_id_type=MESH_DT)

    def start(ins, lands, send_sems, recv_sems, local_sems):
        _, _, c, chips = _place()
        for t in range(n):
            for j, chip in enumerate(chips):
                forward(lands, send_sems, recv_sems, t, j, (*chip, c)).start()

    def finish(ins, lands, send_sems, recv_sems, local_sems):
        _, _, c, chips = _place()
        for t in range(n):
            for j, chip in enumerate(chips):
                forward(lands, send_sems, recv_sems, t, j, (*chip, 1 - c)).wait_recv()
        for t in range(n):
            for j, chip in enumerate(chips):
                forward(lands, send_sems, recv_sems, t, j, (*chip, c)).wait_send()

    out_shape = [jax.ShapeDtypeStruct(b.shape, b.dtype) for b in landing]
    return CommTask(landing, out_shape, (3 * n, 3 * n, 1), start, _no_late, finish, in_place=True)


def rs_chip_sum(grad, recv, axis, core, name, tr=512):
    br, bc = _blk3(grad.shape, axis)
    tr = _row_tile(br, tr)
    nrb = br // tr

    if axis == 0:
        g_map = lambda i, r, c_ref: ((2 * i + c_ref[0]) * nrb + r, 0)
    else:
        g_map = lambda i, r, c_ref: (r, 2 * i + c_ref[0])

    def body(c_ref, g_ref, r_ref, o_ref):
        o_ref[...] = (g_ref[...].astype(F32) + r_ref[...].astype(F32)).astype(BF16)

    return pl.pallas_call(
        body, name=name,
        grid_spec=pltpu.PrefetchScalarGridSpec(
            num_scalar_prefetch=1, grid=(N_CHIP, nrb),
            in_specs=[pl.BlockSpec((tr, bc), g_map), pl.BlockSpec((None, tr, bc), lambda i, r, c_ref: (i, r, 0))],
            out_specs=pl.BlockSpec((None, tr, bc), lambda i, r, c_ref: (i, r, 0))),
        out_shape=jax.ShapeDtypeStruct((N_CHIP, br, bc), BF16), compiler_params=_cparams(),
    )(core, grad, recv)


def _adamw(w, g, m, v):
    m = ADAM_B1 * m + (1.0 - ADAM_B1) * g
    v = ADAM_B2 * v + (1.0 - ADAM_B2) * (g * g)
    m_hat = m / (1.0 - ADAM_B1 ** ADAM_STEP)
    v_hat = v / (1.0 - ADAM_B2 ** ADAM_STEP)
    delta = -ADAM_LR * (m_hat / (jnp.sqrt(v_hat) + ADAM_EPS) + ADAM_WD * w)
    return delta, m, v


def _sum_chips(p_ref):
    g = p_ref[0].astype(F32)
    for i in range(1, N_CHIP):
        g = g + p_ref[i].astype(F32)
    return g


def adam_sharded(parts, w, m, v, name, tr=256):
    r, c = w.shape
    assert parts.shape[2] == c
    tr = _row_tile(r, tr)

    def body(p_ref, w_ref, m_ref, v_ref, g_ref, d_ref, nm_ref, nv_ref):
        g = _sum_chips(p_ref)
        delta, nm, nv = _adamw(w_ref[...], g, m_ref[...], v_ref[...])
        g_ref[...] = g
        d_ref[...] = delta
        nm_ref[...] = nm
        nv_ref[...] = nv

    sp = pl.BlockSpec((tr, c), lambda i: (i, 0))
    return pl.pallas_call(
        body, name=name, grid=(r // tr,),
        in_specs=[pl.BlockSpec((N_CHIP, tr, c), lambda i: (0, i, 0)), sp, sp, sp],
        out_specs=[sp, sp, sp, sp], out_shape=[jax.ShapeDtypeStruct((r, c), F32)] * 4,
        compiler_params=_cparams(),
    )(parts, w, m, v)


def adam_small(items, name):
    n = len(items)

    def body(*refs):
        for i in range(n):
            g_ref, w_ref, m_ref, v_ref = refs[4 * i:4 * i + 4]
            d_ref, nm_ref, nv_ref = refs[4 * n + 3 * i:4 * n + 3 * i + 3]
            delta, nm, nv = _adamw(w_ref[...], g_ref[...], m_ref[...], v_ref[...])
            d_ref[...] = delta
            nm_ref[...] = nm
            nv_ref[...] = nv

    out = pl.pallas_call(
        body, name=name, out_shape=[jax.ShapeDtypeStruct(it[1].shape, F32) for it in items for _ in range(3)],
        compiler_params=_cparams(),
    )(*[a for it in items for a in it])
    return [tuple(out[3 * i:3 * i + 3]) for i in range(n)]


def sum_devices(gathered, name, tr=512):
    _, r, c = gathered.shape
    tr = _row_tile(r, tr)

    def body(x_ref, o_ref):
        s = x_ref[0]
        for k in range(1, N_DEV):
            s = s + x_ref[k]
        o_ref[...] = s

    return pl.pallas_call(
        body, name=name, grid=(r // tr,), in_specs=[pl.BlockSpec((N_DEV, tr, c), lambda i: (0, i, 0))],
        out_specs=pl.BlockSpec((tr, c), lambda i: (i, 0)), out_shape=jax.ShapeDtypeStruct((r, c), F32),
        compiler_params=_cparams(),
    )(gathered)


def _pad_to(a, axis, mult):
    size = a.shape[axis]
    pad = (-size) % mult
    if pad == 0:
        return a
    cfg = [(0, 0)] * a.ndim
    cfg[axis] = (0, pad)
    return jnp.pad(a, cfg)


def _as2d(a):
    if a.ndim == 1:
        return a.reshape(1, -1)
    return a.reshape(-1, a.shape[-1])


def kernel(x, p, norm_ffn1, w1_gate, w1_up, w1_down, norm_mix, w_in, ssm_log_dt, ssm_a_re, ssm_a_im, ssm_b_re, ssm_b_im, ssm_c_re, ssm_c_im, ssm_d, ssm_w_glu, gmlp_norm_v, gmlp_w_s, gmlp_b_s, norm_ssm_out, norm_gmlp_out, w_out, norm_ffn2, w2_gate, w2_up, w2_down, norm_ple, w_ple_gate, w_ple_proj, norm_final, loss_target, m_norm_ffn1, m_w1_gate, m_w1_up, m_w1_down, m_norm_mix, m_w_in, m_ssm_log_dt, m_ssm_a_re, m_ssm_a_im, m_ssm_b_re, m_ssm_b_im, m_ssm_c_re, m_ssm_c_im, m_ssm_d, m_ssm_w_glu, m_gmlp_norm_v, m_gmlp_w_s, m_gmlp_b_s, m_norm_ssm_out, m_norm_gmlp_out, m_w_out, m_norm_ffn2, m_w2_gate, m_w2_up, m_w2_down, m_norm_ple, m_w_ple_gate, m_w_ple_proj, m_norm_final, v_norm_ffn1, v_w1_gate, v_w1_up, v_w1_down, v_norm_mix, v_w_in, v_ssm_log_dt, v_ssm_a_re, v_ssm_a_im, v_ssm_b_re, v_ssm_b_im, v_ssm_c_re, v_ssm_c_im, v_ssm_d, v_ssm_w_glu, v_gmlp_norm_v, v_gmlp_w_s, v_gmlp_b_s, v_norm_ssm_out, v_norm_gmlp_out, v_w_out, v_norm_ffn2, v_w2_gate, v_w2_up, v_w2_down, v_norm_ple, v_w_ple_gate, v_w_ple_proj, v_norm_final):
    weights = dict(
        norm_ffn1=norm_ffn1, w1_gate=w1_gate, w1_up=w1_up, w1_down=w1_down, norm_mix=norm_mix, w_in=w_in,
        ssm_log_dt=ssm_log_dt, ssm_a_re=ssm_a_re, ssm_a_im=ssm_a_im, ssm_b_re=ssm_b_re, ssm_b_im=ssm_b_im,
        ssm_c_re=ssm_c_re, ssm_c_im=ssm_c_im, ssm_d=ssm_d, ssm_w_glu=ssm_w_glu, gmlp_norm_v=gmlp_norm_v,
        gmlp_w_s=gmlp_w_s, gmlp_b_s=gmlp_b_s, norm_ssm_out=norm_ssm_out, norm_gmlp_out=norm_gmlp_out, w_out=w_out,
        norm_ffn2=norm_ffn2, w2_gate=w2_gate, w2_up=w2_up, w2_down=w2_down, norm_ple=norm_ple,
        w_ple_gate=w_ple_gate, w_ple_proj=w_ple_proj, norm_final=norm_final)
    moments_m = dict(
        norm_ffn1=m_norm_ffn1, w1_gate=m_w1_gate, w1_up=m_w1_up, w1_down=m_w1_down, norm_mix=m_norm_mix, w_in=m_w_in,
        ssm_log_dt=m_ssm_log_dt, ssm_a_re=m_ssm_a_re, ssm_a_im=m_ssm_a_im, ssm_b_re=m_ssm_b_re, ssm_b_im=m_ssm_b_im,
        ssm_c_re=m_ssm_c_re, ssm_c_im=m_ssm_c_im, ssm_d=m_ssm_d, ssm_w_glu=m_ssm_w_glu, gmlp_norm_v=m_gmlp_norm_v,
        gmlp_w_s=m_gmlp_w_s, gmlp_b_s=m_gmlp_b_s, norm_ssm_out=m_norm_ssm_out, norm_gmlp_out=m_norm_gmlp_out,
        w_out=m_w_out, norm_ffn2=m_norm_ffn2, w2_gate=m_w2_gate, w2_up=m_w2_up, w2_down=m_w2_down,
        norm_ple=m_norm_ple, w_ple_gate=m_w_ple_gate, w_ple_proj=m_w_ple_proj, norm_final=m_norm_final)
    moments_v = dict(
        norm_ffn1=v_norm_ffn1, w1_gate=v_w1_gate, w1_up=v_w1_up, w1_down=v_w1_down, norm_mix=v_norm_mix, w_in=v_w_in,
        ssm_log_dt=v_ssm_log_dt, ssm_a_re=v_ssm_a_re, ssm_a_im=v_ssm_a_im, ssm_b_re=v_ssm_b_re, ssm_b_im=v_ssm_b_im,
        ssm_c_re=v_ssm_c_re, ssm_c_im=v_ssm_c_im, ssm_d=v_ssm_d, ssm_w_glu=v_ssm_w_glu, gmlp_norm_v=v_gmlp_norm_v,
        gmlp_w_s=v_gmlp_w_s, gmlp_b_s=v_gmlp_b_s, norm_ssm_out=v_norm_ssm_out, norm_gmlp_out=v_norm_gmlp_out,
        w_out=v_w_out, norm_ffn2=v_norm_ffn2, w2_gate=v_w2_gate, w2_up=v_w2_up, w2_down=v_w2_down,
        norm_ple=v_norm_ple, w_ple_gate=v_w_ple_gate, w_ple_proj=v_w_ple_proj, norm_final=v_norm_final)
    names = list(weights)

    xs = x[0]
    ps = p[0, 0].astype(BF16)
    tgt = loss_target[0]
    d_model = xs.shape[1]
    d_ssm = d_model // 2
    n_groups = d_ssm // SSM_GROUP

    transposed = ("w1_gate", "w1_up", "w2_gate", "w2_up")
    big = {
        "w1_gate": 0, "w1_up": 0, "w1_down": 0, "w_in": 1, "ssm_w_glu": 0, "w_out": 0,
        "w2_gate": 0, "w2_up": 0, "w2_down": 0, "w_ple_gate": 0, "w_ple_proj": 1}
    big_names = list(big)

    def view(a, k):
        return a[0].T if k in transposed else a[0]

    def unview(a, k):
        return a.T[None] if k in transposed else a[None]

    shard = {k: _pad_to(view(weights[k], k).astype(BF16), big[k], LANE) for k in big_names}
    W = {}

    abar_r, abar_i, bbar_r, bbar_i = _ssm_discretize(ssm_log_dt[0], ssm_a_re[0], ssm_a_im[0], ssm_b_re[0], ssm_b_im[0])
    bc_r = _block_diag(jnp.swapaxes(bbar_r, 1, 2)).astype(BF16)
    bc_i = _block_diag(jnp.swapaxes(bbar_i, 1, 2)).astype(BF16)
    cc_r = _block_diag(jnp.swapaxes(ssm_c_re[0], 1, 2)).astype(BF16)
    cc_i = _block_diag(jnp.swapaxes(ssm_c_im[0], 1, 2)).astype(BF16)
    apw_f = _scan_constants(abar_r, abar_i, False)
    apw_b = _scan_constants(abar_r, abar_i, True)
    causal = jnp.tril(jnp.ones((CHUNK, CHUNK), dtype=bool))
    wm = jnp.where(causal[None], gmlp_w_s[0], 0.0).astype(BF16)
    wmt = jnp.swapaxes(wm, 1, 2)
    bs = gmlp_b_s[0][:, :, None]

    groups = [["w1_gate"], ["w1_up"], ["w1_down"], ["w_in", "ssm_w_glu", "w_out"], ["w2_gate"], ["w2_up"],
              ["w2_down", "w_ple_gate", "w_ple_proj"]]
    order = [k for g in groups for k in g]
    place = {k: i for i, k in enumerate(order)}
    me = (4 * lax.axis_index("x") + 2 * lax.axis_index("y") + lax.axis_index("c")).astype(jnp.int32).reshape(1)
    size = {k: shard[k].shape[big[k]] for k in order}
    landing, sems = gather_start([place_own_block(shard[k], big[k], me, "place_" + k) for k in order],
                                 [big[k] for k in order], [size[k] for k in order],
                                 [[place[k] for k in g] for g in groups], "gather_start")

    def landed(g, after):
        axes_g, sizes_g = [big[k] for k in groups[g]], [size[k] for k in groups[g]]
        bufs = gather_wait([landing[place[k]] for k in groups[g]], axes_g, sizes_g, *sems[g], after,
                           "gather_wait_%d" % g)
        return forward_task(bufs, axes_g, sizes_g)

    def arrive(g, after):
        W.update(zip(groups[g], run_tasks([landed(g, after)], "gather_forward_%d" % g)[0]))

    def arrive_during(g, after, fn, *a, **kw):
        out, (got,) = fn(*a, tasks=[landed(g, after)], **kw)
        W.update(zip(groups[g], got))
        return out

    xn1 = rmsnorm_fwd(xs, norm_ffn1, "norm_ffn1")
    arrive(0, xn1)
    gate1 = matmul(xn1, W["w1_gate"], "nt", "ffn1_gate")
    arrive(1, gate1)
    gate1, up1, act1 = ffn_up(xn1, W["w1_up"], gate1, "ffn1_up")
    arrive(2, act1)
    h1 = matmul(act1, W["w1_down"], "nn", "ffn1_down", res=xs, scale=0.5)
    arrive(3, h1)
    xn2 = rmsnorm_fwd(h1, norm_mix, "norm_mix")
    z = matmul(xn2, W["w_in"], "nn", "proj_in")
    y_pre, yg, sr, si = s5_fwd(z, bc_r, bc_i, cc_r, cc_i, apw_f, ssm_d, "s5_fwd")
    glin = matmul(yg, W["ssm_w_glu"], "nn", "ssm_glu")
    y_gmlp = gmlp_fwd(z, gmlp_norm_v, wm, bs, "gmlp_fwd")
    ycat = mix_out_fwd(y_pre, glin, y_gmlp, norm_ssm_out, norm_gmlp_out, "mix_out")
    h2 = arrive_during(4, ycat, matmul, ycat, W["w_out"], "nn", "proj_out", res=h1)
    xn3 = rmsnorm_fwd(h2, norm_ffn2, "norm_ffn2")
    gate2 = arrive_during(5, xn3, matmul, xn3, W["w2_gate"], "nt", "ffn2_gate")
    gate2, up2, act2 = arrive_during(6, gate2, ffn_up, xn3, W["w2_up"], gate2, "ffn2_up")
    h3 = matmul(act2, W["w2_down"], "nn", "ffn2_down", res=h2, scale=0.5)
    xn4 = rmsnorm_fwd(h3, norm_ple, "norm_ple")
    pg_lin = matmul(xn4, W["w_ple_gate"], "nn", "ple_gate")
    pp = matmul(ps, W["w_ple_proj"], "nn", "ple_proj")
    h4 = ple_fwd(h3, pg_lin, pp, "ple_fwd")
    dh4, loss_part, g_norm_final = final_loss(h4, tgt, norm_final.reshape(1, -1), "final_loss")
    loss = lax.psum(loss_part[0, 0], ("x", "y", "c"))

    G = {}
    reduced = {}
    chip_part = {}
    wait_sibling, wait_chips = [], []
    core = lax.axis_index("c").astype(jnp.int32).reshape(1)

    def grad(name_, value):
        G[name_] = value
        wait_sibling.append(name_)

    def carry(fn, *a, levels="ab", extra=None, **kw):
        tasks, kinds = [], []
        if extra is not None:
            tasks.append(extra[0])
            kinds.append(("x", extra[1]))
        if "a" in levels and wait_sibling:
            group = list(wait_sibling)
            wait_sibling.clear()
            tasks.append(to_sibling_task([G[k] for k in group], [big[k] for k in group]))
            kinds.append(("a", group))
        if "b" in levels and wait_chips:
            group = list(wait_chips)
            wait_chips.clear()
            tasks.append(across_chips_task([chip_part[k] for k in group]))
            kinds.append(("b", group))
        if not tasks:
            return fn(*a, **kw)
        out, task_outs = fn(*a, tasks=tasks, **kw)
        for (kind, group), outs in zip(kinds, task_outs):
            if kind == "x":
                group(outs)
                continue
            for k, r in zip(group, outs):
                if kind == "a":
                    chip_part[k] = rs_chip_sum(G[k], r, big[k], core, "rs_sum_" + k)
                    wait_chips.append(k)
                else:
                    reduced[k] = r
        return out

    small = {}
    small["norm_final"] = g_norm_final
    dpp, dpg = ple_bwd(dh4, pg_lin, pp, "ple_bwd")
    grad("w_ple_proj", matmul(ps, dpp, "tn", "grad_ple_proj", out_dtype=BF16))
    grad("w_ple_gate", carry(matmul, xn4, dpg, "tn", "grad_ple_gate", out_dtype=BF16))
    dxn4 = carry(matmul, dpg, W["w_ple_gate"], "nt", "ple_gate_bwd")
    dh3, dh3b, small["norm_ple"] = rmsnorm_bwd(dxn4, h3, norm_ple, dh4, "norm_ple_bwd")

    def ffn_bwd(tag, dhb, xn, gate, up, act, wg, wu, wd, extra=None, last_levels="ab"):
        dgate, dup = carry(ffn_bwd_act, dhb, W[wd], gate, up, tag + "_act_bwd", extra=extra)
        grad(wd, carry(matmul, act, dhb, "tn", tag + "_grad_down", out_dtype=BF16, scale=0.5))
        grad(wg, carry(matmul, dgate, xn, "tn", tag + "_grad_gate", out_dtype=BF16))
        grad(wu, carry(matmul, dup, xn, "tn", tag + "_grad_up", out_dtype=BF16))
        dxn = carry(matmul, dgate, W[wg], "nn", tag + "_gate_bwd")
        return carry(matmul, dup, W[wu], "nn", tag + "_up_bwd", res=dxn, levels=last_levels)

    dxn3 = ffn_bwd("ffn2", dh3b, xn3, gate2, up2, act2, "w2_gate", "w2_up", "w2_down", last_levels="a")
    dh2, dh2b, small["norm_ffn2"] = rmsnorm_bwd(dxn3, h2, norm_ffn2, dh3, "norm_ffn2_bwd")

    grad("w_out", matmul(ycat, dh2b, "tn", "grad_out", out_dtype=BF16))
    dycat = carry(matmul, dh2b, W["w_out"], "nt", "proj_out_bwd", levels="a")
    dyg_direct, dglin, dy_gmlp, small["norm_ssm_out"], small["norm_gmlp_out"] = mix_out_bwd(
        dycat, y_pre, glin, y_gmlp, norm_ssm_out, norm_gmlp_out, "mix_out_bwd")
    grad("ssm_w_glu", matmul(yg, dglin, "tn", "grad_glu", out_dtype=BF16))
    dyg = carry(matmul, dglin, W["ssm_w_glu"], "nt", "ssm_glu_bwd", res=dyg_direct, levels="a")
    du, small["ssm_d"], gc_r, gc_i, gb_r, gb_i, ga_r, ga_i = carry(
        s5_bwd, dyg, y_pre, z, sr, si, bc_r, bc_i, cc_r, cc_i, apw_b, ssm_d, "s5_bwd")
    dzu, dzv, small["gmlp_norm_v"], g_wm, g_bs = gmlp_bwd(dy_gmlp, z, gmlp_norm_v, wm, wmt, bs, "gmlp_bwd")
    small["gmlp_w_s"] = g_wm
    small["gmlp_b_s"] = g_bs
    small["c_re"] = _block_diag_extract(gc_r, SSM_GROUP, SSM_STATE)
    small["c_im"] = _block_diag_extract(gc_i, SSM_GROUP, SSM_STATE)
    small["bbar_r"] = jnp.swapaxes(_block_diag_extract(gb_r, SSM_GROUP, SSM_STATE), 1, 2)
    small["bbar_i"] = jnp.swapaxes(_block_diag_extract(gb_i, SSM_GROUP, SSM_STATE), 1, 2)
    small["abar_r"] = jnp.sum(ga_r, axis=0).reshape(n_groups, SSM_STATE)
    small["abar_i"] = jnp.sum(ga_i, axis=0).reshape(n_groups, SSM_STATE)

    dz = jnp.concatenate([du, dzu, dzv], axis=1)
    grad("w_in", matmul(xn2, dz, "tn", "grad_in", out_dtype=BF16))
    dxn2 = carry(matmul, dz, W["w_in"], "nt", "proj_in_bwd")
    dh1, dh1b, small["norm_mix"] = rmsnorm_bwd(dxn2, h1, norm_mix, dh2, "norm_mix_bwd")

    def pack(parts):
        flat = jnp.concatenate([v.reshape(-1) for v in parts.values()])
        return _pad_to(flat, 0, SUBLANE * LANE).reshape(-1, LANE), flat.shape[0]

    def unpack(everyones, n, parts, tag):
        rows = everyones.shape[0] // N_DEV
        summed = sum_devices(everyones.reshape(N_DEV, rows, LANE), "sum_" + tag).reshape(-1)[:n]
        out, off = {}, 0
        for k, v in parts.items():
            out[k] = summed[off:off + v.size].reshape(v.shape)
            off += v.size
        return out

    early = dict(small)
    flat_early, n_early = pack(early)
    small_landed = []
    dxn1 = ffn_bwd("ffn1", dh1b, xn1, gate1, up1, act1, "w1_gate", "w1_up", "w1_down",
                   extra=(gather_task([flat_early], [0]), small_landed.extend))
    tot = unpack(small_landed[0], n_early, early, "small")
    grad_x, _, g_norm_ffn1 = rmsnorm_bwd(dxn1, xs, norm_ffn1, dh1, "norm_ffn1_bwd")
    assert not wait_sibling and not wait_chips and set(reduced) == set(big_names)
    last = {"norm_ffn1": g_norm_ffn1}
    flat_last, n_last = pack(last)
    ((everyones_last,),) = run_tasks([gather_task([flat_last], [0])], "gather_last")
    tot.update(unpack(everyones_last, n_last, last, "last"))

    out_g, out_d, out_m, out_v = {}, {}, {}, {}
    for k in big_names:
        g, dl, nm, nv = adam_sharded(reduced[k], view(weights[k], k), view(moments_m[k], k), view(moments_v[k], k),
                                     "adam_" + k)
        out_g[k], out_d[k], out_m[k], out_v[k] = unview(g, k), unview(dl, k), unview(nm, k), unview(nv, k)

    _, ssm_vjp = jax.vjp(_ssm_discretize, ssm_log_dt[0], ssm_a_re[0], ssm_a_im[0], ssm_b_re[0], ssm_b_im[0])
    g_log_dt, g_a_re, g_a_im, g_b_re, g_b_im = ssm_vjp((tot["abar_r"], tot["abar_i"], tot["bbar_r"], tot["bbar_i"]))
    small_grads = {
        "norm_ffn1": tot["norm_ffn1"], "norm_mix": tot["norm_mix"], "ssm_log_dt": g_log_dt, "ssm_a_re": g_a_re,
        "ssm_a_im": g_a_im, "ssm_b_re": g_b_re, "ssm_b_im": g_b_im, "ssm_c_re": tot["c_re"], "ssm_c_im": tot["c_im"],
        "ssm_d": tot["ssm_d"], "gmlp_norm_v": tot["gmlp_norm_v"],
        "gmlp_w_s": jnp.where(causal[None], tot["gmlp_w_s"], 0.0), "gmlp_b_s": tot["gmlp_b_s"],
        "norm_ssm_out": tot["norm_ssm_out"], "norm_gmlp_out": tot["norm_gmlp_out"], "norm_ffn2": tot["norm_ffn2"],
        "norm_ple": tot["norm_ple"], "norm_final": tot["norm_final"]}
    swapped = ("ssm_b_re", "ssm_b_im")

    def pre(k, a):
        return jnp.swapaxes(a, -1, -2) if k in swapped else a

    def update(group, name_):
        items = [(_as2d(pre(k, small_grads[k].reshape(weights[k].shape))), _as2d(pre(k, weights[k])),
                  _as2d(pre(k, moments_m[k])), _as2d(pre(k, moments_v[k]))) for k in group]
        for k, it, (dl, nm, nv) in zip(group, items, adam_small(items, name_)):
            shp = pre(k, weights[k]).shape
            out_g[k], out_d[k], out_m[k], out_v[k] = [pre(k, a.reshape(shp)) for a in (it[0], dl, nm, nv)]

    update([k for k in small_grads if k != "norm_ffn1"], "adam_replicated")
    update(["norm_ffn1"], "adam_norm_ffn1")

    return (loss, grad_x[None], *[out_g[k] for k in names], *[out_d[k] for k in names],
            *[out_m[k] for k in names], *[out_v[k] for k in names])
```

```python
import math

import jax
import jax.numpy as jnp
from jax import lax
from jax.experimental import pallas as pl
from jax.experimental.pallas import tpu as pltpu

F32 = jnp.float32
BF16 = jnp.bfloat16
MESH_DT = pl.DeviceIdType.MESH

N_DEV = 8
N_CHIP = 4
LANE = 128
SUBLANE = 8
VMEM_LIMIT = 60 * 1024 * 1024

EPS = 1e-6
SSM_GROUP = 16
SSM_STATE = 64
GROUPS_PER_BLOCK = LANE // SSM_GROUP
STATE_BLOCK = GROUPS_PER_BLOCK * SSM_STATE
GMLP_HEAD = 128
CHUNK = 128

ADAM_LR = 0.001
ADAM_B1 = 0.9
ADAM_B2 = 0.999
ADAM_EPS = 1e-08
ADAM_WD = 0.01
ADAM_STEP = 10

GELU_K = math.sqrt(2.0 / math.pi)
GELU_C = 0.044715


def _cparams():
    return pltpu.CompilerParams(vmem_limit_bytes=VMEM_LIMIT)


def _tile(n, pref):
    if n <= pref:
        return n
    t = (pref // LANE) * LANE
    while t > 0:
        if n % t == 0:
            return t
        t -= LANE
    return n


def _row_tile(n, pref):
    if n <= pref:
        return n
    t = (pref // SUBLANE) * SUBLANE
    while t > 0:
        if n % t == 0:
            return t
        t -= SUBLANE
    return n


def _gelu(x):
    t = jnp.tanh(GELU_K * (x + GELU_C * x * x * x))
    return 0.5 * x * (1.0 + t)


def _gelu_grad(x):
    t = jnp.tanh(GELU_K * (x + GELU_C * x * x * x))
    return 0.5 * (1.0 + t) + 0.5 * x * (1.0 - t * t) * (GELU_K * (1.0 + 3.0 * GELU_C * x * x))


def _sigmoid(x):
    return 0.5 * jnp.tanh(0.5 * x) + 0.5


_DN = {
    "nn": (((1,), (0,)), ((), ())),
    "nt": (((1,), (1,)), ((), ())),
    "tn": (((0,), (0,)), ((), ())),
}


def _dot(a, b, mode="nn"):
    return lax.dot_general(a, b, _DN[mode], preferred_element_type=F32)


class CommTask:
    def __init__(self, inputs, out_shape, n_sems, start, late, finish, in_place=False):
        self.inputs, self.out_shape, self.n_sems = list(inputs), list(out_shape), n_sems
        self.start, self.late, self.finish = start, late, finish
        self.in_place = in_place


def _task_aliases(tasks, first_in, first_out):
    aliases = {}
    for t in tasks:
        if t.in_place:
            aliases.update({first_in + i: first_out + i for i in range(len(t.inputs))})
        first_in += len(t.inputs)
        first_out += len(t.out_shape)
    return aliases


def _call(body, *, name, grid, in_specs, out_specs, out_shape, args, scratch_shapes=(), tasks=()):
    in_specs, out_specs, out_shape = list(in_specs), list(out_specs), list(out_shape)
    scratch_shapes = list(scratch_shapes)
    if not tasks:
        return pl.pallas_call(
            body, name=name, grid=grid, in_specs=in_specs, out_specs=out_specs, out_shape=out_shape,
            scratch_shapes=scratch_shapes, compiler_params=_cparams())(*args)
    n_in, n_out, n_scr = len(in_specs), len(out_specs), len(scratch_shapes)
    t_in = [len(t.inputs) for t in tasks]
    t_out = [len(t.out_shape) for t in tasks]
    late_step = grid[0] - max(1, grid[0] // 4)
    has_late = grid[0] >= 2

    def carried(*refs):
        pos = n_in
        task_ins = []
        for k in t_in:
            task_ins.append(refs[pos:pos + k])
            pos += k
        outs = refs[pos:pos + n_out]
        pos += n_out
        task_outs = []
        for k in t_out:
            task_outs.append(refs[pos:pos + k])
            pos += k
        scratch = refs[pos:pos + n_scr]
        pos += n_scr
        sems = [refs[pos + 3 * i:pos + 3 * i + 3] for i in range(len(tasks))]
        ids = [pl.program_id(d) for d in range(len(grid))]
        rest_zero = True
        for d in range(1, len(grid)):
            rest_zero = jnp.logical_and(rest_zero, ids[d] == 0)
        first = jnp.logical_and(ids[0] == 0, rest_zero)
        last = ids[0] == grid[0] - 1
        for d in range(1, len(grid)):
            last = jnp.logical_and(last, ids[d] == grid[d] - 1)

        @pl.when(first)
        def _():
            for t, ti, to, s in zip(tasks, task_ins, task_outs, sems):
                t.start(ti, to, *s)

        if has_late:
            @pl.when(jnp.logical_and(ids[0] == late_step, rest_zero))
            def _():
                for t, ti, to, s in zip(tasks, task_ins, task_outs, sems):
                    t.late(ti, to, *s)

        body(*refs[:n_in], *outs, *scratch)

        @pl.when(last)
        def _():
            for t, ti, to, s in zip(tasks, task_ins, task_outs, sems):
                if not has_late:
                    t.late(ti, to, *s)
                t.finish(ti, to, *s)

    any_spec = pl.BlockSpec(memory_space=pl.ANY)
    sem_shapes = [pltpu.SemaphoreType.DMA((n,)) for t in tasks for n in t.n_sems]
    res = pl.pallas_call(
        carried, name=name, grid=grid,
        in_specs=in_specs + [any_spec] * sum(t_in), out_specs=out_specs + [any_spec] * sum(t_out),
        out_shape=out_shape + [s for t in tasks for s in t.out_shape],
        input_output_aliases=_task_aliases(tasks, n_in, n_out),
        scratch_shapes=scratch_shapes + sem_shapes, compiler_params=_cparams(),
    )(*args, *[a for t in tasks for a in t.inputs])
    res = list(res)
    task_res, pos = [], n_out
    for k in t_out:
        task_res.append(res[pos:pos + k])
        pos += k
    return res[:n_out], task_res


def _mm_dims(a, b, mode):
    if mode == "nn":
        (m, k), (k2, n) = a.shape, b.shape
    elif mode == "nt":
        (m, k), (n, k2) = a.shape, b.shape
    else:
        (k, m), (k2, n) = a.shape, b.shape
    assert k == k2, (a.shape, b.shape, mode)
    return m, n, k


def _mm_specs(mode, tm, tn, tk):
    if mode == "tn":
        a_spec = pl.BlockSpec((tk, tm), lambda i, j, k: (k, i))
    else:
        a_spec = pl.BlockSpec((tm, tk), lambda i, j, k: (i, k))
    if mode == "nt":
        b_spec = pl.BlockSpec((tn, tk), lambda i, j, k: (j, k))
    else:
        b_spec = pl.BlockSpec((tk, tn), lambda i, j, k: (k, j))
    return a_spec, b_spec


def _accumulate(acc, nk, partial, emit):
    if nk == 1:
        emit(partial)
        return
    kk = pl.program_id(2)

    @pl.when(kk == 0)
    def _():
        acc[...] = partial

    @pl.when(kk > 0)
    def _():
        acc[...] += partial

    @pl.when(kk == nk - 1)
    def _():
        emit(acc[...])


def matmul(a, b, mode, name, out_dtype=F32, res=None, scale=1.0, tm=1024, tn=1024, tk=2048, tasks=()):
    m, n, k = _mm_dims(a, b, mode)
    tm, tn, tk = _tile(m, tm), _tile(n, tn), _tile(k, tk)
    nk = k // tk
    a_spec, b_spec = _mm_specs(mode, tm, tn, tk)
    o_spec = pl.BlockSpec((tm, tn), lambda i, j, k: (i, j))
    has_res = res is not None

    def body(*refs):
        if has_res:
            a_ref, b_ref, r_ref, o_ref, acc = refs
        else:
            a_ref, b_ref, o_ref, acc = refs

        def emit(v):
            if scale != 1.0:
                v = v * scale
            if has_res:
                v = r_ref[...] + v
            o_ref[...] = v.astype(out_dtype)

        _accumulate(acc, nk, _dot(a_ref[...], b_ref[...], mode), emit)

    out = _call(
        body, name=name, grid=(m // tm, n // tn, nk),
        in_specs=[a_spec, b_spec] + ([o_spec] if has_res else []), out_specs=[o_spec],
        out_shape=[jax.ShapeDtypeStruct((m, n), out_dtype)], args=(a, b) + ((res,) if has_res else ()),
        scratch_shapes=[pltpu.VMEM((tm, tn) if nk > 1 else (SUBLANE, LANE), F32)], tasks=tasks)
    return (out[0][0], out[1]) if tasks else out[0]


def ffn_up(xn, wu, gate, name, tm=1024, tn=1024, tk=2048, tasks=()):
    m, n, k = _mm_dims(xn, wu, "nt")
    tm, tn, tk = _tile(m, tm), _tile(n, tn), _tile(k, tk)
    nk = k // tk
    a_spec, b_spec = _mm_specs("nt", tm, tn, tk)
    o_spec = pl.BlockSpec((tm, tn), lambda i, j, k: (i, j))

    def body(a_ref, u_ref, gate_ref, gate_b_ref, up_b_ref, act_ref, acc):
        def emit(u):
            g = gate_ref[...]
            gate_b_ref[...] = g.astype(BF16)
            up_b_ref[...] = u.astype(BF16)
            act_ref[...] = (g * _sigmoid(g) * u).astype(BF16)

        _accumulate(acc, nk, _dot(a_ref[...], u_ref[...], "nt"), emit)

    out = _call(
        body, name=name, grid=(m // tm, n // tn, nk), in_specs=[a_spec, b_spec, o_spec],
        out_specs=[o_spec, o_spec, o_spec],
        out_shape=[jax.ShapeDtypeStruct((m, n), BF16), jax.ShapeDtypeStruct((m, n), BF16),
                   jax.ShapeDtypeStruct((m, n), BF16)],
        args=(xn, wu, gate), scratch_shapes=[pltpu.VMEM((tm, tn) if nk > 1 else (SUBLANE, LANE), F32)], tasks=tasks)
    return (tuple(out[0]), out[1]) if tasks else tuple(out)


def ffn_bwd_act(dh, wd, gate, up, name, tm=1024, tn=1024, tk=2048, tasks=()):
    m, n, k = _mm_dims(dh, wd, "nt")
    tm, tn, tk = _tile(m, tm), _tile(n, tn), _tile(k, tk)
    nk = k // tk
    a_spec, b_spec = _mm_specs("nt", tm, tn, tk)
    o_spec = pl.BlockSpec((tm, tn), lambda i, j, k: (i, j))

    def body(a_ref, b_ref, gate_ref, up_ref, dg_ref, du_ref, acc):
        def emit(total):
            dact = 0.5 * total
            g = gate_ref[...].astype(F32)
            sg = _sigmoid(g)
            du_ref[...] = (dact * (g * sg)).astype(BF16)
            dg_ref[...] = (dact * up_ref[...].astype(F32) * (sg * (1.0 + g * (1.0 - sg)))).astype(BF16)

        _accumulate(acc, nk, _dot(a_ref[...], b_ref[...], "nt"), emit)

    out = _call(
        body, name=name, grid=(m // tm, n // tn, nk), in_specs=[a_spec, b_spec, o_spec, o_spec],
        out_specs=[o_spec, o_spec],
        out_shape=[jax.ShapeDtypeStruct((m, n), BF16), jax.ShapeDtypeStruct((m, n), BF16)],
        args=(dh, wd, gate, up), scratch_shapes=[pltpu.VMEM((tm, tn) if nk > 1 else (SUBLANE, LANE), F32)],
        tasks=tasks)
    return (tuple(out[0]), out[1]) if tasks else tuple(out)


def _rows(t, d, tr):
    return pl.BlockSpec((tr, d), lambda i: (i, 0))


def _vec(d):
    return pl.BlockSpec((1, d), lambda i: (0, 0))


def rmsnorm_fwd(x, g, name, tr=512):
    t, d = x.shape
    tr = _row_tile(t, tr)

    def body(x_ref, g_ref, o_ref):
        xf = x_ref[...]
        r = lax.rsqrt(jnp.mean(xf * xf, axis=-1, keepdims=True) + EPS)
        o_ref[...] = (xf * r * g_ref[...]).astype(BF16)

    return pl.pallas_call(
        body, name=name, grid=(t // tr,), in_specs=[_rows(t, d, tr), _vec(d)], out_specs=_rows(t, d, tr),
        out_shape=jax.ShapeDtypeStruct((t, d), BF16), compiler_params=_cparams(),
    )(x, g)


def _rms_bwd(dxn, xf, g):
    r = lax.rsqrt(jnp.mean(xf * xf, axis=-1, keepdims=True) + EPS)
    xhat = xf * r
    dg = jnp.sum(dxn * xhat, axis=0, keepdims=True)
    dxh = dxn * g
    dx = r * (dxh - xhat * jnp.mean(dxh * xhat, axis=-1, keepdims=True))
    return dx, dg


def rmsnorm_bwd(dxn, x, g, dres, name, tr=256):
    t, d = x.shape
    tr = _row_tile(t, tr)

    def body(dxn_ref, x_ref, g_ref, dres_ref, o_ref, ob_ref, dg_ref):
        dx, dg = _rms_bwd(dxn_ref[...], x_ref[...], g_ref[...])
        out = dres_ref[...] + dx
        o_ref[...] = out
        ob_ref[...] = out.astype(BF16)

        @pl.when(pl.program_id(0) == 0)
        def _():
            dg_ref[...] = jnp.zeros_like(dg_ref)

        dg_ref[...] += dg

    return pl.pallas_call(
        body, name=name, grid=(t // tr,),
        in_specs=[_rows(t, d, tr), _rows(t, d, tr), _vec(d), _rows(t, d, tr)],
        out_specs=[_rows(t, d, tr), _rows(t, d, tr), _vec(d)],
        out_shape=[jax.ShapeDtypeStruct((t, d), F32), jax.ShapeDtypeStruct((t, d), BF16),
                   jax.ShapeDtypeStruct((1, d), F32)],
        compiler_params=_cparams(),
    )(dxn, x, g, dres)


def final_loss(h, target, g, name, tr=256):
    t, d = h.shape
    tr = _row_tile(t, tr)

    def body(h_ref, t_ref, g_ref, dh_ref, loss_ref, dg_ref):
        xf = h_ref[...]
        gg = g_ref[...]
        r = lax.rsqrt(jnp.mean(xf * xf, axis=-1, keepdims=True) + EPS)
        xhat = xf * r
        e = xhat * gg - t_ref[...]
        part = jnp.sum(jnp.sum(e * e, axis=1, keepdims=True), axis=0, keepdims=True) * (0.5 / d)
        dout = e * (1.0 / d)
        dg = jnp.sum(dout * xhat, axis=0, keepdims=True)
        dxh = dout * gg
        dh_ref[...] = r * (dxh - xhat * jnp.mean(dxh * xhat, axis=-1, keepdims=True))

        @pl.when(pl.program_id(0) == 0)
        def _():
            dg_ref[...] = jnp.zeros_like(dg_ref)
            loss_ref[...] = jnp.zeros_like(loss_ref)

        dg_ref[...] += dg
        loss_ref[...] += jnp.broadcast_to(part, loss_ref.shape)

    return pl.pallas_call(
        body, name=name, grid=(t // tr,),
        in_specs=[_rows(t, d, tr), _rows(t, d, tr), _vec(d)],
        out_specs=[_rows(t, d, tr), pl.BlockSpec((SUBLANE, LANE), lambda i: (0, 0)), _vec(d)],
        out_shape=[jax.ShapeDtypeStruct((t, d), F32), jax.ShapeDtypeStruct((SUBLANE, LANE), F32),
                   jax.ShapeDtypeStruct((1, d), F32)],
        compiler_params=_cparams(),
    )(h, target, g)


def ple_fwd(h, glin, pp, name, tr=512):
    t, d = h.shape
    tr = _row_tile(t, tr)

    def body(h_ref, gl_ref, pp_ref, o_ref):
        o_ref[...] = h_ref[...] + _sigmoid(gl_ref[...]) * pp_ref[...]

    sp = _rows(t, d, tr)
    return pl.pallas_call(
        body, name=name, grid=(t // tr,), in_specs=[sp, sp, sp], out_specs=sp,
        out_shape=jax.ShapeDtypeStruct((t, d), F32), compiler_params=_cparams(),
    )(h, glin, pp)


def ple_bwd(dh, glin, pp, name, tr=512):
    t, d = dh.shape
    tr = _row_tile(t, tr)

    def body(dh_ref, gl_ref, pp_ref, dpp_ref, dgl_ref):
        gate = _sigmoid(gl_ref[...])
        dh_ = dh_ref[...]
        dpp_ref[...] = (dh_ * gate).astype(BF16)
        dgl_ref[...] = (dh_ * pp_ref[...] * gate * (1.0 - gate)).astype(BF16)

    sp = _rows(t, d, tr)
    return pl.pallas_call(
        body, name=name, grid=(t // tr,), in_specs=[sp, sp, sp], out_specs=[sp, sp],
        out_shape=[jax.ShapeDtypeStruct((t, d), BF16), jax.ShapeDtypeStruct((t, d), BF16)],
        compiler_params=_cparams(),
    )(dh, glin, pp)


def mix_out_fwd(y_pre, glin, y_gmlp, g_so, g_go, name, tr=512):
    t, d = y_pre.shape
    tr = _row_tile(t, tr)

    def body(yp_ref, gl_ref, yg_ref, gs_ref, gg_ref, o_ref):
        ys = _gelu(yp_ref[...]) * _sigmoid(gl_ref[...])
        r = lax.rsqrt(jnp.mean(ys * ys, axis=-1, keepdims=True) + EPS)
        o_ref[:, 0:d] = (ys * r * gs_ref[...]).astype(BF16)
        yq = yg_ref[...]
        r2 = lax.rsqrt(jnp.mean(yq * yq, axis=-1, keepdims=True) + EPS)
        o_ref[:, d:2 * d] = (yq * r2 * gg_ref[...]).astype(BF16)

    sp = _rows(t, d, tr)
    return pl.pallas_call(
        body, name=name, grid=(t // tr,), in_specs=[sp, sp, sp, _vec(d), _vec(d)],
        out_specs=_rows(t, 2 * d, tr), out_shape=jax.ShapeDtypeStruct((t, 2 * d), BF16),
        compiler_params=_cparams(),
    )(y_pre, glin, y_gmlp, g_so, g_go)


def mix_out_bwd(dycat, y_pre, glin, y_gmlp, g_so, g_go, name, tr=256):
    t, d = y_pre.shape
    tr = _row_tile(t, tr)

    def body(dy_ref, yp_ref, gl_ref, yg_ref, gs_ref, gg_ref, dyg_ref, dl_ref, dyq_ref, dgs_ref, dgg_ref):
        yg = _gelu(yp_ref[...])
        sg = _sigmoid(gl_ref[...])
        dys, dgs = _rms_bwd(dy_ref[:, 0:d], yg * sg, gs_ref[...])
        dyg_ref[...] = dys * sg
        dl_ref[...] = (dys * yg * sg * (1.0 - sg)).astype(BF16)
        dyq, dgg = _rms_bwd(dy_ref[:, d:2 * d], yg_ref[...], gg_ref[...])
        dyq_ref[...] = dyq

        @pl.when(pl.program_id(0) == 0)
        def _():
            dgs_ref[...] = jnp.zeros_like(dgs_ref)
            dgg_ref[...] = jnp.zeros_like(dgg_ref)

        dgs_ref[...] += dgs
        dgg_ref[...] += dgg

    sp = _rows(t, d, tr)
    return pl.pallas_call(
        body, name=name, grid=(t // tr,),
        in_specs=[_rows(t, 2 * d, tr), sp, sp, sp, _vec(d), _vec(d)],
        out_specs=[sp, sp, sp, _vec(d), _vec(d)],
        out_shape=[jax.ShapeDtypeStruct((t, d), F32), jax.ShapeDtypeStruct((t, d), BF16),
                   jax.ShapeDtypeStruct((t, d), F32), jax.ShapeDtypeStruct((1, d), F32),
                   jax.ShapeDtypeStruct((1, d), F32)],
        compiler_params=_cparams(),
    )(dycat, y_pre, glin, y_gmlp, g_so, g_go)


SCAN_COLS = 512


def _scan_tile(xr, xi, const, cr, ci, reverse):
    for lvl, sh in enumerate((1, 2, 4)):
        ar, ai = const(2 * lvl), const(2 * lvl + 1)
        s = (SUBLANE - sh) if reverse else sh
        rr = pltpu.roll(xr, s, 0)
        ri = pltpu.roll(xi, s, 0)
        xr, xi = xr + ar * rr - ai * ri, xi + ar * ri + ai * rr
    pr, pi_ = const(6), const(7)
    xr, xi = xr + pr * cr - pi_ * ci, xi + pr * ci + pi_ * cr
    return xr, xi


def _bcast_row(x, row):
    return jnp.broadcast_to(x[row:row + 1, :], x.shape)


def s5_fwd(z, bc_r, bc_i, cc_r, cc_i, apw, dvec, name, tc=512, tasks=()):
    t = z.shape[0]
    nblk = bc_r.shape[0]
    d = nblk * LANE
    ns = nblk * STATE_BLOCK
    tc = _row_tile(t, tc)
    ntile = tc // SUBLANE

    def body(z_ref, br_ref, bi_ref, cr_ref, ci_ref, apw_ref, d_ref, y_ref, yg_ref, sr_ref, si_ref, carry):
        @pl.when(pl.program_id(0) == 0)
        def _():
            carry[...] = jnp.zeros_like(carry)

        for j in range(nblk):
            uj = z_ref[:, j * LANE:(j + 1) * LANE]
            ub = uj.astype(BF16)
            for q in range(STATE_BLOCK // SCAN_COLS):
                c0 = j * STATE_BLOCK + q * SCAN_COLS
                cs = pl.ds(c0, SCAN_COLS)
                bs = slice(q * SCAN_COLS, (q + 1) * SCAN_COLS)
                sr_ref[:, cs] = _dot(ub, br_ref[j, :, bs])
                si_ref[:, cs] = _dot(ub, bi_ref[j, :, bs])
                const = lambda k, cs=cs: apw_ref[k, :, cs]

                def tile(k, c, cs=cs, const=const):
                    rows = pl.ds(pl.multiple_of(k * SUBLANE, SUBLANE), SUBLANE)
                    xr, xi = _scan_tile(sr_ref[rows, cs], si_ref[rows, cs], const, c[0], c[1], False)
                    sr_ref[rows, cs] = xr
                    si_ref[rows, cs] = xi
                    return _bcast_row(xr, SUBLANE - 1), _bcast_row(xi, SUBLANE - 1)

                c_r, c_i = lax.fori_loop(0, ntile, tile, (carry[0, :, cs], carry[1, :, cs]))
                carry[0, :, cs] = c_r
                carry[1, :, cs] = c_i
            sb = pl.ds(j * STATE_BLOCK, STATE_BLOCK)
            y = (_dot(sr_ref[:, sb].astype(BF16), cr_ref[j]) - _dot(si_ref[:, sb].astype(BF16), ci_ref[j])
                 + d_ref[:, j * LANE:(j + 1) * LANE] * uj)
            y_ref[:, j * LANE:(j + 1) * LANE] = y
            yg_ref[:, j * LANE:(j + 1) * LANE] = _gelu(y).astype(BF16)

    full3 = lambda shp: pl.BlockSpec(shp, lambda i: (0, 0, 0))
    out = _call(
        body, name=name, grid=(t // tc,),
        in_specs=[pl.BlockSpec((tc, d), lambda i: (i, 0)), full3(bc_r.shape), full3(bc_i.shape),
                  full3(cc_r.shape), full3(cc_i.shape), full3(apw.shape), _vec(d)],
        out_specs=[pl.BlockSpec((tc, d), lambda i: (i, 0)), pl.BlockSpec((tc, d), lambda i: (i, 0)),
                   pl.BlockSpec((tc, ns), lambda i: (i, 0)), pl.BlockSpec((tc, ns), lambda i: (i, 0))],
        out_shape=[jax.ShapeDtypeStruct((t, d), F32), jax.ShapeDtypeStruct((t, d), BF16),
                   jax.ShapeDtypeStruct((t, ns), F32), jax.ShapeDtypeStruct((t, ns), F32)],
        args=(z, bc_r, bc_i, cc_r, cc_i, apw, dvec), scratch_shapes=[pltpu.VMEM((2, SUBLANE, ns), F32)], tasks=tasks)
    return (tuple(out[0]), out[1]) if tasks else tuple(out)


def s5_bwd(dyg, y_pre, z, sr, si, bc_r, bc_i, cc_r, cc_i, apw_rev, dvec, name, tc=256, tasks=()):
    t = z.shape[0]
    nblk = bc_r.shape[0]
    d = nblk * LANE
    ns = nblk * STATE_BLOCK
    tc = _row_tile(t, tc)
    ntile = tc // SUBLANE
    nchunk = t // tc
    tiles_per_chunk = tc // SUBLANE

    def body(dyg_ref, yp_ref, z_ref, sr_ref, si_ref, pr_ref, pi_ref, br_ref, bi_ref, cr_ref, ci_ref, apw_ref,
             d_ref, du_ref, gd_ref, gcr_ref, gci_ref, gbr_ref, gbi_ref, gar_ref, gai_ref, lr_ref, li_ref, carry):
        step = pl.program_id(0)

        @pl.when(step == 0)
        def _():
            carry[...] = jnp.zeros_like(carry)
            for ref in (gd_ref, gcr_ref, gci_ref, gbr_ref, gbi_ref, gar_ref, gai_ref):
                ref[...] = jnp.zeros_like(ref)

        first_chunk = (step == nchunk - 1).astype(F32)
        keep_prev = 1.0 - first_chunk
        row0 = lax.broadcasted_iota(jnp.int32, (SUBLANE, SCAN_COLS), 0) == 0

        for j in range(nblk):
            lanes = slice(j * LANE, (j + 1) * LANE)
            uj = z_ref[:, lanes]
            ub = uj.astype(BF16)
            gy = dyg_ref[:, lanes] * _gelu_grad(yp_ref[:, lanes])
            gyb = gy.astype(BF16)
            gd_ref[:, lanes] += jnp.sum(gy * uj, axis=0, keepdims=True)
            for q in range(STATE_BLOCK // SCAN_COLS):
                c0 = j * STATE_BLOCK + q * SCAN_COLS
                cs = pl.ds(c0, SCAN_COLS)
                bs = slice(q * SCAN_COLS, (q + 1) * SCAN_COLS)
                lr_ref[:, cs] = _dot(gyb, cr_ref[j, bs, :], "nt")
                li_ref[:, cs] = -_dot(gyb, ci_ref[j, bs, :], "nt")
                const = lambda k, cs=cs: apw_ref[k, :, cs]

                def one_tile(rows, prev_r, prev_i, c, cs=cs, const=const):
                    cr_, ci_, gar, gai = c
                    xr, xi = _scan_tile(lr_ref[rows, cs], li_ref[rows, cs], const, cr_, ci_, True)
                    lr_ref[rows, cs] = xr
                    li_ref[rows, cs] = xi
                    spr = jnp.where(row0, prev_r, pltpu.roll(sr_ref[rows, cs], 1, 0))
                    spi = jnp.where(row0, prev_i, pltpu.roll(si_ref[rows, cs], 1, 0))
                    gar = gar + xr * spr + xi * spi
                    gai = gai + xi * spr - xr * spi
                    return _bcast_row(xr, 0), _bcast_row(xi, 0), gar, gai

                def tile(k, c, cs=cs, one_tile=one_tile):
                    kk = ntile - 1 - k
                    rows = pl.ds(pl.multiple_of(kk * SUBLANE, SUBLANE), SUBLANE)
                    prow = pl.ds(pl.multiple_of((kk - 1) * SUBLANE, SUBLANE), SUBLANE)
                    prev_r = _bcast_row(sr_ref[prow, cs], SUBLANE - 1)
                    prev_i = _bcast_row(si_ref[prow, cs], SUBLANE - 1)
                    return one_tile(rows, prev_r, prev_i, c)

                zero = jnp.zeros((SUBLANE, SCAN_COLS), F32)
                c = lax.fori_loop(0, ntile - 1, tile, (carry[0, :, cs], carry[1, :, cs], zero, zero))
                prev_r = _bcast_row(pr_ref[:, cs], SUBLANE - 1) * keep_prev
                prev_i = _bcast_row(pi_ref[:, cs], SUBLANE - 1) * keep_prev
                c_r, c_i, gar, gai = one_tile(pl.ds(0, SUBLANE), prev_r, prev_i, c)
                carry[0, :, cs] = c_r
                carry[1, :, cs] = c_i
                gar_ref[:, cs] += gar
                gai_ref[:, cs] += gai
            sb = pl.ds(j * STATE_BLOCK, STATE_BLOCK)
            lrb = lr_ref[:, sb].astype(BF16)
            lib = li_ref[:, sb].astype(BF16)
            gcr_ref[j] += _dot(gyb, sr_ref[:, sb].astype(BF16), "tn")
            gci_ref[j] -= _dot(gyb, si_ref[:, sb].astype(BF16), "tn")
            gbr_ref[j] += _dot(ub, lrb, "tn")
            gbi_ref[j] += _dot(ub, lib, "tn")
            du = _dot(lrb, br_ref[j], "nt") + _dot(lib, bi_ref[j], "nt") + gy * d_ref[:, lanes]
            du_ref[:, lanes] = du.astype(BF16)

    rev = lambda i: (nchunk - 1 - i, 0)
    prev = lambda i: (jnp.maximum((nchunk - 1 - i) * tiles_per_chunk - 1, 0), 0)
    full3 = lambda shp: pl.BlockSpec(shp, lambda i: (0, 0, 0))
    acc3 = pl.BlockSpec((nblk, LANE, STATE_BLOCK), lambda i: (0, 0, 0))
    acc_rows = pl.BlockSpec((SUBLANE, ns), lambda i: (0, 0))
    out = _call(
        body, name=name, grid=(nchunk,),
        in_specs=[pl.BlockSpec((tc, d), rev), pl.BlockSpec((tc, d), rev), pl.BlockSpec((tc, d), rev),
                  pl.BlockSpec((tc, ns), rev), pl.BlockSpec((tc, ns), rev),
                  pl.BlockSpec((SUBLANE, ns), prev), pl.BlockSpec((SUBLANE, ns), prev),
                  full3(bc_r.shape), full3(bc_i.shape), full3(cc_r.shape), full3(cc_i.shape), full3(apw_rev.shape),
                  _vec(d)],
        out_specs=[pl.BlockSpec((tc, d), rev), _vec(d), acc3, acc3, acc3, acc3, acc_rows, acc_rows],
        out_shape=[jax.ShapeDtypeStruct((t, d), BF16), jax.ShapeDtypeStruct((1, d), F32)]
        + [jax.ShapeDtypeStruct((nblk, LANE, STATE_BLOCK), F32)] * 4
        + [jax.ShapeDtypeStruct((SUBLANE, ns), F32)] * 2,
        args=(dyg, y_pre, z, sr, si, sr, si, bc_r, bc_i, cc_r, cc_i, apw_rev, dvec),
        scratch_shapes=[pltpu.VMEM((tc, ns), F32), pltpu.VMEM((tc, ns), F32), pltpu.VMEM((2, SUBLANE, ns), F32)],
        tasks=tasks)
    return (tuple(out[0]), out[1]) if tasks else tuple(out)


def _cmul(a, b):
    return a[0] * b[0] - a[1] * b[1], a[0] * b[1] + a[1] * b[0]


def _scan_constants(abar_r, abar_i, reverse):
    ar = abar_r.reshape(1, -1)
    ai = abar_i.reshape(1, -1)
    if reverse:
        ai = -ai
    pw = [(ar, ai)]
    for _ in range(SUBLANE - 1):
        pw.append(_cmul(pw[-1], (ar, ai)))
    rows = lax.broadcasted_iota(jnp.int32, (SUBLANE, 1), 0)
    out = []
    for sh in (1, 2, 4):
        keep = (rows <= SUBLANE - 1 - sh) if reverse else (rows >= sh)
        for part in pw[sh - 1]:
            out.append(jnp.where(keep, part, 0.0))
    for comp in (0, 1):
        stack = jnp.concatenate([pw[k][comp] for k in range(SUBLANE)], axis=0)
        out.append(stack[::-1] if reverse else stack)
    return jnp.stack(out, axis=0).astype(F32)


def _ssm_discretize(log_dt, a_re, a_im, b_re, b_im):
    dt = jnp.exp(log_dt)[:, None]
    lr = jnp.minimum(a_re, -1e-4)
    li = a_im
    mag = jnp.exp(lr * dt)
    ang = li * dt
    abar_r = mag * jnp.cos(ang)
    abar_i = mag * jnp.sin(ang)
    den = lr * lr + li * li
    xr = abar_r - 1.0
    xi = abar_i
    zr = (xr * lr + xi * li) / den
    zi = (xi * lr - xr * li) / den
    bbar_r = zr[..., None] * b_re - zi[..., None] * b_im
    bbar_i = zr[..., None] * b_im + zi[..., None] * b_re
    return abar_r, abar_i, bbar_r, bbar_i


def _block_diag(w):
    g, a, b = w.shape
    nb = g // GROUPS_PER_BLOCK
    eye = jnp.eye(GROUPS_PER_BLOCK, dtype=w.dtype)
    w5 = w.reshape(nb, GROUPS_PER_BLOCK, a, b)
    out = w5[:, :, :, None, :] * eye[None, :, None, :, None]
    return out.reshape(nb, GROUPS_PER_BLOCK * a, GROUPS_PER_BLOCK * b)


def _block_diag_extract(m, a, b):
    nb = m.shape[0]
    eye = jnp.eye(GROUPS_PER_BLOCK, dtype=m.dtype)
    m5 = m.reshape(nb, GROUPS_PER_BLOCK, a, GROUPS_PER_BLOCK, b)
    out = jnp.sum(m5 * eye[None, :, None, :, None], axis=3)
    return out.reshape(nb * GROUPS_PER_BLOCK, a, b)


def _layer_norm(gv, nv):
    mu = jnp.mean(gv, axis=-1, keepdims=True)
    xc = gv - mu
    r = lax.rsqrt(jnp.mean(xc * xc, axis=-1, keepdims=True) + EPS)
    xhat = xc * r
    return xhat * nv, xhat, r


def gmlp_fwd(z, norm_v, wm, bs, name, tr=256):
    t = z.shape[0]
    nh = wm.shape[0]
    d = nh * GMLP_HEAD
    col0 = (z.shape[1] - 2 * d) // d
    tr = _row_tile(t, tr)

    def body(zu_ref, zv_ref, nv_ref, wm_ref, bs_ref, o_ref):
        v, _, _ = _layer_norm(_gelu(zv_ref[...]), nv_ref[...])
        vb = v.astype(BF16)
        u = _gelu(zu_ref[...])
        for c in range(tr // CHUNK):
            rows = slice(c * CHUNK, (c + 1) * CHUNK)
            for h in range(nh):
                cols = slice(h * GMLP_HEAD, (h + 1) * GMLP_HEAD)
                s = _dot(wm_ref[h], vb[rows, cols]) + bs_ref[h]
                o_ref[rows, cols] = u[rows, cols] * s

    return pl.pallas_call(
        body, name=name, grid=(t // tr,),
        in_specs=[pl.BlockSpec((tr, d), lambda i: (i, col0)), pl.BlockSpec((tr, d), lambda i: (i, col0 + 1)),
                  _vec(d), pl.BlockSpec(wm.shape, lambda i: (0, 0, 0)), pl.BlockSpec(bs.shape, lambda i: (0, 0, 0))],
        out_specs=pl.BlockSpec((tr, d), lambda i: (i, 0)),
        out_shape=jax.ShapeDtypeStruct((t, d), F32), compiler_params=_cparams(),
    )(z, z, norm_v, wm, bs)


def gmlp_bwd(dy, z, norm_v, wm, wmt, bs, name, tr=256):
    t = z.shape[0]
    nh = wm.shape[0]
    d = nh * GMLP_HEAD
    col0 = (z.shape[1] - 2 * d) // d
    tr = _row_tile(t, tr)

    def body(dy_ref, zu_ref, zv_ref, nv_ref, wm_ref, wmt_ref, bs_ref, dzu_ref, dzv_ref, dnv_ref, dwm_ref, dbs_ref,
             dv_ref):
        @pl.when(pl.program_id(0) == 0)
        def _():
            dnv_ref[...] = jnp.zeros_like(dnv_ref)
            dwm_ref[...] = jnp.zeros_like(dwm_ref)
            dbs_ref[...] = jnp.zeros_like(dbs_ref)

        zv = zv_ref[...]
        nv = nv_ref[...]
        v, xhat, r = _layer_norm(_gelu(zv), nv)
        vb = v.astype(BF16)
        zu = zu_ref[...]
        u = _gelu(zu)
        dy_ = dy_ref[...]
        for c in range(tr // CHUNK):
            rows = slice(c * CHUNK, (c + 1) * CHUNK)
            for h in range(nh):
                cols = slice(h * GMLP_HEAD, (h + 1) * GMLP_HEAD)
                vh = vb[rows, cols]
                s = _dot(wm_ref[h], vh) + bs_ref[h]
                dyh = dy_[rows, cols]
                dzu_ref[rows, cols] = (dyh * s * _gelu_grad(zu[rows, cols])).astype(BF16)
                ds = dyh * u[rows, cols]
                dsb = ds.astype(BF16)
                dbs_ref[h] += jnp.sum(ds, axis=1, keepdims=True)
                dwm_ref[h] += _dot(dsb, vh, "nt")
                dv_ref[rows, cols] = _dot(wmt_ref[h], dsb)
        dv = dv_ref[...]
        dnv_ref[...] += jnp.sum(dv * xhat, axis=0, keepdims=True)
        dxh = dv * nv
        dgv = r * (dxh - jnp.mean(dxh, axis=-1, keepdims=True) - xhat * jnp.mean(dxh * xhat, axis=-1, keepdims=True))
        dzv_ref[...] = (dgv * _gelu_grad(zv)).astype(BF16)

    full3 = lambda shp: pl.BlockSpec(shp, lambda i: (0, 0, 0))
    rows_d = pl.BlockSpec((tr, d), lambda i: (i, 0))
    return pl.pallas_call(
        body, name=name, grid=(t // tr,),
        in_specs=[rows_d, pl.BlockSpec((tr, d), lambda i: (i, col0)), pl.BlockSpec((tr, d), lambda i: (i, col0 + 1)),
                  _vec(d), full3(wm.shape), full3(wmt.shape), full3(bs.shape)],
        out_specs=[rows_d, rows_d, _vec(d), full3((nh, CHUNK, CHUNK)), full3((nh, CHUNK, 1))],
        out_shape=[jax.ShapeDtypeStruct((t, d), BF16), jax.ShapeDtypeStruct((t, d), BF16),
                   jax.ShapeDtypeStruct((1, d), F32), jax.ShapeDtypeStruct((nh, CHUNK, CHUNK), F32),
                   jax.ShapeDtypeStruct((nh, CHUNK, 1), F32)],
        scratch_shapes=[pltpu.VMEM((tr, d), F32)], compiler_params=_cparams(),
    )(dy, z, z, norm_v, wm, wmt, bs)


def _block(ref, axis, size, k):
    start = pl.multiple_of(k * size, size)
    if axis == 0:
        return ref.at[pl.ds(start, size), :]
    return ref.at[:, pl.ds(start, size)]


def _place():
    x, y, c = lax.axis_index("x"), lax.axis_index("y"), lax.axis_index("c")
    chips = [(1 - x, y), (x, 1 - y), (1 - x, 1 - y)]
    return x, y, c, chips


def _dev(x, y, c):
    return 4 * x + 2 * y + c


def gather_task(shards, axes):
    n = len(shards)
    sizes = [s.shape[ax] for s, ax in zip(shards, axes)]
    out_shape = [
        jax.ShapeDtypeStruct((s.shape[0] * N_DEV, s.shape[1]) if ax == 0 else (s.shape[0], s.shape[1] * N_DEV), s.dtype)
        for s, ax in zip(shards, axes)
    ]

    def copy(ins, outs, send_sems, recv_sems, t, k, block, to, from_input=False):
        dst = _block(outs[t], axes[t], sizes[t], _dev(*block))
        return pltpu.make_async_remote_copy(
            src_ref=ins[t] if from_input else dst, dst_ref=dst,
            send_sem=send_sems.at[t * 7 + k], recv_sem=recv_sems.at[t * 7 + k],
            device_id=to, device_id_type=MESH_DT)

    def local(ins, outs, local_sems, t, me):
        return pltpu.make_async_copy(ins[t], _block(outs[t], axes[t], sizes[t], _dev(*me)), local_sems.at[t])

    def start(ins, outs, send_sems, recv_sems, local_sems):
        x, y, c, chips = _place()
        me, sibling = (x, y, c), (x, y, 1 - c)
        for t in range(n):
            local(ins, outs, local_sems, t, me).start()
        for t in range(n):
            copy(ins, outs, send_sems, recv_sems, t, 0, me, sibling, True).start()
            for j, chip in enumerate(chips):
                copy(ins, outs, send_sems, recv_sems, t, 1 + j, me, (*chip, c), True).start()

    def late(ins, outs, send_sems, recv_sems, local_sems):
        x, y, c, chips = _place()
        me, sibling = (x, y, c), (x, y, 1 - c)
        for t in range(n):
            for j, chip in enumerate(chips):
                copy(ins, outs, send_sems, recv_sems, t, 1 + j, (*chip, c), me).wait_recv()
                copy(ins, outs, send_sems, recv_sems, t, 4 + j, (*chip, c), sibling).start()

    def finish(ins, outs, send_sems, recv_sems, local_sems):
        x, y, c, chips = _place()
        me, sibling = (x, y, c), (x, y, 1 - c)
        for t in range(n):
            copy(ins, outs, send_sems, recv_sems, t, 0, sibling, me).wait_recv()
            for j, chip in enumerate(chips):
                copy(ins, outs, send_sems, recv_sems, t, 4 + j, (*chip, 1 - c), me).wait_recv()
        for t in range(n):
            copy(ins, outs, send_sems, recv_sems, t, 0, me, sibling, True).wait_send()
            for j, chip in enumerate(chips):
                copy(ins, outs, send_sems, recv_sems, t, 1 + j, me, (*chip, c), True).wait_send()
                copy(ins, outs, send_sems, recv_sems, t, 4 + j, (*chip, c), sibling).wait_send()
            local(ins, outs, local_sems, t, me).wait()

    return CommTask(shards, out_shape, (7 * n, 7 * n, n), start, late, finish)


def _blk3(shape2, axis):
    r, c = shape2
    return (r // N_DEV, c) if axis == 0 else (r, c // N_DEV)


def _no_late(ins, outs, send_sems, recv_sems, local_sems):
    pass


def to_sibling_task(grads, axes):
    n = len(grads)
    blks = [_blk3(g.shape, ax) for g, ax in zip(grads, axes)]
    sizes = [b[ax] for b, ax in zip(blks, axes)]

    def copies(ins, outs, send_sems, recv_sems):
        x, y, c, _ = _place()
        return [pltpu.make_async_remote_copy(
            src_ref=_block(ins[t], axes[t], sizes[t], 2 * i + (1 - c)), dst_ref=outs[t].at[i],
            send_sem=send_sems.at[t * N_CHIP + i], recv_sem=recv_sems.at[t * N_CHIP + i],
            device_id=(x, y, 1 - c), device_id_type=MESH_DT) for t in range(n) for i in range(N_CHIP)]

    def start(ins, outs, send_sems, recv_sems, local_sems):
        for cp in copies(ins, outs, send_sems, recv_sems):
            cp.start()

    def finish(ins, outs, send_sems, recv_sems, local_sems):
        cps = copies(ins, outs, send_sems, recv_sems)
        for cp in cps:
            cp.wait_recv()
        for cp in cps:
            cp.wait_send()

    out_shape = [jax.ShapeDtypeStruct((N_CHIP,) + b, g.dtype) for b, g in zip(blks, grads)]
    return CommTask(grads, out_shape, (N_CHIP * n, N_CHIP * n, 1), start, _no_late, finish)


def across_chips_task(parts):
    n = len(parts)

    def copies(ins, outs, send_sems, recv_sems):
        x, y, c, chips = _place()
        my_chip = 2 * x + y
        return [pltpu.make_async_remote_copy(
            src_ref=ins[t].at[2 * chip[0] + chip[1]], dst_ref=outs[t].at[my_chip],
            send_sem=send_sems.at[t * 3 + j], recv_sem=recv_sems.at[t * 3 + j],
            device_id=(*chip, c), device_id_type=MESH_DT) for t in range(n) for j, chip in enumerate(chips)]

    def mine(ins, outs, local_sems):
        x, y, _, _ = _place()
        my_chip = 2 * x + y
        return [pltpu.make_async_copy(ins[t].at[my_chip], outs[t].at[my_chip], local_sems.at[t]) for t in range(n)]

    def start(ins, outs, send_sems, recv_sems, local_sems):
        for cp in mine(ins, outs, local_sems):
            cp.start()
        for cp in copies(ins, outs, send_sems, recv_sems):
            cp.start()

    def finish(ins, outs, send_sems, recv_sems, local_sems):
        cps = copies(ins, outs, send_sems, recv_sems)
        for cp in cps:
            cp.wait_recv()
        for cp in cps:
            cp.wait_send()
        for cp in mine(ins, outs, local_sems):
            cp.wait()

    out_shape = [jax.ShapeDtypeStruct(p.shape, p.dtype) for p in parts]
    return CommTask(parts, out_shape, (3 * n, 3 * n, n), start, _no_late, finish)


def run_tasks(tasks, name):
    t_in = [len(t.inputs) for t in tasks]
    t_out = [len(t.out_shape) for t in tasks]

    def body(*refs):
        pos, views = 0, []
        for k in t_in:
            views.append([refs[pos:pos + k]])
            pos += k
        for v, k in zip(views, t_out):
            v.append(refs[pos:pos + k])
            pos += k
        for i, v in enumerate(views):
            v.extend(refs[pos + 3 * i:pos + 3 * i + 3])
        for phase in ("start", "late", "finish"):
            for t, v in zip(tasks, views):
                getattr(t, phase)(*v)

    any_spec = pl.BlockSpec(memory_space=pl.ANY)
    res = pl.pallas_call(
        body, name=name, in_specs=[any_spec] * sum(t_in), out_specs=[any_spec] * sum(t_out),
        out_shape=[s for t in tasks for s in t.out_shape], input_output_aliases=_task_aliases(tasks, 0, 0),
        scratch_shapes=[pltpu.SemaphoreType.DMA((k,)) for t in tasks for k in t.n_sems],
    )(*[a for t in tasks for a in t.inputs])
    res, out, pos = list(res), [], 0
    for k in t_out:
        out.append(res[pos:pos + k])
        pos += k
    return out


_HBM_SPEC = pl.BlockSpec(memory_space=pl.ANY)
_SEM_SPEC = pl.BlockSpec(memory_space=pltpu.SEMAPHORE)
_DATAFLOW = pltpu.SideEffectType.DATAFLOW_SIDE_EFFECTING


def _full_shape(s, ax):
    return (s.shape[0] * N_DEV, s.shape[1]) if ax == 0 else (s.shape[0], s.shape[1] * N_DEV)


def _level1_copy(src, landing, axis, size, send_sems, recv_sems, slot, sender, to):
    dst = _block(landing, axis, size, _dev(*sender))
    return pltpu.make_async_remote_copy(src_ref=src, dst_ref=dst, send_sem=send_sems.at[slot],
                                        recv_sem=recv_sems.at[slot], device_id=to, device_id_type=MESH_DT)


def place_own_block(shard, landing, axis, me, name, tr=256):
    r, c = shard.shape
    tr = _row_tile(r, tr)
    nrb = r // tr
    if axis == 0:
        o_map = lambda i, me_ref: (me_ref[0] * nrb + i, 0)
    else:
        o_map = lambda i, me_ref: (i, me_ref[0])

    def body(me_ref, x_ref, land_ref, o_ref):
        o_ref[...] = x_ref[...]

    return pl.pallas_call(
        body, name=name,
        grid_spec=pltpu.PrefetchScalarGridSpec(
            num_scalar_prefetch=1, grid=(nrb,),
            in_specs=[pl.BlockSpec((tr, c), lambda i, me_ref: (i, 0)), pl.BlockSpec(memory_space=pl.ANY)],
            out_specs=pl.BlockSpec((tr, c), o_map)),
        out_shape=jax.ShapeDtypeStruct(landing.shape, landing.dtype), input_output_aliases={2: 0},
        compiler_params=_cparams(),
    )(me, shard, landing)


def gather_start(shards, axes, sizes, groups, name):
    n = len(shards)

    def body(*refs):
        srcs, lands, sems = refs[:n], refs[n:2 * n], refs[4 * n:]
        x, y, c, chips = _place()
        me = (x, y, c)
        targets = [(x, y, 1 - c)] + [(*chip, c) for chip in chips]
        for g, members in enumerate(groups):
            for m, t in enumerate(members):
                for k, to in enumerate(targets):
                    _level1_copy(srcs[t], lands[t], axes[t], sizes[t], sems[2 * g], sems[2 * g + 1], 4 * m + k,
                                 me, to).start()

    landing = [lax.empty(_full_shape(s, ax), s.dtype) for s, ax in zip(shards, axes)]
    out = pl.pallas_call(
        body, name=name,
        out_shape=[jax.ShapeDtypeStruct(b.shape, b.dtype) for b in shards + landing]
        + [pltpu.SemaphoreType.DMA((4 * len(members),)) for members in groups for _ in (0, 1)],
        in_specs=[_HBM_SPEC] * (2 * n), out_specs=[_HBM_SPEC] * (2 * n) + [_SEM_SPEC] * (2 * len(groups)),
        input_output_aliases={i: i for i in range(2 * n)},
        compiler_params=pltpu.CompilerParams(has_side_effects=_DATAFLOW),
    )(*shards, *landing)
    out = list(out)
    sems = out[2 * n:]
    return out[:n], out[n:2 * n], [(sems[2 * g], sems[2 * g + 1]) for g in range(len(groups))]


def gather_wait(shards, landing, axes, sizes, send_sems, recv_sems, after, name):
    n = len(landing)

    def body(*refs):
        srcs, lands = refs[:n], refs[n:2 * n]
        send, recv = refs[2 * n], refs[2 * n + 1]
        x, y, c, chips = _place()
        me = (x, y, c)
        peers = [(x, y, 1 - c)] + [(*chip, c) for chip in chips]
        for t in range(n):
            for k, peer in enumerate(peers):
                _level1_copy(srcs[t], lands[t], axes[t], sizes[t], send, recv, 4 * t + k, me, peer).wait_send()
                _level1_copy(srcs[t], lands[t], axes[t], sizes[t], send, recv, 4 * t + k, peer, me).wait_recv()

    out = pl.pallas_call(
        body, name=name, out_shape=[jax.ShapeDtypeStruct(b.shape, b.dtype) for b in shards + landing],
        in_specs=[_HBM_SPEC] * (2 * n) + [_SEM_SPEC, _SEM_SPEC, pl.BlockSpec(memory_space=pl.ANY)],
        out_specs=[_HBM_SPEC] * (2 * n), input_output_aliases={i: i for i in range(2 * n)},
        compiler_params=pltpu.CompilerParams(has_side_effects=_DATAFLOW),
    )(*shards, *landing, send_sems, recv_sems, after)
    return list(out)[n:]


def forward_task(landing, axes, sizes):
    n = len(landing)

    def forward(lands, send_sems, recv_sems, t, j, chip_core):
        x, y, c, _ = _place()
        blk = _block(lands[t], axes[t], sizes[t], _dev(*chip_core))
        return pltpu.make_async_remote_copy(src_ref=blk, dst_ref=blk, send_sem=send_sems.at[3 * t + j],
                                            recv_sem=recv_sems.at[3 * t + j], device_id=(x, y, 1 - c),
                                            device_id_type=MESH_DT)

    def start(ins, lands, send_sems, recv_sems, local_sems):
        _, _, c, chips = _place()
        for t in range(n):
            for j, chip in enumerate(chips):
                forward(lands, send_sems, recv_sems, t, j, (*chip, c)).start()

    def finish(ins, lands, send_sems, recv_sems, local_sems):
        _, _, c, chips = _place()
        for t in range(n):
            for j, chip in enumerate(chips):
                forward(lands, send_sems, recv_sems, t, j, (*chip, 1 - c)).wait_recv()
        for t in range(n):
            for j, chip in enumerate(chips):
                forward(lands, send_sems, recv_sems, t, j, (*chip, c)).wait_send()

    out_shape = [jax.ShapeDtypeStruct(b.shape, b.dtype) for b in landing]
    return CommTask(landing, out_shape, (3 * n, 3 * n, 1), start, _no_late, finish, in_place=True)


def rs_chip_sum(grad, recv, axis, core, name, tr=512):
    br, bc = _blk3(grad.shape, axis)
    tr = _row_tile(br, tr)
    nrb = br // tr

    if axis == 0:
        g_map = lambda i, r, c_ref: ((2 * i + c_ref[0]) * nrb + r, 0)
    else:
        g_map = lambda i, r, c_ref: (r, 2 * i + c_ref[0])

    def body(c_ref, g_ref, r_ref, o_ref):
        o_ref[...] = (g_ref[...].astype(F32) + r_ref[...].astype(F32)).astype(BF16)

    return pl.pallas_call(
        body, name=name,
        grid_spec=pltpu.PrefetchScalarGridSpec(
            num_scalar_prefetch=1, grid=(N_CHIP, nrb),
            in_specs=[pl.BlockSpec((tr, bc), g_map), pl.BlockSpec((None, tr, bc), lambda i, r, c_ref: (i, r, 0))],
            out_specs=pl.BlockSpec((None, tr, bc), lambda i, r, c_ref: (i, r, 0))),
        out_shape=jax.ShapeDtypeStruct((N_CHIP, br, bc), BF16), compiler_params=_cparams(),
    )(core, grad, recv)


def _adamw(w, g, m, v):
    m = ADAM_B1 * m + (1.0 - ADAM_B1) * g
    v = ADAM_B2 * v + (1.0 - ADAM_B2) * (g * g)
    m_hat = m / (1.0 - ADAM_B1 ** ADAM_STEP)
    v_hat = v / (1.0 - ADAM_B2 ** ADAM_STEP)
    delta = -ADAM_LR * (m_hat / (jnp.sqrt(v_hat) + ADAM_EPS) + ADAM_WD * w)
    return delta, m, v


def _sum_chips(p_ref):
    g = p_ref[0].astype(F32)
    for i in range(1, N_CHIP):
        g = g + p_ref[i].astype(F32)
    return g


def adam_sharded(parts, w, m, v, name, tr=256):
    r, c = w.shape
    assert parts.shape[2] == c
    tr = _row_tile(r, tr)

    def body(p_ref, w_ref, m_ref, v_ref, g_ref, d_ref, nm_ref, nv_ref):
        g = _sum_chips(p_ref)
        delta, nm, nv = _adamw(w_ref[...], g, m_ref[...], v_ref[...])
        g_ref[...] = g
        d_ref[...] = delta
        nm_ref[...] = nm
        nv_ref[...] = nv

    sp = pl.BlockSpec((tr, c), lambda i: (i, 0))
    return pl.pallas_call(
        body, name=name, grid=(r // tr,),
        in_specs=[pl.BlockSpec((N_CHIP, tr, c), lambda i: (0, i, 0)), sp, sp, sp],
        out_specs=[sp, sp, sp, sp], out_shape=[jax.ShapeDtypeStruct((r, c), F32)] * 4,
        compiler_params=_cparams(),
    )(parts, w, m, v)


def adam_small(items, name):
    n = len(items)

    def body(*refs):
        for i in range(n):
            g_ref, w_ref, m_ref, v_ref = refs[4 * i:4 * i + 4]
            d_ref, nm_ref, nv_ref = refs[4 * n + 3 * i:4 * n + 3 * i + 3]
            delta, nm, nv = _adamw(w_ref[...], g_ref[...], m_ref[...], v_ref[...])
            d_ref[...] = delta
            nm_ref[...] = nm
            nv_ref[...] = nv

    out = pl.pallas_call(
        body, name=name, out_shape=[jax.ShapeDtypeStruct(it[1].shape, F32) for it in items for _ in range(3)],
        compiler_params=_cparams(),
    )(*[a for it in items for a in it])
    return [tuple(out[3 * i:3 * i + 3]) for i in range(n)]


def sum_devices(gathered, name, tr=512):
    _, r, c = gathered.shape
    tr = _row_tile(r, tr)

    def body(x_ref, o_ref):
        s = x_ref[0]
        for k in range(1, N_DEV):
            s = s + x_ref[k]
        o_ref[...] = s

    return pl.pallas_call(
        body, name=name, grid=(r // tr,), in_specs=[pl.BlockSpec((N_DEV, tr, c), lambda i: (0, i, 0))],
        out_specs=pl.BlockSpec((tr, c), lambda i: (i, 0)), out_shape=jax.ShapeDtypeStruct((r, c), F32),
        compiler_params=_cparams(),
    )(gathered)


def _pad_to(a, axis, mult):
    size = a.shape[axis]
    pad = (-size) % mult
    if pad == 0:
        return a
    cfg = [(0, 0)] * a.ndim
    cfg[axis] = (0, pad)
    return jnp.pad(a, cfg)


def _as2d(a):
    if a.ndim == 1:
        return a.reshape(1, -1)
    return a.reshape(-1, a.shape[-1])


def kernel(x, p, norm_ffn1, w1_gate, w1_up, w1_down, norm_mix, w_in, ssm_log_dt, ssm_a_re, ssm_a_im, ssm_b_re, ssm_b_im, ssm_c_re, ssm_c_im, ssm_d, ssm_w_glu, gmlp_norm_v, gmlp_w_s, gmlp_b_s, norm_ssm_out, norm_gmlp_out, w_out, norm_ffn2, w2_gate, w2_up, w2_down, norm_ple, w_ple_gate, w_ple_proj, norm_final, loss_target, m_norm_ffn1, m_w1_gate, m_w1_up, m_w1_down, m_norm_mix, m_w_in, m_ssm_log_dt, m_ssm_a_re, m_ssm_a_im, m_ssm_b_re, m_ssm_b_im, m_ssm_c_re, m_ssm_c_im, m_ssm_d, m_ssm_w_glu, m_gmlp_norm_v, m_gmlp_w_s, m_gmlp_b_s, m_norm_ssm_out, m_norm_gmlp_out, m_w_out, m_norm_ffn2, m_w2_gate, m_w2_up, m_w2_down, m_norm_ple, m_w_ple_gate, m_w_ple_proj, m_norm_final, v_norm_ffn1, v_w1_gate, v_w1_up, v_w1_down, v_norm_mix, v_w_in, v_ssm_log_dt, v_ssm_a_re, v_ssm_a_im, v_ssm_b_re, v_ssm_b_im, v_ssm_c_re, v_ssm_c_im, v_ssm_d, v_ssm_w_glu, v_gmlp_norm_v, v_gmlp_w_s, v_gmlp_b_s, v_norm_ssm_out, v_norm_gmlp_out, v_w_out, v_norm_ffn2, v_w2_gate, v_w2_up, v_w2_down, v_norm_ple, v_w_ple_gate, v_w_ple_proj, v_norm_final):
    weights = dict(
        norm_ffn1=norm_ffn1, w1_gate=w1_gate, w1_up=w1_up, w1_down=w1_down, norm_mix=norm_mix, w_in=w_in,
        ssm_log_dt=ssm_log_dt, ssm_a_re=ssm_a_re, ssm_a_im=ssm_a_im, ssm_b_re=ssm_b_re, ssm_b_im=ssm_b_im,
        ssm_c_re=ssm_c_re, ssm_c_im=ssm_c_im, ssm_d=ssm_d, ssm_w_glu=ssm_w_glu, gmlp_norm_v=gmlp_norm_v,
        gmlp_w_s=gmlp_w_s, gmlp_b_s=gmlp_b_s, norm_ssm_out=norm_ssm_out, norm_gmlp_out=norm_gmlp_out, w_out=w_out,
        norm_ffn2=norm_ffn2, w2_gate=w2_gate, w2_up=w2_up, w2_down=w2_down, norm_ple=norm_ple,
        w_ple_gate=w_ple_gate, w_ple_proj=w_ple_proj, norm_final=norm_final)
    moments_m = dict(
        norm_ffn1=m_norm_ffn1, w1_gate=m_w1_gate, w1_up=m_w1_up, w1_down=m_w1_down, norm_mix=m_norm_mix, w_in=m_w_in,
        ssm_log_dt=m_ssm_log_dt, ssm_a_re=m_ssm_a_re, ssm_a_im=m_ssm_a_im, ssm_b_re=m_ssm_b_re, ssm_b_im=m_ssm_b_im,
        ssm_c_re=m_ssm_c_re, ssm_c_im=m_ssm_c_im, ssm_d=m_ssm_d, ssm_w_glu=m_ssm_w_glu, gmlp_norm_v=m_gmlp_norm_v,
        gmlp_w_s=m_gmlp_w_s, gmlp_b_s=m_gmlp_b_s, norm_ssm_out=m_norm_ssm_out, norm_gmlp_out=m_norm_gmlp_out,
        w_out=m_w_out, norm_ffn2=m_norm_ffn2, w2_gate=m_w2_gate, w2_up=m_w2_up, w2_down=m_w2_down,
        norm_ple=m_norm_ple, w_ple_gate=m_w_ple_gate, w_ple_proj=m_w_ple_proj, norm_final=m_norm_final)
    moments_v = dict(
        norm_ffn1=v_norm_ffn1, w1_gate=v_w1_gate, w1_up=v_w1_up, w1_down=v_w1_down, norm_mix=v_norm_mix, w_in=v_w_in,
        ssm_log_dt=v_ssm_log_dt, ssm_a_re=v_ssm_a_re, ssm_a_im=v_ssm_a_im, ssm_b_re=v_ssm_b_re, ssm_b_im=v_ssm_b_im,
        ssm_c_re=v_ssm_c_re, ssm_c_im=v_ssm_c_im, ssm_d=v_ssm_d, ssm_w_glu=v_ssm_w_glu, gmlp_norm_v=v_gmlp_norm_v,
        gmlp_w_s=v_gmlp_w_s, gmlp_b_s=v_gmlp_b_s, norm_ssm_out=v_norm_ssm_out, norm_gmlp_out=v_norm_gmlp_out,
        w_out=v_w_out, norm_ffn2=v_norm_ffn2, w2_gate=v_w2_gate, w2_up=v_w2_up, w2_down=v_w2_down,
        norm_ple=v_norm_ple, w_ple_gate=v_w_ple_gate, w_ple_proj=v_w_ple_proj, norm_final=v_norm_final)
    names = list(weights)

    xs = x[0]
    ps = p[0, 0].astype(BF16)
    tgt = loss_target[0]
    d_model = xs.shape[1]
    d_ssm = d_model // 2
    n_groups = d_ssm // SSM_GROUP

    transposed = ("w1_gate", "w1_up", "w2_gate", "w2_up")
    big = {
        "w1_gate": 0, "w1_up": 0, "w1_down": 0, "w_in": 1, "ssm_w_glu": 0, "w_out": 0,
        "w2_gate": 0, "w2_up": 0, "w2_down": 0, "w_ple_gate": 0, "w_ple_proj": 1}
    big_names = list(big)

    def view(a, k):
        return a[0].T if k in transposed else a[0]

    def unview(a, k):
        return a.T[None] if k in transposed else a[None]

    shard = {k: _pad_to(view(weights[k], k).astype(BF16), big[k], LANE) for k in big_names}
    W = {}

    abar_r, abar_i, bbar_r, bbar_i = _ssm_discretize(ssm_log_dt[0], ssm_a_re[0], ssm_a_im[0], ssm_b_re[0], ssm_b_im[0])
    bc_r = _block_diag(jnp.swapaxes(bbar_r, 1, 2)).astype(BF16)
    bc_i = _block_diag(jnp.swapaxes(bbar_i, 1, 2)).astype(BF16)
    cc_r = _block_diag(jnp.swapaxes(ssm_c_re[0], 1, 2)).astype(BF16)
    cc_i = _block_diag(jnp.swapaxes(ssm_c_im[0], 1, 2)).astype(BF16)
    apw_f = _scan_constants(abar_r, abar_i, False)
    apw_b = _scan_constants(abar_r, abar_i, True)
    causal = jnp.tril(jnp.ones((CHUNK, CHUNK), dtype=bool))
    wm = jnp.where(causal[None], gmlp_w_s[0], 0.0).astype(BF16)
    wmt = jnp.swapaxes(wm, 1, 2)
    bs = gmlp_b_s[0][:, :, None]

    groups = [["w1_gate"], ["w1_up"], ["w1_down"], ["w_in", "ssm_w_glu", "w_out"], ["w2_gate"], ["w2_up"],
              ["w2_down", "w_ple_gate", "w_ple_proj"]]
    order = [k for g in groups for k in g]
    place = {k: i for i, k in enumerate(order)}
    me = (4 * lax.axis_index("x") + 2 * lax.axis_index("y") + lax.axis_index("c")).astype(jnp.int32).reshape(1)
    size = {k: shard[k].shape[big[k]] for k in order}
    in_flight, landing, sems = gather_start([shard[k] for k in order], [big[k] for k in order],
                                            [size[k] for k in order], [[place[k] for k in g] for g in groups],
                                            "gather_start")
    landing = [place_own_block(in_flight[place[k]], landing[place[k]], big[k], me, "place_" + k) for k in order]

    def landed(g, after):
        members = [place[k] for k in groups[g]]
        axes_g, sizes_g = [big[k] for k in groups[g]], [size[k] for k in groups[g]]
        bufs = gather_wait([in_flight[i] for i in members], [landing[i] for i in members], axes_g, sizes_g,
                           *sems[g], after, "gather_wait_%d" % g)
        return forward_task(bufs, axes_g, sizes_g)

    def arrive(g, after):
        W.update(zip(groups[g], run_tasks([landed(g, after)], "gather_forward_%d" % g)[0]))

    def arrive_during(g, after, fn, *a, **kw):
        out, (got,) = fn(*a, tasks=[landed(g, after)], **kw)
        W.update(zip(groups[g], got))
        return out

    xn1 = rmsnorm_fwd(xs, norm_ffn1, "norm_ffn1")
    arrive(0, xn1)
    gate1 = matmul(xn1, W["w1_gate"], "nt", "ffn1_gate")
    arrive(1, gate1)
    gate1, up1, act1 = ffn_up(xn1, W["w1_up"], gate1, "ffn1_up")
    arrive(2, act1)
    h1 = matmul(act1, W["w1_down"], "nn", "ffn1_down", res=xs, scale=0.5)
    arrive(3, h1)
    xn2 = rmsnorm_fwd(h1, norm_mix, "norm_mix")
    z = matmul(xn2, W["w_in"], "nn", "proj_in")
    y_pre, yg, sr, si = s5_fwd(z, bc_r, bc_i, cc_r, cc_i, apw_f, ssm_d, "s5_fwd")
    glin = matmul(yg, W["ssm_w_glu"], "nn", "ssm_glu")
    y_gmlp = gmlp_fwd(z, gmlp_norm_v, wm, bs, "gmlp_fwd")
    ycat = mix_out_fwd(y_pre, glin, y_gmlp, norm_ssm_out, norm_gmlp_out, "mix_out")
    h2 = arrive_during(4, ycat, matmul, ycat, W["w_out"], "nn", "proj_out", res=h1)
    xn3 = rmsnorm_fwd(h2, norm_ffn2, "norm_ffn2")
    gate2 = arrive_during(5, xn3, matmul, xn3, W["w2_gate"], "nt", "ffn2_gate")
    gate2, up2, act2 = arrive_during(6, gate2, ffn_up, xn3, W["w2_up"], gate2, "ffn2_up")
    h3 = matmul(act2, W["w2_down"], "nn", "ffn2_down", res=h2, scale=0.5)
    xn4 = rmsnorm_fwd(h3, norm_ple, "norm_ple")
    pg_lin = matmul(xn4, W["w_ple_gate"], "nn", "ple_gate")
    pp = matmul(ps, W["w_ple_proj"], "nn", "ple_proj")
    h4 = ple_fwd(h3, pg_lin, pp, "ple_fwd")
    dh4, loss_part, g_norm_final = final_loss(h4, tgt, norm_final.reshape(1, -1), "final_loss")
    loss = lax.psum(loss_part[0, 0], ("x", "y", "c"))

    G = {}
    reduced = {}
    chip_part = {}
    wait_sibling, wait_chips = [], []
    core = lax.axis_index("c").astype(jnp.int32).reshape(1)

    def grad(name_, value):
        G[name_] = value
        wait_sibling.append(name_)

    def carry(fn, *a, levels="ab", extra=None, **kw):
        tasks, kinds = [], []
        if extra is not None:
            tasks.append(extra[0])
            kinds.append(("x", extra[1]))
        if "a" in levels and wait_sibling:
            group = list(wait_sibling)
            wait_sibling.clear()
            tasks.append(to_sibling_task([G[k] for k in group], [big[k] for k in group]))
            kinds.append(("a", group))
        if "b" in levels and wait_chips:
            group = list(wait_chips)
            wait_chips.clear()
            tasks.append(across_chips_task([chip_part[k] for k in group]))
            kinds.append(("b", group))
        if not tasks:
            return fn(*a, **kw)
        out, task_outs = fn(*a, tasks=tasks, **kw)
        for (kind, group), outs in zip(kinds, task_outs):
            if kind == "x":
                group(outs)
                continue
            for k, r in zip(group, outs):
                if kind == "a":
                    chip_part[k] = rs_chip_sum(G[k], r, big[k], core, "rs_sum_" + k)
                    wait_chips.append(k)
                else:
                    reduced[k] = r
        return out

    small = {}
    small["norm_final"] = g_norm_final
    dpp, dpg = ple_bwd(dh4, pg_lin, pp, "ple_bwd")
    grad("w_ple_proj", matmul(ps, dpp, "tn", "grad_ple_proj", out_dtype=BF16))
    grad("w_ple_gate", carry(matmul, xn4, dpg, "tn", "grad_ple_gate", out_dtype=BF16))
    dxn4 = carry(matmul, dpg, W["w_ple_gate"], "nt", "ple_gate_bwd")
    dh3, dh3b, small["norm_ple"] = rmsnorm_bwd(dxn4, h3, norm_ple, dh4, "norm_ple_bwd")

    def ffn_bwd(tag, dhb, xn, gate, up, act, wg, wu, wd, extra=None, last_levels="ab"):
        dgate, dup = carry(ffn_bwd_act, dhb, W[wd], gate, up, tag + "_act_bwd", extra=extra)
        grad(wd, carry(matmul, act, dhb, "tn", tag + "_grad_down", out_dtype=BF16, scale=0.5))
        grad(wg, carry(matmul, dgate, xn, "tn", tag + "_grad_gate", out_dtype=BF16))
        grad(wu, carry(matmul, dup, xn, "tn", tag + "_grad_up", out_dtype=BF16))
        dxn = carry(matmul, dgate, W[wg], "nn", tag + "_gate_bwd")
        return carry(matmul, dup, W[wu], "nn", tag + "_up_bwd", res=dxn, levels=last_levels)

    dxn3 = ffn_bwd("ffn2", dh3b, xn3, gate2, up2, act2, "w2_gate", "w2_up", "w2_down", last_levels="a")
    dh2, dh2b, small["norm_ffn2"] = rmsnorm_bwd(dxn3, h2, norm_ffn2, dh3, "norm_ffn2_bwd")

    grad("w_out", matmul(ycat, dh2b, "tn", "grad_out", out_dtype=BF16))
    dycat = carry(matmul, dh2b, W["w_out"], "nt", "proj_out_bwd", levels="a")
    dyg_direct, dglin, dy_gmlp, small["norm_ssm_out"], small["norm_gmlp_out"] = mix_out_bwd(
        dycat, y_pre, glin, y_gmlp, norm_ssm_out, norm_gmlp_out, "mix_out_bwd")
    grad("ssm_w_glu", matmul(yg, dglin, "tn", "grad_glu", out_dtype=BF16))
    dyg = carry(matmul, dglin, W["ssm_w_glu"], "nt", "ssm_glu_bwd", res=dyg_direct, levels="a")
    du, small["ssm_d"], gc_r, gc_i, gb_r, gb_i, ga_r, ga_i = carry(
        s5_bwd, dyg, y_pre, z, sr, si, bc_r, bc_i, cc_r, cc_i, apw_b, ssm_d, "s5_bwd")
    dzu, dzv, small["gmlp_norm_v"], g_wm, g_bs = gmlp_bwd(dy_gmlp, z, gmlp_norm_v, wm, wmt, bs, "gmlp_bwd")
    small["gmlp_w_s"] = g_wm
    small["gmlp_b_s"] = g_bs
    small["c_re"] = _block_diag_extract(gc_r, SSM_GROUP, SSM_STATE)
    small["c_im"] = _block_diag_extract(gc_i, SSM_GROUP, SSM_STATE)
    small["bbar_r"] = jnp.swapaxes(_block_diag_extract(gb_r, SSM_GROUP, SSM_STATE), 1, 2)
    small["bbar_i"] = jnp.swapaxes(_block_diag_extract(gb_i, SSM_GROUP, SSM_STATE), 1, 2)
    small["abar_r"] = jnp.sum(ga_r, axis=0).reshape(n_groups, SSM_STATE)
    small["abar_i"] = jnp.sum(ga_i, axis=0).reshape(n_groups, SSM_STATE)

    dz = jnp.concatenate([du, dzu, dzv], axis=1)
    grad("w_in", matmul(xn2, dz, "tn", "grad_in", out_dtype=BF16))
    dxn2 = carry(matmul, dz, W["w_in"], "nt", "proj_in_bwd")
    dh1, dh1b, small["norm_mix"] = rmsnorm_bwd(dxn2, h1, norm_mix, dh2, "norm_mix_bwd")

    def pack(parts):
        flat = jnp.concatenate([v.reshape(-1) for v in parts.values()])
        return _pad_to(flat, 0, SUBLANE * LANE).reshape(-1, LANE), flat.shape[0]

    def unpack(everyones, n, parts, tag):
        rows = everyones.shape[0] // N_DEV
        summed = sum_devices(everyones.reshape(N_DEV, rows, LANE), "sum_" + tag).reshape(-1)[:n]
        out, off = {}, 0
        for k, v in parts.items():
            out[k] = summed[off:off + v.size].reshape(v.shape)
            off += v.size
        return out

    early = dict(small)
    flat_early, n_early = pack(early)
    small_landed = []
    dxn1 = ffn_bwd("ffn1", dh1b, xn1, gate1, up1, act1, "w1_gate", "w1_up", "w1_down",
                   extra=(gather_task([flat_early], [0]), small_landed.extend))
    tot = unpack(small_landed[0], n_early, early, "small")
    grad_x, _, g_norm_ffn1 = rmsnorm_bwd(dxn1, xs, norm_ffn1, dh1, "norm_ffn1_bwd")
    assert not wait_sibling and not wait_chips and set(reduced) == set(big_names)
    last = {"norm_ffn1": g_norm_ffn1}
    flat_last, n_last = pack(last)
    ((everyones_last,),) = run_tasks([gather_task([flat_last], [0])], "gather_last")
    tot.update(unpack(everyones_last, n_last, last, "last"))

    out_g, out_d, out_m, out_v = {}, {}, {}, {}
    for k in big_names:
        g, dl, nm, nv = adam_sharded(reduced[k], view(weights[k], k), view(moments_m[k], k), view(moments_v[k], k),
                                     "adam_" + k)
        out_g[k], out_d[k], out_m[k], out_v[k] = unview(g, k), unview(dl, k), unview(nm, k), unview(nv, k)

    _, ssm_vjp = jax.vjp(_ssm_discretize, ssm_log_dt[0], ssm_a_re[0], ssm_a_im[0], ssm_b_re[0], ssm_b_im[0])
    g_log_dt, g_a_re, g_a_im, g_b_re, g_b_im = ssm_vjp((tot["abar_r"], tot["abar_i"], tot["bbar_r"], tot["bbar_i"]))
    small_grads = {
        "norm_ffn1": tot["norm_ffn1"], "norm_mix": tot["norm_mix"], "ssm_log_dt": g_log_dt, "ssm_a_re": g_a_re,
        "ssm_a_im": g_a_im, "ssm_b_re": g_b_re, "ssm_b_im": g_b_im, "ssm_c_re": tot["c_re"], "ssm_c_im": tot["c_im"],
        "ssm_d": tot["ssm_d"], "gmlp_norm_v": tot["gmlp_norm_v"],
        "gmlp_w_s": jnp.where(causal[None], tot["gmlp_w_s"], 0.0), "gmlp_b_s": tot["gmlp_b_s"],
        "norm_ssm_out": tot["norm_ssm_out"], "norm_gmlp_out": tot["norm_gmlp_out"], "norm_ffn2": tot["norm_ffn2"],
        "norm_ple": tot["norm_ple"], "norm_final": tot["norm_final"]}
    swapped = ("ssm_b_re", "ssm_b_im")

    def pre(k, a):
        return jnp.swapaxes(a, -1, -2) if k in swapped else a

    def update(group, name_):
        items = [(_as2d(pre(k, small_grads[k].reshape(weights[k].shape))), _as2d(pre(k, weights[k])),
                  _as2d(pre(k, moments_m[k])), _as2d(pre(k, moments_v[k]))) for k in group]
        for k, it, (dl, nm, nv) in zip(group, items, adam_small(items, name_)):
            shp = pre(k, weights[k]).shape
            out_g[k], out_d[k], out_m[k], out_v[k] = [pre(k, a.reshape(shp)) for a in (it[0], dl, nm, nv)]

    update([k for k in small_grads if k != "norm_ffn1"], "adam_replicated")
    update(["norm_ffn1"], "adam_norm_ffn1")

    return (loss, grad_x[None], *[out_g[k] for k in names], *[out_d[k] for k in names],
            *[out_m[k] for k in names], *[out_v[k] for k in names])
```

```python
import math

import jax
import jax.numpy as jnp
from jax import lax
from jax.experimental import pallas as pl
from jax.experimental.pallas import tpu as pltpu

F32 = jnp.float32
BF16 = jnp.bfloat16
MESH_DT = pl.DeviceIdType.MESH

N_DEV = 8
N_CHIP = 4
LANE = 128
SUBLANE = 8
VMEM_LIMIT = 60 * 1024 * 1024

EPS = 1e-6
SSM_GROUP = 16
SSM_STATE = 64
GROUPS_PER_BLOCK = LANE // SSM_GROUP
STATE_BLOCK = GROUPS_PER_BLOCK * SSM_STATE
GMLP_HEAD = 128
CHUNK = 128

ADAM_LR = 0.001
ADAM_B1 = 0.9
ADAM_B2 = 0.999
ADAM_EPS = 1e-08
ADAM_WD = 0.01
ADAM_STEP = 10

GELU_K = math.sqrt(2.0 / math.pi)
GELU_C = 0.044715


def _cparams():
    return pltpu.CompilerParams(vmem_limit_bytes=VMEM_LIMIT)


def _tile(n, pref):
    if n <= pref:
        return n
    t = (pref // LANE) * LANE
    while t > 0:
        if n % t == 0:
            return t
        t -= LANE
    return n


def _row_tile(n, pref):
    if n <= pref:
        return n
    t = (pref // SUBLANE) * SUBLANE
    while t > 0:
        if n % t == 0:
            return t
        t -= SUBLANE
    return n


def _gelu(x):
    t = jnp.tanh(GELU_K * (x + GELU_C * x * x * x))
    return 0.5 * x * (1.0 + t)


def _gelu_grad(x):
    t = jnp.tanh(GELU_K * (x + GELU_C * x * x * x))
    return 0.5 * (1.0 + t) + 0.5 * x * (1.0 - t * t) * (GELU_K * (1.0 + 3.0 * GELU_C * x * x))


def _sigmoid(x):
    return 0.5 * jnp.tanh(0.5 * x) + 0.5


_DN = {
    "nn": (((1,), (0,)), ((), ())),
    "nt": (((1,), (1,)), ((), ())),
    "tn": (((0,), (0,)), ((), ())),
}


def _dot(a, b, mode="nn"):
    return lax.dot_general(a, b, _DN[mode], preferred_element_type=F32)


class CommTask:
    def __init__(self, inputs, out_shape, n_sems, start, late, finish, in_place=False):
        self.inputs, self.out_shape, self.n_sems = list(inputs), list(out_shape), n_sems
        self.start, self.late, self.finish = start, late, finish
        self.in_place = in_place


def _task_aliases(tasks, first_in, first_out):
    aliases = {}
    for t in tasks:
        if t.in_place:
            aliases.update({first_in + i: first_out + i for i in range(len(t.inputs))})
        first_in += len(t.inputs)
        first_out += len(t.out_shape)
    return aliases


def _call(body, *, name, grid, in_specs, out_specs, out_shape, args, scratch_shapes=(), tasks=()):
    in_specs, out_specs, out_shape = list(in_specs), list(out_specs), list(out_shape)
    scratch_shapes = list(scratch_shapes)
    if not tasks:
        return pl.pallas_call(
            body, name=name, grid=grid, in_specs=in_specs, out_specs=out_specs, out_shape=out_shape,
            scratch_shapes=scratch_shapes, compiler_params=_cparams())(*args)
    n_in, n_out, n_scr = len(in_specs), len(out_specs), len(scratch_shapes)
    t_in = [len(t.inputs) for t in tasks]
    t_out = [len(t.out_shape) for t in tasks]
    late_step = grid[0] - max(1, grid[0] // 4)
    has_late = grid[0] >= 2

    def carried(*refs):
        pos = n_in
        task_ins = []
        for k in t_in:
            task_ins.append(refs[pos:pos + k])
            pos += k
        outs = refs[pos:pos + n_out]
        pos += n_out
        task_outs = []
        for k in t_out:
            task_outs.append(refs[pos:pos + k])
            pos += k
        scratch = refs[pos:pos + n_scr]
        pos += n_scr
        sems = [refs[pos + 3 * i:pos + 3 * i + 3] for i in range(len(tasks))]
        ids = [pl.program_id(d) for d in range(len(grid))]
        rest_zero = True
        for d in range(1, len(grid)):
            rest_zero = jnp.logical_and(rest_zero, ids[d] == 0)
        first = jnp.logical_and(ids[0] == 0, rest_zero)
        last = ids[0] == grid[0] - 1
        for d in range(1, len(grid)):
            last = jnp.logical_and(last, ids[d] == grid[d] - 1)

        @pl.when(first)
        def _():
            for t, ti, to, s in zip(tasks, task_ins, task_outs, sems):
                t.start(ti, to, *s)

        if has_late:
            @pl.when(jnp.logical_and(ids[0] == late_step, rest_zero))
            def _():
                for t, ti, to, s in zip(tasks, task_ins, task_outs, sems):
                    t.late(ti, to, *s)

        body(*refs[:n_in], *outs, *scratch)

        @pl.when(last)
        def _():
            for t, ti, to, s in zip(tasks, task_ins, task_outs, sems):
                if not has_late:
                    t.late(ti, to, *s)
                t.finish(ti, to, *s)

    any_spec = pl.BlockSpec(memory_space=pl.ANY)
    sem_shapes = [pltpu.SemaphoreType.DMA((n,)) for t in tasks for n in t.n_sems]
    res = pl.pallas_call(
        carried, name=name, grid=grid,
        in_specs=in_specs + [any_spec] * sum(t_in), out_specs=out_specs + [any_spec] * sum(t_out),
        out_shape=out_shape + [s for t in tasks for s in t.out_shape],
        input_output_aliases=_task_aliases(tasks, n_in, n_out),
        scratch_shapes=scratch_shapes + sem_shapes, compiler_params=_cparams(),
    )(*args, *[a for t in tasks for a in t.inputs])
    res = list(res)
    task_res, pos = [], n_out
    for k in t_out:
        task_res.append(res[pos:pos + k])
        pos += k
    return res[:n_out], task_res


def _mm_dims(a, b, mode):
    if mode == "nn":
        (m, k), (k2, n) = a.shape, b.shape
    elif mode == "nt":
        (m, k), (n, k2) = a.shape, b.shape
    else:
        (k, m), (k2, n) = a.shape, b.shape
    assert k == k2, (a.shape, b.shape, mode)
    return m, n, k


def _mm_specs(mode, tm, tn, tk):
    if mode == "tn":
        a_spec = pl.BlockSpec((tk, tm), lambda i, j, k: (k, i))
    else:
        a_spec = pl.BlockSpec((tm, tk), lambda i, j, k: (i, k))
    if mode == "nt":
        b_spec = pl.BlockSpec((tn, tk), lambda i, j, k: (j, k))
    else:
        b_spec = pl.BlockSpec((tk, tn), lambda i, j, k: (k, j))
    return a_spec, b_spec


def _accumulate(acc, nk, partial, emit):
    if nk == 1:
        emit(partial)
        return
    kk = pl.program_id(2)

    @pl.when(kk == 0)
    def _():
        acc[...] = partial

    @pl.when(kk > 0)
    def _():
        acc[...] += partial

    @pl.when(kk == nk - 1)
    def _():
        emit(acc[...])


def matmul(a, b, mode, name, out_dtype=F32, res=None, scale=1.0, tm=1024, tn=1024, tk=2048, tasks=()):
    m, n, k = _mm_dims(a, b, mode)
    if mode == "tn":
        tk = max(tk, 4096)
    tm, tn, tk = _tile(m, tm), _tile(n, tn), _tile(k, tk)
    nk = k // tk
    a_spec, b_spec = _mm_specs(mode, tm, tn, tk)
    o_spec = pl.BlockSpec((tm, tn), lambda i, j, k: (i, j))
    has_res = res is not None

    def body(*refs):
        if has_res:
            a_ref, b_ref, r_ref, o_ref, acc = refs
        else:
            a_ref, b_ref, o_ref, acc = refs

        def emit(v):
            if scale != 1.0:
                v = v * scale
            if has_res:
                v = r_ref[...] + v
            o_ref[...] = v.astype(out_dtype)

        _accumulate(acc, nk, _dot(a_ref[...], b_ref[...], mode), emit)

    out = _call(
        body, name=name, grid=(m // tm, n // tn, nk),
        in_specs=[a_spec, b_spec] + ([o_spec] if has_res else []), out_specs=[o_spec],
        out_shape=[jax.ShapeDtypeStruct((m, n), out_dtype)], args=(a, b) + ((res,) if has_res else ()),
        scratch_shapes=[pltpu.VMEM((tm, tn) if nk > 1 else (SUBLANE, LANE), F32)], tasks=tasks)
    return (out[0][0], out[1]) if tasks else out[0]


def ffn_up(xn, wu, gate, name, tm=1024, tn=1024, tk=2048, tasks=()):
    m, n, k = _mm_dims(xn, wu, "nt")
    tm, tn, tk = _tile(m, tm), _tile(n, tn), _tile(k, tk)
    nk = k // tk
    a_spec, b_spec = _mm_specs("nt", tm, tn, tk)
    o_spec = pl.BlockSpec((tm, tn), lambda i, j, k: (i, j))

    def body(a_ref, u_ref, gate_ref, gate_b_ref, up_b_ref, act_ref, acc):
        def emit(u):
            g = gate_ref[...]
            gate_b_ref[...] = g.astype(BF16)
            up_b_ref[...] = u.astype(BF16)
            act_ref[...] = (g * _sigmoid(g) * u).astype(BF16)

        _accumulate(acc, nk, _dot(a_ref[...], u_ref[...], "nt"), emit)

    out = _call(
        body, name=name, grid=(m // tm, n // tn, nk), in_specs=[a_spec, b_spec, o_spec],
        out_specs=[o_spec, o_spec, o_spec],
        out_shape=[jax.ShapeDtypeStruct((m, n), BF16), jax.ShapeDtypeStruct((m, n), BF16),
                   jax.ShapeDtypeStruct((m, n), BF16)],
        args=(xn, wu, gate), scratch_shapes=[pltpu.VMEM((tm, tn) if nk > 1 else (SUBLANE, LANE), F32)], tasks=tasks)
    return (tuple(out[0]), out[1]) if tasks else tuple(out)


def ffn_bwd_act(dh, wd, gate, up, name, tm=1024, tn=1024, tk=2048, tasks=()):
    m, n, k = _mm_dims(dh, wd, "nt")
    tm, tn, tk = _tile(m, tm), _tile(n, tn), _tile(k, tk)
    nk = k // tk
    a_spec, b_spec = _mm_specs("nt", tm, tn, tk)
    o_spec = pl.BlockSpec((tm, tn), lambda i, j, k: (i, j))

    def body(a_ref, b_ref, gate_ref, up_ref, dg_ref, du_ref, acc):
        def emit(total):
            dact = 0.5 * total
            g = gate_ref[...].astype(F32)
            sg = _sigmoid(g)
            du_ref[...] = (dact * (g * sg)).astype(BF16)
            dg_ref[...] = (dact * up_ref[...].astype(F32) * (sg * (1.0 + g * (1.0 - sg)))).astype(BF16)

        _accumulate(acc, nk, _dot(a_ref[...], b_ref[...], "nt"), emit)

    out = _call(
        body, name=name, grid=(m // tm, n // tn, nk), in_specs=[a_spec, b_spec, o_spec, o_spec],
        out_specs=[o_spec, o_spec],
        out_shape=[jax.ShapeDtypeStruct((m, n), BF16), jax.ShapeDtypeStruct((m, n), BF16)],
        args=(dh, wd, gate, up), scratch_shapes=[pltpu.VMEM((tm, tn) if nk > 1 else (SUBLANE, LANE), F32)],
        tasks=tasks)
    return (tuple(out[0]), out[1]) if tasks else tuple(out)


def _rows(t, d, tr):
    return pl.BlockSpec((tr, d), lambda i: (i, 0))


def _vec(d):
    return pl.BlockSpec((1, d), lambda i: (0, 0))


def rmsnorm_fwd(x, g, name, tr=512):
    t, d = x.shape
    tr = _row_tile(t, tr)

    def body(x_ref, g_ref, o_ref):
        xf = x_ref[...]
        r = lax.rsqrt(jnp.mean(xf * xf, axis=-1, keepdims=True) + EPS)
        o_ref[...] = (xf * r * g_ref[...]).astype(BF16)

    return pl.pallas_call(
        body, name=name, grid=(t // tr,), in_specs=[_rows(t, d, tr), _vec(d)], out_specs=_rows(t, d, tr),
        out_shape=jax.ShapeDtypeStruct((t, d), BF16), compiler_params=_cparams(),
    )(x, g)


def _rms_bwd(dxn, xf, g):
    r = lax.rsqrt(jnp.mean(xf * xf, axis=-1, keepdims=True) + EPS)
    xhat = xf * r
    dg = jnp.sum(dxn * xhat, axis=0, keepdims=True)
    dxh = dxn * g
    dx = r * (dxh - xhat * jnp.mean(dxh * xhat, axis=-1, keepdims=True))
    return dx, dg


def rmsnorm_bwd(dxn, x, g, dres, name, tr=256):
    t, d = x.shape
    tr = _row_tile(t, tr)

    def body(dxn_ref, x_ref, g_ref, dres_ref, o_ref, ob_ref, dg_ref):
        dx, dg = _rms_bwd(dxn_ref[...], x_ref[...], g_ref[...])
        out = dres_ref[...] + dx
        o_ref[...] = out
        ob_ref[...] = out.astype(BF16)

        @pl.when(pl.program_id(0) == 0)
        def _():
            dg_ref[...] = jnp.zeros_like(dg_ref)

        dg_ref[...] += dg

    return pl.pallas_call(
        body, name=name, grid=(t // tr,),
        in_specs=[_rows(t, d, tr), _rows(t, d, tr), _vec(d), _rows(t, d, tr)],
        out_specs=[_rows(t, d, tr), _rows(t, d, tr), _vec(d)],
        out_shape=[jax.ShapeDtypeStruct((t, d), F32), jax.ShapeDtypeStruct((t, d), BF16),
                   jax.ShapeDtypeStruct((1, d), F32)],
        compiler_params=_cparams(),
    )(dxn, x, g, dres)


def final_loss(h, target, g, name, tr=256):
    t, d = h.shape
    tr = _row_tile(t, tr)

    def body(h_ref, t_ref, g_ref, dh_ref, loss_ref, dg_ref):
        xf = h_ref[...]
        gg = g_ref[...]
        r = lax.rsqrt(jnp.mean(xf * xf, axis=-1, keepdims=True) + EPS)
        xhat = xf * r
        e = xhat * gg - t_ref[...]
        part = jnp.sum(jnp.sum(e * e, axis=1, keepdims=True), axis=0, keepdims=True) * (0.5 / d)
        dout = e * (1.0 / d)
        dg = jnp.sum(dout * xhat, axis=0, keepdims=True)
        dxh = dout * gg
        dh_ref[...] = r * (dxh - xhat * jnp.mean(dxh * xhat, axis=-1, keepdims=True))

        @pl.when(pl.program_id(0) == 0)
        def _():
            dg_ref[...] = jnp.zeros_like(dg_ref)
            loss_ref[...] = jnp.zeros_like(loss_ref)

        dg_ref[...] += dg
        loss_ref[...] += jnp.broadcast_to(part, loss_ref.shape)

    return pl.pallas_call(
        body, name=name, grid=(t // tr,),
        in_specs=[_rows(t, d, tr), _rows(t, d, tr), _vec(d)],
        out_specs=[_rows(t, d, tr), pl.BlockSpec((SUBLANE, LANE), lambda i: (0, 0)), _vec(d)],
        out_shape=[jax.ShapeDtypeStruct((t, d), F32), jax.ShapeDtypeStruct((SUBLANE, LANE), F32),
                   jax.ShapeDtypeStruct((1, d), F32)],
        compiler_params=_cparams(),
    )(h, target, g)


def ple_fwd(h, glin, pp, name, tr=512):
    t, d = h.shape
    tr = _row_tile(t, tr)

    def body(h_ref, gl_ref, pp_ref, o_ref):
        o_ref[...] = h_ref[...] + _sigmoid(gl_ref[...]) * pp_ref[...]

    sp = _rows(t, d, tr)
    return pl.pallas_call(
        body, name=name, grid=(t // tr,), in_specs=[sp, sp, sp], out_specs=sp,
        out_shape=jax.ShapeDtypeStruct((t, d), F32), compiler_params=_cparams(),
    )(h, glin, pp)


def ple_bwd(dh, glin, pp, name, tr=512):
    t, d = dh.shape
    tr = _row_tile(t, tr)

    def body(dh_ref, gl_ref, pp_ref, dpp_ref, dgl_ref):
        gate = _sigmoid(gl_ref[...])
        dh_ = dh_ref[...]
        dpp_ref[...] = (dh_ * gate).astype(BF16)
        dgl_ref[...] = (dh_ * pp_ref[...] * gate * (1.0 - gate)).astype(BF16)

    sp = _rows(t, d, tr)
    return pl.pallas_call(
        body, name=name, grid=(t // tr,), in_specs=[sp, sp, sp], out_specs=[sp, sp],
        out_shape=[jax.ShapeDtypeStruct((t, d), BF16), jax.ShapeDtypeStruct((t, d), BF16)],
        compiler_params=_cparams(),
    )(dh, glin, pp)


def mix_out_fwd(y_pre, glin, y_gmlp, g_so, g_go, name, tr=512):
    t, d = y_pre.shape
    tr = _row_tile(t, tr)

    def body(yp_ref, gl_ref, yg_ref, gs_ref, gg_ref, o_ref):
        ys = _gelu(yp_ref[...]) * _sigmoid(gl_ref[...])
        r = lax.rsqrt(jnp.mean(ys * ys, axis=-1, keepdims=True) + EPS)
        o_ref[:, 0:d] = (ys * r * gs_ref[...]).astype(BF16)
        yq = yg_ref[...]
        r2 = lax.rsqrt(jnp.mean(yq * yq, axis=-1, keepdims=True) + EPS)
        o_ref[:, d:2 * d] = (yq * r2 * gg_ref[...]).astype(BF16)

    sp = _rows(t, d, tr)
    return pl.pallas_call(
        body, name=name, grid=(t // tr,), in_specs=[sp, sp, sp, _vec(d), _vec(d)],
        out_specs=_rows(t, 2 * d, tr), out_shape=jax.ShapeDtypeStruct((t, 2 * d), BF16),
        compiler_params=_cparams(),
    )(y_pre, glin, y_gmlp, g_so, g_go)


def mix_out_bwd(dycat, y_pre, glin, y_gmlp, g_so, g_go, name, tr=256):
    t, d = y_pre.shape
    tr = _row_tile(t, tr)

    def body(dy_ref, yp_ref, gl_ref, yg_ref, gs_ref, gg_ref, dyg_ref, dl_ref, dyq_ref, dgs_ref, dgg_ref):
        yg = _gelu(yp_ref[...])
        sg = _sigmoid(gl_ref[...])
        dys, dgs = _rms_bwd(dy_ref[:, 0:d], yg * sg, gs_ref[...])
        dyg_ref[...] = dys * sg
        dl_ref[...] = (dys * yg * sg * (1.0 - sg)).astype(BF16)
        dyq, dgg = _rms_bwd(dy_ref[:, d:2 * d], yg_ref[...], gg_ref[...])
        dyq_ref[...] = dyq

        @pl.when(pl.program_id(0) == 0)
        def _():
            dgs_ref[...] = jnp.zeros_like(dgs_ref)
            dgg_ref[...] = jnp.zeros_like(dgg_ref)

        dgs_ref[...] += dgs
        dgg_ref[...] += dgg

    sp = _rows(t, d, tr)
    return pl.pallas_call(
        body, name=name, grid=(t // tr,),
        in_specs=[_rows(t, 2 * d, tr), sp, sp, sp, _vec(d), _vec(d)],
        out_specs=[sp, sp, sp, _vec(d), _vec(d)],
        out_shape=[jax.ShapeDtypeStruct((t, d), F32), jax.ShapeDtypeStruct((t, d), BF16),
                   jax.ShapeDtypeStruct((t, d), F32), jax.ShapeDtypeStruct((1, d), F32),
                   jax.ShapeDtypeStruct((1, d), F32)],
        compiler_params=_cparams(),
    )(dycat, y_pre, glin, y_gmlp, g_so, g_go)


SCAN_COLS = 512


def _scan_tile(xr, xi, const, cr, ci, reverse):
    for lvl, sh in enumerate((1, 2, 4)):
        ar, ai = const(2 * lvl), const(2 * lvl + 1)
        s = (SUBLANE - sh) if reverse else sh
        rr = pltpu.roll(xr, s, 0)
        ri = pltpu.roll(xi, s, 0)
        xr, xi = xr + ar * rr - ai * ri, xi + ar * ri + ai * rr
    pr, pi_ = const(6), const(7)
    xr, xi = xr + pr * cr - pi_ * ci, xi + pr * ci + pi_ * cr
    return xr, xi


def _bcast_row(x, row):
    return jnp.broadcast_to(x[row:row + 1, :], x.shape)


def s5_fwd(z, bc_r, bc_i, cc_r, cc_i, apw, dvec, name, tc=512, tasks=()):
    t = z.shape[0]
    nblk = bc_r.shape[0]
    d = nblk * LANE
    ns = nblk * STATE_BLOCK
    tc = _row_tile(t, tc)
    ntile = tc // SUBLANE

    def body(z_ref, br_ref, bi_ref, cr_ref, ci_ref, apw_ref, d_ref, y_ref, yg_ref, sr_ref, si_ref, carry):
        @pl.when(pl.program_id(0) == 0)
        def _():
            carry[...] = jnp.zeros_like(carry)

        for j in range(nblk):
            uj = z_ref[:, j * LANE:(j + 1) * LANE]
            ub = uj.astype(BF16)
            for q in range(STATE_BLOCK // SCAN_COLS):
                c0 = j * STATE_BLOCK + q * SCAN_COLS
                cs = pl.ds(c0, SCAN_COLS)
                bs = slice(q * SCAN_COLS, (q + 1) * SCAN_COLS)
                sr_ref[:, cs] = _dot(ub, br_ref[j, :, bs])
                si_ref[:, cs] = _dot(ub, bi_ref[j, :, bs])
                const = lambda k, cs=cs: apw_ref[k, :, cs]

                def tile(k, c, cs=cs, const=const):
                    rows = pl.ds(pl.multiple_of(k * SUBLANE, SUBLANE), SUBLANE)
                    xr, xi = _scan_tile(sr_ref[rows, cs], si_ref[rows, cs], const, c[0], c[1], False)
                    sr_ref[rows, cs] = xr
                    si_ref[rows, cs] = xi
                    return _bcast_row(xr, SUBLANE - 1), _bcast_row(xi, SUBLANE - 1)

                c_r, c_i = lax.fori_loop(0, ntile, tile, (carry[0, :, cs], carry[1, :, cs]))
                carry[0, :, cs] = c_r
                carry[1, :, cs] = c_i
            sb = pl.ds(j * STATE_BLOCK, STATE_BLOCK)
            y = (_dot(sr_ref[:, sb].astype(BF16), cr_ref[j]) - _dot(si_ref[:, sb].astype(BF16), ci_ref[j])
                 + d_ref[:, j * LANE:(j + 1) * LANE] * uj)
            y_ref[:, j * LANE:(j + 1) * LANE] = y
            yg_ref[:, j * LANE:(j + 1) * LANE] = _gelu(y).astype(BF16)

    full3 = lambda shp: pl.BlockSpec(shp, lambda i: (0, 0, 0))
    out = _call(
        body, name=name, grid=(t // tc,),
        in_specs=[pl.BlockSpec((tc, d), lambda i: (i, 0)), full3(bc_r.shape), full3(bc_i.shape),
                  full3(cc_r.shape), full3(cc_i.shape), full3(apw.shape), _vec(d)],
        out_specs=[pl.BlockSpec((tc, d), lambda i: (i, 0)), pl.BlockSpec((tc, d), lambda i: (i, 0)),
                   pl.BlockSpec((tc, ns), lambda i: (i, 0)), pl.BlockSpec((tc, ns), lambda i: (i, 0))],
        out_shape=[jax.ShapeDtypeStruct((t, d), F32), jax.ShapeDtypeStruct((t, d), BF16),
                   jax.ShapeDtypeStruct((t, ns), F32), jax.ShapeDtypeStruct((t, ns), F32)],
        args=(z, bc_r, bc_i, cc_r, cc_i, apw, dvec), scratch_shapes=[pltpu.VMEM((2, SUBLANE, ns), F32)], tasks=tasks)
    return (tuple(out[0]), out[1]) if tasks else tuple(out)


def s5_bwd(dyg, y_pre, z, sr, si, bc_r, bc_i, cc_r, cc_i, apw_rev, dvec, name, tc=256, tasks=()):
    t = z.shape[0]
    nblk = bc_r.shape[0]
    d = nblk * LANE
    ns = nblk * STATE_BLOCK
    tc = _row_tile(t, tc)
    ntile = tc // SUBLANE
    nchunk = t // tc
    tiles_per_chunk = tc // SUBLANE

    def body(dyg_ref, yp_ref, z_ref, sr_ref, si_ref, pr_ref, pi_ref, br_ref, bi_ref, cr_ref, ci_ref, apw_ref,
             d_ref, du_ref, gd_ref, gcr_ref, gci_ref, gbr_ref, gbi_ref, gar_ref, gai_ref, lr_ref, li_ref, carry):
        step = pl.program_id(0)

        @pl.when(step == 0)
        def _():
            carry[...] = jnp.zeros_like(carry)
            for ref in (gd_ref, gcr_ref, gci_ref, gbr_ref, gbi_ref, gar_ref, gai_ref):
                ref[...] = jnp.zeros_like(ref)

        first_chunk = (step == nchunk - 1).astype(F32)
        keep_prev = 1.0 - first_chunk
        row0 = lax.broadcasted_iota(jnp.int32, (SUBLANE, SCAN_COLS), 0) == 0

        for j in range(nblk):
            lanes = slice(j * LANE, (j + 1) * LANE)
            uj = z_ref[:, lanes]
            ub = uj.astype(BF16)
            gy = dyg_ref[:, lanes] * _gelu_grad(yp_ref[:, lanes])
            gyb = gy.astype(BF16)
            gd_ref[:, lanes] += jnp.sum(gy * uj, axis=0, keepdims=True)
            for q in range(STATE_BLOCK // SCAN_COLS):
                c0 = j * STATE_BLOCK + q * SCAN_COLS
                cs = pl.ds(c0, SCAN_COLS)
                bs = slice(q * SCAN_COLS, (q + 1) * SCAN_COLS)
                lr_ref[:, cs] = _dot(gyb, cr_ref[j, bs, :], "nt")
                li_ref[:, cs] = -_dot(gyb, ci_ref[j, bs, :], "nt")
                const = lambda k, cs=cs: apw_ref[k, :, cs]

                def one_tile(rows, prev_r, prev_i, c, cs=cs, const=const):
                    cr_, ci_, gar, gai = c
                    xr, xi = _scan_tile(lr_ref[rows, cs], li_ref[rows, cs], const, cr_, ci_, True)
                    lr_ref[rows, cs] = xr
                    li_ref[rows, cs] = xi
                    spr = jnp.where(row0, prev_r, pltpu.roll(sr_ref[rows, cs], 1, 0))
                    spi = jnp.where(row0, prev_i, pltpu.roll(si_ref[rows, cs], 1, 0))
                    gar = gar + xr * spr + xi * spi
                    gai = gai + xi * spr - xr * spi
                    return _bcast_row(xr, 0), _bcast_row(xi, 0), gar, gai

                def tile(k, c, cs=cs, one_tile=one_tile):
                    kk = ntile - 1 - k
                    rows = pl.ds(pl.multiple_of(kk * SUBLANE, SUBLANE), SUBLANE)
                    prow = pl.ds(pl.multiple_of((kk - 1) * SUBLANE, SUBLANE), SUBLANE)
                    prev_r = _bcast_row(sr_ref[prow, cs], SUBLANE - 1)
                    prev_i = _bcast_row(si_ref[prow, cs], SUBLANE - 1)
                    return one_tile(rows, prev_r, prev_i, c)

                zero = jnp.zeros((SUBLANE, SCAN_COLS), F32)
                c = lax.fori_loop(0, ntile - 1, tile, (carry[0, :, cs], carry[1, :, cs], zero, zero))
                prev_r = _bcast_row(pr_ref[:, cs], SUBLANE - 1) * keep_prev
                prev_i = _bcast_row(pi_ref[:, cs], SUBLANE - 1) * keep_prev
                c_r, c_i, gar, gai = one_tile(pl.ds(0, SUBLANE), prev_r, prev_i, c)
                carry[0, :, cs] = c_r
                carry[1, :, cs] = c_i
                gar_ref[:, cs] += gar
                gai_ref[:, cs] += gai
            sb = pl.ds(j * STATE_BLOCK, STATE_BLOCK)
            lrb = lr_ref[:, sb].astype(BF16)
            lib = li_ref[:, sb].astype(BF16)
            gcr_ref[j] += _dot(gyb, sr_ref[:, sb].astype(BF16), "tn")
            gci_ref[j] -= _dot(gyb, si_ref[:, sb].astype(BF16), "tn")
            gbr_ref[j] += _dot(ub, lrb, "tn")
            gbi_ref[j] += _dot(ub, lib, "tn")
            du = _dot(lrb, br_ref[j], "nt") + _dot(lib, bi_ref[j], "nt") + gy * d_ref[:, lanes]
            du_ref[:, lanes] = du.astype(BF16)

    rev = lambda i: (nchunk - 1 - i, 0)
    prev = lambda i: (jnp.maximum((nchunk - 1 - i) * tiles_per_chunk - 1, 0), 0)
    full3 = lambda shp: pl.BlockSpec(shp, lambda i: (0, 0, 0))
    acc3 = pl.BlockSpec((nblk, LANE, STATE_BLOCK), lambda i: (0, 0, 0))
    acc_rows = pl.BlockSpec((SUBLANE, ns), lambda i: (0, 0))
    out = _call(
        body, name=name, grid=(nchunk,),
        in_specs=[pl.BlockSpec((tc, d), rev), pl.BlockSpec((tc, d), rev), pl.BlockSpec((tc, d), rev),
                  pl.BlockSpec((tc, ns), rev), pl.BlockSpec((tc, ns), rev),
                  pl.BlockSpec((SUBLANE, ns), prev), pl.BlockSpec((SUBLANE, ns), prev),
                  full3(bc_r.shape), full3(bc_i.shape), full3(cc_r.shape), full3(cc_i.shape), full3(apw_rev.shape),
                  _vec(d)],
        out_specs=[pl.BlockSpec((tc, d), rev), _vec(d), acc3, acc3, acc3, acc3, acc_rows, acc_rows],
        out_shape=[jax.ShapeDtypeStruct((t, d), BF16), jax.ShapeDtypeStruct((1, d), F32)]
        + [jax.ShapeDtypeStruct((nblk, LANE, STATE_BLOCK), F32)] * 4
        + [jax.ShapeDtypeStruct((SUBLANE, ns), F32)] * 2,
        args=(dyg, y_pre, z, sr, si, sr, si, bc_r, bc_i, cc_r, cc_i, apw_rev, dvec),
        scratch_shapes=[pltpu.VMEM((tc, ns), F32), pltpu.VMEM((tc, ns), F32), pltpu.VMEM((2, SUBLANE, ns), F32)],
        tasks=tasks)
    return (tuple(out[0]), out[1]) if tasks else tuple(out)


def _cmul(a, b):
    return a[0] * b[0] - a[1] * b[1], a[0] * b[1] + a[1] * b[0]


def _scan_constants(abar_r, abar_i, reverse):
    ar = abar_r.reshape(1, -1)
    ai = abar_i.reshape(1, -1)
    if reverse:
        ai = -ai
    pw = [(ar, ai)]
    for _ in range(SUBLANE - 1):
        pw.append(_cmul(pw[-1], (ar, ai)))
    rows = lax.broadcasted_iota(jnp.int32, (SUBLANE, 1), 0)
    out = []
    for sh in (1, 2, 4):
        keep = (rows <= SUBLANE - 1 - sh) if reverse else (rows >= sh)
        for part in pw[sh - 1]:
            out.append(jnp.where(keep, part, 0.0))
    for comp in (0, 1):
        stack = jnp.concatenate([pw[k][comp] for k in range(SUBLANE)], axis=0)
        out.append(stack[::-1] if reverse else stack)
    return jnp.stack(out, axis=0).astype(F32)


def _ssm_discretize(log_dt, a_re, a_im, b_re, b_im):
    dt = jnp.exp(log_dt)[:, None]
    lr = jnp.minimum(a_re, -1e-4)
    li = a_im
    mag = jnp.exp(lr * dt)
    ang = li * dt
    abar_r = mag * jnp.cos(ang)
    abar_i = mag * jnp.sin(ang)
    den = lr * lr + li * li
    xr = abar_r - 1.0
    xi = abar_i
    zr = (xr * lr + xi * li) / den
    zi = (xi * lr - xr * li) / den
    bbar_r = zr[..., None] * b_re - zi[..., None] * b_im
    bbar_i = zr[..., None] * b_im + zi[..., None] * b_re
    return abar_r, abar_i, bbar_r, bbar_i


def _block_diag(w):
    g, a, b = w.shape
    nb = g // GROUPS_PER_BLOCK
    eye = jnp.eye(GROUPS_PER_BLOCK, dtype=w.dtype)
    w5 = w.reshape(nb, GROUPS_PER_BLOCK, a, b)
    out = w5[:, :, :, None, :] * eye[None, :, None, :, None]
    return out.reshape(nb, GROUPS_PER_BLOCK * a, GROUPS_PER_BLOCK * b)


def _block_diag_extract(m, a, b):
    nb = m.shape[0]
    eye = jnp.eye(GROUPS_PER_BLOCK, dtype=m.dtype)
    m5 = m.reshape(nb, GROUPS_PER_BLOCK, a, GROUPS_PER_BLOCK, b)
    out = jnp.sum(m5 * eye[None, :, None, :, None], axis=3)
    return out.reshape(nb * GROUPS_PER_BLOCK, a, b)


def _layer_norm(gv, nv):
    mu = jnp.mean(gv, axis=-1, keepdims=True)
    xc = gv - mu
    r = lax.rsqrt(jnp.mean(xc * xc, axis=-1, keepdims=True) + EPS)
    xhat = xc * r
    return xhat * nv, xhat, r


def gmlp_fwd(z, norm_v, wm, bs, name, tr=256):
    t = z.shape[0]
    nh = wm.shape[0]
    d = nh * GMLP_HEAD
    col0 = (z.shape[1] - 2 * d) // d
    tr = _row_tile(t, tr)

    def body(zu_ref, zv_ref, nv_ref, wm_ref, bs_ref, o_ref):
        v, _, _ = _layer_norm(_gelu(zv_ref[...]), nv_ref[...])
        vb = v.astype(BF16)
        u = _gelu(zu_ref[...])
        for c in range(tr // CHUNK):
            rows = slice(c * CHUNK, (c + 1) * CHUNK)
            for h in range(nh):
                cols = slice(h * GMLP_HEAD, (h + 1) * GMLP_HEAD)
                s = _dot(wm_ref[h], vb[rows, cols]) + bs_ref[h]
                o_ref[rows, cols] = u[rows, cols] * s

    return pl.pallas_call(
        body, name=name, grid=(t // tr,),
        in_specs=[pl.BlockSpec((tr, d), lambda i: (i, col0)), pl.BlockSpec((tr, d), lambda i: (i, col0 + 1)),
                  _vec(d), pl.BlockSpec(wm.shape, lambda i: (0, 0, 0)), pl.BlockSpec(bs.shape, lambda i: (0, 0, 0))],
        out_specs=pl.BlockSpec((tr, d), lambda i: (i, 0)),
        out_shape=jax.ShapeDtypeStruct((t, d), F32), compiler_params=_cparams(),
    )(z, z, norm_v, wm, bs)


def gmlp_bwd(dy, z, norm_v, wm, wmt, bs, name, tr=256):
    t = z.shape[0]
    nh = wm.shape[0]
    d = nh * GMLP_HEAD
    col0 = (z.shape[1] - 2 * d) // d
    tr = _row_tile(t, tr)

    def body(dy_ref, zu_ref, zv_ref, nv_ref, wm_ref, wmt_ref, bs_ref, dzu_ref, dzv_ref, dnv_ref, dwm_ref, dbs_ref,
             dv_ref):
        @pl.when(pl.program_id(0) == 0)
        def _():
            dnv_ref[...] = jnp.zeros_like(dnv_ref)
            dwm_ref[...] = jnp.zeros_like(dwm_ref)
            dbs_ref[...] = jnp.zeros_like(dbs_ref)

        zv = zv_ref[...]
        nv = nv_ref[...]
        v, xhat, r = _layer_norm(_gelu(zv), nv)
        vb = v.astype(BF16)
        zu = zu_ref[...]
        u = _gelu(zu)
        dy_ = dy_ref[...]
        for c in range(tr // CHUNK):
            rows = slice(c * CHUNK, (c + 1) * CHUNK)
            for h in range(nh):
                cols = slice(h * GMLP_HEAD, (h + 1) * GMLP_HEAD)
                vh = vb[rows, cols]
                s = _dot(wm_ref[h], vh) + bs_ref[h]
                dyh = dy_[rows, cols]
                dzu_ref[rows, cols] = (dyh * s * _gelu_grad(zu[rows, cols])).astype(BF16)
                ds = dyh * u[rows, cols]
                dsb = ds.astype(BF16)
                dbs_ref[h] += jnp.sum(ds, axis=1, keepdims=True)
                dwm_ref[h] += _dot(dsb, vh, "nt")
                dv_ref[rows, cols] = _dot(wmt_ref[h], dsb)
        dv = dv_ref[...]
        dnv_ref[...] += jnp.sum(dv * xhat, axis=0, keepdims=True)
        dxh = dv * nv
        dgv = r * (dxh - jnp.mean(dxh, axis=-1, keepdims=True) - xhat * jnp.mean(dxh * xhat, axis=-1, keepdims=True))
        dzv_ref[...] = (dgv * _gelu_grad(zv)).astype(BF16)

    full3 = lambda shp: pl.BlockSpec(shp, lambda i: (0, 0, 0))
    rows_d = pl.BlockSpec((tr, d), lambda i: (i, 0))
    return pl.pallas_call(
        body, name=name, grid=(t // tr,),
        in_specs=[rows_d, pl.BlockSpec((tr, d), lambda i: (i, col0)), pl.BlockSpec((tr, d), lambda i: (i, col0 + 1)),
                  _vec(d), full3(wm.shape), full3(wmt.shape), full3(bs.shape)],
        out_specs=[rows_d, rows_d, _vec(d), full3((nh, CHUNK, CHUNK)), full3((nh, CHUNK, 1))],
        out_shape=[jax.ShapeDtypeStruct((t, d), BF16), jax.ShapeDtypeStruct((t, d), BF16),
                   jax.ShapeDtypeStruct((1, d), F32), jax.ShapeDtypeStruct((nh, CHUNK, CHUNK), F32),
                   jax.ShapeDtypeStruct((nh, CHUNK, 1), F32)],
        scratch_shapes=[pltpu.VMEM((tr, d), F32)], compiler_params=_cparams(),
    )(dy, z, z, norm_v, wm, wmt, bs)


def _block(ref, axis, size, k):
    start = pl.multiple_of(k * size, size)
    if axis == 0:
        return ref.at[pl.ds(start, size), :]
    return ref.at[:, pl.ds(start, size)]


def _place():
    x, y, c = lax.axis_index("x"), lax.axis_index("y"), lax.axis_index("c")
    chips = [(1 - x, y), (x, 1 - y), (1 - x, 1 - y)]
    return x, y, c, chips


def _dev(x, y, c):
    return 4 * x + 2 * y + c


def gather_task(shards, axes):
    n = len(shards)
    sizes = [s.shape[ax] for s, ax in zip(shards, axes)]
    out_shape = [
        jax.ShapeDtypeStruct((s.shape[0] * N_DEV, s.shape[1]) if ax == 0 else (s.shape[0], s.shape[1] * N_DEV), s.dtype)
        for s, ax in zip(shards, axes)
    ]

    def copy(ins, outs, send_sems, recv_sems, t, k, block, to, from_input=False):
        dst = _block(outs[t], axes[t], sizes[t], _dev(*block))
        return pltpu.make_async_remote_copy(
            src_ref=ins[t] if from_input else dst, dst_ref=dst,
            send_sem=send_sems.at[t * 7 + k], recv_sem=recv_sems.at[t * 7 + k],
            device_id=to, device_id_type=MESH_DT)

    def local(ins, outs, local_sems, t, me):
        return pltpu.make_async_copy(ins[t], _block(outs[t], axes[t], sizes[t], _dev(*me)), local_sems.at[t])

    def start(ins, outs, send_sems, recv_sems, local_sems):
        x, y, c, chips = _place()
        me, sibling = (x, y, c), (x, y, 1 - c)
        for t in range(n):
            local(ins, outs, local_sems, t, me).start()
        for t in range(n):
            copy(ins, outs, send_sems, recv_sems, t, 0, me, sibling, True).start()
            for j, chip in enumerate(chips):
                copy(ins, outs, send_sems, recv_sems, t, 1 + j, me, (*chip, c), True).start()

    def late(ins, outs, send_sems, recv_sems, local_sems):
        x, y, c, chips = _place()
        me, sibling = (x, y, c), (x, y, 1 - c)
        for t in range(n):
            for j, chip in enumerate(chips):
                copy(ins, outs, send_sems, recv_sems, t, 1 + j, (*chip, c), me).wait_recv()
                copy(ins, outs, send_sems, recv_sems, t, 4 + j, (*chip, c), sibling).start()

    def finish(ins, outs, send_sems, recv_sems, local_sems):
        x, y, c, chips = _place()
        me, sibling = (x, y, c), (x, y, 1 - c)
        for t in range(n):
            copy(ins, outs, send_sems, recv_sems, t, 0, sibling, me).wait_recv()
            for j, chip in enumerate(chips):
                copy(ins, outs, send_sems, recv_sems, t, 4 + j, (*chip, 1 - c), me).wait_recv()
        for t in range(n):
            copy(ins, outs, send_sems, recv_sems, t, 0, me, sibling, True).wait_send()
            for j, chip in enumerate(chips):
                copy(ins, outs, send_sems, recv_sems, t, 1 + j, me, (*chip, c), True).wait_send()
                copy(ins, outs, send_sems, recv_sems, t, 4 + j, (*chip, c), sibling).wait_send()
            local(ins, outs, local_sems, t, me).wait()

    return CommTask(shards, out_shape, (7 * n, 7 * n, n), start, late, finish)


def _blk3(shape2, axis):
    r, c = shape2
    return (r // N_DEV, c) if axis == 0 else (r, c // N_DEV)


def _no_late(ins, outs, send_sems, recv_sems, local_sems):
    pass


def to_sibling_task(grads, axes):
    n = len(grads)
    blks = [_blk3(g.shape, ax) for g, ax in zip(grads, axes)]
    sizes = [b[ax] for b, ax in zip(blks, axes)]

    def copies(ins, outs, send_sems, recv_sems):
        x, y, c, _ = _place()
        return [pltpu.make_async_remote_copy(
            src_ref=_block(ins[t], axes[t], sizes[t], 2 * i + (1 - c)), dst_ref=outs[t].at[i],
            send_sem=send_sems.at[t * N_CHIP + i], recv_sem=recv_sems.at[t * N_CHIP + i],
            device_id=(x, y, 1 - c), device_id_type=MESH_DT) for t in range(n) for i in range(N_CHIP)]

    def start(ins, outs, send_sems, recv_sems, local_sems):
        for cp in copies(ins, outs, send_sems, recv_sems):
            cp.start()

    def finish(ins, outs, send_sems, recv_sems, local_sems):
        cps = copies(ins, outs, send_sems, recv_sems)
        for cp in cps:
            cp.wait_recv()
        for cp in cps:
            cp.wait_send()

    out_shape = [jax.ShapeDtypeStruct((N_CHIP,) + b, g.dtype) for b, g in zip(blks, grads)]
    return CommTask(grads, out_shape, (N_CHIP * n, N_CHIP * n, 1), start, _no_late, finish)


def across_chips_task(parts):
    n = len(parts)

    def copies(ins, outs, send_sems, recv_sems):
        x, y, c, chips = _place()
        my_chip = 2 * x + y
        return [pltpu.make_async_remote_copy(
            src_ref=ins[t].at[2 * chip[0] + chip[1]], dst_ref=outs[t].at[my_chip],
            send_sem=send_sems.at[t * 3 + j], recv_sem=recv_sems.at[t * 3 + j],
            device_id=(*chip, c), device_id_type=MESH_DT) for t in range(n) for j, chip in enumerate(chips)]

    def mine(ins, outs, local_sems):
        x, y, _, _ = _place()
        my_chip = 2 * x + y
        return [pltpu.make_async_copy(ins[t].at[my_chip], outs[t].at[my_chip], local_sems.at[t]) for t in range(n)]

    def start(ins, outs, send_sems, recv_sems, local_sems):
        for cp in mine(ins, outs, local_sems):
            cp.start()
        for cp in copies(ins, outs, send_sems, recv_sems):
            cp.start()

    def finish(ins, outs, send_sems, recv_sems, local_sems):
        cps = copies(ins, outs, send_sems, recv_sems)
        for cp in cps:
            cp.wait_recv()
        for cp in cps:
            cp.wait_send()
        for cp in mine(ins, outs, local_sems):
            cp.wait()

    out_shape = [jax.ShapeDtypeStruct(p.shape, p.dtype) for p in parts]
    return CommTask(parts, out_shape, (3 * n, 3 * n, n), start, _no_late, finish)


def run_tasks(tasks, name):
    t_in = [len(t.inputs) for t in tasks]
    t_out = [len(t.out_shape) for t in tasks]

    def body(*refs):
        pos, views = 0, []
        for k in t_in:
            views.append([refs[pos:pos + k]])
            pos += k
        for v, k in zip(views, t_out):
            v.append(refs[pos:pos + k])
            pos += k
        for i, v in enumerate(views):
            v.extend(refs[pos + 3 * i:pos + 3 * i + 3])
        for phase in ("start", "late", "finish"):
            for t, v in zip(tasks, views):
                getattr(t, phase)(*v)

    any_spec = pl.BlockSpec(memory_space=pl.ANY)
    res = pl.pallas_call(
        body, name=name, in_specs=[any_spec] * sum(t_in), out_specs=[any_spec] * sum(t_out),
        out_shape=[s for t in tasks for s in t.out_shape], input_output_aliases=_task_aliases(tasks, 0, 0),
        scratch_shapes=[pltpu.SemaphoreType.DMA((k,)) for t in tasks for k in t.n_sems],
    )(*[a for t in tasks for a in t.inputs])
    res, out, pos = list(res), [], 0
    for k in t_out:
        out.append(res[pos:pos + k])
        pos += k
    return out


_HBM_SPEC = pl.BlockSpec(memory_space=pl.ANY)
_SEM_SPEC = pl.BlockSpec(memory_space=pltpu.SEMAPHORE)
_DATAFLOW = pltpu.SideEffectType.DATAFLOW_SIDE_EFFECTING


def _full_shape(s, ax):
    return (s.shape[0] * N_DEV, s.shape[1]) if ax == 0 else (s.shape[0], s.shape[1] * N_DEV)


def _level1_copy(src, landing, axis, size, send_sems, recv_sems, slot, sender, to):
    dst = _block(landing, axis, size, _dev(*sender))
    return pltpu.make_async_remote_copy(src_ref=src, dst_ref=dst, send_sem=send_sems.at[slot],
                                        recv_sem=recv_sems.at[slot], device_id=to, device_id_type=MESH_DT)


def place_own_block(shard, landing, axis, me, name, tr=256):
    r, c = shard.shape
    tr = _row_tile(r, tr)
    nrb = r // tr
    if axis == 0:
        o_map = lambda i, me_ref: (me_ref[0] * nrb + i, 0)
    else:
        o_map = lambda i, me_ref: (i, me_ref[0])

    def body(me_ref, x_ref, land_ref, o_ref):
        o_ref[...] = x_ref[...]

    return pl.pallas_call(
        body, name=name,
        grid_spec=pltpu.PrefetchScalarGridSpec(
            num_scalar_prefetch=1, grid=(nrb,),
            in_specs=[pl.BlockSpec((tr, c), lambda i, me_ref: (i, 0)), pl.BlockSpec(memory_space=pl.ANY)],
            out_specs=pl.BlockSpec((tr, c), o_map)),
        out_shape=jax.ShapeDtypeStruct(landing.shape, landing.dtype), input_output_aliases={2: 0},
        compiler_params=_cparams(),
    )(me, shard, landing)


def gather_start(shards, axes, sizes, groups, name):
    n = len(shards)

    def body(*refs):
        srcs, lands, sems = refs[:n], refs[n:2 * n], refs[4 * n:]
        x, y, c, chips = _place()
        me = (x, y, c)
        targets = [(x, y, 1 - c)] + [(*chip, c) for chip in chips]
        for g, members in enumerate(groups):
            for m, t in enumerate(members):
                for k, to in enumerate(targets):
                    _level1_copy(srcs[t], lands[t], axes[t], sizes[t], sems[2 * g], sems[2 * g + 1], 4 * m + k,
                                 me, to).start()

    landing = [lax.empty(_full_shape(s, ax), s.dtype) for s, ax in zip(shards, axes)]
    out = pl.pallas_call(
        body, name=name,
        out_shape=[jax.ShapeDtypeStruct(b.shape, b.dtype) for b in shards + landing]
        + [pltpu.SemaphoreType.DMA((4 * len(members),)) for members in groups for _ in (0, 1)],
        in_specs=[_HBM_SPEC] * (2 * n), out_specs=[_HBM_SPEC] * (2 * n) + [_SEM_SPEC] * (2 * len(groups)),
        input_output_aliases={i: i for i in range(2 * n)},
        compiler_params=pltpu.CompilerParams(has_side_effects=_DATAFLOW),
    )(*shards, *landing)
    out = list(out)
    sems = out[2 * n:]
    return out[:n], out[n:2 * n], [(sems[2 * g], sems[2 * g + 1]) for g in range(len(groups))]


def gather_wait(shards, landing, axes, sizes, send_sems, recv_sems, after, name):
    n = len(landing)

    def body(*refs):
        srcs, lands = refs[:n], refs[n:2 * n]
        send, recv = refs[2 * n], refs[2 * n + 1]
        x, y, c, chips = _place()
        me = (x, y, c)
        peers = [(x, y, 1 - c)] + [(*chip, c) for chip in chips]
        for t in range(n):
            for k, peer in enumerate(peers):
                _level1_copy(srcs[t], lands[t], axes[t], sizes[t], send, recv, 4 * t + k, me, peer).wait_send()
                _level1_copy(srcs[t], lands[t], axes[t], sizes[t], send, recv, 4 * t + k, peer, me).wait_recv()

    out = pl.pallas_call(
        body, name=name, out_shape=[jax.ShapeDtypeStruct(b.shape, b.dtype) for b in shards + landing],
        in_specs=[_HBM_SPEC] * (2 * n) + [_SEM_SPEC, _SEM_SPEC, pl.BlockSpec(memory_space=pl.ANY)],
        out_specs=[_HBM_SPEC] * (2 * n), input_output_aliases={i: i for i in range(2 * n)},
        compiler_params=pltpu.CompilerParams(has_side_effects=_DATAFLOW),
    )(*shards, *landing, send_sems, recv_sems, after)
    return list(out)[n:]


def forward_task(landing, axes, sizes):
    n = len(landing)

    def forward(lands, send_sems, recv_sems, t, j, chip_core):
        x, y, c, _ = _place()
        blk = _block(lands[t], axes[t], sizes[t], _dev(*chip_core))
        return pltpu.make_async_remote_copy(src_ref=blk, dst_ref=blk, send_sem=send_sems.at[3 * t + j],
                                            recv_sem=recv_sems.at[3 * t + j], device_id=(x, y, 1 - c),
                                            device_id_type=MESH_DT)

    def start(ins, lands, send_sems, recv_sems, local_sems):
        _, _, c, chips = _place()
        for t in range(n):
            for j, chip in enumerate(chips):
                forward(lands, send_sems, recv_sems, t, j, (*chip, c)).start()

    def finish(ins, lands, send_sems, recv_sems, local_sems):
        _, _, c, chips = _place()
        for t in range(n):
            for j, chip in enumerate(chips):
                forward(lands, send_sems, recv_sems, t, j, (*chip, 1 - c)).wait_recv()
        for t in range(n):
            for j, chip in enumerate(chips):
                forward(lands, send_sems, recv_sems, t, j, (*chip, c)).wait_send()

    out_shape = [jax.ShapeDtypeStruct(b.shape, b.dtype) for b in landing]
    return CommTask(landing, out_shape, (3 * n, 3 * n, 1), start, _no_late, finish, in_place=True)


def rs_chip_sum(grad, recv, axis, core, name, tr=512):
    br, bc = _blk3(grad.shape, axis)
    tr = _row_tile(br, tr)
    nrb = br // tr

    if axis == 0:
        g_map = lambda i, r, c_ref: ((2 * i + c_ref[0]) * nrb + r, 0)
    else:
        g_map = lambda i, r, c_ref: (r, 2 * i + c_ref[0])

    def body(c_ref, g_ref, r_ref, o_ref):
        o_ref[...] = (g_ref[...].astype(F32) + r_ref[...].astype(F32)).astype(BF16)

    return pl.pallas_call(
        body, name=name,
        grid_spec=pltpu.PrefetchScalarGridSpec(
            num_scalar_prefetch=1, grid=(N_CHIP, nrb),
            in_specs=[pl.BlockSpec((tr, bc), g_map), pl.BlockSpec((None, tr, bc), lambda i, r, c_ref: (i, r, 0))],
            out_specs=pl.BlockSpec((None, tr, bc), lambda i, r, c_ref: (i, r, 0))),
        out_shape=jax.ShapeDtypeStruct((N_CHIP, br, bc), BF16), compiler_params=_cparams(),
    )(core, grad, recv)


def _adamw(w, g, m, v):
    m = ADAM_B1 * m + (1.0 - ADAM_B1) * g
    v = ADAM_B2 * v + (1.0 - ADAM_B2) * (g * g)
    m_hat = m / (1.0 - ADAM_B1 ** ADAM_STEP)
    v_hat = v / (1.0 - ADAM_B2 ** ADAM_STEP)
    delta = -ADAM_LR * (m_hat / (jnp.sqrt(v_hat) + ADAM_EPS) + ADAM_WD * w)
    return delta, m, v


def _sum_chips(p_ref):
    g = p_ref[0].astype(F32)
    for i in range(1, N_CHIP):
        g = g + p_ref[i].astype(F32)
    return g


def adam_sharded(parts, w, m, v, name, tr=256):
    r, c = w.shape
    assert parts.shape[2] == c
    tr = _row_tile(r, tr)

    def body(p_ref, w_ref, m_ref, v_ref, g_ref, d_ref, nm_ref, nv_ref):
        g = _sum_chips(p_ref)
        delta, nm, nv = _adamw(w_ref[...], g, m_ref[...], v_ref[...])
        g_ref[...] = g
        d_ref[...] = delta
        nm_ref[...] = nm
        nv_ref[...] = nv

    sp = pl.BlockSpec((tr, c), lambda i: (i, 0))
    return pl.pallas_call(
        body, name=name, grid=(r // tr,),
        in_specs=[pl.BlockSpec((N_CHIP, tr, c), lambda i: (0, i, 0)), sp, sp, sp],
        out_specs=[sp, sp, sp, sp], out_shape=[jax.ShapeDtypeStruct((r, c), F32)] * 4,
        compiler_params=_cparams(),
    )(parts, w, m, v)


def adam_small(items, name):
    n = len(items)

    def body(*refs):
        for i in range(n):
            g_ref, w_ref, m_ref, v_ref = refs[4 * i:4 * i + 4]
            d_ref, nm_ref, nv_ref = refs[4 * n + 3 * i:4 * n + 3 * i + 3]
            delta, nm, nv = _adamw(w_ref[...], g_ref[...], m_ref[...], v_ref[...])
            d_ref[...] = delta
            nm_ref[...] = nm
            nv_ref[...] = nv

    out = pl.pallas_call(
        body, name=name, out_shape=[jax.ShapeDtypeStruct(it[1].shape, F32) for it in items for _ in range(3)],
        compiler_params=_cparams(),
    )(*[a for it in items for a in it])
    return [tuple(out[3 * i:3 * i + 3]) for i in range(n)]


def sum_devices(gathered, name, tr=512):
    _, r, c = gathered.shape
    tr = _row_tile(r, tr)

    def body(x_ref, o_ref):
        s = x_ref[0]
        for k in range(1, N_DEV):
            s = s + x_ref[k]
        o_ref[...] = s

    return pl.pallas_call(
        body, name=name, grid=(r // tr,), in_specs=[pl.BlockSpec((N_DEV, tr, c), lambda i: (0, i, 0))],
        out_specs=pl.BlockSpec((tr, c), lambda i: (i, 0)), out_shape=jax.ShapeDtypeStruct((r, c), F32),
        compiler_params=_cparams(),
    )(gathered)


def _pad_to(a, axis, mult):
    size = a.shape[axis]
    pad = (-size) % mult
    if pad == 0:
        return a
    cfg = [(0, 0)] * a.ndim
    cfg[axis] = (0, pad)
    return jnp.pad(a, cfg)


def _as2d(a):
    if a.ndim == 1:
        return a.reshape(1, -1)
    return a.reshape(-1, a.shape[-1])


def kernel(x, p, norm_ffn1, w1_gate, w1_up, w1_down, norm_mix, w_in, ssm_log_dt, ssm_a_re, ssm_a_im, ssm_b_re, ssm_b_im, ssm_c_re, ssm_c_im, ssm_d, ssm_w_glu, gmlp_norm_v, gmlp_w_s, gmlp_b_s, norm_ssm_out, norm_gmlp_out, w_out, norm_ffn2, w2_gate, w2_up, w2_down, norm_ple, w_ple_gate, w_ple_proj, norm_final, loss_target, m_norm_ffn1, m_w1_gate, m_w1_up, m_w1_down, m_norm_mix, m_w_in, m_ssm_log_dt, m_ssm_a_re, m_ssm_a_im, m_ssm_b_re, m_ssm_b_im, m_ssm_c_re, m_ssm_c_im, m_ssm_d, m_ssm_w_glu, m_gmlp_norm_v, m_gmlp_w_s, m_gmlp_b_s, m_norm_ssm_out, m_norm_gmlp_out, m_w_out, m_norm_ffn2, m_w2_gate, m_w2_up, m_w2_down, m_norm_ple, m_w_ple_gate, m_w_ple_proj, m_norm_final, v_norm_ffn1, v_w1_gate, v_w1_up, v_w1_down, v_norm_mix, v_w_in, v_ssm_log_dt, v_ssm_a_re, v_ssm_a_im, v_ssm_b_re, v_ssm_b_im, v_ssm_c_re, v_ssm_c_im, v_ssm_d, v_ssm_w_glu, v_gmlp_norm_v, v_gmlp_w_s, v_gmlp_b_s, v_norm_ssm_out, v_norm_gmlp_out, v_w_out, v_norm_ffn2, v_w2_gate, v_w2_up, v_w2_down, v_norm_ple, v_w_ple_gate, v_w_ple_proj, v_norm_final):
    weights = dict(
        norm_ffn1=norm_ffn1, w1_gate=w1_gate, w1_up=w1_up, w1_down=w1_down, norm_mix=norm_mix, w_in=w_in,
        ssm_log_dt=ssm_log_dt, ssm_a_re=ssm_a_re, ssm_a_im=ssm_a_im, ssm_b_re=ssm_b_re, ssm_b_im=ssm_b_im,
        ssm_c_re=ssm_c_re, ssm_c_im=ssm_c_im, ssm_d=ssm_d, ssm_w_glu=ssm_w_glu, gmlp_norm_v=gmlp_norm_v,
        gmlp_w_s=gmlp_w_s, gmlp_b_s=gmlp_b_s, norm_ssm_out=norm_ssm_out, norm_gmlp_out=norm_gmlp_out, w_out=w_out,
        norm_ffn2=norm_ffn2, w2_gate=w2_gate, w2_up=w2_up, w2_down=w2_down, norm_ple=norm_ple,
        w_ple_gate=w_ple_gate, w_ple_proj=w_ple_proj, norm_final=norm_final)
    moments_m = dict(
        norm_ffn1=m_norm_ffn1, w1_gate=m_w1_gate, w1_up=m_w1_up, w1_down=m_w1_down, norm_mix=m_norm_mix, w_in=m_w_in,
        ssm_log_dt=m_ssm_log_dt, ssm_a_re=m_ssm_a_re, ssm_a_im=m_ssm_a_im, ssm_b_re=m_ssm_b_re, ssm_b_im=m_ssm_b_im,
        ssm_c_re=m_ssm_c_re, ssm_c_im=m_ssm_c_im, ssm_d=m_ssm_d, ssm_w_glu=m_ssm_w_glu, gmlp_norm_v=m_gmlp_norm_v,
        gmlp_w_s=m_gmlp_w_s, gmlp_b_s=m_gmlp_b_s, norm_ssm_out=m_norm_ssm_out, norm_gmlp_out=m_norm_gmlp_out,
        w_out=m_w_out, norm_ffn2=m_norm_ffn2, w2_gate=m_w2_gate, w2_up=m_w2_up, w2_down=m_w2_down,
        norm_ple=m_norm_ple, w_ple_gate=m_w_ple_gate, w_ple_proj=m_w_ple_proj, norm_final=m_norm_final)
    moments_v = dict(
        norm_ffn1=v_norm_ffn1, w1_gate=v_w1_gate, w1_up=v_w1_up, w1_down=v_w1_down, norm_mix=v_norm_mix, w_in=v_w_in,
        ssm_log_dt=v_ssm_log_dt, ssm_a_re=v_ssm_a_re, ssm_a_im=v_ssm_a_im, ssm_b_re=v_ssm_b_re, ssm_b_im=v_ssm_b_im,
        ssm_c_re=v_ssm_c_re, ssm_c_im=v_ssm_c_im, ssm_d=v_ssm_d, ssm_w_glu=v_ssm_w_glu, gmlp_norm_v=v_gmlp_norm_v,
        gmlp_w_s=v_gmlp_w_s, gmlp_b_s=v_gmlp_b_s, norm_ssm_out=v_norm_ssm_out, norm_gmlp_out=v_norm_gmlp_out,
        w_out=v_w_out, norm_ffn2=v_norm_ffn2, w2_gate=v_w2_gate, w2_up=v_w2_up, w2_down=v_w2_down,
        norm_ple=v_norm_ple, w_ple_gate=v_w_ple_gate, w_ple_proj=v_w_ple_proj, norm_final=v_norm_final)
    names = list(weights)

    xs = x[0]
    ps = p[0, 0].astype(BF16)
    tgt = loss_target[0]
    d_model = xs.shape[1]
    d_ssm = d_model // 2
    n_groups = d_ssm // SSM_GROUP

    transposed = ("w1_gate", "w1_up", "w2_gate", "w2_up")
    big = {
        "w1_gate": 0, "w1_up": 0, "w1_down": 0, "w_in": 1, "ssm_w_glu": 0, "w_out": 0,
        "w2_gate": 0, "w2_up": 0, "w2_down": 0, "w_ple_gate": 0, "w_ple_proj": 1}
    big_names = list(big)

    def view(a, k):
        return a[0].T if k in transposed else a[0]

    def unview(a, k):
        return a.T[None] if k in transposed else a[None]

    shard = {k: _pad_to(view(weights[k], k).astype(BF16), big[k], LANE) for k in big_names}
    W = {}

    abar_r, abar_i, bbar_r, bbar_i = _ssm_discretize(ssm_log_dt[0], ssm_a_re[0], ssm_a_im[0], ssm_b_re[0], ssm_b_im[0])
    bc_r = _block_diag(jnp.swapaxes(bbar_r, 1, 2)).astype(BF16)
    bc_i = _block_diag(jnp.swapaxes(bbar_i, 1, 2)).astype(BF16)
    cc_r = _block_diag(jnp.swapaxes(ssm_c_re[0], 1, 2)).astype(BF16)
    cc_i = _block_diag(jnp.swapaxes(ssm_c_im[0], 1, 2)).astype(BF16)
    apw_f = _scan_constants(abar_r, abar_i, False)
    apw_b = _scan_constants(abar_r, abar_i, True)
    causal = jnp.tril(jnp.ones((CHUNK, CHUNK), dtype=bool))
    wm = jnp.where(causal[None], gmlp_w_s[0], 0.0).astype(BF16)
    wmt = jnp.swapaxes(wm, 1, 2)
    bs = gmlp_b_s[0][:, :, None]

    groups = [["w1_gate"], ["w1_up"], ["w1_down"], ["w_in", "ssm_w_glu", "w_out"], ["w2_gate"], ["w2_up"],
              ["w2_down", "w_ple_gate", "w_ple_proj"]]
    order = [k for g in groups for k in g]
    place = {k: i for i, k in enumerate(order)}
    me = (4 * lax.axis_index("x") + 2 * lax.axis_index("y") + lax.axis_index("c")).astype(jnp.int32).reshape(1)
    size = {k: shard[k].shape[big[k]] for k in order}
    in_flight, landing, sems = gather_start([shard[k] for k in order], [big[k] for k in order],
                                            [size[k] for k in order], [[place[k] for k in g] for g in groups],
                                            "gather_start")
    landing = [place_own_block(in_flight[place[k]], landing[place[k]], big[k], me, "place_" + k) for k in order]

    def landed(g, after):
        members = [place[k] for k in groups[g]]
        axes_g, sizes_g = [big[k] for k in groups[g]], [size[k] for k in groups[g]]
        bufs = gather_wait([in_flight[i] for i in members], [landing[i] for i in members], axes_g, sizes_g,
                           *sems[g], after, "gather_wait_%d" % g)
        return forward_task(bufs, axes_g, sizes_g)

    def arrive(g, after):
        W.update(zip(groups[g], run_tasks([landed(g, after)], "gather_forward_%d" % g)[0]))

    def arrive_during(g, after, fn, *a, **kw):
        out, (got,) = fn(*a, tasks=[landed(g, after)], **kw)
        W.update(zip(groups[g], got))
        return out

    xn1 = rmsnorm_fwd(xs, norm_ffn1, "norm_ffn1")
    arrive(0, xn1)
    gate1 = matmul(xn1, W["w1_gate"], "nt", "ffn1_gate")
    arrive(1, gate1)
    gate1, up1, act1 = ffn_up(xn1, W["w1_up"], gate1, "ffn1_up")
    arrive(2, act1)
    h1 = matmul(act1, W["w1_down"], "nn", "ffn1_down", res=xs, scale=0.5)
    arrive(3, h1)
    xn2 = rmsnorm_fwd(h1, norm_mix, "norm_mix")
    z = matmul(xn2, W["w_in"], "nn", "proj_in")
    y_pre, yg, sr, si = s5_fwd(z, bc_r, bc_i, cc_r, cc_i, apw_f, ssm_d, "s5_fwd")
    glin = matmul(yg, W["ssm_w_glu"], "nn", "ssm_glu")
    y_gmlp = gmlp_fwd(z, gmlp_norm_v, wm, bs, "gmlp_fwd")
    ycat = mix_out_fwd(y_pre, glin, y_gmlp, norm_ssm_out, norm_gmlp_out, "mix_out")
    h2 = arrive_during(4, ycat, matmul, ycat, W["w_out"], "nn", "proj_out", res=h1)
    xn3 = rmsnorm_fwd(h2, norm_ffn2, "norm_ffn2")
    gate2 = arrive_during(5, xn3, matmul, xn3, W["w2_gate"], "nt", "ffn2_gate")
    gate2, up2, act2 = arrive_during(6, gate2, ffn_up, xn3, W["w2_up"], gate2, "ffn2_up")
    h3 = matmul(act2, W["w2_down"], "nn", "ffn2_down", res=h2, scale=0.5)
    xn4 = rmsnorm_fwd(h3, norm_ple, "norm_ple")
    pg_lin = matmul(xn4, W["w_ple_gate"], "nn", "ple_gate")
    pp = matmul(ps, W["w_ple_proj"], "nn", "ple_proj")
    h4 = ple_fwd(h3, pg_lin, pp, "ple_fwd")
    dh4, loss_part, g_norm_final = final_loss(h4, tgt, norm_final.reshape(1, -1), "final_loss")
    loss = lax.psum(loss_part[0, 0], ("x", "y", "c"))

    G = {}
    reduced = {}
    chip_part = {}
    wait_sibling, wait_chips = [], []
    core = lax.axis_index("c").astype(jnp.int32).reshape(1)

    def grad(name_, value):
        G[name_] = value
        wait_sibling.append(name_)

    def carry(fn, *a, levels="ab", extra=None, **kw):
        tasks, kinds = [], []
        if extra is not None:
            tasks.append(extra[0])
            kinds.append(("x", extra[1]))
        if "a" in levels and wait_sibling:
            group = list(wait_sibling)
            wait_sibling.clear()
            tasks.append(to_sibling_task([G[k] for k in group], [big[k] for k in group]))
            kinds.append(("a", group))
        if "b" in levels and wait_chips:
            group = list(wait_chips)
            wait_chips.clear()
            tasks.append(across_chips_task([chip_part[k] for k in group]))
            kinds.append(("b", group))
        if not tasks:
            return fn(*a, **kw)
        out, task_outs = fn(*a, tasks=tasks, **kw)
        for (kind, group), outs in zip(kinds, task_outs):
            if kind == "x":
                group(outs)
                continue
            for k, r in zip(group, outs):
                if kind == "a":
                    chip_part[k] = rs_chip_sum(G[k], r, big[k], core, "rs_sum_" + k)
                    wait_chips.append(k)
                else:
                    reduced[k] = r
        return out

    small = {}
    small["norm_final"] = g_norm_final
    dpp, dpg = ple_bwd(dh4, pg_lin, pp, "ple_bwd")
    grad("w_ple_proj", matmul(ps, dpp, "tn", "grad_ple_proj", out_dtype=BF16))
    grad("w_ple_gate", carry(matmul, xn4, dpg, "tn", "grad_ple_gate", out_dtype=BF16))
    dxn4 = carry(matmul, dpg, W["w_ple_gate"], "nt", "ple_gate_bwd")
    dh3, dh3b, small["norm_ple"] = rmsnorm_bwd(dxn4, h3, norm_ple, dh4, "norm_ple_bwd")

    def ffn_bwd(tag, dhb, xn, gate, up, act, wg, wu, wd, extra=None, last_levels="ab"):
        dgate, dup = carry(ffn_bwd_act, dhb, W[wd], gate, up, tag + "_act_bwd", extra=extra)
        grad(wd, carry(matmul, act, dhb, "tn", tag + "_grad_down", out_dtype=BF16, scale=0.5))
        grad(wg, carry(matmul, dgate, xn, "tn", tag + "_grad_gate", out_dtype=BF16))
        grad(wu, carry(matmul, dup, xn, "tn", tag + "_grad_up", out_dtype=BF16))
        dxn = carry(matmul, dgate, W[wg], "nn", tag + "_gate_bwd")
        return carry(matmul, dup, W[wu], "nn", tag + "_up_bwd", res=dxn, levels=last_levels)

    dxn3 = ffn_bwd("ffn2", dh3b, xn3, gate2, up2, act2, "w2_gate", "w2_up", "w2_down", last_levels="a")
    dh2, dh2b, small["norm_ffn2"] = rmsnorm_bwd(dxn3, h2, norm_ffn2, dh3, "norm_ffn2_bwd")

    grad("w_out", matmul(ycat, dh2b, "tn", "grad_out", out_dtype=BF16))
    dycat = carry(matmul, dh2b, W["w_out"], "nt", "proj_out_bwd", levels="a")
    dyg_direct, dglin, dy_gmlp, small["norm_ssm_out"], small["norm_gmlp_out"] = mix_out_bwd(
        dycat, y_pre, glin, y_gmlp, norm_ssm_out, norm_gmlp_out, "mix_out_bwd")
    grad("ssm_w_glu", matmul(yg, dglin, "tn", "grad_glu", out_dtype=BF16))
    dyg = carry(matmul, dglin, W["ssm_w_glu"], "nt", "ssm_glu_bwd", res=dyg_direct, levels="a")
    du, small["ssm_d"], gc_r, gc_i, gb_r, gb_i, ga_r, ga_i = carry(
        s5_bwd, dyg, y_pre, z, sr, si, bc_r, bc_i, cc_r, cc_i, apw_b, ssm_d, "s5_bwd")
    dzu, dzv, small["gmlp_norm_v"], g_wm, g_bs = gmlp_bwd(dy_gmlp, z, gmlp_norm_v, wm, wmt, bs, "gmlp_bwd")
    small["gmlp_w_s"] = g_wm
    small["gmlp_b_s"] = g_bs
    small["c_re"] = _block_diag_extract(gc_r, SSM_GROUP, SSM_STATE)
    small["c_im"] = _block_diag_extract(gc_i, SSM_GROUP, SSM_STATE)
    small["bbar_r"] = jnp.swapaxes(_block_diag_extract(gb_r, SSM_GROUP, SSM_STATE), 1, 2)
    small["bbar_i"] = jnp.swapaxes(_block_diag_extract(gb_i, SSM_GROUP, SSM_STATE), 1, 2)
    small["abar_r"] = jnp.sum(ga_r, axis=0).reshape(n_groups, SSM_STATE)
    small["abar_i"] = jnp.sum(ga_i, axis=0).reshape(n_groups, SSM_STATE)

    dz = jnp.concatenate([du, dzu, dzv], axis=1)
    grad("w_in", matmul(xn2, dz, "tn", "grad_in", out_dtype=BF16))
    dxn2 = carry(matmul, dz, W["w_in"], "nt", "proj_in_bwd")
    dh1, dh1b, small["norm_mix"] = rmsnorm_bwd(dxn2, h1, norm_mix, dh2, "norm_mix_bwd")

    def pack(parts):
        flat = jnp.concatenate([v.reshape(-1) for v in parts.values()])
        return _pad_to(flat, 0, SUBLANE * LANE).reshape(-1, LANE), flat.shape[0]

    def unpack(everyones, n, parts, tag):
        rows = everyones.shape[0] // N_DEV
        summed = sum_devices(everyones.reshape(N_DEV, rows, LANE), "sum_" + tag).reshape(-1)[:n]
        out, off = {}, 0
        for k, v in parts.items():
            out[k] = summed[off:off + v.size].reshape(v.shape)
            off += v.size
        return out

    early = dict(small)
    flat_early, n_early = pack(early)
    small_landed = []
    dxn1 = ffn_bwd("ffn1", dh1b, xn1, gate1, up1, act1, "w1_gate", "w1_up", "w1_down",
                   extra=(gather_task([flat_early], [0]), small_landed.extend))
    tot = unpack(small_landed[0], n_early, early, "small")
    grad_x, _, g_norm_ffn1 = rmsnorm_bwd(dxn1, xs, norm_ffn1, dh1, "norm_ffn1_bwd")
    assert not wait_sibling and not wait_chips and set(reduced) == set(big_names)
    last = {"norm_ffn1": g_norm_ffn1}
    flat_last, n_last = pack(last)
    ((everyones_last,),) = run_tasks([gather_task([flat_last], [0])], "gather_last")
    tot.update(unpack(everyones_last, n_last, last, "last"))

    out_g, out_d, out_m, out_v = {}, {}, {}, {}
    for k in big_names:
        g, dl, nm, nv = adam_sharded(reduced[k], view(weights[k], k), view(moments_m[k], k), view(moments_v[k], k),
                                     "adam_" + k)
        out_g[k], out_d[k], out_m[k], out_v[k] = unview(g, k), unview(dl, k), unview(nm, k), unview(nv, k)

    _, ssm_vjp = jax.vjp(_ssm_discretize, ssm_log_dt[0], ssm_a_re[0], ssm_a_im[0], ssm_b_re[0], ssm_b_im[0])
    g_log_dt, g_a_re, g_a_im, g_b_re, g_b_im = ssm_vjp((tot["abar_r"], tot["abar_i"], tot["bbar_r"], tot["bbar_i"]))
    small_grads = {
        "norm_ffn1": tot["norm_ffn1"], "norm_mix": tot["norm_mix"], "ssm_log_dt": g_log_dt, "ssm_a_re": g_a_re,
        "ssm_a_im": g_a_im, "ssm_b_re": g_b_re, "ssm_b_im": g_b_im, "ssm_c_re": tot["c_re"], "ssm_c_im": tot["c_im"],
        "ssm_d": tot["ssm_d"], "gmlp_norm_v": tot["gmlp_norm_v"],
        "gmlp_w_s": jnp.where(causal[None], tot["gmlp_w_s"], 0.0), "gmlp_b_s": tot["gmlp_b_s"],
        "norm_ssm_out": tot["norm_ssm_out"], "norm_gmlp_out": tot["norm_gmlp_out"], "norm_ffn2": tot["norm_ffn2"],
        "norm_ple": tot["norm_ple"], "norm_final": tot["norm_final"]}
    swapped = ("ssm_b_re", "ssm_b_im")

    def pre(k, a):
        return jnp.swapaxes(a, -1, -2) if k in swapped else a

    def update(group, name_):
        items = [(_as2d(pre(k, small_grads[k].reshape(weights[k].shape))), _as2d(pre(k, weights[k])),
                  _as2d(pre(k, moments_m[k])), _as2d(pre(k, moments_v[k]))) for k in group]
        for k, it, (dl, nm, nv) in zip(group, items, adam_small(items, name_)):
            shp = pre(k, weights[k]).shape
            out_g[k], out_d[k], out_m[k], out_v[k] = [pre(k, a.reshape(shp)) for a in (it[0], dl, nm, nv)]

    update([k for k in small_grads if k != "norm_ffn1"], "adam_replicated")
    update(["norm_ffn1"], "adam_norm_ffn1")

    return (loss, grad_x[None], *[out_g[k] for k in names], *[out_d[k] for k in names],
            *[out_m[k] for k in names], *[out_v[k] for k in names])
```

```python
import math

import jax
import jax.numpy as jnp
from jax import lax
from jax.experimental import pallas as pl
from jax.experimental.pallas import tpu as pltpu

F32 = jnp.float32
BF16 = jnp.bfloat16
MESH_DT = pl.DeviceIdType.MESH

N_DEV = 8
N_CHIP = 4
LANE = 128
SUBLANE = 8
VMEM_LIMIT = 60 * 1024 * 1024

EPS = 1e-6
SSM_GROUP = 16
SSM_STATE = 64
GROUPS_PER_BLOCK = LANE // SSM_GROUP
STATE_BLOCK = GROUPS_PER_BLOCK * SSM_STATE
GMLP_HEAD = 128
CHUNK = 128

ADAM_LR = 0.001
ADAM_B1 = 0.9
ADAM_B2 = 0.999
ADAM_EPS = 1e-08
ADAM_WD = 0.01
ADAM_STEP = 10

GELU_K = math.sqrt(2.0 / math.pi)
GELU_C = 0.044715


def _cparams():
    return pltpu.CompilerParams(vmem_limit_bytes=VMEM_LIMIT)


def _tile(n, pref):
    if n <= pref:
        return n
    t = (pref // LANE) * LANE
    while t > 0:
        if n % t == 0:
            return t
        t -= LANE
    return n


def _row_tile(n, pref):
    if n <= pref:
        return n
    t = (pref // SUBLANE) * SUBLANE
    while t > 0:
        if n % t == 0:
            return t
        t -= SUBLANE
    return n


def _gelu(x):
    t = jnp.tanh(GELU_K * (x + GELU_C * x * x * x))
    return 0.5 * x * (1.0 + t)


def _gelu_grad(x):
    t = jnp.tanh(GELU_K * (x + GELU_C * x * x * x))
    return 0.5 * (1.0 + t) + 0.5 * x * (1.0 - t * t) * (GELU_K * (1.0 + 3.0 * GELU_C * x * x))


def _sigmoid(x):
    return 0.5 * jnp.tanh(0.5 * x) + 0.5


_DN = {
    "nn": (((1,), (0,)), ((), ())),
    "nt": (((1,), (1,)), ((), ())),
    "tn": (((0,), (0,)), ((), ())),
}


def _dot(a, b, mode="nn"):
    return lax.dot_general(a, b, _DN[mode], preferred_element_type=F32)


class CommTask:
    def __init__(self, inputs, out_shape, n_sems, start, late, finish, in_place=False):
        self.inputs, self.out_shape, self.n_sems = list(inputs), list(out_shape), n_sems
        self.start, self.late, self.finish = start, late, finish
        self.in_place = in_place


def _task_aliases(tasks, first_in, first_out):
    aliases = {}
    for t in tasks:
        if t.in_place:
            aliases.update({first_in + i: first_out + i for i in range(len(t.inputs))})
        first_in += len(t.inputs)
        first_out += len(t.out_shape)
    return aliases


def _call(body, *, name, grid, in_specs, out_specs, out_shape, args, scratch_shapes=(), tasks=()):
    in_specs, out_specs, out_shape = list(in_specs), list(out_specs), list(out_shape)
    scratch_shapes = list(scratch_shapes)
    if not tasks:
        return pl.pallas_call(
            body, name=name, grid=grid, in_specs=in_specs, out_specs=out_specs, out_shape=out_shape,
            scratch_shapes=scratch_shapes, compiler_params=_cparams())(*args)
    n_in, n_out, n_scr = len(in_specs), len(out_specs), len(scratch_shapes)
    t_in = [len(t.inputs) for t in tasks]
    t_out = [len(t.out_shape) for t in tasks]
    late_step = grid[0] - max(1, grid[0] // 4)
    has_late = grid[0] >= 2

    def carried(*refs):
        pos = n_in
        task_ins = []
        for k in t_in:
            task_ins.append(refs[pos:pos + k])
            pos += k
        outs = refs[pos:pos + n_out]
        pos += n_out
        task_outs = []
        for k in t_out:
            task_outs.append(refs[pos:pos + k])
            pos += k
        scratch = refs[pos:pos + n_scr]
        pos += n_scr
        sems = [refs[pos + 3 * i:pos + 3 * i + 3] for i in range(len(tasks))]
        ids = [pl.program_id(d) for d in range(len(grid))]
        rest_zero = True
        for d in range(1, len(grid)):
            rest_zero = jnp.logical_and(rest_zero, ids[d] == 0)
        first = jnp.logical_and(ids[0] == 0, rest_zero)
        last = ids[0] == grid[0] - 1
        for d in range(1, len(grid)):
            last = jnp.logical_and(last, ids[d] == grid[d] - 1)

        @pl.when(first)
        def _():
            for t, ti, to, s in zip(tasks, task_ins, task_outs, sems):
                t.start(ti, to, *s)

        if has_late:
            @pl.when(jnp.logical_and(ids[0] == late_step, rest_zero))
            def _():
                for t, ti, to, s in zip(tasks, task_ins, task_outs, sems):
                    t.late(ti, to, *s)

        body(*refs[:n_in], *outs, *scratch)

        @pl.when(last)
        def _():
            for t, ti, to, s in zip(tasks, task_ins, task_outs, sems):
                if not has_late:
                    t.late(ti, to, *s)
                t.finish(ti, to, *s)

    any_spec = pl.BlockSpec(memory_space=pl.ANY)
    sem_shapes = [pltpu.SemaphoreType.DMA((n,)) for t in tasks for n in t.n_sems]
    res = pl.pallas_call(
        carried, name=name, grid=grid,
        in_specs=in_specs + [any_spec] * sum(t_in), out_specs=out_specs + [any_spec] * sum(t_out),
        out_shape=out_shape + [s for t in tasks for s in t.out_shape],
        input_output_aliases=_task_aliases(tasks, n_in, n_out),
        scratch_shapes=scratch_shapes + sem_shapes, compiler_params=_cparams(),
    )(*args, *[a for t in tasks for a in t.inputs])
    res = list(res)
    task_res, pos = [], n_out
    for k in t_out:
        task_res.append(res[pos:pos + k])
        pos += k
    return res[:n_out], task_res


def _mm_dims(a, b, mode):
    if mode == "nn":
        (m, k), (k2, n) = a.shape, b.shape
    elif mode == "nt":
        (m, k), (n, k2) = a.shape, b.shape
    else:
        (k, m), (k2, n) = a.shape, b.shape
    assert k == k2, (a.shape, b.shape, mode)
    return m, n, k


def _mm_specs(mode, tm, tn, tk):
    if mode == "tn":
        a_spec = pl.BlockSpec((tk, tm), lambda i, j, k: (k, i))
    else:
        a_spec = pl.BlockSpec((tm, tk), lambda i, j, k: (i, k))
    if mode == "nt":
        b_spec = pl.BlockSpec((tn, tk), lambda i, j, k: (j, k))
    else:
        b_spec = pl.BlockSpec((tk, tn), lambda i, j, k: (k, j))
    return a_spec, b_spec


def _accumulate(acc, nk, partial, emit):
    if nk == 1:
        emit(partial)
        return
    kk = pl.program_id(2)

    @pl.when(kk == 0)
    def _():
        acc[...] = partial

    @pl.when(kk > 0)
    def _():
        acc[...] += partial

    @pl.when(kk == nk - 1)
    def _():
        emit(acc[...])


def matmul(a, b, mode, name, out_dtype=F32, res=None, scale=1.0, tm=1024, tn=1024, tk=2048, tasks=()):
    m, n, k = _mm_dims(a, b, mode)
    if mode == "tn":
        tk = max(tk, 4096)
    elif k > tk:
        tk, tn = k, (tn if k <= 3072 else tn // 2)
    tm, tn, tk = _tile(m, tm), _tile(n, tn), _tile(k, tk)
    nk = k // tk
    a_spec, b_spec = _mm_specs(mode, tm, tn, tk)
    o_spec = pl.BlockSpec((tm, tn), lambda i, j, k: (i, j))
    has_res = res is not None

    def body(*refs):
        if has_res:
            a_ref, b_ref, r_ref, o_ref, acc = refs
        else:
            a_ref, b_ref, o_ref, acc = refs

        def emit(v):
            if scale != 1.0:
                v = v * scale
            if has_res:
                v = r_ref[...] + v
            o_ref[...] = v.astype(out_dtype)

        _accumulate(acc, nk, _dot(a_ref[...], b_ref[...], mode), emit)

    out = _call(
        body, name=name, grid=(m // tm, n // tn, nk),
        in_specs=[a_spec, b_spec] + ([o_spec] if has_res else []), out_specs=[o_spec],
        out_shape=[jax.ShapeDtypeStruct((m, n), out_dtype)], args=(a, b) + ((res,) if has_res else ()),
        scratch_shapes=[pltpu.VMEM((tm, tn) if nk > 1 else (SUBLANE, LANE), F32)], tasks=tasks)
    return (out[0][0], out[1]) if tasks else out[0]


def ffn_up(xn, wu, gate, name, tm=1024, tn=1024, tk=2048, tasks=()):
    m, n, k = _mm_dims(xn, wu, "nt")
    tm, tn, tk = _tile(m, tm), _tile(n, tn), _tile(k, tk)
    nk = k // tk
    a_spec, b_spec = _mm_specs("nt", tm, tn, tk)
    o_spec = pl.BlockSpec((tm, tn), lambda i, j, k: (i, j))

    def body(a_ref, u_ref, gate_ref, gate_b_ref, up_b_ref, act_ref, acc):
        def emit(u):
            g = gate_ref[...]
            gate_b_ref[...] = g.astype(BF16)
            up_b_ref[...] = u.astype(BF16)
            act_ref[...] = (g * _sigmoid(g) * u).astype(BF16)

        _accumulate(acc, nk, _dot(a_ref[...], u_ref[...], "nt"), emit)

    out = _call(
        body, name=name, grid=(m // tm, n // tn, nk), in_specs=[a_spec, b_spec, o_spec],
        out_specs=[o_spec, o_spec, o_spec],
        out_shape=[jax.ShapeDtypeStruct((m, n), BF16), jax.ShapeDtypeStruct((m, n), BF16),
                   jax.ShapeDtypeStruct((m, n), BF16)],
        args=(xn, wu, gate), scratch_shapes=[pltpu.VMEM((tm, tn) if nk > 1 else (SUBLANE, LANE), F32)], tasks=tasks)
    return (tuple(out[0]), out[1]) if tasks else tuple(out)


def ffn_bwd_act(dh, wd, gate, up, name, tm=1024, tn=1024, tk=2048, tasks=()):
    m, n, k = _mm_dims(dh, wd, "nt")
    tm, tn, tk = _tile(m, tm), _tile(n, tn), _tile(k, tk)
    nk = k // tk
    a_spec, b_spec = _mm_specs("nt", tm, tn, tk)
    o_spec = pl.BlockSpec((tm, tn), lambda i, j, k: (i, j))

    def body(a_ref, b_ref, gate_ref, up_ref, dg_ref, du_ref, acc):
        def emit(total):
            dact = 0.5 * total
            g = gate_ref[...].astype(F32)
            sg = _sigmoid(g)
            du_ref[...] = (dact * (g * sg)).astype(BF16)
            dg_ref[...] = (dact * up_ref[...].astype(F32) * (sg * (1.0 + g * (1.0 - sg)))).astype(BF16)

        _accumulate(acc, nk, _dot(a_ref[...], b_ref[...], "nt"), emit)

    out = _call(
        body, name=name, grid=(m // tm, n // tn, nk), in_specs=[a_spec, b_spec, o_spec, o_spec],
        out_specs=[o_spec, o_spec],
        out_shape=[jax.ShapeDtypeStruct((m, n), BF16), jax.ShapeDtypeStruct((m, n), BF16)],
        args=(dh, wd, gate, up), scratch_shapes=[pltpu.VMEM((tm, tn) if nk > 1 else (SUBLANE, LANE), F32)],
        tasks=tasks)
    return (tuple(out[0]), out[1]) if tasks else tuple(out)


def _rows(t, d, tr):
    return pl.BlockSpec((tr, d), lambda i: (i, 0))


def _vec(d):
    return pl.BlockSpec((1, d), lambda i: (0, 0))


def rmsnorm_fwd(x, g, name, tr=512):
    t, d = x.shape
    tr = _row_tile(t, tr)

    def body(x_ref, g_ref, o_ref):
        xf = x_ref[...]
        r = lax.rsqrt(jnp.mean(xf * xf, axis=-1, keepdims=True) + EPS)
        o_ref[...] = (xf * r * g_ref[...]).astype(BF16)

    return pl.pallas_call(
        body, name=name, grid=(t // tr,), in_specs=[_rows(t, d, tr), _vec(d)], out_specs=_rows(t, d, tr),
        out_shape=jax.ShapeDtypeStruct((t, d), BF16), compiler_params=_cparams(),
    )(x, g)


def _rms_bwd(dxn, xf, g):
    r = lax.rsqrt(jnp.mean(xf * xf, axis=-1, keepdims=True) + EPS)
    xhat = xf * r
    dg = jnp.sum(dxn * xhat, axis=0, keepdims=True)
    dxh = dxn * g
    dx = r * (dxh - xhat * jnp.mean(dxh * xhat, axis=-1, keepdims=True))
    return dx, dg


def rmsnorm_bwd(dxn, x, g, dres, name, tr=256):
    t, d = x.shape
    tr = _row_tile(t, tr)

    def body(dxn_ref, x_ref, g_ref, dres_ref, o_ref, ob_ref, dg_ref):
        dx, dg = _rms_bwd(dxn_ref[...], x_ref[...], g_ref[...])
        out = dres_ref[...] + dx
        o_ref[...] = out
        ob_ref[...] = out.astype(BF16)

        @pl.when(pl.program_id(0) == 0)
        def _():
            dg_ref[...] = jnp.zeros_like(dg_ref)

        dg_ref[...] += dg

    return pl.pallas_call(
        body, name=name, grid=(t // tr,),
        in_specs=[_rows(t, d, tr), _rows(t, d, tr), _vec(d), _rows(t, d, tr)],
        out_specs=[_rows(t, d, tr), _rows(t, d, tr), _vec(d)],
        out_shape=[jax.ShapeDtypeStruct((t, d), F32), jax.ShapeDtypeStruct((t, d), BF16),
                   jax.ShapeDtypeStruct((1, d), F32)],
        compiler_params=_cparams(),
    )(dxn, x, g, dres)


def final_loss(h, target, g, name, tr=256):
    t, d = h.shape
    tr = _row_tile(t, tr)

    def body(h_ref, t_ref, g_ref, dh_ref, loss_ref, dg_ref):
        xf = h_ref[...]
        gg = g_ref[...]
        r = lax.rsqrt(jnp.mean(xf * xf, axis=-1, keepdims=True) + EPS)
        xhat = xf * r
        e = xhat * gg - t_ref[...]
        part = jnp.sum(jnp.sum(e * e, axis=1, keepdims=True), axis=0, keepdims=True) * (0.5 / d)
        dout = e * (1.0 / d)
        dg = jnp.sum(dout * xhat, axis=0, keepdims=True)
        dxh = dout * gg
        dh_ref[...] = r * (dxh - xhat * jnp.mean(dxh * xhat, axis=-1, keepdims=True))

        @pl.when(pl.program_id(0) == 0)
        def _():
            dg_ref[...] = jnp.zeros_like(dg_ref)
            loss_ref[...] = jnp.zeros_like(loss_ref)

        dg_ref[...] += dg
        loss_ref[...] += jnp.broadcast_to(part, loss_ref.shape)

    return pl.pallas_call(
        body, name=name, grid=(t // tr,),
        in_specs=[_rows(t, d, tr), _rows(t, d, tr), _vec(d)],
        out_specs=[_rows(t, d, tr), pl.BlockSpec((SUBLANE, LANE), lambda i: (0, 0)), _vec(d)],
        out_shape=[jax.ShapeDtypeStruct((t, d), F32), jax.ShapeDtypeStruct((SUBLANE, LANE), F32),
                   jax.ShapeDtypeStruct((1, d), F32)],
        compiler_params=_cparams(),
    )(h, target, g)


def ple_fwd(h, glin, pp, name, tr=512):
    t, d = h.shape
    tr = _row_tile(t, tr)

    def body(h_ref, gl_ref, pp_ref, o_ref):
        o_ref[...] = h_ref[...] + _sigmoid(gl_ref[...]) * pp_ref[...]

    sp = _rows(t, d, tr)
    return pl.pallas_call(
        body, name=name, grid=(t // tr,), in_specs=[sp, sp, sp], out_specs=sp,
        out_shape=jax.ShapeDtypeStruct((t, d), F32), compiler_params=_cparams(),
    )(h, glin, pp)


def ple_bwd(dh, glin, pp, name, tr=512):
    t, d = dh.shape
    tr = _row_tile(t, tr)

    def body(dh_ref, gl_ref, pp_ref, dpp_ref, dgl_ref):
        gate = _sigmoid(gl_ref[...])
        dh_ = dh_ref[...]
        dpp_ref[...] = (dh_ * gate).astype(BF16)
        dgl_ref[...] = (dh_ * pp_ref[...] * gate * (1.0 - gate)).astype(BF16)

    sp = _rows(t, d, tr)
    return pl.pallas_call(
        body, name=name, grid=(t // tr,), in_specs=[sp, sp, sp], out_specs=[sp, sp],
        out_shape=[jax.ShapeDtypeStruct((t, d), BF16), jax.ShapeDtypeStruct((t, d), BF16)],
        compiler_params=_cparams(),
    )(dh, glin, pp)


def mix_out_fwd(y_pre, glin, y_gmlp, g_so, g_go, name, tr=512):
    t, d = y_pre.shape
    tr = _row_tile(t, tr)

    def body(yp_ref, gl_ref, yg_ref, gs_ref, gg_ref, o_ref):
        ys = _gelu(yp_ref[...]) * _sigmoid(gl_ref[...])
        r = lax.rsqrt(jnp.mean(ys * ys, axis=-1, keepdims=True) + EPS)
        o_ref[:, 0:d] = (ys * r * gs_ref[...]).astype(BF16)
        yq = yg_ref[...]
        r2 = lax.rsqrt(jnp.mean(yq * yq, axis=-1, keepdims=True) + EPS)
        o_ref[:, d:2 * d] = (yq * r2 * gg_ref[...]).astype(BF16)

    sp = _rows(t, d, tr)
    return pl.pallas_call(
        body, name=name, grid=(t // tr,), in_specs=[sp, sp, sp, _vec(d), _vec(d)],
        out_specs=_rows(t, 2 * d, tr), out_shape=jax.ShapeDtypeStruct((t, 2 * d), BF16),
        compiler_params=_cparams(),
    )(y_pre, glin, y_gmlp, g_so, g_go)


def mix_out_bwd(dycat, y_pre, glin, y_gmlp, g_so, g_go, name, tr=256):
    t, d = y_pre.shape
    tr = _row_tile(t, tr)

    def body(dy_ref, yp_ref, gl_ref, yg_ref, gs_ref, gg_ref, dyg_ref, dl_ref, dyq_ref, dgs_ref, dgg_ref):
        yg = _gelu(yp_ref[...])
        sg = _sigmoid(gl_ref[...])
        dys, dgs = _rms_bwd(dy_ref[:, 0:d], yg * sg, gs_ref[...])
        dyg_ref[...] = dys * sg
        dl_ref[...] = (dys * yg * sg * (1.0 - sg)).astype(BF16)
        dyq, dgg = _rms_bwd(dy_ref[:, d:2 * d], yg_ref[...], gg_ref[...])
        dyq_ref[...] = dyq

        @pl.when(pl.program_id(0) == 0)
        def _():
            dgs_ref[...] = jnp.zeros_like(dgs_ref)
            dgg_ref[...] = jnp.zeros_like(dgg_ref)

        dgs_ref[...] += dgs
        dgg_ref[...] += dgg

    sp = _rows(t, d, tr)
    return pl.pallas_call(
        body, name=name, grid=(t // tr,),
        in_specs=[_rows(t, 2 * d, tr), sp, sp, sp, _vec(d), _vec(d)],
        out_specs=[sp, sp, sp, _vec(d), _vec(d)],
        out_shape=[jax.ShapeDtypeStruct((t, d), F32), jax.ShapeDtypeStruct((t, d), BF16),
                   jax.ShapeDtypeStruct((t, d), F32), jax.ShapeDtypeStruct((1, d), F32),
                   jax.ShapeDtypeStruct((1, d), F32)],
        compiler_params=_cparams(),
    )(dycat, y_pre, glin, y_gmlp, g_so, g_go)


SCAN_COLS = 512


def _scan_tile(xr, xi, const, cr, ci, reverse):
    for lvl, sh in enumerate((1, 2, 4)):
        ar, ai = const(2 * lvl), const(2 * lvl + 1)
        s = (SUBLANE - sh) if reverse else sh
        rr = pltpu.roll(xr, s, 0)
        ri = pltpu.roll(xi, s, 0)
        xr, xi = xr + ar * rr - ai * ri, xi + ar * ri + ai * rr
    pr, pi_ = const(6), const(7)
    xr, xi = xr + pr * cr - pi_ * ci, xi + pr * ci + pi_ * cr
    return xr, xi


def _bcast_row(x, row):
    return jnp.broadcast_to(x[row:row + 1, :], x.shape)


def s5_fwd(z, bc_r, bc_i, cc_r, cc_i, apw, dvec, name, tc=512, tasks=()):
    t = z.shape[0]
    nblk = bc_r.shape[0]
    d = nblk * LANE
    ns = nblk * STATE_BLOCK
    tc = _row_tile(t, tc)
    ntile = tc // SUBLANE

    def body(z_ref, br_ref, bi_ref, cr_ref, ci_ref, apw_ref, d_ref, y_ref, yg_ref, sr_ref, si_ref, carry):
        @pl.when(pl.program_id(0) == 0)
        def _():
            carry[...] = jnp.zeros_like(carry)

        for j in range(nblk):
            uj = z_ref[:, j * LANE:(j + 1) * LANE]
            ub = uj.astype(BF16)
            for q in range(STATE_BLOCK // SCAN_COLS):
                c0 = j * STATE_BLOCK + q * SCAN_COLS
                cs = pl.ds(c0, SCAN_COLS)
                bs = slice(q * SCAN_COLS, (q + 1) * SCAN_COLS)
                sr_ref[:, cs] = _dot(ub, br_ref[j, :, bs])
                si_ref[:, cs] = _dot(ub, bi_ref[j, :, bs])
                const = lambda k, cs=cs: apw_ref[k, :, cs]

                def tile(k, c, cs=cs, const=const):
                    rows = pl.ds(pl.multiple_of(k * SUBLANE, SUBLANE), SUBLANE)
                    xr, xi = _scan_tile(sr_ref[rows, cs], si_ref[rows, cs], const, c[0], c[1], False)
                    sr_ref[rows, cs] = xr
                    si_ref[rows, cs] = xi
                    return _bcast_row(xr, SUBLANE - 1), _bcast_row(xi, SUBLANE - 1)

                c_r, c_i = lax.fori_loop(0, ntile, tile, (carry[0, :, cs], carry[1, :, cs]))
                carry[0, :, cs] = c_r
                carry[1, :, cs] = c_i
            sb = pl.ds(j * STATE_BLOCK, STATE_BLOCK)
            y = (_dot(sr_ref[:, sb].astype(BF16), cr_ref[j]) - _dot(si_ref[:, sb].astype(BF16), ci_ref[j])
                 + d_ref[:, j * LANE:(j + 1) * LANE] * uj)
            y_ref[:, j * LANE:(j + 1) * LANE] = y
            yg_ref[:, j * LANE:(j + 1) * LANE] = _gelu(y).astype(BF16)

    full3 = lambda shp: pl.BlockSpec(shp, lambda i: (0, 0, 0))
    out = _call(
        body, name=name, grid=(t // tc,),
        in_specs=[pl.BlockSpec((tc, d), lambda i: (i, 0)), full3(bc_r.shape), full3(bc_i.shape),
                  full3(cc_r.shape), full3(cc_i.shape), full3(apw.shape), _vec(d)],
        out_specs=[pl.BlockSpec((tc, d), lambda i: (i, 0)), pl.BlockSpec((tc, d), lambda i: (i, 0)),
                   pl.BlockSpec((tc, ns), lambda i: (i, 0)), pl.BlockSpec((tc, ns), lambda i: (i, 0))],
        out_shape=[jax.ShapeDtypeStruct((t, d), F32), jax.ShapeDtypeStruct((t, d), BF16),
                   jax.ShapeDtypeStruct((t, ns), F32), jax.ShapeDtypeStruct((t, ns), F32)],
        args=(z, bc_r, bc_i, cc_r, cc_i, apw, dvec), scratch_shapes=[pltpu.VMEM((2, SUBLANE, ns), F32)], tasks=tasks)
    return (tuple(out[0]), out[1]) if tasks else tuple(out)


def s5_bwd(dyg, y_pre, z, sr, si, bc_r, bc_i, cc_r, cc_i, apw_rev, dvec, name, tc=256, tasks=()):
    t = z.shape[0]
    nblk = bc_r.shape[0]
    d = nblk * LANE
    ns = nblk * STATE_BLOCK
    tc = _row_tile(t, tc)
    ntile = tc // SUBLANE
    nchunk = t // tc
    tiles_per_chunk = tc // SUBLANE

    def body(dyg_ref, yp_ref, z_ref, sr_ref, si_ref, pr_ref, pi_ref, br_ref, bi_ref, cr_ref, ci_ref, apw_ref,
             d_ref, du_ref, gd_ref, gcr_ref, gci_ref, gbr_ref, gbi_ref, gar_ref, gai_ref, lr_ref, li_ref, carry):
        step = pl.program_id(0)

        @pl.when(step == 0)
        def _():
            carry[...] = jnp.zeros_like(carry)
            for ref in (gd_ref, gcr_ref, gci_ref, gbr_ref, gbi_ref, gar_ref, gai_ref):
                ref[...] = jnp.zeros_like(ref)

        first_chunk = (step == nchunk - 1).astype(F32)
        keep_prev = 1.0 - first_chunk
        row0 = lax.broadcasted_iota(jnp.int32, (SUBLANE, SCAN_COLS), 0) == 0

        for j in range(nblk):
            lanes = slice(j * LANE, (j + 1) * LANE)
            uj = z_ref[:, lanes]
            ub = uj.astype(BF16)
            gy = dyg_ref[:, lanes] * _gelu_grad(yp_ref[:, lanes])
            gyb = gy.astype(BF16)
            gd_ref[:, lanes] += jnp.sum(gy * uj, axis=0, keepdims=True)
            for q in range(STATE_BLOCK // SCAN_COLS):
                c0 = j * STATE_BLOCK + q * SCAN_COLS
                cs = pl.ds(c0, SCAN_COLS)
                bs = slice(q * SCAN_COLS, (q + 1) * SCAN_COLS)
                lr_ref[:, cs] = _dot(gyb, cr_ref[j, bs, :], "nt")
                li_ref[:, cs] = -_dot(gyb, ci_ref[j, bs, :], "nt")
                const = lambda k, cs=cs: apw_ref[k, :, cs]

                def one_tile(rows, prev_r, prev_i, c, cs=cs, const=const):
                    cr_, ci_, gar, gai = c
                    xr, xi = _scan_tile(lr_ref[rows, cs], li_ref[rows, cs], const, cr_, ci_, True)
                    lr_ref[rows, cs] = xr
                    li_ref[rows, cs] = xi
                    spr = jnp.where(row0, prev_r, pltpu.roll(sr_ref[rows, cs], 1, 0))
                    spi = jnp.where(row0, prev_i, pltpu.roll(si_ref[rows, cs], 1, 0))
                    gar = gar + xr * spr + xi * spi
                    gai = gai + xi * spr - xr * spi
                    return _bcast_row(xr, 0), _bcast_row(xi, 0), gar, gai

                def tile(k, c, cs=cs, one_tile=one_tile):
                    kk = ntile - 1 - k
                    rows = pl.ds(pl.multiple_of(kk * SUBLANE, SUBLANE), SUBLANE)
                    prow = pl.ds(pl.multiple_of((kk - 1) * SUBLANE, SUBLANE), SUBLANE)
                    prev_r = _bcast_row(sr_ref[prow, cs], SUBLANE - 1)
                    prev_i = _bcast_row(si_ref[prow, cs], SUBLANE - 1)
                    return one_tile(rows, prev_r, prev_i, c)

                zero = jnp.zeros((SUBLANE, SCAN_COLS), F32)
                c = lax.fori_loop(0, ntile - 1, tile, (carry[0, :, cs], carry[1, :, cs], zero, zero))
                prev_r = _bcast_row(pr_ref[:, cs], SUBLANE - 1) * keep_prev
                prev_i = _bcast_row(pi_ref[:, cs], SUBLANE - 1) * keep_prev
                c_r, c_i, gar, gai = one_tile(pl.ds(0, SUBLANE), prev_r, prev_i, c)
                carry[0, :, cs] = c_r
                carry[1, :, cs] = c_i
                gar_ref[:, cs] += gar
                gai_ref[:, cs] += gai
            sb = pl.ds(j * STATE_BLOCK, STATE_BLOCK)
            lrb = lr_ref[:, sb].astype(BF16)
            lib = li_ref[:, sb].astype(BF16)
            gcr_ref[j] += _dot(gyb, sr_ref[:, sb].astype(BF16), "tn")
            gci_ref[j] -= _dot(gyb, si_ref[:, sb].astype(BF16), "tn")
            gbr_ref[j] += _dot(ub, lrb, "tn")
            gbi_ref[j] += _dot(ub, lib, "tn")
            du = _dot(lrb, br_ref[j], "nt") + _dot(lib, bi_ref[j], "nt") + gy * d_ref[:, lanes]
            du_ref[:, lanes] = du.astype(BF16)

    rev = lambda i: (nchunk - 1 - i, 0)
    prev = lambda i: (jnp.maximum((nchunk - 1 - i) * tiles_per_chunk - 1, 0), 0)
    full3 = lambda shp: pl.BlockSpec(shp, lambda i: (0, 0, 0))
    acc3 = pl.BlockSpec((nblk, LANE, STATE_BLOCK), lambda i: (0, 0, 0))
    acc_rows = pl.BlockSpec((SUBLANE, ns), lambda i: (0, 0))
    out = _call(
        body, name=name, grid=(nchunk,),
        in_specs=[pl.BlockSpec((tc, d), rev), pl.BlockSpec((tc, d), rev), pl.BlockSpec((tc, d), rev),
                  pl.BlockSpec((tc, ns), rev), pl.BlockSpec((tc, ns), rev),
                  pl.BlockSpec((SUBLANE, ns), prev), pl.BlockSpec((SUBLANE, ns), prev),
                  full3(bc_r.shape), full3(bc_i.shape), full3(cc_r.shape), full3(cc_i.shape), full3(apw_rev.shape),
                  _vec(d)],
        out_specs=[pl.BlockSpec((tc, d), rev), _vec(d), acc3, acc3, acc3, acc3, acc_rows, acc_rows],
        out_shape=[jax.ShapeDtypeStruct((t, d), BF16), jax.ShapeDtypeStruct((1, d), F32)]
        + [jax.ShapeDtypeStruct((nblk, LANE, STATE_BLOCK), F32)] * 4
        + [jax.ShapeDtypeStruct((SUBLANE, ns), F32)] * 2,
        args=(dyg, y_pre, z, sr, si, sr, si, bc_r, bc_i, cc_r, cc_i, apw_rev, dvec),
        scratch_shapes=[pltpu.VMEM((tc, ns), F32), pltpu.VMEM((tc, ns), F32), pltpu.VMEM((2, SUBLANE, ns), F32)],
        tasks=tasks)
    return (tuple(out[0]), out[1]) if tasks else tuple(out)


def _cmul(a, b):
    return a[0] * b[0] - a[1] * b[1], a[0] * b[1] + a[1] * b[0]


def _scan_constants(abar_r, abar_i, reverse):
    ar = abar_r.reshape(1, -1)
    ai = abar_i.reshape(1, -1)
    if reverse:
        ai = -ai
    pw = [(ar, ai)]
    for _ in range(SUBLANE - 1):
        pw.append(_cmul(pw[-1], (ar, ai)))
    rows = lax.broadcasted_iota(jnp.int32, (SUBLANE, 1), 0)
    out = []
    for sh in (1, 2, 4):
        keep = (rows <= SUBLANE - 1 - sh) if reverse else (rows >= sh)
        for part in pw[sh - 1]:
            out.append(jnp.where(keep, part, 0.0))
    for comp in (0, 1):
        stack = jnp.concatenate([pw[k][comp] for k in range(SUBLANE)], axis=0)
        out.append(stack[::-1] if reverse else stack)
    return jnp.stack(out, axis=0).astype(F32)


def _ssm_discretize(log_dt, a_re, a_im, b_re, b_im):
    dt = jnp.exp(log_dt)[:, None]
    lr = jnp.minimum(a_re, -1e-4)
    li = a_im
    mag = jnp.exp(lr * dt)
    ang = li * dt
    abar_r = mag * jnp.cos(ang)
    abar_i = mag * jnp.sin(ang)
    den = lr * lr + li * li
    xr = abar_r - 1.0
    xi = abar_i
    zr = (xr * lr + xi * li) / den
    zi = (xi * lr - xr * li) / den
    bbar_r = zr[..., None] * b_re - zi[..., None] * b_im
    bbar_i = zr[..., None] * b_im + zi[..., None] * b_re
    return abar_r, abar_i, bbar_r, bbar_i


def _block_diag(w):
    g, a, b = w.shape
    nb = g // GROUPS_PER_BLOCK
    eye = jnp.eye(GROUPS_PER_BLOCK, dtype=w.dtype)
    w5 = w.reshape(nb, GROUPS_PER_BLOCK, a, b)
    out = w5[:, :, :, None, :] * eye[None, :, None, :, None]
    return out.reshape(nb, GROUPS_PER_BLOCK * a, GROUPS_PER_BLOCK * b)


def _block_diag_extract(m, a, b):
    nb = m.shape[0]
    eye = jnp.eye(GROUPS_PER_BLOCK, dtype=m.dtype)
    m5 = m.reshape(nb, GROUPS_PER_BLOCK, a, GROUPS_PER_BLOCK, b)
    out = jnp.sum(m5 * eye[None, :, None, :, None], axis=3)
    return out.reshape(nb * GROUPS_PER_BLOCK, a, b)


def _layer_norm(gv, nv):
    mu = jnp.mean(gv, axis=-1, keepdims=True)
    xc = gv - mu
    r = lax.rsqrt(jnp.mean(xc * xc, axis=-1, keepdims=True) + EPS)
    xhat = xc * r
    return xhat * nv, xhat, r


def gmlp_fwd(z, norm_v, wm, bs, name, tr=256):
    t = z.shape[0]
    nh = wm.shape[0]
    d = nh * GMLP_HEAD
    col0 = (z.shape[1] - 2 * d) // d
    tr = _row_tile(t, tr)

    def body(zu_ref, zv_ref, nv_ref, wm_ref, bs_ref, o_ref):
        v, _, _ = _layer_norm(_gelu(zv_ref[...]), nv_ref[...])
        vb = v.astype(BF16)
        u = _gelu(zu_ref[...])
        for c in range(tr // CHUNK):
            rows = slice(c * CHUNK, (c + 1) * CHUNK)
            for h in range(nh):
                cols = slice(h * GMLP_HEAD, (h + 1) * GMLP_HEAD)
                s = _dot(wm_ref[h], vb[rows, cols]) + bs_ref[h]
                o_ref[rows, cols] = u[rows, cols] * s

    return pl.pallas_call(
        body, name=name, grid=(t // tr,),
        in_specs=[pl.BlockSpec((tr, d), lambda i: (i, col0)), pl.BlockSpec((tr, d), lambda i: (i, col0 + 1)),
                  _vec(d), pl.BlockSpec(wm.shape, lambda i: (0, 0, 0)), pl.BlockSpec(bs.shape, lambda i: (0, 0, 0))],
        out_specs=pl.BlockSpec((tr, d), lambda i: (i, 0)),
        out_shape=jax.ShapeDtypeStruct((t, d), F32), compiler_params=_cparams(),
    )(z, z, norm_v, wm, bs)


def gmlp_bwd(dy, z, norm_v, wm, wmt, bs, name, tr=256):
    t = z.shape[0]
    nh = wm.shape[0]
    d = nh * GMLP_HEAD
    col0 = (z.shape[1] - 2 * d) // d
    tr = _row_tile(t, tr)

    def body(dy_ref, zu_ref, zv_ref, nv_ref, wm_ref, wmt_ref, bs_ref, dzu_ref, dzv_ref, dnv_ref, dwm_ref, dbs_ref,
             dv_ref):
        @pl.when(pl.program_id(0) == 0)
        def _():
            dnv_ref[...] = jnp.zeros_like(dnv_ref)
            dwm_ref[...] = jnp.zeros_like(dwm_ref)
            dbs_ref[...] = jnp.zeros_like(dbs_ref)

        zv = zv_ref[...]
        nv = nv_ref[...]
        v, xhat, r = _layer_norm(_gelu(zv), nv)
        vb = v.astype(BF16)
        zu = zu_ref[...]
        u = _gelu(zu)
        dy_ = dy_ref[...]
        for c in range(tr // CHUNK):
            rows = slice(c * CHUNK, (c + 1) * CHUNK)
            for h in range(nh):
                cols = slice(h * GMLP_HEAD, (h + 1) * GMLP_HEAD)
                vh = vb[rows, cols]
                s = _dot(wm_ref[h], vh) + bs_ref[h]
                dyh = dy_[rows, cols]
                dzu_ref[rows, cols] = (dyh * s * _gelu_grad(zu[rows, cols])).astype(BF16)
                ds = dyh * u[rows, cols]
                dsb = ds.astype(BF16)
                dbs_ref[h] += jnp.sum(ds, axis=1, keepdims=True)
                dwm_ref[h] += _dot(dsb, vh, "nt")
                dv_ref[rows, cols] = _dot(wmt_ref[h], dsb)
        dv = dv_ref[...]
        dnv_ref[...] += jnp.sum(dv * xhat, axis=0, keepdims=True)
        dxh = dv * nv
        dgv = r * (dxh - jnp.mean(dxh, axis=-1, keepdims=True) - xhat * jnp.mean(dxh * xhat, axis=-1, keepdims=True))
        dzv_ref[...] = (dgv * _gelu_grad(zv)).astype(BF16)

    full3 = lambda shp: pl.BlockSpec(shp, lambda i: (0, 0, 0))
    rows_d = pl.BlockSpec((tr, d), lambda i: (i, 0))
    return pl.pallas_call(
        body, name=name, grid=(t // tr,),
        in_specs=[rows_d, pl.BlockSpec((tr, d), lambda i: (i, col0)), pl.BlockSpec((tr, d), lambda i: (i, col0 + 1)),
                  _vec(d), full3(wm.shape), full3(wmt.shape), full3(bs.shape)],
        out_specs=[rows_d, rows_d, _vec(d), full3((nh, CHUNK, CHUNK)), full3((nh, CHUNK, 1))],
        out_shape=[jax.ShapeDtypeStruct((t, d), BF16), jax.ShapeDtypeStruct((t, d), BF16),
                   jax.ShapeDtypeStruct((1, d), F32), jax.ShapeDtypeStruct((nh, CHUNK, CHUNK), F32),
                   jax.ShapeDtypeStruct((nh, CHUNK, 1), F32)],
        scratch_shapes=[pltpu.VMEM((tr, d), F32)], compiler_params=_cparams(),
    )(dy, z, z, norm_v, wm, wmt, bs)


def _block(ref, axis, size, k):
    start = pl.multiple_of(k * size, size)
    if axis == 0:
        return ref.at[pl.ds(start, size), :]
    return ref.at[:, pl.ds(start, size)]


def _place():
    x, y, c = lax.axis_index("x"), lax.axis_index("y"), lax.axis_index("c")
    chips = [(1 - x, y), (x, 1 - y), (1 - x, 1 - y)]
    return x, y, c, chips


def _dev(x, y, c):
    return 4 * x + 2 * y + c


def gather_task(shards, axes):
    n = len(shards)
    sizes = [s.shape[ax] for s, ax in zip(shards, axes)]
    out_shape = [
        jax.ShapeDtypeStruct((s.shape[0] * N_DEV, s.shape[1]) if ax == 0 else (s.shape[0], s.shape[1] * N_DEV), s.dtype)
        for s, ax in zip(shards, axes)
    ]

    def copy(ins, outs, send_sems, recv_sems, t, k, block, to, from_input=False):
        dst = _block(outs[t], axes[t], sizes[t], _dev(*block))
        return pltpu.make_async_remote_copy(
            src_ref=ins[t] if from_input else dst, dst_ref=dst,
            send_sem=send_sems.at[t * 7 + k], recv_sem=recv_sems.at[t * 7 + k],
            device_id=to, device_id_type=MESH_DT)

    def local(ins, outs, local_sems, t, me):
        return pltpu.make_async_copy(ins[t], _block(outs[t], axes[t], sizes[t], _dev(*me)), local_sems.at[t])

    def start(ins, outs, send_sems, recv_sems, local_sems):
        x, y, c, chips = _place()
        me, sibling = (x, y, c), (x, y, 1 - c)
        for t in range(n):
            local(ins, outs, local_sems, t, me).start()
        for t in range(n):
            copy(ins, outs, send_sems, recv_sems, t, 0, me, sibling, True).start()
            for j, chip in enumerate(chips):
                copy(ins, outs, send_sems, recv_sems, t, 1 + j, me, (*chip, c), True).start()

    def late(ins, outs, send_sems, recv_sems, local_sems):
        x, y, c, chips = _place()
        me, sibling = (x, y, c), (x, y, 1 - c)
        for t in range(n):
            for j, chip in enumerate(chips):
                copy(ins, outs, send_sems, recv_sems, t, 1 + j, (*chip, c), me).wait_recv()
                copy(ins, outs, send_sems, recv_sems, t, 4 + j, (*chip, c), sibling).start()

    def finish(ins, outs, send_sems, recv_sems, local_sems):
        x, y, c, chips = _place()
        me, sibling = (x, y, c), (x, y, 1 - c)
        for t in range(n):
            copy(ins, outs, send_sems, recv_sems, t, 0, sibling, me).wait_recv()
            for j, chip in enumerate(chips):
                copy(ins, outs, send_sems, recv_sems, t, 4 + j, (*chip, 1 - c), me).wait_recv()
        for t in range(n):
            copy(ins, outs, send_sems, recv_sems, t, 0, me, sibling, True).wait_send()
            for j, chip in enumerate(chips):
                copy(ins, outs, send_sems, recv_sems, t, 1 + j, me, (*chip, c), True).wait_send()
                copy(ins, outs, send_sems, recv_sems, t, 4 + j, (*chip, c), sibling).wait_send()
            local(ins, outs, local_sems, t, me).wait()

    return CommTask(shards, out_shape, (7 * n, 7 * n, n), start, late, finish)


def _blk3(shape2, axis):
    r, c = shape2
    return (r // N_DEV, c) if axis == 0 else (r, c // N_DEV)


def _no_late(ins, outs, send_sems, recv_sems, local_sems):
    pass


def to_sibling_task(grads, axes):
    n = len(grads)
    blks = [_blk3(g.shape, ax) for g, ax in zip(grads, axes)]
    sizes = [b[ax] for b, ax in zip(blks, axes)]

    def copies(ins, outs, send_sems, recv_sems):
        x, y, c, _ = _place()
        return [pltpu.make_async_remote_copy(
            src_ref=_block(ins[t], axes[t], sizes[t], 2 * i + (1 - c)), dst_ref=outs[t].at[i],
            send_sem=send_sems.at[t * N_CHIP + i], recv_sem=recv_sems.at[t * N_CHIP + i],
            device_id=(x, y, 1 - c), device_id_type=MESH_DT) for t in range(n) for i in range(N_CHIP)]

    def start(ins, outs, send_sems, recv_sems, local_sems):
        for cp in copies(ins, outs, send_sems, recv_sems):
            cp.start()

    def finish(ins, outs, send_sems, recv_sems, local_sems):
        cps = copies(ins, outs, send_sems, recv_sems)
        for cp in cps:
            cp.wait_recv()
        for cp in cps:
            cp.wait_send()

    out_shape = [jax.ShapeDtypeStruct((N_CHIP,) + b, g.dtype) for b, g in zip(blks, grads)]
    return CommTask(grads, out_shape, (N_CHIP * n, N_CHIP * n, 1), start, _no_late, finish)


def across_chips_task(parts):
    n = len(parts)

    def copies(ins, outs, send_sems, recv_sems):
        x, y, c, chips = _place()
        my_chip = 2 * x + y
        return [pltpu.make_async_remote_copy(
            src_ref=ins[t].at[2 * chip[0] + chip[1]], dst_ref=outs[t].at[my_chip],
            send_sem=send_sems.at[t * 3 + j], recv_sem=recv_sems.at[t * 3 + j],
            device_id=(*chip, c), device_id_type=MESH_DT) for t in range(n) for j, chip in enumerate(chips)]

    def mine(ins, outs, local_sems):
        x, y, _, _ = _place()
        my_chip = 2 * x + y
        return [pltpu.make_async_copy(ins[t].at[my_chip], outs[t].at[my_chip], local_sems.at[t]) for t in range(n)]

    def start(ins, outs, send_sems, recv_sems, local_sems):
        for cp in mine(ins, outs, local_sems):
            cp.start()
        for cp in copies(ins, outs, send_sems, recv_sems):
            cp.start()

    def finish(ins, outs, send_sems, recv_sems, local_sems):
        cps = copies(ins, outs, send_sems, recv_sems)
        for cp in cps:
            cp.wait_recv()
        for cp in cps:
            cp.wait_send()
        for cp in mine(ins, outs, local_sems):
            cp.wait()

    out_shape = [jax.ShapeDtypeStruct(p.shape, p.dtype) for p in parts]
    return CommTask(parts, out_shape, (3 * n, 3 * n, n), start, _no_late, finish)


def run_tasks(tasks, name):
    t_in = [len(t.inputs) for t in tasks]
    t_out = [len(t.out_shape) for t in tasks]

    def body(*refs):
        pos, views = 0, []
        for k in t_in:
            views.append([refs[pos:pos + k]])
            pos += k
        for v, k in zip(views, t_out):
            v.append(refs[pos:pos + k])
            pos += k
        for i, v in enumerate(views):
            v.extend(refs[pos + 3 * i:pos + 3 * i + 3])
        for phase in ("start", "late", "finish"):
            for t, v in zip(tasks, views):
                getattr(t, phase)(*v)

    any_spec = pl.BlockSpec(memory_space=pl.ANY)
    res = pl.pallas_call(
        body, name=name, in_specs=[any_spec] * sum(t_in), out_specs=[any_spec] * sum(t_out),
        out_shape=[s for t in tasks for s in t.out_shape], input_output_aliases=_task_aliases(tasks, 0, 0),
        scratch_shapes=[pltpu.SemaphoreType.DMA((k,)) for t in tasks for k in t.n_sems],
    )(*[a for t in tasks for a in t.inputs])
    res, out, pos = list(res), [], 0
    for k in t_out:
        out.append(res[pos:pos + k])
        pos += k
    return out


_HBM_SPEC = pl.BlockSpec(memory_space=pl.ANY)
_SEM_SPEC = pl.BlockSpec(memory_space=pltpu.SEMAPHORE)
_DATAFLOW = pltpu.SideEffectType.DATAFLOW_SIDE_EFFECTING


def _full_shape(s, ax):
    return (s.shape[0] * N_DEV, s.shape[1]) if ax == 0 else (s.shape[0], s.shape[1] * N_DEV)


def _level1_copy(src, landing, axis, size, send_sems, recv_sems, slot, sender, to):
    dst = _block(landing, axis, size, _dev(*sender))
    return pltpu.make_async_remote_copy(src_ref=src, dst_ref=dst, send_sem=send_sems.at[slot],
                                        recv_sem=recv_sems.at[slot], device_id=to, device_id_type=MESH_DT)


def place_own_block(shard, landing, axis, me, name, tr=256):
    r, c = shard.shape
    tr = _row_tile(r, tr)
    nrb = r // tr
    if axis == 0:
        o_map = lambda i, me_ref: (me_ref[0] * nrb + i, 0)
    else:
        o_map = lambda i, me_ref: (i, me_ref[0])

    def body(me_ref, x_ref, land_ref, o_ref):
        o_ref[...] = x_ref[...]

    return pl.pallas_call(
        body, name=name,
        grid_spec=pltpu.PrefetchScalarGridSpec(
            num_scalar_prefetch=1, grid=(nrb,),
            in_specs=[pl.BlockSpec((tr, c), lambda i, me_ref: (i, 0)), pl.BlockSpec(memory_space=pl.ANY)],
            out_specs=pl.BlockSpec((tr, c), o_map)),
        out_shape=jax.ShapeDtypeStruct(landing.shape, landing.dtype), input_output_aliases={2: 0},
        compiler_params=_cparams(),
    )(me, shard, landing)


def gather_start(shards, axes, sizes, groups, name):
    n = len(shards)

    def body(*refs):
        srcs, lands, sems = refs[:n], refs[n:2 * n], refs[4 * n:]
        x, y, c, chips = _place()
        me = (x, y, c)
        targets = [(x, y, 1 - c)] + [(*chip, c) for chip in chips]
        for g, members in enumerate(groups):
            for m, t in enumerate(members):
                for k, to in enumerate(targets):
                    _level1_copy(srcs[t], lands[t], axes[t], sizes[t], sems[2 * g], sems[2 * g + 1], 4 * m + k,
                                 me, to).start()

    landing = [lax.empty(_full_shape(s, ax), s.dtype) for s, ax in zip(shards, axes)]
    out = pl.pallas_call(
        body, name=name,
        out_shape=[jax.ShapeDtypeStruct(b.shape, b.dtype) for b in shards + landing]
        + [pltpu.SemaphoreType.DMA((4 * len(members),)) for members in groups for _ in (0, 1)],
        in_specs=[_HBM_SPEC] * (2 * n), out_specs=[_HBM_SPEC] * (2 * n) + [_SEM_SPEC] * (2 * len(groups)),
        input_output_aliases={i: i for i in range(2 * n)},
        compiler_params=pltpu.CompilerParams(has_side_effects=_DATAFLOW),
    )(*shards, *landing)
    out = list(out)
    sems = out[2 * n:]
    return out[:n], out[n:2 * n], [(sems[2 * g], sems[2 * g + 1]) for g in range(len(groups))]


def gather_wait(shards, landing, axes, sizes, send_sems, recv_sems, after, name):
    n = len(landing)

    def body(*refs):
        srcs, lands = refs[:n], refs[n:2 * n]
        send, recv = refs[2 * n], refs[2 * n + 1]
        x, y, c, chips = _place()
        me = (x, y, c)
        peers = [(x, y, 1 - c)] + [(*chip, c) for chip in chips]
        for t in range(n):
            for k, peer in enumerate(peers):
                _level1_copy(srcs[t], lands[t], axes[t], sizes[t], send, recv, 4 * t + k, me, peer).wait_send()
                _level1_copy(srcs[t], lands[t], axes[t], sizes[t], send, recv, 4 * t + k, peer, me).wait_recv()

    out = pl.pallas_call(
        body, name=name, out_shape=[jax.ShapeDtypeStruct(b.shape, b.dtype) for b in shards + landing],
        in_specs=[_HBM_SPEC] * (2 * n) + [_SEM_SPEC, _SEM_SPEC, pl.BlockSpec(memory_space=pl.ANY)],
        out_specs=[_HBM_SPEC] * (2 * n), input_output_aliases={i: i for i in range(2 * n)},
        compiler_params=pltpu.CompilerParams(has_side_effects=_DATAFLOW),
    )(*shards, *landing, send_sems, recv_sems, after)
    return list(out)[n:]


def forward_task(landing, axes, sizes):
    n = len(landing)

    def forward(lands, send_sems, recv_sems, t, j, chip_core):
        x, y, c, _ = _place()
        blk = _block(lands[t], axes[t], sizes[t], _dev(*chip_core))
        return pltpu.make_async_remote_copy(src_ref=blk, dst_ref=blk, send_sem=send_sems.at[3 * t + j],
                                            recv_sem=recv_sems.at[3 * t + j], device_id=(x, y, 1 - c),
                                            device_id_type=MESH_DT)

    def start(ins, lands, send_sems, recv_sems, local_sems):
        _, _, c, chips = _place()
        for t in range(n):
            for j, chip in enumerate(chips):
                forward(lands, send_sems, recv_sems, t, j, (*chip, c)).start()

    def finish(ins, lands, send_sems, recv_sems, local_sems):
        _, _, c, chips = _place()
        for t in range(n):
            for j, chip in enumerate(chips):
                forward(lands, send_sems, recv_sems, t, j, (*chip, 1 - c)).wait_recv()
        for t in range(n):
            for j, chip in enumerate(chips):
                forward(lands, send_sems, recv_sems, t, j, (*chip, c)).wait_send()

    out_shape = [jax.ShapeDtypeStruct(b.shape, b.dtype) for b in landing]
    return CommTask(landing, out_shape, (3 * n, 3 * n, 1), start, _no_late, finish, in_place=True)


def rs_chip_sum(grad, recv, axis, core, name, tr=512):
    br, bc = _blk3(grad.shape, axis)
    tr = _row_tile(br, tr)
    nrb = br // tr

    if axis == 0:
        g_map = lambda i, r, c_ref: ((2 * i + c_ref[0]) * nrb + r, 0)
    else:
        g_map = lambda i, r, c_ref: (r, 2 * i + c_ref[0])

    def body(c_ref, g_ref, r_ref, o_ref):
        o_ref[...] = (g_ref[...].astype(F32) + r_ref[...].astype(F32)).astype(BF16)

    return pl.pallas_call(
        body, name=name,
        grid_spec=pltpu.PrefetchScalarGridSpec(
            num_scalar_prefetch=1, grid=(N_CHIP, nrb),
            in_specs=[pl.BlockSpec((tr, bc), g_map), pl.BlockSpec((None, tr, bc), lambda i, r, c_ref: (i, r, 0))],
            out_specs=pl.BlockSpec((None, tr, bc), lambda i, r, c_ref: (i, r, 0))),
        out_shape=jax.ShapeDtypeStruct((N_CHIP, br, bc), BF16), compiler_params=_cparams(),
    )(core, grad, recv)


def _adamw(w, g, m, v):
    m = ADAM_B1 * m + (1.0 - ADAM_B1) * g
    v = ADAM_B2 * v + (1.0 - ADAM_B2) * (g * g)
    m_hat = m / (1.0 - ADAM_B1 ** ADAM_STEP)
    v_hat = v / (1.0 - ADAM_B2 ** ADAM_STEP)
    delta = -ADAM_LR * (m_hat / (jnp.sqrt(v_hat) + ADAM_EPS) + ADAM_WD * w)
    return delta, m, v


def _sum_chips(p_ref):
    g = p_ref[0].astype(F32)
    for i in range(1, N_CHIP):
        g = g + p_ref[i].astype(F32)
    return g


def adam_sharded(parts, w, m, v, name, tr=256):
    r, c = w.shape
    assert parts.shape[2] == c
    tr = _row_tile(r, tr)

    def body(p_ref, w_ref, m_ref, v_ref, g_ref, d_ref, nm_ref, nv_ref):
        g = _sum_chips(p_ref)
        delta, nm, nv = _adamw(w_ref[...], g, m_ref[...], v_ref[...])
        g_ref[...] = g
        d_ref[...] = delta
        nm_ref[...] = nm
        nv_ref[...] = nv

    sp = pl.BlockSpec((tr, c), lambda i: (i, 0))
    return pl.pallas_call(
        body, name=name, grid=(r // tr,),
        in_specs=[pl.BlockSpec((N_CHIP, tr, c), lambda i: (0, i, 0)), sp, sp, sp],
        out_specs=[sp, sp, sp, sp], out_shape=[jax.ShapeDtypeStruct((r, c), F32)] * 4,
        compiler_params=_cparams(),
    )(parts, w, m, v)


def adam_small(items, name):
    n = len(items)

    def body(*refs):
        for i in range(n):
            g_ref, w_ref, m_ref, v_ref = refs[4 * i:4 * i + 4]
            d_ref, nm_ref, nv_ref = refs[4 * n + 3 * i:4 * n + 3 * i + 3]
            delta, nm, nv = _adamw(w_ref[...], g_ref[...], m_ref[...], v_ref[...])
            d_ref[...] = delta
            nm_ref[...] = nm
            nv_ref[...] = nv

    out = pl.pallas_call(
        body, name=name, out_shape=[jax.ShapeDtypeStruct(it[1].shape, F32) for it in items for _ in range(3)],
        compiler_params=_cparams(),
    )(*[a for it in items for a in it])
    return [tuple(out[3 * i:3 * i + 3]) for i in range(n)]


def sum_devices(gathered, name, tr=512):
    _, r, c = gathered.shape
    tr = _row_tile(r, tr)

    def body(x_ref, o_ref):
        s = x_ref[0]
        for k in range(1, N_DEV):
            s = s + x_ref[k]
        o_ref[...] = s

    return pl.pallas_call(
        body, name=name, grid=(r // tr,), in_specs=[pl.BlockSpec((N_DEV, tr, c), lambda i: (0, i, 0))],
        out_specs=pl.BlockSpec((tr, c), lambda i: (i, 0)), out_shape=jax.ShapeDtypeStruct((r, c), F32),
        compiler_params=_cparams(),
    )(gathered)


def _pad_to(a, axis, mult):
    size = a.shape[axis]
    pad = (-size) % mult
    if pad == 0:
        return a
    cfg = [(0, 0)] * a.ndim
    cfg[axis] = (0, pad)
    return jnp.pad(a, cfg)


def _as2d(a):
    if a.ndim == 1:
        return a.reshape(1, -1)
    return a.reshape(-1, a.shape[-1])


def kernel(x, p, norm_ffn1, w1_gate, w1_up, w1_down, norm_mix, w_in, ssm_log_dt, ssm_a_re, ssm_a_im, ssm_b_re, ssm_b_im, ssm_c_re, ssm_c_im, ssm_d, ssm_w_glu, gmlp_norm_v, gmlp_w_s, gmlp_b_s, norm_ssm_out, norm_gmlp_out, w_out, norm_ffn2, w2_gate, w2_up, w2_down, norm_ple, w_ple_gate, w_ple_proj, norm_final, loss_target, m_norm_ffn1, m_w1_gate, m_w1_up, m_w1_down, m_norm_mix, m_w_in, m_ssm_log_dt, m_ssm_a_re, m_ssm_a_im, m_ssm_b_re, m_ssm_b_im, m_ssm_c_re, m_ssm_c_im, m_ssm_d, m_ssm_w_glu, m_gmlp_norm_v, m_gmlp_w_s, m_gmlp_b_s, m_norm_ssm_out, m_norm_gmlp_out, m_w_out, m_norm_ffn2, m_w2_gate, m_w2_up, m_w2_down, m_norm_ple, m_w_ple_gate, m_w_ple_proj, m_norm_final, v_norm_ffn1, v_w1_gate, v_w1_up, v_w1_down, v_norm_mix, v_w_in, v_ssm_log_dt, v_ssm_a_re, v_ssm_a_im, v_ssm_b_re, v_ssm_b_im, v_ssm_c_re, v_ssm_c_im, v_ssm_d, v_ssm_w_glu, v_gmlp_norm_v, v_gmlp_w_s, v_gmlp_b_s, v_norm_ssm_out, v_norm_gmlp_out, v_w_out, v_norm_ffn2, v_w2_gate, v_w2_up, v_w2_down, v_norm_ple, v_w_ple_gate, v_w_ple_proj, v_norm_final):
    weights = dict(
        norm_ffn1=norm_ffn1, w1_gate=w1_gate, w1_up=w1_up, w1_down=w1_down, norm_mix=norm_mix, w_in=w_in,
        ssm_log_dt=ssm_log_dt, ssm_a_re=ssm_a_re, ssm_a_im=ssm_a_im, ssm_b_re=ssm_b_re, ssm_b_im=ssm_b_im,
        ssm_c_re=ssm_c_re, ssm_c_im=ssm_c_im, ssm_d=ssm_d, ssm_w_glu=ssm_w_glu, gmlp_norm_v=gmlp_norm_v,
        gmlp_w_s=gmlp_w_s, gmlp_b_s=gmlp_b_s, norm_ssm_out=norm_ssm_out, norm_gmlp_out=norm_gmlp_out, w_out=w_out,
        norm_ffn2=norm_ffn2, w2_gate=w2_gate, w2_up=w2_up, w2_down=w2_down, norm_ple=norm_ple,
        w_ple_gate=w_ple_gate, w_ple_proj=w_ple_proj, norm_final=norm_final)
    moments_m = dict(
        norm_ffn1=m_norm_ffn1, w1_gate=m_w1_gate, w1_up=m_w1_up, w1_down=m_w1_down, norm_mix=m_norm_mix, w_in=m_w_in,
        ssm_log_dt=m_ssm_log_dt, ssm_a_re=m_ssm_a_re, ssm_a_im=m_ssm_a_im, ssm_b_re=m_ssm_b_re, ssm_b_im=m_ssm_b_im,
        ssm_c_re=m_ssm_c_re, ssm_c_im=m_ssm_c_im, ssm_d=m_ssm_d, ssm_w_glu=m_ssm_w_glu, gmlp_norm_v=m_gmlp_norm_v,
        gmlp_w_s=m_gmlp_w_s, gmlp_b_s=m_gmlp_b_s, norm_ssm_out=m_norm_ssm_out, norm_gmlp_out=m_norm_gmlp_out,
        w_out=m_w_out, norm_ffn2=m_norm_ffn2, w2_gate=m_w2_gate, w2_up=m_w2_up, w2_down=m_w2_down,
        norm_ple=m_norm_ple, w_ple_gate=m_w_ple_gate, w_ple_proj=m_w_ple_proj, norm_final=m_norm_final)
    moments_v = dict(
        norm_ffn1=v_norm_ffn1, w1_gate=v_w1_gate, w1_up=v_w1_up, w1_down=v_w1_down, norm_mix=v_norm_mix, w_in=v_w_in,
        ssm_log_dt=v_ssm_log_dt, ssm_a_re=v_ssm_a_re, ssm_a_im=v_ssm_a_im, ssm_b_re=v_ssm_b_re, ssm_b_im=v_ssm_b_im,
        ssm_c_re=v_ssm_c_re, ssm_c_im=v_ssm_c_im, ssm_d=v_ssm_d, ssm_w_glu=v_ssm_w_glu, gmlp_norm_v=v_gmlp_norm_v,
        gmlp_w_s=v_gmlp_w_s, gmlp_b_s=v_gmlp_b_s, norm_ssm_out=v_norm_ssm_out, norm_gmlp_out=v_norm_gmlp_out,
        w_out=v_w_out, norm_ffn2=v_norm_ffn2, w2_gate=v_w2_gate, w2_up=v_w2_up, w2_down=v_w2_down,
        norm_ple=v_norm_ple, w_ple_gate=v_w_ple_gate, w_ple_proj=v_w_ple_proj, norm_final=v_norm_final)
    names = list(weights)

    xs = x[0]
    ps = p[0, 0].astype(BF16)
    tgt = loss_target[0]
    d_model = xs.shape[1]
    d_ssm = d_model // 2
    n_groups = d_ssm // SSM_GROUP

    transposed = ("w1_gate", "w1_up", "w2_gate", "w2_up")
    big = {
        "w1_gate": 0, "w1_up": 0, "w1_down": 0, "w_in": 1, "ssm_w_glu": 0, "w_out": 0,
        "w2_gate": 0, "w2_up": 0, "w2_down": 0, "w_ple_gate": 0, "w_ple_proj": 1}
    big_names = list(big)

    def view(a, k):
        return a[0].T if k in transposed else a[0]

    def unview(a, k):
        return a.T[None] if k in transposed else a[None]

    shard = {k: _pad_to(view(weights[k], k).astype(BF16), big[k], LANE) for k in big_names}
    W = {}

    abar_r, abar_i, bbar_r, bbar_i = _ssm_discretize(ssm_log_dt[0], ssm_a_re[0], ssm_a_im[0], ssm_b_re[0], ssm_b_im[0])
    bc_r = _block_diag(jnp.swapaxes(bbar_r, 1, 2)).astype(BF16)
    bc_i = _block_diag(jnp.swapaxes(bbar_i, 1, 2)).astype(BF16)
    cc_r = _block_diag(jnp.swapaxes(ssm_c_re[0], 1, 2)).astype(BF16)
    cc_i = _block_diag(jnp.swapaxes(ssm_c_im[0], 1, 2)).astype(BF16)
    apw_f = _scan_constants(abar_r, abar_i, False)
    apw_b = _scan_constants(abar_r, abar_i, True)
    causal = jnp.tril(jnp.ones((CHUNK, CHUNK), dtype=bool))
    wm = jnp.where(causal[None], gmlp_w_s[0], 0.0).astype(BF16)
    wmt = jnp.swapaxes(wm, 1, 2)
    bs = gmlp_b_s[0][:, :, None]

    groups = [["w1_gate"], ["w1_up"], ["w1_down"], ["w_in", "ssm_w_glu", "w_out"], ["w2_gate"], ["w2_up"],
              ["w2_down", "w_ple_gate", "w_ple_proj"]]
    order = [k for g in groups for k in g]
    place = {k: i for i, k in enumerate(order)}
    me = (4 * lax.axis_index("x") + 2 * lax.axis_index("y") + lax.axis_index("c")).astype(jnp.int32).reshape(1)
    size = {k: shard[k].shape[big[k]] for k in order}
    in_flight, landing, sems = gather_start([shard[k] for k in order], [big[k] for k in order],
                                            [size[k] for k in order], [[place[k] for k in g] for g in groups],
                                            "gather_start")
    landing = [place_own_block(in_flight[place[k]], landing[place[k]], big[k], me, "place_" + k) for k in order]

    def landed(g, after):
        members = [place[k] for k in groups[g]]
        axes_g, sizes_g = [big[k] for k in groups[g]], [size[k] for k in groups[g]]
        bufs = gather_wait([in_flight[i] for i in members], [landing[i] for i in members], axes_g, sizes_g,
                           *sems[g], after, "gather_wait_%d" % g)
        return forward_task(bufs, axes_g, sizes_g)

    def arrive(g, after):
        W.update(zip(groups[g], run_tasks([landed(g, after)], "gather_forward_%d" % g)[0]))

    def arrive_during(g, after, fn, *a, **kw):
        out, (got,) = fn(*a, tasks=[landed(g, after)], **kw)
        W.update(zip(groups[g], got))
        return out

    xn1 = rmsnorm_fwd(xs, norm_ffn1, "norm_ffn1")
    arrive(0, xn1)
    gate1 = matmul(xn1, W["w1_gate"], "nt", "ffn1_gate")
    arrive(1, gate1)
    gate1, up1, act1 = ffn_up(xn1, W["w1_up"], gate1, "ffn1_up")
    arrive(2, act1)
    h1 = matmul(act1, W["w1_down"], "nn", "ffn1_down", res=xs, scale=0.5)
    arrive(3, h1)
    xn2 = rmsnorm_fwd(h1, norm_mix, "norm_mix")
    z = matmul(xn2, W["w_in"], "nn", "proj_in")
    y_pre, yg, sr, si = s5_fwd(z, bc_r, bc_i, cc_r, cc_i, apw_f, ssm_d, "s5_fwd")
    glin = matmul(yg, W["ssm_w_glu"], "nn", "ssm_glu")
    y_gmlp = gmlp_fwd(z, gmlp_norm_v, wm, bs, "gmlp_fwd")
    ycat = mix_out_fwd(y_pre, glin, y_gmlp, norm_ssm_out, norm_gmlp_out, "mix_out")
    h2 = arrive_during(4, ycat, matmul, ycat, W["w_out"], "nn", "proj_out", res=h1)
    xn3 = rmsnorm_fwd(h2, norm_ffn2, "norm_ffn2")
    gate2 = arrive_during(5, xn3, matmul, xn3, W["w2_gate"], "nt", "ffn2_gate")
    gate2, up2, act2 = arrive_during(6, gate2, ffn_up, xn3, W["w2_up"], gate2, "ffn2_up")
    h3 = matmul(act2, W["w2_down"], "nn", "ffn2_down", res=h2, scale=0.5)
    xn4 = rmsnorm_fwd(h3, norm_ple, "norm_ple")
    pg_lin = matmul(xn4, W["w_ple_gate"], "nn", "ple_gate")
    pp = matmul(ps, W["w_ple_proj"], "nn", "ple_proj")
    h4 = ple_fwd(h3, pg_lin, pp, "ple_fwd")
    dh4, loss_part, g_norm_final = final_loss(h4, tgt, norm_final.reshape(1, -1), "final_loss")
    loss = lax.psum(loss_part[0, 0], ("x", "y", "c"))

    G = {}
    reduced = {}
    chip_part = {}
    wait_sibling, wait_chips = [], []
    core = lax.axis_index("c").astype(jnp.int32).reshape(1)

    def grad(name_, value):
        G[name_] = value
        wait_sibling.append(name_)

    def carry(fn, *a, levels="ab", extra=None, **kw):
        tasks, kinds = [], []
        if extra is not None:
            tasks.append(extra[0])
            kinds.append(("x", extra[1]))
        if "a" in levels and wait_sibling:
            group = list(wait_sibling)
            wait_sibling.clear()
            tasks.append(to_sibling_task([G[k] for k in group], [big[k] for k in group]))
            kinds.append(("a", group))
        if "b" in levels and wait_chips:
            group = list(wait_chips)
            wait_chips.clear()
            tasks.append(across_chips_task([chip_part[k] for k in group]))
            kinds.append(("b", group))
        if not tasks:
            return fn(*a, **kw)
        out, task_outs = fn(*a, tasks=tasks, **kw)
        for (kind, group), outs in zip(kinds, task_outs):
            if kind == "x":
                group(outs)
                continue
            for k, r in zip(group, outs):
                if kind == "a":
                    chip_part[k] = rs_chip_sum(G[k], r, big[k], core, "rs_sum_" + k)
                    wait_chips.append(k)
                else:
                    reduced[k] = r
        return out

    small = {}
    small["norm_final"] = g_norm_final
    dpp, dpg = ple_bwd(dh4, pg_lin, pp, "ple_bwd")
    grad("w_ple_proj", matmul(ps, dpp, "tn", "grad_ple_proj", out_dtype=BF16))
    grad("w_ple_gate", carry(matmul, xn4, dpg, "tn", "grad_ple_gate", out_dtype=BF16))
    dxn4 = carry(matmul, dpg, W["w_ple_gate"], "nt", "ple_gate_bwd")
    dh3, dh3b, small["norm_ple"] = rmsnorm_bwd(dxn4, h3, norm_ple, dh4, "norm_ple_bwd")

    def ffn_bwd(tag, dhb, xn, gate, up, act, wg, wu, wd, extra=None, last_levels="ab"):
        dgate, dup = carry(ffn_bwd_act, dhb, W[wd], gate, up, tag + "_act_bwd", extra=extra)
        grad(wd, carry(matmul, act, dhb, "tn", tag + "_grad_down", out_dtype=BF16, scale=0.5))
        grad(wg, carry(matmul, dgate, xn, "tn", tag + "_grad_gate", out_dtype=BF16))
        grad(wu, carry(matmul, dup, xn, "tn", tag + "_grad_up", out_dtype=BF16))
        dxn = carry(matmul, dgate, W[wg], "nn", tag + "_gate_bwd")
        return carry(matmul, dup, W[wu], "nn", tag + "_up_bwd", res=dxn, levels=last_levels)

    dxn3 = ffn_bwd("ffn2", dh3b, xn3, gate2, up2, act2, "w2_gate", "w2_up", "w2_down", last_levels="a")
    dh2, dh2b, small["norm_ffn2"] = rmsnorm_bwd(dxn3, h2, norm_ffn2, dh3, "norm_ffn2_bwd")

    grad("w_out", matmul(ycat, dh2b, "tn", "grad_out", out_dtype=BF16))
    dycat = carry(matmul, dh2b, W["w_out"], "nt", "proj_out_bwd", levels="a")
    dyg_direct, dglin, dy_gmlp, small["norm_ssm_out"], small["norm_gmlp_out"] = mix_out_bwd(
        dycat, y_pre, glin, y_gmlp, norm_ssm_out, norm_gmlp_out, "mix_out_bwd")
    grad("ssm_w_glu", matmul(yg, dglin, "tn", "grad_glu", out_dtype=BF16))
    dyg = carry(matmul, dglin, W["ssm_w_glu"], "nt", "ssm_glu_bwd", res=dyg_direct, levels="a")
    du, small["ssm_d"], gc_r, gc_i, gb_r, gb_i, ga_r, ga_i = carry(
        s5_bwd, dyg, y_pre, z, sr, si, bc_r, bc_i, cc_r, cc_i, apw_b, ssm_d, "s5_bwd")
    dzu, dzv, small["gmlp_norm_v"], g_wm, g_bs = gmlp_bwd(dy_gmlp, z, gmlp_norm_v, wm, wmt, bs, "gmlp_bwd")
    small["gmlp_w_s"] = g_wm
    small["gmlp_b_s"] = g_bs
    small["c_re"] = _block_diag_extract(gc_r, SSM_GROUP, SSM_STATE)
    small["c_im"] = _block_diag_extract(gc_i, SSM_GROUP, SSM_STATE)
    small["bbar_r"] = jnp.swapaxes(_block_diag_extract(gb_r, SSM_GROUP, SSM_STATE), 1, 2)
    small["bbar_i"] = jnp.swapaxes(_block_diag_extract(gb_i, SSM_GROUP, SSM_STATE), 1, 2)
    small["abar_r"] = jnp.sum(ga_r, axis=0).reshape(n_groups, SSM_STATE)
    small["abar_i"] = jnp.sum(ga_i, axis=0).reshape(n_groups, SSM_STATE)

    dz = jnp.concatenate([du, dzu, dzv], axis=1)
    grad("w_in", matmul(xn2, dz, "tn", "grad_in", out_dtype=BF16))
    dxn2 = carry(matmul, dz, W["w_in"], "nt", "proj_in_bwd")
    dh1, dh1b, small["norm_mix"] = rmsnorm_bwd(dxn2, h1, norm_mix, dh2, "norm_mix_bwd")

    def pack(parts):
        flat = jnp.concatenate([v.reshape(-1) for v in parts.values()])
        return _pad_to(flat, 0, SUBLANE * LANE).reshape(-1, LANE), flat.shape[0]

    def unpack(everyones, n, parts, tag):
        rows = everyones.shape[0] // N_DEV
        summed = sum_devices(everyones.reshape(N_DEV, rows, LANE), "sum_" + tag).reshape(-1)[:n]
        out, off = {}, 0
        for k, v in parts.items():
            out[k] = summed[off:off + v.size].reshape(v.shape)
            off += v.size
        return out

    early = dict(small)
    flat_early, n_early = pack(early)
    small_landed = []
    dxn1 = ffn_bwd("ffn1", dh1b, xn1, gate1, up1, act1, "w1_gate", "w1_up", "w1_down",
                   extra=(gather_task([flat_early], [0]), small_landed.extend))
    tot = unpack(small_landed[0], n_early, early, "small")
    grad_x, _, g_norm_ffn1 = rmsnorm_bwd(dxn1, xs, norm_ffn1, dh1, "norm_ffn1_bwd")
    assert not wait_sibling and not wait_chips and set(reduced) == set(big_names)
    last = {"norm_ffn1": g_norm_ffn1}
    flat_last, n_last = pack(last)
    ((everyones_last,),) = run_tasks([gather_task([flat_last], [0])], "gather_last")
    tot.update(unpack(everyones_last, n_last, last, "last"))

    out_g, out_d, out_m, out_v = {}, {}, {}, {}
    for k in big_names:
        g, dl, nm, nv = adam_sharded(reduced[k], view(weights[k], k), view(moments_m[k], k), view(moments_v[k], k),
                                     "adam_" + k)
        out_g[k], out_d[k], out_m[k], out_v[k] = unview(g, k), unview(dl, k), unview(nm, k), unview(nv, k)

    _, ssm_vjp = jax.vjp(_ssm_discretize, ssm_log_dt[0], ssm_a_re[0], ssm_a_im[0], ssm_b_re[0], ssm_b_im[0])
    g_log_dt, g_a_re, g_a_im, g_b_re, g_b_im = ssm_vjp((tot["abar_r"], tot["abar_i"], tot["bbar_r"], tot["bbar_i"]))
    small_grads = {
        "norm_ffn1": tot["norm_ffn1"], "norm_mix": tot["norm_mix"], "ssm_log_dt": g_log_dt, "ssm_a_re": g_a_re,
        "ssm_a_im": g_a_im, "ssm_b_re": g_b_re, "ssm_b_im": g_b_im, "ssm_c_re": tot["c_re"], "ssm_c_im": tot["c_im"],
        "ssm_d": tot["ssm_d"], "gmlp_norm_v": tot["gmlp_norm_v"],
        "gmlp_w_s": jnp.where(causal[None], tot["gmlp_w_s"], 0.0), "gmlp_b_s": tot["gmlp_b_s"],
        "norm_ssm_out": tot["norm_ssm_out"], "norm_gmlp_out": tot["norm_gmlp_out"], "norm_ffn2": tot["norm_ffn2"],
        "norm_ple": tot["norm_ple"], "norm_final": tot["norm_final"]}
    swapped = ("ssm_b_re", "ssm_b_im")

    def pre(k, a):
        return jnp.swapaxes(a, -1, -2) if k in swapped else a

    def update(group, name_):
        items = [(_as2d(pre(k, small_grads[k].reshape(weights[k].shape))), _as2d(pre(k, weights[k])),
                  _as2d(pre(k, moments_m[k])), _as2d(pre(k, moments_v[k]))) for k in group]
        for k, it, (dl, nm, nv) in zip(group, items, adam_small(items, name_)):
            shp = pre(k, weights[k]).shape
            out_g[k], out_d[k], out_m[k], out_v[k] = [pre(k, a.reshape(shp)) for a in (it[0], dl, nm, nv)]

    update([k for k in small_grads if k != "norm_ffn1"], "adam_replicated")
    update(["norm_ffn1"], "adam_norm_ffn1")

    return (loss, grad_x[None], *[out_g[k] for k in names], *[out_d[k] for k in names],
            *[out_m[k] for k in names], *[out_v[k] for k in names])
```

```python
import math

import jax
import jax.numpy as jnp
from jax import lax
from jax.experimental import pallas as pl
from jax.experimental.pallas import tpu as pltpu

F32 = jnp.float32
BF16 = jnp.bfloat16
MESH_DT = pl.DeviceIdType.MESH

N_DEV = 8
N_CHIP = 4
LANE = 128
SUBLANE = 8
VMEM_LIMIT = 60 * 1024 * 1024

EPS = 1e-6
SSM_GROUP = 16
SSM_STATE = 64
GROUPS_PER_BLOCK = LANE // SSM_GROUP
STATE_BLOCK = GROUPS_PER_BLOCK * SSM_STATE
GMLP_HEAD = 128
CHUNK = 128

ADAM_LR = 0.001
ADAM_B1 = 0.9
ADAM_B2 = 0.999
ADAM_EPS = 1e-08
ADAM_WD = 0.01
ADAM_STEP = 10

GELU_K = math.sqrt(2.0 / math.pi)
GELU_C = 0.044715


def _cparams():
    return pltpu.CompilerParams(vmem_limit_bytes=VMEM_LIMIT)


def _tile(n, pref):
    if n <= pref:
        return n
    t = (pref // LANE) * LANE
    while t > 0:
        if n % t == 0:
            return t
        t -= LANE
    return n


def _row_tile(n, pref):
    if n <= pref:
        return n
    t = (pref // SUBLANE) * SUBLANE
    while t > 0:
        if n % t == 0:
            return t
        t -= SUBLANE
    return n


def _gelu(x):
    t = jnp.tanh(GELU_K * (x + GELU_C * x * x * x))
    return 0.5 * x * (1.0 + t)


def _gelu_grad(x):
    t = jnp.tanh(GELU_K * (x + GELU_C * x * x * x))
    return 0.5 * (1.0 + t) + 0.5 * x * (1.0 - t * t) * (GELU_K * (1.0 + 3.0 * GELU_C * x * x))


def _sigmoid(x):
    return 0.5 * jnp.tanh(0.5 * x) + 0.5


_DN = {
    "nn": (((1,), (0,)), ((), ())),
    "nt": (((1,), (1,)), ((), ())),
    "tn": (((0,), (0,)), ((), ())),
}


def _dot(a, b, mode="nn"):
    return lax.dot_general(a, b, _DN[mode], preferred_element_type=F32)


class CommTask:
    def __init__(self, inputs, out_shape, n_sems, start, late, finish, in_place=False):
        self.inputs, self.out_shape, self.n_sems = list(inputs), list(out_shape), n_sems
        self.start, self.late, self.finish = start, late, finish
        self.in_place = in_place


def _task_aliases(tasks, first_in, first_out):
    aliases = {}
    for t in tasks:
        if t.in_place:
            aliases.update({first_in + i: first_out + i for i in range(len(t.inputs))})
        first_in += len(t.inputs)
        first_out += len(t.out_shape)
    return aliases


def _call(body, *, name, grid, in_specs, out_specs, out_shape, args, scratch_shapes=(), tasks=()):
    in_specs, out_specs, out_shape = list(in_specs), list(out_specs), list(out_shape)
    scratch_shapes = list(scratch_shapes)
    if not tasks:
        return pl.pallas_call(
            body, name=name, grid=grid, in_specs=in_specs, out_specs=out_specs, out_shape=out_shape,
            scratch_shapes=scratch_shapes, compiler_params=_cparams())(*args)
    n_in, n_out, n_scr = len(in_specs), len(out_specs), len(scratch_shapes)
    t_in = [len(t.inputs) for t in tasks]
    t_out = [len(t.out_shape) for t in tasks]
    late_step = grid[0] - max(1, grid[0] // 4)
    has_late = grid[0] >= 2

    def carried(*refs):
        pos = n_in
        task_ins = []
        for k in t_in:
            task_ins.append(refs[pos:pos + k])
            pos += k
        outs = refs[pos:pos + n_out]
        pos += n_out
        task_outs = []
        for k in t_out:
            task_outs.append(refs[pos:pos + k])
            pos += k
        scratch = refs[pos:pos + n_scr]
        pos += n_scr
        sems = [refs[pos + 3 * i:pos + 3 * i + 3] for i in range(len(tasks))]
        ids = [pl.program_id(d) for d in range(len(grid))]
        rest_zero = True
        for d in range(1, len(grid)):
            rest_zero = jnp.logical_and(rest_zero, ids[d] == 0)
        first = jnp.logical_and(ids[0] == 0, rest_zero)
        last = ids[0] == grid[0] - 1
        for d in range(1, len(grid)):
            last = jnp.logical_and(last, ids[d] == grid[d] - 1)

        @pl.when(first)
        def _():
            for t, ti, to, s in zip(tasks, task_ins, task_outs, sems):
                t.start(ti, to, *s)

        if has_late:
            @pl.when(jnp.logical_and(ids[0] == late_step, rest_zero))
            def _():
                for t, ti, to, s in zip(tasks, task_ins, task_outs, sems):
                    t.late(ti, to, *s)

        body(*refs[:n_in], *outs, *scratch)

        @pl.when(last)
        def _():
            for t, ti, to, s in zip(tasks, task_ins, task_outs, sems):
                if not has_late:
                    t.late(ti, to, *s)
                t.finish(ti, to, *s)

    any_spec = pl.BlockSpec(memory_space=pl.ANY)
    sem_shapes = [pltpu.SemaphoreType.DMA((n,)) for t in tasks for n in t.n_sems]
    res = pl.pallas_call(
        carried, name=name, grid=grid,
        in_specs=in_specs + [any_spec] * sum(t_in), out_specs=out_specs + [any_spec] * sum(t_out),
        out_shape=out_shape + [s for t in tasks for s in t.out_shape],
        input_output_aliases=_task_aliases(tasks, n_in, n_out),
        scratch_shapes=scratch_shapes + sem_shapes, compiler_params=_cparams(),
    )(*args, *[a for t in tasks for a in t.inputs])
    res = list(res)
    task_res, pos = [], n_out
    for k in t_out:
        task_res.append(res[pos:pos + k])
        pos += k
    return res[:n_out], task_res


def _mm_dims(a, b, mode):
    if mode == "nn":
        (m, k), (k2, n) = a.shape, b.shape
    elif mode == "nt":
        (m, k), (n, k2) = a.shape, b.shape
    else:
        (k, m), (k2, n) = a.shape, b.shape
    assert k == k2, (a.shape, b.shape, mode)
    return m, n, k


def _mm_specs(mode, tm, tn, tk):
    if mode == "tn":
        a_spec = pl.BlockSpec((tk, tm), lambda i, j, k: (k, i))
    else:
        a_spec = pl.BlockSpec((tm, tk), lambda i, j, k: (i, k))
    if mode == "nt":
        b_spec = pl.BlockSpec((tn, tk), lambda i, j, k: (j, k))
    else:
        b_spec = pl.BlockSpec((tk, tn), lambda i, j, k: (k, j))
    return a_spec, b_spec


def _accumulate(acc, nk, partial, emit):
    if nk == 1:
        emit(partial)
        return
    kk = pl.program_id(2)

    @pl.when(kk == 0)
    def _():
        acc[...] = partial

    @pl.when(kk > 0)
    def _():
        acc[...] += partial

    @pl.when(kk == nk - 1)
    def _():
        emit(acc[...])


def matmul(a, b, mode, name, out_dtype=F32, res=None, scale=1.0, tm=1024, tn=1024, tk=2048, tasks=()):
    m, n, k = _mm_dims(a, b, mode)
    if mode == "tn":
        tk = max(tk, 4096)
    elif k > tk:
        tk, tn = k, (tn if k <= 3072 else tn // 2)
    tm, tn, tk = _tile(m, tm), _tile(n, tn), _tile(k, tk)
    nk = k // tk
    a_spec, b_spec = _mm_specs(mode, tm, tn, tk)
    o_spec = pl.BlockSpec((tm, tn), lambda i, j, k: (i, j))
    has_res = res is not None

    def body(*refs):
        if has_res:
            a_ref, b_ref, r_ref, o_ref, acc = refs
        else:
            a_ref, b_ref, o_ref, acc = refs

        def emit(v):
            if scale != 1.0:
                v = v * scale
            if has_res:
                v = r_ref[...] + v
            o_ref[...] = v.astype(out_dtype)

        _accumulate(acc, nk, _dot(a_ref[...], b_ref[...], mode), emit)

    out = _call(
        body, name=name, grid=(m // tm, n // tn, nk),
        in_specs=[a_spec, b_spec] + ([o_spec] if has_res else []), out_specs=[o_spec],
        out_shape=[jax.ShapeDtypeStruct((m, n), out_dtype)], args=(a, b) + ((res,) if has_res else ()),
        scratch_shapes=[pltpu.VMEM((tm, tn) if nk > 1 else (SUBLANE, LANE), F32)], tasks=tasks)
    return (out[0][0], out[1]) if tasks else out[0]


def ffn_up(xn, wu, gate, name, tm=1024, tn=1024, tk=2048, tasks=()):
    m, n, k = _mm_dims(xn, wu, "nt")
    tm, tn, tk = _tile(m, tm), _tile(n, tn), _tile(k, tk)
    nk = k // tk
    a_spec, b_spec = _mm_specs("nt", tm, tn, tk)
    o_spec = pl.BlockSpec((tm, tn), lambda i, j, k: (i, j))

    def body(a_ref, u_ref, gate_ref, gate_b_ref, up_b_ref, act_ref, acc):
        def emit(u):
            g = gate_ref[...]
            gate_b_ref[...] = g.astype(BF16)
            up_b_ref[...] = u.astype(BF16)
            act_ref[...] = (g * _sigmoid(g) * u).astype(BF16)

        _accumulate(acc, nk, _dot(a_ref[...], u_ref[...], "nt"), emit)

    out = _call(
        body, name=name, grid=(m // tm, n // tn, nk), in_specs=[a_spec, b_spec, o_spec],
        out_specs=[o_spec, o_spec, o_spec],
        out_shape=[jax.ShapeDtypeStruct((m, n), BF16), jax.ShapeDtypeStruct((m, n), BF16),
                   jax.ShapeDtypeStruct((m, n), BF16)],
        args=(xn, wu, gate), scratch_shapes=[pltpu.VMEM((tm, tn) if nk > 1 else (SUBLANE, LANE), F32)], tasks=tasks)
    return (tuple(out[0]), out[1]) if tasks else tuple(out)


def ffn_gate_up(xn, wg, wu, name, tm=1024, tn=1024, tasks=()):
    m, n, k = _mm_dims(xn, wg, "nt")
    tm, tn = _tile(m, tm), _tile(n, tn)
    a_spec, b_spec = _mm_specs("nt", tm, tn, k)
    o_spec = pl.BlockSpec((tm, tn), lambda i, j, k: (i, j))

    def body(a_ref, g_ref, u_ref, gate_b_ref, up_b_ref, act_ref):
        a = a_ref[...]
        g = _dot(a, g_ref[...], "nt")
        u = _dot(a, u_ref[...], "nt")
        gate_b_ref[...] = g.astype(BF16)
        up_b_ref[...] = u.astype(BF16)
        act_ref[...] = (g * _sigmoid(g) * u).astype(BF16)

    out = _call(
        body, name=name, grid=(m // tm, n // tn, 1), in_specs=[a_spec, b_spec, b_spec],
        out_specs=[o_spec, o_spec, o_spec], out_shape=[jax.ShapeDtypeStruct((m, n), BF16)] * 3,
        args=(xn, wg, wu), tasks=tasks)
    return (tuple(out[0]), out[1]) if tasks else tuple(out)


def ffn_bwd_act(dh, wd, gate, up, name, tm=1024, tn=1024, tk=2048, tasks=()):
    m, n, k = _mm_dims(dh, wd, "nt")
    tm, tn, tk = _tile(m, tm), _tile(n, tn), _tile(k, tk)
    nk = k // tk
    a_spec, b_spec = _mm_specs("nt", tm, tn, tk)
    o_spec = pl.BlockSpec((tm, tn), lambda i, j, k: (i, j))

    def body(a_ref, b_ref, gate_ref, up_ref, dg_ref, du_ref, acc):
        def emit(total):
            dact = 0.5 * total
            g = gate_ref[...].astype(F32)
            sg = _sigmoid(g)
            du_ref[...] = (dact * (g * sg)).astype(BF16)
            dg_ref[...] = (dact * up_ref[...].astype(F32) * (sg * (1.0 + g * (1.0 - sg)))).astype(BF16)

        _accumulate(acc, nk, _dot(a_ref[...], b_ref[...], "nt"), emit)

    out = _call(
        body, name=name, grid=(m // tm, n // tn, nk), in_specs=[a_spec, b_spec, o_spec, o_spec],
        out_specs=[o_spec, o_spec],
        out_shape=[jax.ShapeDtypeStruct((m, n), BF16), jax.ShapeDtypeStruct((m, n), BF16)],
        args=(dh, wd, gate, up), scratch_shapes=[pltpu.VMEM((tm, tn) if nk > 1 else (SUBLANE, LANE), F32)],
        tasks=tasks)
    return (tuple(out[0]), out[1]) if tasks else tuple(out)


def _rows(t, d, tr):
    return pl.BlockSpec((tr, d), lambda i: (i, 0))


def _vec(d):
    return pl.BlockSpec((1, d), lambda i: (0, 0))


def rmsnorm_fwd(x, g, name, tr=512):
    t, d = x.shape
    tr = _row_tile(t, tr)

    def body(x_ref, g_ref, o_ref):
        xf = x_ref[...]
        r = lax.rsqrt(jnp.mean(xf * xf, axis=-1, keepdims=True) + EPS)
        o_ref[...] = (xf * r * g_ref[...]).astype(BF16)

    return pl.pallas_call(
        body, name=name, grid=(t // tr,), in_specs=[_rows(t, d, tr), _vec(d)], out_specs=_rows(t, d, tr),
        out_shape=jax.ShapeDtypeStruct((t, d), BF16), compiler_params=_cparams(),
    )(x, g)


def _rms_bwd(dxn, xf, g):
    r = lax.rsqrt(jnp.mean(xf * xf, axis=-1, keepdims=True) + EPS)
    xhat = xf * r
    dg = jnp.sum(dxn * xhat, axis=0, keepdims=True)
    dxh = dxn * g
    dx = r * (dxh - xhat * jnp.mean(dxh * xhat, axis=-1, keepdims=True))
    return dx, dg


def rmsnorm_bwd(dxn, x, g, dres, name, tr=256):
    t, d = x.shape
    tr = _row_tile(t, tr)

    def body(dxn_ref, x_ref, g_ref, dres_ref, o_ref, ob_ref, dg_ref):
        dx, dg = _rms_bwd(dxn_ref[...], x_ref[...], g_ref[...])
        out = dres_ref[...] + dx
        o_ref[...] = out
        ob_ref[...] = out.astype(BF16)

        @pl.when(pl.program_id(0) == 0)
        def _():
            dg_ref[...] = jnp.zeros_like(dg_ref)

        dg_ref[...] += dg

    return pl.pallas_call(
        body, name=name, grid=(t // tr,),
        in_specs=[_rows(t, d, tr), _rows(t, d, tr), _vec(d), _rows(t, d, tr)],
        out_specs=[_rows(t, d, tr), _rows(t, d, tr), _vec(d)],
        out_shape=[jax.ShapeDtypeStruct((t, d), F32), jax.ShapeDtypeStruct((t, d), BF16),
                   jax.ShapeDtypeStruct((1, d), F32)],
        compiler_params=_cparams(),
    )(dxn, x, g, dres)


def final_loss(h, target, g, name, tr=256):
    t, d = h.shape
    tr = _row_tile(t, tr)

    def body(h_ref, t_ref, g_ref, dh_ref, loss_ref, dg_ref):
        xf = h_ref[...]
        gg = g_ref[...]
        r = lax.rsqrt(jnp.mean(xf * xf, axis=-1, keepdims=True) + EPS)
        xhat = xf * r
        e = xhat * gg - t_ref[...]
        part = jnp.sum(jnp.sum(e * e, axis=1, keepdims=True), axis=0, keepdims=True) * (0.5 / d)
        dout = e * (1.0 / d)
        dg = jnp.sum(dout * xhat, axis=0, keepdims=True)
        dxh = dout * gg
        dh_ref[...] = r * (dxh - xhat * jnp.mean(dxh * xhat, axis=-1, keepdims=True))

        @pl.when(pl.program_id(0) == 0)
        def _():
            dg_ref[...] = jnp.zeros_like(dg_ref)
            loss_ref[...] = jnp.zeros_like(loss_ref)

        dg_ref[...] += dg
        loss_ref[...] += jnp.broadcast_to(part, loss_ref.shape)

    return pl.pallas_call(
        body, name=name, grid=(t // tr,),
        in_specs=[_rows(t, d, tr), _rows(t, d, tr), _vec(d)],
        out_specs=[_rows(t, d, tr), pl.BlockSpec((SUBLANE, LANE), lambda i: (0, 0)), _vec(d)],
        out_shape=[jax.ShapeDtypeStruct((t, d), F32), jax.ShapeDtypeStruct((SUBLANE, LANE), F32),
                   jax.ShapeDtypeStruct((1, d), F32)],
        compiler_params=_cparams(),
    )(h, target, g)


def ple_fwd(h, glin, pp, name, tr=512):
    t, d = h.shape
    tr = _row_tile(t, tr)

    def body(h_ref, gl_ref, pp_ref, o_ref):
        o_ref[...] = h_ref[...] + _sigmoid(gl_ref[...]) * pp_ref[...]

    sp = _rows(t, d, tr)
    return pl.pallas_call(
        body, name=name, grid=(t // tr,), in_specs=[sp, sp, sp], out_specs=sp,
        out_shape=jax.ShapeDtypeStruct((t, d), F32), compiler_params=_cparams(),
    )(h, glin, pp)


def ple_bwd(dh, glin, pp, name, tr=512):
    t, d = dh.shape
    tr = _row_tile(t, tr)

    def body(dh_ref, gl_ref, pp_ref, dpp_ref, dgl_ref):
        gate = _sigmoid(gl_ref[...])
        dh_ = dh_ref[...]
        dpp_ref[...] = (dh_ * gate).astype(BF16)
        dgl_ref[...] = (dh_ * pp_ref[...] * gate * (1.0 - gate)).astype(BF16)

    sp = _rows(t, d, tr)
    return pl.pallas_call(
        body, name=name, grid=(t // tr,), in_specs=[sp, sp, sp], out_specs=[sp, sp],
        out_shape=[jax.ShapeDtypeStruct((t, d), BF16), jax.ShapeDtypeStruct((t, d), BF16)],
        compiler_params=_cparams(),
    )(dh, glin, pp)


def mix_out_fwd(y_pre, glin, y_gmlp, g_so, g_go, name, tr=512):
    t, d = y_pre.shape
    tr = _row_tile(t, tr)

    def body(yp_ref, gl_ref, yg_ref, gs_ref, gg_ref, o_ref):
        ys = _gelu(yp_ref[...]) * _sigmoid(gl_ref[...])
        r = lax.rsqrt(jnp.mean(ys * ys, axis=-1, keepdims=True) + EPS)
        o_ref[:, 0:d] = (ys * r * gs_ref[...]).astype(BF16)
        yq = yg_ref[...]
        r2 = lax.rsqrt(jnp.mean(yq * yq, axis=-1, keepdims=True) + EPS)
        o_ref[:, d:2 * d] = (yq * r2 * gg_ref[...]).astype(BF16)

    sp = _rows(t, d, tr)
    return pl.pallas_call(
        body, name=name, grid=(t // tr,), in_specs=[sp, sp, sp, _vec(d), _vec(d)],
        out_specs=_rows(t, 2 * d, tr), out_shape=jax.ShapeDtypeStruct((t, 2 * d), BF16),
        compiler_params=_cparams(),
    )(y_pre, glin, y_gmlp, g_so, g_go)


def mix_out_bwd(dycat, y_pre, glin, y_gmlp, g_so, g_go, name, tr=256):
    t, d = y_pre.shape
    tr = _row_tile(t, tr)

    def body(dy_ref, yp_ref, gl_ref, yg_ref, gs_ref, gg_ref, dyg_ref, dl_ref, dyq_ref, dgs_ref, dgg_ref):
        yg = _gelu(yp_ref[...])
        sg = _sigmoid(gl_ref[...])
        dys, dgs = _rms_bwd(dy_ref[:, 0:d], yg * sg, gs_ref[...])
        dyg_ref[...] = dys * sg
        dl_ref[...] = (dys * yg * sg * (1.0 - sg)).astype(BF16)
        dyq, dgg = _rms_bwd(dy_ref[:, d:2 * d], yg_ref[...], gg_ref[...])
        dyq_ref[...] = dyq

        @pl.when(pl.program_id(0) == 0)
        def _():
            dgs_ref[...] = jnp.zeros_like(dgs_ref)
            dgg_ref[...] = jnp.zeros_like(dgg_ref)

        dgs_ref[...] += dgs
        dgg_ref[...] += dgg

    sp = _rows(t, d, tr)
    return pl.pallas_call(
        body, name=name, grid=(t // tr,),
        in_specs=[_rows(t, 2 * d, tr), sp, sp, sp, _vec(d), _vec(d)],
        out_specs=[sp, sp, sp, _vec(d), _vec(d)],
        out_shape=[jax.ShapeDtypeStruct((t, d), F32), jax.ShapeDtypeStruct((t, d), BF16),
                   jax.ShapeDtypeStruct((t, d), F32), jax.ShapeDtypeStruct((1, d), F32),
                   jax.ShapeDtypeStruct((1, d), F32)],
        compiler_params=_cparams(),
    )(dycat, y_pre, glin, y_gmlp, g_so, g_go)


SCAN_COLS = 512


def _scan_tile(xr, xi, const, cr, ci, reverse):
    for lvl, sh in enumerate((1, 2, 4)):
        ar, ai = const(2 * lvl), const(2 * lvl + 1)
        s = (SUBLANE - sh) if reverse else sh
        rr = pltpu.roll(xr, s, 0)
        ri = pltpu.roll(xi, s, 0)
        xr, xi = xr + ar * rr - ai * ri, xi + ar * ri + ai * rr
    pr, pi_ = const(6), const(7)
    xr, xi = xr + pr * cr - pi_ * ci, xi + pr * ci + pi_ * cr
    return xr, xi


def _bcast_row(x, row):
    return jnp.broadcast_to(x[row:row + 1, :], x.shape)


def s5_fwd(z, bc_r, bc_i, cc_r, cc_i, apw, dvec, name, tc=512, tasks=()):
    t = z.shape[0]
    nblk = bc_r.shape[0]
    d = nblk * LANE
    ns = nblk * STATE_BLOCK
    tc = _row_tile(t, tc)
    ntile = tc // SUBLANE

    def body(z_ref, br_ref, bi_ref, cr_ref, ci_ref, apw_ref, d_ref, y_ref, yg_ref, sr_ref, si_ref, carry):
        @pl.when(pl.program_id(0) == 0)
        def _():
            carry[...] = jnp.zeros_like(carry)

        for j in range(nblk):
            uj = z_ref[:, j * LANE:(j + 1) * LANE]
            ub = uj.astype(BF16)
            for q in range(STATE_BLOCK // SCAN_COLS):
                c0 = j * STATE_BLOCK + q * SCAN_COLS
                cs = pl.ds(c0, SCAN_COLS)
                bs = slice(q * SCAN_COLS, (q + 1) * SCAN_COLS)
                sr_ref[:, cs] = _dot(ub, br_ref[j, :, bs])
                si_ref[:, cs] = _dot(ub, bi_ref[j, :, bs])
                const = lambda k, cs=cs: apw_ref[k, :, cs]

                def tile(k, c, cs=cs, const=const):
                    rows = pl.ds(pl.multiple_of(k * SUBLANE, SUBLANE), SUBLANE)
                    xr, xi = _scan_tile(sr_ref[rows, cs], si_ref[rows, cs], const, c[0], c[1], False)
                    sr_ref[rows, cs] = xr
                    si_ref[rows, cs] = xi
                    return _bcast_row(xr, SUBLANE - 1), _bcast_row(xi, SUBLANE - 1)

                c_r, c_i = lax.fori_loop(0, ntile, tile, (carry[0, :, cs], carry[1, :, cs]))
                carry[0, :, cs] = c_r
                carry[1, :, cs] = c_i
            sb = pl.ds(j * STATE_BLOCK, STATE_BLOCK)
            y = (_dot(sr_ref[:, sb].astype(BF16), cr_ref[j]) - _dot(si_ref[:, sb].astype(BF16), ci_ref[j])
                 + d_ref[:, j * LANE:(j + 1) * LANE] * uj)
            y_ref[:, j * LANE:(j + 1) * LANE] = y
            yg_ref[:, j * LANE:(j + 1) * LANE] = _gelu(y).astype(BF16)

    full3 = lambda shp: pl.BlockSpec(shp, lambda i: (0, 0, 0))
    out = _call(
        body, name=name, grid=(t // tc,),
        in_specs=[pl.BlockSpec((tc, d), lambda i: (i, 0)), full3(bc_r.shape), full3(bc_i.shape),
                  full3(cc_r.shape), full3(cc_i.shape), full3(apw.shape), _vec(d)],
        out_specs=[pl.BlockSpec((tc, d), lambda i: (i, 0)), pl.BlockSpec((tc, d), lambda i: (i, 0)),
                   pl.BlockSpec((tc, ns), lambda i: (i, 0)), pl.BlockSpec((tc, ns), lambda i: (i, 0))],
        out_shape=[jax.ShapeDtypeStruct((t, d), F32), jax.ShapeDtypeStruct((t, d), BF16),
                   jax.ShapeDtypeStruct((t, ns), F32), jax.ShapeDtypeStruct((t, ns), F32)],
        args=(z, bc_r, bc_i, cc_r, cc_i, apw, dvec), scratch_shapes=[pltpu.VMEM((2, SUBLANE, ns), F32)], tasks=tasks)
    return (tuple(out[0]), out[1]) if tasks else tuple(out)


def s5_bwd(dyg, y_pre, z, sr, si, bc_r, bc_i, cc_r, cc_i, apw_rev, dvec, name, tc=256, tasks=()):
    t = z.shape[0]
    nblk = bc_r.shape[0]
    d = nblk * LANE
    ns = nblk * STATE_BLOCK
    tc = _row_tile(t, tc)
    ntile = tc // SUBLANE
    nchunk = t // tc
    tiles_per_chunk = tc // SUBLANE

    def body(dyg_ref, yp_ref, z_ref, sr_ref, si_ref, pr_ref, pi_ref, br_ref, bi_ref, cr_ref, ci_ref, apw_ref,
             d_ref, du_ref, gd_ref, gcr_ref, gci_ref, gbr_ref, gbi_ref, gar_ref, gai_ref, lr_ref, li_ref, carry):
        step = pl.program_id(0)

        @pl.when(step == 0)
        def _():
            carry[...] = jnp.zeros_like(carry)
            for ref in (gd_ref, gcr_ref, gci_ref, gbr_ref, gbi_ref, gar_ref, gai_ref):
                ref[...] = jnp.zeros_like(ref)

        first_chunk = (step == nchunk - 1).astype(F32)
        keep_prev = 1.0 - first_chunk
        row0 = lax.broadcasted_iota(jnp.int32, (SUBLANE, SCAN_COLS), 0) == 0

        for j in range(nblk):
            lanes = slice(j * LANE, (j + 1) * LANE)
            uj = z_ref[:, lanes]
            ub = uj.astype(BF16)
            gy = dyg_ref[:, lanes] * _gelu_grad(yp_ref[:, lanes])
            gyb = gy.astype(BF16)
            gd_ref[:, lanes] += jnp.sum(gy * uj, axis=0, keepdims=True)
            for q in range(STATE_BLOCK // SCAN_COLS):
                c0 = j * STATE_BLOCK + q * SCAN_COLS
                cs = pl.ds(c0, SCAN_COLS)
                bs = slice(q * SCAN_COLS, (q + 1) * SCAN_COLS)
                lr_ref[:, cs] = _dot(gyb, cr_ref[j, bs, :], "nt")
                li_ref[:, cs] = -_dot(gyb, ci_ref[j, bs, :], "nt")
                const = lambda k, cs=cs: apw_ref[k, :, cs]

                def one_tile(rows, prev_r, prev_i, c, cs=cs, const=const):
                    cr_, ci_, gar, gai = c
                    xr, xi = _scan_tile(lr_ref[rows, cs], li_ref[rows, cs], const, cr_, ci_, True)
                    lr_ref[rows, cs] = xr
                    li_ref[rows, cs] = xi
                    spr = jnp.where(row0, prev_r, pltpu.roll(sr_ref[rows, cs], 1, 0))
                    spi = jnp.where(row0, prev_i, pltpu.roll(si_ref[rows, cs], 1, 0))
                    gar = gar + xr * spr + xi * spi
                    gai = gai + xi * spr - xr * spi
                    return _bcast_row(xr, 0), _bcast_row(xi, 0), gar, gai

                def tile(k, c, cs=cs, one_tile=one_tile):
                    kk = ntile - 1 - k
                    rows = pl.ds(pl.multiple_of(kk * SUBLANE, SUBLANE), SUBLANE)
                    prow = pl.ds(pl.multiple_of((kk - 1) * SUBLANE, SUBLANE), SUBLANE)
                    prev_r = _bcast_row(sr_ref[prow, cs], SUBLANE - 1)
                    prev_i = _bcast_row(si_ref[prow, cs], SUBLANE - 1)
                    return one_tile(rows, prev_r, prev_i, c)

                zero = jnp.zeros((SUBLANE, SCAN_COLS), F32)
                c = lax.fori_loop(0, ntile - 1, tile, (carry[0, :, cs], carry[1, :, cs], zero, zero))
                prev_r = _bcast_row(pr_ref[:, cs], SUBLANE - 1) * keep_prev
                prev_i = _bcast_row(pi_ref[:, cs], SUBLANE - 1) * keep_prev
                c_r, c_i, gar, gai = one_tile(pl.ds(0, SUBLANE), prev_r, prev_i, c)
                carry[0, :, cs] = c_r
                carry[1, :, cs] = c_i
                gar_ref[:, cs] += gar
                gai_ref[:, cs] += gai
            sb = pl.ds(j * STATE_BLOCK, STATE_BLOCK)
            lrb = lr_ref[:, sb].astype(BF16)
            lib = li_ref[:, sb].astype(BF16)
            gcr_ref[j] += _dot(gyb, sr_ref[:, sb].astype(BF16), "tn")
            gci_ref[j] -= _dot(gyb, si_ref[:, sb].astype(BF16), "tn")
            gbr_ref[j] += _dot(ub, lrb, "tn")
            gbi_ref[j] += _dot(ub, lib, "tn")
            du = _dot(lrb, br_ref[j], "nt") + _dot(lib, bi_ref[j], "nt") + gy * d_ref[:, lanes]
            du_ref[:, lanes] = du.astype(BF16)

    rev = lambda i: (nchunk - 1 - i, 0)
    prev = lambda i: (jnp.maximum((nchunk - 1 - i) * tiles_per_chunk - 1, 0), 0)
    full3 = lambda shp: pl.BlockSpec(shp, lambda i: (0, 0, 0))
    acc3 = pl.BlockSpec((nblk, LANE, STATE_BLOCK), lambda i: (0, 0, 0))
    acc_rows = pl.BlockSpec((SUBLANE, ns), lambda i: (0, 0))
    out = _call(
        body, name=name, grid=(nchunk,),
        in_specs=[pl.BlockSpec((tc, d), rev), pl.BlockSpec((tc, d), rev), pl.BlockSpec((tc, d), rev),
                  pl.BlockSpec((tc, ns), rev), pl.BlockSpec((tc, ns), rev),
                  pl.BlockSpec((SUBLANE, ns), prev), pl.BlockSpec((SUBLANE, ns), prev),
                  full3(bc_r.shape), full3(bc_i.shape), full3(cc_r.shape), full3(cc_i.shape), full3(apw_rev.shape),
                  _vec(d)],
        out_specs=[pl.BlockSpec((tc, d), rev), _vec(d), acc3, acc3, acc3, acc3, acc_rows, acc_rows],
        out_shape=[jax.ShapeDtypeStruct((t, d), BF16), jax.ShapeDtypeStruct((1, d), F32)]
        + [jax.ShapeDtypeStruct((nblk, LANE, STATE_BLOCK), F32)] * 4
        + [jax.ShapeDtypeStruct((SUBLANE, ns), F32)] * 2,
        args=(dyg, y_pre, z, sr, si, sr, si, bc_r, bc_i, cc_r, cc_i, apw_rev, dvec),
        scratch_shapes=[pltpu.VMEM((tc, ns), F32), pltpu.VMEM((tc, ns), F32), pltpu.VMEM((2, SUBLANE, ns), F32)],
        tasks=tasks)
    return (tuple(out[0]), out[1]) if tasks else tuple(out)


def _cmul(a, b):
    return a[0] * b[0] - a[1] * b[1], a[0] * b[1] + a[1] * b[0]


def _scan_constants(abar_r, abar_i, reverse):
    ar = abar_r.reshape(1, -1)
    ai = abar_i.reshape(1, -1)
    if reverse:
        ai = -ai
    pw = [(ar, ai)]
    for _ in range(SUBLANE - 1):
        pw.append(_cmul(pw[-1], (ar, ai)))
    rows = lax.broadcasted_iota(jnp.int32, (SUBLANE, 1), 0)
    out = []
    for sh in (1, 2, 4):
        keep = (rows <= SUBLANE - 1 - sh) if reverse else (rows >= sh)
        for part in pw[sh - 1]:
            out.append(jnp.where(keep, part, 0.0))
    for comp in (0, 1):
        stack = jnp.concatenate([pw[k][comp] for k in range(SUBLANE)], axis=0)
        out.append(stack[::-1] if reverse else stack)
    return jnp.stack(out, axis=0).astype(F32)


def _ssm_discretize(log_dt, a_re, a_im, b_re, b_im):
    dt = jnp.exp(log_dt)[:, None]
    lr = jnp.minimum(a_re, -1e-4)
    li = a_im
    mag = jnp.exp(lr * dt)
    ang = li * dt
    abar_r = mag * jnp.cos(ang)
    abar_i = mag * jnp.sin(ang)
    den = lr * lr + li * li
    xr = abar_r - 1.0
    xi = abar_i
    zr = (xr * lr + xi * li) / den
    zi = (xi * lr - xr * li) / den
    bbar_r = zr[..., None] * b_re - zi[..., None] * b_im
    bbar_i = zr[..., None] * b_im + zi[..., None] * b_re
    return abar_r, abar_i, bbar_r, bbar_i


def _block_diag(w):
    g, a, b = w.shape
    nb = g // GROUPS_PER_BLOCK
    eye = jnp.eye(GROUPS_PER_BLOCK, dtype=w.dtype)
    w5 = w.reshape(nb, GROUPS_PER_BLOCK, a, b)
    out = w5[:, :, :, None, :] * eye[None, :, None, :, None]
    return out.reshape(nb, GROUPS_PER_BLOCK * a, GROUPS_PER_BLOCK * b)


def _block_diag_extract(m, a, b):
    nb = m.shape[0]
    eye = jnp.eye(GROUPS_PER_BLOCK, dtype=m.dtype)
    m5 = m.reshape(nb, GROUPS_PER_BLOCK, a, GROUPS_PER_BLOCK, b)
    out = jnp.sum(m5 * eye[None, :, None, :, None], axis=3)
    return out.reshape(nb * GROUPS_PER_BLOCK, a, b)


def _layer_norm(gv, nv):
    mu = jnp.mean(gv, axis=-1, keepdims=True)
    xc = gv - mu
    r = lax.rsqrt(jnp.mean(xc * xc, axis=-1, keepdims=True) + EPS)
    xhat = xc * r
    return xhat * nv, xhat, r


def gmlp_fwd(z, norm_v, wm, bs, name, tr=256):
    t = z.shape[0]
    nh = wm.shape[0]
    d = nh * GMLP_HEAD
    col0 = (z.shape[1] - 2 * d) // d
    tr = _row_tile(t, tr)

    def body(zu_ref, zv_ref, nv_ref, wm_ref, bs_ref, o_ref):
        v, _, _ = _layer_norm(_gelu(zv_ref[...]), nv_ref[...])
        vb = v.astype(BF16)
        u = _gelu(zu_ref[...])
        for c in range(tr // CHUNK):
            rows = slice(c * CHUNK, (c + 1) * CHUNK)
            for h in range(nh):
                cols = slice(h * GMLP_HEAD, (h + 1) * GMLP_HEAD)
                s = _dot(wm_ref[h], vb[rows, cols]) + bs_ref[h]
                o_ref[rows, cols] = u[rows, cols] * s

    return pl.pallas_call(
        body, name=name, grid=(t // tr,),
        in_specs=[pl.BlockSpec((tr, d), lambda i: (i, col0)), pl.BlockSpec((tr, d), lambda i: (i, col0 + 1)),
                  _vec(d), pl.BlockSpec(wm.shape, lambda i: (0, 0, 0)), pl.BlockSpec(bs.shape, lambda i: (0, 0, 0))],
        out_specs=pl.BlockSpec((tr, d), lambda i: (i, 0)),
        out_shape=jax.ShapeDtypeStruct((t, d), F32), compiler_params=_cparams(),
    )(z, z, norm_v, wm, bs)


def gmlp_bwd(dy, z, norm_v, wm, wmt, bs, name, tr=256):
    t = z.shape[0]
    nh = wm.shape[0]
    d = nh * GMLP_HEAD
    col0 = (z.shape[1] - 2 * d) // d
    tr = _row_tile(t, tr)

    def body(dy_ref, zu_ref, zv_ref, nv_ref, wm_ref, wmt_ref, bs_ref, dzu_ref, dzv_ref, dnv_ref, dwm_ref, dbs_ref,
             dv_ref):
        @pl.when(pl.program_id(0) == 0)
        def _():
            dnv_ref[...] = jnp.zeros_like(dnv_ref)
            dwm_ref[...] = jnp.zeros_like(dwm_ref)
            dbs_ref[...] = jnp.zeros_like(dbs_ref)

        zv = zv_ref[...]
        nv = nv_ref[...]
        v, xhat, r = _layer_norm(_gelu(zv), nv)
        vb = v.astype(BF16)
        zu = zu_ref[...]
        u = _gelu(zu)
        dy_ = dy_ref[...]
        for c in range(tr // CHUNK):
            rows = slice(c * CHUNK, (c + 1) * CHUNK)
            for h in range(nh):
                cols = slice(h * GMLP_HEAD, (h + 1) * GMLP_HEAD)
                vh = vb[rows, cols]
                s = _dot(wm_ref[h], vh) + bs_ref[h]
                dyh = dy_[rows, cols]
                dzu_ref[rows, cols] = (dyh * s * _gelu_grad(zu[rows, cols])).astype(BF16)
                ds = dyh * u[rows, cols]
                dsb = ds.astype(BF16)
                dbs_ref[h] += jnp.sum(ds, axis=1, keepdims=True)
                dwm_ref[h] += _dot(dsb, vh, "nt")
                dv_ref[rows, cols] = _dot(wmt_ref[h], dsb)
        dv = dv_ref[...]
        dnv_ref[...] += jnp.sum(dv * xhat, axis=0, keepdims=True)
        dxh = dv * nv
        dgv = r * (dxh - jnp.mean(dxh, axis=-1, keepdims=True) - xhat * jnp.mean(dxh * xhat, axis=-1, keepdims=True))
        dzv_ref[...] = (dgv * _gelu_grad(zv)).astype(BF16)

    full3 = lambda shp: pl.BlockSpec(shp, lambda i: (0, 0, 0))
    rows_d = pl.BlockSpec((tr, d), lambda i: (i, 0))
    return pl.pallas_call(
        body, name=name, grid=(t // tr,),
        in_specs=[rows_d, pl.BlockSpec((tr, d), lambda i: (i, col0)), pl.BlockSpec((tr, d), lambda i: (i, col0 + 1)),
                  _vec(d), full3(wm.shape), full3(wmt.shape), full3(bs.shape)],
        out_specs=[rows_d, rows_d, _vec(d), full3((nh, CHUNK, CHUNK)), full3((nh, CHUNK, 1))],
        out_shape=[jax.ShapeDtypeStruct((t, d), BF16), jax.ShapeDtypeStruct((t, d), BF16),
                   jax.ShapeDtypeStruct((1, d), F32), jax.ShapeDtypeStruct((nh, CHUNK, CHUNK), F32),
                   jax.ShapeDtypeStruct((nh, CHUNK, 1), F32)],
        scratch_shapes=[pltpu.VMEM((tr, d), F32)], compiler_params=_cparams(),
    )(dy, z, z, norm_v, wm, wmt, bs)


def _block(ref, axis, size, k):
    start = pl.multiple_of(k * size, size)
    if axis == 0:
        return ref.at[pl.ds(start, size), :]
    return ref.at[:, pl.ds(start, size)]


def _place():
    x, y, c = lax.axis_index("x"), lax.axis_index("y"), lax.axis_index("c")
    chips = [(1 - x, y), (x, 1 - y), (1 - x, 1 - y)]
    return x, y, c, chips


def _dev(x, y, c):
    return 4 * x + 2 * y + c


def gather_task(shards, axes):
    n = len(shards)
    sizes = [s.shape[ax] for s, ax in zip(shards, axes)]
    out_shape = [
        jax.ShapeDtypeStruct((s.shape[0] * N_DEV, s.shape[1]) if ax == 0 else (s.shape[0], s.shape[1] * N_DEV), s.dtype)
        for s, ax in zip(shards, axes)
    ]

    def copy(ins, outs, send_sems, recv_sems, t, k, block, to, from_input=False):
        dst = _block(outs[t], axes[t], sizes[t], _dev(*block))
        return pltpu.make_async_remote_copy(
            src_ref=ins[t] if from_input else dst, dst_ref=dst,
            send_sem=send_sems.at[t * 7 + k], recv_sem=recv_sems.at[t * 7 + k],
            device_id=to, device_id_type=MESH_DT)

    def local(ins, outs, local_sems, t, me):
        return pltpu.make_async_copy(ins[t], _block(outs[t], axes[t], sizes[t], _dev(*me)), local_sems.at[t])

    def start(ins, outs, send_sems, recv_sems, local_sems):
        x, y, c, chips = _place()
        me, sibling = (x, y, c), (x, y, 1 - c)
        for t in range(n):
            local(ins, outs, local_sems, t, me).start()
        for t in range(n):
            copy(ins, outs, send_sems, recv_sems, t, 0, me, sibling, True).start()
            for j, chip in enumerate(chips):
                copy(ins, outs, send_sems, recv_sems, t, 1 + j, me, (*chip, c), True).start()

    def late(ins, outs, send_sems, recv_sems, local_sems):
        x, y, c, chips = _place()
        me, sibling = (x, y, c), (x, y, 1 - c)
        for t in range(n):
            for j, chip in enumerate(chips):
                copy(ins, outs, send_sems, recv_sems, t, 1 + j, (*chip, c), me).wait_recv()
                copy(ins, outs, send_sems, recv_sems, t, 4 + j, (*chip, c), sibling).start()

    def finish(ins, outs, send_sems, recv_sems, local_sems):
        x, y, c, chips = _place()
        me, sibling = (x, y, c), (x, y, 1 - c)
        for t in range(n):
            copy(ins, outs, send_sems, recv_sems, t, 0, sibling, me).wait_recv()
            for j, chip in enumerate(chips):
                copy(ins, outs, send_sems, recv_sems, t, 4 + j, (*chip, 1 - c), me).wait_recv()
        for t in range(n):
            copy(ins, outs, send_sems, recv_sems, t, 0, me, sibling, True).wait_send()
            for j, chip in enumerate(chips):
                copy(ins, outs, send_sems, recv_sems, t, 1 + j, me, (*chip, c), True).wait_send()
                copy(ins, outs, send_sems, recv_sems, t, 4 + j, (*chip, c), sibling).wait_send()
            local(ins, outs, local_sems, t, me).wait()

    return CommTask(shards, out_shape, (7 * n, 7 * n, n), start, late, finish)


def _blk3(shape2, axis):
    r, c = shape2
    return (r // N_DEV, c) if axis == 0 else (r, c // N_DEV)


def _no_late(ins, outs, send_sems, recv_sems, local_sems):
    pass


def to_sibling_task(grads, axes):
    n = len(grads)
    blks = [_blk3(g.shape, ax) for g, ax in zip(grads, axes)]
    sizes = [b[ax] for b, ax in zip(blks, axes)]

    def copies(ins, outs, send_sems, recv_sems):
        x, y, c, _ = _place()
        return [pltpu.make_async_remote_copy(
            src_ref=_block(ins[t], axes[t], sizes[t], 2 * i + (1 - c)), dst_ref=outs[t].at[i],
            send_sem=send_sems.at[t * N_CHIP + i], recv_sem=recv_sems.at[t * N_CHIP + i],
            device_id=(x, y, 1 - c), device_id_type=MESH_DT) for t in range(n) for i in range(N_CHIP)]

    def start(ins, outs, send_sems, recv_sems, local_sems):
        for cp in copies(ins, outs, send_sems, recv_sems):
            cp.start()

    def finish(ins, outs, send_sems, recv_sems, local_sems):
        cps = copies(ins, outs, send_sems, recv_sems)
        for cp in cps:
            cp.wait_recv()
        for cp in cps:
            cp.wait_send()

    out_shape = [jax.ShapeDtypeStruct((N_CHIP,) + b, g.dtype) for b, g in zip(blks, grads)]
    return CommTask(grads, out_shape, (N_CHIP * n, N_CHIP * n, 1), start, _no_late, finish)


def across_chips_task(parts):
    n = len(parts)

    def copies(ins, outs, send_sems, recv_sems):
        x, y, c, chips = _place()
        my_chip = 2 * x + y
        return [pltpu.make_async_remote_copy(
            src_ref=ins[t].at[2 * chip[0] + chip[1]], dst_ref=outs[t].at[my_chip],
            send_sem=send_sems.at[t * 3 + j], recv_sem=recv_sems.at[t * 3 + j],
            device_id=(*chip, c), device_id_type=MESH_DT) for t in range(n) for j, chip in enumerate(chips)]

    def mine(ins, outs, local_sems):
        x, y, _, _ = _place()
        my_chip = 2 * x + y
        return [pltpu.make_async_copy(ins[t].at[my_chip], outs[t].at[my_chip], local_sems.at[t]) for t in range(n)]

    def start(ins, outs, send_sems, recv_sems, local_sems):
        for cp in mine(ins, outs, local_sems):
            cp.start()
        for cp in copies(ins, outs, send_sems, recv_sems):
            cp.start()

    def finish(ins, outs, send_sems, recv_sems, local_sems):
        cps = copies(ins, outs, send_sems, recv_sems)
        for cp in cps:
            cp.wait_recv()
        for cp in cps:
            cp.wait_send()
        for cp in mine(ins, outs, local_sems):
            cp.wait()

    out_shape = [jax.ShapeDtypeStruct(p.shape, p.dtype) for p in parts]
    return CommTask(parts, out_shape, (3 * n, 3 * n, n), start, _no_late, finish)


def run_tasks(tasks, name):
    t_in = [len(t.inputs) for t in tasks]
    t_out = [len(t.out_shape) for t in tasks]

    def body(*refs):
        pos, views = 0, []
        for k in t_in:
            views.append([refs[pos:pos + k]])
            pos += k
        for v, k in zip(views, t_out):
            v.append(refs[pos:pos + k])
            pos += k
        for i, v in enumerate(views):
            v.extend(refs[pos + 3 * i:pos + 3 * i + 3])
        for phase in ("start", "late", "finish"):
            for t, v in zip(tasks, views):
                getattr(t, phase)(*v)

    any_spec = pl.BlockSpec(memory_space=pl.ANY)
    res = pl.pallas_call(
        body, name=name, in_specs=[any_spec] * sum(t_in), out_specs=[any_spec] * sum(t_out),
        out_shape=[s for t in tasks for s in t.out_shape], input_output_aliases=_task_aliases(tasks, 0, 0),
        scratch_shapes=[pltpu.SemaphoreType.DMA((k,)) for t in tasks for k in t.n_sems],
    )(*[a for t in tasks for a in t.inputs])
    res, out, pos = list(res), [], 0
    for k in t_out:
        out.append(res[pos:pos + k])
        pos += k
    return out


_HBM_SPEC = pl.BlockSpec(memory_space=pl.ANY)
_SEM_SPEC = pl.BlockSpec(memory_space=pltpu.SEMAPHORE)
_DATAFLOW = pltpu.SideEffectType.DATAFLOW_SIDE_EFFECTING


def _full_shape(s, ax):
    return (s.shape[0] * N_DEV, s.shape[1]) if ax == 0 else (s.shape[0], s.shape[1] * N_DEV)


def _level1_copy(src, landing, axis, size, send_sems, recv_sems, slot, sender, to):
    dst = _block(landing, axis, size, _dev(*sender))
    return pltpu.make_async_remote_copy(src_ref=src, dst_ref=dst, send_sem=send_sems.at[slot],
                                        recv_sem=recv_sems.at[slot], device_id=to, device_id_type=MESH_DT)


def place_own_block(shard, landing, axis, me, name, tr=256):
    r, c = shard.shape
    tr = _row_tile(r, tr)
    nrb = r // tr
    if axis == 0:
        o_map = lambda i, me_ref: (me_ref[0] * nrb + i, 0)
    else:
        o_map = lambda i, me_ref: (i, me_ref[0])

    def body(me_ref, x_ref, land_ref, o_ref):
        o_ref[...] = x_ref[...]

    return pl.pallas_call(
        body, name=name,
        grid_spec=pltpu.PrefetchScalarGridSpec(
            num_scalar_prefetch=1, grid=(nrb,),
            in_specs=[pl.BlockSpec((tr, c), lambda i, me_ref: (i, 0)), pl.BlockSpec(memory_space=pl.ANY)],
            out_specs=pl.BlockSpec((tr, c), o_map)),
        out_shape=jax.ShapeDtypeStruct(landing.shape, landing.dtype), input_output_aliases={2: 0},
        compiler_params=_cparams(),
    )(me, shard, landing)


def gather_start(shards, axes, sizes, groups, name):
    n = len(shards)

    def body(*refs):
        srcs, lands, sems = refs[:n], refs[n:2 * n], refs[4 * n:]
        x, y, c, chips = _place()
        me = (x, y, c)
        targets = [(x, y, 1 - c)] + [(*chip, c) for chip in chips]
        for g, members in enumerate(groups):
            for m, t in enumerate(members):
                for k, to in enumerate(targets):
                    _level1_copy(srcs[t], lands[t], axes[t], sizes[t], sems[2 * g], sems[2 * g + 1], 4 * m + k,
                                 me, to).start()

    landing = [lax.empty(_full_shape(s, ax), s.dtype) for s, ax in zip(shards, axes)]
    out = pl.pallas_call(
        body, name=name,
        out_shape=[jax.ShapeDtypeStruct(b.shape, b.dtype) for b in shards + landing]
        + [pltpu.SemaphoreType.DMA((4 * len(members),)) for members in groups for _ in (0, 1)],
        in_specs=[_HBM_SPEC] * (2 * n), out_specs=[_HBM_SPEC] * (2 * n) + [_SEM_SPEC] * (2 * len(groups)),
        input_output_aliases={i: i for i in range(2 * n)},
        compiler_params=pltpu.CompilerParams(has_side_effects=_DATAFLOW),
    )(*shards, *landing)
    out = list(out)
    sems = out[2 * n:]
    return out[:n], out[n:2 * n], [(sems[2 * g], sems[2 * g + 1]) for g in range(len(groups))]


def gather_wait(shards, landing, axes, sizes, send_sems, recv_sems, after, name):
    n = len(landing)

    def body(*refs):
        srcs, lands = refs[:n], refs[n:2 * n]
        send, recv = refs[2 * n], refs[2 * n + 1]
        x, y, c, chips = _place()
        me = (x, y, c)
        peers = [(x, y, 1 - c)] + [(*chip, c) for chip in chips]
        for t in range(n):
            for k, peer in enumerate(peers):
                _level1_copy(srcs[t], lands[t], axes[t], sizes[t], send, recv, 4 * t + k, me, peer).wait_send()
                _level1_copy(srcs[t], lands[t], axes[t], sizes[t], send, recv, 4 * t + k, peer, me).wait_recv()

    out = pl.pallas_call(
        body, name=name, out_shape=[jax.ShapeDtypeStruct(b.shape, b.dtype) for b in shards + landing],
        in_specs=[_HBM_SPEC] * (2 * n) + [_SEM_SPEC, _SEM_SPEC, pl.BlockSpec(memory_space=pl.ANY)],
        out_specs=[_HBM_SPEC] * (2 * n), input_output_aliases={i: i for i in range(2 * n)},
        compiler_params=pltpu.CompilerParams(has_side_effects=_DATAFLOW),
    )(*shards, *landing, send_sems, recv_sems, after)
    return list(out)[n:]


def forward_task(landing, axes, sizes):
    n = len(landing)

    def forward(lands, send_sems, recv_sems, t, j, chip_core):
        x, y, c, _ = _place()
        blk = _block(lands[t], axes[t], sizes[t], _dev(*chip_core))
        return pltpu.make_async_remote_copy(src_ref=blk, dst_ref=blk, send_sem=send_sems.at[3 * t + j],
                                            recv_sem=recv_sems.at[3 * t + j], device_id=(x, y, 1 - c),
                                            device_id_type=MESH_DT)

    def start(ins, lands, send_sems, recv_sems, local_sems):
        _, _, c, chips = _place()
        for t in range(n):
            for j, chip in enumerate(chips):
                forward(lands, send_sems, recv_sems, t, j, (*chip, c)).start()

    def finish(ins, lands, send_sems, recv_sems, local_sems):
        _, _, c, chips = _place()
        for t in range(n):
            for j, chip in enumerate(chips):
                forward(lands, send_sems, recv_sems, t, j, (*chip, 1 - c)).wait_recv()
        for t in range(n):
            for j, chip in enumerate(chips):
                forward(lands, send_sems, recv_sems, t, j, (*chip, c)).wait_send()

    out_shape = [jax.ShapeDtypeStruct(b.shape, b.dtype) for b in landing]
    return CommTask(landing, out_shape, (3 * n, 3 * n, 1), start, _no_late, finish, in_place=True)


def rs_chip_sum(grad, recv, axis, core, name, tr=512):
    br, bc = _blk3(grad.shape, axis)
    tr = _row_tile(br, tr)
    nrb = br // tr

    if axis == 0:
        g_map = lambda i, r, c_ref: ((2 * i + c_ref[0]) * nrb + r, 0)
    else:
        g_map = lambda i, r, c_ref: (r, 2 * i + c_ref[0])

    def body(c_ref, g_ref, r_ref, o_ref):
        o_ref[...] = (g_ref[...].astype(F32) + r_ref[...].astype(F32)).astype(BF16)

    return pl.pallas_call(
        body, name=name,
        grid_spec=pltpu.PrefetchScalarGridSpec(
            num_scalar_prefetch=1, grid=(N_CHIP, nrb),
            in_specs=[pl.BlockSpec((tr, bc), g_map), pl.BlockSpec((None, tr, bc), lambda i, r, c_ref: (i, r, 0))],
            out_specs=pl.BlockSpec((None, tr, bc), lambda i, r, c_ref: (i, r, 0))),
        out_shape=jax.ShapeDtypeStruct((N_CHIP, br, bc), BF16), compiler_params=_cparams(),
    )(core, grad, recv)


def _adamw(w, g, m, v):
    m = ADAM_B1 * m + (1.0 - ADAM_B1) * g
    v = ADAM_B2 * v + (1.0 - ADAM_B2) * (g * g)
    m_hat = m / (1.0 - ADAM_B1 ** ADAM_STEP)
    v_hat = v / (1.0 - ADAM_B2 ** ADAM_STEP)
    delta = -ADAM_LR * (m_hat / (jnp.sqrt(v_hat) + ADAM_EPS) + ADAM_WD * w)
    return delta, m, v


def _sum_chips(p_ref):
    g = p_ref[0].astype(F32)
    for i in range(1, N_CHIP):
        g = g + p_ref[i].astype(F32)
    return g


def adam_sharded(parts, w, m, v, name, tr=256):
    r, c = w.shape
    assert parts.shape[2] == c
    tr = _row_tile(r, tr)

    def body(p_ref, w_ref, m_ref, v_ref, g_ref, d_ref, nm_ref, nv_ref):
        g = _sum_chips(p_ref)
        delta, nm, nv = _adamw(w_ref[...], g, m_ref[...], v_ref[...])
        g_ref[...] = g
        d_ref[...] = delta
        nm_ref[...] = nm
        nv_ref[...] = nv

    sp = pl.BlockSpec((tr, c), lambda i: (i, 0))
    return pl.pallas_call(
        body, name=name, grid=(r // tr,),
        in_specs=[pl.BlockSpec((N_CHIP, tr, c), lambda i: (0, i, 0)), sp, sp, sp],
        out_specs=[sp, sp, sp, sp], out_shape=[jax.ShapeDtypeStruct((r, c), F32)] * 4,
        compiler_params=_cparams(),
    )(parts, w, m, v)


def adam_small(items, name):
    n = len(items)

    def body(*refs):
        for i in range(n):
            g_ref, w_ref, m_ref, v_ref = refs[4 * i:4 * i + 4]
            d_ref, nm_ref, nv_ref = refs[4 * n + 3 * i:4 * n + 3 * i + 3]
            delta, nm, nv = _adamw(w_ref[...], g_ref[...], m_ref[...], v_ref[...])
            d_ref[...] = delta
            nm_ref[...] = nm
            nv_ref[...] = nv

    out = pl.pallas_call(
        body, name=name, out_shape=[jax.ShapeDtypeStruct(it[1].shape, F32) for it in items for _ in range(3)],
        compiler_params=_cparams(),
    )(*[a for it in items for a in it])
    return [tuple(out[3 * i:3 * i + 3]) for i in range(n)]


def sum_devices(gathered, name, tr=512):
    _, r, c = gathered.shape
    tr = _row_tile(r, tr)

    def body(x_ref, o_ref):
        s = x_ref[0]
        for k in range(1, N_DEV):
            s = s + x_ref[k]
        o_ref[...] = s

    return pl.pallas_call(
        body, name=name, grid=(r // tr,), in_specs=[pl.BlockSpec((N_DEV, tr, c), lambda i: (0, i, 0))],
        out_specs=pl.BlockSpec((tr, c), lambda i: (i, 0)), out_shape=jax.ShapeDtypeStruct((r, c), F32),
        compiler_params=_cparams(),
    )(gathered)


def _pad_to(a, axis, mult):
    size = a.shape[axis]
    pad = (-size) % mult
    if pad == 0:
        return a
    cfg = [(0, 0)] * a.ndim
    cfg[axis] = (0, pad)
    return jnp.pad(a, cfg)


def _as2d(a):
    if a.ndim == 1:
        return a.reshape(1, -1)
    return a.reshape(-1, a.shape[-1])


def kernel(x, p, norm_ffn1, w1_gate, w1_up, w1_down, norm_mix, w_in, ssm_log_dt, ssm_a_re, ssm_a_im, ssm_b_re, ssm_b_im, ssm_c_re, ssm_c_im, ssm_d, ssm_w_glu, gmlp_norm_v, gmlp_w_s, gmlp_b_s, norm_ssm_out, norm_gmlp_out, w_out, norm_ffn2, w2_gate, w2_up, w2_down, norm_ple, w_ple_gate, w_ple_proj, norm_final, loss_target, m_norm_ffn1, m_w1_gate, m_w1_up, m_w1_down, m_norm_mix, m_w_in, m_ssm_log_dt, m_ssm_a_re, m_ssm_a_im, m_ssm_b_re, m_ssm_b_im, m_ssm_c_re, m_ssm_c_im, m_ssm_d, m_ssm_w_glu, m_gmlp_norm_v, m_gmlp_w_s, m_gmlp_b_s, m_norm_ssm_out, m_norm_gmlp_out, m_w_out, m_norm_ffn2, m_w2_gate, m_w2_up, m_w2_down, m_norm_ple, m_w_ple_gate, m_w_ple_proj, m_norm_final, v_norm_ffn1, v_w1_gate, v_w1_up, v_w1_down, v_norm_mix, v_w_in, v_ssm_log_dt, v_ssm_a_re, v_ssm_a_im, v_ssm_b_re, v_ssm_b_im, v_ssm_c_re, v_ssm_c_im, v_ssm_d, v_ssm_w_glu, v_gmlp_norm_v, v_gmlp_w_s, v_gmlp_b_s, v_norm_ssm_out, v_norm_gmlp_out, v_w_out, v_norm_ffn2, v_w2_gate, v_w2_up, v_w2_down, v_norm_ple, v_w_ple_gate, v_w_ple_proj, v_norm_final):
    weights = dict(
        norm_ffn1=norm_ffn1, w1_gate=w1_gate, w1_up=w1_up, w1_down=w1_down, norm_mix=norm_mix, w_in=w_in,
        ssm_log_dt=ssm_log_dt, ssm_a_re=ssm_a_re, ssm_a_im=ssm_a_im, ssm_b_re=ssm_b_re, ssm_b_im=ssm_b_im,
        ssm_c_re=ssm_c_re, ssm_c_im=ssm_c_im, ssm_d=ssm_d, ssm_w_glu=ssm_w_glu, gmlp_norm_v=gmlp_norm_v,
        gmlp_w_s=gmlp_w_s, gmlp_b_s=gmlp_b_s, norm_ssm_out=norm_ssm_out, norm_gmlp_out=norm_gmlp_out, w_out=w_out,
        norm_ffn2=norm_ffn2, w2_gate=w2_gate, w2_up=w2_up, w2_down=w2_down, norm_ple=norm_ple,
        w_ple_gate=w_ple_gate, w_ple_proj=w_ple_proj, norm_final=norm_final)
    moments_m = dict(
        norm_ffn1=m_norm_ffn1, w1_gate=m_w1_gate, w1_up=m_w1_up, w1_down=m_w1_down, norm_mix=m_norm_mix, w_in=m_w_in,
        ssm_log_dt=m_ssm_log_dt, ssm_a_re=m_ssm_a_re, ssm_a_im=m_ssm_a_im, ssm_b_re=m_ssm_b_re, ssm_b_im=m_ssm_b_im,
        ssm_c_re=m_ssm_c_re, ssm_c_im=m_ssm_c_im, ssm_d=m_ssm_d, ssm_w_glu=m_ssm_w_glu, gmlp_norm_v=m_gmlp_norm_v,
        gmlp_w_s=m_gmlp_w_s, gmlp_b_s=m_gmlp_b_s, norm_ssm_out=m_norm_ssm_out, norm_gmlp_out=m_norm_gmlp_out,
        w_out=m_w_out, norm_ffn2=m_norm_ffn2, w2_gate=m_w2_gate, w2_up=m_w2_up, w2_down=m_w2_down,
        norm_ple=m_norm_ple, w_ple_gate=m_w_ple_gate, w_ple_proj=m_w_ple_proj, norm_final=m_norm_final)
    moments_v = dict(
        norm_ffn1=v_norm_ffn1, w1_gate=v_w1_gate, w1_up=v_w1_up, w1_down=v_w1_down, norm_mix=v_norm_mix, w_in=v_w_in,
        ssm_log_dt=v_ssm_log_dt, ssm_a_re=v_ssm_a_re, ssm_a_im=v_ssm_a_im, ssm_b_re=v_ssm_b_re, ssm_b_im=v_ssm_b_im,
        ssm_c_re=v_ssm_c_re, ssm_c_im=v_ssm_c_im, ssm_d=v_ssm_d, ssm_w_glu=v_ssm_w_glu, gmlp_norm_v=v_gmlp_norm_v,
        gmlp_w_s=v_gmlp_w_s, gmlp_b_s=v_gmlp_b_s, norm_ssm_out=v_norm_ssm_out, norm_gmlp_out=v_norm_gmlp_out,
        w_out=v_w_out, norm_ffn2=v_norm_ffn2, w2_gate=v_w2_gate, w2_up=v_w2_up, w2_down=v_w2_down,
        norm_ple=v_norm_ple, w_ple_gate=v_w_ple_gate, w_ple_proj=v_w_ple_proj, norm_final=v_norm_final)
    names = list(weights)

    xs = x[0]
    ps = p[0, 0].astype(BF16)
    tgt = loss_target[0]
    d_model = xs.shape[1]
    d_ssm = d_model // 2
    n_groups = d_ssm // SSM_GROUP

    transposed = ("w1_gate", "w1_up", "w2_gate", "w2_up")
    big = {
        "w1_gate": 0, "w1_up": 0, "w1_down": 0, "w_in": 1, "ssm_w_glu": 0, "w_out": 0,
        "w2_gate": 0, "w2_up": 0, "w2_down": 0, "w_ple_gate": 0, "w_ple_proj": 1}
    big_names = list(big)

    def view(a, k):
        return a[0].T if k in transposed else a[0]

    def unview(a, k):
        return a.T[None] if k in transposed else a[None]

    shard = {k: _pad_to(view(weights[k], k).astype(BF16), big[k], LANE) for k in big_names}
    W = {}

    abar_r, abar_i, bbar_r, bbar_i = _ssm_discretize(ssm_log_dt[0], ssm_a_re[0], ssm_a_im[0], ssm_b_re[0], ssm_b_im[0])
    bc_r = _block_diag(jnp.swapaxes(bbar_r, 1, 2)).astype(BF16)
    bc_i = _block_diag(jnp.swapaxes(bbar_i, 1, 2)).astype(BF16)
    cc_r = _block_diag(jnp.swapaxes(ssm_c_re[0], 1, 2)).astype(BF16)
    cc_i = _block_diag(jnp.swapaxes(ssm_c_im[0], 1, 2)).astype(BF16)
    apw_f = _scan_constants(abar_r, abar_i, False)
    apw_b = _scan_constants(abar_r, abar_i, True)
    causal = jnp.tril(jnp.ones((CHUNK, CHUNK), dtype=bool))
    wm = jnp.where(causal[None], gmlp_w_s[0], 0.0).astype(BF16)
    wmt = jnp.swapaxes(wm, 1, 2)
    bs = gmlp_b_s[0][:, :, None]

    groups = [["w1_gate"], ["w1_up"], ["w1_down"], ["w_in", "ssm_w_glu", "w_out"], ["w2_gate", "w2_up"],
              ["w2_down", "w_ple_gate", "w_ple_proj"]]
    order = [k for g in groups for k in g]
    place = {k: i for i, k in enumerate(order)}
    me = (4 * lax.axis_index("x") + 2 * lax.axis_index("y") + lax.axis_index("c")).astype(jnp.int32).reshape(1)
    size = {k: shard[k].shape[big[k]] for k in order}
    in_flight, landing, sems = gather_start([shard[k] for k in order], [big[k] for k in order],
                                            [size[k] for k in order], [[place[k] for k in g] for g in groups],
                                            "gather_start")
    landing = [place_own_block(in_flight[place[k]], landing[place[k]], big[k], me, "place_" + k) for k in order]

    def landed(g, after):
        members = [place[k] for k in groups[g]]
        axes_g, sizes_g = [big[k] for k in groups[g]], [size[k] for k in groups[g]]
        bufs = gather_wait([in_flight[i] for i in members], [landing[i] for i in members], axes_g, sizes_g,
                           *sems[g], after, "gather_wait_%d" % g)
        return forward_task(bufs, axes_g, sizes_g)

    def arrive(g, after):
        W.update(zip(groups[g], run_tasks([landed(g, after)], "gather_forward_%d" % g)[0]))

    def arrive_during(g, after, fn, *a, **kw):
        out, (got,) = fn(*a, tasks=[landed(g, after)], **kw)
        W.update(zip(groups[g], got))
        return out

    xn1 = rmsnorm_fwd(xs, norm_ffn1, "norm_ffn1")
    arrive(0, xn1)
    gate1 = matmul(xn1, W["w1_gate"], "nt", "ffn1_gate")
    arrive(1, gate1)
    gate1, up1, act1 = ffn_up(xn1, W["w1_up"], gate1, "ffn1_up")
    arrive(2, act1)
    h1 = matmul(act1, W["w1_down"], "nn", "ffn1_down", res=xs, scale=0.5)
    arrive(3, h1)
    xn2 = rmsnorm_fwd(h1, norm_mix, "norm_mix")
    z = matmul(xn2, W["w_in"], "nn", "proj_in")
    y_pre, yg, sr, si = s5_fwd(z, bc_r, bc_i, cc_r, cc_i, apw_f, ssm_d, "s5_fwd")
    glin = matmul(yg, W["ssm_w_glu"], "nn", "ssm_glu")
    y_gmlp = gmlp_fwd(z, gmlp_norm_v, wm, bs, "gmlp_fwd")
    ycat = mix_out_fwd(y_pre, glin, y_gmlp, norm_ssm_out, norm_gmlp_out, "mix_out")
    h2 = arrive_during(4, ycat, matmul, ycat, W["w_out"], "nn", "proj_out", res=h1)
    xn3 = rmsnorm_fwd(h2, norm_ffn2, "norm_ffn2")
    gate2, up2, act2 = arrive_during(5, xn3, ffn_gate_up, xn3, W["w2_gate"], W["w2_up"], "ffn2_gate_up")
    h3 = matmul(act2, W["w2_down"], "nn", "ffn2_down", res=h2, scale=0.5)
    xn4 = rmsnorm_fwd(h3, norm_ple, "norm_ple")
    pg_lin = matmul(xn4, W["w_ple_gate"], "nn", "ple_gate")
    pp = matmul(ps, W["w_ple_proj"], "nn", "ple_proj")
    h4 = ple_fwd(h3, pg_lin, pp, "ple_fwd")
    dh4, loss_part, g_norm_final = final_loss(h4, tgt, norm_final.reshape(1, -1), "final_loss")
    loss = lax.psum(loss_part[0, 0], ("x", "y", "c"))

    G = {}
    reduced = {}
    chip_part = {}
    wait_sibling, wait_chips = [], []
    core = lax.axis_index("c").astype(jnp.int32).reshape(1)

    def grad(name_, value):
        G[name_] = value
        wait_sibling.append(name_)

    def carry(fn, *a, levels="ab", extra=None, **kw):
        tasks, kinds = [], []
        if extra is not None:
            tasks.append(extra[0])
            kinds.append(("x", extra[1]))
        if "a" in levels and wait_sibling:
            group = list(wait_sibling)
            wait_sibling.clear()
            tasks.append(to_sibling_task([G[k] for k in group], [big[k] for k in group]))
            kinds.append(("a", group))
        if "b" in levels and wait_chips:
            group = list(wait_chips)
            wait_chips.clear()
            tasks.append(across_chips_task([chip_part[k] for k in group]))
            kinds.append(("b", group))
        if not tasks:
            return fn(*a, **kw)
        out, task_outs = fn(*a, tasks=tasks, **kw)
        for (kind, group), outs in zip(kinds, task_outs):
            if kind == "x":
                group(outs)
                continue
            for k, r in zip(group, outs):
                if kind == "a":
                    chip_part[k] = rs_chip_sum(G[k], r, big[k], core, "rs_sum_" + k)
                    wait_chips.append(k)
                else:
                    reduced[k] = r
        return out

    small = {}
    small["norm_final"] = g_norm_final
    dpp, dpg = ple_bwd(dh4, pg_lin, pp, "ple_bwd")
    grad("w_ple_proj", matmul(ps, dpp, "tn", "grad_ple_proj", out_dtype=BF16))
    grad("w_ple_gate", carry(matmul, xn4, dpg, "tn", "grad_ple_gate", out_dtype=BF16))
    dxn4 = carry(matmul, dpg, W["w_ple_gate"], "nt", "ple_gate_bwd")
    dh3, dh3b, small["norm_ple"] = rmsnorm_bwd(dxn4, h3, norm_ple, dh4, "norm_ple_bwd")

    def ffn_bwd(tag, dhb, xn, gate, up, act, wg, wu, wd, extra=None, last_levels="ab"):
        dgate, dup = carry(ffn_bwd_act, dhb, W[wd], gate, up, tag + "_act_bwd", extra=extra)
        grad(wd, carry(matmul, act, dhb, "tn", tag + "_grad_down", out_dtype=BF16, scale=0.5))
        grad(wg, carry(matmul, dgate, xn, "tn", tag + "_grad_gate", out_dtype=BF16))
        grad(wu, carry(matmul, dup, xn, "tn", tag + "_grad_up", out_dtype=BF16))
        dxn = carry(matmul, dgate, W[wg], "nn", tag + "_gate_bwd")
        return carry(matmul, dup, W[wu], "nn", tag + "_up_bwd", res=dxn, levels=last_levels)

    dxn3 = ffn_bwd("ffn2", dh3b, xn3, gate2, up2, act2, "w2_gate", "w2_up", "w2_down", last_levels="a")
    dh2, dh2b, small["norm_ffn2"] = rmsnorm_bwd(dxn3, h2, norm_ffn2, dh3, "norm_ffn2_bwd")

    grad("w_out", matmul(ycat, dh2b, "tn", "grad_out", out_dtype=BF16))
    dycat = carry(matmul, dh2b, W["w_out"], "nt", "proj_out_bwd", levels="a")
    dyg_direct, dglin, dy_gmlp, small["norm_ssm_out"], small["norm_gmlp_out"] = mix_out_bwd(
        dycat, y_pre, glin, y_gmlp, norm_ssm_out, norm_gmlp_out, "mix_out_bwd")
    grad("ssm_w_glu", matmul(yg, dglin, "tn", "grad_glu", out_dtype=BF16))
    dyg = carry(matmul, dglin, W["ssm_w_glu"], "nt", "ssm_glu_bwd", res=dyg_direct, levels="a")
    du, small["ssm_d"], gc_r, gc_i, gb_r, gb_i, ga_r, ga_i = carry(
        s5_bwd, dyg, y_pre, z, sr, si, bc_r, bc_i, cc_r, cc_i, apw_b, ssm_d, "s5_bwd")
    dzu, dzv, small["gmlp_norm_v"], g_wm, g_bs = gmlp_bwd(dy_gmlp, z, gmlp_norm_v, wm, wmt, bs, "gmlp_bwd")
    small["gmlp_w_s"] = g_wm
    small["gmlp_b_s"] = g_bs
    small["c_re"] = _block_diag_extract(gc_r, SSM_GROUP, SSM_STATE)
    small["c_im"] = _block_diag_extract(gc_i, SSM_GROUP, SSM_STATE)
    small["bbar_r"] = jnp.swapaxes(_block_diag_extract(gb_r, SSM_GROUP, SSM_STATE), 1, 2)
    small["bbar_i"] = jnp.swapaxes(_block_diag_extract(gb_i, SSM_GROUP, SSM_STATE), 1, 2)
    small["abar_r"] = jnp.sum(ga_r, axis=0).reshape(n_groups, SSM_STATE)
    small["abar_i"] = jnp.sum(ga_i, axis=0).reshape(n_groups, SSM_STATE)

    dz = jnp.concatenate([du, dzu, dzv], axis=1)
    grad("w_in", matmul(xn2, dz, "tn", "grad_in", out_dtype=BF16))
    dxn2 = carry(matmul, dz, W["w_in"], "nt", "proj_in_bwd")
    dh1, dh1b, small["norm_mix"] = rmsnorm_bwd(dxn2, h1, norm_mix, dh2, "norm_mix_bwd")

    def pack(parts):
        flat = jnp.concatenate([v.reshape(-1) for v in parts.values()])
        return _pad_to(flat, 0, SUBLANE * LANE).reshape(-1, LANE), flat.shape[0]

    def unpack(everyones, n, parts, tag):
        rows = everyones.shape[0] // N_DEV
        summed = sum_devices(everyones.reshape(N_DEV, rows, LANE), "sum_" + tag).reshape(-1)[:n]
        out, off = {}, 0
        for k, v in parts.items():
            out[k] = summed[off:off + v.size].reshape(v.shape)
            off += v.size
        return out

    early = dict(small)
    flat_early, n_early = pack(early)
    small_landed = []
    dxn1 = ffn_bwd("ffn1", dh1b, xn1, gate1, up1, act1, "w1_gate", "w1_up", "w1_down",
                   extra=(gather_task([flat_early], [0]), small_landed.extend))
    tot = unpack(small_landed[0], n_early, early, "small")
    grad_x, _, g_norm_ffn1 = rmsnorm_bwd(dxn1, xs, norm_ffn1, dh1, "norm_ffn1_bwd")
    assert not wait_sibling and not wait_chips and set(reduced) == set(big_names)
    last = {"norm_ffn1": g_norm_ffn1}
    flat_last, n_last = pack(last)
    ((everyones_last,),) = run_tasks([gather_task([flat_last], [0])], "gather_last")
    tot.update(unpack(everyones_last, n_last, last, "last"))

    out_g, out_d, out_m, out_v = {}, {}, {}, {}
    for k in big_names:
        g, dl, nm, nv = adam_sharded(reduced[k], view(weights[k], k), view(moments_m[k], k), view(moments_v[k], k),
                                     "adam_" + k)
        out_g[k], out_d[k], out_m[k], out_v[k] = unview(g, k), unview(dl, k), unview(nm, k), unview(nv, k)

    _, ssm_vjp = jax.vjp(_ssm_discretize, ssm_log_dt[0], ssm_a_re[0], ssm_a_im[0], ssm_b_re[0], ssm_b_im[0])
    g_log_dt, g_a_re, g_a_im, g_b_re, g_b_im = ssm_vjp((tot["abar_r"], tot["abar_i"], tot["bbar_r"], tot["bbar_i"]))
    small_grads = {
        "norm_ffn1": tot["norm_ffn1"], "norm_mix": tot["norm_mix"], "ssm_log_dt": g_log_dt, "ssm_a_re": g_a_re,
        "ssm_a_im": g_a_im, "ssm_b_re": g_b_re, "ssm_b_im": g_b_im, "ssm_c_re": tot["c_re"], "ssm_c_im": tot["c_im"],
        "ssm_d": tot["ssm_d"], "gmlp_norm_v": tot["gmlp_norm_v"],
        "gmlp_w_s": jnp.where(causal[None], tot["gmlp_w_s"], 0.0), "gmlp_b_s": tot["gmlp_b_s"],
        "norm_ssm_out": tot["norm_ssm_out"], "norm_gmlp_out": tot["norm_gmlp_out"], "norm_ffn2": tot["norm_ffn2"],
        "norm_ple": tot["norm_ple"], "norm_final": tot["norm_final"]}
    swapped = ("ssm_b_re", "ssm_b_im")

    def pre(k, a):
        return jnp.swapaxes(a, -1, -2) if k in swapped else a

    def update(group, name_):
        items = [(_as2d(pre(k, small_grads[k].reshape(weights[k].shape))), _as2d(pre(k, weights[k])),
                  _as2d(pre(k, moments_m[k])), _as2d(pre(k, moments_v[k]))) for k in group]
        for k, it, (dl, nm, nv) in zip(group, items, adam_small(items, name_)):
            shp = pre(k, weights[k]).shape
            out_g[k], out_d[k], out_m[k], out_v[k] = [pre(k, a.reshape(shp)) for a in (it[0], dl, nm, nv)]

    update([k for k in small_grads if k != "norm_ffn1"], "adam_replicated")
    update(["norm_ffn1"], "adam_norm_ffn1")

    return (loss, grad_x[None], *[out_g[k] for k in names], *[out_d[k] for k in names],
            *[out_m[k] for k in names], *[out_v[k] for k in names])
```

```python
import math

import jax
import jax.numpy as jnp
from jax import lax
from jax.experimental import pallas as pl
from jax.experimental.pallas import tpu as pltpu

F32 = jnp.float32
BF16 = jnp.bfloat16
MESH_DT = pl.DeviceIdType.MESH

N_DEV = 8
N_CHIP = 4
LANE = 128
SUBLANE = 8
VMEM_LIMIT = 60 * 1024 * 1024

EPS = 1e-6
SSM_GROUP = 16
SSM_STATE = 64
GROUPS_PER_BLOCK = LANE // SSM_GROUP
STATE_BLOCK = GROUPS_PER_BLOCK * SSM_STATE
GMLP_HEAD = 128
CHUNK = 128

ADAM_LR = 0.001
ADAM_B1 = 0.9
ADAM_B2 = 0.999
ADAM_EPS = 1e-08
ADAM_WD = 0.01
ADAM_STEP = 10

GELU_K = math.sqrt(2.0 / math.pi)
GELU_C = 0.044715


def _cparams():
    return pltpu.CompilerParams(vmem_limit_bytes=VMEM_LIMIT)


def _tile(n, pref):
    if n <= pref:
        return n
    t = (pref // LANE) * LANE
    while t > 0:
        if n % t == 0:
            return t
        t -= LANE
    return n


def _row_tile(n, pref):
    if n <= pref:
        return n
    t = (pref // SUBLANE) * SUBLANE
    while t > 0:
        if n % t == 0:
            return t
        t -= SUBLANE
    return n


def _gelu(x):
    t = jnp.tanh(GELU_K * (x + GELU_C * x * x * x))
    return 0.5 * x * (1.0 + t)


def _gelu_grad(x):
    t = jnp.tanh(GELU_K * (x + GELU_C * x * x * x))
    return 0.5 * (1.0 + t) + 0.5 * x * (1.0 - t * t) * (GELU_K * (1.0 + 3.0 * GELU_C * x * x))


def _sigmoid(x):
    return 0.5 * jnp.tanh(0.5 * x) + 0.5


_DN = {
    "nn": (((1,), (0,)), ((), ())),
    "nt": (((1,), (1,)), ((), ())),
    "tn": (((0,), (0,)), ((), ())),
}


def _dot(a, b, mode="nn"):
    return lax.dot_general(a, b, _DN[mode], preferred_element_type=F32)


class CommTask:
    def __init__(self, inputs, out_shape, n_sems, start, late, finish, in_place=False):
        self.inputs, self.out_shape, self.n_sems = list(inputs), list(out_shape), n_sems
        self.start, self.late, self.finish = start, late, finish
        self.in_place = in_place


def _task_aliases(tasks, first_in, first_out):
    aliases = {}
    for t in tasks:
        if t.in_place:
            aliases.update({first_in + i: first_out + i for i in range(len(t.inputs))})
        first_in += len(t.inputs)
        first_out += len(t.out_shape)
    return aliases


def _call(body, *, name, grid, in_specs, out_specs, out_shape, args, scratch_shapes=(), tasks=()):
    in_specs, out_specs, out_shape = list(in_specs), list(out_specs), list(out_shape)
    scratch_shapes = list(scratch_shapes)
    if not tasks:
        return pl.pallas_call(
            body, name=name, grid=grid, in_specs=in_specs, out_specs=out_specs, out_shape=out_shape,
            scratch_shapes=scratch_shapes, compiler_params=_cparams())(*args)
    n_in, n_out, n_scr = len(in_specs), len(out_specs), len(scratch_shapes)
    t_in = [len(t.inputs) for t in tasks]
    t_out = [len(t.out_shape) for t in tasks]
    late_step = grid[0] - max(1, grid[0] // 4)
    has_late = grid[0] >= 2

    def carried(*refs):
        pos = n_in
        task_ins = []
        for k in t_in:
            task_ins.append(refs[pos:pos + k])
            pos += k
        outs = refs[pos:pos + n_out]
        pos += n_out
        task_outs = []
        for k in t_out:
            task_outs.append(refs[pos:pos + k])
            pos += k
        scratch = refs[pos:pos + n_scr]
        pos += n_scr
        sems = [refs[pos + 3 * i:pos + 3 * i + 3] for i in range(len(tasks))]
        ids = [pl.program_id(d) for d in range(len(grid))]
        rest_zero = True
        for d in range(1, len(grid)):
            rest_zero = jnp.logical_and(rest_zero, ids[d] == 0)
        first = jnp.logical_and(ids[0] == 0, rest_zero)
        last = ids[0] == grid[0] - 1
        for d in range(1, len(grid)):
            last = jnp.logical_and(last, ids[d] == grid[d] - 1)

        @pl.when(first)
        def _():
            for t, ti, to, s in zip(tasks, task_ins, task_outs, sems):
                t.start(ti, to, *s)

        if has_late:
            @pl.when(jnp.logical_and(ids[0] == late_step, rest_zero))
            def _():
                for t, ti, to, s in zip(tasks, task_ins, task_outs, sems):
                    t.late(ti, to, *s)

        body(*refs[:n_in], *outs, *scratch)

        @pl.when(last)
        def _():
            for t, ti, to, s in zip(tasks, task_ins, task_outs, sems):
                if not has_late:
                    t.late(ti, to, *s)
                t.finish(ti, to, *s)

    any_spec = pl.BlockSpec(memory_space=pl.ANY)
    sem_shapes = [pltpu.SemaphoreType.DMA((n,)) for t in tasks for n in t.n_sems]
    res = pl.pallas_call(
        carried, name=name, grid=grid,
        in_specs=in_specs + [any_spec] * sum(t_in), out_specs=out_specs + [any_spec] * sum(t_out),
        out_shape=out_shape + [s for t in tasks for s in t.out_shape],
        input_output_aliases=_task_aliases(tasks, n_in, n_out),
        scratch_shapes=scratch_shapes + sem_shapes, compiler_params=_cparams(),
    )(*args, *[a for t in tasks for a in t.inputs])
    res = list(res)
    task_res, pos = [], n_out
    for k in t_out:
        task_res.append(res[pos:pos + k])
        pos += k
    return res[:n_out], task_res


def _mm_dims(a, b, mode):
    if mode == "nn":
        (m, k), (k2, n) = a.shape, b.shape
    elif mode == "nt":
        (m, k), (n, k2) = a.shape, b.shape
    else:
        (k, m), (k2, n) = a.shape, b.shape
    assert k == k2, (a.shape, b.shape, mode)
    return m, n, k


def _mm_specs(mode, tm, tn, tk):
    if mode == "tn":
        a_spec = pl.BlockSpec((tk, tm), lambda i, j, k: (k, i))
    else:
        a_spec = pl.BlockSpec((tm, tk), lambda i, j, k: (i, k))
    if mode == "nt":
        b_spec = pl.BlockSpec((tn, tk), lambda i, j, k: (j, k))
    else:
        b_spec = pl.BlockSpec((tk, tn), lambda i, j, k: (k, j))
    return a_spec, b_spec


def _accumulate(acc, nk, partial, emit):
    if nk == 1:
        emit(partial)
        return
    kk = pl.program_id(2)

    @pl.when(kk == 0)
    def _():
        acc[...] = partial

    @pl.when(kk > 0)
    def _():
        acc[...] += partial

    @pl.when(kk == nk - 1)
    def _():
        emit(acc[...])


def matmul(a, b, mode, name, out_dtype=F32, res=None, scale=1.0, tm=1024, tn=1024, tk=2048, tasks=()):
    m, n, k = _mm_dims(a, b, mode)
    if mode == "tn":
        tk = max(tk, 4096)
    elif k > tk:
        tk, tn = k, (tn if k <= 3072 else tn // 2)
    tm, tn, tk = _tile(m, tm), _tile(n, tn), _tile(k, tk)
    nk = k // tk
    a_spec, b_spec = _mm_specs(mode, tm, tn, tk)
    o_spec = pl.BlockSpec((tm, tn), lambda i, j, k: (i, j))
    has_res = res is not None

    def body(*refs):
        if has_res:
            a_ref, b_ref, r_ref, o_ref, acc = refs
        else:
            a_ref, b_ref, o_ref, acc = refs

        def emit(v):
            if scale != 1.0:
                v = v * scale
            if has_res:
                v = r_ref[...] + v
            o_ref[...] = v.astype(out_dtype)

        _accumulate(acc, nk, _dot(a_ref[...], b_ref[...], mode), emit)

    out = _call(
        body, name=name, grid=(m // tm, n // tn, nk),
        in_specs=[a_spec, b_spec] + ([o_spec] if has_res else []), out_specs=[o_spec],
        out_shape=[jax.ShapeDtypeStruct((m, n), out_dtype)], args=(a, b) + ((res,) if has_res else ()),
        scratch_shapes=[pltpu.VMEM((tm, tn) if nk > 1 else (SUBLANE, LANE), F32)], tasks=tasks)
    return (out[0][0], out[1]) if tasks else out[0]


def ffn_up(xn, wu, gate, name, tm=1024, tn=1024, tk=2048, tasks=()):
    m, n, k = _mm_dims(xn, wu, "nt")
    tm, tn, tk = _tile(m, tm), _tile(n, tn), _tile(k, tk)
    nk = k // tk
    a_spec, b_spec = _mm_specs("nt", tm, tn, tk)
    o_spec = pl.BlockSpec((tm, tn), lambda i, j, k: (i, j))

    def body(a_ref, u_ref, gate_ref, gate_b_ref, up_b_ref, act_ref, acc):
        def emit(u):
            g = gate_ref[...]
            gate_b_ref[...] = g.astype(BF16)
            up_b_ref[...] = u.astype(BF16)
            act_ref[...] = (g * _sigmoid(g) * u).astype(BF16)

        _accumulate(acc, nk, _dot(a_ref[...], u_ref[...], "nt"), emit)

    out = _call(
        body, name=name, grid=(m // tm, n // tn, nk), in_specs=[a_spec, b_spec, o_spec],
        out_specs=[o_spec, o_spec, o_spec],
        out_shape=[jax.ShapeDtypeStruct((m, n), BF16), jax.ShapeDtypeStruct((m, n), BF16),
                   jax.ShapeDtypeStruct((m, n), BF16)],
        args=(xn, wu, gate), scratch_shapes=[pltpu.VMEM((tm, tn) if nk > 1 else (SUBLANE, LANE), F32)], tasks=tasks)
    return (tuple(out[0]), out[1]) if tasks else tuple(out)


def ffn_gate_up(xn, wg, wu, name, tm=1024, tn=1024, tasks=()):
    m, n, k = _mm_dims(xn, wg, "nt")
    tm, tn = _tile(m, tm), _tile(n, tn)
    a_spec, b_spec = _mm_specs("nt", tm, tn, k)
    o_spec = pl.BlockSpec((tm, tn), lambda i, j, k: (i, j))

    def body(a_ref, g_ref, u_ref, gate_b_ref, up_b_ref, act_ref):
        a = a_ref[...]
        g = _dot(a, g_ref[...], "nt")
        u = _dot(a, u_ref[...], "nt")
        gate_b_ref[...] = g.astype(BF16)
        up_b_ref[...] = u.astype(BF16)
        act_ref[...] = (g * _sigmoid(g) * u).astype(BF16)

    out = _call(
        body, name=name, grid=(m // tm, n // tn, 1), in_specs=[a_spec, b_spec, b_spec],
        out_specs=[o_spec, o_spec, o_spec], out_shape=[jax.ShapeDtypeStruct((m, n), BF16)] * 3,
        args=(xn, wg, wu), tasks=tasks)
    return (tuple(out[0]), out[1]) if tasks else tuple(out)


def ple_fused_fwd(xn, w_gate, ps, w_proj, h, name, tm=1024, tn=1024):
    m, n, k = _mm_dims(xn, w_gate, "nn")
    kp = ps.shape[1]
    tm, tn = _tile(m, tm), _tile(n, tn)
    o_spec = pl.BlockSpec((tm, tn), lambda i, j: (i, j))

    def body(a_ref, wg_ref, p_ref, wp_ref, h_ref, gl_ref, pp_ref, o_ref):
        gl = _dot(a_ref[...], wg_ref[...])
        pp = _dot(p_ref[...], wp_ref[...])
        gl_ref[...] = gl
        pp_ref[...] = pp
        o_ref[...] = h_ref[...] + _sigmoid(gl) * pp

    return tuple(pl.pallas_call(
        body, name=name, grid=(m // tm, n // tn),
        in_specs=[pl.BlockSpec((tm, k), lambda i, j: (i, 0)), pl.BlockSpec((k, tn), lambda i, j: (0, j)),
                  pl.BlockSpec((tm, kp), lambda i, j: (i, 0)), pl.BlockSpec((kp, tn), lambda i, j: (0, j)), o_spec],
        out_specs=[o_spec, o_spec, o_spec], out_shape=[jax.ShapeDtypeStruct((m, n), F32)] * 3,
        compiler_params=_cparams(),
    )(xn, w_gate, ps, w_proj, h))


def ffn_bwd_act(dh, wd, gate, up, name, tm=1024, tn=1024, tk=2048, tasks=()):
    m, n, k = _mm_dims(dh, wd, "nt")
    tm, tn, tk = _tile(m, tm), _tile(n, tn), _tile(k, tk)
    nk = k // tk
    a_spec, b_spec = _mm_specs("nt", tm, tn, tk)
    o_spec = pl.BlockSpec((tm, tn), lambda i, j, k: (i, j))

    def body(a_ref, b_ref, gate_ref, up_ref, dg_ref, du_ref, acc):
        def emit(total):
            dact = 0.5 * total
            g = gate_ref[...].astype(F32)
            sg = _sigmoid(g)
            du_ref[...] = (dact * (g * sg)).astype(BF16)
            dg_ref[...] = (dact * up_ref[...].astype(F32) * (sg * (1.0 + g * (1.0 - sg)))).astype(BF16)

        _accumulate(acc, nk, _dot(a_ref[...], b_ref[...], "nt"), emit)

    out = _call(
        body, name=name, grid=(m // tm, n // tn, nk), in_specs=[a_spec, b_spec, o_spec, o_spec],
        out_specs=[o_spec, o_spec],
        out_shape=[jax.ShapeDtypeStruct((m, n), BF16), jax.ShapeDtypeStruct((m, n), BF16)],
        args=(dh, wd, gate, up), scratch_shapes=[pltpu.VMEM((tm, tn) if nk > 1 else (SUBLANE, LANE), F32)],
        tasks=tasks)
    return (tuple(out[0]), out[1]) if tasks else tuple(out)


def _rows(t, d, tr):
    return pl.BlockSpec((tr, d), lambda i: (i, 0))


def _vec(d):
    return pl.BlockSpec((1, d), lambda i: (0, 0))


def rmsnorm_fwd(x, g, name, tr=512):
    t, d = x.shape
    tr = _row_tile(t, tr)

    def body(x_ref, g_ref, o_ref):
        xf = x_ref[...]
        r = lax.rsqrt(jnp.mean(xf * xf, axis=-1, keepdims=True) + EPS)
        o_ref[...] = (xf * r * g_ref[...]).astype(BF16)

    return pl.pallas_call(
        body, name=name, grid=(t // tr,), in_specs=[_rows(t, d, tr), _vec(d)], out_specs=_rows(t, d, tr),
        out_shape=jax.ShapeDtypeStruct((t, d), BF16), compiler_params=_cparams(),
    )(x, g)


def _rms_bwd(dxn, xf, g):
    r = lax.rsqrt(jnp.mean(xf * xf, axis=-1, keepdims=True) + EPS)
    xhat = xf * r
    dg = jnp.sum(dxn * xhat, axis=0, keepdims=True)
    dxh = dxn * g
    dx = r * (dxh - xhat * jnp.mean(dxh * xhat, axis=-1, keepdims=True))
    return dx, dg


def rmsnorm_bwd(dxn, x, g, dres, name, tr=256):
    t, d = x.shape
    tr = _row_tile(t, tr)

    def body(dxn_ref, x_ref, g_ref, dres_ref, o_ref, ob_ref, dg_ref):
        dx, dg = _rms_bwd(dxn_ref[...], x_ref[...], g_ref[...])
        out = dres_ref[...] + dx
        o_ref[...] = out
        ob_ref[...] = out.astype(BF16)

        @pl.when(pl.program_id(0) == 0)
        def _():
            dg_ref[...] = jnp.zeros_like(dg_ref)

        dg_ref[...] += dg

    return pl.pallas_call(
        body, name=name, grid=(t // tr,),
        in_specs=[_rows(t, d, tr), _rows(t, d, tr), _vec(d), _rows(t, d, tr)],
        out_specs=[_rows(t, d, tr), _rows(t, d, tr), _vec(d)],
        out_shape=[jax.ShapeDtypeStruct((t, d), F32), jax.ShapeDtypeStruct((t, d), BF16),
                   jax.ShapeDtypeStruct((1, d), F32)],
        compiler_params=_cparams(),
    )(dxn, x, g, dres)


def final_loss(h, target, g, name, tr=256):
    t, d = h.shape
    tr = _row_tile(t, tr)

    def body(h_ref, t_ref, g_ref, dh_ref, loss_ref, dg_ref):
        xf = h_ref[...]
        gg = g_ref[...]
        r = lax.rsqrt(jnp.mean(xf * xf, axis=-1, keepdims=True) + EPS)
        xhat = xf * r
        e = xhat * gg - t_ref[...]
        part = jnp.sum(jnp.sum(e * e, axis=1, keepdims=True), axis=0, keepdims=True) * (0.5 / d)
        dout = e * (1.0 / d)
        dg = jnp.sum(dout * xhat, axis=0, keepdims=True)
        dxh = dout * gg
        dh_ref[...] = r * (dxh - xhat * jnp.mean(dxh * xhat, axis=-1, keepdims=True))

        @pl.when(pl.program_id(0) == 0)
        def _():
            dg_ref[...] = jnp.zeros_like(dg_ref)
            loss_ref[...] = jnp.zeros_like(loss_ref)

        dg_ref[...] += dg
        loss_ref[...] += jnp.broadcast_to(part, loss_ref.shape)

    return pl.pallas_call(
        body, name=name, grid=(t // tr,),
        in_specs=[_rows(t, d, tr), _rows(t, d, tr), _vec(d)],
        out_specs=[_rows(t, d, tr), pl.BlockSpec((SUBLANE, LANE), lambda i: (0, 0)), _vec(d)],
        out_shape=[jax.ShapeDtypeStruct((t, d), F32), jax.ShapeDtypeStruct((SUBLANE, LANE), F32),
                   jax.ShapeDtypeStruct((1, d), F32)],
        compiler_params=_cparams(),
    )(h, target, g)


def ple_bwd(dh, glin, pp, name, tr=512):
    t, d = dh.shape
    tr = _row_tile(t, tr)

    def body(dh_ref, gl_ref, pp_ref, dpp_ref, dgl_ref):
        gate = _sigmoid(gl_ref[...])
        dh_ = dh_ref[...]
        dpp_ref[...] = (dh_ * gate).astype(BF16)
        dgl_ref[...] = (dh_ * pp_ref[...] * gate * (1.0 - gate)).astype(BF16)

    sp = _rows(t, d, tr)
    return pl.pallas_call(
        body, name=name, grid=(t // tr,), in_specs=[sp, sp, sp], out_specs=[sp, sp],
        out_shape=[jax.ShapeDtypeStruct((t, d), BF16), jax.ShapeDtypeStruct((t, d), BF16)],
        compiler_params=_cparams(),
    )(dh, glin, pp)


def mix_out_fwd(y_pre, glin, y_gmlp, g_so, g_go, name, tr=512):
    t, d = y_pre.shape
    tr = _row_tile(t, tr)

    def body(yp_ref, gl_ref, yg_ref, gs_ref, gg_ref, o_ref):
        ys = _gelu(yp_ref[...]) * _sigmoid(gl_ref[...])
        r = lax.rsqrt(jnp.mean(ys * ys, axis=-1, keepdims=True) + EPS)
        o_ref[:, 0:d] = (ys * r * gs_ref[...]).astype(BF16)
        yq = yg_ref[...]
        r2 = lax.rsqrt(jnp.mean(yq * yq, axis=-1, keepdims=True) + EPS)
        o_ref[:, d:2 * d] = (yq * r2 * gg_ref[...]).astype(BF16)

    sp = _rows(t, d, tr)
    return pl.pallas_call(
        body, name=name, grid=(t // tr,), in_specs=[sp, sp, sp, _vec(d), _vec(d)],
        out_specs=_rows(t, 2 * d, tr), out_shape=jax.ShapeDtypeStruct((t, 2 * d), BF16),
        compiler_params=_cparams(),
    )(y_pre, glin, y_gmlp, g_so, g_go)


def mix_out_bwd(dycat, y_pre, glin, y_gmlp, g_so, g_go, name, tr=256):
    t, d = y_pre.shape
    tr = _row_tile(t, tr)

    def body(dy_ref, yp_ref, gl_ref, yg_ref, gs_ref, gg_ref, dyg_ref, dl_ref, dyq_ref, dgs_ref, dgg_ref):
        yg = _gelu(yp_ref[...])
        sg = _sigmoid(gl_ref[...])
        dys, dgs = _rms_bwd(dy_ref[:, 0:d], yg * sg, gs_ref[...])
        dyg_ref[...] = dys * sg
        dl_ref[...] = (dys * yg * sg * (1.0 - sg)).astype(BF16)
        dyq, dgg = _rms_bwd(dy_ref[:, d:2 * d], yg_ref[...], gg_ref[...])
        dyq_ref[...] = dyq

        @pl.when(pl.program_id(0) == 0)
        def _():
            dgs_ref[...] = jnp.zeros_like(dgs_ref)
            dgg_ref[...] = jnp.zeros_like(dgg_ref)

        dgs_ref[...] += dgs
        dgg_ref[...] += dgg

    sp = _rows(t, d, tr)
    return pl.pallas_call(
        body, name=name, grid=(t // tr,),
        in_specs=[_rows(t, 2 * d, tr), sp, sp, sp, _vec(d), _vec(d)],
        out_specs=[sp, sp, sp, _vec(d), _vec(d)],
        out_shape=[jax.ShapeDtypeStruct((t, d), F32), jax.ShapeDtypeStruct((t, d), BF16),
                   jax.ShapeDtypeStruct((t, d), F32), jax.ShapeDtypeStruct((1, d), F32),
                   jax.ShapeDtypeStruct((1, d), F32)],
        compiler_params=_cparams(),
    )(dycat, y_pre, glin, y_gmlp, g_so, g_go)


SCAN_COLS = 512


def _scan_tile(xr, xi, const, cr, ci, reverse):
    for lvl, sh in enumerate((1, 2, 4)):
        ar, ai = const(2 * lvl), const(2 * lvl + 1)
        s = (SUBLANE - sh) if reverse else sh
        rr = pltpu.roll(xr, s, 0)
        ri = pltpu.roll(xi, s, 0)
        xr, xi = xr + ar * rr - ai * ri, xi + ar * ri + ai * rr
    pr, pi_ = const(6), const(7)
    xr, xi = xr + pr * cr - pi_ * ci, xi + pr * ci + pi_ * cr
    return xr, xi


def _bcast_row(x, row):
    return jnp.broadcast_to(x[row:row + 1, :], x.shape)


def s5_fwd(z, bc_r, bc_i, cc_r, cc_i, apw, dvec, name, tc=512, tasks=()):
    t = z.shape[0]
    nblk = bc_r.shape[0]
    d = nblk * LANE
    ns = nblk * STATE_BLOCK
    tc = _row_tile(t, tc)
    ntile = tc // SUBLANE

    def body(z_ref, br_ref, bi_ref, cr_ref, ci_ref, apw_ref, d_ref, y_ref, yg_ref, sr_ref, si_ref, carry):
        @pl.when(pl.program_id(0) == 0)
        def _():
            carry[...] = jnp.zeros_like(carry)

        for j in range(nblk):
            uj = z_ref[:, j * LANE:(j + 1) * LANE]
            ub = uj.astype(BF16)
            for q in range(STATE_BLOCK // SCAN_COLS):
                c0 = j * STATE_BLOCK + q * SCAN_COLS
                cs = pl.ds(c0, SCAN_COLS)
                bs = slice(q * SCAN_COLS, (q + 1) * SCAN_COLS)
                sr_ref[:, cs] = _dot(ub, br_ref[j, :, bs])
                si_ref[:, cs] = _dot(ub, bi_ref[j, :, bs])
                const = lambda k, cs=cs: apw_ref[k, :, cs]

                def tile(k, c, cs=cs, const=const):
                    rows = pl.ds(pl.multiple_of(k * SUBLANE, SUBLANE), SUBLANE)
                    xr, xi = _scan_tile(sr_ref[rows, cs], si_ref[rows, cs], const, c[0], c[1], False)
                    sr_ref[rows, cs] = xr
                    si_ref[rows, cs] = xi
                    return _bcast_row(xr, SUBLANE - 1), _bcast_row(xi, SUBLANE - 1)

                c_r, c_i = lax.fori_loop(0, ntile, tile, (carry[0, :, cs], carry[1, :, cs]))
                carry[0, :, cs] = c_r
                carry[1, :, cs] = c_i
            sb = pl.ds(j * STATE_BLOCK, STATE_BLOCK)
            y = (_dot(sr_ref[:, sb].astype(BF16), cr_ref[j]) - _dot(si_ref[:, sb].astype(BF16), ci_ref[j])
                 + d_ref[:, j * LANE:(j + 1) * LANE] * uj)
            y_ref[:, j * LANE:(j + 1) * LANE] = y
            yg_ref[:, j * LANE:(j + 1) * LANE] = _gelu(y).astype(BF16)

    full3 = lambda shp: pl.BlockSpec(shp, lambda i: (0, 0, 0))
    out = _call(
        body, name=name, grid=(t // tc,),
        in_specs=[pl.BlockSpec((tc, d), lambda i: (i, 0)), full3(bc_r.shape), full3(bc_i.shape),
                  full3(cc_r.shape), full3(cc_i.shape), full3(apw.shape), _vec(d)],
        out_specs=[pl.BlockSpec((tc, d), lambda i: (i, 0)), pl.BlockSpec((tc, d), lambda i: (i, 0)),
                   pl.BlockSpec((tc, ns), lambda i: (i, 0)), pl.BlockSpec((tc, ns), lambda i: (i, 0))],
        out_shape=[jax.ShapeDtypeStruct((t, d), F32), jax.ShapeDtypeStruct((t, d), BF16),
                   jax.ShapeDtypeStruct((t, ns), F32), jax.ShapeDtypeStruct((t, ns), F32)],
        args=(z, bc_r, bc_i, cc_r, cc_i, apw, dvec), scratch_shapes=[pltpu.VMEM((2, SUBLANE, ns), F32)], tasks=tasks)
    return (tuple(out[0]), out[1]) if tasks else tuple(out)


def s5_bwd(dyg, y_pre, z, sr, si, bc_r, bc_i, cc_r, cc_i, apw_rev, dvec, name, tc=256, tasks=()):
    t = z.shape[0]
    nblk = bc_r.shape[0]
    d = nblk * LANE
    ns = nblk * STATE_BLOCK
    tc = _row_tile(t, tc)
    ntile = tc // SUBLANE
    nchunk = t // tc
    tiles_per_chunk = tc // SUBLANE

    def body(dyg_ref, yp_ref, z_ref, sr_ref, si_ref, pr_ref, pi_ref, br_ref, bi_ref, cr_ref, ci_ref, apw_ref,
             d_ref, du_ref, gd_ref, gcr_ref, gci_ref, gbr_ref, gbi_ref, gar_ref, gai_ref, lr_ref, li_ref, carry):
        step = pl.program_id(0)

        @pl.when(step == 0)
        def _():
            carry[...] = jnp.zeros_like(carry)
            for ref in (gd_ref, gcr_ref, gci_ref, gbr_ref, gbi_ref, gar_ref, gai_ref):
                ref[...] = jnp.zeros_like(ref)

        first_chunk = (step == nchunk - 1).astype(F32)
        keep_prev = 1.0 - first_chunk
        row0 = lax.broadcasted_iota(jnp.int32, (SUBLANE, SCAN_COLS), 0) == 0

        for j in range(nblk):
            lanes = slice(j * LANE, (j + 1) * LANE)
            uj = z_ref[:, lanes]
            ub = uj.astype(BF16)
            gy = dyg_ref[:, lanes] * _gelu_grad(yp_ref[:, lanes])
            gyb = gy.astype(BF16)
            gd_ref[:, lanes] += jnp.sum(gy * uj, axis=0, keepdims=True)
            for q in range(STATE_BLOCK // SCAN_COLS):
                c0 = j * STATE_BLOCK + q * SCAN_COLS
                cs = pl.ds(c0, SCAN_COLS)
                bs = slice(q * SCAN_COLS, (q + 1) * SCAN_COLS)
                lr_ref[:, cs] = _dot(gyb, cr_ref[j, bs, :], "nt")
                li_ref[:, cs] = -_dot(gyb, ci_ref[j, bs, :], "nt")
                const = lambda k, cs=cs: apw_ref[k, :, cs]

                def one_tile(rows, prev_r, prev_i, c, cs=cs, const=const):
                    cr_, ci_, gar, gai = c
                    xr, xi = _scan_tile(lr_ref[rows, cs], li_ref[rows, cs], const, cr_, ci_, True)
                    lr_ref[rows, cs] = xr
                    li_ref[rows, cs] = xi
                    spr = jnp.where(row0, prev_r, pltpu.roll(sr_ref[rows, cs], 1, 0))
                    spi = jnp.where(row0, prev_i, pltpu.roll(si_ref[rows, cs], 1, 0))
                    gar = gar + xr * spr + xi * spi
                    gai = gai + xi * spr - xr * spi
                    return _bcast_row(xr, 0), _bcast_row(xi, 0), gar, gai

                def tile(k, c, cs=cs, one_tile=one_tile):
                    kk = ntile - 1 - k
                    rows = pl.ds(pl.multiple_of(kk * SUBLANE, SUBLANE), SUBLANE)
                    prow = pl.ds(pl.multiple_of((kk - 1) * SUBLANE, SUBLANE), SUBLANE)
                    prev_r = _bcast_row(sr_ref[prow, cs], SUBLANE - 1)
                    prev_i = _bcast_row(si_ref[prow, cs], SUBLANE - 1)
                    return one_tile(rows, prev_r, prev_i, c)

                zero = jnp.zeros((SUBLANE, SCAN_COLS), F32)
                c = lax.fori_loop(0, ntile - 1, tile, (carry[0, :, cs], carry[1, :, cs], zero, zero))
                prev_r = _bcast_row(pr_ref[:, cs], SUBLANE - 1) * keep_prev
                prev_i = _bcast_row(pi_ref[:, cs], SUBLANE - 1) * keep_prev
                c_r, c_i, gar, gai = one_tile(pl.ds(0, SUBLANE), prev_r, prev_i, c)
                carry[0, :, cs] = c_r
                carry[1, :, cs] = c_i
                gar_ref[:, cs] += gar
                gai_ref[:, cs] += gai
            sb = pl.ds(j * STATE_BLOCK, STATE_BLOCK)
            lrb = lr_ref[:, sb].astype(BF16)
            lib = li_ref[:, sb].astype(BF16)
            gcr_ref[j] += _dot(gyb, sr_ref[:, sb].astype(BF16), "tn")
            gci_ref[j] -= _dot(gyb, si_ref[:, sb].astype(BF16), "tn")
            gbr_ref[j] += _dot(ub, lrb, "tn")
            gbi_ref[j] += _dot(ub, lib, "tn")
            du = _dot(lrb, br_ref[j], "nt") + _dot(lib, bi_ref[j], "nt") + gy * d_ref[:, lanes]
            du_ref[:, lanes] = du.astype(BF16)

    rev = lambda i: (nchunk - 1 - i, 0)
    prev = lambda i: (jnp.maximum((nchunk - 1 - i) * tiles_per_chunk - 1, 0), 0)
    full3 = lambda shp: pl.BlockSpec(shp, lambda i: (0, 0, 0))
    acc3 = pl.BlockSpec((nblk, LANE, STATE_BLOCK), lambda i: (0, 0, 0))
    acc_rows = pl.BlockSpec((SUBLANE, ns), lambda i: (0, 0))
    out = _call(
        body, name=name, grid=(nchunk,),
        in_specs=[pl.BlockSpec((tc, d), rev), pl.BlockSpec((tc, d), rev), pl.BlockSpec((tc, d), rev),
                  pl.BlockSpec((tc, ns), rev), pl.BlockSpec((tc, ns), rev),
                  pl.BlockSpec((SUBLANE, ns), prev), pl.BlockSpec((SUBLANE, ns), prev),
                  full3(bc_r.shape), full3(bc_i.shape), full3(cc_r.shape), full3(cc_i.shape), full3(apw_rev.shape),
                  _vec(d)],
        out_specs=[pl.BlockSpec((tc, d), rev), _vec(d), acc3, acc3, acc3, acc3, acc_rows, acc_rows],
        out_shape=[jax.ShapeDtypeStruct((t, d), BF16), jax.ShapeDtypeStruct((1, d), F32)]
        + [jax.ShapeDtypeStruct((nblk, LANE, STATE_BLOCK), F32)] * 4
        + [jax.ShapeDtypeStruct((SUBLANE, ns), F32)] * 2,
        args=(dyg, y_pre, z, sr, si, sr, si, bc_r, bc_i, cc_r, cc_i, apw_rev, dvec),
        scratch_shapes=[pltpu.VMEM((tc, ns), F32), pltpu.VMEM((tc, ns), F32), pltpu.VMEM((2, SUBLANE, ns), F32)],
        tasks=tasks)
    return (tuple(out[0]), out[1]) if tasks else tuple(out)


def _cmul(a, b):
    return a[0] * b[0] - a[1] * b[1], a[0] * b[1] + a[1] * b[0]


def _scan_constants(abar_r, abar_i, reverse):
    ar = abar_r.reshape(1, -1)
    ai = abar_i.reshape(1, -1)
    if reverse:
        ai = -ai
    pw = [(ar, ai)]
    for _ in range(SUBLANE - 1):
        pw.append(_cmul(pw[-1], (ar, ai)))
    rows = lax.broadcasted_iota(jnp.int32, (SUBLANE, 1), 0)
    out = []
    for sh in (1, 2, 4):
        keep = (rows <= SUBLANE - 1 - sh) if reverse else (rows >= sh)
        for part in pw[sh - 1]:
            out.append(jnp.where(keep, part, 0.0))
    for comp in (0, 1):
        stack = jnp.concatenate([pw[k][comp] for k in range(SUBLANE)], axis=0)
        out.append(stack[::-1] if reverse else stack)
    return jnp.stack(out, axis=0).astype(F32)


def _ssm_discretize(log_dt, a_re, a_im, b_re, b_im):
    dt = jnp.exp(log_dt)[:, None]
    lr = jnp.minimum(a_re, -1e-4)
    li = a_im
    mag = jnp.exp(lr * dt)
    ang = li * dt
    abar_r = mag * jnp.cos(ang)
    abar_i = mag * jnp.sin(ang)
    den = lr * lr + li * li
    xr = abar_r - 1.0
    xi = abar_i
    zr = (xr * lr + xi * li) / den
    zi = (xi * lr - xr * li) / den
    bbar_r = zr[..., None] * b_re - zi[..., None] * b_im
    bbar_i = zr[..., None] * b_im + zi[..., None] * b_re
    return abar_r, abar_i, bbar_r, bbar_i


def _block_diag(w):
    g, a, b = w.shape
    nb = g // GROUPS_PER_BLOCK
    eye = jnp.eye(GROUPS_PER_BLOCK, dtype=w.dtype)
    w5 = w.reshape(nb, GROUPS_PER_BLOCK, a, b)
    out = w5[:, :, :, None, :] * eye[None, :, None, :, None]
    return out.reshape(nb, GROUPS_PER_BLOCK * a, GROUPS_PER_BLOCK * b)


def _block_diag_extract(m, a, b):
    nb = m.shape[0]
    eye = jnp.eye(GROUPS_PER_BLOCK, dtype=m.dtype)
    m5 = m.reshape(nb, GROUPS_PER_BLOCK, a, GROUPS_PER_BLOCK, b)
    out = jnp.sum(m5 * eye[None, :, None, :, None], axis=3)
    return out.reshape(nb * GROUPS_PER_BLOCK, a, b)


def _layer_norm(gv, nv):
    mu = jnp.mean(gv, axis=-1, keepdims=True)
    xc = gv - mu
    r = lax.rsqrt(jnp.mean(xc * xc, axis=-1, keepdims=True) + EPS)
    xhat = xc * r
    return xhat * nv, xhat, r


def gmlp_fwd(z, norm_v, wm, bs, name, tr=256):
    t = z.shape[0]
    nh = wm.shape[0]
    d = nh * GMLP_HEAD
    col0 = (z.shape[1] - 2 * d) // d
    tr = _row_tile(t, tr)

    def body(zu_ref, zv_ref, nv_ref, wm_ref, bs_ref, o_ref):
        v, _, _ = _layer_norm(_gelu(zv_ref[...]), nv_ref[...])
        vb = v.astype(BF16)
        u = _gelu(zu_ref[...])
        for c in range(tr // CHUNK):
            rows = slice(c * CHUNK, (c + 1) * CHUNK)
            for h in range(nh):
                cols = slice(h * GMLP_HEAD, (h + 1) * GMLP_HEAD)
                s = _dot(wm_ref[h], vb[rows, cols]) + bs_ref[h]
                o_ref[rows, cols] = u[rows, cols] * s

    return pl.pallas_call(
        body, name=name, grid=(t // tr,),
        in_specs=[pl.BlockSpec((tr, d), lambda i: (i, col0)), pl.BlockSpec((tr, d), lambda i: (i, col0 + 1)),
                  _vec(d), pl.BlockSpec(wm.shape, lambda i: (0, 0, 0)), pl.BlockSpec(bs.shape, lambda i: (0, 0, 0))],
        out_specs=pl.BlockSpec((tr, d), lambda i: (i, 0)),
        out_shape=jax.ShapeDtypeStruct((t, d), F32), compiler_params=_cparams(),
    )(z, z, norm_v, wm, bs)


def gmlp_bwd(dy, z, norm_v, wm, wmt, bs, name, tr=256):
    t = z.shape[0]
    nh = wm.shape[0]
    d = nh * GMLP_HEAD
    col0 = (z.shape[1] - 2 * d) // d
    tr = _row_tile(t, tr)

    def body(dy_ref, zu_ref, zv_ref, nv_ref, wm_ref, wmt_ref, bs_ref, dzu_ref, dzv_ref, dnv_ref, dwm_ref, dbs_ref,
             dv_ref):
        @pl.when(pl.program_id(0) == 0)
        def _():
            dnv_ref[...] = jnp.zeros_like(dnv_ref)
            dwm_ref[...] = jnp.zeros_like(dwm_ref)
            dbs_ref[...] = jnp.zeros_like(dbs_ref)

        zv = zv_ref[...]
        nv = nv_ref[...]
        v, xhat, r = _layer_norm(_gelu(zv), nv)
        vb = v.astype(BF16)
        zu = zu_ref[...]
        u = _gelu(zu)
        dy_ = dy_ref[...]
        for c in range(tr // CHUNK):
            rows = slice(c * CHUNK, (c + 1) * CHUNK)
            for h in range(nh):
                cols = slice(h * GMLP_HEAD, (h + 1) * GMLP_HEAD)
                vh = vb[rows, cols]
                s = _dot(wm_ref[h], vh) + bs_ref[h]
                dyh = dy_[rows, cols]
                dzu_ref[rows, cols] = (dyh * s * _gelu_grad(zu[rows, cols])).astype(BF16)
                ds = dyh * u[rows, cols]
                dsb = ds.astype(BF16)
                dbs_ref[h] += jnp.sum(ds, axis=1, keepdims=True)
                dwm_ref[h] += _dot(dsb, vh, "nt")
                dv_ref[rows, cols] = _dot(wmt_ref[h], dsb)
        dv = dv_ref[...]
        dnv_ref[...] += jnp.sum(dv * xhat, axis=0, keepdims=True)
        dxh = dv * nv
        dgv = r * (dxh - jnp.mean(dxh, axis=-1, keepdims=True) - xhat * jnp.mean(dxh * xhat, axis=-1, keepdims=True))
        dzv_ref[...] = (dgv * _gelu_grad(zv)).astype(BF16)

    full3 = lambda shp: pl.BlockSpec(shp, lambda i: (0, 0, 0))
    rows_d = pl.BlockSpec((tr, d), lambda i: (i, 0))
    return pl.pallas_call(
        body, name=name, grid=(t // tr,),
        in_specs=[rows_d, pl.BlockSpec((tr, d), lambda i: (i, col0)), pl.BlockSpec((tr, d), lambda i: (i, col0 + 1)),
                  _vec(d), full3(wm.shape), full3(wmt.shape), full3(bs.shape)],
        out_specs=[rows_d, rows_d, _vec(d), full3((nh, CHUNK, CHUNK)), full3((nh, CHUNK, 1))],
        out_shape=[jax.ShapeDtypeStruct((t, d), BF16), jax.ShapeDtypeStruct((t, d), BF16),
                   jax.ShapeDtypeStruct((1, d), F32), jax.ShapeDtypeStruct((nh, CHUNK, CHUNK), F32),
                   jax.ShapeDtypeStruct((nh, CHUNK, 1), F32)],
        scratch_shapes=[pltpu.VMEM((tr, d), F32)], compiler_params=_cparams(),
    )(dy, z, z, norm_v, wm, wmt, bs)


def _block(ref, axis, size, k):
    start = pl.multiple_of(k * size, size)
    if axis == 0:
        return ref.at[pl.ds(start, size), :]
    return ref.at[:, pl.ds(start, size)]


def _place():
    x, y, c = lax.axis_index("x"), lax.axis_index("y"), lax.axis_index("c")
    chips = [(1 - x, y), (x, 1 - y), (1 - x, 1 - y)]
    return x, y, c, chips


def _dev(x, y, c):
    return 4 * x + 2 * y + c


def gather_task(shards, axes):
    n = len(shards)
    sizes = [s.shape[ax] for s, ax in zip(shards, axes)]
    out_shape = [
        jax.ShapeDtypeStruct((s.shape[0] * N_DEV, s.shape[1]) if ax == 0 else (s.shape[0], s.shape[1] * N_DEV), s.dtype)
        for s, ax in zip(shards, axes)
    ]

    def copy(ins, outs, send_sems, recv_sems, t, k, block, to, from_input=False):
        dst = _block(outs[t], axes[t], sizes[t], _dev(*block))
        return pltpu.make_async_remote_copy(
            src_ref=ins[t] if from_input else dst, dst_ref=dst,
            send_sem=send_sems.at[t * 7 + k], recv_sem=recv_sems.at[t * 7 + k],
            device_id=to, device_id_type=MESH_DT)

    def local(ins, outs, local_sems, t, me):
        return pltpu.make_async_copy(ins[t], _block(outs[t], axes[t], sizes[t], _dev(*me)), local_sems.at[t])

    def start(ins, outs, send_sems, recv_sems, local_sems):
        x, y, c, chips = _place()
        me, sibling = (x, y, c), (x, y, 1 - c)
        for t in range(n):
            local(ins, outs, local_sems, t, me).start()
        for t in range(n):
            copy(ins, outs, send_sems, recv_sems, t, 0, me, sibling, True).start()
            for j, chip in enumerate(chips):
                copy(ins, outs, send_sems, recv_sems, t, 1 + j, me, (*chip, c), True).start()

    def late(ins, outs, send_sems, recv_sems, local_sems):
        x, y, c, chips = _place()
        me, sibling = (x, y, c), (x, y, 1 - c)
        for t in range(n):
            for j, chip in enumerate(chips):
                copy(ins, outs, send_sems, recv_sems, t, 1 + j, (*chip, c), me).wait_recv()
                copy(ins, outs, send_sems, recv_sems, t, 4 + j, (*chip, c), sibling).start()

    def finish(ins, outs, send_sems, recv_sems, local_sems):
        x, y, c, chips = _place()
        me, sibling = (x, y, c), (x, y, 1 - c)
        for t in range(n):
            copy(ins, outs, send_sems, recv_sems, t, 0, sibling, me).wait_recv()
            for j, chip in enumerate(chips):
                copy(ins, outs, send_sems, recv_sems, t, 4 + j, (*chip, 1 - c), me).wait_recv()
        for t in range(n):
            copy(ins, outs, send_sems, recv_sems, t, 0, me, sibling, True).wait_send()
            for j, chip in enumerate(chips):
                copy(ins, outs, send_sems, recv_sems, t, 1 + j, me, (*chip, c), True).wait_send()
                copy(ins, outs, send_sems, recv_sems, t, 4 + j, (*chip, c), sibling).wait_send()
            local(ins, outs, local_sems, t, me).wait()

    return CommTask(shards, out_shape, (7 * n, 7 * n, n), start, late, finish)


def _blk3(shape2, axis):
    r, c = shape2
    return (r // N_DEV, c) if axis == 0 else (r, c // N_DEV)


def _no_late(ins, outs, send_sems, recv_sems, local_sems):
    pass


def to_sibling_task(grads, axes):
    n = len(grads)
    blks = [_blk3(g.shape, ax) for g, ax in zip(grads, axes)]
    sizes = [b[ax] for b, ax in zip(blks, axes)]

    def copies(ins, outs, send_sems, recv_sems):
        x, y, c, _ = _place()
        return [pltpu.make_async_remote_copy(
            src_ref=_block(ins[t], axes[t], sizes[t], 2 * i + (1 - c)), dst_ref=outs[t].at[i],
            send_sem=send_sems.at[t * N_CHIP + i], recv_sem=recv_sems.at[t * N_CHIP + i],
            device_id=(x, y, 1 - c), device_id_type=MESH_DT) for t in range(n) for i in range(N_CHIP)]

    def start(ins, outs, send_sems, recv_sems, local_sems):
        for cp in copies(ins, outs, send_sems, recv_sems):
            cp.start()

    def finish(ins, outs, send_sems, recv_sems, local_sems):
        cps = copies(ins, outs, send_sems, recv_sems)
        for cp in cps:
            cp.wait_recv()
        for cp in cps:
            cp.wait_send()

    out_shape = [jax.ShapeDtypeStruct((N_CHIP,) + b, g.dtype) for b, g in zip(blks, grads)]
    return CommTask(grads, out_shape, (N_CHIP * n, N_CHIP * n, 1), start, _no_late, finish)


def across_chips_task(parts):
    n = len(parts)

    def copies(ins, outs, send_sems, recv_sems):
        x, y, c, chips = _place()
        my_chip = 2 * x + y
        return [pltpu.make_async_remote_copy(
            src_ref=ins[t].at[2 * chip[0] + chip[1]], dst_ref=outs[t].at[my_chip],
            send_sem=send_sems.at[t * 3 + j], recv_sem=recv_sems.at[t * 3 + j],
            device_id=(*chip, c), device_id_type=MESH_DT) for t in range(n) for j, chip in enumerate(chips)]

    def mine(ins, outs, local_sems):
        x, y, _, _ = _place()
        my_chip = 2 * x + y
        return [pltpu.make_async_copy(ins[t].at[my_chip], outs[t].at[my_chip], local_sems.at[t]) for t in range(n)]

    def start(ins, outs, send_sems, recv_sems, local_sems):
        for cp in mine(ins, outs, local_sems):
            cp.start()
        for cp in copies(ins, outs, send_sems, recv_sems):
            cp.start()

    def finish(ins, outs, send_sems, recv_sems, local_sems):
        cps = copies(ins, outs, send_sems, recv_sems)
        for cp in cps:
            cp.wait_recv()
        for cp in cps:
            cp.wait_send()
        for cp in mine(ins, outs, local_sems):
            cp.wait()

    out_shape = [jax.ShapeDtypeStruct(p.shape, p.dtype) for p in parts]
    return CommTask(parts, out_shape, (3 * n, 3 * n, n), start, _no_late, finish)


def run_tasks(tasks, name):
    t_in = [len(t.inputs) for t in tasks]
    t_out = [len(t.out_shape) for t in tasks]

    def body(*refs):
        pos, views = 0, []
        for k in t_in:
            views.append([refs[pos:pos + k]])
            pos += k
        for v, k in zip(views, t_out):
            v.append(refs[pos:pos + k])
            pos += k
        for i, v in enumerate(views):
            v.extend(refs[pos + 3 * i:pos + 3 * i + 3])
        for phase in ("start", "late", "finish"):
            for t, v in zip(tasks, views):
                getattr(t, phase)(*v)

    any_spec = pl.BlockSpec(memory_space=pl.ANY)
    res = pl.pallas_call(
        body, name=name, in_specs=[any_spec] * sum(t_in), out_specs=[any_spec] * sum(t_out),
        out_shape=[s for t in tasks for s in t.out_shape], input_output_aliases=_task_aliases(tasks, 0, 0),
        scratch_shapes=[pltpu.SemaphoreType.DMA((k,)) for t in tasks for k in t.n_sems],
    )(*[a for t in tasks for a in t.inputs])
    res, out, pos = list(res), [], 0
    for k in t_out:
        out.append(res[pos:pos + k])
        pos += k
    return out


_HBM_SPEC = pl.BlockSpec(memory_space=pl.ANY)
_SEM_SPEC = pl.BlockSpec(memory_space=pltpu.SEMAPHORE)
_DATAFLOW = pltpu.SideEffectType.DATAFLOW_SIDE_EFFECTING


def _full_shape(s, ax):
    return (s.shape[0] * N_DEV, s.shape[1]) if ax == 0 else (s.shape[0], s.shape[1] * N_DEV)


def _level1_copy(src, landing, axis, size, send_sems, recv_sems, slot, sender, to):
    dst = _block(landing, axis, size, _dev(*sender))
    return pltpu.make_async_remote_copy(src_ref=src, dst_ref=dst, send_sem=send_sems.at[slot],
                                        recv_sem=recv_sems.at[slot], device_id=to, device_id_type=MESH_DT)


def place_own_block(shard, landing, axis, me, name, tr=256):
    r, c = shard.shape
    tr = _row_tile(r, tr)
    nrb = r // tr
    if axis == 0:
        o_map = lambda i, me_ref: (me_ref[0] * nrb + i, 0)
    else:
        o_map = lambda i, me_ref: (i, me_ref[0])

    def body(me_ref, x_ref, land_ref, o_ref):
        o_ref[...] = x_ref[...]

    return pl.pallas_call(
        body, name=name,
        grid_spec=pltpu.PrefetchScalarGridSpec(
            num_scalar_prefetch=1, grid=(nrb,),
            in_specs=[pl.BlockSpec((tr, c), lambda i, me_ref: (i, 0)), pl.BlockSpec(memory_space=pl.ANY)],
            out_specs=pl.BlockSpec((tr, c), o_map)),
        out_shape=jax.ShapeDtypeStruct(landing.shape, landing.dtype), input_output_aliases={2: 0},
        compiler_params=_cparams(),
    )(me, shard, landing)


def gather_start(shards, axes, sizes, groups, name):
    n = len(shards)

    def body(*refs):
        srcs, lands, sems = refs[:n], refs[n:2 * n], refs[4 * n:]
        x, y, c, chips = _place()
        me = (x, y, c)
        targets = [(x, y, 1 - c)] + [(*chip, c) for chip in chips]
        for g, members in enumerate(groups):
            for m, t in enumerate(members):
                for k, to in enumerate(targets):
                    _level1_copy(srcs[t], lands[t], axes[t], sizes[t], sems[2 * g], sems[2 * g + 1], 4 * m + k,
                                 me, to).start()

    landing = [lax.empty(_full_shape(s, ax), s.dtype) for s, ax in zip(shards, axes)]
    out = pl.pallas_call(
        body, name=name,
        out_shape=[jax.ShapeDtypeStruct(b.shape, b.dtype) for b in shards + landing]
        + [pltpu.SemaphoreType.DMA((4 * len(members),)) for members in groups for _ in (0, 1)],
        in_specs=[_HBM_SPEC] * (2 * n), out_specs=[_HBM_SPEC] * (2 * n) + [_SEM_SPEC] * (2 * len(groups)),
        input_output_aliases={i: i for i in range(2 * n)},
        compiler_params=pltpu.CompilerParams(has_side_effects=_DATAFLOW),
    )(*shards, *landing)
    out = list(out)
    sems = out[2 * n:]
    return out[:n], out[n:2 * n], [(sems[2 * g], sems[2 * g + 1]) for g in range(len(groups))]


def gather_wait(shards, landing, axes, sizes, send_sems, recv_sems, after, name):
    n = len(landing)

    def body(*refs):
        srcs, lands = refs[:n], refs[n:2 * n]
        send, recv = refs[2 * n], refs[2 * n + 1]
        x, y, c, chips = _place()
        me = (x, y, c)
        peers = [(x, y, 1 - c)] + [(*chip, c) for chip in chips]
        for t in range(n):
            for k, peer in enumerate(peers):
                _level1_copy(srcs[t], lands[t], axes[t], sizes[t], send, recv, 4 * t + k, me, peer).wait_send()
                _level1_copy(srcs[t], lands[t], axes[t], sizes[t], send, recv, 4 * t + k, peer, me).wait_recv()

    out = pl.pallas_call(
        body, name=name, out_shape=[jax.ShapeDtypeStruct(b.shape, b.dtype) for b in shards + landing],
        in_specs=[_HBM_SPEC] * (2 * n) + [_SEM_SPEC, _SEM_SPEC, pl.BlockSpec(memory_space=pl.ANY)],
        out_specs=[_HBM_SPEC] * (2 * n), input_output_aliases={i: i for i in range(2 * n)},
        compiler_params=pltpu.CompilerParams(has_side_effects=_DATAFLOW),
    )(*shards, *landing, send_sems, recv_sems, after)
    return list(out)[n:]


def forward_task(landing, axes, sizes):
    n = len(landing)

    def forward(lands, send_sems, recv_sems, t, j, chip_core):
        x, y, c, _ = _place()
        blk = _block(lands[t], axes[t], sizes[t], _dev(*chip_core))
        return pltpu.make_async_remote_copy(src_ref=blk, dst_ref=blk, send_sem=send_sems.at[3 * t + j],
                                            recv_sem=recv_sems.at[3 * t + j], device_id=(x, y, 1 - c),
                                            device_id_type=MESH_DT)

    def start(ins, lands, send_sems, recv_sems, local_sems):
        _, _, c, chips = _place()
        for t in range(n):
            for j, chip in enumerate(chips):
                forward(lands, send_sems, recv_sems, t, j, (*chip, c)).start()

    def finish(ins, lands, send_sems, recv_sems, local_sems):
        _, _, c, chips = _place()
        for t in range(n):
            for j, chip in enumerate(chips):
                forward(lands, send_sems, recv_sems, t, j, (*chip, 1 - c)).wait_recv()
        for t in range(n):
            for j, chip in enumerate(chips):
                forward(lands, send_sems, recv_sems, t, j, (*chip, c)).wait_send()

    out_shape = [jax.ShapeDtypeStruct(b.shape, b.dtype) for b in landing]
    return CommTask(landing, out_shape, (3 * n, 3 * n, 1), start, _no_late, finish, in_place=True)


def rs_chip_sum(grad, recv, axis, core, name, tr=512):
    br, bc = _blk3(grad.shape, axis)
    tr = _row_tile(br, tr)
    nrb = br // tr

    if axis == 0:
        g_map = lambda i, r, c_ref: ((2 * i + c_ref[0]) * nrb + r, 0)
    else:
        g_map = lambda i, r, c_ref: (r, 2 * i + c_ref[0])

    def body(c_ref, g_ref, r_ref, o_ref):
        o_ref[...] = (g_ref[...].astype(F32) + r_ref[...].astype(F32)).astype(BF16)

    return pl.pallas_call(
        body, name=name,
        grid_spec=pltpu.PrefetchScalarGridSpec(
            num_scalar_prefetch=1, grid=(N_CHIP, nrb),
            in_specs=[pl.BlockSpec((tr, bc), g_map), pl.BlockSpec((None, tr, bc), lambda i, r, c_ref: (i, r, 0))],
            out_specs=pl.BlockSpec((None, tr, bc), lambda i, r, c_ref: (i, r, 0))),
        out_shape=jax.ShapeDtypeStruct((N_CHIP, br, bc), BF16), compiler_params=_cparams(),
    )(core, grad, recv)


def _adamw(w, g, m, v):
    m = ADAM_B1 * m + (1.0 - ADAM_B1) * g
    v = ADAM_B2 * v + (1.0 - ADAM_B2) * (g * g)
    m_hat = m / (1.0 - ADAM_B1 ** ADAM_STEP)
    v_hat = v / (1.0 - ADAM_B2 ** ADAM_STEP)
    delta = -ADAM_LR * (m_hat / (jnp.sqrt(v_hat) + ADAM_EPS) + ADAM_WD * w)
    return delta, m, v


def _sum_chips(p_ref):
    g = p_ref[0].astype(F32)
    for i in range(1, N_CHIP):
        g = g + p_ref[i].astype(F32)
    return g


def adam_sharded(parts, w, m, v, name, tr=256):
    r, c = w.shape
    assert parts.shape[2] == c
    tr = _row_tile(r, tr)

    def body(p_ref, w_ref, m_ref, v_ref, g_ref, d_ref, nm_ref, nv_ref):
        g = _sum_chips(p_ref)
        delta, nm, nv = _adamw(w_ref[...], g, m_ref[...], v_ref[...])
        g_ref[...] = g
        d_ref[...] = delta
        nm_ref[...] = nm
        nv_ref[...] = nv

    sp = pl.BlockSpec((tr, c), lambda i: (i, 0))
    return pl.pallas_call(
        body, name=name, grid=(r // tr,),
        in_specs=[pl.BlockSpec((N_CHIP, tr, c), lambda i: (0, i, 0)), sp, sp, sp],
        out_specs=[sp, sp, sp, sp], out_shape=[jax.ShapeDtypeStruct((r, c), F32)] * 4,
        compiler_params=_cparams(),
    )(parts, w, m, v)


def adam_small(items, name):
    n = len(items)

    def body(*refs):
        for i in range(n):
            g_ref, w_ref, m_ref, v_ref = refs[4 * i:4 * i + 4]
            d_ref, nm_ref, nv_ref = refs[4 * n + 3 * i:4 * n + 3 * i + 3]
            delta, nm, nv = _adamw(w_ref[...], g_ref[...], m_ref[...], v_ref[...])
            d_ref[...] = delta
            nm_ref[...] = nm
            nv_ref[...] = nv

    out = pl.pallas_call(
        body, name=name, out_shape=[jax.ShapeDtypeStruct(it[1].shape, F32) for it in items for _ in range(3)],
        compiler_params=_cparams(),
    )(*[a for it in items for a in it])
    return [tuple(out[3 * i:3 * i + 3]) for i in range(n)]


def sum_devices(gathered, name, tr=512):
    _, r, c = gathered.shape
    tr = _row_tile(r, tr)

    def body(x_ref, o_ref):
        s = x_ref[0]
        for k in range(1, N_DEV):
            s = s + x_ref[k]
        o_ref[...] = s

    return pl.pallas_call(
        body, name=name, grid=(r // tr,), in_specs=[pl.BlockSpec((N_DEV, tr, c), lambda i: (0, i, 0))],
        out_specs=pl.BlockSpec((tr, c), lambda i: (i, 0)), out_shape=jax.ShapeDtypeStruct((r, c), F32),
        compiler_params=_cparams(),
    )(gathered)


def _pad_to(a, axis, mult):
    size = a.shape[axis]
    pad = (-size) % mult
    if pad == 0:
        return a
    cfg = [(0, 0)] * a.ndim
    cfg[axis] = (0, pad)
    return jnp.pad(a, cfg)


def _as2d(a):
    if a.ndim == 1:
        return a.reshape(1, -1)
    return a.reshape(-1, a.shape[-1])


def kernel(x, p, norm_ffn1, w1_gate, w1_up, w1_down, norm_mix, w_in, ssm_log_dt, ssm_a_re, ssm_a_im, ssm_b_re, ssm_b_im, ssm_c_re, ssm_c_im, ssm_d, ssm_w_glu, gmlp_norm_v, gmlp_w_s, gmlp_b_s, norm_ssm_out, norm_gmlp_out, w_out, norm_ffn2, w2_gate, w2_up, w2_down, norm_ple, w_ple_gate, w_ple_proj, norm_final, loss_target, m_norm_ffn1, m_w1_gate, m_w1_up, m_w1_down, m_norm_mix, m_w_in, m_ssm_log_dt, m_ssm_a_re, m_ssm_a_im, m_ssm_b_re, m_ssm_b_im, m_ssm_c_re, m_ssm_c_im, m_ssm_d, m_ssm_w_glu, m_gmlp_norm_v, m_gmlp_w_s, m_gmlp_b_s, m_norm_ssm_out, m_norm_gmlp_out, m_w_out, m_norm_ffn2, m_w2_gate, m_w2_up, m_w2_down, m_norm_ple, m_w_ple_gate, m_w_ple_proj, m_norm_final, v_norm_ffn1, v_w1_gate, v_w1_up, v_w1_down, v_norm_mix, v_w_in, v_ssm_log_dt, v_ssm_a_re, v_ssm_a_im, v_ssm_b_re, v_ssm_b_im, v_ssm_c_re, v_ssm_c_im, v_ssm_d, v_ssm_w_glu, v_gmlp_norm_v, v_gmlp_w_s, v_gmlp_b_s, v_norm_ssm_out, v_norm_gmlp_out, v_w_out, v_norm_ffn2, v_w2_gate, v_w2_up, v_w2_down, v_norm_ple, v_w_ple_gate, v_w_ple_proj, v_norm_final):
    weights = dict(
        norm_ffn1=norm_ffn1, w1_gate=w1_gate, w1_up=w1_up, w1_down=w1_down, norm_mix=norm_mix, w_in=w_in,
        ssm_log_dt=ssm_log_dt, ssm_a_re=ssm_a_re, ssm_a_im=ssm_a_im, ssm_b_re=ssm_b_re, ssm_b_im=ssm_b_im,
        ssm_c_re=ssm_c_re, ssm_c_im=ssm_c_im, ssm_d=ssm_d, ssm_w_glu=ssm_w_glu, gmlp_norm_v=gmlp_norm_v,
        gmlp_w_s=gmlp_w_s, gmlp_b_s=gmlp_b_s, norm_ssm_out=norm_ssm_out, norm_gmlp_out=norm_gmlp_out, w_out=w_out,
        norm_ffn2=norm_ffn2, w2_gate=w2_gate, w2_up=w2_up, w2_down=w2_down, norm_ple=norm_ple,
        w_ple_gate=w_ple_gate, w_ple_proj=w_ple_proj, norm_final=norm_final)
    moments_m = dict(
        norm_ffn1=m_norm_ffn1, w1_gate=m_w1_gate, w1_up=m_w1_up, w1_down=m_w1_down, norm_mix=m_norm_mix, w_in=m_w_in,
        ssm_log_dt=m_ssm_log_dt, ssm_a_re=m_ssm_a_re, ssm_a_im=m_ssm_a_im, ssm_b_re=m_ssm_b_re, ssm_b_im=m_ssm_b_im,
        ssm_c_re=m_ssm_c_re, ssm_c_im=m_ssm_c_im, ssm_d=m_ssm_d, ssm_w_glu=m_ssm_w_glu, gmlp_norm_v=m_gmlp_norm_v,
        gmlp_w_s=m_gmlp_w_s, gmlp_b_s=m_gmlp_b_s, norm_ssm_out=m_norm_ssm_out, norm_gmlp_out=m_norm_gmlp_out,
        w_out=m_w_out, norm_ffn2=m_norm_ffn2, w2_gate=m_w2_gate, w2_up=m_w2_up, w2_down=m_w2_down,
        norm_ple=m_norm_ple, w_ple_gate=m_w_ple_gate, w_ple_proj=m_w_ple_proj, norm_final=m_norm_final)
    moments_v = dict(
        norm_ffn1=v_norm_ffn1, w1_gate=v_w1_gate, w1_up=v_w1_up, w1_down=v_w1_down, norm_mix=v_norm_mix, w_in=v_w_in,
        ssm_log_dt=v_ssm_log_dt, ssm_a_re=v_ssm_a_re, ssm_a_im=v_ssm_a_im, ssm_b_re=v_ssm_b_re, ssm_b_im=v_ssm_b_im,
        ssm_c_re=v_ssm_c_re, ssm_c_im=v_ssm_c_im, ssm_d=v_ssm_d, ssm_w_glu=v_ssm_w_glu, gmlp_norm_v=v_gmlp_norm_v,
        gmlp_w_s=v_gmlp_w_s, gmlp_b_s=v_gmlp_b_s, norm_ssm_out=v_norm_ssm_out, norm_gmlp_out=v_norm_gmlp_out,
        w_out=v_w_out, norm_ffn2=v_norm_ffn2, w2_gate=v_w2_gate, w2_up=v_w2_up, w2_down=v_w2_down,
        norm_ple=v_norm_ple, w_ple_gate=v_w_ple_gate, w_ple_proj=v_w_ple_proj, norm_final=v_norm_final)
    names = list(weights)

    xs = x[0]
    ps = p[0, 0].astype(BF16)
    tgt = loss_target[0]
    d_model = xs.shape[1]
    d_ssm = d_model // 2
    n_groups = d_ssm // SSM_GROUP

    transposed = ("w1_gate", "w1_up", "w2_gate", "w2_up")
    big = {
        "w1_gate": 0, "w1_up": 0, "w1_down": 0, "w_in": 1, "ssm_w_glu": 0, "w_out": 0,
        "w2_gate": 0, "w2_up": 0, "w2_down": 0, "w_ple_gate": 0, "w_ple_proj": 1}
    big_names = list(big)

    def view(a, k):
        return a[0].T if k in transposed else a[0]

    def unview(a, k):
        return a.T[None] if k in transposed else a[None]

    shard = {k: _pad_to(view(weights[k], k).astype(BF16), big[k], LANE) for k in big_names}
    W = {}

    abar_r, abar_i, bbar_r, bbar_i = _ssm_discretize(ssm_log_dt[0], ssm_a_re[0], ssm_a_im[0], ssm_b_re[0], ssm_b_im[0])
    bc_r = _block_diag(jnp.swapaxes(bbar_r, 1, 2)).astype(BF16)
    bc_i = _block_diag(jnp.swapaxes(bbar_i, 1, 2)).astype(BF16)
    cc_r = _block_diag(jnp.swapaxes(ssm_c_re[0], 1, 2)).astype(BF16)
    cc_i = _block_diag(jnp.swapaxes(ssm_c_im[0], 1, 2)).astype(BF16)
    apw_f = _scan_constants(abar_r, abar_i, False)
    apw_b = _scan_constants(abar_r, abar_i, True)
    causal = jnp.tril(jnp.ones((CHUNK, CHUNK), dtype=bool))
    wm = jnp.where(causal[None], gmlp_w_s[0], 0.0).astype(BF16)
    wmt = jnp.swapaxes(wm, 1, 2)
    bs = gmlp_b_s[0][:, :, None]

    groups = [["w1_gate"], ["w1_up"], ["w1_down"], ["w_in", "ssm_w_glu", "w_out"], ["w2_gate", "w2_up"],
              ["w2_down", "w_ple_gate", "w_ple_proj"]]
    order = [k for g in groups for k in g]
    place = {k: i for i, k in enumerate(order)}
    me = (4 * lax.axis_index("x") + 2 * lax.axis_index("y") + lax.axis_index("c")).astype(jnp.int32).reshape(1)
    size = {k: shard[k].shape[big[k]] for k in order}
    in_flight, landing, sems = gather_start([shard[k] for k in order], [big[k] for k in order],
                                            [size[k] for k in order], [[place[k] for k in g] for g in groups],
                                            "gather_start")
    landing = [place_own_block(in_flight[place[k]], landing[place[k]], big[k], me, "place_" + k) for k in order]

    def landed(g, after):
        members = [place[k] for k in groups[g]]
        axes_g, sizes_g = [big[k] for k in groups[g]], [size[k] for k in groups[g]]
        bufs = gather_wait([in_flight[i] for i in members], [landing[i] for i in members], axes_g, sizes_g,
                           *sems[g], after, "gather_wait_%d" % g)
        return forward_task(bufs, axes_g, sizes_g)

    def arrive(g, after):
        W.update(zip(groups[g], run_tasks([landed(g, after)], "gather_forward_%d" % g)[0]))

    def arrive_during(g, after, fn, *a, **kw):
        out, (got,) = fn(*a, tasks=[landed(g, after)], **kw)
        W.update(zip(groups[g], got))
        return out

    xn1 = rmsnorm_fwd(xs, norm_ffn1, "norm_ffn1")
    arrive(0, xn1)
    gate1 = matmul(xn1, W["w1_gate"], "nt", "ffn1_gate")
    arrive(1, gate1)
    gate1, up1, act1 = ffn_up(xn1, W["w1_up"], gate1, "ffn1_up")
    arrive(2, act1)
    h1 = matmul(act1, W["w1_down"], "nn", "ffn1_down", res=xs, scale=0.5)
    arrive(3, h1)
    xn2 = rmsnorm_fwd(h1, norm_mix, "norm_mix")
    z = matmul(xn2, W["w_in"], "nn", "proj_in")
    y_pre, yg, sr, si = s5_fwd(z, bc_r, bc_i, cc_r, cc_i, apw_f, ssm_d, "s5_fwd")
    glin = matmul(yg, W["ssm_w_glu"], "nn", "ssm_glu")
    y_gmlp = gmlp_fwd(z, gmlp_norm_v, wm, bs, "gmlp_fwd")
    ycat = mix_out_fwd(y_pre, glin, y_gmlp, norm_ssm_out, norm_gmlp_out, "mix_out")
    h2 = arrive_during(4, ycat, matmul, ycat, W["w_out"], "nn", "proj_out", res=h1)
    xn3 = rmsnorm_fwd(h2, norm_ffn2, "norm_ffn2")
    gate2, up2, act2 = arrive_during(5, xn3, ffn_gate_up, xn3, W["w2_gate"], W["w2_up"], "ffn2_gate_up")
    h3 = matmul(act2, W["w2_down"], "nn", "ffn2_down", res=h2, scale=0.5)
    xn4 = rmsnorm_fwd(h3, norm_ple, "norm_ple")
    pg_lin, pp, h4 = ple_fused_fwd(xn4, W["w_ple_gate"], ps, W["w_ple_proj"], h3, "ple_fwd")
    dh4, loss_part, g_norm_final = final_loss(h4, tgt, norm_final.reshape(1, -1), "final_loss")
    loss = lax.psum(loss_part[0, 0], ("x", "y", "c"))

    G = {}
    reduced = {}
    chip_part = {}
    wait_sibling, wait_chips = [], []
    core = lax.axis_index("c").astype(jnp.int32).reshape(1)

    def grad(name_, value):
        G[name_] = value
        wait_sibling.append(name_)

    def carry(fn, *a, levels="ab", extra=None, **kw):
        tasks, kinds = [], []
        if extra is not None:
            tasks.append(extra[0])
            kinds.append(("x", extra[1]))
        if "a" in levels and wait_sibling:
            group = list(wait_sibling)
            wait_sibling.clear()
            tasks.append(to_sibling_task([G[k] for k in group], [big[k] for k in group]))
            kinds.append(("a", group))
        if "b" in levels and wait_chips:
            group = list(wait_chips)
            wait_chips.clear()
            tasks.append(across_chips_task([chip_part[k] for k in group]))
            kinds.append(("b", group))
        if not tasks:
            return fn(*a, **kw)
        out, task_outs = fn(*a, tasks=tasks, **kw)
        for (kind, group), outs in zip(kinds, task_outs):
            if kind == "x":
                group(outs)
                continue
            for k, r in zip(group, outs):
                if kind == "a":
                    chip_part[k] = rs_chip_sum(G[k], r, big[k], core, "rs_sum_" + k)
                    wait_chips.append(k)
                else:
                    reduced[k] = r
        return out

    small = {}
    small["norm_final"] = g_norm_final
    dpp, dpg = ple_bwd(dh4, pg_lin, pp, "ple_bwd")
    grad("w_ple_proj", matmul(ps, dpp, "tn", "grad_ple_proj", out_dtype=BF16))
    grad("w_ple_gate", carry(matmul, xn4, dpg, "tn", "grad_ple_gate", out_dtype=BF16))
    dxn4 = carry(matmul, dpg, W["w_ple_gate"], "nt", "ple_gate_bwd")
    dh3, dh3b, small["norm_ple"] = rmsnorm_bwd(dxn4, h3, norm_ple, dh4, "norm_ple_bwd")

    def ffn_bwd(tag, dhb, xn, gate, up, act, wg, wu, wd, extra=None, last_levels="ab"):
        dgate, dup = carry(ffn_bwd_act, dhb, W[wd], gate, up, tag + "_act_bwd", extra=extra)
        grad(wd, carry(matmul, act, dhb, "tn", tag + "_grad_down", out_dtype=BF16, scale=0.5))
        grad(wg, carry(matmul, dgate, xn, "tn", tag + "_grad_gate", out_dtype=BF16))
        grad(wu, carry(matmul, dup, xn, "tn", tag + "_grad_up", out_dtype=BF16))
        dxn = carry(matmul, dgate, W[wg], "nn", tag + "_gate_bwd")
        return carry(matmul, dup, W[wu], "nn", tag + "_up_bwd", res=dxn, levels=last_levels)

    dxn3 = ffn_bwd("ffn2", dh3b, xn3, gate2, up2, act2, "w2_gate", "w2_up", "w2_down", last_levels="a")
    dh2, dh2b, small["norm_ffn2"] = rmsnorm_bwd(dxn3, h2, norm_ffn2, dh3, "norm_ffn2_bwd")

    grad("w_out", matmul(ycat, dh2b, "tn", "grad_out", out_dtype=BF16))
    dycat = carry(matmul, dh2b, W["w_out"], "nt", "proj_out_bwd", levels="a")
    dyg_direct, dglin, dy_gmlp, small["norm_ssm_out"], small["norm_gmlp_out"] = mix_out_bwd(
        dycat, y_pre, glin, y_gmlp, norm_ssm_out, norm_gmlp_out, "mix_out_bwd")
    grad("ssm_w_glu", matmul(yg, dglin, "tn", "grad_glu", out_dtype=BF16))
    dyg = carry(matmul, dglin, W["ssm_w_glu"], "nt", "ssm_glu_bwd", res=dyg_direct, levels="a")
    du, small["ssm_d"], gc_r, gc_i, gb_r, gb_i, ga_r, ga_i = carry(
        s5_bwd, dyg, y_pre, z, sr, si, bc_r, bc_i, cc_r, cc_i, apw_b, ssm_d, "s5_bwd")
    dzu, dzv, small["gmlp_norm_v"], g_wm, g_bs = gmlp_bwd(dy_gmlp, z, gmlp_norm_v, wm, wmt, bs, "gmlp_bwd")
    small["gmlp_w_s"] = g_wm
    small["gmlp_b_s"] = g_bs
    small["c_re"] = _block_diag_extract(gc_r, SSM_GROUP, SSM_STATE)
    small["c_im"] = _block_diag_extract(gc_i, SSM_GROUP, SSM_STATE)
    small["bbar_r"] = jnp.swapaxes(_block_diag_extract(gb_r, SSM_GROUP, SSM_STATE), 1, 2)
    small["bbar_i"] = jnp.swapaxes(_block_diag_extract(gb_i, SSM_GROUP, SSM_STATE), 1, 2)
    small["abar_r"] = jnp.sum(ga_r, axis=0).reshape(n_groups, SSM_STATE)
    small["abar_i"] = jnp.sum(ga_i, axis=0).reshape(n_groups, SSM_STATE)

    dz = jnp.concatenate([du, dzu, dzv], axis=1)
    grad("w_in", matmul(xn2, dz, "tn", "grad_in", out_dtype=BF16))
    dxn2 = carry(matmul, dz, W["w_in"], "nt", "proj_in_bwd")
    dh1, dh1b, small["norm_mix"] = rmsnorm_bwd(dxn2, h1, norm_mix, dh2, "norm_mix_bwd")

    def pack(parts):
        flat = jnp.concatenate([v.reshape(-1) for v in parts.values()])
        return _pad_to(flat, 0, SUBLANE * LANE).reshape(-1, LANE), flat.shape[0]

    def unpack(everyones, n, parts, tag):
        rows = everyones.shape[0] // N_DEV
        summed = sum_devices(everyones.reshape(N_DEV, rows, LANE), "sum_" + tag).reshape(-1)[:n]
        out, off = {}, 0
        for k, v in parts.items():
            out[k] = summed[off:off + v.size].reshape(v.shape)
            off += v.size
        return out

    early = dict(small)
    flat_early, n_early = pack(early)
    small_landed = []
    dxn1 = ffn_bwd("ffn1", dh1b, xn1, gate1, up1, act1, "w1_gate", "w1_up", "w1_down",
                   extra=(gather_task([flat_early], [0]), small_landed.extend))
    tot = unpack(small_landed[0], n_early, early, "small")
    grad_x, _, g_norm_ffn1 = rmsnorm_bwd(dxn1, xs, norm_ffn1, dh1, "norm_ffn1_bwd")
    assert not wait_sibling and not wait_chips and set(reduced) == set(big_names)
    last = {"norm_ffn1": g_norm_ffn1}
    flat_last, n_last = pack(last)
    ((everyones_last,),) = run_tasks([gather_task([flat_last], [0])], "gather_last")
    tot.update(unpack(everyones_last, n_last, last, "last"))

    out_g, out_d, out_m, out_v = {}, {}, {}, {}
    for k in big_names:
        g, dl, nm, nv = adam_sharded(reduced[k], view(weights[k], k), view(moments_m[k], k), view(moments_v[k], k),
                                     "adam_" + k)
        out_g[k], out_d[k], out_m[k], out_v[k] = unview(g, k), unview(dl, k), unview(nm, k), unview(nv, k)

    _, ssm_vjp = jax.vjp(_ssm_discretize, ssm_log_dt[0], ssm_a_re[0], ssm_a_im[0], ssm_b_re[0], ssm_b_im[0])
    g_log_dt, g_a_re, g_a_im, g_b_re, g_b_im = ssm_vjp((tot["abar_r"], tot["abar_i"], tot["bbar_r"], tot["bbar_i"]))
    small_grads = {
        "norm_ffn1": tot["norm_ffn1"], "norm_mix": tot["norm_mix"], "ssm_log_dt": g_log_dt, "ssm_a_re": g_a_re,
        "ssm_a_im": g_a_im, "ssm_b_re": g_b_re, "ssm_b_im": g_b_im, "ssm_c_re": tot["c_re"], "ssm_c_im": tot["c_im"],
        "ssm_d": tot["ssm_d"], "gmlp_norm_v": tot["gmlp_norm_v"],
        "gmlp_w_s": jnp.where(causal[None], tot["gmlp_w_s"], 0.0), "gmlp_b_s": tot["gmlp_b_s"],
        "norm_ssm_out": tot["norm_ssm_out"], "norm_gmlp_out": tot["norm_gmlp_out"], "norm_ffn2": tot["norm_ffn2"],
        "norm_ple": tot["norm_ple"], "norm_final": tot["norm_final"]}
    swapped = ("ssm_b_re", "ssm_b_im")

    def pre(k, a):
        return jnp.swapaxes(a, -1, -2) if k in swapped else a

    def update(group, name_):
        items = [(_as2d(pre(k, small_grads[k].reshape(weights[k].shape))), _as2d(pre(k, weights[k])),
                  _as2d(pre(k, moments_m[k])), _as2d(pre(k, moments_v[k]))) for k in group]
        for k, it, (dl, nm, nv) in zip(group, items, adam_small(items, name_)):
            shp = pre(k, weights[k]).shape
            out_g[k], out_d[k], out_m[k], out_v[k] = [pre(k, a.reshape(shp)) for a in (it[0], dl, nm, nv)]

    update([k for k in small_grads if k != "norm_ffn1"], "adam_replicated")
    update(["norm_ffn1"], "adam_norm_ffn1")

    return (loss, grad_x[None], *[out_g[k] for k in names], *[out_d[k] for k in names],
            *[out_m[k] for k in names], *[out_v[k] for k in names])
```
